```python
import jax
import jax.numpy as jnp
from jax import lax
import numpy as np

D_MODEL = 1024
BATCH = 8
SEQ = 4096
DEPTH = 4

GRID_W = 64
HEAD_DIM = 64
LRU_BLOCKS = 6
LRU_BLOCK_DIM = 64
LRU_WIDTH = LRU_BLOCKS * LRU_BLOCK_DIM
RET_HEADS = 6
RET_WIDTH = RET_HEADS * HEAD_DIM
NA_HEADS = 4
NA_WIDTH = NA_HEADS * HEAD_DIM
MIX_WIDTH = LRU_WIDTH + RET_WIDTH + NA_WIDTH
IN_WIDTH = 2 * LRU_WIDTH + 4 * RET_WIDTH + 3 * NA_WIDTH
CONV_WIDTH = 4
LRU_C = 8.0
LRU_A_MIN = 0.9
LRU_A_MAX = 0.999
RET_CHUNK = 128
ROPE_BASE = 10000.0
GN_EPS = 1e-6
NA_MAX_KH = 8
NA_KW = 16
NA_QB = 16
NA_KS = NA_QB + NA_KW
D_FF = -(-8 * D_MODEL // (3 * 256)) * 256
DEEPNORM_ALPHA = (2 * DEPTH) ** 0.25
DEEPNORM_BETA = (8 * DEPTH) ** -0.25
LN_EPS = 1e-5

kernel_name = "hybrid_lru_retention_natten_deepnorm_encoder"


def _layer_norm(x, g, b):
    xf = x.astype(jnp.float32)
    mu = jnp.mean(xf, axis=-1, keepdims=True)
    xc = xf - mu
    var = jnp.mean(xc * xc, axis=-1, keepdims=True)
    y = xc * lax.rsqrt(var + LN_EPS) * g.astype(jnp.float32) + b.astype(jnp.float32)
    return y.astype(x.dtype)


def _linear_combine(c1, c2):
    a1, b1 = c1
    a2, b2 = c2
    return a1 * a2, a2 * b1 + b2


def _rg_lru_group(xb, gate, conv_w, conv_b, w_a, b_a, w_x, b_x, lam):
    bsz, seq, _ = xb.shape
    left = CONV_WIDTH // 2
    xp = jnp.pad(xb.astype(jnp.float32), ((0, 0), (left, CONV_WIDTH - 1 - left), (0, 0)))
    cw = conv_w.astype(jnp.float32)
    xc = conv_b.astype(jnp.float32)
    for j in range(CONV_WIDTH):
        xc = xc + xp[:, j:j + seq] * cw[j]
    xg = xc.reshape(bsz, seq, LRU_BLOCKS, LRU_BLOCK_DIM)
    h_sum = jnp.zeros_like(xc)
    for d, rev in enumerate((False, True)):
        r = jax.nn.sigmoid(jnp.einsum('bsgi,gij->bsgj', xg, w_a[d].astype(jnp.float32)).reshape(bsz, seq, LRU_WIDTH) + b_a[d].astype(jnp.float32))
        i = jax.nn.sigmoid(jnp.einsum('bsgi,gij->bsgj', xg, w_x[d].astype(jnp.float32)).reshape(bsz, seq, LRU_WIDTH) + b_x[d].astype(jnp.float32))
        log_a = -LRU_C * r * jax.nn.softplus(-lam[d].astype(jnp.float32))
        a = jnp.exp(log_a)
        u = jnp.sqrt(-jnp.expm1(2.0 * log_a)) * (i * xc)
        _, h = lax.associative_scan(_linear_combine, (a, u), axis=1, reverse=rev)
        h_sum = h_sum + h
    return (h_sum * jax.nn.gelu(gate.astype(jnp.float32))).astype(xb.dtype)


def _rope(t, pos):
    half = HEAD_DIM // 2
    inv_freq = ROPE_BASE ** (-jnp.arange(half, dtype=jnp.float32) / half)
    ang = pos[:, None] * inv_freq[None, :]
    cos = jnp.cos(ang)[None, :, None, :]
    sin = jnp.sin(ang)[None, :, None, :]
    t1, t2 = t[..., :half], t[..., half:]
    return jnp.concatenate([t1 * cos - t2 * sin, t1 * sin + t2 * cos], axis=-1)


def _retention_group(q, k, v, g, gn_w):
    bsz, seq, _ = q.shape
    C = RET_CHUNK
    nc = seq // C
    pos = jnp.arange(seq, dtype=jnp.float32)

    def heads(t):
        return t.astype(jnp.float32).reshape(bsz, seq, RET_HEADS, HEAD_DIM)

    def chunks(t):
        return t.transpose(0, 2, 1, 3).reshape(bsz, RET_HEADS, nc, C, HEAD_DIM)

    qc = chunks(_rope(heads(q), pos))
    kc = chunks(_rope(heads(k), pos) * HEAD_DIM ** -0.5)
    vc = chunks(heads(v))
    log_g = jnp.log1p(-jnp.exp2(-5.0 - jnp.arange(RET_HEADS, dtype=jnp.float32)))
    lg = log_g[:, None]
    idx = jnp.arange(C, dtype=jnp.float32)

    def bc(w):
        return w[:, None, :, None]

    decay = jnp.exp(jnp.abs(idx[:, None] - idx[None, :]) * log_g[:, None, None])
    s = jnp.einsum('bhnid,bhnjd->bhnij', qc, kc) * decay[:, None]
    o = jnp.einsum('bhnij,bhnjd->bhnid', s, vc)
    kv_f = jnp.einsum('bhnjd,bhnje->nbhde', kc * bc(jnp.exp((C - 1 - idx) * lg)), vc)
    kv_b = jnp.einsum('bhnjd,bhnje->nbhde', kc * bc(jnp.exp(idx * lg)), vc)
    g_chunk = jnp.exp(C * log_g)[:, None, None]

    def step(state, kv):
        return g_chunk * state + kv, state

    init = jnp.zeros((bsz, RET_HEADS, HEAD_DIM, HEAD_DIM), jnp.float32)
    _, st_f = lax.scan(step, init, kv_f)
    _, st_b = lax.scan(step, init, kv_b, reverse=True)
    o = (o
         + jnp.einsum('bhnid,nbhde->bhnie', qc * bc(jnp.exp((idx + 1.0) * lg)), st_f)
         + jnp.einsum('bhnid,nbhde->bhnie', qc * bc(jnp.exp((C - idx) * lg)), st_b))
    mu = jnp.mean(o, axis=-1, keepdims=True)
    oc = o - mu
    var = jnp.mean(oc * oc, axis=-1, keepdims=True)
    o = oc * lax.rsqrt(var + GN_EPS)
    o = o.reshape(bsz, RET_HEADS, seq, HEAD_DIM).transpose(0, 2, 1, 3).reshape(bsz, seq, RET_WIDTH)
    o = o * gn_w.astype(jnp.float32)
    return (jax.nn.silu(g.astype(jnp.float32)) * o).astype(q.dtype)


def _neighbourhood_attention_group(q, k, v, rpb):
    bsz, seq, _ = q.shape
    rows_n = seq // GRID_W
    kh = min(NA_MAX_KH, rows_n)
    nb = GRID_W // NA_QB
    rows = np.arange(rows_n)
    rstart = np.clip(rows - kh // 2, 0, rows_n - kh)
    row_idx = rstart[:, None] + np.arange(kh)
    dr = row_idx - rows[:, None]
    c0 = np.arange(nb) * NA_QB
    sstart = np.clip(c0 - NA_KW // 2, 0, GRID_W - NA_KS)
    col_idx = sstart[:, None] + np.arange(NA_KS)
    qcol = c0[:, None] + np.arange(NA_QB)
    cstart = np.clip(qcol - NA_KW // 2, 0, GRID_W - NA_KW)
    dc = col_idx[:, None, :] - qcol[:, :, None]
    valid = (col_idx[:, None, :] >= cstart[:, :, None]) & (col_idx[:, None, :] < cstart[:, :, None] + NA_KW)
    ri = (dr + NA_MAX_KH - 1)[:, None, None, :, None]
    ci = (np.clip(dc, 1 - NA_KW, NA_KW - 1) + NA_KW - 1)[None, :, :, None, :]
    bias = rpb.astype(jnp.float32)[:, ri, ci]
    bias = jnp.where(valid[None, None, :, :, None, :], bias, -jnp.inf)
    bias = bias.reshape(NA_HEADS, rows_n, nb, NA_QB, kh * NA_KS)

    def heads(t):
        return t.reshape(bsz, seq, NA_HEADS, HEAD_DIM).transpose(0, 2, 1, 3)

    qg = heads(q).reshape(bsz, NA_HEADS, rows_n, nb, NA_QB, HEAD_DIM)
    kgrid = heads(k).reshape(bsz, NA_HEADS, rows_n, GRID_W, HEAD_DIM)
    vgrid = heads(v).reshape(bsz, NA_HEADS, rows_n, GRID_W, HEAD_DIM)
    gr = row_idx[:, None, :, None]
    gc = col_idx[None, :, None, :]
    kb = kgrid[:, :, gr, gc].reshape(bsz, NA_HEADS, rows_n, nb, kh * NA_KS, HEAD_DIM)
    vb = vgrid[:, :, gr, gc].reshape(bsz, NA_HEADS, rows_n, nb, kh * NA_KS, HEAD_DIM)
    s = jnp.einsum('bhrnqd,bhrnkd->bhrnqk', qg, kb).astype(jnp.float32) * HEAD_DIM ** -0.5 + bias[None]
    p = jax.nn.softmax(s, axis=-1).astype(v.dtype)
    o = jnp.einsum('bhrnqk,bhrnkd->bhrnqd', p, vb)
    return o.reshape(bsz, NA_HEADS, seq, HEAD_DIM).transpose(0, 2, 1, 3).reshape(bsz, seq, NA_WIDTH)


def _fwd_setup_inputs(seed: int = 0) -> dict:
    key = jax.random.key(seed)
    ks = jax.random.split(key, 20)

    def nrm(k, shape, scale):
        return jax.random.normal(k, shape, jnp.float32) * scale

    x = nrm(ks[0], (BATCH, SEQ, D_MODEL), 1.0)
    w_in = nrm(ks[1], (DEPTH, D_MODEL, IN_WIDTH), D_MODEL ** -0.5)
    conv_w = nrm(ks[2], (DEPTH, CONV_WIDTH, LRU_WIDTH), CONV_WIDTH ** -0.5)
    conv_b = nrm(ks[3], (DEPTH, LRU_WIDTH), 0.01)
    lru_w_a = nrm(ks[4], (DEPTH, 2, LRU_BLOCKS, LRU_BLOCK_DIM, LRU_BLOCK_DIM), LRU_BLOCK_DIM ** -0.5)
    lru_b_a = nrm(ks[5], (DEPTH, 2, LRU_WIDTH), 0.01)
    lru_w_x = nrm(ks[6], (DEPTH, 2, LRU_BLOCKS, LRU_BLOCK_DIM, LRU_BLOCK_DIM), LRU_BLOCK_DIM ** -0.5)
    lru_b_x = nrm(ks[7], (DEPTH, 2, LRU_WIDTH), 0.01)
    a_pow = jax.random.uniform(ks[8], (DEPTH, 2, LRU_WIDTH), jnp.float32, minval=LRU_A_MIN, maxval=LRU_A_MAX)
    a0 = a_pow ** (1.0 / LRU_C)
    lru_lam = jnp.log(a0) - jnp.log1p(-a0)
    ret_gn_w = 1.0 + nrm(ks[9], (DEPTH, RET_WIDTH), 0.02)
    na_rpb = nrm(ks[10], (DEPTH, NA_HEADS, 2 * NA_MAX_KH - 1, 2 * NA_KW - 1), 0.02)
    w_out = nrm(ks[11], (DEPTH, MIX_WIDTH, D_MODEL), MIX_WIDTH ** -0.5 * DEEPNORM_BETA)
    ln1_g = 1.0 + nrm(ks[12], (DEPTH, D_MODEL), 0.02)
    ln1_b = nrm(ks[13], (DEPTH, D_MODEL), 0.01)
    w_gate = nrm(ks[14], (DEPTH, D_MODEL, D_FF), D_MODEL ** -0.5)
    w_up = nrm(ks[15], (DEPTH, D_MODEL, D_FF), D_MODEL ** -0.5)
    w_down = nrm(ks[16], (DEPTH, D_FF, D_MODEL), D_FF ** -0.5 * DEEPNORM_BETA)
    ln2_g = 1.0 + nrm(ks[17], (DEPTH, D_MODEL), 0.02)
    ln2_b = nrm(ks[18], (DEPTH, D_MODEL), 0.01)
    return {'x': x, 'w_in': w_in, 'conv_w': conv_w, 'conv_b': conv_b,
            'lru_w_a': lru_w_a, 'lru_b_a': lru_b_a, 'lru_w_x': lru_w_x, 'lru_b_x': lru_b_x,
            'lru_lam': lru_lam, 'ret_gn_w': ret_gn_w, 'na_rpb': na_rpb, 'w_out': w_out,
            'ln1_g': ln1_g, 'ln1_b': ln1_b, 'w_gate': w_gate, 'w_up': w_up, 'w_down': w_down,
            'ln2_g': ln2_g, 'ln2_b': ln2_b}


def _fwd_reference(x, w_in, conv_w, conv_b, lru_w_a, lru_b_a, lru_w_x, lru_b_x, lru_lam,
              ret_gn_w, na_rpb, w_out, ln1_g, ln1_b, w_gate, w_up, w_down, ln2_g, ln2_b):
    sizes = [LRU_WIDTH, LRU_WIDTH, RET_WIDTH, RET_WIDTH, RET_WIDTH, RET_WIDTH, NA_WIDTH, NA_WIDTH, NA_WIDTH]
    offsets = [int(o) for o in np.cumsum(sizes)[:-1]]
    for l in range(DEPTH):
        proj = x @ w_in[l]
        lru_x, lru_g, rq, rk, rv, rg, nq, nk, nv = jnp.split(proj, offsets, axis=-1)
        y_lru = _rg_lru_group(lru_x, lru_g, conv_w[l], conv_b[l], lru_w_a[l], lru_b_a[l],
                              lru_w_x[l], lru_b_x[l], lru_lam[l]).astype(proj.dtype)
        y_ret = _retention_group(rq, rk, rv, rg, ret_gn_w[l]).astype(proj.dtype)
        y_na = _neighbourhood_attention_group(nq, nk, nv, na_rpb[l]).astype(proj.dtype)
        mix = jnp.concatenate([y_lru, y_ret, y_na], axis=-1) @ w_out[l]
        x = _layer_norm(DEEPNORM_ALPHA * x + mix, ln1_g[l], ln1_b[l])
        hid = jax.nn.silu(x @ w_gate[l]) * (x @ w_up[l])
        x = _layer_norm(DEEPNORM_ALPHA * x + hid @ w_down[l], ln2_g[l], ln2_b[l])
    return x


import jax as _jax
import jax.numpy as _jnp

TWIN_FORMAT = 'train_step'
FWD_PARAMS = ['x', 'w_in', 'conv_w', 'conv_b', 'lru_w_a', 'lru_b_a', 'lru_w_x', 'lru_b_x', 'lru_lam', 'ret_gn_w', 'na_rpb', 'w_out', 'ln1_g', 'ln1_b', 'w_gate', 'w_up', 'w_down', 'ln2_g', 'ln2_b']
TWIN_WEIGHTS = ['w_in', 'conv_w', 'conv_b', 'lru_w_a', 'lru_b_a', 'lru_w_x', 'lru_b_x', 'lru_lam', 'ret_gn_w', 'na_rpb', 'w_out', 'ln1_g', 'ln1_b', 'w_gate', 'w_up', 'w_down', 'ln2_g', 'ln2_b']
TWIN_DIFF_INPUT = 'x'
TWIN_INPUTS = ['x', 'w_in', 'conv_w', 'conv_b', 'lru_w_a', 'lru_b_a', 'lru_w_x', 'lru_b_x', 'lru_lam', 'ret_gn_w', 'na_rpb', 'w_out', 'ln1_g', 'ln1_b', 'w_gate', 'w_up', 'w_down', 'ln2_g', 'ln2_b', 'loss_target', 'm_w_in', 'm_conv_w', 'm_conv_b', 'm_lru_w_a', 'm_lru_b_a', 'm_lru_w_x', 'm_lru_b_x', 'm_lru_lam', 'm_ret_gn_w', 'm_na_rpb', 'm_w_out', 'm_ln1_g', 'm_ln1_b', 'm_w_gate', 'm_w_up', 'm_w_down', 'm_ln2_g', 'm_ln2_b', 'v_w_in', 'v_conv_w', 'v_conv_b', 'v_lru_w_a', 'v_lru_b_a', 'v_lru_w_x', 'v_lru_b_x', 'v_lru_lam', 'v_ret_gn_w', 'v_na_rpb', 'v_w_out', 'v_ln1_g', 'v_ln1_b', 'v_w_gate', 'v_w_up', 'v_w_down', 'v_ln2_g', 'v_ln2_b']
TWIN_OUTPUTS = ['loss', 'grad_x', 'grad_w_in', 'grad_conv_w', 'grad_conv_b', 'grad_lru_w_a', 'grad_lru_b_a', 'grad_lru_w_x', 'grad_lru_b_x', 'grad_lru_lam', 'grad_ret_gn_w', 'grad_na_rpb', 'grad_w_out', 'grad_ln1_g', 'grad_ln1_b', 'grad_w_gate', 'grad_w_up', 'grad_w_down', 'grad_ln2_g', 'grad_ln2_b', 'delta_w_in', 'delta_conv_w', 'delta_conv_b', 'delta_lru_w_a', 'delta_lru_b_a', 'delta_lru_w_x', 'delta_lru_b_x', 'delta_lru_lam', 'delta_ret_gn_w', 'delta_na_rpb', 'delta_w_out', 'delta_ln1_g', 'delta_ln1_b', 'delta_w_gate', 'delta_w_up', 'delta_w_down', 'delta_ln2_g', 'delta_ln2_b', 'new_m_w_in', 'new_m_conv_w', 'new_m_conv_b', 'new_m_lru_w_a', 'new_m_lru_b_a', 'new_m_lru_w_x', 'new_m_lru_b_x', 'new_m_lru_lam', 'new_m_ret_gn_w', 'new_m_na_rpb', 'new_m_w_out', 'new_m_ln1_g', 'new_m_ln1_b', 'new_m_w_gate', 'new_m_w_up', 'new_m_w_down', 'new_m_ln2_g', 'new_m_ln2_b', 'new_v_w_in', 'new_v_conv_w', 'new_v_conv_b', 'new_v_lru_w_a', 'new_v_lru_b_a', 'new_v_lru_w_x', 'new_v_lru_b_x', 'new_v_lru_lam', 'new_v_ret_gn_w', 'new_v_na_rpb', 'new_v_w_out', 'new_v_ln1_g', 'new_v_ln1_b', 'new_v_w_gate', 'new_v_w_up', 'new_v_w_down', 'new_v_ln2_g', 'new_v_ln2_b']
TWIN_LEAF_KINDS = {'loss': 'loss', 'grad_x': 'grad_x', 'grad_w_in': 'grad_w', 'grad_conv_w': 'grad_w', 'grad_conv_b': 'grad_w', 'grad_lru_w_a': 'grad_w', 'grad_lru_b_a': 'grad_w', 'grad_lru_w_x': 'grad_w', 'grad_lru_b_x': 'grad_w', 'grad_lru_lam': 'grad_w', 'grad_ret_gn_w': 'grad_w', 'grad_na_rpb': 'grad_w', 'grad_w_out': 'grad_w', 'grad_ln1_g': 'grad_w', 'grad_ln1_b': 'grad_w', 'grad_w_gate': 'grad_w', 'grad_w_up': 'grad_w', 'grad_w_down': 'grad_w', 'grad_ln2_g': 'grad_w', 'grad_ln2_b': 'grad_w', 'delta_w_in': 'delta_w', 'delta_conv_w': 'delta_w', 'delta_conv_b': 'delta_w', 'delta_lru_w_a': 'delta_w', 'delta_lru_b_a': 'delta_w', 'delta_lru_w_x': 'delta_w', 'delta_lru_b_x': 'delta_w', 'delta_lru_lam': 'delta_w', 'delta_ret_gn_w': 'delta_w', 'delta_na_rpb': 'delta_w', 'delta_w_out': 'delta_w', 'delta_ln1_g': 'delta_w', 'delta_ln1_b': 'delta_w', 'delta_w_gate': 'delta_w', 'delta_w_up': 'delta_w', 'delta_w_down': 'delta_w', 'delta_ln2_g': 'delta_w', 'delta_ln2_b': 'delta_w', 'new_m_w_in': 'new_m', 'new_m_conv_w': 'new_m', 'new_m_conv_b': 'new_m', 'new_m_lru_w_a': 'new_m', 'new_m_lru_b_a': 'new_m', 'new_m_lru_w_x': 'new_m', 'new_m_lru_b_x': 'new_m', 'new_m_lru_lam': 'new_m', 'new_m_ret_gn_w': 'new_m', 'new_m_na_rpb': 'new_m', 'new_m_w_out': 'new_m', 'new_m_ln1_g': 'new_m', 'new_m_ln1_b': 'new_m', 'new_m_w_gate': 'new_m', 'new_m_w_up': 'new_m', 'new_m_w_down': 'new_m', 'new_m_ln2_g': 'new_m', 'new_m_ln2_b': 'new_m', 'new_v_w_in': 'new_v', 'new_v_conv_w': 'new_v', 'new_v_conv_b': 'new_v', 'new_v_lru_w_a': 'new_v', 'new_v_lru_b_a': 'new_v', 'new_v_lru_w_x': 'new_v', 'new_v_lru_b_x': 'new_v', 'new_v_lru_lam': 'new_v', 'new_v_ret_gn_w': 'new_v', 'new_v_na_rpb': 'new_v', 'new_v_w_out': 'new_v', 'new_v_ln1_g': 'new_v', 'new_v_ln1_b': 'new_v', 'new_v_w_gate': 'new_v', 'new_v_w_up': 'new_v', 'new_v_w_down': 'new_v', 'new_v_ln2_g': 'new_v', 'new_v_ln2_b': 'new_v'}


def _forward(args):
    return _fwd_reference(*[args[k] for k in FWD_PARAMS])


def _output_shape():
    out = _jax.eval_shape(lambda: _forward(_fwd_setup_inputs(0)))
    return out.shape, out.dtype

N_MICROBATCH = 1
ADAM_LR = 0.001
ADAM_B1 = 0.9
ADAM_B2 = 0.999
ADAM_EPS = 1e-08
ADAM_WD = 0.01
ADAM_STEP = 10
PER_EXAMPLE_BATCH_AXIS = {'x': 0, 'loss_target': 0}
SHARED_INPUTS = []
_WEIGHT_DTYPES = {'w_in': _jnp.float32, 'conv_w': _jnp.float32, 'conv_b': _jnp.float32, 'lru_w_a': _jnp.float32, 'lru_b_a': _jnp.float32, 'lru_w_x': _jnp.float32, 'lru_b_x': _jnp.float32, 'lru_lam': _jnp.float32, 'ret_gn_w': _jnp.float32, 'na_rpb': _jnp.float32, 'w_out': _jnp.float32, 'ln1_g': _jnp.float32, 'ln1_b': _jnp.float32, 'w_gate': _jnp.float32, 'w_up': _jnp.float32, 'w_down': _jnp.float32, 'ln2_g': _jnp.float32, 'ln2_b': _jnp.float32}
MOMENT_SCALE = {'w_in': 2.658643e-02, 'conv_w': 4.308237e-02, 'conv_b': 6.110209e-01, 'lru_w_a': 1.115628e-02, 'lru_b_a': 8.302682e-03, 'lru_w_x': 2.040276e-02, 'lru_b_x': 9.094714e-03, 'lru_lam': 1.473106e-02, 'ret_gn_w': 2.833380e-02, 'na_rpb': 3.168993e-03, 'w_out': 7.172026e-02, 'ln1_g': 9.334075e-01, 'ln1_b': 4.602869e-01, 'w_gate': 1.694871e-02, 'w_up': 1.642039e-02, 'w_down': 6.491970e-02, 'ln2_g': 1.605640e+01, 'ln2_b': 9.063427e-01}


def _to_microbatches(a, axis):
    t = _jnp.moveaxis(a, axis, 0)
    t = t.reshape((N_MICROBATCH, t.shape[0] // N_MICROBATCH) + t.shape[1:])
    return _jnp.moveaxis(t, 1, axis + 1)


def setup_inputs(seed: int = 0) -> dict:
    inp = _fwd_setup_inputs(seed)
    key = _jax.random.fold_in(_jax.random.key(seed), 7919)
    shape, _ = _output_shape()
    out = dict(inp)
    out["loss_target"] = _jax.random.normal(_jax.random.fold_in(key, 0), shape, _jnp.float32)
    for i, name in enumerate(TWIN_WEIGHTS):
        w = inp[name].astype(_jnp.float32)
        if MOMENT_SCALE is None:
            s = _jnp.sqrt(_jnp.mean(_jnp.square(w)) + 1e-30)
        else:
            s = MOMENT_SCALE[name]
        km, kv = _jax.random.split(_jax.random.fold_in(key, i + 1))
        out[name] = w
        out["m_" + name] = s * _jax.random.normal(km, w.shape, _jnp.float32)
        out["v_" + name] = (s * s) * _jax.random.uniform(kv, w.shape, _jnp.float32, 0.5, 1.5)
    if N_MICROBATCH > 1:
        for name, axis in PER_EXAMPLE_BATCH_AXIS.items():
            out[name] = _to_microbatches(out[name], axis)
    return {'x': out['x'], 'w_in': out['w_in'], 'conv_w': out['conv_w'], 'conv_b': out['conv_b'], 'lru_w_a': out['lru_w_a'], 'lru_b_a': out['lru_b_a'], 'lru_w_x': out['lru_w_x'], 'lru_b_x': out['lru_b_x'], 'lru_lam': out['lru_lam'], 'ret_gn_w': out['ret_gn_w'], 'na_rpb': out['na_rpb'], 'w_out': out['w_out'], 'ln1_g': out['ln1_g'], 'ln1_b': out['ln1_b'], 'w_gate': out['w_gate'], 'w_up': out['w_up'], 'w_down': out['w_down'], 'ln2_g': out['ln2_g'], 'ln2_b': out['ln2_b'], 'loss_target': out['loss_target'], 'm_w_in': out['m_w_in'], 'm_conv_w': out['m_conv_w'], 'm_conv_b': out['m_conv_b'], 'm_lru_w_a': out['m_lru_w_a'], 'm_lru_b_a': out['m_lru_b_a'], 'm_lru_w_x': out['m_lru_w_x'], 'm_lru_b_x': out['m_lru_b_x'], 'm_lru_lam': out['m_lru_lam'], 'm_ret_gn_w': out['m_ret_gn_w'], 'm_na_rpb': out['m_na_rpb'], 'm_w_out': out['m_w_out'], 'm_ln1_g': out['m_ln1_g'], 'm_ln1_b': out['m_ln1_b'], 'm_w_gate': out['m_w_gate'], 'm_w_up': out['m_w_up'], 'm_w_down': out['m_w_down'], 'm_ln2_g': out['m_ln2_g'], 'm_ln2_b': out['m_ln2_b'], 'v_w_in': out['v_w_in'], 'v_conv_w': out['v_conv_w'], 'v_conv_b': out['v_conv_b'], 'v_lru_w_a': out['v_lru_w_a'], 'v_lru_b_a': out['v_lru_b_a'], 'v_lru_w_x': out['v_lru_w_x'], 'v_lru_b_x': out['v_lru_b_x'], 'v_lru_lam': out['v_lru_lam'], 'v_ret_gn_w': out['v_ret_gn_w'], 'v_na_rpb': out['v_na_rpb'], 'v_w_out': out['v_w_out'], 'v_ln1_g': out['v_ln1_g'], 'v_ln1_b': out['v_ln1_b'], 'v_w_gate': out['v_w_gate'], 'v_w_up': out['v_w_up'], 'v_w_down': out['v_w_down'], 'v_ln2_g': out['v_ln2_g'], 'v_ln2_b': out['v_ln2_b']}


def _loss(weights, diff, rest, loss_target):
    with _jax.named_scope("forward"):
        args = {**rest, TWIN_DIFF_INPUT: diff, **{k: w.astype(_WEIGHT_DTYPES[k]) for k, w in weights.items()}}
        y = _forward(args)
    with _jax.named_scope("loss_head"):
        err = _jnp.square(y.astype(_jnp.float32) - loss_target)
        return 0.5 * _jnp.sum(_jnp.mean(err, axis=-1)) if err.ndim else 0.5 * err


def _adamw(w, g, m, v):
    m = ADAM_B1 * m + (1.0 - ADAM_B1) * g
    v = ADAM_B2 * v + (1.0 - ADAM_B2) * _jnp.square(g)
    m_hat = m / (1.0 - ADAM_B1 ** ADAM_STEP)
    v_hat = v / (1.0 - ADAM_B2 ** ADAM_STEP)
    delta = -ADAM_LR * (m_hat / (_jnp.sqrt(v_hat) + ADAM_EPS) + ADAM_WD * w)
    return delta, m, v


def reference(x, w_in, conv_w, conv_b, lru_w_a, lru_b_a, lru_w_x, lru_b_x, lru_lam, ret_gn_w, na_rpb, w_out, ln1_g, ln1_b, w_gate, w_up, w_down, ln2_g, ln2_b, loss_target, m_w_in, m_conv_w, m_conv_b, m_lru_w_a, m_lru_b_a, m_lru_w_x, m_lru_b_x, m_lru_lam, m_ret_gn_w, m_na_rpb, m_w_out, m_ln1_g, m_ln1_b, m_w_gate, m_w_up, m_w_down, m_ln2_g, m_ln2_b, v_w_in, v_conv_w, v_conv_b, v_lru_w_a, v_lru_b_a, v_lru_w_x, v_lru_b_x, v_lru_lam, v_ret_gn_w, v_na_rpb, v_w_out, v_ln1_g, v_ln1_b, v_w_gate, v_w_up, v_w_down, v_ln2_g, v_ln2_b):
    given = dict(x=x, w_in=w_in, conv_w=conv_w, conv_b=conv_b, lru_w_a=lru_w_a, lru_b_a=lru_b_a, lru_w_x=lru_w_x, lru_b_x=lru_b_x, lru_lam=lru_lam, ret_gn_w=ret_gn_w, na_rpb=na_rpb, w_out=w_out, ln1_g=ln1_g, ln1_b=ln1_b, w_gate=w_gate, w_up=w_up, w_down=w_down, ln2_g=ln2_g, ln2_b=ln2_b, loss_target=loss_target, m_w_in=m_w_in, m_conv_w=m_conv_w, m_conv_b=m_conv_b, m_lru_w_a=m_lru_w_a, m_lru_b_a=m_lru_b_a, m_lru_w_x=m_lru_w_x, m_lru_b_x=m_lru_b_x, m_lru_lam=m_lru_lam, m_ret_gn_w=m_ret_gn_w, m_na_rpb=m_na_rpb, m_w_out=m_w_out, m_ln1_g=m_ln1_g, m_ln1_b=m_ln1_b, m_w_gate=m_w_gate, m_w_up=m_w_up, m_w_down=m_w_down, m_ln2_g=m_ln2_g, m_ln2_b=m_ln2_b, v_w_in=v_w_in, v_conv_w=v_conv_w, v_conv_b=v_conv_b, v_lru_w_a=v_lru_w_a, v_lru_b_a=v_lru_b_a, v_lru_w_x=v_lru_w_x, v_lru_b_x=v_lru_b_x, v_lru_lam=v_lru_lam, v_ret_gn_w=v_ret_gn_w, v_na_rpb=v_na_rpb, v_w_out=v_w_out, v_ln1_g=v_ln1_g, v_ln1_b=v_ln1_b, v_w_gate=v_w_gate, v_w_up=v_w_up, v_w_down=v_w_down, v_ln2_g=v_ln2_g, v_ln2_b=v_ln2_b)
    weights = {n: given[n] for n in TWIN_WEIGHTS}
    shared = {n: given[n] for n in SHARED_INPUTS}
    per_example = {n: given[n] for n in ['x']}
    grad_fn = _jax.value_and_grad(_loss, argnums=(0, 1))

    def one_microbatch(ex, loss_target):
        ex = dict(ex)
        diff = ex.pop(TWIN_DIFF_INPUT)
        return grad_fn(weights, diff, {**shared, **ex}, loss_target)

    if N_MICROBATCH == 1:
        loss, (grad_w, grad_x) = one_microbatch(per_example, given["loss_target"])
    else:
        def body(carry, xs):
            loss_sum, grad_sum = carry
            l_k, (gw_k, gx_k) = one_microbatch(xs[0], xs[1])
            with _jax.named_scope("update"):
                return (loss_sum + l_k, _jax.tree.map(_jnp.add, grad_sum, gw_k)), gx_k

        init = (_jnp.zeros((), _jnp.float32), _jax.tree.map(_jnp.zeros_like, weights))
        (loss, grad_w), grad_x = _jax.lax.scan(body, init, (per_example, given["loss_target"]))
    with _jax.named_scope("update"):
        delta_w, new_m, new_v = {}, {}, {}
        for n in TWIN_WEIGHTS:
            delta_w[n], new_m[n], new_v[n] = _adamw(weights[n], grad_w[n], given["m_" + n], given["v_" + n])
    return (loss, grad_x, *[grad_w[n] for n in TWIN_WEIGHTS], *[delta_w[n] for n in TWIN_WEIGHTS],
            *[new_m[n] for n in TWIN_WEIGHTS], *[new_v[n] for n in TWIN_WEIGHTS])
```

```python
import functools
import math

import numpy as np
import jax
import jax.numpy as jnp
from jax import lax
from jax.experimental import pallas as pl
from jax.experimental.pallas import tpu as pltpu

F32 = jnp.float32
_BF = jnp.bfloat16

D_MODEL = 1024
DEPTH = 4
GRID_W = 64
HEAD_DIM = 64
LRU_WIDTH = 384
RET_WIDTH = 384
RET_HEADS = 6
NA_WIDTH = 256
NA_HEADS = 4
IN_WIDTH = 3072
CONV_WIDTH = 4
LRU_C = 8.0
RET_CHUNK = 128
ROPE_BASE = 10000.0
GN_EPS = 1e-6
NA_KH = 8
NA_KW = 16
D_FF = 2816
FF_BLK = 352
FF_PAD = 384
N_DEV = 8
ALPHA = (2 * DEPTH) ** 0.25
LN_EPS = 1e-5
ADAM_LR = 0.001
ADAM_B1 = 0.9
ADAM_B2 = 0.999
ADAM_EPS = 1e-08
ADAM_WD = 0.01
ADAM_STEP = 10

LANE = 128
SUB = 8
VMEM_MB = 56
NEG = -1e30

MESH = pl.DeviceIdType.MESH


def _cparams(sem=None, vmem_mb=VMEM_MB):
    return pltpu.CompilerParams(dimension_semantics=sem, vmem_limit_bytes=vmem_mb << 20)


def _mm(a, b):
    return jnp.dot(a.astype(_BF), b.astype(_BF), preferred_element_type=F32)


def _mm_nt(a, b):
    return lax.dot_general(a.astype(_BF), b.astype(_BF), (((1,), (1,)), ((), ())), preferred_element_type=F32)


def _mm_tn(a, b):
    return lax.dot_general(a.astype(_BF), b.astype(_BF), (((0,), (0,)), ((), ())), preferred_element_type=F32)


def _sigmoid(x):
    return jax.nn.sigmoid(x)


def _rows(start, size):
    return pl.ds(pl.multiple_of(start, SUB), size)


def _strip(T, col):
    return pl.BlockSpec((T, LANE), lambda j: (0, col(j)), pipeline_mode=pl.Buffered(1))


LRU_CH = 256
_GELU_C0 = math.sqrt(2.0 / math.pi)
_GELU_C1 = 0.044715


def _gelu_parts(x):
    x2 = x * x
    t = jnp.tanh(_GELU_C0 * (x + _GELU_C1 * x * x2))
    val = 0.5 * x * (1.0 + t)
    der = 0.5 * (1.0 + t) + 0.5 * x * (1.0 - t * t) * _GELU_C0 * (1.0 + 3.0 * _GELU_C1 * x2)
    return val, der


def _softplus_neg(lam):
    e = jnp.exp(-jnp.abs(lam))
    w = 1.0 + e
    l1p = jnp.where(w == 1.0, e, jnp.log(w) * (e / jnp.where(w == 1.0, 1.0, w - 1.0)))
    return jnp.maximum(-lam, 0.0) + l1p


def _window(ref, t0, ch, T):
    prev = ref[_rows(jnp.maximum(t0 - SUB, 0), SUB), :].astype(F32)
    nxt = ref[_rows(jnp.minimum(t0 + ch, T - SUB), SUB), :].astype(F32)
    prev = jnp.where(t0 > 0, prev, 0.0)
    nxt = jnp.where(t0 + ch < T, nxt, 0.0)
    return jnp.concatenate([prev, ref[_rows(t0, ch), :].astype(F32), nxt], axis=0)


def _tap(win, shift, ch):
    n = win.shape[0]
    return pltpu.roll(win, (-shift) % n, 0)[SUB:SUB + ch]


def _lru_conv(xb_ref, vec, t0, T):
    win = _window(xb_ref, t0, LRU_CH, T)
    xc = jnp.broadcast_to(vec[4:5, :], (LRU_CH, LANE))
    for j in range(CONV_WIDTH):
        xc = xc + _tap(win, j - CONV_WIDTH // 2, LRU_CH) * vec[j:j + 1, :]
    return xc


def _lru_dir(pre_a, pre_x, sp):
    r = _sigmoid(pre_a)
    i = _sigmoid(pre_x)
    log_a = (-LRU_C) * r * sp
    a = jnp.exp(log_a)
    z = jnp.tanh(-log_a) * (a * a + 1.0)
    s = jnp.sqrt(z)
    return r, i, a, s


def _scan_tile(a, b, reverse, row):
    for k in (1, 2, 4):
        if not reverse:
            a_s, b_s, m = pltpu.roll(a, k, 0), pltpu.roll(b, k, 0), row >= k
        else:
            a_s, b_s, m = pltpu.roll(a, SUB - k, 0), pltpu.roll(b, SUB - k, 0), row < SUB - k
        b = jnp.where(m, a * b_s + b, b)
        a = jnp.where(m, a * a_s, a)
    return a, b


def _bcast_row(x, r):
    return jnp.broadcast_to(x[r:r + 1, :], (SUB, LANE))


def _lru_prepare(xb_ref, w4_ref, vec, xc_ref, af_ref, uf_ref, ab_ref, ub_ref, T):
    sp_f = _softplus_neg(vec[9:10, :])
    sp_b = _softplus_neg(vec[10:11, :])
    w4 = w4_ref[0]

    def body(c, carry):
        t0 = c * LRU_CH
        xc = _lru_conv(xb_ref, vec, t0, T)
        if xc_ref is not None:
            xc_ref[_rows(t0, LRU_CH), :] = xc
        pre = _mm(xc, w4)
        _, i, a, s = _lru_dir(pre[:, 0:128] + vec[5:6, :], pre[:, 128:256] + vec[6:7, :], sp_f)
        af_ref[_rows(t0, LRU_CH), :] = a
        uf_ref[_rows(t0, LRU_CH), :] = s * (i * xc)
        _, i, a, s = _lru_dir(pre[:, 256:384] + vec[7:8, :], pre[:, 384:512] + vec[8:9, :], sp_b)
        ab_ref[_rows(t0, LRU_CH), :] = a
        ub_ref[_rows(t0, LRU_CH), :] = s * (i * xc)
        return carry

    lax.fori_loop(0, T // LRU_CH, body, 0)


def _lru_scan(af_ref, uf_ref, ab_ref, ub_ref, T):
    nt = T // SUB
    row = lax.broadcasted_iota(jnp.int32, (SUB, LANE), 0)

    def body(j, carry):
        hf, hb = carry
        sf = _rows(j * SUB, SUB)
        sb = _rows((nt - 1 - j) * SUB, SUB)
        a, b = _scan_tile(af_ref[sf, :], uf_ref[sf, :], False, row)
        h = a * hf + b
        uf_ref[sf, :] = h
        hf = _bcast_row(h, SUB - 1)
        a, b = _scan_tile(ab_ref[sb, :], ub_ref[sb, :], True, row)
        h = a * hb + b
        ub_ref[sb, :] = h
        hb = _bcast_row(h, 0)
        return hf, hb

    z = jnp.zeros((SUB, LANE), F32)
    lax.fori_loop(0, nt, body, (z, z), unroll=4)


def _lru_fwd_call(proj, vec, w4):
    T = proj.shape[0]

    def body(xb_ref, gate_ref, vec_ref, w4_ref, y_ref, af_ref, uf_ref, ab_ref, ub_ref):
        vec = vec_ref[...]
        _lru_prepare(xb_ref, w4_ref, vec, None, af_ref, uf_ref, ab_ref, ub_ref, T)
        _lru_scan(af_ref, uf_ref, ab_ref, ub_ref, T)

        def out(c, carry):
            rows = _rows(c * LRU_CH, LRU_CH)
            gl, _ = _gelu_parts(gate_ref[rows, :])
            y_ref[rows, :] = (uf_ref[rows, :] + ub_ref[rows, :]) * gl
            return carry

        lax.fori_loop(0, T // LRU_CH, out, 0)

    return pl.pallas_call(
        body, name="lru_fwd", grid=(LRU_WIDTH // LANE,),
        in_specs=[_strip(T, lambda j: j), _strip(T, lambda j: j + 3),
                  pl.BlockSpec((16, LANE), lambda j: (0, j)),
                  pl.BlockSpec((1, LANE, 4 * LANE), lambda j: (j, 0, 0))],
        out_specs=_strip(T, lambda j: j),
        out_shape=jax.ShapeDtypeStruct((T, LRU_WIDTH), F32),
        scratch_shapes=[pltpu.VMEM((T, LANE), F32)] * 4,
        compiler_params=_cparams(("arbitrary",)),
    )(proj, proj, vec, w4)


def _lru_bwd_call(proj, dycat, vec, w4):
    T = proj.shape[0]
    nt = T // SUB
    nch = T // LRU_CH

    def body(xb_ref, gate_ref, dy_ref, vec_ref, w4_ref, dp_ref, dvec_ref, dw4_ref,
             xc_ref, af_ref, hf_ref, ab_ref, hb_ref, dh_ref):
        dxb_ref, dgate_ref = dp_ref.at[0], dp_ref.at[1]
        vec = vec_ref[...]
        _lru_prepare(xb_ref, w4_ref, vec, xc_ref, af_ref, hf_ref, ab_ref, hb_ref, T)
        _lru_scan(af_ref, hf_ref, ab_ref, hb_ref, T)

        def gate_bwd(c, carry):
            rows = _rows(c * LRU_CH, LRU_CH)
            gl, dgl = _gelu_parts(gate_ref[rows, :])
            dy = dy_ref[rows, :]
            dgate_ref[rows, :] = (dy * (hf_ref[rows, :] + hb_ref[rows, :]) * dgl).astype(dgate_ref.dtype)
            dh_ref[rows, :] = dy * gl
            return carry

        lax.fori_loop(0, nch, gate_bwd, 0)

        row = lax.broadcasted_iota(jnp.int32, (SUB, LANE), 0)

        def adj(j, carry):
            gf, a_next, gb, a_prev = carry
            tf = nt - 1 - j
            sf = _rows(tf * SUB, SUB)
            a_t = af_ref[sf, :]
            h_t = hf_ref[sf, :]
            coef = jnp.where(row == SUB - 1, a_next, pltpu.roll(a_t, SUB - 1, 0))
            ac, bc = _scan_tile(coef, dh_ref[sf, :], True, row)
            g = ac * gf + bc
            h_prev = hf_ref[_rows(jnp.maximum(tf - 1, 0) * SUB, SUB), :]
            h_prev = jnp.where(tf > 0, _bcast_row(h_prev, SUB - 1), 0.0)
            hs = jnp.where(row == 0, h_prev, pltpu.roll(h_t, 1, 0))
            af_ref[sf, :] = g * hs
            hf_ref[sf, :] = g
            gf = _bcast_row(g, 0)
            a_next = _bcast_row(a_t, 0)
            sb = _rows(j * SUB, SUB)
            a_t = ab_ref[sb, :]
            h_t = hb_ref[sb, :]
            coef = jnp.where(row == 0, a_prev, pltpu.roll(a_t, 1, 0))
            ac, bc = _scan_tile(coef, dh_ref[sb, :], False, row)
            g = ac * gb + bc
            h_next = hb_ref[_rows(jnp.minimum(j + 1, nt - 1) * SUB, SUB), :]
            h_next = jnp.where(j < nt - 1, _bcast_row(h_next, 0), 0.0)
            hs = jnp.where(row == SUB - 1, h_next, pltpu.roll(h_t, SUB - 1, 0))
            ab_ref[sb, :] = g * hs
            hb_ref[sb, :] = g
            gb = _bcast_row(g, SUB - 1)
            a_prev = _bcast_row(a_t, SUB - 1)
            return gf, a_next, gb, a_prev

        z = jnp.zeros((SUB, LANE), F32)
        lax.fori_loop(0, nt, adj, (z, z, z, z), unroll=2)

        sp_f = _softplus_neg(vec[9:10, :])
        sp_b = _softplus_neg(vec[10:11, :])
        w4 = w4_ref[0]
        dw4_ref[...] = jnp.zeros_like(dw4_ref)

        def one_dir(pre_a, pre_x, sp, xc, du, da):
            r, i, a, s = _lru_dir(pre_a, pre_x, sp)
            d_i = du * s * xc
            dxc = du * s * i
            d_s = du * i * xc
            d_log = da * a - d_s * (a * a) / s
            d_r = d_log * (-LRU_C) * sp
            d_sp = jnp.sum(d_log * (-LRU_C) * r, axis=0, keepdims=True)
            return d_r * r * (1.0 - r), d_i * i * (1.0 - i), dxc, d_sp

        def gates_bwd(c, carry):
            db, dspf, dspb = carry
            rows = _rows(c * LRU_CH, LRU_CH)
            xc = xc_ref[rows, :]
            pre = _mm(xc, w4)
            dpa_f, dpx_f, dxc_f, d_sp_f = one_dir(pre[:, 0:128] + vec[5:6, :], pre[:, 128:256] + vec[6:7, :],
                                                  sp_f, xc, hf_ref[rows, :], af_ref[rows, :])
            dpa_b, dpx_b, dxc_b, d_sp_b = one_dir(pre[:, 256:384] + vec[7:8, :], pre[:, 384:512] + vec[8:9, :],
                                                  sp_b, xc, hb_ref[rows, :], ab_ref[rows, :])
            dpre = jnp.concatenate([dpa_f, dpx_f, dpa_b, dpx_b], axis=1)
            dw4_ref[0] += _mm_tn(xc, dpre)
            dh_ref[rows, :] = dxc_f + dxc_b + _mm_nt(dpre, w4)
            return db + jnp.sum(dpre, axis=0, keepdims=True), dspf + d_sp_f, dspb + d_sp_b

        z1 = jnp.zeros((1, LANE), F32)
        db, dspf, dspb = lax.fori_loop(0, nch, gates_bwd, (jnp.zeros((1, 4 * LANE), F32), z1, z1))

        def conv_bwd(c, carry):
            t0 = c * LRU_CH
            rows = _rows(t0, LRU_CH)
            dwin = _window(dh_ref, t0, LRU_CH, T)
            xwin = _window(xb_ref, t0, LRU_CH, T)
            dxc = dh_ref[rows, :]
            dxb = jnp.zeros((LRU_CH, LANE), F32)
            out = []
            for j in range(CONV_WIDTH):
                off = j - CONV_WIDTH // 2
                dxb = dxb + _tap(dwin, -off, LRU_CH) * vec[j:j + 1, :]
                out.append(carry[j] + jnp.sum(dxc * _tap(xwin, off, LRU_CH), axis=0, keepdims=True))
            dxb_ref[rows, :] = dxb.astype(dxb_ref.dtype)
            out.append(carry[CONV_WIDTH] + jnp.sum(dxc, axis=0, keepdims=True))
            return tuple(out)

        dconv = lax.fori_loop(0, nch, conv_bwd, (z1,) * (CONV_WIDTH + 1))
        dlam_f = dspf * (-_sigmoid(-vec[9:10, :]))
        dlam_b = dspb * (-_sigmoid(-vec[10:11, :]))
        dvec_ref[...] = jnp.concatenate(
            list(dconv) + [db[:, 0:128], db[:, 128:256], db[:, 256:384], db[:, 384:512], dlam_f, dlam_b,
                           jnp.zeros((5, LANE), F32)], axis=0)

    ns = LRU_WIDTH // LANE
    return pl.pallas_call(
        body, name="lru_bwd", grid=(ns,),
        in_specs=[_strip(T, lambda j: j), _strip(T, lambda j: j + 3), _strip(T, lambda j: j),
                  pl.BlockSpec((16, LANE), lambda j: (0, j)),
                  pl.BlockSpec((1, LANE, 4 * LANE), lambda j: (j, 0, 0))],
        out_specs=[pl.BlockSpec((2, T, LANE), lambda j: (0, 0, j), pipeline_mode=pl.Buffered(1)),
                   pl.BlockSpec((16, LANE), lambda j: (0, j)),
                   pl.BlockSpec((1, LANE, 4 * LANE), lambda j: (j, 0, 0))],
        out_shape=[jax.ShapeDtypeStruct((2, T, LRU_WIDTH), _BF),
                   jax.ShapeDtypeStruct((16, LRU_WIDTH), F32), jax.ShapeDtypeStruct((ns, LANE, 4 * LANE), F32)],
        scratch_shapes=[pltpu.VMEM((T, LANE), F32)] * 6,
        compiler_params=_cparams(("arbitrary",)),
    )(proj, proj, dycat, vec, w4)


def _lru_pack(cw, cb, wa, ba, wx, bx, lam):
    vec = jnp.concatenate([cw, cb[None], ba[0:1], bx[0:1], ba[1:2], bx[1:2], lam, jnp.zeros((5, LRU_WIDTH), F32)], axis=0)
    eye = jnp.eye(2, dtype=F32)
    mats = []
    for w in (wa[0], wx[0], wa[1], wx[1]):
        bd = jnp.einsum("jsio,st->jsito", w.reshape(3, 2, 64, 64), eye)
        mats.append(bd.reshape(3, LANE, LANE))
    return vec, jnp.concatenate(mats, axis=2).astype(_BF)


def _lru_unpack(dvec, dw4):
    def blocks(m):
        m = m.reshape(3, 2, 64, 2, 64)
        return jnp.stack([m[:, 0, :, 0, :], m[:, 1, :, 1, :]], axis=1).reshape(6, 64, 64)
    parts = [blocks(dw4[:, :, k * LANE:(k + 1) * LANE]) for k in range(4)]
    dwa = jnp.stack([parts[0], parts[2]])
    dwx = jnp.stack([parts[1], parts[3]])
    dba = jnp.stack([dvec[5], dvec[7]])
    dbx = jnp.stack([dvec[6], dvec[8]])
    return dvec[0:4], dvec[4], dwa, dba, dwx, dbx, dvec[9:11]


RC = RET_CHUNK


def _ret_tables(T):
    half = HEAD_DIM // 2
    pos = jnp.arange(T, dtype=F32)
    inv_freq = ROPE_BASE ** (-jnp.arange(half, dtype=F32) / half)
    ang = pos[:, None] * inv_freq[None, :]
    cos = jnp.tile(jnp.cos(ang), (1, 4))
    sin = jnp.tile(jnp.concatenate([-jnp.sin(ang), jnp.sin(ang)], axis=1), (1, 2))
    log_g = jnp.log1p(-jnp.exp2(-5.0 - jnp.arange(RET_HEADS, dtype=F32)))
    idx = jnp.arange(RC, dtype=F32)
    dec = jnp.exp(jnp.abs(idx[:, None] - idx[None, :]) * log_g[:, None, None])
    lg = jnp.repeat(log_g, HEAD_DIM).reshape(3, 1, LANE)
    col = idx[None, :, None]
    rtab = jnp.stack([jnp.exp((RC - 1 - col) * lg), jnp.exp(col * lg),
                      jnp.exp((col + 1.0) * lg), jnp.exp((RC - col) * lg)], axis=1)
    gch = jnp.broadcast_to(jnp.exp(RC * lg), (3, SUB, LANE))
    return cos, sin, dec, rtab, gch


def _swap32(x, lane):
    return jnp.where((lane & 32) == 0, pltpu.roll(x, LANE - 32, 1), pltpu.roll(x, 32, 1))


def _head_mean(x, m0, m1):
    s0 = jnp.sum(x * m0, axis=-1, keepdims=True)
    s1 = jnp.sum(x * m1, axis=-1, keepdims=True)
    return (s0 * m0 + s1 * m1) * (1.0 / HEAD_DIM)


def _ret_masks():
    lane = lax.broadcasted_iota(jnp.int32, (RC, LANE), 1)
    m0 = (lane < HEAD_DIM).astype(F32)
    r = lax.broadcasted_iota(jnp.int32, (LANE, LANE), 0) // HEAD_DIM
    c = lax.broadcasted_iota(jnp.int32, (LANE, LANE), 1) // HEAD_DIM
    return lane, m0, 1.0 - m0, (r == c).astype(F32)


def _ret_specs(T):
    const = lambda shape, imap: pl.BlockSpec(shape, imap)
    return [_strip(T, lambda j: j + 6), _strip(T, lambda j: j + 9), _strip(T, lambda j: j + 12),
            _strip(T, lambda j: j + 15),
            pl.BlockSpec((T, LANE), lambda j: (0, 0), pipeline_mode=pl.Buffered(1)),
            pl.BlockSpec((T, LANE), lambda j: (0, 0), pipeline_mode=pl.Buffered(1)),
            const((2, RC, RC), lambda j: (j, 0, 0)),
            const((1, 4, RC, LANE), lambda j: (j, 0, 0, 0)),
            const((1, SUB, LANE), lambda j: (j, 0, 0)),
            const((SUB, LANE), lambda j: (0, j))]


def _ret_fwd_call(proj, tables, gnw8):
    T = proj.shape[0]
    nc = T // RC
    cos, sin, dec, rtab, gch = tables

    def body(q_ref, k_ref, v_ref, g_ref, cos_ref, sin_ref, dec_ref, rtab_ref, gch_ref, gnw_ref, y_ref, stf_ref):
        lane, m0, m1, bd = _ret_masks()
        gch_v = gch_ref[0][0:1, :]
        gnw = gnw_ref[0:1, :]
        dkf, dkb, dqf, dqb = rtab_ref[0, 0], rtab_ref[0, 1], rtab_ref[0, 2], rtab_ref[0, 3]

        def rope(x, rows):
            return x * cos_ref[rows, :] + _swap32(x, lane) * sin_ref[rows, :]

        def pass_a(n, st):
            rows = _rows(n * RC, RC)
            stf_ref[n] = st
            kr = rope(k_ref[rows, :], rows) * (HEAD_DIM ** -0.5)
            return gch_v * st + _mm_tn(kr * dkf, v_ref[rows, :]) * bd

        lax.fori_loop(0, nc, pass_a, jnp.zeros((LANE, LANE), F32))

        def pass_b(i, stb):
            n = nc - 1 - i
            rows = _rows(n * RC, RC)
            qr = rope(q_ref[rows, :], rows)
            kr = rope(k_ref[rows, :], rows) * (HEAD_DIM ** -0.5)
            v = v_ref[rows, :]
            o = _mm(qr * dqf, stf_ref[n]) + _mm(qr * dqb, stb)
            for h, m in ((0, m0), (1, m1)):
                s = _mm_nt(qr * m, kr) * dec_ref[h]
                o = o + _mm(s, v * m)
            oc = o - _head_mean(o, m0, m1)
            on = oc * lax.rsqrt(_head_mean(oc * oc, m0, m1) + GN_EPS)
            g = g_ref[rows, :]
            y_ref[rows, :] = (g * _sigmoid(g)) * (on * gnw)
            return gch_v * stb + _mm_tn(kr * dkb, v) * bd

        lax.fori_loop(0, nc, pass_b, jnp.zeros((LANE, LANE), F32))

    return pl.pallas_call(
        body, name="ret_fwd", grid=(RET_WIDTH // LANE,),
        in_specs=_ret_specs(T),
        out_specs=_strip(T, lambda j: j),
        out_shape=jax.ShapeDtypeStruct((T, RET_WIDTH), F32),
        scratch_shapes=[pltpu.VMEM((nc, LANE, LANE), F32)],
        compiler_params=_cparams(("arbitrary",)),
    )(proj, proj, proj, proj, cos, sin, dec, rtab, gch, gnw8)


def _ret_bwd_call(proj, dycat, tables, gnw8):
    T = proj.shape[0]
    nc = T // RC
    cos, sin, dec, rtab, gch = tables

    def body(q_ref, k_ref, v_ref, g_ref, cos_ref, sin_ref, dec_ref, rtab_ref, gch_ref, gnw_ref, dy_ref,
             dp_ref, dgnw_ref, stf_ref, dstb_ref, dkr_ref, dv_ref):
        lane, m0, m1, bd = _ret_masks()
        gch_v = gch_ref[0][0:1, :]
        gnw = gnw_ref[0:1, :]
        dkf, dkb, dqf, dqb = rtab_ref[0, 0], rtab_ref[0, 1], rtab_ref[0, 2], rtab_ref[0, 3]
        scale = HEAD_DIM ** -0.5
        zst = jnp.zeros((LANE, LANE), F32)

        def rope(x, rows):
            return x * cos_ref[rows, :] + _swap32(x, lane) * sin_ref[rows, :]

        def rope_t(d, rows):
            return d * cos_ref[rows, :] + _swap32(d * sin_ref[rows, :], lane)

        def pass_a(n, st):
            rows = _rows(n * RC, RC)
            stf_ref[n] = st
            kr = rope(k_ref[rows, :], rows) * scale
            return gch_v * st + _mm_tn(kr * dkf, v_ref[rows, :]) * bd

        lax.fori_loop(0, nc, pass_a, zst)

        def pass_b(i, carry):
            stb, d_f, dgnw = carry
            n = nc - 1 - i
            rows = _rows(n * RC, RC)
            qr = rope(q_ref[rows, :], rows)
            kr = rope(k_ref[rows, :], rows) * scale
            v = v_ref[rows, :]
            stf = stf_ref[n]
            qf = qr * dqf
            qb = qr * dqb
            o = _mm(qf, stf) + _mm(qb, stb)
            s_h = []
            for h, m in ((0, m0), (1, m1)):
                s = _mm_nt(qr * m, kr) * dec_ref[h]
                s_h.append(s)
                o = o + _mm(s, v * m)
            oc = o - _head_mean(o, m0, m1)
            rstd = lax.rsqrt(_head_mean(oc * oc, m0, m1) + GN_EPS)
            on = oc * rstd
            g = g_ref[rows, :]
            sg = _sigmoid(g)
            dy = dy_ref[rows, :]
            dp_ref[3, rows, :] = (dy * (on * gnw) * (sg * (1.0 + g * (1.0 - sg)))).astype(dp_ref.dtype)
            t = dy * (g * sg)
            dgnw = dgnw + jnp.sum(t * on, axis=0, keepdims=True)
            don = t * gnw
            do = rstd * (don - _head_mean(don, m0, m1) - on * _head_mean(don * on, m0, m1))
            dqr = _mm_nt(do, stf) * dqf + _mm_nt(do, stb) * dqb
            dkr = _mm_nt(v, d_f) * dkf
            dv = _mm(kr * dkf, d_f)
            for h, m in ((0, m0), (1, m1)):
                ds = _mm_nt(do * m, v) * dec_ref[h]
                dqr = dqr + _mm(ds, kr * m)
                dkr = dkr + _mm_tn(ds, qr * m)
                dv = dv + _mm_tn(s_h[h], do * m)
            dp_ref[0, rows, :] = rope_t(dqr, rows).astype(dp_ref.dtype)
            dkr_ref[rows, :] = dkr
            dv_ref[rows, :] = dv
            dstb_ref[n] = _mm_tn(qb, do) * bd
            d_f = _mm_tn(qf, do) * bd + gch_v * d_f
            stb = gch_v * stb + _mm_tn(kr * dkb, v) * bd
            return stb, d_f, dgnw

        _, _, dgnw = lax.fori_loop(0, nc, pass_b, (zst, zst, jnp.zeros((1, LANE), F32)))
        dgnw_ref[...] = jnp.concatenate([dgnw, jnp.zeros((SUB - 1, LANE), F32)], axis=0)

        def pass_c(n, d_b):
            rows = _rows(n * RC, RC)
            kr = rope(k_ref[rows, :], rows) * scale
            v = v_ref[rows, :]
            dkr = dkr_ref[rows, :] + _mm_nt(v, d_b) * dkb
            dp_ref[1, rows, :] = (rope_t(dkr, rows) * scale).astype(dp_ref.dtype)
            dp_ref[2, rows, :] = (dv_ref[rows, :] + _mm(kr * dkb, d_b)).astype(dp_ref.dtype)
            return dstb_ref[n] + gch_v * d_b

        lax.fori_loop(0, nc, pass_c, zst)

    return pl.pallas_call(
        body, name="ret_bwd", grid=(RET_WIDTH // LANE,),
        in_specs=_ret_specs(T) + [_strip(T, lambda j: j + 3)],
        out_specs=[pl.BlockSpec((4, T, LANE), lambda j: (0, 0, j), pipeline_mode=pl.Buffered(1)),
                   pl.BlockSpec((SUB, LANE), lambda j: (0, j))],
        out_shape=[jax.ShapeDtypeStruct((4, T, RET_WIDTH), _BF), jax.ShapeDtypeStruct((SUB, RET_WIDTH), F32)],
        scratch_shapes=[pltpu.VMEM((nc, LANE, LANE), F32), pltpu.VMEM((nc, LANE, LANE), F32),
                        pltpu.VMEM((T, LANE), F32), pltpu.VMEM((T, LANE), F32)],
        compiler_params=_cparams(("arbitrary",)),
    )(proj, proj, proj, proj, cos, sin, dec, rtab, gch, gnw8, dycat)


NA_Q = 2 * GRID_W
NA_WROWS = 10
NA_K = NA_WROWS * GRID_W
NA_TYPES = 5


def _na_onehots(rows_n):
    reps = [(0, 0), (2, 0), (4, 0), (rows_n - 4, rows_n - NA_WROWS), (rows_n - 2, rows_n - NA_WROWS)]
    rm = np.zeros((NA_TYPES, 2, NA_WROWS, 2 * NA_KH - 1), np.float32)
    for t, (r, ws) in enumerate(reps):
        for qh in range(2):
            qrow = r + qh
            rstart = min(max(qrow - NA_KH // 2, 0), rows_n - NA_KH)
            for kh in range(NA_WROWS):
                krow = ws + kh
                if rstart <= krow < rstart + NA_KH:
                    rm[t, qh, kh, krow - qrow + NA_KH - 1] = 1.0
    cm = np.zeros((GRID_W, GRID_W, 2 * NA_KW - 1), np.float32)
    for qc in range(GRID_W):
        cstart = min(max(qc - NA_KW // 2, 0), GRID_W - NA_KW)
        for kc in range(cstart, cstart + NA_KW):
            cm[qc, kc, kc - qc + NA_KW - 1] = 1.0
    return rm, cm


def _na_bias_tables(rpb, rows_n):
    rm, cm = _na_onehots(rows_n)
    val = jnp.einsum("hab,tqka,xyb->htqxky", rpb, rm, cm, precision=lax.Precision.HIGHEST)
    valid = np.einsum("tqk,xy->tqxky", rm.sum(-1), cm.sum(-1)) > 0.5
    tab = jnp.where(valid[None], val, NEG).reshape(2, 2, NA_TYPES, NA_Q, NA_K)
    return tab, jnp.swapaxes(tab, -1, -2)


def _na_bias_grad(dtab, rows_n):
    rm, cm = _na_onehots(rows_n)
    d6 = dtab.reshape(NA_HEADS, NA_TYPES, 2, GRID_W, NA_WROWS, GRID_W)
    return jnp.einsum("htqxky,tqka,xyb->hab", d6, rm, cm, precision=lax.Precision.HIGHEST)


def _na_step(p, npairs, rows_n):
    ws = jnp.clip(2 * p - NA_KH // 2, 0, rows_n - NA_WROWS)
    koff = pl.multiple_of(ws * GRID_W, LANE)
    typ = jnp.where(p == 0, 0, jnp.where(p == 1, 1, jnp.where(p == npairs - 2, 3, jnp.where(p == npairs - 1, 4, 2))))
    return _rows(p * NA_Q, NA_Q), pl.ds(koff, NA_K), typ


def _na_fwd_call(proj, btab):
    T = proj.shape[0]
    npairs, rows_n = T // NA_Q, T // GRID_W

    def body(q_ref, k_ref, v_ref, b_ref, o_ref):
        lane = lax.broadcasted_iota(jnp.int32, (NA_Q, LANE), 1)
        m0 = (lane < HEAD_DIM).astype(F32)
        m1 = 1.0 - m0

        def step(p, carry):
            qrows, krows, typ = _na_step(p, npairs, rows_n)
            q = q_ref[qrows, :]
            kw = k_ref[krows, :]
            vw = v_ref[krows, :]
            o = jnp.zeros((NA_Q, LANE), F32)
            for h, m in ((0, m0), (1, m1)):
                s = _mm_nt(q * m, kw) * (HEAD_DIM ** -0.5) + b_ref[0, h, typ]
                e = jnp.exp(s - jnp.max(s, axis=-1, keepdims=True))
                pr = e / jnp.sum(e, axis=-1, keepdims=True)
                o = o + _mm(pr, vw) * m
            o_ref[qrows, :] = o
            return carry

        lax.fori_loop(0, npairs, step, 0)

    return pl.pallas_call(
        body, name="na_fwd", grid=(NA_WIDTH // LANE,),
        in_specs=[_strip(T, lambda j: j + 18), _strip(T, lambda j: j + 20), _strip(T, lambda j: j + 22),
                  pl.BlockSpec((1, 2, NA_TYPES, NA_Q, NA_K), lambda j: (j, 0, 0, 0, 0))],
        out_specs=_strip(T, lambda j: j),
        out_shape=jax.ShapeDtypeStruct((T, NA_WIDTH), F32),
        compiler_params=_cparams(("arbitrary",)),
    )(proj, proj, proj, btab)


def _na_bwd_call(proj, dycat, btab, btab_t):
    T = proj.shape[0]
    npairs, rows_n = T // NA_Q, T // GRID_W
    scale = HEAD_DIM ** -0.5

    def body(q_ref, k_ref, v_ref, do_ref, b_ref, bt_ref, dq_ref, dk_ref, dv_ref, db_ref, dka_ref, dva_ref):
        lane = lax.broadcasted_iota(jnp.int32, (NA_Q, LANE), 1)
        m0 = (lane < HEAD_DIM).astype(F32)
        m1 = 1.0 - m0
        dka_ref[...] = jnp.zeros_like(dka_ref)
        dva_ref[...] = jnp.zeros_like(dva_ref)
        db_ref[...] = jnp.zeros_like(db_ref)

        def step(p, carry):
            qrows, krows, typ = _na_step(p, npairs, rows_n)
            q = q_ref[qrows, :]
            do = do_ref[qrows, :]
            kw = k_ref[krows, :]
            vw = v_ref[krows, :]
            dq = jnp.zeros((NA_Q, LANE), F32)
            dk = jnp.zeros((NA_K, LANE), F32)
            dv = jnp.zeros((NA_K, LANE), F32)
            for h, m in ((0, m0), (1, m1)):
                qm = q * m
                dom = do * m
                s = _mm_nt(qm, kw) * scale + b_ref[0, h, typ]
                e = jnp.exp(s - jnp.max(s, axis=-1, keepdims=True))
                pr = e / jnp.sum(e, axis=-1, keepdims=True)
                dpr = _mm_nt(dom, vw)
                ds = pr * (dpr - jnp.sum(pr * dpr, axis=-1, keepdims=True))
                db_ref[0, h, typ] += ds
                dq = dq + _mm(ds * scale, kw) * m
                st = _mm_nt(kw, qm) * scale + bt_ref[0, h, typ]
                et = jnp.exp(st - jnp.max(st, axis=0, keepdims=True))
                prt = et / jnp.sum(et, axis=0, keepdims=True)
                dprt = _mm_nt(vw, dom)
                dst = prt * (dprt - jnp.sum(prt * dprt, axis=0, keepdims=True))
                dk = dk + _mm(dst * scale, qm)
                dv = dv + _mm(prt, dom)
            dq_ref[qrows, :] = dq.astype(dq_ref.dtype)
            dka_ref[krows, :] += dk
            dva_ref[krows, :] += dv
            return carry

        lax.fori_loop(0, npairs, step, 0)
        dk_ref[...] = dka_ref[...].astype(dk_ref.dtype)
        dv_ref[...] = dva_ref[...].astype(dv_ref.dtype)

    tab = pl.BlockSpec((1, 2, NA_TYPES, NA_Q, NA_K), lambda j: (j, 0, 0, 0, 0))
    tab_t = pl.BlockSpec((1, 2, NA_TYPES, NA_K, NA_Q), lambda j: (j, 0, 0, 0, 0))
    out = lambda col: pl.BlockSpec((T, LANE), lambda j: (0, col(j)), pipeline_mode=pl.Buffered(1))
    dq, dk, dv, db = pl.pallas_call(
        body, name="na_bwd", grid=(NA_WIDTH // LANE,),
        in_specs=[_strip(T, lambda j: j + 18), _strip(T, lambda j: j + 20), _strip(T, lambda j: j + 22),
                  _strip(T, lambda j: j + 6), tab, tab_t],
        out_specs=[out(lambda j: j), out(lambda j: j), out(lambda j: j), tab],
        out_shape=[jax.ShapeDtypeStruct((T, NA_WIDTH), _BF)] * 3
        + [jax.ShapeDtypeStruct((2, 2, NA_TYPES, NA_Q, NA_K), F32)],
        scratch_shapes=[pltpu.VMEM((T, LANE), F32), pltpu.VMEM((T, LANE), F32)],
        compiler_params=_cparams(("arbitrary",)),
    )(proj, proj, proj, dycat, btab, btab_t)
    dp = jnp.concatenate([dq, dk, dv], axis=1)
    return jnp.stack([dp[:, :W_BLK], dp[:, W_BLK:]]), db


W_BLK = IN_WIDTH // N_DEV
TM = 512


def _ln_fwd(z, g, b):
    zc = z - jnp.mean(z, axis=-1, keepdims=True)
    var = jnp.mean(zc * zc, axis=-1, keepdims=True)
    return zc * lax.rsqrt(var + LN_EPS) * g + b


def _ln_bwd(dy, z, g):
    zc = z - jnp.mean(z, axis=-1, keepdims=True)
    rstd = lax.rsqrt(jnp.mean(zc * zc, axis=-1, keepdims=True) + LN_EPS)
    xhat = zc * rstd
    dxh = dy * g
    dz = rstd * (dxh - jnp.mean(dxh, axis=-1, keepdims=True) - xhat * jnp.mean(dxh * xhat, axis=-1, keepdims=True))
    return dz, dy * xhat


def _row_tile(T):
    return 1024 if T % 1024 == 0 else TM


def _inproj_call(xb, w):
    T = xb.shape[0]
    tm = _row_tile(T)

    def body(x_ref, w_ref, o_ref):
        o_ref[...] = _mm(x_ref[...], w_ref[...])

    return pl.pallas_call(
        body, name="inproj", grid=(T // tm, N_DEV),
        in_specs=[pl.BlockSpec((tm, D_MODEL), lambda i, n: (i, 0)),
                  pl.BlockSpec((None, D_MODEL, W_BLK), lambda i, n: (n, 0, 0))],
        out_specs=pl.BlockSpec((tm, W_BLK), lambda i, n: (i, n)),
        out_shape=jax.ShapeDtypeStruct((T, IN_WIDTH), F32),
        compiler_params=_cparams(("parallel", "arbitrary")),
    )(xb, w)


def _vec_spec():
    return pl.BlockSpec((1, D_MODEL), lambda *_: (0, 0))


def _outproj_ln_call(y_lru, y_ret, y_na, x, w, g, b):
    T = x.shape[0]

    def body(yl_ref, yr_ref, yn_ref, x_ref, w_ref, g_ref, b_ref, z_ref, x1_ref, x1b_ref, yc_ref):
        yc_ref[:, 0:LRU_WIDTH] = yl_ref[...].astype(yc_ref.dtype)
        yc_ref[:, LRU_WIDTH:LRU_WIDTH + RET_WIDTH] = yr_ref[...].astype(yc_ref.dtype)
        yc_ref[:, LRU_WIDTH + RET_WIDTH:] = yn_ref[...].astype(yc_ref.dtype)
        z = ALPHA * x_ref[...] + _mm(yc_ref[...], w_ref[...].reshape(D_MODEL, D_MODEL))
        z_ref[...] = z
        x1 = _ln_fwd(z, g_ref[...], b_ref[...])
        x1_ref[...] = x1
        x1b_ref[...] = x1.astype(x1b_ref.dtype)

    row = lambda w_: pl.BlockSpec((TM, w_), lambda i: (i, 0))
    return pl.pallas_call(
        body, name="outproj_ln", grid=(T // TM,),
        in_specs=[row(LRU_WIDTH), row(RET_WIDTH), row(NA_WIDTH), row(D_MODEL),
                  pl.BlockSpec((N_DEV, LANE, D_MODEL), lambda i: (0, 0, 0)), _vec_spec(), _vec_spec()],
        out_specs=[row(D_MODEL)] * 4,
        out_shape=[jax.ShapeDtypeStruct((T, D_MODEL), F32), jax.ShapeDtypeStruct((T, D_MODEL), F32),
                   jax.ShapeDtypeStruct((T, D_MODEL), _BF), jax.ShapeDtypeStruct((T, D_MODEL), _BF)],
        compiler_params=_cparams(("parallel",)),
    )(y_lru, y_ret, y_na, x, w, g, b)


def _ffn_ln_call(x1, x1b, wg, wu, wd, g, b):
    T = x1.shape[0]

    def body(x_ref, xb_ref, wg_ref, wu_ref, wd_ref, g_ref, b_ref, z_ref, x2_ref, x2b_ref, acc_ref):
        n = pl.program_id(1)

        @pl.when(n == 0)
        def _():
            acc_ref[...] = jnp.zeros_like(acc_ref)

        xb = xb_ref[...]
        gp = _mm(xb, wg_ref[...])
        hid = gp * _sigmoid(gp) * _mm(xb, wu_ref[...])
        acc_ref[...] += _mm(hid, wd_ref[...])

        @pl.when(n == N_DEV - 1)
        def _():
            z = ALPHA * x_ref[...] + acc_ref[...]
            z_ref[...] = z
            x2 = _ln_fwd(z, g_ref[...], b_ref[...])
            x2_ref[...] = x2
            x2b_ref[...] = x2.astype(x2b_ref.dtype)

    row = pl.BlockSpec((TM, D_MODEL), lambda i, n: (i, 0))
    return pl.pallas_call(
        body, name="ffn_ln", grid=(T // TM, N_DEV),
        in_specs=[row, row,
                  pl.BlockSpec((None, D_MODEL, W_BLK), lambda i, n: (n, 0, 0)),
                  pl.BlockSpec((None, D_MODEL, W_BLK), lambda i, n: (n, 0, 0)),
                  pl.BlockSpec((None, W_BLK, D_MODEL), lambda i, n: (n, 0, 0)), _vec_spec(), _vec_spec()],
        out_specs=[row] * 3,
        out_shape=[jax.ShapeDtypeStruct((T, D_MODEL), F32), jax.ShapeDtypeStruct((T, D_MODEL), F32),
                   jax.ShapeDtypeStruct((T, D_MODEL), _BF)],
        scratch_shapes=[pltpu.VMEM((TM, D_MODEL), F32)],
        compiler_params=_cparams(("parallel", "arbitrary")),
    )(x1, x1b, wg, wu, wd, g, b)


def _loss_call(y, t):
    T = y.shape[0]

    def body(y_ref, t_ref, dy_ref, loss_ref):
        @pl.when(pl.program_id(0) == 0)
        def _():
            loss_ref[...] = jnp.zeros_like(loss_ref)

        err = y_ref[...] - t_ref[...]
        dy_ref[...] = err * (1.0 / D_MODEL)
        part = 0.5 * jnp.sum(jnp.mean(err * err, axis=-1, keepdims=True), axis=0, keepdims=True)
        loss_ref[...] += jnp.broadcast_to(part, loss_ref.shape)

    row = pl.BlockSpec((TM, D_MODEL), lambda i: (i, 0))
    return pl.pallas_call(
        body, name="loss", grid=(T // TM,),
        in_specs=[row, row],
        out_specs=[row, pl.BlockSpec((SUB, LANE), lambda i: (0, 0))],
        out_shape=[jax.ShapeDtypeStruct((T, D_MODEL), F32), jax.ShapeDtypeStruct((SUB, LANE), F32)],
        compiler_params=_cparams(("arbitrary",)),
    )(y, t)


def _ffn_bwd_call(dx2, z2, x1, x1b, wg, wu, wd, g):
    T = x1.shape[0]

    def body(dx2_ref, z_ref, x_ref, xb_ref, wg_ref, wu_ref, wd_ref, g_ref,
             dx1_ref, dgp_ref, dup_ref, hid_ref, dzb_ref, dln_ref, acc_ref):
        i, n = pl.program_id(0), pl.program_id(1)

        @pl.when((i == 0) & (n == 0))
        def _():
            dln_ref[...] = jnp.zeros_like(dln_ref)

        @pl.when(n == 0)
        def _():
            dy = dx2_ref[...]
            dz, dg_rows = _ln_bwd(dy, z_ref[...], g_ref[...])
            dzb_ref[...] = dz.astype(dzb_ref.dtype)
            acc_ref[...] = ALPHA * dz
            dln_ref[0:1, :] += jnp.sum(dg_rows, axis=0, keepdims=True)
            dln_ref[1:2, :] += jnp.sum(dy, axis=0, keepdims=True)

        xb = xb_ref[...]
        gp = _mm(xb, wg_ref[...])
        up = _mm(xb, wu_ref[...])
        sg = _sigmoid(gp)
        act = gp * sg
        hid_ref[...] = (act * up).astype(hid_ref.dtype)
        dhid = _mm_nt(dzb_ref[...], wd_ref[...])
        dup = dhid * act
        dgp = dhid * up * (sg * (1.0 + gp * (1.0 - sg)))
        dgp_ref[...] = dgp.astype(dgp_ref.dtype)
        dup_ref[...] = dup.astype(dup_ref.dtype)
        acc_ref[...] += _mm_nt(dgp, wg_ref[...]) + _mm_nt(dup, wu_ref[...])

        @pl.when(n == N_DEV - 1)
        def _():
            dx1_ref[...] = acc_ref[...]

    row = pl.BlockSpec((TM, D_MODEL), lambda i, n: (i, 0))
    blk = pl.BlockSpec((None, TM, W_BLK), lambda i, n: (n, i, 0))
    return pl.pallas_call(
        body, name="ffn_bwd", grid=(T // TM, N_DEV),
        in_specs=[row, row, row, row,
                  pl.BlockSpec((None, D_MODEL, W_BLK), lambda i, n: (n, 0, 0)),
                  pl.BlockSpec((None, D_MODEL, W_BLK), lambda i, n: (n, 0, 0)),
                  pl.BlockSpec((None, W_BLK, D_MODEL), lambda i, n: (n, 0, 0)), _vec_spec()],
        out_specs=[row, blk, blk, pl.BlockSpec((TM, W_BLK), lambda i, n: (i, n)), row,
                   pl.BlockSpec((SUB, D_MODEL), lambda i, n: (0, 0))],
        out_shape=[jax.ShapeDtypeStruct((T, D_MODEL), F32),
                   jax.ShapeDtypeStruct((N_DEV, T, W_BLK), _BF), jax.ShapeDtypeStruct((N_DEV, T, W_BLK), _BF),
                   jax.ShapeDtypeStruct((T, N_DEV * W_BLK), _BF), jax.ShapeDtypeStruct((T, D_MODEL), _BF),
                   jax.ShapeDtypeStruct((SUB, D_MODEL), F32)],
        scratch_shapes=[pltpu.VMEM((TM, D_MODEL), F32)],
        compiler_params=_cparams(("arbitrary", "arbitrary")),
    )(dx2, z2, x1, x1b, wg, wu, wd, g)


def _outproj_bwd_call(dx1, z1, w, g):
    T = dx1.shape[0]

    def body(dx_ref, z_ref, w_ref, g_ref, dzb_ref, dyc_ref, dres_ref, dln_ref):
        @pl.when(pl.program_id(0) == 0)
        def _():
            dln_ref[...] = jnp.zeros_like(dln_ref)

        dy = dx_ref[...]
        dz, dg_rows = _ln_bwd(dy, z_ref[...], g_ref[...])
        dzb_ref[...] = dz.astype(dzb_ref.dtype)
        dres_ref[...] = ALPHA * dz
        dyc_ref[...] = _mm_nt(dz, w_ref[...].reshape(D_MODEL, D_MODEL))
        dln_ref[0:1, :] += jnp.sum(dg_rows, axis=0, keepdims=True)
        dln_ref[1:2, :] += jnp.sum(dy, axis=0, keepdims=True)

    row = pl.BlockSpec((TM, D_MODEL), lambda i: (i, 0))
    return pl.pallas_call(
        body, name="outproj_bwd", grid=(T // TM,),
        in_specs=[row, row, pl.BlockSpec((N_DEV, LANE, D_MODEL), lambda i: (0, 0, 0)), _vec_spec()],
        out_specs=[row, row, row, pl.BlockSpec((SUB, D_MODEL), lambda i: (0, 0))],
        out_shape=[jax.ShapeDtypeStruct((T, D_MODEL), _BF), jax.ShapeDtypeStruct((T, D_MODEL), F32),
                   jax.ShapeDtypeStruct((T, D_MODEL), F32), jax.ShapeDtypeStruct((SUB, D_MODEL), F32)],
        compiler_params=_cparams(("arbitrary",)),
    )(dx1, z1, w, g)


def _inproj_bwd_call(dres, dp, w):
    T = dres.shape[0]

    def body(dres_ref, dp_ref, w_ref, dx_ref):
        acc = dres_ref[...]
        for n in range(N_DEV):
            acc = acc + _mm_nt(dp_ref[n], w_ref[n])
        dx_ref[...] = acc

    row = pl.BlockSpec((TM, D_MODEL), lambda i: (i, 0))
    return pl.pallas_call(
        body, name="inproj_bwd", grid=(T // TM,),
        in_specs=[row, pl.BlockSpec((N_DEV, TM, W_BLK), lambda i: (0, i, 0)),
                  pl.BlockSpec((N_DEV, D_MODEL, W_BLK), lambda i: (0, 0, 0))],
        out_specs=row,
        out_shape=jax.ShapeDtypeStruct((T, D_MODEL), F32),
        compiler_params=_cparams(("parallel",)),
    )(dres, dp, w)


def _tn_cols_call(a, b3, name):
    T, ka = a.shape
    nblk, _, nb = b3.shape

    def body(a_ref, b_ref, o_ref):
        o_ref[...] = _mm_tn(a_ref[...], b_ref[...]).astype(o_ref.dtype)

    return pl.pallas_call(
        body, name=name, grid=(nblk,),
        in_specs=[pl.BlockSpec((T, ka), lambda n: (0, 0), pipeline_mode=pl.Buffered(1)),
                  pl.BlockSpec((None, T, nb), lambda n: (n, 0, 0))],
        out_specs=pl.BlockSpec((None, ka, nb), lambda n: (n, 0, 0)),
        out_shape=jax.ShapeDtypeStruct((nblk, ka, nb), _BF),
        compiler_params=_cparams(("parallel",)),
    )(a, b3)


def _tn_rows_call(a, b, kb, name):
    T, ka = a.shape
    n = b.shape[1]

    def body(a_ref, b_ref, o_ref):
        o_ref[...] = _mm_tn(a_ref[...], b_ref[...]).astype(o_ref.dtype)

    return pl.pallas_call(
        body, name=name, grid=(ka // kb,),
        in_specs=[pl.BlockSpec((T, kb), lambda r: (0, r)),
                  pl.BlockSpec((T, n), lambda r: (0, 0), pipeline_mode=pl.Buffered(1))],
        out_specs=pl.BlockSpec((None, kb, n), lambda r: (r, 0, 0)),
        out_shape=jax.ShapeDtypeStruct((ka // kb, kb, n), _BF),
        compiler_params=_cparams(("parallel",)),
    )(a, b)


def _me():
    return lax.axis_index("x"), lax.axis_index("y"), lax.axis_index("c")


def _flip(k):
    x, y, c = _me()
    return (1 - x if k & 4 else x, 1 - y if k & 2 else y, 1 - c if k & 1 else c)


def _dev_index(pos):
    return 4 * pos[0] + 2 * pos[1] + pos[2]


def _exchange(ins, outs, src_of, send_sems, recv_sems, local_sems):
    me = _dev_index(_me())
    n = len(ins)
    local = [pltpu.make_async_copy(src_of(ins[a], a, me), outs[a].at[me], local_sems.at[a]) for a in range(n)]
    for cp in local:
        cp.start()
    remote = []
    for k in range(1, N_DEV):
        peer = _flip(k)
        for a in range(n):
            cp = pltpu.make_async_remote_copy(
                src_ref=src_of(ins[a], a, _dev_index(peer)), dst_ref=outs[a].at[me],
                send_sem=send_sems.at[k - 1, a], recv_sem=recv_sems.at[k - 1, a],
                device_id=peer, device_id_type=MESH)
            cp.start()
            remote.append(cp)
    for cp in remote:
        cp.wait()
    for cp in local:
        cp.wait()


def _comm_call(arrs, gather_flags, name):
    n = len(arrs)

    def body(*refs):
        ins, outs = refs[:n], refs[n:2 * n]
        send_sems, recv_sems, local_sems = refs[2 * n:]

        def src_of(ref, a, idx):
            return ref if gather_flags[a] else ref.at[idx]

        _exchange(ins, outs, src_of, send_sems, recv_sems, local_sems)

    out_shape = [jax.ShapeDtypeStruct((N_DEV,) + (v.shape if gf else v.shape[1:]), v.dtype)
                 for v, gf in zip(arrs, gather_flags)]
    return pl.pallas_call(
        body, name=name,
        in_specs=[pl.BlockSpec(memory_space=pl.ANY)] * n,
        out_specs=[pl.BlockSpec(memory_space=pl.ANY)] * n,
        out_shape=out_shape,
        scratch_shapes=[pltpu.SemaphoreType.DMA((N_DEV - 1, n)), pltpu.SemaphoreType.DMA((N_DEV - 1, n)),
                        pltpu.SemaphoreType.DMA((n,))],
    )(*arrs)


def _sum8_call(recv, rows, name):
    _, r, c = recv.shape

    def body(x_ref, o_ref):
        acc = x_ref[0].astype(F32)
        for s in range(1, N_DEV):
            acc = acc + x_ref[s].astype(F32)
        o_ref[...] = acc

    return pl.pallas_call(
        body, name=name, grid=(r // rows,),
        in_specs=[pl.BlockSpec((N_DEV, rows, c), lambda i: (0, i, 0))],
        out_specs=pl.BlockSpec((rows, c), lambda i: (i, 0)),
        out_shape=jax.ShapeDtypeStruct((r, c), F32),
        compiler_params=_cparams(("parallel",)),
    )(recv)


def _adamw_call(w, g, m, v, rows, name):
    r, c = w.shape

    def body(w_ref, g_ref, m_ref, v_ref, d_ref, nm_ref, nv_ref):
        gr = g_ref[...]
        nm = ADAM_B1 * m_ref[...] + (1.0 - ADAM_B1) * gr
        nv = ADAM_B2 * v_ref[...] + (1.0 - ADAM_B2) * (gr * gr)
        m_hat = nm / (1.0 - ADAM_B1 ** ADAM_STEP)
        v_hat = nv / (1.0 - ADAM_B2 ** ADAM_STEP)
        d_ref[...] = -ADAM_LR * (m_hat / (jnp.sqrt(v_hat) + ADAM_EPS) + ADAM_WD * w_ref[...])
        nm_ref[...] = nm
        nv_ref[...] = nv

    spec = pl.BlockSpec((rows, c), lambda i: (i, 0))
    return pl.pallas_call(
        body, name=name, grid=(r // rows,),
        in_specs=[spec] * 4, out_specs=[spec] * 3,
        out_shape=[jax.ShapeDtypeStruct((r, c), F32)] * 3,
        compiler_params=_cparams(("parallel",)),
    )(w, g, m, v)


SH_ROWS = 16
SH_W = LRU_WIDTH // N_DEV
REP_ROWS = 824
_REP_SIZES = (LRU_WIDTH, 2 * 6 * 64 * 64, 2 * 6 * 64 * 64, RET_WIDTH, 1920, D_MODEL, D_MODEL, D_MODEL, D_MODEL)
_RPB_SIZE = NA_HEADS * (2 * NA_KH - 1) * (2 * NA_KW - 1)


def _pack_sh(cw, ba, bx, lam):
    return jnp.concatenate([cw, ba, bx, lam], axis=0)


def _pad_sh(p):
    pad = [(0, 0)] * (p.ndim - 2) + [(0, SH_ROWS - p.shape[-2]), (0, LANE - p.shape[-1])]
    return jnp.pad(p, pad)


def _pack_rep(cb, wa, wx, gnw, rpb, l1g, l1b, l2g, l2b):
    flat = jnp.concatenate([cb.reshape(-1), wa.reshape(-1), wx.reshape(-1), gnw.reshape(-1),
                            jnp.pad(rpb.reshape(-1), (0, 1920 - _RPB_SIZE)), l1g, l1b, l2g, l2b,
                            jnp.zeros((REP_ROWS * LANE - sum(_REP_SIZES),), F32)])
    return flat.reshape(REP_ROWS, LANE)


def _unpack_rep(p):
    nl = p.shape[0]
    flat = p.reshape(nl, -1)
    out, off = [], 0
    for size in _REP_SIZES:
        out.append(flat[:, off:off + size])
        off += size
    cb, wa, wx, gnw, rpb, l1g, l1b, l2g, l2b = out
    return (cb, wa.reshape(nl, 2, 6, 64, 64), wx.reshape(nl, 2, 6, 64, 64), gnw,
            rpb[:, :_RPB_SIZE].reshape(nl, NA_HEADS, 2 * NA_KH - 1, 2 * NA_KW - 1), l1g, l1b, l2g, l2b)


def _adamw_nd(w, g, m, v, rows, name):
    shp = w.shape
    f = lambda t: t.reshape(-1, shp[-1])
    return [t.reshape(shp) for t in _adamw_call(f(w), f(g), f(m), f(v), rows, name)]


def kernel(x, w_in, conv_w, conv_b, lru_w_a, lru_b_a, lru_w_x, lru_b_x, lru_lam, ret_gn_w, na_rpb, w_out, ln1_g, ln1_b, w_gate, w_up, w_down, ln2_g, ln2_b, loss_target, m_w_in, m_conv_w, m_conv_b, m_lru_w_a, m_lru_b_a, m_lru_w_x, m_lru_b_x, m_lru_lam, m_ret_gn_w, m_na_rpb, m_w_out, m_ln1_g, m_ln1_b, m_w_gate, m_w_up, m_w_down, m_ln2_g, m_ln2_b, v_w_in, v_conv_w, v_conv_b, v_lru_w_a, v_lru_b_a, v_lru_w_x, v_lru_b_x, v_lru_lam, v_ret_gn_w, v_na_rpb, v_w_out, v_ln1_g, v_ln1_b, v_w_gate, v_w_up, v_w_down, v_ln2_g, v_ln2_b):
    nl = w_in.shape[0]
    T = x.shape[1]
    rows_n = T // GRID_W
    x0, target = x[0], loss_target[0]
    ffpad = W_BLK - FF_BLK

    win_b = w_in.astype(_BF)
    wg_b = jnp.pad(w_gate, ((0, 0), (0, 0), (0, ffpad))).astype(_BF)
    wu_b = jnp.pad(w_up, ((0, 0), (0, 0), (0, ffpad))).astype(_BF)
    wd_b = jnp.pad(w_down, ((0, 0), (0, ffpad), (0, 0))).astype(_BF)
    wout_b = w_out.astype(_BF)
    gathered = []
    for l in range(nl):
        sh = _pad_sh(_pack_sh(conv_w[l], lru_b_a[l], lru_b_x[l], lru_lam[l]))
        gathered.append(_comm_call([win_b[l], wg_b[l], wu_b[l], wd_b[l], wout_b[l], sh], [True] * 6, "allgather"))

    tables = _ret_tables(T)
    layers = []
    xs, xb = x0, x0.astype(_BF)
    for l in range(nl):
        win, wg, wu, wd, wout, shg = gathered[l]
        full = shg[:, :10, :SH_W].transpose(1, 0, 2).reshape(10, LRU_WIDTH)
        vec, w4 = _lru_pack(full[0:4], conv_b[l], lru_w_a[l], full[4:6], lru_w_x[l], full[6:8], full[8:10])
        gnw8 = jnp.pad(ret_gn_w[l][None], ((0, SUB - 1), (0, 0)))
        btab, btab_t = _na_bias_tables(na_rpb[l], rows_n)
        proj = _inproj_call(xb, win)
        y_lru = _lru_fwd_call(proj, vec, w4)
        y_ret = _ret_fwd_call(proj, tables, gnw8)
        y_na = _na_fwd_call(proj, btab)
        z1, x1, x1b, ycb = _outproj_ln_call(y_lru, y_ret, y_na, xs, wout, ln1_g[l][None], ln1_b[l][None])
        z2, x2, x2b = _ffn_ln_call(x1, x1b, wg, wu, wd, ln2_g[l][None], ln2_b[l][None])
        layers.append(dict(xb=xb, proj=proj, vec=vec, w4=w4, gnw8=gnw8, btab=btab, btab_t=btab_t,
                           z1=z1, x1=x1, x1b=x1b, ycb=ycb, z2=z2))
        xs, xb = x2, x2b

    dx, loss_blk = _loss_call(xs, target)
    loss = lax.psum(loss_blk[0, 0], ("x", "y", "c"))

    g_big = [[None] * nl for _ in range(5)]
    g_sh = [None] * nl
    g_rep = [None] * nl
    for l in reversed(range(nl)):
        s = layers[l]
        win, wg, wu, wd, wout, _ = gathered[l]
        dx1, dgp, dup, hid, dz2b, dln2 = _ffn_bwd_call(dx, s["z2"], s["x1"], s["x1b"], wg, wu, wd, ln2_g[l][None])
        dwg = _tn_cols_call(s["x1b"], dgp, "tn_cols")
        dwu = _tn_cols_call(s["x1b"], dup, "tn_cols")
        dwd = _tn_rows_call(hid, dz2b, W_BLK, "tn_rows_down")
        dz1b, dyc, dres, dln1 = _outproj_bwd_call(dx1, s["z1"], wout, ln1_g[l][None])
        dwout = _tn_rows_call(s["ycb"], dz1b, LANE, "tn_rows_out")
        dp_lru, dvec, dw4 = _lru_bwd_call(s["proj"], dyc, s["vec"], s["w4"])
        dp_ret, dgnw = _ret_bwd_call(s["proj"], dyc, tables, s["gnw8"])
        dp_na, dbias = _na_bwd_call(s["proj"], dyc, s["btab"], s["btab_t"])
        dp = jnp.concatenate([dp_lru, dp_ret, dp_na], axis=0)
        dwin = _tn_cols_call(s["xb"], dp, "tn_cols")
        dx = _inproj_bwd_call(dres, dp, win)
        dcw, dcb, dwa, dba, dwx, dbx, dlam = _lru_unpack(dvec, dw4)
        rep = _pack_rep(dcb, dwa, dwx, dgnw[0], _na_bias_grad(dbias, rows_n), dln1[0], dln1[1], dln2[0], dln2[1])
        sh = _pack_sh(dcw, dba, dbx, dlam).reshape(10, N_DEV, SH_W).transpose(1, 0, 2)
        recv = _comm_call([dwin, dwg, dwu, dwd, dwout, _pad_sh(sh), rep], [False] * 6 + [True], "exchange")
        g_big[0][l] = _sum8_call(recv[0], TM, "sum8_cols")
        g_big[1][l] = _sum8_call(recv[1], TM, "sum8_cols")[:, :FF_BLK]
        g_big[2][l] = _sum8_call(recv[2], TM, "sum8_cols")[:, :FF_BLK]
        g_big[3][l] = _sum8_call(recv[3], W_BLK, "sum8_down")[:FF_BLK]
        g_big[4][l] = _sum8_call(recv[4], LANE, "sum8_out")
        g_sh[l] = _sum8_call(recv[5], SH_ROWS, "sum8_sh")
        g_rep[l] = _sum8_call(recv[6], REP_ROWS, "sum8_rep")

    g_w_in, g_w_gate, g_w_up, g_w_down, g_w_out = [jnp.stack(t) for t in g_big]
    big = {
        "w_in": _adamw_nd(w_in, g_w_in, m_w_in, v_w_in, TM, "adamw_in"),
        "w_gate": _adamw_nd(w_gate, g_w_gate, m_w_gate, v_w_gate, TM, "adamw_ff"),
        "w_up": _adamw_nd(w_up, g_w_up, m_w_up, v_w_up, TM, "adamw_ff"),
        "w_down": _adamw_nd(w_down, g_w_down, m_w_down, v_w_down, FF_BLK, "adamw_down"),
        "w_out": _adamw_nd(w_out, g_w_out, m_w_out, v_w_out, LANE, "adamw_out"),
    }
    g_shp = jnp.stack(g_sh)
    pack_sh = lambda cw, ba, bx, lam: _pad_sh(jnp.concatenate([cw, ba, bx, lam], axis=1))
    sh_out = _adamw_nd(pack_sh(conv_w, lru_b_a, lru_b_x, lru_lam), g_shp,
                       pack_sh(m_conv_w, m_lru_b_a, m_lru_b_x, m_lru_lam),
                       pack_sh(v_conv_w, v_lru_b_a, v_lru_b_x, v_lru_lam), SH_ROWS, "adamw_sh")

    def split_sh(p):
        p = p[:, :, :SH_W]
        return {"conv_w": p[:, 0:4], "lru_b_a": p[:, 4:6], "lru_b_x": p[:, 6:8], "lru_lam": p[:, 8:10]}

    g_repp = jnp.stack(g_rep)
    pack_rep = lambda *ps: jnp.stack([_pack_rep(*[p[l] for p in ps]) for l in range(nl)])
    rep_names = ("conv_b", "lru_w_a", "lru_w_x", "ret_gn_w", "na_rpb", "ln1_g", "ln1_b", "ln2_g", "ln2_b")
    rep_out = _adamw_nd(pack_rep(conv_b, lru_w_a, lru_w_x, ret_gn_w, na_rpb, ln1_g, ln1_b, ln2_g, ln2_b), g_repp,
                        pack_rep(m_conv_b, m_lru_w_a, m_lru_w_x, m_ret_gn_w, m_na_rpb, m_ln1_g, m_ln1_b, m_ln2_g, m_ln2_b),
                        pack_rep(v_conv_b, v_lru_w_a, v_lru_w_x, v_ret_gn_w, v_na_rpb, v_ln1_g, v_ln1_b, v_ln2_g, v_ln2_b),
                        REP_ROWS, "adamw_rep")

    grads = {"w_in": g_w_in, "w_gate": g_w_gate, "w_up": g_w_up, "w_down": g_w_down, "w_out": g_w_out}
    grads.update(split_sh(g_shp))
    grads.update(dict(zip(rep_names, _unpack_rep(g_repp))))
    kinds = []
    for k in range(3):
        d = {n: big[n][k] for n in big}
        d.update(split_sh(sh_out[k]))
        d.update(dict(zip(rep_names, _unpack_rep(rep_out[k]))))
        kinds.append(d)
    order = ("w_in", "conv_w", "conv_b", "lru_w_a", "lru_b_a", "lru_w_x", "lru_b_x", "lru_lam", "ret_gn_w", "na_rpb",
             "w_out", "ln1_g", "ln1_b", "w_gate", "w_up", "w_down", "ln2_g", "ln2_b")
    outs = [loss, dx[None]]
    for d in (grads, *kinds):
        outs.extend(d[n] for n in order)
    return tuple(outs)
```

```python
import functools
import math

import numpy as np
import jax
import jax.numpy as jnp
from jax import lax
from jax.experimental import pallas as pl
from jax.experimental.pallas import tpu as pltpu

F32 = jnp.float32
_BF = jnp.bfloat16

D_MODEL = 1024
DEPTH = 4
GRID_W = 64
HEAD_DIM = 64
LRU_WIDTH = 384
RET_WIDTH = 384
RET_HEADS = 6
NA_WIDTH = 256
NA_HEADS = 4
IN_WIDTH = 3072
CONV_WIDTH = 4
LRU_C = 8.0
RET_CHUNK = 128
ROPE_BASE = 10000.0
GN_EPS = 1e-6
NA_KH = 8
NA_KW = 16
D_FF = 2816
FF_BLK = 352
N_DEV = 8
ALPHA = (2 * DEPTH) ** 0.25
LN_EPS = 1e-5
ADAM_LR = 0.001
ADAM_B1 = 0.9
ADAM_B2 = 0.999
ADAM_EPS = 1e-08
ADAM_WD = 0.01
ADAM_STEP = 10

LANE = 128
SUB = 8
VMEM_MB = 56
NEG = -1e30

MESH = pl.DeviceIdType.MESH


def _cparams(sem=None, vmem_mb=VMEM_MB):
    return pltpu.CompilerParams(dimension_semantics=sem, vmem_limit_bytes=vmem_mb << 20)


def _mm(a, b):
    return jnp.dot(a.astype(_BF), b.astype(_BF), preferred_element_type=F32)


def _mm_nt(a, b):
    return lax.dot_general(a.astype(_BF), b.astype(_BF), (((1,), (1,)), ((), ())), preferred_element_type=F32)


def _mm_tn(a, b):
    return lax.dot_general(a.astype(_BF), b.astype(_BF), (((0,), (0,)), ((), ())), preferred_element_type=F32)


def _sigmoid(x):
    return jax.nn.sigmoid(x)


def _rows(start, size):
    return pl.ds(pl.multiple_of(start, SUB), size)


def _strip(T, col):
    return pl.BlockSpec((T, LANE), lambda j: (0, col(j)), pipeline_mode=pl.Buffered(1))


LRU_CH = 256
_GELU_C0 = math.sqrt(2.0 / math.pi)
_GELU_C1 = 0.044715


def _gelu_parts(x):
    x2 = x * x
    t = jnp.tanh(_GELU_C0 * (x + _GELU_C1 * x * x2))
    val = 0.5 * x * (1.0 + t)
    der = 0.5 * (1.0 + t) + 0.5 * x * (1.0 - t * t) * _GELU_C0 * (1.0 + 3.0 * _GELU_C1 * x2)
    return val, der


def _softplus_neg(lam):
    e = jnp.exp(-jnp.abs(lam))
    w = 1.0 + e
    l1p = jnp.where(w == 1.0, e, jnp.log(w) * (e / jnp.where(w == 1.0, 1.0, w - 1.0)))
    return jnp.maximum(-lam, 0.0) + l1p


def _window(ref, t0, ch, T):
    prev = ref[_rows(jnp.maximum(t0 - SUB, 0), SUB), :].astype(F32)
    nxt = ref[_rows(jnp.minimum(t0 + ch, T - SUB), SUB), :].astype(F32)
    prev = jnp.where(t0 > 0, prev, 0.0)
    nxt = jnp.where(t0 + ch < T, nxt, 0.0)
    return jnp.concatenate([prev, ref[_rows(t0, ch), :].astype(F32), nxt], axis=0)


def _tap(win, shift, ch):
    n = win.shape[0]
    return pltpu.roll(win, (-shift) % n, 0)[SUB:SUB + ch]


def _lru_conv(xb_ref, vec, t0, T):
    win = _window(xb_ref, t0, LRU_CH, T)
    xc = jnp.broadcast_to(vec[4:5, :], (LRU_CH, LANE))
    for j in range(CONV_WIDTH):
        xc = xc + _tap(win, j - CONV_WIDTH // 2, LRU_CH) * vec[j:j + 1, :]
    return xc


def _lru_dir(pre_a, pre_x, sp):
    r = _sigmoid(pre_a)
    i = _sigmoid(pre_x)
    log_a = (-LRU_C) * r * sp
    a = jnp.exp(log_a)
    z = jnp.tanh(-log_a) * (a * a + 1.0)
    s = jnp.sqrt(z)
    return r, i, a, s


def _scan_tile(a, b, reverse, row):
    for k in (1, 2, 4):
        if not reverse:
            a_s, b_s, m = pltpu.roll(a, k, 0), pltpu.roll(b, k, 0), row >= k
        else:
            a_s, b_s, m = pltpu.roll(a, SUB - k, 0), pltpu.roll(b, SUB - k, 0), row < SUB - k
        b = jnp.where(m, a * b_s + b, b)
        a = jnp.where(m, a * a_s, a)
    return a, b


def _bcast_row(x, r):
    return jnp.broadcast_to(x[r:r + 1, :], (SUB, LANE))


def _lru_prepare(xb_ref, w4_ref, vec, xc_ref, af_ref, uf_ref, ab_ref, ub_ref, T):
    sp_f = _softplus_neg(vec[9:10, :])
    sp_b = _softplus_neg(vec[10:11, :])
    w4 = w4_ref[0]

    def body(c, carry):
        t0 = c * LRU_CH
        xc = _lru_conv(xb_ref, vec, t0, T)
        if xc_ref is not None:
            xc_ref[_rows(t0, LRU_CH), :] = xc
        pre = _mm(xc, w4)
        _, i, a, s = _lru_dir(pre[:, 0:128] + vec[5:6, :], pre[:, 128:256] + vec[6:7, :], sp_f)
        af_ref[_rows(t0, LRU_CH), :] = a
        uf_ref[_rows(t0, LRU_CH), :] = s * (i * xc)
        _, i, a, s = _lru_dir(pre[:, 256:384] + vec[7:8, :], pre[:, 384:512] + vec[8:9, :], sp_b)
        ab_ref[_rows(t0, LRU_CH), :] = a
        ub_ref[_rows(t0, LRU_CH), :] = s * (i * xc)
        return carry

    lax.fori_loop(0, T // LRU_CH, body, 0)


def _lru_scan(af_ref, uf_ref, ab_ref, ub_ref, T):
    nt = T // SUB
    row = lax.broadcasted_iota(jnp.int32, (SUB, LANE), 0)

    def body(j, carry):
        hf, hb = carry
        sf = _rows(j * SUB, SUB)
        sb = _rows((nt - 1 - j) * SUB, SUB)
        a, b = _scan_tile(af_ref[sf, :], uf_ref[sf, :], False, row)
        h = a * hf + b
        uf_ref[sf, :] = h
        hf = _bcast_row(h, SUB - 1)
        a, b = _scan_tile(ab_ref[sb, :], ub_ref[sb, :], True, row)
        h = a * hb + b
        ub_ref[sb, :] = h
        hb = _bcast_row(h, 0)
        return hf, hb

    z = jnp.zeros((SUB, LANE), F32)
    lax.fori_loop(0, nt, body, (z, z), unroll=4)


def _lru_fwd_call(proj, vec, w4):
    T = proj.shape[0]

    def body(xb_ref, gate_ref, vec_ref, w4_ref, y_ref, af_ref, uf_ref, ab_ref, ub_ref):
        vec = vec_ref[...]
        _lru_prepare(xb_ref, w4_ref, vec, None, af_ref, uf_ref, ab_ref, ub_ref, T)
        _lru_scan(af_ref, uf_ref, ab_ref, ub_ref, T)

        def out(c, carry):
            rows = _rows(c * LRU_CH, LRU_CH)
            gl, _ = _gelu_parts(gate_ref[rows, :])
            y_ref[rows, :] = (uf_ref[rows, :] + ub_ref[rows, :]) * gl
            return carry

        lax.fori_loop(0, T // LRU_CH, out, 0)

    return pl.pallas_call(
        body, name="lru_fwd", grid=(LRU_WIDTH // LANE,),
        in_specs=[_strip(T, lambda j: j), _strip(T, lambda j: j + 3),
                  pl.BlockSpec((16, LANE), lambda j: (0, j)),
                  pl.BlockSpec((1, LANE, 4 * LANE), lambda j: (j, 0, 0))],
        out_specs=_strip(T, lambda j: j),
        out_shape=jax.ShapeDtypeStruct((T, LRU_WIDTH), F32),
        scratch_shapes=[pltpu.VMEM((T, LANE), F32)] * 4,
        compiler_params=_cparams(("arbitrary",)),
    )(proj, proj, vec, w4)


def _lru_bwd_call(proj, dycat, vec, w4):
    T = proj.shape[0]
    nt = T // SUB
    nch = T // LRU_CH

    def body(xb_ref, gate_ref, dy_ref, vec_ref, w4_ref, dp_ref, dvec_ref, dw4_ref,
             xc_ref, af_ref, hf_ref, ab_ref, hb_ref, dh_ref):
        dxb_ref, dgate_ref = dp_ref.at[0], dp_ref.at[1]
        vec = vec_ref[...]
        _lru_prepare(xb_ref, w4_ref, vec, xc_ref, af_ref, hf_ref, ab_ref, hb_ref, T)
        _lru_scan(af_ref, hf_ref, ab_ref, hb_ref, T)

        def gate_bwd(c, carry):
            rows = _rows(c * LRU_CH, LRU_CH)
            gl, dgl = _gelu_parts(gate_ref[rows, :])
            dy = dy_ref[rows, :]
            dgate_ref[rows, :] = (dy * (hf_ref[rows, :] + hb_ref[rows, :]) * dgl).astype(dgate_ref.dtype)
            dh_ref[rows, :] = dy * gl
            return carry

        lax.fori_loop(0, nch, gate_bwd, 0)

        row = lax.broadcasted_iota(jnp.int32, (SUB, LANE), 0)

        def adj(j, carry):
            gf, a_next, gb, a_prev = carry
            tf = nt - 1 - j
            sf = _rows(tf * SUB, SUB)
            a_t = af_ref[sf, :]
            h_t = hf_ref[sf, :]
            coef = jnp.where(row == SUB - 1, a_next, pltpu.roll(a_t, SUB - 1, 0))
            ac, bc = _scan_tile(coef, dh_ref[sf, :], True, row)
            g = ac * gf + bc
            h_prev = hf_ref[_rows(jnp.maximum(tf - 1, 0) * SUB, SUB), :]
            h_prev = jnp.where(tf > 0, _bcast_row(h_prev, SUB - 1), 0.0)
            hs = jnp.where(row == 0, h_prev, pltpu.roll(h_t, 1, 0))
            af_ref[sf, :] = g * hs
            hf_ref[sf, :] = g
            gf = _bcast_row(g, 0)
            a_next = _bcast_row(a_t, 0)
            sb = _rows(j * SUB, SUB)
            a_t = ab_ref[sb, :]
            h_t = hb_ref[sb, :]
            coef = jnp.where(row == 0, a_prev, pltpu.roll(a_t, 1, 0))
            ac, bc = _scan_tile(coef, dh_ref[sb, :], False, row)
            g = ac * gb + bc
            h_next = hb_ref[_rows(jnp.minimum(j + 1, nt - 1) * SUB, SUB), :]
            h_next = jnp.where(j < nt - 1, _bcast_row(h_next, 0), 0.0)
            hs = jnp.where(row == SUB - 1, h_next, pltpu.roll(h_t, SUB - 1, 0))
            ab_ref[sb, :] = g * hs
            hb_ref[sb, :] = g
            gb = _bcast_row(g, SUB - 1)
            a_prev = _bcast_row(a_t, SUB - 1)
            return gf, a_next, gb, a_prev

        z = jnp.zeros((SUB, LANE), F32)
        lax.fori_loop(0, nt, adj, (z, z, z, z), unroll=2)

        sp_f = _softplus_neg(vec[9:10, :])
        sp_b = _softplus_neg(vec[10:11, :])
        w4 = w4_ref[0]
        dw4_ref[...] = jnp.zeros_like(dw4_ref)

        def one_dir(pre_a, pre_x, sp, xc, du, da):
            r, i, a, s = _lru_dir(pre_a, pre_x, sp)
            d_i = du * s * xc
            dxc = du * s * i
            d_s = du * i * xc
            d_log = da * a - d_s * (a * a) / s
            d_r = d_log * (-LRU_C) * sp
            d_sp = jnp.sum(d_log * (-LRU_C) * r, axis=0, keepdims=True)
            return d_r * r * (1.0 - r), d_i * i * (1.0 - i), dxc, d_sp

        def gates_bwd(c, carry):
            db, dspf, dspb = carry
            rows = _rows(c * LRU_CH, LRU_CH)
            xc = xc_ref[rows, :]
            pre = _mm(xc, w4)
            dpa_f, dpx_f, dxc_f, d_sp_f = one_dir(pre[:, 0:128] + vec[5:6, :], pre[:, 128:256] + vec[6:7, :],
                                                  sp_f, xc, hf_ref[rows, :], af_ref[rows, :])
            dpa_b, dpx_b, dxc_b, d_sp_b = one_dir(pre[:, 256:384] + vec[7:8, :], pre[:, 384:512] + vec[8:9, :],
                                                  sp_b, xc, hb_ref[rows, :], ab_ref[rows, :])
            dpre = jnp.concatenate([dpa_f, dpx_f, dpa_b, dpx_b], axis=1)
            dw4_ref[0] += _mm_tn(xc, dpre)
            dh_ref[rows, :] = dxc_f + dxc_b + _mm_nt(dpre, w4)
            return db + jnp.sum(dpre, axis=0, keepdims=True), dspf + d_sp_f, dspb + d_sp_b

        z1 = jnp.zeros((1, LANE), F32)
        db, dspf, dspb = lax.fori_loop(0, nch, gates_bwd, (jnp.zeros((1, 4 * LANE), F32), z1, z1))

        def conv_bwd(c, carry):
            t0 = c * LRU_CH
            rows = _rows(t0, LRU_CH)
            dwin = _window(dh_ref, t0, LRU_CH, T)
            xwin = _window(xb_ref, t0, LRU_CH, T)
            dxc = dh_ref[rows, :]
            dxb = jnp.zeros((LRU_CH, LANE), F32)
            out = []
            for j in range(CONV_WIDTH):
                off = j - CONV_WIDTH // 2
                dxb = dxb + _tap(dwin, -off, LRU_CH) * vec[j:j + 1, :]
                out.append(carry[j] + jnp.sum(dxc * _tap(xwin, off, LRU_CH), axis=0, keepdims=True))
            dxb_ref[rows, :] = dxb.astype(dxb_ref.dtype)
            out.append(carry[CONV_WIDTH] + jnp.sum(dxc, axis=0, keepdims=True))
            return tuple(out)

        dconv = lax.fori_loop(0, nch, conv_bwd, (z1,) * (CONV_WIDTH + 1))
        dlam_f = dspf * (-_sigmoid(-vec[9:10, :]))
        dlam_b = dspb * (-_sigmoid(-vec[10:11, :]))
        dvec_ref[...] = jnp.concatenate(
            list(dconv) + [db[:, 0:128], db[:, 128:256], db[:, 256:384], db[:, 384:512], dlam_f, dlam_b,
                           jnp.zeros((5, LANE), F32)], axis=0)

    ns = LRU_WIDTH // LANE
    return pl.pallas_call(
        body, name="lru_bwd", grid=(ns,),
        in_specs=[_strip(T, lambda j: j), _strip(T, lambda j: j + 3), _strip(T, lambda j: j),
                  pl.BlockSpec((16, LANE), lambda j: (0, j)),
                  pl.BlockSpec((1, LANE, 4 * LANE), lambda j: (j, 0, 0))],
        out_specs=[pl.BlockSpec((2, T, LANE), lambda j: (0, 0, j), pipeline_mode=pl.Buffered(1)),
                   pl.BlockSpec((16, LANE), lambda j: (0, j)),
                   pl.BlockSpec((1, LANE, 4 * LANE), lambda j: (j, 0, 0))],
        out_shape=[jax.ShapeDtypeStruct((2, T, LRU_WIDTH), _BF),
                   jax.ShapeDtypeStruct((16, LRU_WIDTH), F32), jax.ShapeDtypeStruct((ns, LANE, 4 * LANE), F32)],
        scratch_shapes=[pltpu.VMEM((T, LANE), F32)] * 6,
        compiler_params=_cparams(("arbitrary",)),
    )(proj, proj, dycat, vec, w4)


def _lru_pack(cw, cb, wa, ba, wx, bx, lam):
    vec = jnp.concatenate([cw, cb[None], ba[0:1], bx[0:1], ba[1:2], bx[1:2], lam, jnp.zeros((5, LRU_WIDTH), F32)], axis=0)
    eye = jnp.eye(2, dtype=F32)
    mats = []
    for w in (wa[0], wx[0], wa[1], wx[1]):
        bd = jnp.einsum("jsio,st->jsito", w.reshape(3, 2, 64, 64), eye)
        mats.append(bd.reshape(3, LANE, LANE))
    return vec, jnp.concatenate(mats, axis=2).astype(_BF)


def _lru_unpack(dvec, dw4):
    def blocks(m):
        m = m.reshape(3, 2, 64, 2, 64)
        return jnp.stack([m[:, 0, :, 0, :], m[:, 1, :, 1, :]], axis=1).reshape(6, 64, 64)
    parts = [blocks(dw4[:, :, k * LANE:(k + 1) * LANE]) for k in range(4)]
    dwa = jnp.stack([parts[0], parts[2]])
    dwx = jnp.stack([parts[1], parts[3]])
    dba = jnp.stack([dvec[5], dvec[7]])
    dbx = jnp.stack([dvec[6], dvec[8]])
    return dvec[0:4], dvec[4], dwa, dba, dwx, dbx, dvec[9:11]


RC = RET_CHUNK


def _ret_tables(T):
    half = HEAD_DIM // 2
    pos = jnp.arange(T, dtype=F32)
    inv_freq = ROPE_BASE ** (-jnp.arange(half, dtype=F32) / half)
    ang = pos[:, None] * inv_freq[None, :]
    cos = jnp.tile(jnp.cos(ang), (1, 4))
    sin = jnp.tile(jnp.concatenate([-jnp.sin(ang), jnp.sin(ang)], axis=1), (1, 2))
    log_g = jnp.log1p(-jnp.exp2(-5.0 - jnp.arange(RET_HEADS, dtype=F32)))
    idx = jnp.arange(RC, dtype=F32)
    dec = jnp.exp(jnp.abs(idx[:, None] - idx[None, :]) * log_g[:, None, None])
    lg = jnp.repeat(log_g, HEAD_DIM).reshape(3, 1, LANE)
    col = idx[None, :, None]
    rtab = jnp.stack([jnp.exp((RC - 1 - col) * lg), jnp.exp(col * lg),
                      jnp.exp((col + 1.0) * lg), jnp.exp((RC - col) * lg)], axis=1)
    gch = jnp.broadcast_to(jnp.exp(RC * lg), (3, SUB, LANE))
    return cos, sin, dec, rtab, gch


def _swap32(x, lane):
    return jnp.where((lane & 32) == 0, pltpu.roll(x, LANE - 32, 1), pltpu.roll(x, 32, 1))


def _head_mean(x, m0, m1):
    s0 = jnp.sum(x * m0, axis=-1, keepdims=True)
    s1 = jnp.sum(x * m1, axis=-1, keepdims=True)
    return (s0 * m0 + s1 * m1) * (1.0 / HEAD_DIM)


def _ret_masks():
    lane = lax.broadcasted_iota(jnp.int32, (RC, LANE), 1)
    m0 = (lane < HEAD_DIM).astype(F32)
    r = lax.broadcasted_iota(jnp.int32, (LANE, LANE), 0) // HEAD_DIM
    c = lax.broadcasted_iota(jnp.int32, (LANE, LANE), 1) // HEAD_DIM
    return lane, m0, 1.0 - m0, (r == c).astype(F32)


def _ret_specs(T):
    const = lambda shape, imap: pl.BlockSpec(shape, imap)
    return [_strip(T, lambda j: j + 6), _strip(T, lambda j: j + 9), _strip(T, lambda j: j + 12),
            _strip(T, lambda j: j + 15),
            pl.BlockSpec((T, LANE), lambda j: (0, 0), pipeline_mode=pl.Buffered(1)),
            pl.BlockSpec((T, LANE), lambda j: (0, 0), pipeline_mode=pl.Buffered(1)),
            const((2, RC, RC), lambda j: (j, 0, 0)),
            const((1, 4, RC, LANE), lambda j: (j, 0, 0, 0)),
            const((1, SUB, LANE), lambda j: (j, 0, 0)),
            const((SUB, LANE), lambda j: (0, j))]


def _ret_fwd_call(proj, tables, gnw8):
    T = proj.shape[0]
    nc = T // RC
    cos, sin, dec, rtab, gch = tables

    def body(q_ref, k_ref, v_ref, g_ref, cos_ref, sin_ref, dec_ref, rtab_ref, gch_ref, gnw_ref, y_ref, stf_ref):
        lane, m0, m1, bd = _ret_masks()
        gch_v = gch_ref[0][0:1, :]
        gnw = gnw_ref[0:1, :]
        dkf, dkb, dqf, dqb = rtab_ref[0, 0], rtab_ref[0, 1], rtab_ref[0, 2], rtab_ref[0, 3]

        def rope(x, rows):
            return x * cos_ref[rows, :] + _swap32(x, lane) * sin_ref[rows, :]

        def pass_a(n, st):
            rows = _rows(n * RC, RC)
            stf_ref[n] = st
            kr = rope(k_ref[rows, :], rows) * (HEAD_DIM ** -0.5)
            return gch_v * st + _mm_tn(kr * dkf, v_ref[rows, :]) * bd

        lax.fori_loop(0, nc, pass_a, jnp.zeros((LANE, LANE), F32))

        def pass_b(i, stb):
            n = nc - 1 - i
            rows = _rows(n * RC, RC)
            qr = rope(q_ref[rows, :], rows)
            kr = rope(k_ref[rows, :], rows) * (HEAD_DIM ** -0.5)
            v = v_ref[rows, :]
            o = _mm(qr * dqf, stf_ref[n]) + _mm(qr * dqb, stb)
            for h, m in ((0, m0), (1, m1)):
                s = _mm_nt(qr * m, kr) * dec_ref[h]
                o = o + _mm(s, v * m)
            oc = o - _head_mean(o, m0, m1)
            on = oc * lax.rsqrt(_head_mean(oc * oc, m0, m1) + GN_EPS)
            g = g_ref[rows, :]
            y_ref[rows, :] = (g * _sigmoid(g)) * (on * gnw)
            return gch_v * stb + _mm_tn(kr * dkb, v) * bd

        lax.fori_loop(0, nc, pass_b, jnp.zeros((LANE, LANE), F32))

    return pl.pallas_call(
        body, name="ret_fwd", grid=(RET_WIDTH // LANE,),
        in_specs=_ret_specs(T),
        out_specs=_strip(T, lambda j: j),
        out_shape=jax.ShapeDtypeStruct((T, RET_WIDTH), F32),
        scratch_shapes=[pltpu.VMEM((nc, LANE, LANE), F32)],
        compiler_params=_cparams(("arbitrary",)),
    )(proj, proj, proj, proj, cos, sin, dec, rtab, gch, gnw8)


def _ret_bwd_call(proj, dycat, tables, gnw8):
    T = proj.shape[0]
    nc = T // RC
    cos, sin, dec, rtab, gch = tables

    def body(q_ref, k_ref, v_ref, g_ref, cos_ref, sin_ref, dec_ref, rtab_ref, gch_ref, gnw_ref, dy_ref,
             dp_ref, dgnw_ref, stf_ref, dstb_ref, dkr_ref, dv_ref):
        lane, m0, m1, bd = _ret_masks()
        gch_v = gch_ref[0][0:1, :]
        gnw = gnw_ref[0:1, :]
        dkf, dkb, dqf, dqb = rtab_ref[0, 0], rtab_ref[0, 1], rtab_ref[0, 2], rtab_ref[0, 3]
        scale = HEAD_DIM ** -0.5
        zst = jnp.zeros((LANE, LANE), F32)

        def rope(x, rows):
            return x * cos_ref[rows, :] + _swap32(x, lane) * sin_ref[rows, :]

        def rope_t(d, rows):
            return d * cos_ref[rows, :] + _swap32(d * sin_ref[rows, :], lane)

        def pass_a(n, st):
            rows = _rows(n * RC, RC)
            stf_ref[n] = st
            kr = rope(k_ref[rows, :], rows) * scale
            return gch_v * st + _mm_tn(kr * dkf, v_ref[rows, :]) * bd

        lax.fori_loop(0, nc, pass_a, zst)

        def pass_b(i, carry):
            stb, d_f, dgnw = carry
            n = nc - 1 - i
            rows = _rows(n * RC, RC)
            qr = rope(q_ref[rows, :], rows)
            kr = rope(k_ref[rows, :], rows) * scale
            v = v_ref[rows, :]
            stf = stf_ref[n]
            qf = qr * dqf
            qb = qr * dqb
            o = _mm(qf, stf) + _mm(qb, stb)
            s_h = []
            for h, m in ((0, m0), (1, m1)):
                s = _mm_nt(qr * m, kr) * dec_ref[h]
                s_h.append(s)
                o = o + _mm(s, v * m)
            oc = o - _head_mean(o, m0, m1)
            rstd = lax.rsqrt(_head_mean(oc * oc, m0, m1) + GN_EPS)
            on = oc * rstd
            g = g_ref[rows, :]
            sg = _sigmoid(g)
            dy = dy_ref[rows, :]
            dp_ref[3, rows, :] = (dy * (on * gnw) * (sg * (1.0 + g * (1.0 - sg)))).astype(dp_ref.dtype)
            t = dy * (g * sg)
            dgnw = dgnw + jnp.sum(t * on, axis=0, keepdims=True)
            don = t * gnw
            do = rstd * (don - _head_mean(don, m0, m1) - on * _head_mean(don * on, m0, m1))
            dqr = _mm_nt(do, stf) * dqf + _mm_nt(do, stb) * dqb
            dkr = _mm_nt(v, d_f) * dkf
            dv = _mm(kr * dkf, d_f)
            for h, m in ((0, m0), (1, m1)):
                ds = _mm_nt(do * m, v) * dec_ref[h]
                dqr = dqr + _mm(ds, kr * m)
                dkr = dkr + _mm_tn(ds, qr * m)
                dv = dv + _mm_tn(s_h[h], do * m)
            dp_ref[0, rows, :] = rope_t(dqr, rows).astype(dp_ref.dtype)
            dkr_ref[rows, :] = dkr
            dv_ref[rows, :] = dv
            dstb_ref[n] = _mm_tn(qb, do) * bd
            d_f = _mm_tn(qf, do) * bd + gch_v * d_f
            stb = gch_v * stb + _mm_tn(kr * dkb, v) * bd
            return stb, d_f, dgnw

        _, _, dgnw = lax.fori_loop(0, nc, pass_b, (zst, zst, jnp.zeros((1, LANE), F32)))
        dgnw_ref[...] = jnp.concatenate([dgnw, jnp.zeros((SUB - 1, LANE), F32)], axis=0)

        def pass_c(n, d_b):
            rows = _rows(n * RC, RC)
            kr = rope(k_ref[rows, :], rows) * scale
            v = v_ref[rows, :]
            dkr = dkr_ref[rows, :] + _mm_nt(v, d_b) * dkb
            dp_ref[1, rows, :] = (rope_t(dkr, rows) * scale).astype(dp_ref.dtype)
            dp_ref[2, rows, :] = (dv_ref[rows, :] + _mm(kr * dkb, d_b)).astype(dp_ref.dtype)
            return dstb_ref[n] + gch_v * d_b

        lax.fori_loop(0, nc, pass_c, zst)

    return pl.pallas_call(
        body, name="ret_bwd", grid=(RET_WIDTH // LANE,),
        in_specs=_ret_specs(T) + [_strip(T, lambda j: j + 3)],
        out_specs=[pl.BlockSpec((4, T, LANE), lambda j: (0, 0, j), pipeline_mode=pl.Buffered(1)),
                   pl.BlockSpec((SUB, LANE), lambda j: (0, j))],
        out_shape=[jax.ShapeDtypeStruct((4, T, RET_WIDTH), _BF), jax.ShapeDtypeStruct((SUB, RET_WIDTH), F32)],
        scratch_shapes=[pltpu.VMEM((nc, LANE, LANE), F32), pltpu.VMEM((nc, LANE, LANE), F32),
                        pltpu.VMEM((T, LANE), F32), pltpu.VMEM((T, LANE), F32)],
        compiler_params=_cparams(("arbitrary",)),
    )(proj, proj, proj, proj, cos, sin, dec, rtab, gch, gnw8, dycat)


NA_Q = 2 * GRID_W
NA_WROWS = 10
NA_K = NA_WROWS * GRID_W
NA_TYPES = 5


def _na_onehots(rows_n):
    reps = [(0, 0), (2, 0), (4, 0), (rows_n - 4, rows_n - NA_WROWS), (rows_n - 2, rows_n - NA_WROWS)]
    rm = np.zeros((NA_TYPES, 2, NA_WROWS, 2 * NA_KH - 1), np.float32)
    for t, (r, ws) in enumerate(reps):
        for qh in range(2):
            qrow = r + qh
            rstart = min(max(qrow - NA_KH // 2, 0), rows_n - NA_KH)
            for kh in range(NA_WROWS):
                krow = ws + kh
                if rstart <= krow < rstart + NA_KH:
                    rm[t, qh, kh, krow - qrow + NA_KH - 1] = 1.0
    cm = np.zeros((GRID_W, GRID_W, 2 * NA_KW - 1), np.float32)
    for qc in range(GRID_W):
        cstart = min(max(qc - NA_KW // 2, 0), GRID_W - NA_KW)
        for kc in range(cstart, cstart + NA_KW):
            cm[qc, kc, kc - qc + NA_KW - 1] = 1.0
    return rm, cm


def _na_bias_tables(rpb, rows_n):
    rm, cm = _na_onehots(rows_n)
    val = jnp.einsum("hab,tqka,xyb->htqxky", rpb, rm, cm, precision=lax.Precision.HIGHEST)
    valid = np.einsum("tqk,xy->tqxky", rm.sum(-1), cm.sum(-1)) > 0.5
    tab = jnp.where(valid[None], val, NEG).reshape(2, 2, NA_TYPES, NA_Q, NA_K)
    return tab, jnp.swapaxes(tab, -1, -2)


def _na_bias_grad(dtab, rows_n):
    rm, cm = _na_onehots(rows_n)
    d6 = dtab.reshape(NA_HEADS, NA_TYPES, 2, GRID_W, NA_WROWS, GRID_W)
    return jnp.einsum("htqxky,tqka,xyb->hab", d6, rm, cm, precision=lax.Precision.HIGHEST)


def _na_step(p, npairs, rows_n):
    ws = jnp.clip(2 * p - NA_KH // 2, 0, rows_n - NA_WROWS)
    koff = pl.multiple_of(ws * GRID_W, LANE)
    typ = jnp.where(p == 0, 0, jnp.where(p == 1, 1, jnp.where(p == npairs - 2, 3, jnp.where(p == npairs - 1, 4, 2))))
    return _rows(p * NA_Q, NA_Q), pl.ds(koff, NA_K), typ


def _na_fwd_call(proj, btab):
    T = proj.shape[0]
    npairs, rows_n = T // NA_Q, T // GRID_W

    def body(q_ref, k_ref, v_ref, b_ref, o_ref):
        lane = lax.broadcasted_iota(jnp.int32, (NA_Q, LANE), 1)
        m0 = (lane < HEAD_DIM).astype(F32)
        m1 = 1.0 - m0

        def step(p, carry):
            qrows, krows, typ = _na_step(p, npairs, rows_n)
            q = q_ref[qrows, :]
            kw = k_ref[krows, :]
            vw = v_ref[krows, :]
            o = jnp.zeros((NA_Q, LANE), F32)
            for h, m in ((0, m0), (1, m1)):
                s = _mm_nt(q * m, kw) * (HEAD_DIM ** -0.5) + b_ref[0, h, typ]
                e = jnp.exp(s - jnp.max(s, axis=-1, keepdims=True))
                pr = e / jnp.sum(e, axis=-1, keepdims=True)
                o = o + _mm(pr, vw) * m
            o_ref[qrows, :] = o
            return carry

        lax.fori_loop(0, npairs, step, 0)

    return pl.pallas_call(
        body, name="na_fwd", grid=(NA_WIDTH // LANE,),
        in_specs=[_strip(T, lambda j: j + 18), _strip(T, lambda j: j + 20), _strip(T, lambda j: j + 22),
                  pl.BlockSpec((1, 2, NA_TYPES, NA_Q, NA_K), lambda j: (j, 0, 0, 0, 0))],
        out_specs=_strip(T, lambda j: j),
        out_shape=jax.ShapeDtypeStruct((T, NA_WIDTH), F32),
        compiler_params=_cparams(("arbitrary",)),
    )(proj, proj, proj, btab)


def _na_bwd_call(proj, dycat, btab, btab_t):
    T = proj.shape[0]
    npairs, rows_n = T // NA_Q, T // GRID_W
    scale = HEAD_DIM ** -0.5

    def body(q_ref, k_ref, v_ref, do_ref, b_ref, bt_ref, dq_ref, dk_ref, dv_ref, db_ref, dka_ref, dva_ref):
        lane = lax.broadcasted_iota(jnp.int32, (NA_Q, LANE), 1)
        m0 = (lane < HEAD_DIM).astype(F32)
        m1 = 1.0 - m0
        dka_ref[...] = jnp.zeros_like(dka_ref)
        dva_ref[...] = jnp.zeros_like(dva_ref)
        db_ref[...] = jnp.zeros_like(db_ref)

        def step(p, carry):
            qrows, krows, typ = _na_step(p, npairs, rows_n)
            q = q_ref[qrows, :]
            do = do_ref[qrows, :]
            kw = k_ref[krows, :]
            vw = v_ref[krows, :]
            dq = jnp.zeros((NA_Q, LANE), F32)
            dk = jnp.zeros((NA_K, LANE), F32)
            dv = jnp.zeros((NA_K, LANE), F32)
            for h, m in ((0, m0), (1, m1)):
                qm = q * m
                dom = do * m
                s = _mm_nt(qm, kw) * scale + b_ref[0, h, typ]
                e = jnp.exp(s - jnp.max(s, axis=-1, keepdims=True))
                pr = e / jnp.sum(e, axis=-1, keepdims=True)
                dpr = _mm_nt(dom, vw)
                ds = pr * (dpr - jnp.sum(pr * dpr, axis=-1, keepdims=True))
                db_ref[0, h, typ] += ds
                dq = dq + _mm(ds * scale, kw) * m
                st = _mm_nt(kw, qm) * scale + bt_ref[0, h, typ]
                et = jnp.exp(st - jnp.max(st, axis=0, keepdims=True))
                prt = et / jnp.sum(et, axis=0, keepdims=True)
                dprt = _mm_nt(vw, dom)
                dst = prt * (dprt - jnp.sum(prt * dprt, axis=0, keepdims=True))
                dk = dk + _mm(dst * scale, qm)
                dv = dv + _mm(prt, dom)
            dq_ref[qrows, :] = dq.astype(dq_ref.dtype)
            dka_ref[krows, :] += dk
            dva_ref[krows, :] += dv
            return carry

        lax.fori_loop(0, npairs, step, 0)
        dk_ref[...] = dka_ref[...].astype(dk_ref.dtype)
        dv_ref[...] = dva_ref[...].astype(dv_ref.dtype)

    tab = pl.BlockSpec((1, 2, NA_TYPES, NA_Q, NA_K), lambda j: (j, 0, 0, 0, 0))
    tab_t = pl.BlockSpec((1, 2, NA_TYPES, NA_K, NA_Q), lambda j: (j, 0, 0, 0, 0))
    out = lambda col: pl.BlockSpec((T, LANE), lambda j: (0, col(j)), pipeline_mode=pl.Buffered(1))
    dq, dk, dv, db = pl.pallas_call(
        body, name="na_bwd", grid=(NA_WIDTH // LANE,),
        in_specs=[_strip(T, lambda j: j + 18), _strip(T, lambda j: j + 20), _strip(T, lambda j: j + 22),
                  _strip(T, lambda j: j + 6), tab, tab_t],
        out_specs=[out(lambda j: j), out(lambda j: j), out(lambda j: j), tab],
        out_shape=[jax.ShapeDtypeStruct((T, NA_WIDTH), _BF)] * 3
        + [jax.ShapeDtypeStruct((2, 2, NA_TYPES, NA_Q, NA_K), F32)],
        scratch_shapes=[pltpu.VMEM((T, LANE), F32), pltpu.VMEM((T, LANE), F32)],
        compiler_params=_cparams(("arbitrary",)),
    )(proj, proj, proj, dycat, btab, btab_t)
    dp = jnp.concatenate([dq, dk, dv], axis=1)
    return jnp.stack([dp[:, :W_BLK], dp[:, W_BLK:]]), db


W_BLK = IN_WIDTH // N_DEV
TM = 512


def _ln_fwd(z, g, b):
    zc = z - jnp.mean(z, axis=-1, keepdims=True)
    var = jnp.mean(zc * zc, axis=-1, keepdims=True)
    return zc * lax.rsqrt(var + LN_EPS) * g + b


def _ln_bwd(dy, z, g):
    zc = z - jnp.mean(z, axis=-1, keepdims=True)
    rstd = lax.rsqrt(jnp.mean(zc * zc, axis=-1, keepdims=True) + LN_EPS)
    xhat = zc * rstd
    dxh = dy * g
    dz = rstd * (dxh - jnp.mean(dxh, axis=-1, keepdims=True) - xhat * jnp.mean(dxh * xhat, axis=-1, keepdims=True))
    return dz, dy * xhat


def _row_tile(T):
    return 1024 if T % 1024 == 0 else TM


def _inproj_call(xb, w):
    T = xb.shape[0]
    tm = _row_tile(T)

    def body(x_ref, w_ref, o_ref):
        o_ref[...] = _mm(x_ref[...], w_ref[...])

    return pl.pallas_call(
        body, name="inproj", grid=(T // tm, N_DEV),
        in_specs=[pl.BlockSpec((tm, D_MODEL), lambda i, n: (i, 0)),
                  pl.BlockSpec((None, D_MODEL, W_BLK), lambda i, n: (n, 0, 0))],
        out_specs=pl.BlockSpec((tm, W_BLK), lambda i, n: (i, n)),
        out_shape=jax.ShapeDtypeStruct((T, IN_WIDTH), F32),
        compiler_params=_cparams(("parallel", "arbitrary")),
    )(xb, w)


def _vec_spec():
    return pl.BlockSpec((1, D_MODEL), lambda *_: (0, 0))


def _outproj_ln_call(y_lru, y_ret, y_na, x, w, g, b):
    T = x.shape[0]

    def body(yl_ref, yr_ref, yn_ref, x_ref, w_ref, g_ref, b_ref, z_ref, x1_ref, x1b_ref, yc_ref):
        yc_ref[:, 0:LRU_WIDTH] = yl_ref[...].astype(yc_ref.dtype)
        yc_ref[:, LRU_WIDTH:LRU_WIDTH + RET_WIDTH] = yr_ref[...].astype(yc_ref.dtype)
        yc_ref[:, LRU_WIDTH + RET_WIDTH:] = yn_ref[...].astype(yc_ref.dtype)
        z = ALPHA * x_ref[...] + _mm(yc_ref[...], w_ref[...].reshape(D_MODEL, D_MODEL))
        z_ref[...] = z
        x1 = _ln_fwd(z, g_ref[...], b_ref[...])
        x1_ref[...] = x1
        x1b_ref[...] = x1.astype(x1b_ref.dtype)

    row = lambda w_: pl.BlockSpec((TM, w_), lambda i: (i, 0))
    return pl.pallas_call(
        body, name="outproj_ln", grid=(T // TM,),
        in_specs=[row(LRU_WIDTH), row(RET_WIDTH), row(NA_WIDTH), row(D_MODEL),
                  pl.BlockSpec((N_DEV, LANE, D_MODEL), lambda i: (0, 0, 0)), _vec_spec(), _vec_spec()],
        out_specs=[row(D_MODEL)] * 4,
        out_shape=[jax.ShapeDtypeStruct((T, D_MODEL), F32), jax.ShapeDtypeStruct((T, D_MODEL), F32),
                   jax.ShapeDtypeStruct((T, D_MODEL), _BF), jax.ShapeDtypeStruct((T, D_MODEL), _BF)],
        compiler_params=_cparams(("parallel",)),
    )(y_lru, y_ret, y_na, x, w, g, b)


def _ffn_ln_call(x1, x1b, wg, wu, wd, g, b):
    T = x1.shape[0]

    def body(x_ref, xb_ref, wg_ref, wu_ref, wd_ref, g_ref, b_ref, z_ref, x2_ref, x2b_ref, acc_ref):
        n = pl.program_id(1)

        @pl.when(n == 0)
        def _():
            acc_ref[...] = jnp.zeros_like(acc_ref)

        xb = xb_ref[...]
        gp = _mm(xb, wg_ref[...])
        hid = gp * _sigmoid(gp) * _mm(xb, wu_ref[...])
        acc_ref[...] += _mm(hid, wd_ref[...])

        @pl.when(n == N_DEV - 1)
        def _():
            z = ALPHA * x_ref[...] + acc_ref[...]
            z_ref[...] = z
            x2 = _ln_fwd(z, g_ref[...], b_ref[...])
            x2_ref[...] = x2
            x2b_ref[...] = x2.astype(x2b_ref.dtype)

    row = pl.BlockSpec((TM, D_MODEL), lambda i, n: (i, 0))
    return pl.pallas_call(
        body, name="ffn_ln", grid=(T // TM, N_DEV),
        in_specs=[row, row,
                  pl.BlockSpec((None, D_MODEL, W_BLK), lambda i, n: (n, 0, 0)),
                  pl.BlockSpec((None, D_MODEL, W_BLK), lambda i, n: (n, 0, 0)),
                  pl.BlockSpec((None, W_BLK, D_MODEL), lambda i, n: (n, 0, 0)), _vec_spec(), _vec_spec()],
        out_specs=[row] * 3,
        out_shape=[jax.ShapeDtypeStruct((T, D_MODEL), F32), jax.ShapeDtypeStruct((T, D_MODEL), F32),
                   jax.ShapeDtypeStruct((T, D_MODEL), _BF)],
        scratch_shapes=[pltpu.VMEM((TM, D_MODEL), F32)],
        compiler_params=_cparams(("parallel", "arbitrary")),
    )(x1, x1b, wg, wu, wd, g, b)


def _loss_call(y, t):
    T = y.shape[0]

    def body(y_ref, t_ref, dy_ref, loss_ref):
        @pl.when(pl.program_id(0) == 0)
        def _():
            loss_ref[...] = jnp.zeros_like(loss_ref)

        err = y_ref[...] - t_ref[...]
        dy_ref[...] = err * (1.0 / D_MODEL)
        part = 0.5 * jnp.sum(jnp.mean(err * err, axis=-1, keepdims=True), axis=0, keepdims=True)
        loss_ref[...] += jnp.broadcast_to(part, loss_ref.shape)

    row = pl.BlockSpec((TM, D_MODEL), lambda i: (i, 0))
    return pl.pallas_call(
        body, name="loss", grid=(T // TM,),
        in_specs=[row, row],
        out_specs=[row, pl.BlockSpec((SUB, LANE), lambda i: (0, 0))],
        out_shape=[jax.ShapeDtypeStruct((T, D_MODEL), F32), jax.ShapeDtypeStruct((SUB, LANE), F32)],
        compiler_params=_cparams(("arbitrary",)),
    )(y, t)


def _ffn_bwd_call(dx2, z2, x1, x1b, wg, wu, wd, g, after):
    T = x1.shape[0]

    def body(dx2_ref, z_ref, x_ref, xb_ref, wg_ref, wu_ref, wd_ref, g_ref, after_ref,
             dx1_ref, dgp_ref, dup_ref, hid_ref, dzb_ref, dln_ref, acc_ref):
        i, n = pl.program_id(0), pl.program_id(1)

        @pl.when((i == 0) & (n == 0))
        def _():
            dln_ref[...] = jnp.zeros_like(dln_ref)

        @pl.when(n == 0)
        def _():
            dy = dx2_ref[...]
            dz, dg_rows = _ln_bwd(dy, z_ref[...], g_ref[...])
            dzb_ref[...] = dz.astype(dzb_ref.dtype)
            acc_ref[...] = ALPHA * dz
            dln_ref[0:1, :] += jnp.sum(dg_rows, axis=0, keepdims=True)
            dln_ref[1:2, :] += jnp.sum(dy, axis=0, keepdims=True)

        xb = xb_ref[...]
        gp = _mm(xb, wg_ref[...])
        up = _mm(xb, wu_ref[...])
        sg = _sigmoid(gp)
        act = gp * sg
        hid_ref[...] = (act * up).astype(hid_ref.dtype)
        dhid = _mm_nt(dzb_ref[...], wd_ref[...])
        dup = dhid * act
        dgp = dhid * up * (sg * (1.0 + gp * (1.0 - sg)))
        dgp_ref[...] = dgp.astype(dgp_ref.dtype)
        dup_ref[...] = dup.astype(dup_ref.dtype)
        acc_ref[...] += _mm_nt(dgp, wg_ref[...]) + _mm_nt(dup, wu_ref[...])

        @pl.when(n == N_DEV - 1)
        def _():
            dx1_ref[...] = acc_ref[...]

    row = pl.BlockSpec((TM, D_MODEL), lambda i, n: (i, 0))
    blk = pl.BlockSpec((None, TM, W_BLK), lambda i, n: (n, i, 0))
    return pl.pallas_call(
        body, name="ffn_bwd", grid=(T // TM, N_DEV),
        in_specs=[row, row, row, row,
                  pl.BlockSpec((None, D_MODEL, W_BLK), lambda i, n: (n, 0, 0)),
                  pl.BlockSpec((None, D_MODEL, W_BLK), lambda i, n: (n, 0, 0)),
                  pl.BlockSpec((None, W_BLK, D_MODEL), lambda i, n: (n, 0, 0)), _vec_spec(),
                  pl.BlockSpec(memory_space=pl.ANY)],
        out_specs=[row, blk, blk, pl.BlockSpec((TM, W_BLK), lambda i, n: (i, n)), row,
                   pl.BlockSpec((SUB, D_MODEL), lambda i, n: (0, 0))],
        out_shape=[jax.ShapeDtypeStruct((T, D_MODEL), F32),
                   jax.ShapeDtypeStruct((N_DEV, T, W_BLK), _BF), jax.ShapeDtypeStruct((N_DEV, T, W_BLK), _BF),
                   jax.ShapeDtypeStruct((T, N_DEV * W_BLK), _BF), jax.ShapeDtypeStruct((T, D_MODEL), _BF),
                   jax.ShapeDtypeStruct((SUB, D_MODEL), F32)],
        scratch_shapes=[pltpu.VMEM((TM, D_MODEL), F32)],
        compiler_params=_cparams(("arbitrary", "arbitrary")),
    )(dx2, z2, x1, x1b, wg, wu, wd, g, after)


def _outproj_bwd_call(dx1, z1, w, g):
    T = dx1.shape[0]

    def body(dx_ref, z_ref, w_ref, g_ref, dzb_ref, dyc_ref, dres_ref, dln_ref):
        @pl.when(pl.program_id(0) == 0)
        def _():
            dln_ref[...] = jnp.zeros_like(dln_ref)

        dy = dx_ref[...]
        dz, dg_rows = _ln_bwd(dy, z_ref[...], g_ref[...])
        dzb_ref[...] = dz.astype(dzb_ref.dtype)
        dres_ref[...] = ALPHA * dz
        dyc_ref[...] = _mm_nt(dz, w_ref[...].reshape(D_MODEL, D_MODEL))
        dln_ref[0:1, :] += jnp.sum(dg_rows, axis=0, keepdims=True)
        dln_ref[1:2, :] += jnp.sum(dy, axis=0, keepdims=True)

    row = pl.BlockSpec((TM, D_MODEL), lambda i: (i, 0))
    return pl.pallas_call(
        body, name="outproj_bwd", grid=(T // TM,),
        in_specs=[row, row, pl.BlockSpec((N_DEV, LANE, D_MODEL), lambda i: (0, 0, 0)), _vec_spec()],
        out_specs=[row, row, row, pl.BlockSpec((SUB, D_MODEL), lambda i: (0, 0))],
        out_shape=[jax.ShapeDtypeStruct((T, D_MODEL), _BF), jax.ShapeDtypeStruct((T, D_MODEL), F32),
                   jax.ShapeDtypeStruct((T, D_MODEL), F32), jax.ShapeDtypeStruct((SUB, D_MODEL), F32)],
        compiler_params=_cparams(("arbitrary",)),
    )(dx1, z1, w, g)


def _inproj_bwd_call(dres, dp, w):
    T = dres.shape[0]

    def body(dres_ref, dp_ref, w_ref, dx_ref):
        acc = dres_ref[...]
        for n in range(N_DEV):
            acc = acc + _mm_nt(dp_ref[n], w_ref[n])
        dx_ref[...] = acc

    row = pl.BlockSpec((TM, D_MODEL), lambda i: (i, 0))
    return pl.pallas_call(
        body, name="inproj_bwd", grid=(T // TM,),
        in_specs=[row, pl.BlockSpec((N_DEV, TM, W_BLK), lambda i: (0, i, 0)),
                  pl.BlockSpec((N_DEV, D_MODEL, W_BLK), lambda i: (0, 0, 0))],
        out_specs=row,
        out_shape=jax.ShapeDtypeStruct((T, D_MODEL), F32),
        compiler_params=_cparams(("parallel",)),
    )(dres, dp, w)


def _tn_cols_call(a, b3, name):
    T, ka = a.shape
    nblk, _, nb = b3.shape

    def body(a_ref, b_ref, o_ref):
        o_ref[...] = _mm_tn(a_ref[...], b_ref[...]).astype(o_ref.dtype)

    return pl.pallas_call(
        body, name=name, grid=(nblk,),
        in_specs=[pl.BlockSpec((T, ka), lambda n: (0, 0), pipeline_mode=pl.Buffered(1)),
                  pl.BlockSpec((None, T, nb), lambda n: (n, 0, 0))],
        out_specs=pl.BlockSpec((None, ka, nb), lambda n: (n, 0, 0)),
        out_shape=jax.ShapeDtypeStruct((nblk, ka, nb), _BF),
        compiler_params=_cparams(("parallel",)),
    )(a, b3)


def _tn_rows_call(a, b, kb, name):
    T, ka = a.shape
    n = b.shape[1]

    def body(a_ref, b_ref, o_ref):
        o_ref[...] = _mm_tn(a_ref[...], b_ref[...]).astype(o_ref.dtype)

    return pl.pallas_call(
        body, name=name, grid=(ka // kb,),
        in_specs=[pl.BlockSpec((T, kb), lambda r: (0, r)),
                  pl.BlockSpec((T, n), lambda r: (0, 0), pipeline_mode=pl.Buffered(1))],
        out_specs=pl.BlockSpec((None, kb, n), lambda r: (r, 0, 0)),
        out_shape=jax.ShapeDtypeStruct((ka // kb, kb, n), _BF),
        compiler_params=_cparams(("parallel",)),
    )(a, b)


def _me():
    return lax.axis_index("x"), lax.axis_index("y"), lax.axis_index("c")


def _flip(k):
    x, y, c = _me()
    return (1 - x if k & 4 else x, 1 - y if k & 2 else y, 1 - c if k & 1 else c)


def _dev_index(pos):
    return 4 * pos[0] + 2 * pos[1] + pos[2]


_HBM = pl.BlockSpec(memory_space=pltpu.HBM)
_SEM = pl.BlockSpec(memory_space=pltpu.SEMAPHORE)


def _comm_copies(ins, lands, gather_flags, send_sems, recv_sems):
    me = _dev_index(_me())
    copies = []
    for k in range(N_DEV):
        peer = _flip(k)
        for a, (src, land) in enumerate(zip(ins, lands)):
            copies.append(pltpu.make_async_remote_copy(
                src_ref=src if gather_flags[a] else src.at[_dev_index(peer)], dst_ref=land.at[me],
                send_sem=send_sems.at[k * len(ins) + a], recv_sem=recv_sems.at[k * len(ins) + a],
                device_id=peer, device_id_type=MESH))
    return copies


def _comm_start_call(arrs, gather_flags, name):
    n = len(arrs)
    lands = [lax.empty((N_DEV,) + (v.shape if gf else v.shape[1:]), v.dtype) for v, gf in zip(arrs, gather_flags)]

    def body(*refs):
        ins, lnd, send_sems, recv_sems = refs[:n], refs[n:2 * n], refs[2 * n], refs[2 * n + 1]
        for cp in _comm_copies(ins, lnd, gather_flags, send_sems, recv_sems):
            cp.start()
        refs[-1][...] = jnp.zeros_like(refs[-1])

    hbm = [pltpu.with_memory_space_constraint(v, pltpu.HBM) for v in list(arrs) + lands]
    out = pl.pallas_call(
        body, name=name,
        out_shape=(pltpu.SemaphoreType.DMA((N_DEV * n,)), pltpu.SemaphoreType.DMA((N_DEV * n,)),
                   *[pltpu.HBM(v.shape, v.dtype) for v in hbm], jax.ShapeDtypeStruct((SUB, LANE), F32)),
        in_specs=[_HBM] * (2 * n),
        out_specs=(_SEM, _SEM, *[_HBM] * (2 * n), pl.BlockSpec(memory_space=pltpu.VMEM)),
        input_output_aliases={i: 2 + i for i in range(2 * n)},
        compiler_params=pltpu.CompilerParams(has_side_effects=pltpu.SideEffectType.DATAFLOW_SIDE_EFFECTING),
    )(*hbm)
    return out[:-1], out[-1]


def _comm_wait_call(state, gather_flags, after, name):
    n = len(gather_flags)
    send_sems, recv_sems, thru = state[0], state[1], state[2:]

    def body(*refs):
        ins, lnd, ssem, rsem = refs[:n], refs[n:2 * n], refs[2 * n], refs[2 * n + 1]
        for cp in _comm_copies(ins, lnd, gather_flags, ssem, rsem):
            cp.wait_send()
            cp.wait_recv()

    out = pl.pallas_call(
        body, name=name,
        out_shape=tuple(pltpu.HBM(v.shape, v.dtype) for v in thru),
        in_specs=[_HBM] * (2 * n) + [_SEM, _SEM] + [pl.BlockSpec(memory_space=pl.ANY)] * len(after),
        out_specs=tuple([_HBM] * (2 * n)),
        input_output_aliases={i: i for i in range(2 * n)},
        compiler_params=pltpu.CompilerParams(has_side_effects=pltpu.SideEffectType.DATAFLOW_SIDE_EFFECTING),
    )(*thru, send_sems, recv_sems, *after)
    return out[n:]


def _sum8_call(recv, rows, name):
    _, r, c = recv.shape

    def body(x_ref, o_ref):
        acc = x_ref[0].astype(F32)
        for s in range(1, N_DEV):
            acc = acc + x_ref[s].astype(F32)
        o_ref[...] = acc

    return pl.pallas_call(
        body, name=name, grid=(r // rows,),
        in_specs=[pl.BlockSpec((N_DEV, rows, c), lambda i: (0, i, 0))],
        out_specs=pl.BlockSpec((rows, c), lambda i: (i, 0)),
        out_shape=jax.ShapeDtypeStruct((r, c), F32),
        compiler_params=_cparams(("parallel",)),
    )(recv)


def _adamw_call(w, g, m, v, rows, name):
    r, c = w.shape

    def body(w_ref, g_ref, m_ref, v_ref, d_ref, nm_ref, nv_ref):
        gr = g_ref[...]
        nm = ADAM_B1 * m_ref[...] + (1.0 - ADAM_B1) * gr
        nv = ADAM_B2 * v_ref[...] + (1.0 - ADAM_B2) * (gr * gr)
        m_hat = nm / (1.0 - ADAM_B1 ** ADAM_STEP)
        v_hat = nv / (1.0 - ADAM_B2 ** ADAM_STEP)
        d_ref[...] = -ADAM_LR * (m_hat / (jnp.sqrt(v_hat) + ADAM_EPS) + ADAM_WD * w_ref[...])
        nm_ref[...] = nm
        nv_ref[...] = nv

    spec = pl.BlockSpec((rows, c), lambda i: (i, 0))
    return pl.pallas_call(
        body, name=name, grid=(r // rows,),
        in_specs=[spec] * 4, out_specs=[spec] * 3,
        out_shape=[jax.ShapeDtypeStruct((r, c), F32)] * 3,
        compiler_params=_cparams(("parallel",)),
    )(w, g, m, v)


SH_ROWS = 16
SH_W = LRU_WIDTH // N_DEV
REP_ROWS = 824
_REP_SIZES = (LRU_WIDTH, 2 * 6 * 64 * 64, 2 * 6 * 64 * 64, RET_WIDTH, 1920, D_MODEL, D_MODEL, D_MODEL, D_MODEL)
_RPB_SIZE = NA_HEADS * (2 * NA_KH - 1) * (2 * NA_KW - 1)


def _pack_sh(cw, ba, bx, lam):
    return jnp.concatenate([cw, ba, bx, lam], axis=0)


def _pad_sh(p):
    pad = [(0, 0)] * (p.ndim - 2) + [(0, SH_ROWS - p.shape[-2]), (0, LANE - p.shape[-1])]
    return jnp.pad(p, pad)


def _pack_rep(cb, wa, wx, gnw, rpb, l1g, l1b, l2g, l2b):
    flat = jnp.concatenate([cb.reshape(-1), wa.reshape(-1), wx.reshape(-1), gnw.reshape(-1),
                            jnp.pad(rpb.reshape(-1), (0, 1920 - _RPB_SIZE)), l1g, l1b, l2g, l2b,
                            jnp.zeros((REP_ROWS * LANE - sum(_REP_SIZES),), F32)])
    return flat.reshape(REP_ROWS, LANE)


def _unpack_rep(p):
    nl = p.shape[0]
    flat = p.reshape(nl, -1)
    out, off = [], 0
    for size in _REP_SIZES:
        out.append(flat[:, off:off + size])
        off += size
    cb, wa, wx, gnw, rpb, l1g, l1b, l2g, l2b = out
    return (cb, wa.reshape(nl, 2, 6, 64, 64), wx.reshape(nl, 2, 6, 64, 64), gnw,
            rpb[:, :_RPB_SIZE].reshape(nl, NA_HEADS, 2 * NA_KH - 1, 2 * NA_KW - 1), l1g, l1b, l2g, l2b)


def _adamw_nd(w, g, m, v, rows, name):
    shp = w.shape
    f = lambda t: t.reshape(-1, shp[-1])
    return [t.reshape(shp) for t in _adamw_call(f(w), f(g), f(m), f(v), rows, name)]


def kernel(x, w_in, conv_w, conv_b, lru_w_a, lru_b_a, lru_w_x, lru_b_x, lru_lam, ret_gn_w, na_rpb, w_out, ln1_g, ln1_b, w_gate, w_up, w_down, ln2_g, ln2_b, loss_target, m_w_in, m_conv_w, m_conv_b, m_lru_w_a, m_lru_b_a, m_lru_w_x, m_lru_b_x, m_lru_lam, m_ret_gn_w, m_na_rpb, m_w_out, m_ln1_g, m_ln1_b, m_w_gate, m_w_up, m_w_down, m_ln2_g, m_ln2_b, v_w_in, v_conv_w, v_conv_b, v_lru_w_a, v_lru_b_a, v_lru_w_x, v_lru_b_x, v_lru_lam, v_ret_gn_w, v_na_rpb, v_w_out, v_ln1_g, v_ln1_b, v_w_gate, v_w_up, v_w_down, v_ln2_g, v_ln2_b):
    nl = w_in.shape[0]
    T = x.shape[1]
    rows_n = T // GRID_W
    x0, target = x[0], loss_target[0]
    ffpad = W_BLK - FF_BLK

    win_b = w_in.astype(_BF)
    wg_b = jnp.pad(w_gate, ((0, 0), (0, 0), (0, ffpad))).astype(_BF)
    wu_b = jnp.pad(w_up, ((0, 0), (0, 0), (0, ffpad))).astype(_BF)
    wd_b = jnp.pad(w_down, ((0, 0), (0, ffpad), (0, 0))).astype(_BF)
    wout_b = w_out.astype(_BF)
    ag_flags = [True] * 6
    ag_state, ag_tokens = [], []
    for l in range(nl):
        sh = _pad_sh(_pack_sh(conv_w[l], lru_b_a[l], lru_b_x[l], lru_lam[l]))
        state, token = _comm_start_call([win_b[l], wg_b[l], wu_b[l], wd_b[l], wout_b[l], sh], ag_flags, f"ag_start{l}")
        ag_state.append(state)
        ag_tokens.append(token)

    tables = _ret_tables(T)
    layers = []
    gathered = []
    xs, xb = x0, x0.astype(_BF)
    for l in range(nl):
        gathered.append(_comm_wait_call(ag_state[l], ag_flags, ag_tokens[1:] if l == 0 else [xb], f"ag_wait{l}"))
        win, wg, wu, wd, wout, shg = gathered[l]
        full = shg[:, :10, :SH_W].transpose(1, 0, 2).reshape(10, LRU_WIDTH)
        vec, w4 = _lru_pack(full[0:4], conv_b[l], lru_w_a[l], full[4:6], lru_w_x[l], full[6:8], full[8:10])
        gnw8 = jnp.pad(ret_gn_w[l][None], ((0, SUB - 1), (0, 0)))
        btab, btab_t = _na_bias_tables(na_rpb[l], rows_n)
        proj = _inproj_call(xb, win)
        y_lru = _lru_fwd_call(proj, vec, w4)
        y_ret = _ret_fwd_call(proj, tables, gnw8)
        y_na = _na_fwd_call(proj, btab)
        z1, x1, x1b, ycb = _outproj_ln_call(y_lru, y_ret, y_na, xs, wout, ln1_g[l][None], ln1_b[l][None])
        z2, x2, x2b = _ffn_ln_call(x1, x1b, wg, wu, wd, ln2_g[l][None], ln2_b[l][None])
        layers.append(dict(xb=xb, proj=proj, vec=vec, w4=w4, gnw8=gnw8, btab=btab, btab_t=btab_t,
                           z1=z1, x1=x1, x1b=x1b, ycb=ycb, z2=z2))
        xs, xb = x2, x2b

    dx, loss_blk = _loss_call(xs, target)
    loss = lax.psum(loss_blk[0, 0], ("x", "y", "c"))

    gx_flags = [False] * 6 + [True]
    gx_state = [None] * nl
    token = loss_blk
    for l in reversed(range(nl)):
        s = layers[l]
        win, wg, wu, wd, wout, _ = gathered[l]
        dx1, dgp, dup, hid, dz2b, dln2 = _ffn_bwd_call(dx, s["z2"], s["x1"], s["x1b"], wg, wu, wd, ln2_g[l][None], token)
        dwg = _tn_cols_call(s["x1b"], dgp, "tn_cols")
        dwu = _tn_cols_call(s["x1b"], dup, "tn_cols")
        dwd = _tn_rows_call(hid, dz2b, W_BLK, "tn_rows_down")
        dz1b, dyc, dres, dln1 = _outproj_bwd_call(dx1, s["z1"], wout, ln1_g[l][None])
        dwout = _tn_rows_call(s["ycb"], dz1b, LANE, "tn_rows_out")
        dp_lru, dvec, dw4 = _lru_bwd_call(s["proj"], dyc, s["vec"], s["w4"])
        dp_ret, dgnw = _ret_bwd_call(s["proj"], dyc, tables, s["gnw8"])
        dp_na, dbias = _na_bwd_call(s["proj"], dyc, s["btab"], s["btab_t"])
        dp = jnp.concatenate([dp_lru, dp_ret, dp_na], axis=0)
        dwin = _tn_cols_call(s["xb"], dp, "tn_cols")
        dx = _inproj_bwd_call(dres, dp, win)
        dcw, dcb, dwa, dba, dwx, dbx, dlam = _lru_unpack(dvec, dw4)
        rep = _pack_rep(dcb, dwa, dwx, dgnw[0], _na_bias_grad(dbias, rows_n), dln1[0], dln1[1], dln2[0], dln2[1])
        sh = _pack_sh(dcw, dba, dbx, dlam).reshape(10, N_DEV, SH_W).transpose(1, 0, 2)
        gx_state[l], token = _comm_start_call([dwin, dwg, dwu, dwd, dwout, _pad_sh(sh), rep], gx_flags, f"gx_start{l}")

    g_big = [[None] * nl for _ in range(5)]
    g_sh = [None] * nl
    g_rep = [None] * nl
    after = [dx, token]
    for l in reversed(range(nl)):
        recv = _comm_wait_call(gx_state[l], gx_flags, after, f"gx_wait{l}")
        after = [recv[6]]
        g_big[0][l] = _sum8_call(recv[0], TM, "sum8_cols")
        g_big[1][l] = _sum8_call(recv[1], TM, "sum8_cols")[:, :FF_BLK]
        g_big[2][l] = _sum8_call(recv[2], TM, "sum8_cols")[:, :FF_BLK]
        g_big[3][l] = _sum8_call(recv[3], W_BLK, "sum8_down")[:FF_BLK]
        g_big[4][l] = _sum8_call(recv[4], LANE, "sum8_out")
        g_sh[l] = _sum8_call(recv[5], SH_ROWS, "sum8_sh")
        g_rep[l] = _sum8_call(recv[6], REP_ROWS, "sum8_rep")

    g_w_in, g_w_gate, g_w_up, g_w_down, g_w_out = [jnp.stack(t) for t in g_big]
    big = {
        "w_in": _adamw_nd(w_in, g_w_in, m_w_in, v_w_in, TM, "adamw_in"),
        "w_gate": _adamw_nd(w_gate, g_w_gate, m_w_gate, v_w_gate, TM, "adamw_ff"),
        "w_up": _adamw_nd(w_up, g_w_up, m_w_up, v_w_up, TM, "adamw_ff"),
        "w_down": _adamw_nd(w_down, g_w_down, m_w_down, v_w_down, FF_BLK, "adamw_down"),
        "w_out": _adamw_nd(w_out, g_w_out, m_w_out, v_w_out, LANE, "adamw_out"),
    }
    g_shp = jnp.stack(g_sh)
    pack_sh = lambda cw, ba, bx, lam: _pad_sh(jnp.concatenate([cw, ba, bx, lam], axis=1))
    sh_out = _adamw_nd(pack_sh(conv_w, lru_b_a, lru_b_x, lru_lam), g_shp,
                       pack_sh(m_conv_w, m_lru_b_a, m_lru_b_x, m_lru_lam),
                       pack_sh(v_conv_w, v_lru_b_a, v_lru_b_x, v_lru_lam), SH_ROWS, "adamw_sh")

    def split_sh(p):
        p = p[:, :, :SH_W]
        return {"conv_w": p[:, 0:4], "lru_b_a": p[:, 4:6], "lru_b_x": p[:, 6:8], "lru_lam": p[:, 8:10]}

    g_repp = jnp.stack(g_rep)
    pack_rep = lambda *ps: jnp.stack([_pack_rep(*[p[l] for p in ps]) for l in range(nl)])
    rep_names = ("conv_b", "lru_w_a", "lru_w_x", "ret_gn_w", "na_rpb", "ln1_g", "ln1_b", "ln2_g", "ln2_b")
    rep_out = _adamw_nd(pack_rep(conv_b, lru_w_a, lru_w_x, ret_gn_w, na_rpb, ln1_g, ln1_b, ln2_g, ln2_b), g_repp,
                        pack_rep(m_conv_b, m_lru_w_a, m_lru_w_x, m_ret_gn_w, m_na_rpb, m_ln1_g, m_ln1_b, m_ln2_g, m_ln2_b),
                        pack_rep(v_conv_b, v_lru_w_a, v_lru_w_x, v_ret_gn_w, v_na_rpb, v_ln1_g, v_ln1_b, v_ln2_g, v_ln2_b),
                        REP_ROWS, "adamw_rep")

    grads = {"w_in": g_w_in, "w_gate": g_w_gate, "w_up": g_w_up, "w_down": g_w_down, "w_out": g_w_out}
    grads.update(split_sh(g_shp))
    grads.update(dict(zip(rep_names, _unpack_rep(g_repp))))
    kinds = []
    for k in range(3):
        d = {n: big[n][k] for n in big}
        d.update(split_sh(sh_out[k]))
        d.update(dict(zip(rep_names, _unpack_rep(rep_out[k]))))
        kinds.append(d)
    order = ("w_in", "conv_w", "conv_b", "lru_w_a", "lru_b_a", "lru_w_x", "lru_b_x", "lru_lam", "ret_gn_w", "na_rpb",
             "w_out", "ln1_g", "ln1_b", "w_gate", "w_up", "w_down", "ln2_g", "ln2_b")
    outs = [loss, dx[None]]
    for d in (grads, *kinds):
        outs.extend(d[n] for n in order)
    return tuple(outs)
```

```python
import functools
import math

import numpy as np
import jax
import jax.numpy as jnp
from jax import lax
from jax.experimental import pallas as pl
from jax.experimental.pallas import tpu as pltpu

F32 = jnp.float32
_BF = jnp.bfloat16

D_MODEL = 1024
DEPTH = 4
GRID_W = 64
HEAD_DIM = 64
LRU_WIDTH = 384
RET_WIDTH = 384
RET_HEADS = 6
NA_WIDTH = 256
NA_HEADS = 4
IN_WIDTH = 3072
CONV_WIDTH = 4
LRU_C = 8.0
RET_CHUNK = 128
ROPE_BASE = 10000.0
GN_EPS = 1e-6
NA_KH = 8
NA_KW = 16
D_FF = 2816
FF_BLK = 352
N_DEV = 8
ALPHA = (2 * DEPTH) ** 0.25
LN_EPS = 1e-5
ADAM_LR = 0.001
ADAM_B1 = 0.9
ADAM_B2 = 0.999
ADAM_EPS = 1e-08
ADAM_WD = 0.01
ADAM_STEP = 10

LANE = 128
SUB = 8
VMEM_MB = 56
NEG = -1e30

MESH = pl.DeviceIdType.MESH


def _cparams(sem=None, vmem_mb=VMEM_MB):
    return pltpu.CompilerParams(dimension_semantics=sem, vmem_limit_bytes=vmem_mb << 20)


def _mm(a, b):
    return jnp.dot(a.astype(_BF), b.astype(_BF), preferred_element_type=F32)


def _mm_nt(a, b):
    return lax.dot_general(a.astype(_BF), b.astype(_BF), (((1,), (1,)), ((), ())), preferred_element_type=F32)


def _mm_tn(a, b):
    return lax.dot_general(a.astype(_BF), b.astype(_BF), (((0,), (0,)), ((), ())), preferred_element_type=F32)


def _sigmoid(x):
    return jax.nn.sigmoid(x)


def _rows(start, size):
    return pl.ds(pl.multiple_of(start, SUB), size)


def _strip(T, col):
    return pl.BlockSpec((T, LANE), lambda j: (0, col(j)), pipeline_mode=pl.Buffered(1))


LRU_CH = 256
_GELU_C0 = math.sqrt(2.0 / math.pi)
_GELU_C1 = 0.044715


def _gelu_parts(x):
    x2 = x * x
    t = jnp.tanh(_GELU_C0 * (x + _GELU_C1 * x * x2))
    val = 0.5 * x * (1.0 + t)
    der = 0.5 * (1.0 + t) + 0.5 * x * (1.0 - t * t) * _GELU_C0 * (1.0 + 3.0 * _GELU_C1 * x2)
    return val, der


def _softplus_neg(lam):
    e = jnp.exp(-jnp.abs(lam))
    w = 1.0 + e
    l1p = jnp.where(w == 1.0, e, jnp.log(w) * (e / jnp.where(w == 1.0, 1.0, w - 1.0)))
    return jnp.maximum(-lam, 0.0) + l1p


def _window(ref, t0, ch, T):
    prev = ref[_rows(jnp.maximum(t0 - SUB, 0), SUB), :].astype(F32)
    nxt = ref[_rows(jnp.minimum(t0 + ch, T - SUB), SUB), :].astype(F32)
    prev = jnp.where(t0 > 0, prev, 0.0)
    nxt = jnp.where(t0 + ch < T, nxt, 0.0)
    return jnp.concatenate([prev, ref[_rows(t0, ch), :].astype(F32), nxt], axis=0)


def _tap(win, shift, ch):
    n = win.shape[0]
    return pltpu.roll(win, (-shift) % n, 0)[SUB:SUB + ch]


def _lru_conv(xb_ref, vec, t0, T):
    win = _window(xb_ref, t0, LRU_CH, T)
    xc = jnp.broadcast_to(vec[4:5, :], (LRU_CH, LANE))
    for j in range(CONV_WIDTH):
        xc = xc + _tap(win, j - CONV_WIDTH // 2, LRU_CH) * vec[j:j + 1, :]
    return xc


def _lru_dir(pre_a, pre_x, sp):
    r = _sigmoid(pre_a)
    i = _sigmoid(pre_x)
    log_a = (-LRU_C) * r * sp
    a = jnp.exp(log_a)
    z = jnp.tanh(-log_a) * (a * a + 1.0)
    s = jnp.sqrt(z)
    return r, i, a, s


def _scan_tile(a, b, reverse, row):
    for k in (1, 2, 4):
        if not reverse:
            a_s, b_s, m = pltpu.roll(a, k, 0), pltpu.roll(b, k, 0), row >= k
        else:
            a_s, b_s, m = pltpu.roll(a, SUB - k, 0), pltpu.roll(b, SUB - k, 0), row < SUB - k
        b = jnp.where(m, a * b_s + b, b)
        a = jnp.where(m, a * a_s, a)
    return a, b


def _bcast_row(x, r):
    return jnp.broadcast_to(x[r:r + 1, :], (SUB, LANE))


def _lru_prepare(xb_ref, w4_ref, vec, xc_ref, af_ref, uf_ref, ab_ref, ub_ref, T):
    sp_f = _softplus_neg(vec[9:10, :])
    sp_b = _softplus_neg(vec[10:11, :])
    w4 = w4_ref[0]

    def body(c, carry):
        t0 = c * LRU_CH
        xc = _lru_conv(xb_ref, vec, t0, T)
        if xc_ref is not None:
            xc_ref[_rows(t0, LRU_CH), :] = xc
        pre = _mm(xc, w4)
        _, i, a, s = _lru_dir(pre[:, 0:128] + vec[5:6, :], pre[:, 128:256] + vec[6:7, :], sp_f)
        af_ref[_rows(t0, LRU_CH), :] = a
        uf_ref[_rows(t0, LRU_CH), :] = s * (i * xc)
        _, i, a, s = _lru_dir(pre[:, 256:384] + vec[7:8, :], pre[:, 384:512] + vec[8:9, :], sp_b)
        ab_ref[_rows(t0, LRU_CH), :] = a
        ub_ref[_rows(t0, LRU_CH), :] = s * (i * xc)
        return carry

    lax.fori_loop(0, T // LRU_CH, body, 0)


def _lru_scan(af_ref, uf_ref, ab_ref, ub_ref, T):
    nt = T // SUB
    row = lax.broadcasted_iota(jnp.int32, (SUB, LANE), 0)

    def body(j, carry):
        hf, hb = carry
        sf = _rows(j * SUB, SUB)
        sb = _rows((nt - 1 - j) * SUB, SUB)
        a, b = _scan_tile(af_ref[sf, :], uf_ref[sf, :], False, row)
        h = a * hf + b
        uf_ref[sf, :] = h
        hf = _bcast_row(h, SUB - 1)
        a, b = _scan_tile(ab_ref[sb, :], ub_ref[sb, :], True, row)
        h = a * hb + b
        ub_ref[sb, :] = h
        hb = _bcast_row(h, 0)
        return hf, hb

    z = jnp.zeros((SUB, LANE), F32)
    lax.fori_loop(0, nt, body, (z, z), unroll=4)


def _lru_fwd_call(proj, vec, w4):
    T = proj.shape[0]

    def body(xb_ref, gate_ref, vec_ref, w4_ref, y_ref, af_ref, uf_ref, ab_ref, ub_ref):
        vec = vec_ref[...]
        _lru_prepare(xb_ref, w4_ref, vec, None, af_ref, uf_ref, ab_ref, ub_ref, T)
        _lru_scan(af_ref, uf_ref, ab_ref, ub_ref, T)

        def out(c, carry):
            rows = _rows(c * LRU_CH, LRU_CH)
            gl, _ = _gelu_parts(gate_ref[rows, :])
            y_ref[rows, :] = (uf_ref[rows, :] + ub_ref[rows, :]) * gl
            return carry

        lax.fori_loop(0, T // LRU_CH, out, 0)

    return pl.pallas_call(
        body, name="lru_fwd", grid=(LRU_WIDTH // LANE,),
        in_specs=[_strip(T, lambda j: j), _strip(T, lambda j: j + 3),
                  pl.BlockSpec((16, LANE), lambda j: (0, j)),
                  pl.BlockSpec((1, LANE, 4 * LANE), lambda j: (j, 0, 0))],
        out_specs=_strip(T, lambda j: j),
        out_shape=jax.ShapeDtypeStruct((T, LRU_WIDTH), F32),
        scratch_shapes=[pltpu.VMEM((T, LANE), F32)] * 4,
        compiler_params=_cparams(("arbitrary",)),
    )(proj, proj, vec, w4)


def _lru_bwd_call(proj, dycat, vec, w4, after):
    T = proj.shape[0]
    nt = T // SUB
    nch = T // LRU_CH

    def body(xb_ref, gate_ref, dy_ref, vec_ref, w4_ref, after_ref, dp_ref, dvec_ref, dw4_ref,
             xc_ref, af_ref, hf_ref, ab_ref, hb_ref, dh_ref):
        dxb_ref, dgate_ref = dp_ref.at[0], dp_ref.at[1]
        vec = vec_ref[...]
        _lru_prepare(xb_ref, w4_ref, vec, xc_ref, af_ref, hf_ref, ab_ref, hb_ref, T)
        _lru_scan(af_ref, hf_ref, ab_ref, hb_ref, T)

        def gate_bwd(c, carry):
            rows = _rows(c * LRU_CH, LRU_CH)
            gl, dgl = _gelu_parts(gate_ref[rows, :])
            dy = dy_ref[rows, :]
            dgate_ref[rows, :] = (dy * (hf_ref[rows, :] + hb_ref[rows, :]) * dgl).astype(dgate_ref.dtype)
            dh_ref[rows, :] = dy * gl
            return carry

        lax.fori_loop(0, nch, gate_bwd, 0)

        row = lax.broadcasted_iota(jnp.int32, (SUB, LANE), 0)

        def adj(j, carry):
            gf, a_next, gb, a_prev = carry
            tf = nt - 1 - j
            sf = _rows(tf * SUB, SUB)
            a_t = af_ref[sf, :]
            h_t = hf_ref[sf, :]
            coef = jnp.where(row == SUB - 1, a_next, pltpu.roll(a_t, SUB - 1, 0))
            ac, bc = _scan_tile(coef, dh_ref[sf, :], True, row)
            g = ac * gf + bc
            h_prev = hf_ref[_rows(jnp.maximum(tf - 1, 0) * SUB, SUB), :]
            h_prev = jnp.where(tf > 0, _bcast_row(h_prev, SUB - 1), 0.0)
            hs = jnp.where(row == 0, h_prev, pltpu.roll(h_t, 1, 0))
            af_ref[sf, :] = g * hs
            hf_ref[sf, :] = g
            gf = _bcast_row(g, 0)
            a_next = _bcast_row(a_t, 0)
            sb = _rows(j * SUB, SUB)
            a_t = ab_ref[sb, :]
            h_t = hb_ref[sb, :]
            coef = jnp.where(row == 0, a_prev, pltpu.roll(a_t, 1, 0))
            ac, bc = _scan_tile(coef, dh_ref[sb, :], False, row)
            g = ac * gb + bc
            h_next = hb_ref[_rows(jnp.minimum(j + 1, nt - 1) * SUB, SUB), :]
            h_next = jnp.where(j < nt - 1, _bcast_row(h_next, 0), 0.0)
            hs = jnp.where(row == SUB - 1, h_next, pltpu.roll(h_t, SUB - 1, 0))
            ab_ref[sb, :] = g * hs
            hb_ref[sb, :] = g
            gb = _bcast_row(g, SUB - 1)
            a_prev = _bcast_row(a_t, SUB - 1)
            return gf, a_next, gb, a_prev

        z = jnp.zeros((SUB, LANE), F32)
        lax.fori_loop(0, nt, adj, (z, z, z, z), unroll=2)

        sp_f = _softplus_neg(vec[9:10, :])
        sp_b = _softplus_neg(vec[10:11, :])
        w4 = w4_ref[0]
        dw4_ref[...] = jnp.zeros_like(dw4_ref)

        def one_dir(pre_a, pre_x, sp, xc, du, da):
            r, i, a, s = _lru_dir(pre_a, pre_x, sp)
            d_i = du * s * xc
            dxc = du * s * i
            d_s = du * i * xc
            d_log = da * a - d_s * (a * a) / s
            d_r = d_log * (-LRU_C) * sp
            d_sp = jnp.sum(d_log * (-LRU_C) * r, axis=0, keepdims=True)
            return d_r * r * (1.0 - r), d_i * i * (1.0 - i), dxc, d_sp

        def gates_bwd(c, carry):
            db, dspf, dspb = carry
            rows = _rows(c * LRU_CH, LRU_CH)
            xc = xc_ref[rows, :]
            pre = _mm(xc, w4)
            dpa_f, dpx_f, dxc_f, d_sp_f = one_dir(pre[:, 0:128] + vec[5:6, :], pre[:, 128:256] + vec[6:7, :],
                                                  sp_f, xc, hf_ref[rows, :], af_ref[rows, :])
            dpa_b, dpx_b, dxc_b, d_sp_b = one_dir(pre[:, 256:384] + vec[7:8, :], pre[:, 384:512] + vec[8:9, :],
                                                  sp_b, xc, hb_ref[rows, :], ab_ref[rows, :])
            dpre = jnp.concatenate([dpa_f, dpx_f, dpa_b, dpx_b], axis=1)
            dw4_ref[0] += _mm_tn(xc, dpre)
            dh_ref[rows, :] = dxc_f + dxc_b + _mm_nt(dpre, w4)
            return db + jnp.sum(dpre, axis=0, keepdims=True), dspf + d_sp_f, dspb + d_sp_b

        z1 = jnp.zeros((1, LANE), F32)
        db, dspf, dspb = lax.fori_loop(0, nch, gates_bwd, (jnp.zeros((1, 4 * LANE), F32), z1, z1))

        def conv_bwd(c, carry):
            t0 = c * LRU_CH
            rows = _rows(t0, LRU_CH)
            dwin = _window(dh_ref, t0, LRU_CH, T)
            xwin = _window(xb_ref, t0, LRU_CH, T)
            dxc = dh_ref[rows, :]
            dxb = jnp.zeros((LRU_CH, LANE), F32)
            out = []
            for j in range(CONV_WIDTH):
                off = j - CONV_WIDTH // 2
                dxb = dxb + _tap(dwin, -off, LRU_CH) * vec[j:j + 1, :]
                out.append(carry[j] + jnp.sum(dxc * _tap(xwin, off, LRU_CH), axis=0, keepdims=True))
            dxb_ref[rows, :] = dxb.astype(dxb_ref.dtype)
            out.append(carry[CONV_WIDTH] + jnp.sum(dxc, axis=0, keepdims=True))
            return tuple(out)

        dconv = lax.fori_loop(0, nch, conv_bwd, (z1,) * (CONV_WIDTH + 1))
        dlam_f = dspf * (-_sigmoid(-vec[9:10, :]))
        dlam_b = dspb * (-_sigmoid(-vec[10:11, :]))
        dvec_ref[...] = jnp.concatenate(
            list(dconv) + [db[:, 0:128], db[:, 128:256], db[:, 256:384], db[:, 384:512], dlam_f, dlam_b,
                           jnp.zeros((5, LANE), F32)], axis=0)

    ns = LRU_WIDTH // LANE
    return pl.pallas_call(
        body, name="lru_bwd", grid=(ns,),
        in_specs=[_strip(T, lambda j: j), _strip(T, lambda j: j + 3), _strip(T, lambda j: j),
                  pl.BlockSpec((16, LANE), lambda j: (0, j)),
                  pl.BlockSpec((1, LANE, 4 * LANE), lambda j: (j, 0, 0)),
                  pl.BlockSpec(memory_space=pl.ANY)],
        out_specs=[pl.BlockSpec((2, T, LANE), lambda j: (0, 0, j), pipeline_mode=pl.Buffered(1)),
                   pl.BlockSpec((16, LANE), lambda j: (0, j)),
                   pl.BlockSpec((1, LANE, 4 * LANE), lambda j: (j, 0, 0))],
        out_shape=[jax.ShapeDtypeStruct((2, T, LRU_WIDTH), _BF),
                   jax.ShapeDtypeStruct((16, LRU_WIDTH), F32), jax.ShapeDtypeStruct((ns, LANE, 4 * LANE), F32)],
        scratch_shapes=[pltpu.VMEM((T, LANE), F32)] * 6,
        compiler_params=_cparams(("arbitrary",)),
    )(proj, proj, dycat, vec, w4, after)


def _lru_pack(cw, cb, wa, ba, wx, bx, lam):
    vec = jnp.concatenate([cw, cb[None], ba[0:1], bx[0:1], ba[1:2], bx[1:2], lam, jnp.zeros((5, LRU_WIDTH), F32)], axis=0)
    eye = jnp.eye(2, dtype=F32)
    mats = []
    for w in (wa[0], wx[0], wa[1], wx[1]):
        bd = jnp.einsum("jsio,st->jsito", w.reshape(3, 2, 64, 64), eye)
        mats.append(bd.reshape(3, LANE, LANE))
    return vec, jnp.concatenate(mats, axis=2).astype(_BF)


def _lru_unpack(dvec, dw4):
    def blocks(m):
        m = m.reshape(3, 2, 64, 2, 64)
        return jnp.stack([m[:, 0, :, 0, :], m[:, 1, :, 1, :]], axis=1).reshape(6, 64, 64)
    parts = [blocks(dw4[:, :, k * LANE:(k + 1) * LANE]) for k in range(4)]
    dwa = jnp.stack([parts[0], parts[2]])
    dwx = jnp.stack([parts[1], parts[3]])
    dba = jnp.stack([dvec[5], dvec[7]])
    dbx = jnp.stack([dvec[6], dvec[8]])
    return dvec[0:4], dvec[4], dwa, dba, dwx, dbx, dvec[9:11]


RC = RET_CHUNK


def _ret_tables(T):
    half = HEAD_DIM // 2
    pos = jnp.arange(T, dtype=F32)
    inv_freq = ROPE_BASE ** (-jnp.arange(half, dtype=F32) / half)
    ang = pos[:, None] * inv_freq[None, :]
    cos = jnp.tile(jnp.cos(ang), (1, 4))
    sin = jnp.tile(jnp.concatenate([-jnp.sin(ang), jnp.sin(ang)], axis=1), (1, 2))
    log_g = jnp.log1p(-jnp.exp2(-5.0 - jnp.arange(RET_HEADS, dtype=F32)))
    idx = jnp.arange(RC, dtype=F32)
    dec = jnp.exp(jnp.abs(idx[:, None] - idx[None, :]) * log_g[:, None, None])
    lg = jnp.repeat(log_g, HEAD_DIM).reshape(3, 1, LANE)
    col = idx[None, :, None]
    rtab = jnp.stack([jnp.exp((RC - 1 - col) * lg), jnp.exp(col * lg),
                      jnp.exp((col + 1.0) * lg), jnp.exp((RC - col) * lg)], axis=1)
    gch = jnp.broadcast_to(jnp.exp(RC * lg), (3, SUB, LANE))
    return cos, sin, dec, rtab, gch


def _swap32(x, lane):
    return jnp.where((lane & 32) == 0, pltpu.roll(x, LANE - 32, 1), pltpu.roll(x, 32, 1))


def _head_mean(x, m0, m1):
    s0 = jnp.sum(x * m0, axis=-1, keepdims=True)
    s1 = jnp.sum(x * m1, axis=-1, keepdims=True)
    return (s0 * m0 + s1 * m1) * (1.0 / HEAD_DIM)


def _ret_masks():
    lane = lax.broadcasted_iota(jnp.int32, (RC, LANE), 1)
    m0 = (lane < HEAD_DIM).astype(F32)
    r = lax.broadcasted_iota(jnp.int32, (LANE, LANE), 0) // HEAD_DIM
    c = lax.broadcasted_iota(jnp.int32, (LANE, LANE), 1) // HEAD_DIM
    return lane, m0, 1.0 - m0, (r == c).astype(F32)


def _ret_specs(T):
    const = lambda shape, imap: pl.BlockSpec(shape, imap)
    return [_strip(T, lambda j: j + 6), _strip(T, lambda j: j + 9), _strip(T, lambda j: j + 12),
            _strip(T, lambda j: j + 15),
            pl.BlockSpec((T, LANE), lambda j: (0, 0), pipeline_mode=pl.Buffered(1)),
            pl.BlockSpec((T, LANE), lambda j: (0, 0), pipeline_mode=pl.Buffered(1)),
            const((2, RC, RC), lambda j: (j, 0, 0)),
            const((1, 4, RC, LANE), lambda j: (j, 0, 0, 0)),
            const((1, SUB, LANE), lambda j: (j, 0, 0)),
            const((SUB, LANE), lambda j: (0, j))]


def _ret_fwd_call(proj, tables, gnw8):
    T = proj.shape[0]
    nc = T // RC
    cos, sin, dec, rtab, gch = tables

    def body(q_ref, k_ref, v_ref, g_ref, cos_ref, sin_ref, dec_ref, rtab_ref, gch_ref, gnw_ref, y_ref, stf_ref):
        lane, m0, m1, bd = _ret_masks()
        gch_v = gch_ref[0][0:1, :]
        gnw = gnw_ref[0:1, :]
        dkf, dkb, dqf, dqb = rtab_ref[0, 0], rtab_ref[0, 1], rtab_ref[0, 2], rtab_ref[0, 3]

        def rope(x, rows):
            return x * cos_ref[rows, :] + _swap32(x, lane) * sin_ref[rows, :]

        def pass_a(n, st):
            rows = _rows(n * RC, RC)
            stf_ref[n] = st
            kr = rope(k_ref[rows, :], rows) * (HEAD_DIM ** -0.5)
            return gch_v * st + _mm_tn(kr * dkf, v_ref[rows, :]) * bd

        lax.fori_loop(0, nc, pass_a, jnp.zeros((LANE, LANE), F32))

        def pass_b(i, stb):
            n = nc - 1 - i
            rows = _rows(n * RC, RC)
            qr = rope(q_ref[rows, :], rows)
            kr = rope(k_ref[rows, :], rows) * (HEAD_DIM ** -0.5)
            v = v_ref[rows, :]
            o = _mm(qr * dqf, stf_ref[n]) + _mm(qr * dqb, stb)
            for h, m in ((0, m0), (1, m1)):
                s = _mm_nt(qr * m, kr) * dec_ref[h]
                o = o + _mm(s, v * m)
            oc = o - _head_mean(o, m0, m1)
            on = oc * lax.rsqrt(_head_mean(oc * oc, m0, m1) + GN_EPS)
            g = g_ref[rows, :]
            y_ref[rows, :] = (g * _sigmoid(g)) * (on * gnw)
            return gch_v * stb + _mm_tn(kr * dkb, v) * bd

        lax.fori_loop(0, nc, pass_b, jnp.zeros((LANE, LANE), F32))

    return pl.pallas_call(
        body, name="ret_fwd", grid=(RET_WIDTH // LANE,),
        in_specs=_ret_specs(T),
        out_specs=_strip(T, lambda j: j),
        out_shape=jax.ShapeDtypeStruct((T, RET_WIDTH), F32),
        scratch_shapes=[pltpu.VMEM((nc, LANE, LANE), F32)],
        compiler_params=_cparams(("arbitrary",)),
    )(proj, proj, proj, proj, cos, sin, dec, rtab, gch, gnw8)


def _ret_bwd_call(proj, dycat, tables, gnw8):
    T = proj.shape[0]
    nc = T // RC
    cos, sin, dec, rtab, gch = tables

    def body(q_ref, k_ref, v_ref, g_ref, cos_ref, sin_ref, dec_ref, rtab_ref, gch_ref, gnw_ref, dy_ref,
             dp_ref, dgnw_ref, stf_ref, dstb_ref, dkr_ref, dv_ref):
        lane, m0, m1, bd = _ret_masks()
        gch_v = gch_ref[0][0:1, :]
        gnw = gnw_ref[0:1, :]
        dkf, dkb, dqf, dqb = rtab_ref[0, 0], rtab_ref[0, 1], rtab_ref[0, 2], rtab_ref[0, 3]
        scale = HEAD_DIM ** -0.5
        zst = jnp.zeros((LANE, LANE), F32)

        def rope(x, rows):
            return x * cos_ref[rows, :] + _swap32(x, lane) * sin_ref[rows, :]

        def rope_t(d, rows):
            return d * cos_ref[rows, :] + _swap32(d * sin_ref[rows, :], lane)

        def pass_a(n, st):
            rows = _rows(n * RC, RC)
            stf_ref[n] = st
            kr = rope(k_ref[rows, :], rows) * scale
            return gch_v * st + _mm_tn(kr * dkf, v_ref[rows, :]) * bd

        lax.fori_loop(0, nc, pass_a, zst)

        def pass_b(i, carry):
            stb, d_f, dgnw = carry
            n = nc - 1 - i
            rows = _rows(n * RC, RC)
            qr = rope(q_ref[rows, :], rows)
            kr = rope(k_ref[rows, :], rows) * scale
            v = v_ref[rows, :]
            stf = stf_ref[n]
            qf = qr * dqf
            qb = qr * dqb
            o = _mm(qf, stf) + _mm(qb, stb)
            s_h = []
            for h, m in ((0, m0), (1, m1)):
                s = _mm_nt(qr * m, kr) * dec_ref[h]
                s_h.append(s)
                o = o + _mm(s, v * m)
            oc = o - _head_mean(o, m0, m1)
            rstd = lax.rsqrt(_head_mean(oc * oc, m0, m1) + GN_EPS)
            on = oc * rstd
            g = g_ref[rows, :]
            sg = _sigmoid(g)
            dy = dy_ref[rows, :]
            dp_ref[3, rows, :] = (dy * (on * gnw) * (sg * (1.0 + g * (1.0 - sg)))).astype(dp_ref.dtype)
            t = dy * (g * sg)
            dgnw = dgnw + jnp.sum(t * on, axis=0, keepdims=True)
            don = t * gnw
            do = rstd * (don - _head_mean(don, m0, m1) - on * _head_mean(don * on, m0, m1))
            dqr = _mm_nt(do, stf) * dqf + _mm_nt(do, stb) * dqb
            dkr = _mm_nt(v, d_f) * dkf
            dv = _mm(kr * dkf, d_f)
            for h, m in ((0, m0), (1, m1)):
                ds = _mm_nt(do * m, v) * dec_ref[h]
                dqr = dqr + _mm(ds, kr * m)
                dkr = dkr + _mm_tn(ds, qr * m)
                dv = dv + _mm_tn(s_h[h], do * m)
            dp_ref[0, rows, :] = rope_t(dqr, rows).astype(dp_ref.dtype)
            dkr_ref[rows, :] = dkr
            dv_ref[rows, :] = dv
            dstb_ref[n] = _mm_tn(qb, do) * bd
            d_f = _mm_tn(qf, do) * bd + gch_v * d_f
            stb = gch_v * stb + _mm_tn(kr * dkb, v) * bd
            return stb, d_f, dgnw

        _, _, dgnw = lax.fori_loop(0, nc, pass_b, (zst, zst, jnp.zeros((1, LANE), F32)))
        dgnw_ref[...] = jnp.concatenate([dgnw, jnp.zeros((SUB - 1, LANE), F32)], axis=0)

        def pass_c(n, d_b):
            rows = _rows(n * RC, RC)
            kr = rope(k_ref[rows, :], rows) * scale
            v = v_ref[rows, :]
            dkr = dkr_ref[rows, :] + _mm_nt(v, d_b) * dkb
            dp_ref[1, rows, :] = (rope_t(dkr, rows) * scale).astype(dp_ref.dtype)
            dp_ref[2, rows, :] = (dv_ref[rows, :] + _mm(kr * dkb, d_b)).astype(dp_ref.dtype)
            return dstb_ref[n] + gch_v * d_b

        lax.fori_loop(0, nc, pass_c, zst)

    return pl.pallas_call(
        body, name="ret_bwd", grid=(RET_WIDTH // LANE,),
        in_specs=_ret_specs(T) + [_strip(T, lambda j: j + 3)],
        out_specs=[pl.BlockSpec((4, T, LANE), lambda j: (0, 0, j), pipeline_mode=pl.Buffered(1)),
                   pl.BlockSpec((SUB, LANE), lambda j: (0, j))],
        out_shape=[jax.ShapeDtypeStruct((4, T, RET_WIDTH), _BF), jax.ShapeDtypeStruct((SUB, RET_WIDTH), F32)],
        scratch_shapes=[pltpu.VMEM((nc, LANE, LANE), F32), pltpu.VMEM((nc, LANE, LANE), F32),
                        pltpu.VMEM((T, LANE), F32), pltpu.VMEM((T, LANE), F32)],
        compiler_params=_cparams(("arbitrary",)),
    )(proj, proj, proj, proj, cos, sin, dec, rtab, gch, gnw8, dycat)


NA_Q = 2 * GRID_W
NA_WROWS = 10
NA_K = NA_WROWS * GRID_W
NA_TYPES = 5


def _na_onehots(rows_n):
    reps = [(0, 0), (2, 0), (4, 0), (rows_n - 4, rows_n - NA_WROWS), (rows_n - 2, rows_n - NA_WROWS)]
    rm = np.zeros((NA_TYPES, 2, NA_WROWS, 2 * NA_KH - 1), np.float32)
    for t, (r, ws) in enumerate(reps):
        for qh in range(2):
            qrow = r + qh
            rstart = min(max(qrow - NA_KH // 2, 0), rows_n - NA_KH)
            for kh in range(NA_WROWS):
                krow = ws + kh
                if rstart <= krow < rstart + NA_KH:
                    rm[t, qh, kh, krow - qrow + NA_KH - 1] = 1.0
    cm = np.zeros((GRID_W, GRID_W, 2 * NA_KW - 1), np.float32)
    for qc in range(GRID_W):
        cstart = min(max(qc - NA_KW // 2, 0), GRID_W - NA_KW)
        for kc in range(cstart, cstart + NA_KW):
            cm[qc, kc, kc - qc + NA_KW - 1] = 1.0
    return rm, cm


def _na_bias_tables(rpb, rows_n):
    rm, cm = _na_onehots(rows_n)
    val = jnp.einsum("hab,tqka,xyb->htqxky", rpb, rm, cm, precision=lax.Precision.HIGHEST)
    valid = np.einsum("tqk,xy->tqxky", rm.sum(-1), cm.sum(-1)) > 0.5
    tab = jnp.where(valid[None], val, NEG).reshape(2, 2, NA_TYPES, NA_Q, NA_K)
    return tab, jnp.swapaxes(tab, -1, -2)


def _na_bias_grad(dtab, rows_n):
    rm, cm = _na_onehots(rows_n)
    d6 = dtab.reshape(NA_HEADS, NA_TYPES, 2, GRID_W, NA_WROWS, GRID_W)
    return jnp.einsum("htqxky,tqka,xyb->hab", d6, rm, cm, precision=lax.Precision.HIGHEST)


def _na_step(p, npairs, rows_n):
    ws = jnp.clip(2 * p - NA_KH // 2, 0, rows_n - NA_WROWS)
    koff = pl.multiple_of(ws * GRID_W, LANE)
    typ = jnp.where(p == 0, 0, jnp.where(p == 1, 1, jnp.where(p == npairs - 2, 3, jnp.where(p == npairs - 1, 4, 2))))
    return _rows(p * NA_Q, NA_Q), pl.ds(koff, NA_K), typ


def _na_fwd_call(proj, btab):
    T = proj.shape[0]
    npairs, rows_n = T // NA_Q, T // GRID_W

    def body(q_ref, k_ref, v_ref, b_ref, o_ref):
        lane = lax.broadcasted_iota(jnp.int32, (NA_Q, LANE), 1)
        m0 = (lane < HEAD_DIM).astype(F32)
        m1 = 1.0 - m0

        def step(p, carry):
            qrows, krows, typ = _na_step(p, npairs, rows_n)
            q = q_ref[qrows, :]
            kw = k_ref[krows, :]
            vw = v_ref[krows, :]
            o = jnp.zeros((NA_Q, LANE), F32)
            for h, m in ((0, m0), (1, m1)):
                s = _mm_nt(q * m, kw) * (HEAD_DIM ** -0.5) + b_ref[0, h, typ]
                e = jnp.exp(s - jnp.max(s, axis=-1, keepdims=True))
                pr = e / jnp.sum(e, axis=-1, keepdims=True)
                o = o + _mm(pr, vw) * m
            o_ref[qrows, :] = o
            return carry

        lax.fori_loop(0, npairs, step, 0)

    return pl.pallas_call(
        body, name="na_fwd", grid=(NA_WIDTH // LANE,),
        in_specs=[_strip(T, lambda j: j + 18), _strip(T, lambda j: j + 20), _strip(T, lambda j: j + 22),
                  pl.BlockSpec((1, 2, NA_TYPES, NA_Q, NA_K), lambda j: (j, 0, 0, 0, 0))],
        out_specs=_strip(T, lambda j: j),
        out_shape=jax.ShapeDtypeStruct((T, NA_WIDTH), F32),
        compiler_params=_cparams(("arbitrary",)),
    )(proj, proj, proj, btab)


def _na_bwd_call(proj, dycat, btab, btab_t):
    T = proj.shape[0]
    npairs, rows_n = T // NA_Q, T // GRID_W
    scale = HEAD_DIM ** -0.5

    def body(q_ref, k_ref, v_ref, do_ref, b_ref, bt_ref, dq_ref, dk_ref, dv_ref, db_ref, dka_ref, dva_ref):
        lane = lax.broadcasted_iota(jnp.int32, (NA_Q, LANE), 1)
        m0 = (lane < HEAD_DIM).astype(F32)
        m1 = 1.0 - m0
        dka_ref[...] = jnp.zeros_like(dka_ref)
        dva_ref[...] = jnp.zeros_like(dva_ref)
        db_ref[...] = jnp.zeros_like(db_ref)

        def step(p, carry):
            qrows, krows, typ = _na_step(p, npairs, rows_n)
            q = q_ref[qrows, :]
            do = do_ref[qrows, :]
            kw = k_ref[krows, :]
            vw = v_ref[krows, :]
            dq = jnp.zeros((NA_Q, LANE), F32)
            dk = jnp.zeros((NA_K, LANE), F32)
            dv = jnp.zeros((NA_K, LANE), F32)
            for h, m in ((0, m0), (1, m1)):
                qm = q * m
                dom = do * m
                s = _mm_nt(qm, kw) * scale + b_ref[0, h, typ]
                e = jnp.exp(s - jnp.max(s, axis=-1, keepdims=True))
                pr = e / jnp.sum(e, axis=-1, keepdims=True)
                dpr = _mm_nt(dom, vw)
                ds = pr * (dpr - jnp.sum(pr * dpr, axis=-1, keepdims=True))
                db_ref[0, h, typ] += ds
                dq = dq + _mm(ds * scale, kw) * m
                st = _mm_nt(kw, qm) * scale + bt_ref[0, h, typ]
                et = jnp.exp(st - jnp.max(st, axis=0, keepdims=True))
                prt = et / jnp.sum(et, axis=0, keepdims=True)
                dprt = _mm_nt(vw, dom)
                dst = prt * (dprt - jnp.sum(prt * dprt, axis=0, keepdims=True))
                dk = dk + _mm(dst * scale, qm)
                dv = dv + _mm(prt, dom)
            dq_ref[qrows, :] = dq.astype(dq_ref.dtype)
            dka_ref[krows, :] += dk
            dva_ref[krows, :] += dv
            return carry

        lax.fori_loop(0, npairs, step, 0)
        dk_ref[...] = dka_ref[...].astype(dk_ref.dtype)
        dv_ref[...] = dva_ref[...].astype(dv_ref.dtype)

    tab = pl.BlockSpec((1, 2, NA_TYPES, NA_Q, NA_K), lambda j: (j, 0, 0, 0, 0))
    tab_t = pl.BlockSpec((1, 2, NA_TYPES, NA_K, NA_Q), lambda j: (j, 0, 0, 0, 0))
    out = lambda col: pl.BlockSpec((T, LANE), lambda j: (0, col(j)), pipeline_mode=pl.Buffered(1))
    dq, dk, dv, db = pl.pallas_call(
        body, name="na_bwd", grid=(NA_WIDTH // LANE,),
        in_specs=[_strip(T, lambda j: j + 18), _strip(T, lambda j: j + 20), _strip(T, lambda j: j + 22),
                  _strip(T, lambda j: j + 6), tab, tab_t],
        out_specs=[out(lambda j: j), out(lambda j: j), out(lambda j: j), tab],
        out_shape=[jax.ShapeDtypeStruct((T, NA_WIDTH), _BF)] * 3
        + [jax.ShapeDtypeStruct((2, 2, NA_TYPES, NA_Q, NA_K), F32)],
        scratch_shapes=[pltpu.VMEM((T, LANE), F32), pltpu.VMEM((T, LANE), F32)],
        compiler_params=_cparams(("arbitrary",)),
    )(proj, proj, proj, dycat, btab, btab_t)
    dp = jnp.concatenate([dq, dk, dv], axis=1)
    return jnp.stack([dp[:, :W_BLK], dp[:, W_BLK:]]), db


W_BLK = IN_WIDTH // N_DEV
TM = 512


def _ln_fwd(z, g, b):
    zc = z - jnp.mean(z, axis=-1, keepdims=True)
    var = jnp.mean(zc * zc, axis=-1, keepdims=True)
    return zc * lax.rsqrt(var + LN_EPS) * g + b


def _ln_bwd(dy, z, g):
    zc = z - jnp.mean(z, axis=-1, keepdims=True)
    rstd = lax.rsqrt(jnp.mean(zc * zc, axis=-1, keepdims=True) + LN_EPS)
    xhat = zc * rstd
    dxh = dy * g
    dz = rstd * (dxh - jnp.mean(dxh, axis=-1, keepdims=True) - xhat * jnp.mean(dxh * xhat, axis=-1, keepdims=True))
    return dz, dy * xhat


def _row_tile(T):
    return 1024 if T % 1024 == 0 else TM


def _inproj_call(xb, w, after):
    T = xb.shape[0]
    tm = _row_tile(T)

    def body(x_ref, w_ref, after_ref, o_ref):
        o_ref[...] = _mm(x_ref[...], w_ref[...])

    return pl.pallas_call(
        body, name="inproj", grid=(T // tm, N_DEV),
        in_specs=[pl.BlockSpec((tm, D_MODEL), lambda i, n: (i, 0)),
                  pl.BlockSpec((None, D_MODEL, W_BLK), lambda i, n: (n, 0, 0)),
                  pl.BlockSpec(memory_space=pl.ANY)],
        out_specs=pl.BlockSpec((tm, W_BLK), lambda i, n: (i, n)),
        out_shape=jax.ShapeDtypeStruct((T, IN_WIDTH), F32),
        compiler_params=_cparams(("parallel", "arbitrary")),
    )(xb, w, after)


def _vec_spec():
    return pl.BlockSpec((1, D_MODEL), lambda *_: (0, 0))


def _outproj_ln_call(y_lru, y_ret, y_na, x, w, g, b):
    T = x.shape[0]

    def body(yl_ref, yr_ref, yn_ref, x_ref, w_ref, g_ref, b_ref, z_ref, x1_ref, x1b_ref, yc_ref):
        yc_ref[:, 0:LRU_WIDTH] = yl_ref[...].astype(yc_ref.dtype)
        yc_ref[:, LRU_WIDTH:LRU_WIDTH + RET_WIDTH] = yr_ref[...].astype(yc_ref.dtype)
        yc_ref[:, LRU_WIDTH + RET_WIDTH:] = yn_ref[...].astype(yc_ref.dtype)
        z = ALPHA * x_ref[...] + _mm(yc_ref[...], w_ref[...].reshape(D_MODEL, D_MODEL))
        z_ref[...] = z
        x1 = _ln_fwd(z, g_ref[...], b_ref[...])
        x1_ref[...] = x1
        x1b_ref[...] = x1.astype(x1b_ref.dtype)

    row = lambda w_: pl.BlockSpec((TM, w_), lambda i: (i, 0))
    return pl.pallas_call(
        body, name="outproj_ln", grid=(T // TM,),
        in_specs=[row(LRU_WIDTH), row(RET_WIDTH), row(NA_WIDTH), row(D_MODEL),
                  pl.BlockSpec((N_DEV, LANE, D_MODEL), lambda i: (0, 0, 0)), _vec_spec(), _vec_spec()],
        out_specs=[row(D_MODEL)] * 4,
        out_shape=[jax.ShapeDtypeStruct((T, D_MODEL), F32), jax.ShapeDtypeStruct((T, D_MODEL), F32),
                   jax.ShapeDtypeStruct((T, D_MODEL), _BF), jax.ShapeDtypeStruct((T, D_MODEL), _BF)],
        compiler_params=_cparams(("parallel",)),
    )(y_lru, y_ret, y_na, x, w, g, b)


def _ffn_ln_call(x1, x1b, wg, wu, wd, g, b):
    T = x1.shape[0]

    def body(x_ref, xb_ref, wg_ref, wu_ref, wd_ref, g_ref, b_ref, z_ref, x2_ref, x2b_ref, acc_ref):
        n = pl.program_id(1)

        @pl.when(n == 0)
        def _():
            acc_ref[...] = jnp.zeros_like(acc_ref)

        xb = xb_ref[...]
        gp = _mm(xb, wg_ref[...])
        hid = gp * _sigmoid(gp) * _mm(xb, wu_ref[...])
        acc_ref[...] += _mm(hid, wd_ref[...])

        @pl.when(n == N_DEV - 1)
        def _():
            z = ALPHA * x_ref[...] + acc_ref[...]
            z_ref[...] = z
            x2 = _ln_fwd(z, g_ref[...], b_ref[...])
            x2_ref[...] = x2
            x2b_ref[...] = x2.astype(x2b_ref.dtype)

    row = pl.BlockSpec((TM, D_MODEL), lambda i, n: (i, 0))
    return pl.pallas_call(
        body, name="ffn_ln", grid=(T // TM, N_DEV),
        in_specs=[row, row,
                  pl.BlockSpec((None, D_MODEL, W_BLK), lambda i, n: (n, 0, 0)),
                  pl.BlockSpec((None, D_MODEL, W_BLK), lambda i, n: (n, 0, 0)),
                  pl.BlockSpec((None, W_BLK, D_MODEL), lambda i, n: (n, 0, 0)), _vec_spec(), _vec_spec()],
        out_specs=[row] * 3,
        out_shape=[jax.ShapeDtypeStruct((T, D_MODEL), F32), jax.ShapeDtypeStruct((T, D_MODEL), F32),
                   jax.ShapeDtypeStruct((T, D_MODEL), _BF)],
        scratch_shapes=[pltpu.VMEM((TM, D_MODEL), F32)],
        compiler_params=_cparams(("parallel", "arbitrary")),
    )(x1, x1b, wg, wu, wd, g, b)


def _loss_call(y, t):
    T = y.shape[0]

    def body(y_ref, t_ref, dy_ref, loss_ref):
        @pl.when(pl.program_id(0) == 0)
        def _():
            loss_ref[...] = jnp.zeros_like(loss_ref)

        err = y_ref[...] - t_ref[...]
        dy_ref[...] = err * (1.0 / D_MODEL)
        part = 0.5 * jnp.sum(jnp.mean(err * err, axis=-1, keepdims=True), axis=0, keepdims=True)
        loss_ref[...] += jnp.broadcast_to(part, loss_ref.shape)

    row = pl.BlockSpec((TM, D_MODEL), lambda i: (i, 0))
    return pl.pallas_call(
        body, name="loss", grid=(T // TM,),
        in_specs=[row, row],
        out_specs=[row, pl.BlockSpec((SUB, LANE), lambda i: (0, 0))],
        out_shape=[jax.ShapeDtypeStruct((T, D_MODEL), F32), jax.ShapeDtypeStruct((SUB, LANE), F32)],
        compiler_params=_cparams(("arbitrary",)),
    )(y, t)


def _ffn_bwd_call(dx2, z2, x1, x1b, wg, wu, wd, g, after):
    T = x1.shape[0]

    def body(dx2_ref, z_ref, x_ref, xb_ref, wg_ref, wu_ref, wd_ref, g_ref, after_ref,
             dx1_ref, dgp_ref, dup_ref, hid_ref, dzb_ref, dln_ref, acc_ref):
        i, n = pl.program_id(0), pl.program_id(1)

        @pl.when((i == 0) & (n == 0))
        def _():
            dln_ref[...] = jnp.zeros_like(dln_ref)

        @pl.when(n == 0)
        def _():
            dy = dx2_ref[...]
            dz, dg_rows = _ln_bwd(dy, z_ref[...], g_ref[...])
            dzb_ref[...] = dz.astype(dzb_ref.dtype)
            acc_ref[...] = ALPHA * dz
            dln_ref[0:1, :] += jnp.sum(dg_rows, axis=0, keepdims=True)
            dln_ref[1:2, :] += jnp.sum(dy, axis=0, keepdims=True)

        xb = xb_ref[...]
        gp = _mm(xb, wg_ref[...])
        up = _mm(xb, wu_ref[...])
        sg = _sigmoid(gp)
        act = gp * sg
        hid_ref[...] = (act * up).astype(hid_ref.dtype)
        dhid = _mm_nt(dzb_ref[...], wd_ref[...])
        dup = dhid * act
        dgp = dhid * up * (sg * (1.0 + gp * (1.0 - sg)))
        dgp_ref[...] = dgp.astype(dgp_ref.dtype)
        dup_ref[...] = dup.astype(dup_ref.dtype)
        acc_ref[...] += _mm_nt(dgp, wg_ref[...]) + _mm_nt(dup, wu_ref[...])

        @pl.when(n == N_DEV - 1)
        def _():
            dx1_ref[...] = acc_ref[...]

    row = pl.BlockSpec((TM, D_MODEL), lambda i, n: (i, 0))
    blk = pl.BlockSpec((None, TM, W_BLK), lambda i, n: (n, i, 0))
    return pl.pallas_call(
        body, name="ffn_bwd", grid=(T // TM, N_DEV),
        in_specs=[row, row, row, row,
                  pl.BlockSpec((None, D_MODEL, W_BLK), lambda i, n: (n, 0, 0)),
                  pl.BlockSpec((None, D_MODEL, W_BLK), lambda i, n: (n, 0, 0)),
                  pl.BlockSpec((None, W_BLK, D_MODEL), lambda i, n: (n, 0, 0)), _vec_spec(),
                  pl.BlockSpec(memory_space=pl.ANY)],
        out_specs=[row, blk, blk, pl.BlockSpec((TM, W_BLK), lambda i, n: (i, n)), row,
                   pl.BlockSpec((SUB, D_MODEL), lambda i, n: (0, 0))],
        out_shape=[jax.ShapeDtypeStruct((T, D_MODEL), F32),
                   jax.ShapeDtypeStruct((N_DEV, T, W_BLK), _BF), jax.ShapeDtypeStruct((N_DEV, T, W_BLK), _BF),
                   jax.ShapeDtypeStruct((T, N_DEV * W_BLK), _BF), jax.ShapeDtypeStruct((T, D_MODEL), _BF),
                   jax.ShapeDtypeStruct((SUB, D_MODEL), F32)],
        scratch_shapes=[pltpu.VMEM((TM, D_MODEL), F32)],
        compiler_params=_cparams(("arbitrary", "arbitrary")),
    )(dx2, z2, x1, x1b, wg, wu, wd, g, after)


def _outproj_bwd_call(dx1, z1, w, g):
    T = dx1.shape[0]

    def body(dx_ref, z_ref, w_ref, g_ref, dzb_ref, dyc_ref, dres_ref, dln_ref):
        @pl.when(pl.program_id(0) == 0)
        def _():
            dln_ref[...] = jnp.zeros_like(dln_ref)

        dy = dx_ref[...]
        dz, dg_rows = _ln_bwd(dy, z_ref[...], g_ref[...])
        dzb_ref[...] = dz.astype(dzb_ref.dtype)
        dres_ref[...] = ALPHA * dz
        dyc_ref[...] = _mm_nt(dz, w_ref[...].reshape(D_MODEL, D_MODEL))
        dln_ref[0:1, :] += jnp.sum(dg_rows, axis=0, keepdims=True)
        dln_ref[1:2, :] += jnp.sum(dy, axis=0, keepdims=True)

    row = pl.BlockSpec((TM, D_MODEL), lambda i: (i, 0))
    return pl.pallas_call(
        body, name="outproj_bwd", grid=(T // TM,),
        in_specs=[row, row, pl.BlockSpec((N_DEV, LANE, D_MODEL), lambda i: (0, 0, 0)), _vec_spec()],
        out_specs=[row, row, row, pl.BlockSpec((SUB, D_MODEL), lambda i: (0, 0))],
        out_shape=[jax.ShapeDtypeStruct((T, D_MODEL), _BF), jax.ShapeDtypeStruct((T, D_MODEL), F32),
                   jax.ShapeDtypeStruct((T, D_MODEL), F32), jax.ShapeDtypeStruct((SUB, D_MODEL), F32)],
        compiler_params=_cparams(("arbitrary",)),
    )(dx1, z1, w, g)


def _inproj_bwd_call(dres, dp, w):
    T = dres.shape[0]

    def body(dres_ref, dp_ref, w_ref, dx_ref):
        acc = dres_ref[...]
        for n in range(N_DEV):
            acc = acc + _mm_nt(dp_ref[n], w_ref[n])
        dx_ref[...] = acc

    row = pl.BlockSpec((TM, D_MODEL), lambda i: (i, 0))
    return pl.pallas_call(
        body, name="inproj_bwd", grid=(T // TM,),
        in_specs=[row, pl.BlockSpec((N_DEV, TM, W_BLK), lambda i: (0, i, 0)),
                  pl.BlockSpec((N_DEV, D_MODEL, W_BLK), lambda i: (0, 0, 0))],
        out_specs=row,
        out_shape=jax.ShapeDtypeStruct((T, D_MODEL), F32),
        compiler_params=_cparams(("parallel",)),
    )(dres, dp, w)


def _tn_cols_call(a, b3, name):
    T, ka = a.shape
    nblk, _, nb = b3.shape

    def body(a_ref, b_ref, o_ref):
        o_ref[...] = _mm_tn(a_ref[...], b_ref[...]).astype(o_ref.dtype)

    return pl.pallas_call(
        body, name=name, grid=(nblk,),
        in_specs=[pl.BlockSpec((T, ka), lambda n: (0, 0), pipeline_mode=pl.Buffered(1)),
                  pl.BlockSpec((None, T, nb), lambda n: (n, 0, 0))],
        out_specs=pl.BlockSpec((None, ka, nb), lambda n: (n, 0, 0)),
        out_shape=jax.ShapeDtypeStruct((nblk, ka, nb), _BF),
        compiler_params=_cparams(("parallel",)),
    )(a, b3)


def _tn_rows_call(a, b, kb, name):
    T, ka = a.shape
    n = b.shape[1]

    def body(a_ref, b_ref, o_ref):
        o_ref[...] = _mm_tn(a_ref[...], b_ref[...]).astype(o_ref.dtype)

    return pl.pallas_call(
        body, name=name, grid=(ka // kb,),
        in_specs=[pl.BlockSpec((T, kb), lambda r: (0, r)),
                  pl.BlockSpec((T, n), lambda r: (0, 0), pipeline_mode=pl.Buffered(1))],
        out_specs=pl.BlockSpec((None, kb, n), lambda r: (r, 0, 0)),
        out_shape=jax.ShapeDtypeStruct((ka // kb, kb, n), _BF),
        compiler_params=_cparams(("parallel",)),
    )(a, b)


def _me():
    return lax.axis_index("x"), lax.axis_index("y"), lax.axis_index("c")


def _flip(k):
    x, y, c = _me()
    return (1 - x if k & 4 else x, 1 - y if k & 2 else y, 1 - c if k & 1 else c)


def _dev_index(pos):
    return 4 * pos[0] + 2 * pos[1] + pos[2]


_HBM = pl.BlockSpec(memory_space=pltpu.HBM)
_SEM = pl.BlockSpec(memory_space=pltpu.SEMAPHORE)


def _comm_copies(ins, lands, gather_flags, send_sems, recv_sems):
    me = _dev_index(_me())
    copies = []
    for k in range(N_DEV):
        peer = _flip(k)
        for a, (src, land) in enumerate(zip(ins, lands)):
            copies.append(pltpu.make_async_remote_copy(
                src_ref=src if gather_flags[a] else src.at[_dev_index(peer)], dst_ref=land.at[me],
                send_sem=send_sems.at[k * len(ins) + a], recv_sem=recv_sems.at[k * len(ins) + a],
                device_id=peer, device_id_type=MESH))
    return copies


def _comm_start_call(arrs, gather_flags, after, name):
    n = len(arrs)
    lands = [lax.empty((N_DEV,) + (v.shape if gf else v.shape[1:]), v.dtype) for v, gf in zip(arrs, gather_flags)]

    def body(*refs):
        ins, lnd = refs[:n], refs[n:2 * n]
        send_sems, recv_sems = refs[2 * n + len(after)], refs[2 * n + len(after) + 1]
        for cp in _comm_copies(ins, lnd, gather_flags, send_sems, recv_sems):
            cp.start()
        refs[-1][...] = jnp.zeros_like(refs[-1])

    hbm = [pltpu.with_memory_space_constraint(v, pltpu.HBM) for v in list(arrs) + lands]
    out = pl.pallas_call(
        body, name=name,
        out_shape=(pltpu.SemaphoreType.DMA((N_DEV * n,)), pltpu.SemaphoreType.DMA((N_DEV * n,)),
                   *[pltpu.HBM(v.shape, v.dtype) for v in hbm], jax.ShapeDtypeStruct((SUB, LANE), F32)),
        in_specs=[_HBM] * (2 * n) + [pl.BlockSpec(memory_space=pl.ANY)] * len(after),
        out_specs=(_SEM, _SEM, *[_HBM] * (2 * n), pl.BlockSpec(memory_space=pltpu.VMEM)),
        input_output_aliases={i: 2 + i for i in range(2 * n)},
        compiler_params=pltpu.CompilerParams(has_side_effects=pltpu.SideEffectType.DATAFLOW_SIDE_EFFECTING),
    )(*hbm, *after)
    return out[:-1], out[-1]


def _comm_wait_call(state, gather_flags, after, name):
    n = len(gather_flags)
    send_sems, recv_sems, thru = state[0], state[1], state[2:]

    def body(*refs):
        ins, lnd, ssem, rsem = refs[:n], refs[n:2 * n], refs[2 * n], refs[2 * n + 1]
        for cp in _comm_copies(ins, lnd, gather_flags, ssem, rsem):
            cp.wait_send()
            cp.wait_recv()

    out = pl.pallas_call(
        body, name=name,
        out_shape=tuple(pltpu.HBM(v.shape, v.dtype) for v in thru),
        in_specs=[_HBM] * (2 * n) + [_SEM, _SEM] + [pl.BlockSpec(memory_space=pl.ANY)] * len(after),
        out_specs=tuple([_HBM] * (2 * n)),
        input_output_aliases={i: i for i in range(2 * n)},
        compiler_params=pltpu.CompilerParams(has_side_effects=pltpu.SideEffectType.DATAFLOW_SIDE_EFFECTING),
    )(*thru, send_sems, recv_sems, *after)
    return out[n:]


def _sum8_call(recv, rows, name):
    _, r, c = recv.shape

    def body(x_ref, o_ref):
        acc = x_ref[0].astype(F32)
        for s in range(1, N_DEV):
            acc = acc + x_ref[s].astype(F32)
        o_ref[...] = acc

    return pl.pallas_call(
        body, name=name, grid=(r // rows,),
        in_specs=[pl.BlockSpec((N_DEV, rows, c), lambda i: (0, i, 0))],
        out_specs=pl.BlockSpec((rows, c), lambda i: (i, 0)),
        out_shape=jax.ShapeDtypeStruct((r, c), F32),
        compiler_params=_cparams(("parallel",)),
    )(recv)


def _adamw_call(w, g, m, v, rows, name):
    r, c = w.shape

    def body(w_ref, g_ref, m_ref, v_ref, d_ref, nm_ref, nv_ref):
        gr = g_ref[...]
        nm = ADAM_B1 * m_ref[...] + (1.0 - ADAM_B1) * gr
        nv = ADAM_B2 * v_ref[...] + (1.0 - ADAM_B2) * (gr * gr)
        m_hat = nm / (1.0 - ADAM_B1 ** ADAM_STEP)
        v_hat = nv / (1.0 - ADAM_B2 ** ADAM_STEP)
        d_ref[...] = -ADAM_LR * (m_hat / (jnp.sqrt(v_hat) + ADAM_EPS) + ADAM_WD * w_ref[...])
        nm_ref[...] = nm
        nv_ref[...] = nv

    spec = pl.BlockSpec((rows, c), lambda i: (i, 0))
    return pl.pallas_call(
        body, name=name, grid=(r // rows,),
        in_specs=[spec] * 4, out_specs=[spec] * 3,
        out_shape=[jax.ShapeDtypeStruct((r, c), F32)] * 3,
        compiler_params=_cparams(("parallel",)),
    )(w, g, m, v)


SH_ROWS = 16
SH_W = LRU_WIDTH // N_DEV
REP_ROWS = 824
_REP_SIZES = (LRU_WIDTH, 2 * 6 * 64 * 64, 2 * 6 * 64 * 64, RET_WIDTH, 1920, D_MODEL, D_MODEL, D_MODEL, D_MODEL)
_RPB_SIZE = NA_HEADS * (2 * NA_KH - 1) * (2 * NA_KW - 1)


def _pack_sh(cw, ba, bx, lam):
    return jnp.concatenate([cw, ba, bx, lam], axis=0)


def _pad_sh(p):
    pad = [(0, 0)] * (p.ndim - 2) + [(0, SH_ROWS - p.shape[-2]), (0, LANE - p.shape[-1])]
    return jnp.pad(p, pad)


def _pack_rep(cb, wa, wx, gnw, rpb, l1g, l1b, l2g, l2b):
    flat = jnp.concatenate([cb.reshape(-1), wa.reshape(-1), wx.reshape(-1), gnw.reshape(-1),
                            jnp.pad(rpb.reshape(-1), (0, 1920 - _RPB_SIZE)), l1g, l1b, l2g, l2b,
                            jnp.zeros((REP_ROWS * LANE - sum(_REP_SIZES),), F32)])
    return flat.reshape(REP_ROWS, LANE)


def _unpack_rep(p):
    nl = p.shape[0]
    flat = p.reshape(nl, -1)
    out, off = [], 0
    for size in _REP_SIZES:
        out.append(flat[:, off:off + size])
        off += size
    cb, wa, wx, gnw, rpb, l1g, l1b, l2g, l2b = out
    return (cb, wa.reshape(nl, 2, 6, 64, 64), wx.reshape(nl, 2, 6, 64, 64), gnw,
            rpb[:, :_RPB_SIZE].reshape(nl, NA_HEADS, 2 * NA_KH - 1, 2 * NA_KW - 1), l1g, l1b, l2g, l2b)


def _adamw_nd(w, g, m, v, rows, name):
    shp = w.shape
    f = lambda t: t.reshape(-1, shp[-1])
    return [t.reshape(shp) for t in _adamw_call(f(w), f(g), f(m), f(v), rows, name)]


def kernel(x, w_in, conv_w, conv_b, lru_w_a, lru_b_a, lru_w_x, lru_b_x, lru_lam, ret_gn_w, na_rpb, w_out, ln1_g, ln1_b, w_gate, w_up, w_down, ln2_g, ln2_b, loss_target, m_w_in, m_conv_w, m_conv_b, m_lru_w_a, m_lru_b_a, m_lru_w_x, m_lru_b_x, m_lru_lam, m_ret_gn_w, m_na_rpb, m_w_out, m_ln1_g, m_ln1_b, m_w_gate, m_w_up, m_w_down, m_ln2_g, m_ln2_b, v_w_in, v_conv_w, v_conv_b, v_lru_w_a, v_lru_b_a, v_lru_w_x, v_lru_b_x, v_lru_lam, v_ret_gn_w, v_na_rpb, v_w_out, v_ln1_g, v_ln1_b, v_w_gate, v_w_up, v_w_down, v_ln2_g, v_ln2_b):
    nl = w_in.shape[0]
    T = x.shape[1]
    rows_n = T // GRID_W
    x0, target = x[0], loss_target[0]
    ffpad = W_BLK - FF_BLK

    win_b = w_in.astype(_BF)
    wg_b = jnp.pad(w_gate, ((0, 0), (0, 0), (0, ffpad))).astype(_BF)
    wu_b = jnp.pad(w_up, ((0, 0), (0, 0), (0, ffpad))).astype(_BF)
    wd_b = jnp.pad(w_down, ((0, 0), (0, ffpad), (0, 0))).astype(_BF)
    wout_b = w_out.astype(_BF)
    ag_flags = [True] * 6

    def ag_start(l, after):
        sh = _pad_sh(_pack_sh(conv_w[l], lru_b_a[l], lru_b_x[l], lru_lam[l]))
        return _comm_start_call([win_b[l], wg_b[l], wu_b[l], wd_b[l], wout_b[l], sh], ag_flags, after, f"ag_start{l}")

    tables = _ret_tables(T)
    layers = []
    gathered = []
    xs, xb = x0, x0.astype(_BF)
    ag_state, token = ag_start(0, [])
    for l in range(nl):
        gathered.append(_comm_wait_call(ag_state, ag_flags, [xb], f"ag_wait{l}"))
        win, wg, wu, wd, wout, shg = gathered[l]
        if l + 1 < nl:
            ag_state, token = ag_start(l + 1, [shg])
        full = shg[:, :10, :SH_W].transpose(1, 0, 2).reshape(10, LRU_WIDTH)
        vec, w4 = _lru_pack(full[0:4], conv_b[l], lru_w_a[l], full[4:6], lru_w_x[l], full[6:8], full[8:10])
        gnw8 = jnp.pad(ret_gn_w[l][None], ((0, SUB - 1), (0, 0)))
        btab, btab_t = _na_bias_tables(na_rpb[l], rows_n)
        proj = _inproj_call(xb, win, token)
        y_lru = _lru_fwd_call(proj, vec, w4)
        y_ret = _ret_fwd_call(proj, tables, gnw8)
        y_na = _na_fwd_call(proj, btab)
        z1, x1, x1b, ycb = _outproj_ln_call(y_lru, y_ret, y_na, xs, wout, ln1_g[l][None], ln1_b[l][None])
        z2, x2, x2b = _ffn_ln_call(x1, x1b, wg, wu, wd, ln2_g[l][None], ln2_b[l][None])
        layers.append(dict(xb=xb, proj=proj, vec=vec, w4=w4, gnw8=gnw8, btab=btab, btab_t=btab_t,
                           z1=z1, x1=x1, x1b=x1b, ycb=ycb, z2=z2))
        xs, xb = x2, x2b

    dx, loss_blk = _loss_call(xs, target)
    loss = lax.psum(loss_blk[0, 0], ("x", "y", "c"))

    gxa_flags = [False] * 4
    gxb_flags = [False, False, True]
    gxa_state, gxb_state = [None] * nl, [None] * nl
    token = loss_blk
    for l in reversed(range(nl)):
        s = layers[l]
        win, wg, wu, wd, wout, _ = gathered[l]
        dx1, dgp, dup, hid, dz2b, dln2 = _ffn_bwd_call(dx, s["z2"], s["x1"], s["x1b"], wg, wu, wd, ln2_g[l][None], token)
        dwg = _tn_cols_call(s["x1b"], dgp, "tn_cols")
        dwu = _tn_cols_call(s["x1b"], dup, "tn_cols")
        dwd = _tn_rows_call(hid, dz2b, W_BLK, "tn_rows_down")
        dz1b, dyc, dres, dln1 = _outproj_bwd_call(dx1, s["z1"], wout, ln1_g[l][None])
        dwout = _tn_rows_call(s["ycb"], dz1b, LANE, "tn_rows_out")
        gxa_state[l], token = _comm_start_call([dwg, dwu, dwd, dwout], gxa_flags, [], f"gxa_start{l}")
        dp_lru, dvec, dw4 = _lru_bwd_call(s["proj"], dyc, s["vec"], s["w4"], token)
        dp_ret, dgnw = _ret_bwd_call(s["proj"], dyc, tables, s["gnw8"])
        dp_na, dbias = _na_bwd_call(s["proj"], dyc, s["btab"], s["btab_t"])
        dp = jnp.concatenate([dp_lru, dp_ret, dp_na], axis=0)
        dwin = _tn_cols_call(s["xb"], dp, "tn_cols")
        dx = _inproj_bwd_call(dres, dp, win)
        dcw, dcb, dwa, dba, dwx, dbx, dlam = _lru_unpack(dvec, dw4)
        rep = _pack_rep(dcb, dwa, dwx, dgnw[0], _na_bias_grad(dbias, rows_n), dln1[0], dln1[1], dln2[0], dln2[1])
        sh = _pack_sh(dcw, dba, dbx, dlam).reshape(10, N_DEV, SH_W).transpose(1, 0, 2)
        gxb_state[l], token = _comm_start_call([dwin, _pad_sh(sh), rep], gxb_flags, [], f"gxb_start{l}")

    g_big = [[None] * nl for _ in range(5)]
    g_sh = [None] * nl
    g_rep = [None] * nl
    after = [dx, token]
    for l in reversed(range(nl)):
        ra = _comm_wait_call(gxa_state[l], gxa_flags, after, f"gxa_wait{l}")
        g_big[1][l] = _sum8_call(ra[0], TM, "sum8_cols")[:, :FF_BLK]
        g_big[2][l] = _sum8_call(ra[1], TM, "sum8_cols")[:, :FF_BLK]
        g_big[3][l] = _sum8_call(ra[2], W_BLK, "sum8_down")[:FF_BLK]
        g_big[4][l] = _sum8_call(ra[3], LANE, "sum8_out")
        rb = _comm_wait_call(gxb_state[l], gxb_flags, [g_big[4][l]], f"gxb_wait{l}")
        g_big[0][l] = _sum8_call(rb[0], TM, "sum8_cols")
        g_sh[l] = _sum8_call(rb[1], SH_ROWS, "sum8_sh")
        g_rep[l] = _sum8_call(rb[2], REP_ROWS, "sum8_rep")
        after = [g_rep[l]]

    g_w_in, g_w_gate, g_w_up, g_w_down, g_w_out = [jnp.stack(t) for t in g_big]
    big = {
        "w_in": _adamw_nd(w_in, g_w_in, m_w_in, v_w_in, TM, "adamw_in"),
        "w_gate": _adamw_nd(w_gate, g_w_gate, m_w_gate, v_w_gate, TM, "adamw_ff"),
        "w_up": _adamw_nd(w_up, g_w_up, m_w_up, v_w_up, TM, "adamw_ff"),
        "w_down": _adamw_nd(w_down, g_w_down, m_w_down, v_w_down, FF_BLK, "adamw_down"),
        "w_out": _adamw_nd(w_out, g_w_out, m_w_out, v_w_out, LANE, "adamw_out"),
    }
    g_shp = jnp.stack(g_sh)
    pack_sh = lambda cw, ba, bx, lam: _pad_sh(jnp.concatenate([cw, ba, bx, lam], axis=1))
    sh_out = _adamw_nd(pack_sh(conv_w, lru_b_a, lru_b_x, lru_lam), g_shp,
                       pack_sh(m_conv_w, m_lru_b_a, m_lru_b_x, m_lru_lam),
                       pack_sh(v_conv_w, v_lru_b_a, v_lru_b_x, v_lru_lam), SH_ROWS, "adamw_sh")

    def split_sh(p):
        p = p[:, :, :SH_W]
        return {"conv_w": p[:, 0:4], "lru_b_a": p[:, 4:6], "lru_b_x": p[:, 6:8], "lru_lam": p[:, 8:10]}

    g_repp = jnp.stack(g_rep)
    pack_rep = lambda *ps: jnp.stack([_pack_rep(*[p[l] for p in ps]) for l in range(nl)])
    rep_names = ("conv_b", "lru_w_a", "lru_w_x", "ret_gn_w", "na_rpb", "ln1_g", "ln1_b", "ln2_g", "ln2_b")
    rep_out = _adamw_nd(pack_rep(conv_b, lru_w_a, lru_w_x, ret_gn_w, na_rpb, ln1_g, ln1_b, ln2_g, ln2_b), g_repp,
                        pack_rep(m_conv_b, m_lru_w_a, m_lru_w_x, m_ret_gn_w, m_na_rpb, m_ln1_g, m_ln1_b, m_ln2_g, m_ln2_b),
                        pack_rep(v_conv_b, v_lru_w_a, v_lru_w_x, v_ret_gn_w, v_na_rpb, v_ln1_g, v_ln1_b, v_ln2_g, v_ln2_b),
                        REP_ROWS, "adamw_rep")

    grads = {"w_in": g_w_in, "w_gate": g_w_gate, "w_up": g_w_up, "w_down": g_w_down, "w_out": g_w_out}
    grads.update(split_sh(g_shp))
    grads.update(dict(zip(rep_names, _unpack_rep(g_repp))))
    kinds = []
    for k in range(3):
        d = {n: big[n][k] for n in big}
        d.update(split_sh(sh_out[k]))
        d.update(dict(zip(rep_names, _unpack_rep(rep_out[k]))))
        kinds.append(d)
    order = ("w_in", "conv_w", "conv_b", "lru_w_a", "lru_b_a", "lru_w_x", "lru_b_x", "lru_lam", "ret_gn_w", "na_rpb",
             "w_out", "ln1_g", "ln1_b", "w_gate", "w_up", "w_down", "ln2_g", "ln2_b")
    outs = [loss, dx[None]]
    for d in (grads, *kinds):
        outs.extend(d[n] for n in order)
    return tuple(outs)
```

```python
import functools
import math

import numpy as np
import jax
import jax.numpy as jnp
from jax import lax
from jax.experimental import pallas as pl
from jax.experimental.pallas import tpu as pltpu

F32 = jnp.float32
_BF = jnp.bfloat16

D_MODEL = 1024
DEPTH = 4
GRID_W = 64
HEAD_DIM = 64
LRU_WIDTH = 384
RET_WIDTH = 384
RET_HEADS = 6
NA_WIDTH = 256
NA_HEADS = 4
IN_WIDTH = 3072
CONV_WIDTH = 4
LRU_C = 8.0
RET_CHUNK = 128
ROPE_BASE = 10000.0
GN_EPS = 1e-6
NA_KH = 8
NA_KW = 16
D_FF = 2816
FF_BLK = 352
N_DEV = 8
ALPHA = (2 * DEPTH) ** 0.25
LN_EPS = 1e-5
ADAM_LR = 0.001
ADAM_B1 = 0.9
ADAM_B2 = 0.999
ADAM_EPS = 1e-08
ADAM_WD = 0.01
ADAM_STEP = 10

LANE = 128
SUB = 8
VMEM_MB = 56
NEG = -1e30

MESH = pl.DeviceIdType.MESH


def _cparams(sem=None, vmem_mb=VMEM_MB):
    return pltpu.CompilerParams(dimension_semantics=sem, vmem_limit_bytes=vmem_mb << 20)


def _mm(a, b):
    return jnp.dot(a.astype(_BF), b.astype(_BF), preferred_element_type=F32)


def _mm_nt(a, b):
    return lax.dot_general(a.astype(_BF), b.astype(_BF), (((1,), (1,)), ((), ())), preferred_element_type=F32)


def _mm_tn(a, b):
    return lax.dot_general(a.astype(_BF), b.astype(_BF), (((0,), (0,)), ((), ())), preferred_element_type=F32)


def _sigmoid(x):
    return jax.nn.sigmoid(x)


def _rows(start, size):
    return pl.ds(pl.multiple_of(start, SUB), size)


def _loop2(n, body, init):
    assert n % 2 == 0
    return lax.fori_loop(0, n // 2, lambda i, c: body(2 * i + 1, body(2 * i, c)), init)


def _strip(T, col):
    return pl.BlockSpec((T, LANE), lambda j: (0, col(j)), pipeline_mode=pl.Buffered(1))


LRU_CH = 256
_GELU_C0 = math.sqrt(2.0 / math.pi)
_GELU_C1 = 0.044715


def _gelu_parts(x):
    x2 = x * x
    t = jnp.tanh(_GELU_C0 * (x + _GELU_C1 * x * x2))
    val = 0.5 * x * (1.0 + t)
    der = 0.5 * (1.0 + t) + 0.5 * x * (1.0 - t * t) * _GELU_C0 * (1.0 + 3.0 * _GELU_C1 * x2)
    return val, der


def _softplus_neg(lam):
    e = jnp.exp(-jnp.abs(lam))
    w = 1.0 + e
    l1p = jnp.where(w == 1.0, e, jnp.log(w) * (e / jnp.where(w == 1.0, 1.0, w - 1.0)))
    return jnp.maximum(-lam, 0.0) + l1p


def _window(ref, t0, ch, T):
    prev = ref[_rows(jnp.maximum(t0 - SUB, 0), SUB), :].astype(F32)
    nxt = ref[_rows(jnp.minimum(t0 + ch, T - SUB), SUB), :].astype(F32)
    prev = jnp.where(t0 > 0, prev, 0.0)
    nxt = jnp.where(t0 + ch < T, nxt, 0.0)
    return jnp.concatenate([prev, ref[_rows(t0, ch), :].astype(F32), nxt], axis=0)


def _tap(win, shift, ch):
    n = win.shape[0]
    return pltpu.roll(win, (-shift) % n, 0)[SUB:SUB + ch]


def _lru_conv(xb_ref, vec, t0, T):
    win = _window(xb_ref, t0, LRU_CH, T)
    xc = jnp.broadcast_to(vec[4:5, :], (LRU_CH, LANE))
    for j in range(CONV_WIDTH):
        xc = xc + _tap(win, j - CONV_WIDTH // 2, LRU_CH) * vec[j:j + 1, :]
    return xc


def _lru_dir(pre_a, pre_x, sp):
    r = _sigmoid(pre_a)
    i = _sigmoid(pre_x)
    log_a = (-LRU_C) * r * sp
    a = jnp.exp(log_a)
    z = jnp.tanh(-log_a) * (a * a + 1.0)
    s = jnp.sqrt(z)
    return r, i, a, s


def _scan_tile(a, b, reverse, row):
    for k in (1, 2, 4):
        if not reverse:
            a_s, b_s, m = pltpu.roll(a, k, 0), pltpu.roll(b, k, 0), row >= k
        else:
            a_s, b_s, m = pltpu.roll(a, SUB - k, 0), pltpu.roll(b, SUB - k, 0), row < SUB - k
        b = jnp.where(m, a * b_s + b, b)
        a = jnp.where(m, a * a_s, a)
    return a, b


def _bcast_row(x, r):
    return jnp.broadcast_to(x[r:r + 1, :], (SUB, LANE))


def _lru_prepare(xb_ref, w4_ref, vec, xc_ref, af_ref, uf_ref, ab_ref, ub_ref, T):
    sp_f = _softplus_neg(vec[9:10, :])
    sp_b = _softplus_neg(vec[10:11, :])
    w4 = w4_ref[0]

    def body(c, carry):
        t0 = c * LRU_CH
        xc = _lru_conv(xb_ref, vec, t0, T)
        if xc_ref is not None:
            xc_ref[_rows(t0, LRU_CH), :] = xc
        pre = _mm(xc, w4)
        _, i, a, s = _lru_dir(pre[:, 0:128] + vec[5:6, :], pre[:, 128:256] + vec[6:7, :], sp_f)
        af_ref[_rows(t0, LRU_CH), :] = a
        uf_ref[_rows(t0, LRU_CH), :] = s * (i * xc)
        _, i, a, s = _lru_dir(pre[:, 256:384] + vec[7:8, :], pre[:, 384:512] + vec[8:9, :], sp_b)
        ab_ref[_rows(t0, LRU_CH), :] = a
        ub_ref[_rows(t0, LRU_CH), :] = s * (i * xc)
        return carry

    lax.fori_loop(0, T // LRU_CH, body, 0)


def _lru_scan(af_ref, uf_ref, ab_ref, ub_ref, T):
    nt = T // SUB
    row = lax.broadcasted_iota(jnp.int32, (SUB, LANE), 0)

    def body(j, carry):
        hf, hb = carry
        sf = _rows(j * SUB, SUB)
        sb = _rows((nt - 1 - j) * SUB, SUB)
        a, b = _scan_tile(af_ref[sf, :], uf_ref[sf, :], False, row)
        h = a * hf + b
        uf_ref[sf, :] = h
        hf = _bcast_row(h, SUB - 1)
        a, b = _scan_tile(ab_ref[sb, :], ub_ref[sb, :], True, row)
        h = a * hb + b
        ub_ref[sb, :] = h
        hb = _bcast_row(h, 0)
        return hf, hb

    z = jnp.zeros((SUB, LANE), F32)
    lax.fori_loop(0, nt, body, (z, z))


def _lru_fwd_call(proj, vec, w4):
    T = proj.shape[0]

    def body(xb_ref, gate_ref, vec_ref, w4_ref, y_ref, af_ref, uf_ref, ab_ref, ub_ref):
        vec = vec_ref[...]
        _lru_prepare(xb_ref, w4_ref, vec, None, af_ref, uf_ref, ab_ref, ub_ref, T)
        _lru_scan(af_ref, uf_ref, ab_ref, ub_ref, T)

        def out(c, carry):
            rows = _rows(c * LRU_CH, LRU_CH)
            gl, _ = _gelu_parts(gate_ref[rows, :])
            y_ref[rows, :] = (uf_ref[rows, :] + ub_ref[rows, :]) * gl
            return carry

        lax.fori_loop(0, T // LRU_CH, out, 0)

    return pl.pallas_call(
        body, name="lru_fwd", grid=(LRU_WIDTH // LANE,),
        in_specs=[_strip(T, lambda j: j), _strip(T, lambda j: j + 3),
                  pl.BlockSpec((16, LANE), lambda j: (0, j)),
                  pl.BlockSpec((1, LANE, 4 * LANE), lambda j: (j, 0, 0))],
        out_specs=_strip(T, lambda j: j),
        out_shape=jax.ShapeDtypeStruct((T, LRU_WIDTH), F32),
        scratch_shapes=[pltpu.VMEM((T, LANE), F32)] * 4,
        compiler_params=_cparams(("arbitrary",)),
    )(proj, proj, vec, w4)


def _lru_bwd_call(proj, dycat, vec, w4, after):
    T = proj.shape[0]
    nt = T // SUB
    nch = T // LRU_CH

    def body(xb_ref, gate_ref, dy_ref, vec_ref, w4_ref, after_ref, dp_ref, dvec_ref, dw4_ref,
             xc_ref, af_ref, hf_ref, ab_ref, hb_ref, dh_ref):
        dxb_ref, dgate_ref = dp_ref.at[0], dp_ref.at[1]
        vec = vec_ref[...]
        _lru_prepare(xb_ref, w4_ref, vec, xc_ref, af_ref, hf_ref, ab_ref, hb_ref, T)
        _lru_scan(af_ref, hf_ref, ab_ref, hb_ref, T)

        def gate_bwd(c, carry):
            rows = _rows(c * LRU_CH, LRU_CH)
            gl, dgl = _gelu_parts(gate_ref[rows, :])
            dy = dy_ref[rows, :]
            dgate_ref[rows, :] = (dy * (hf_ref[rows, :] + hb_ref[rows, :]) * dgl).astype(dgate_ref.dtype)
            dh_ref[rows, :] = dy * gl
            return carry

        lax.fori_loop(0, nch, gate_bwd, 0)

        row = lax.broadcasted_iota(jnp.int32, (SUB, LANE), 0)

        def adj(j, carry):
            gf, a_next, gb, a_prev = carry
            tf = nt - 1 - j
            sf = _rows(tf * SUB, SUB)
            a_t = af_ref[sf, :]
            h_t = hf_ref[sf, :]
            coef = jnp.where(row == SUB - 1, a_next, pltpu.roll(a_t, SUB - 1, 0))
            ac, bc = _scan_tile(coef, dh_ref[sf, :], True, row)
            g = ac * gf + bc
            h_prev = hf_ref[_rows(jnp.maximum(tf - 1, 0) * SUB, SUB), :]
            h_prev = jnp.where(tf > 0, _bcast_row(h_prev, SUB - 1), 0.0)
            hs = jnp.where(row == 0, h_prev, pltpu.roll(h_t, 1, 0))
            af_ref[sf, :] = g * hs
            hf_ref[sf, :] = g
            gf = _bcast_row(g, 0)
            a_next = _bcast_row(a_t, 0)
            sb = _rows(j * SUB, SUB)
            a_t = ab_ref[sb, :]
            h_t = hb_ref[sb, :]
            coef = jnp.where(row == 0, a_prev, pltpu.roll(a_t, 1, 0))
            ac, bc = _scan_tile(coef, dh_ref[sb, :], False, row)
            g = ac * gb + bc
            h_next = hb_ref[_rows(jnp.minimum(j + 1, nt - 1) * SUB, SUB), :]
            h_next = jnp.where(j < nt - 1, _bcast_row(h_next, 0), 0.0)
            hs = jnp.where(row == SUB - 1, h_next, pltpu.roll(h_t, SUB - 1, 0))
            ab_ref[sb, :] = g * hs
            hb_ref[sb, :] = g
            gb = _bcast_row(g, SUB - 1)
            a_prev = _bcast_row(a_t, SUB - 1)
            return gf, a_next, gb, a_prev

        z = jnp.zeros((SUB, LANE), F32)
        lax.fori_loop(0, nt, adj, (z, z, z, z))

        sp_f = _softplus_neg(vec[9:10, :])
        sp_b = _softplus_neg(vec[10:11, :])
        w4 = w4_ref[0]
        dw4_ref[...] = jnp.zeros_like(dw4_ref)

        def one_dir(pre_a, pre_x, sp, xc, du, da):
            r, i, a, s = _lru_dir(pre_a, pre_x, sp)
            d_i = du * s * xc
            dxc = du * s * i
            d_s = du * i * xc
            d_log = da * a - d_s * (a * a) / s
            d_r = d_log * (-LRU_C) * sp
            d_sp = jnp.sum(d_log * (-LRU_C) * r, axis=0, keepdims=True)
            return d_r * r * (1.0 - r), d_i * i * (1.0 - i), dxc, d_sp

        def gates_bwd(c, carry):
            db, dspf, dspb = carry
            rows = _rows(c * LRU_CH, LRU_CH)
            xc = xc_ref[rows, :]
            pre = _mm(xc, w4)
            dpa_f, dpx_f, dxc_f, d_sp_f = one_dir(pre[:, 0:128] + vec[5:6, :], pre[:, 128:256] + vec[6:7, :],
                                                  sp_f, xc, hf_ref[rows, :], af_ref[rows, :])
            dpa_b, dpx_b, dxc_b, d_sp_b = one_dir(pre[:, 256:384] + vec[7:8, :], pre[:, 384:512] + vec[8:9, :],
                                                  sp_b, xc, hb_ref[rows, :], ab_ref[rows, :])
            dpre = jnp.concatenate([dpa_f, dpx_f, dpa_b, dpx_b], axis=1)
            dw4_ref[0] += _mm_tn(xc, dpre)
            dh_ref[rows, :] = dxc_f + dxc_b + _mm_nt(dpre, w4)
            return db + jnp.sum(dpre, axis=0, keepdims=True), dspf + d_sp_f, dspb + d_sp_b

        z1 = jnp.zeros((1, LANE), F32)
        db, dspf, dspb = lax.fori_loop(0, nch, gates_bwd, (jnp.zeros((1, 4 * LANE), F32), z1, z1))

        def conv_bwd(c, carry):
            t0 = c * LRU_CH
            rows = _rows(t0, LRU_CH)
            dwin = _window(dh_ref, t0, LRU_CH, T)
            xwin = _window(xb_ref, t0, LRU_CH, T)
            dxc = dh_ref[rows, :]
            dxb = jnp.zeros((LRU_CH, LANE), F32)
            out = []
            for j in range(CONV_WIDTH):
                off = j - CONV_WIDTH // 2
                dxb = dxb + _tap(dwin, -off, LRU_CH) * vec[j:j + 1, :]
                out.append(carry[j] + jnp.sum(dxc * _tap(xwin, off, LRU_CH), axis=0, keepdims=True))
            dxb_ref[rows, :] = dxb.astype(dxb_ref.dtype)
            out.append(carry[CONV_WIDTH] + jnp.sum(dxc, axis=0, keepdims=True))
            return tuple(out)

        dconv = lax.fori_loop(0, nch, conv_bwd, (z1,) * (CONV_WIDTH + 1))
        dlam_f = dspf * (-_sigmoid(-vec[9:10, :]))
        dlam_b = dspb * (-_sigmoid(-vec[10:11, :]))
        dvec_ref[...] = jnp.concatenate(
            list(dconv) + [db[:, 0:128], db[:, 128:256], db[:, 256:384], db[:, 384:512], dlam_f, dlam_b,
                           jnp.zeros((5, LANE), F32)], axis=0)

    ns = LRU_WIDTH // LANE
    return pl.pallas_call(
        body, name="lru_bwd", grid=(ns,),
        in_specs=[_strip(T, lambda j: j), _strip(T, lambda j: j + 3), _strip(T, lambda j: j),
                  pl.BlockSpec((16, LANE), lambda j: (0, j)),
                  pl.BlockSpec((1, LANE, 4 * LANE), lambda j: (j, 0, 0)),
                  pl.BlockSpec(memory_space=pl.ANY)],
        out_specs=[pl.BlockSpec((2, T, LANE), lambda j: (0, 0, j), pipeline_mode=pl.Buffered(1)),
                   pl.BlockSpec((16, LANE), lambda j: (0, j)),
                   pl.BlockSpec((1, LANE, 4 * LANE), lambda j: (j, 0, 0))],
        out_shape=[jax.ShapeDtypeStruct((2, T, LRU_WIDTH), _BF),
                   jax.ShapeDtypeStruct((16, LRU_WIDTH), F32), jax.ShapeDtypeStruct((ns, LANE, 4 * LANE), F32)],
        scratch_shapes=[pltpu.VMEM((T, LANE), F32)] * 6,
        compiler_params=_cparams(("arbitrary",)),
    )(proj, proj, dycat, vec, w4, after)


def _lru_pack(cw, cb, wa, ba, wx, bx, lam):
    vec = jnp.concatenate([cw, cb[None], ba[0:1], bx[0:1], ba[1:2], bx[1:2], lam, jnp.zeros((5, LRU_WIDTH), F32)], axis=0)
    eye = jnp.eye(2, dtype=F32)
    mats = []
    for w in (wa[0], wx[0], wa[1], wx[1]):
        bd = jnp.einsum("jsio,st->jsito", w.reshape(3, 2, 64, 64), eye)
        mats.append(bd.reshape(3, LANE, LANE))
    return vec, jnp.concatenate(mats, axis=2).astype(_BF)


def _lru_unpack(dvec, dw4):
    def blocks(m):
        m = m.reshape(3, 2, 64, 2, 64)
        return jnp.stack([m[:, 0, :, 0, :], m[:, 1, :, 1, :]], axis=1).reshape(6, 64, 64)
    parts = [blocks(dw4[:, :, k * LANE:(k + 1) * LANE]) for k in range(4)]
    dwa = jnp.stack([parts[0], parts[2]])
    dwx = jnp.stack([parts[1], parts[3]])
    dba = jnp.stack([dvec[5], dvec[7]])
    dbx = jnp.stack([dvec[6], dvec[8]])
    return dvec[0:4], dvec[4], dwa, dba, dwx, dbx, dvec[9:11]


RC = RET_CHUNK


def _ret_tables(T):
    half = HEAD_DIM // 2
    pos = jnp.arange(T, dtype=F32)
    inv_freq = ROPE_BASE ** (-jnp.arange(half, dtype=F32) / half)
    ang = pos[:, None] * inv_freq[None, :]
    cos = jnp.tile(jnp.cos(ang), (1, 4))
    sin = jnp.tile(jnp.concatenate([-jnp.sin(ang), jnp.sin(ang)], axis=1), (1, 2))
    log_g = jnp.log1p(-jnp.exp2(-5.0 - jnp.arange(RET_HEADS, dtype=F32)))
    idx = jnp.arange(RC, dtype=F32)
    dec = jnp.exp(jnp.abs(idx[:, None] - idx[None, :]) * log_g[:, None, None])
    lg = jnp.repeat(log_g, HEAD_DIM).reshape(3, 1, LANE)
    col = idx[None, :, None]
    rtab = jnp.stack([jnp.exp((RC - 1 - col) * lg), jnp.exp(col * lg),
                      jnp.exp((col + 1.0) * lg), jnp.exp((RC - col) * lg)], axis=1)
    gch = jnp.broadcast_to(jnp.exp(RC * lg), (3, SUB, LANE))
    return cos, sin, dec, rtab, gch


def _swap32(x, lane):
    return jnp.where((lane & 32) == 0, pltpu.roll(x, LANE - 32, 1), pltpu.roll(x, 32, 1))


def _head_mean(x, m0, m1):
    s0 = jnp.sum(x * m0, axis=-1, keepdims=True)
    s1 = jnp.sum(x * m1, axis=-1, keepdims=True)
    return (s0 * m0 + s1 * m1) * (1.0 / HEAD_DIM)


def _ret_masks():
    lane = lax.broadcasted_iota(jnp.int32, (RC, LANE), 1)
    m0 = (lane < HEAD_DIM).astype(F32)
    r = lax.broadcasted_iota(jnp.int32, (LANE, LANE), 0) // HEAD_DIM
    c = lax.broadcasted_iota(jnp.int32, (LANE, LANE), 1) // HEAD_DIM
    return lane, m0, 1.0 - m0, (r == c).astype(F32)


def _ret_specs(T):
    const = lambda shape, imap: pl.BlockSpec(shape, imap)
    return [_strip(T, lambda j: j + 6), _strip(T, lambda j: j + 9), _strip(T, lambda j: j + 12),
            _strip(T, lambda j: j + 15),
            pl.BlockSpec((T, LANE), lambda j: (0, 0), pipeline_mode=pl.Buffered(1)),
            pl.BlockSpec((T, LANE), lambda j: (0, 0), pipeline_mode=pl.Buffered(1)),
            const((2, RC, RC), lambda j: (j, 0, 0)),
            const((1, 4, RC, LANE), lambda j: (j, 0, 0, 0)),
            const((1, SUB, LANE), lambda j: (j, 0, 0)),
            const((SUB, LANE), lambda j: (0, j))]


def _ret_fwd_call(proj, tables, gnw8):
    T = proj.shape[0]
    nc = T // RC
    cos, sin, dec, rtab, gch = tables

    def body(q_ref, k_ref, v_ref, g_ref, cos_ref, sin_ref, dec_ref, rtab_ref, gch_ref, gnw_ref, y_ref, stf_ref):
        lane, m0, m1, bd = _ret_masks()
        gch_v = gch_ref[0][0:1, :]
        gnw = gnw_ref[0:1, :]
        dkf, dkb, dqf, dqb = rtab_ref[0, 0], rtab_ref[0, 1], rtab_ref[0, 2], rtab_ref[0, 3]

        def rope(x, rows):
            return x * cos_ref[rows, :] + _swap32(x, lane) * sin_ref[rows, :]

        def pass_a(n, st):
            rows = _rows(n * RC, RC)
            stf_ref[n] = st
            kr = rope(k_ref[rows, :], rows) * (HEAD_DIM ** -0.5)
            return gch_v * st + _mm_tn(kr * dkf, v_ref[rows, :]) * bd

        _loop2(nc, pass_a, jnp.zeros((LANE, LANE), F32))

        def pass_b(i, stb):
            n = nc - 1 - i
            rows = _rows(n * RC, RC)
            qr = rope(q_ref[rows, :], rows)
            kr = rope(k_ref[rows, :], rows) * (HEAD_DIM ** -0.5)
            v = v_ref[rows, :]
            o = _mm(qr * dqf, stf_ref[n]) + _mm(qr * dqb, stb)
            for h, m in ((0, m0), (1, m1)):
                s = _mm_nt(qr * m, kr) * dec_ref[h]
                o = o + _mm(s, v * m)
            oc = o - _head_mean(o, m0, m1)
            on = oc * lax.rsqrt(_head_mean(oc * oc, m0, m1) + GN_EPS)
            g = g_ref[rows, :]
            y_ref[rows, :] = (g * _sigmoid(g)) * (on * gnw)
            return gch_v * stb + _mm_tn(kr * dkb, v) * bd

        _loop2(nc, pass_b, jnp.zeros((LANE, LANE), F32))

    return pl.pallas_call(
        body, name="ret_fwd", grid=(RET_WIDTH // LANE,),
        in_specs=_ret_specs(T),
        out_specs=_strip(T, lambda j: j),
        out_shape=jax.ShapeDtypeStruct((T, RET_WIDTH), F32),
        scratch_shapes=[pltpu.VMEM((nc, LANE, LANE), F32)],
        compiler_params=_cparams(("arbitrary",)),
    )(proj, proj, proj, proj, cos, sin, dec, rtab, gch, gnw8)


def _ret_bwd_call(proj, dycat, tables, gnw8):
    T = proj.shape[0]
    nc = T // RC
    cos, sin, dec, rtab, gch = tables

    def body(q_ref, k_ref, v_ref, g_ref, cos_ref, sin_ref, dec_ref, rtab_ref, gch_ref, gnw_ref, dy_ref,
             dp_ref, dgnw_ref, stf_ref, dstb_ref, dkr_ref, dv_ref):
        lane, m0, m1, bd = _ret_masks()
        gch_v = gch_ref[0][0:1, :]
        gnw = gnw_ref[0:1, :]
        dkf, dkb, dqf, dqb = rtab_ref[0, 0], rtab_ref[0, 1], rtab_ref[0, 2], rtab_ref[0, 3]
        scale = HEAD_DIM ** -0.5
        zst = jnp.zeros((LANE, LANE), F32)

        def rope(x, rows):
            return x * cos_ref[rows, :] + _swap32(x, lane) * sin_ref[rows, :]

        def rope_t(d, rows):
            return d * cos_ref[rows, :] + _swap32(d * sin_ref[rows, :], lane)

        def pass_a(n, st):
            rows = _rows(n * RC, RC)
            stf_ref[n] = st
            kr = rope(k_ref[rows, :], rows) * scale
            return gch_v * st + _mm_tn(kr * dkf, v_ref[rows, :]) * bd

        _loop2(nc, pass_a, zst)

        def pass_b(i, carry):
            stb, d_f, dgnw = carry
            n = nc - 1 - i
            rows = _rows(n * RC, RC)
            qr = rope(q_ref[rows, :], rows)
            kr = rope(k_ref[rows, :], rows) * scale
            v = v_ref[rows, :]
            stf = stf_ref[n]
            qf = qr * dqf
            qb = qr * dqb
            o = _mm(qf, stf) + _mm(qb, stb)
            s_h = []
            for h, m in ((0, m0), (1, m1)):
                s = _mm_nt(qr * m, kr) * dec_ref[h]
                s_h.append(s)
                o = o + _mm(s, v * m)
            oc = o - _head_mean(o, m0, m1)
            rstd = lax.rsqrt(_head_mean(oc * oc, m0, m1) + GN_EPS)
            on = oc * rstd
            g = g_ref[rows, :]
            sg = _sigmoid(g)
            dy = dy_ref[rows, :]
            dp_ref[3, rows, :] = (dy * (on * gnw) * (sg * (1.0 + g * (1.0 - sg)))).astype(dp_ref.dtype)
            t = dy * (g * sg)
            dgnw = dgnw + jnp.sum(t * on, axis=0, keepdims=True)
            don = t * gnw
            do = rstd * (don - _head_mean(don, m0, m1) - on * _head_mean(don * on, m0, m1))
            dqr = _mm_nt(do, stf) * dqf + _mm_nt(do, stb) * dqb
            dkr = _mm_nt(v, d_f) * dkf
            dv = _mm(kr * dkf, d_f)
            for h, m in ((0, m0), (1, m1)):
                ds = _mm_nt(do * m, v) * dec_ref[h]
                dqr = dqr + _mm(ds, kr * m)
                dkr = dkr + _mm_tn(ds, qr * m)
                dv = dv + _mm_tn(s_h[h], do * m)
            dp_ref[0, rows, :] = rope_t(dqr, rows).astype(dp_ref.dtype)
            dkr_ref[rows, :] = dkr
            dv_ref[rows, :] = dv
            dstb_ref[n] = _mm_tn(qb, do) * bd
            d_f = _mm_tn(qf, do) * bd + gch_v * d_f
            stb = gch_v * stb + _mm_tn(kr * dkb, v) * bd
            return stb, d_f, dgnw

        _, _, dgnw = _loop2(nc, pass_b, (zst, zst, jnp.zeros((1, LANE), F32)))
        dgnw_ref[...] = jnp.concatenate([dgnw, jnp.zeros((SUB - 1, LANE), F32)], axis=0)

        def pass_c(n, d_b):
            rows = _rows(n * RC, RC)
            kr = rope(k_ref[rows, :], rows) * scale
            v = v_ref[rows, :]
            dkr = dkr_ref[rows, :] + _mm_nt(v, d_b) * dkb
            dp_ref[1, rows, :] = (rope_t(dkr, rows) * scale).astype(dp_ref.dtype)
            dp_ref[2, rows, :] = (dv_ref[rows, :] + _mm(kr * dkb, d_b)).astype(dp_ref.dtype)
            return dstb_ref[n] + gch_v * d_b

        _loop2(nc, pass_c, zst)

    return pl.pallas_call(
        body, name="ret_bwd", grid=(RET_WIDTH // LANE,),
        in_specs=_ret_specs(T) + [_strip(T, lambda j: j + 3)],
        out_specs=[pl.BlockSpec((4, T, LANE), lambda j: (0, 0, j), pipeline_mode=pl.Buffered(1)),
                   pl.BlockSpec((SUB, LANE), lambda j: (0, j))],
        out_shape=[jax.ShapeDtypeStruct((4, T, RET_WIDTH), _BF), jax.ShapeDtypeStruct((SUB, RET_WIDTH), F32)],
        scratch_shapes=[pltpu.VMEM((nc, LANE, LANE), F32), pltpu.VMEM((nc, LANE, LANE), F32),
                        pltpu.VMEM((T, LANE), F32), pltpu.VMEM((T, LANE), F32)],
        compiler_params=_cparams(("arbitrary",)),
    )(proj, proj, proj, proj, cos, sin, dec, rtab, gch, gnw8, dycat)


NA_Q = 2 * GRID_W
NA_WROWS = 10
NA_K = NA_WROWS * GRID_W
NA_CHUNKS = NA_K // LANE
NA_TYPES = 5


def _na_onehots(rows_n):
    reps = [(0, 0), (2, 0), (4, 0), (rows_n - 4, rows_n - NA_WROWS), (rows_n - 2, rows_n - NA_WROWS)]
    rm = np.zeros((NA_TYPES, 2, NA_WROWS, 2 * NA_KH - 1), np.float32)
    for t, (r, ws) in enumerate(reps):
        for qh in range(2):
            qrow = r + qh
            rstart = min(max(qrow - NA_KH // 2, 0), rows_n - NA_KH)
            for kh in range(NA_WROWS):
                krow = ws + kh
                if rstart <= krow < rstart + NA_KH:
                    rm[t, qh, kh, krow - qrow + NA_KH - 1] = 1.0
    cm = np.zeros((GRID_W, GRID_W, 2 * NA_KW - 1), np.float32)
    for qc in range(GRID_W):
        cstart = min(max(qc - NA_KW // 2, 0), GRID_W - NA_KW)
        for kc in range(cstart, cstart + NA_KW):
            cm[qc, kc, kc - qc + NA_KW - 1] = 1.0
    rm2 = rm.reshape(NA_TYPES, 2, NA_CHUNKS, 2, 2 * NA_KH - 1)
    cm2 = np.zeros((GRID_W, LANE, 2, 2 * NA_KW - 1), np.float32)
    for z in range(2):
        cm2[:, z * GRID_W:(z + 1) * GRID_W, z, :] = cm
    return rm2, cm2


def _na_bias_tables(rpb, rows_n):
    rm, cm = _na_onehots(rows_n)
    val = jnp.einsum("hab,tqpza,xkzb->htpqxk", rpb, rm, cm, precision=lax.Precision.HIGHEST)
    valid = np.einsum("tqpz,xkz->tpqxk", rm.sum(-1), cm.sum(-1)) > 0.5
    return jnp.where(valid[None], val, NEG).reshape(2, 2, NA_TYPES, NA_CHUNKS, NA_Q, LANE)


def _na_bias_grad(dtab, rows_n):
    rm, cm = _na_onehots(rows_n)
    d6 = dtab.reshape(NA_HEADS, NA_TYPES, NA_CHUNKS, 2, GRID_W, LANE)
    return jnp.einsum("htpqxk,tqpza,xkzb->hab", d6, rm, cm, precision=lax.Precision.HIGHEST)


def _na_bias(b_ref, h, typ):
    return jnp.concatenate([b_ref[0, h, typ, c] for c in range(NA_CHUNKS)], axis=1)


def _na_step(p, npairs, rows_n):
    ws = jnp.clip(2 * p - NA_KH // 2, 0, rows_n - NA_WROWS)
    koff = pl.multiple_of(ws * GRID_W, LANE)
    typ = jnp.where(p == 0, 0, jnp.where(p == 1, 1, jnp.where(p == npairs - 2, 3, jnp.where(p == npairs - 1, 4, 2))))
    return _rows(p * NA_Q, NA_Q), pl.ds(koff, NA_K), typ


def _na_fwd_call(proj, btab):
    T = proj.shape[0]
    npairs, rows_n = T // NA_Q, T // GRID_W

    def body(q_ref, k_ref, v_ref, b_ref, o_ref):
        lane = lax.broadcasted_iota(jnp.int32, (NA_Q, LANE), 1)
        m0 = (lane < HEAD_DIM).astype(F32)
        m1 = 1.0 - m0

        def step(p, carry):
            qrows, krows, typ = _na_step(p, npairs, rows_n)
            q = q_ref[qrows, :]
            kw = k_ref[krows, :]
            vw = v_ref[krows, :]
            o = jnp.zeros((NA_Q, LANE), F32)
            for h, m in ((0, m0), (1, m1)):
                s = _mm_nt(q * m, kw) * (HEAD_DIM ** -0.5) + _na_bias(b_ref, h, typ)
                e = jnp.exp(s - jnp.max(s, axis=-1, keepdims=True))
                pr = e / jnp.sum(e, axis=-1, keepdims=True)
                o = o + _mm(pr, vw) * m
            o_ref[qrows, :] = o
            return carry

        _loop2(npairs, step, 0)

    return pl.pallas_call(
        body, name="na_fwd", grid=(NA_WIDTH // LANE,),
        in_specs=[_strip(T, lambda j: j + 18), _strip(T, lambda j: j + 20), _strip(T, lambda j: j + 22),
                  pl.BlockSpec((1, 2, NA_TYPES, NA_CHUNKS, NA_Q, LANE), lambda j: (j, 0, 0, 0, 0, 0))],
        out_specs=_strip(T, lambda j: j),
        out_shape=jax.ShapeDtypeStruct((T, NA_WIDTH), F32),
        compiler_params=_cparams(("arbitrary",)),
    )(proj, proj, proj, btab)


def _na_bwd_call(proj, dycat, btab):
    T = proj.shape[0]
    npairs, rows_n = T // NA_Q, T // GRID_W
    scale = HEAD_DIM ** -0.5

    def body(q_ref, k_ref, v_ref, do_ref, b_ref, dq_ref, dk_ref, dv_ref, db_ref, dka_ref, dva_ref):
        lane = lax.broadcasted_iota(jnp.int32, (NA_Q, LANE), 1)
        m0 = (lane < HEAD_DIM).astype(F32)
        m1 = 1.0 - m0
        dka_ref[...] = jnp.zeros_like(dka_ref)
        dva_ref[...] = jnp.zeros_like(dva_ref)
        db_ref[...] = jnp.zeros_like(db_ref)

        def step(p, carry):
            qrows, krows, typ = _na_step(p, npairs, rows_n)
            q = q_ref[qrows, :]
            do = do_ref[qrows, :]
            kw = k_ref[krows, :]
            vw = v_ref[krows, :]
            dq = jnp.zeros((NA_Q, LANE), F32)
            dk = jnp.zeros((NA_K, LANE), F32)
            dv = jnp.zeros((NA_K, LANE), F32)
            for h, m in ((0, m0), (1, m1)):
                qm = q * m
                dom = do * m
                s = _mm_nt(qm, kw) * scale + _na_bias(b_ref, h, typ)
                e = jnp.exp(s - jnp.max(s, axis=-1, keepdims=True))
                pr = e / jnp.sum(e, axis=-1, keepdims=True)
                dpr = _mm_nt(dom, vw)
                ds = pr * (dpr - jnp.sum(pr * dpr, axis=-1, keepdims=True))
                for c in range(NA_CHUNKS):
                    db_ref[0, h, typ, c] += ds[:, c * LANE:(c + 1) * LANE]
                dsb = (ds * scale).astype(_BF)
                dq = dq + _mm(dsb, kw) * m
                dk = dk + _mm_tn(dsb, qm)
                dv = dv + _mm_tn(pr, dom)
            dq_ref[qrows, :] = dq.astype(dq_ref.dtype)
            dka_ref[krows, :] += dk
            dva_ref[krows, :] += dv
            return carry

        _loop2(npairs, step, 0)
        dk_ref[...] = dka_ref[...].astype(dk_ref.dtype)
        dv_ref[...] = dva_ref[...].astype(dv_ref.dtype)

    tab = pl.BlockSpec((1, 2, NA_TYPES, NA_CHUNKS, NA_Q, LANE), lambda j: (j, 0, 0, 0, 0, 0))
    out = lambda col: pl.BlockSpec((T, LANE), lambda j: (0, col(j)), pipeline_mode=pl.Buffered(1))
    dq, dk, dv, db = pl.pallas_call(
        body, name="na_bwd", grid=(NA_WIDTH // LANE,),
        in_specs=[_strip(T, lambda j: j + 18), _strip(T, lambda j: j + 20), _strip(T, lambda j: j + 22),
                  _strip(T, lambda j: j + 6), tab],
        out_specs=[out(lambda j: j), out(lambda j: j), out(lambda j: j), tab],
        out_shape=[jax.ShapeDtypeStruct((T, NA_WIDTH), _BF)] * 3
        + [jax.ShapeDtypeStruct((2, 2, NA_TYPES, NA_CHUNKS, NA_Q, LANE), F32)],
        scratch_shapes=[pltpu.VMEM((T, LANE), F32), pltpu.VMEM((T, LANE), F32)],
        compiler_params=_cparams(("arbitrary",)),
    )(proj, proj, proj, dycat, btab)
    dp = jnp.concatenate([dq, dk, dv], axis=1)
    return jnp.stack([dp[:, :W_BLK], dp[:, W_BLK:]]), db


W_BLK = IN_WIDTH // N_DEV
TM = 512


def _ln_fwd(z, g, b):
    zc = z - jnp.mean(z, axis=-1, keepdims=True)
    var = jnp.mean(zc * zc, axis=-1, keepdims=True)
    return zc * lax.rsqrt(var + LN_EPS) * g + b


def _ln_bwd(dy, z, g):
    zc = z - jnp.mean(z, axis=-1, keepdims=True)
    rstd = lax.rsqrt(jnp.mean(zc * zc, axis=-1, keepdims=True) + LN_EPS)
    xhat = zc * rstd
    dxh = dy * g
    dz = rstd * (dxh - jnp.mean(dxh, axis=-1, keepdims=True) - xhat * jnp.mean(dxh * xhat, axis=-1, keepdims=True))
    return dz, dy * xhat


def _row_tile(T):
    return 1024 if T % 1024 == 0 else TM


def _inproj_call(xb, w, after):
    T = xb.shape[0]
    tm = _row_tile(T)

    def body(x_ref, w_ref, after_ref, o_ref):
        o_ref[...] = _mm(x_ref[...], w_ref[...])

    return pl.pallas_call(
        body, name="inproj", grid=(T // tm, N_DEV),
        in_specs=[pl.BlockSpec((tm, D_MODEL), lambda i, n: (i, 0)),
                  pl.BlockSpec((None, D_MODEL, W_BLK), lambda i, n: (n, 0, 0)),
                  pl.BlockSpec(memory_space=pl.ANY)],
        out_specs=pl.BlockSpec((tm, W_BLK), lambda i, n: (i, n)),
        out_shape=jax.ShapeDtypeStruct((T, IN_WIDTH), F32),
        compiler_params=_cparams(("parallel", "arbitrary")),
    )(xb, w, after)


def _vec_spec():
    return pl.BlockSpec((1, D_MODEL), lambda *_: (0, 0))


def _outproj_ln_call(y_lru, y_ret, y_na, x, w, g, b, after):
    T = x.shape[0]

    def body(yl_ref, yr_ref, yn_ref, x_ref, w_ref, g_ref, b_ref, after_ref, z_ref, x1_ref, x1b_ref, yc_ref):
        yc_ref[:, 0:LRU_WIDTH] = yl_ref[...].astype(yc_ref.dtype)
        yc_ref[:, LRU_WIDTH:LRU_WIDTH + RET_WIDTH] = yr_ref[...].astype(yc_ref.dtype)
        yc_ref[:, LRU_WIDTH + RET_WIDTH:] = yn_ref[...].astype(yc_ref.dtype)
        z = ALPHA * x_ref[...] + _mm(yc_ref[...], w_ref[...].reshape(D_MODEL, D_MODEL))
        z_ref[...] = z
        x1 = _ln_fwd(z, g_ref[...], b_ref[...])
        x1_ref[...] = x1
        x1b_ref[...] = x1.astype(x1b_ref.dtype)

    row = lambda w_: pl.BlockSpec((TM, w_), lambda i: (i, 0))
    return pl.pallas_call(
        body, name="outproj_ln", grid=(T // TM,),
        in_specs=[row(LRU_WIDTH), row(RET_WIDTH), row(NA_WIDTH), row(D_MODEL),
                  pl.BlockSpec((N_DEV, LANE, D_MODEL), lambda i: (0, 0, 0)), _vec_spec(), _vec_spec(),
                  pl.BlockSpec(memory_space=pl.ANY)],
        out_specs=[row(D_MODEL)] * 4,
        out_shape=[jax.ShapeDtypeStruct((T, D_MODEL), F32), jax.ShapeDtypeStruct((T, D_MODEL), F32),
                   jax.ShapeDtypeStruct((T, D_MODEL), _BF), jax.ShapeDtypeStruct((T, D_MODEL), _BF)],
        compiler_params=_cparams(("parallel",)),
    )(y_lru, y_ret, y_na, x, w, g, b, after)


def _ffn_ln_call(x1, x1b, wg, wu, wd, g, b):
    T = x1.shape[0]

    def body(x_ref, xb_ref, wg_ref, wu_ref, wd_ref, g_ref, b_ref, z_ref, x2_ref, x2b_ref, acc_ref):
        n = pl.program_id(1)

        @pl.when(n == 0)
        def _():
            acc_ref[...] = jnp.zeros_like(acc_ref)

        xb = xb_ref[...]
        gp = _mm(xb, wg_ref[...])
        hid = gp * _sigmoid(gp) * _mm(xb, wu_ref[...])
        acc_ref[...] += _mm(hid, wd_ref[...])

        @pl.when(n == N_DEV - 1)
        def _():
            z = ALPHA * x_ref[...] + acc_ref[...]
            z_ref[...] = z
            x2 = _ln_fwd(z, g_ref[...], b_ref[...])
            x2_ref[...] = x2
            x2b_ref[...] = x2.astype(x2b_ref.dtype)

    row = pl.BlockSpec((TM, D_MODEL), lambda i, n: (i, 0))
    return pl.pallas_call(
        body, name="ffn_ln", grid=(T // TM, N_DEV),
        in_specs=[row, row,
                  pl.BlockSpec((None, D_MODEL, W_BLK), lambda i, n: (n, 0, 0)),
                  pl.BlockSpec((None, D_MODEL, W_BLK), lambda i, n: (n, 0, 0)),
                  pl.BlockSpec((None, W_BLK, D_MODEL), lambda i, n: (n, 0, 0)), _vec_spec(), _vec_spec()],
        out_specs=[row] * 3,
        out_shape=[jax.ShapeDtypeStruct((T, D_MODEL), F32), jax.ShapeDtypeStruct((T, D_MODEL), F32),
                   jax.ShapeDtypeStruct((T, D_MODEL), _BF)],
        scratch_shapes=[pltpu.VMEM((TM, D_MODEL), F32)],
        compiler_params=_cparams(("parallel", "arbitrary")),
    )(x1, x1b, wg, wu, wd, g, b)


def _loss_call(y, t):
    T = y.shape[0]

    def body(y_ref, t_ref, dy_ref, loss_ref):
        @pl.when(pl.program_id(0) == 0)
        def _():
            loss_ref[...] = jnp.zeros_like(loss_ref)

        err = y_ref[...] - t_ref[...]
        dy_ref[...] = err * (1.0 / D_MODEL)
        part = 0.5 * jnp.sum(jnp.mean(err * err, axis=-1, keepdims=True), axis=0, keepdims=True)
        loss_ref[...] += jnp.broadcast_to(part, loss_ref.shape)

    row = pl.BlockSpec((TM, D_MODEL), lambda i: (i, 0))
    return pl.pallas_call(
        body, name="loss", grid=(T // TM,),
        in_specs=[row, row],
        out_specs=[row, pl.BlockSpec((SUB, LANE), lambda i: (0, 0))],
        out_shape=[jax.ShapeDtypeStruct((T, D_MODEL), F32), jax.ShapeDtypeStruct((SUB, LANE), F32)],
        compiler_params=_cparams(("arbitrary",)),
    )(y, t)


def _ffn_bwd_call(dx2, z2, x1, x1b, wg, wu, wd, g, after):
    T = x1.shape[0]

    def body(dx2_ref, z_ref, x_ref, xb_ref, wg_ref, wu_ref, wd_ref, g_ref, after_ref,
             dx1_ref, dgp_ref, dup_ref, hid_ref, dzb_ref, dln_ref, acc_ref):
        i, n = pl.program_id(0), pl.program_id(1)

        @pl.when((i == 0) & (n == 0))
        def _():
            dln_ref[...] = jnp.zeros_like(dln_ref)

        @pl.when(n == 0)
        def _():
            dy = dx2_ref[...]
            dz, dg_rows = _ln_bwd(dy, z_ref[...], g_ref[...])
            dzb_ref[...] = dz.astype(dzb_ref.dtype)
            acc_ref[...] = ALPHA * dz
            dln_ref[0:1, :] += jnp.sum(dg_rows, axis=0, keepdims=True)
            dln_ref[1:2, :] += jnp.sum(dy, axis=0, keepdims=True)

        xb = xb_ref[...]
        gp = _mm(xb, wg_ref[...])
        up = _mm(xb, wu_ref[...])
        sg = _sigmoid(gp)
        act = gp * sg
        hid_ref[...] = (act * up).astype(hid_ref.dtype)
        dhid = _mm_nt(dzb_ref[...], wd_ref[...])
        dup = dhid * act
        dgp = dhid * up * (sg * (1.0 + gp * (1.0 - sg)))
        dgp_ref[...] = dgp.astype(dgp_ref.dtype)
        dup_ref[...] = dup.astype(dup_ref.dtype)
        acc_ref[...] += _mm_nt(dgp, wg_ref[...]) + _mm_nt(dup, wu_ref[...])

        @pl.when(n == N_DEV - 1)
        def _():
            dx1_ref[...] = acc_ref[...]

    row = pl.BlockSpec((TM, D_MODEL), lambda i, n: (i, 0))
    blk = pl.BlockSpec((None, TM, W_BLK), lambda i, n: (n, i, 0))
    return pl.pallas_call(
        body, name="ffn_bwd", grid=(T // TM, N_DEV),
        in_specs=[row, row, row, row,
                  pl.BlockSpec((None, D_MODEL, W_BLK), lambda i, n: (n, 0, 0)),
                  pl.BlockSpec((None, D_MODEL, W_BLK), lambda i, n: (n, 0, 0)),
                  pl.BlockSpec((None, W_BLK, D_MODEL), lambda i, n: (n, 0, 0)), _vec_spec(),
                  pl.BlockSpec(memory_space=pl.ANY)],
        out_specs=[row, blk, blk, pl.BlockSpec((TM, W_BLK), lambda i, n: (i, n)), row,
                   pl.BlockSpec((SUB, D_MODEL), lambda i, n: (0, 0))],
        out_shape=[jax.ShapeDtypeStruct((T, D_MODEL), F32),
                   jax.ShapeDtypeStruct((N_DEV, T, W_BLK), _BF), jax.ShapeDtypeStruct((N_DEV, T, W_BLK), _BF),
                   jax.ShapeDtypeStruct((T, N_DEV * W_BLK), _BF), jax.ShapeDtypeStruct((T, D_MODEL), _BF),
                   jax.ShapeDtypeStruct((SUB, D_MODEL), F32)],
        scratch_shapes=[pltpu.VMEM((TM, D_MODEL), F32)],
        compiler_params=_cparams(("arbitrary", "arbitrary")),
    )(dx2, z2, x1, x1b, wg, wu, wd, g, after)


def _outproj_bwd_call(dx1, z1, w, g):
    T = dx1.shape[0]

    def body(dx_ref, z_ref, w_ref, g_ref, dzb_ref, dyc_ref, dres_ref, dln_ref):
        @pl.when(pl.program_id(0) == 0)
        def _():
            dln_ref[...] = jnp.zeros_like(dln_ref)

        dy = dx_ref[...]
        dz, dg_rows = _ln_bwd(dy, z_ref[...], g_ref[...])
        dzb_ref[...] = dz.astype(dzb_ref.dtype)
        dres_ref[...] = ALPHA * dz
        dyc_ref[...] = _mm_nt(dz, w_ref[...].reshape(D_MODEL, D_MODEL))
        dln_ref[0:1, :] += jnp.sum(dg_rows, axis=0, keepdims=True)
        dln_ref[1:2, :] += jnp.sum(dy, axis=0, keepdims=True)

    row = pl.BlockSpec((TM, D_MODEL), lambda i: (i, 0))
    return pl.pallas_call(
        body, name="outproj_bwd", grid=(T // TM,),
        in_specs=[row, row, pl.BlockSpec((N_DEV, LANE, D_MODEL), lambda i: (0, 0, 0)), _vec_spec()],
        out_specs=[row, row, row, pl.BlockSpec((SUB, D_MODEL), lambda i: (0, 0))],
        out_shape=[jax.ShapeDtypeStruct((T, D_MODEL), _BF), jax.ShapeDtypeStruct((T, D_MODEL), F32),
                   jax.ShapeDtypeStruct((T, D_MODEL), F32), jax.ShapeDtypeStruct((SUB, D_MODEL), F32)],
        compiler_params=_cparams(("arbitrary",)),
    )(dx1, z1, w, g)


def _inproj_bwd_call(dres, dp, w):
    T = dres.shape[0]

    def body(dres_ref, dp_ref, w_ref, dx_ref):
        acc = dres_ref[...]
        for n in range(N_DEV):
            acc = acc + _mm_nt(dp_ref[n], w_ref[n])
        dx_ref[...] = acc

    row = pl.BlockSpec((TM, D_MODEL), lambda i: (i, 0))
    return pl.pallas_call(
        body, name="inproj_bwd", grid=(T // TM,),
        in_specs=[row, pl.BlockSpec((N_DEV, TM, W_BLK), lambda i: (0, i, 0)),
                  pl.BlockSpec((N_DEV, D_MODEL, W_BLK), lambda i: (0, 0, 0))],
        out_specs=row,
        out_shape=jax.ShapeDtypeStruct((T, D_MODEL), F32),
        compiler_params=_cparams(("parallel",)),
    )(dres, dp, w)


def _tn_cols_call(a, b3, name):
    T, ka = a.shape
    nblk, _, nb = b3.shape

    def body(a_ref, b_ref, o_ref):
        o_ref[...] = _mm_tn(a_ref[...], b_ref[...]).astype(o_ref.dtype)

    return pl.pallas_call(
        body, name=name, grid=(nblk,),
        in_specs=[pl.BlockSpec((T, ka), lambda n: (0, 0), pipeline_mode=pl.Buffered(1)),
                  pl.BlockSpec((None, T, nb), lambda n: (n, 0, 0))],
        out_specs=pl.BlockSpec((None, ka, nb), lambda n: (n, 0, 0)),
        out_shape=jax.ShapeDtypeStruct((nblk, ka, nb), _BF),
        compiler_params=_cparams(("parallel",)),
    )(a, b3)


def _tn_rows_call(a, b, kb, name):
    T, ka = a.shape
    n = b.shape[1]

    def body(a_ref, b_ref, o_ref):
        o_ref[...] = _mm_tn(a_ref[...], b_ref[...]).astype(o_ref.dtype)

    return pl.pallas_call(
        body, name=name, grid=(ka // kb,),
        in_specs=[pl.BlockSpec((T, kb), lambda r: (0, r)),
                  pl.BlockSpec((T, n), lambda r: (0, 0), pipeline_mode=pl.Buffered(1))],
        out_specs=pl.BlockSpec((None, kb, n), lambda r: (r, 0, 0)),
        out_shape=jax.ShapeDtypeStruct((ka // kb, kb, n), _BF),
        compiler_params=_cparams(("parallel",)),
    )(a, b)


def _me():
    return lax.axis_index("x"), lax.axis_index("y"), lax.axis_index("c")


def _flip(k):
    x, y, c = _me()
    return (1 - x if k & 4 else x, 1 - y if k & 2 else y, 1 - c if k & 1 else c)


def _dev_index(pos):
    return 4 * pos[0] + 2 * pos[1] + pos[2]


_HBM = pl.BlockSpec(memory_space=pltpu.HBM)
_SEM = pl.BlockSpec(memory_space=pltpu.SEMAPHORE)


def _comm_copies(ins, lands, gather_flags, send_sems, recv_sems):
    me = _dev_index(_me())
    copies = []
    for k in range(N_DEV):
        peer = _flip(k)
        for a, (src, land) in enumerate(zip(ins, lands)):
            copies.append(pltpu.make_async_remote_copy(
                src_ref=src if gather_flags[a] else src.at[_dev_index(peer)], dst_ref=land.at[me],
                send_sem=send_sems.at[k * len(ins) + a], recv_sem=recv_sems.at[k * len(ins) + a],
                device_id=peer, device_id_type=MESH))
    return copies


def _comm_start_call(arrs, gather_flags, after, name):
    n = len(arrs)
    lands = [lax.empty((N_DEV,) + (v.shape if gf else v.shape[1:]), v.dtype) for v, gf in zip(arrs, gather_flags)]

    def body(*refs):
        ins, lnd = refs[:n], refs[n:2 * n]
        send_sems, recv_sems = refs[2 * n + len(after)], refs[2 * n + len(after) + 1]
        for cp in _comm_copies(ins, lnd, gather_flags, send_sems, recv_sems):
            cp.start()
        refs[-1][...] = jnp.zeros_like(refs[-1])

    hbm = [pltpu.with_memory_space_constraint(v, pltpu.HBM) for v in list(arrs) + lands]
    out = pl.pallas_call(
        body, name=name,
        out_shape=(pltpu.SemaphoreType.DMA((N_DEV * n,)), pltpu.SemaphoreType.DMA((N_DEV * n,)),
                   *[pltpu.HBM(v.shape, v.dtype) for v in hbm], jax.ShapeDtypeStruct((SUB, LANE), F32)),
        in_specs=[_HBM] * (2 * n) + [pl.BlockSpec(memory_space=pl.ANY)] * len(after),
        out_specs=(_SEM, _SEM, *[_HBM] * (2 * n), pl.BlockSpec(memory_space=pltpu.VMEM)),
        input_output_aliases={i: 2 + i for i in range(2 * n)},
        compiler_params=pltpu.CompilerParams(has_side_effects=pltpu.SideEffectType.DATAFLOW_SIDE_EFFECTING),
    )(*hbm, *after)
    return out[:-1], out[-1]


def _comm_wait_call(state, gather_flags, after, name):
    n = len(gather_flags)
    send_sems, recv_sems, thru = state[0], state[1], state[2:]

    def body(*refs):
        ins, lnd, ssem, rsem = refs[:n], refs[n:2 * n], refs[2 * n], refs[2 * n + 1]
        for cp in _comm_copies(ins, lnd, gather_flags, ssem, rsem):
            cp.wait_send()
            cp.wait_recv()

    out = pl.pallas_call(
        body, name=name,
        out_shape=tuple(pltpu.HBM(v.shape, v.dtype) for v in thru),
        in_specs=[_HBM] * (2 * n) + [_SEM, _SEM] + [pl.BlockSpec(memory_space=pl.ANY)] * len(after),
        out_specs=tuple([_HBM] * (2 * n)),
        input_output_aliases={i: i for i in range(2 * n)},
        compiler_params=pltpu.CompilerParams(has_side_effects=pltpu.SideEffectType.DATAFLOW_SIDE_EFFECTING),
    )(*thru, send_sems, recv_sems, *after)
    return out[n:]


def _sum8_call(recv, rows, name):
    _, r, c = recv.shape

    def body(x_ref, o_ref):
        acc = x_ref[0].astype(F32)
        for s in range(1, N_DEV):
            acc = acc + x_ref[s].astype(F32)
        o_ref[...] = acc

    return pl.pallas_call(
        body, name=name, grid=(r // rows,),
        in_specs=[pl.BlockSpec((N_DEV, rows, c), lambda i: (0, i, 0))],
        out_specs=pl.BlockSpec((rows, c), lambda i: (i, 0)),
        out_shape=jax.ShapeDtypeStruct((r, c), F32),
        compiler_params=_cparams(("parallel",)),
    )(recv)


def _adamw_call(w, g, m, v, rows, name):
    r, c = w.shape

    def body(w_ref, g_ref, m_ref, v_ref, d_ref, nm_ref, nv_ref):
        gr = g_ref[...]
        nm = ADAM_B1 * m_ref[...] + (1.0 - ADAM_B1) * gr
        nv = ADAM_B2 * v_ref[...] + (1.0 - ADAM_B2) * (gr * gr)
        m_hat = nm / (1.0 - ADAM_B1 ** ADAM_STEP)
        v_hat = nv / (1.0 - ADAM_B2 ** ADAM_STEP)
        d_ref[...] = -ADAM_LR * (m_hat / (jnp.sqrt(v_hat) + ADAM_EPS) + ADAM_WD * w_ref[...])
        nm_ref[...] = nm
        nv_ref[...] = nv

    spec = pl.BlockSpec((rows, c), lambda i: (i, 0))
    return pl.pallas_call(
        body, name=name, grid=(r // rows,),
        in_specs=[spec] * 4, out_specs=[spec] * 3,
        out_shape=[jax.ShapeDtypeStruct((r, c), F32)] * 3,
        compiler_params=_cparams(("parallel",)),
    )(w, g, m, v)


SH_ROWS = 16
SH_W = LRU_WIDTH // N_DEV
REP_ROWS = 824
_REP_SIZES = (LRU_WIDTH, 2 * 6 * 64 * 64, 2 * 6 * 64 * 64, RET_WIDTH, 1920, D_MODEL, D_MODEL, D_MODEL, D_MODEL)
_RPB_SIZE = NA_HEADS * (2 * NA_KH - 1) * (2 * NA_KW - 1)


def _pack_sh(cw, ba, bx, lam):
    return jnp.concatenate([cw, ba, bx, lam], axis=0)


def _pad_sh(p):
    pad = [(0, 0)] * (p.ndim - 2) + [(0, SH_ROWS - p.shape[-2]), (0, LANE - p.shape[-1])]
    return jnp.pad(p, pad)


def _pack_rep(cb, wa, wx, gnw, rpb, l1g, l1b, l2g, l2b):
    flat = jnp.concatenate([cb.reshape(-1), wa.reshape(-1), wx.reshape(-1), gnw.reshape(-1),
                            jnp.pad(rpb.reshape(-1), (0, 1920 - _RPB_SIZE)), l1g, l1b, l2g, l2b,
                            jnp.zeros((REP_ROWS * LANE - sum(_REP_SIZES),), F32)])
    return flat.reshape(REP_ROWS, LANE)


def _unpack_rep(p):
    nl = p.shape[0]
    flat = p.reshape(nl, -1)
    out, off = [], 0
    for size in _REP_SIZES:
        out.append(flat[:, off:off + size])
        off += size
    cb, wa, wx, gnw, rpb, l1g, l1b, l2g, l2b = out
    return (cb, wa.reshape(nl, 2, 6, 64, 64), wx.reshape(nl, 2, 6, 64, 64), gnw,
            rpb[:, :_RPB_SIZE].reshape(nl, NA_HEADS, 2 * NA_KH - 1, 2 * NA_KW - 1), l1g, l1b, l2g, l2b)


def _adamw_nd(w, g, m, v, rows, name):
    shp = w.shape
    f = lambda t: t.reshape(-1, shp[-1])
    return [t.reshape(shp) for t in _adamw_call(f(w), f(g), f(m), f(v), rows, name)]


def kernel(x, w_in, conv_w, conv_b, lru_w_a, lru_b_a, lru_w_x, lru_b_x, lru_lam, ret_gn_w, na_rpb, w_out, ln1_g, ln1_b, w_gate, w_up, w_down, ln2_g, ln2_b, loss_target, m_w_in, m_conv_w, m_conv_b, m_lru_w_a, m_lru_b_a, m_lru_w_x, m_lru_b_x, m_lru_lam, m_ret_gn_w, m_na_rpb, m_w_out, m_ln1_g, m_ln1_b, m_w_gate, m_w_up, m_w_down, m_ln2_g, m_ln2_b, v_w_in, v_conv_w, v_conv_b, v_lru_w_a, v_lru_b_a, v_lru_w_x, v_lru_b_x, v_lru_lam, v_ret_gn_w, v_na_rpb, v_w_out, v_ln1_g, v_ln1_b, v_w_gate, v_w_up, v_w_down, v_ln2_g, v_ln2_b):
    nl = w_in.shape[0]
    T = x.shape[1]
    rows_n = T // GRID_W
    x0, target = x[0], loss_target[0]
    ffpad = W_BLK - FF_BLK

    win_b = w_in.astype(_BF)
    wg_b = jnp.pad(w_gate, ((0, 0), (0, 0), (0, ffpad))).astype(_BF)
    wu_b = jnp.pad(w_up, ((0, 0), (0, 0), (0, ffpad))).astype(_BF)
    wd_b = jnp.pad(w_down, ((0, 0), (0, ffpad), (0, 0))).astype(_BF)
    wout_b = w_out.astype(_BF)
    def agf_start(l, after):
        sh = _pad_sh(_pack_sh(conv_w[l], lru_b_a[l], lru_b_x[l], lru_lam[l]))
        return _comm_start_call([win_b[l], sh], [True] * 2, after, f"agf_start{l}")

    def agk_start(l, after):
        return _comm_start_call([wg_b[l], wu_b[l], wd_b[l], wout_b[l]], [True] * 4, after, f"agk_start{l}")

    tables = _ret_tables(T)
    layers = []
    gathered = []
    xs, xb = x0, x0.astype(_BF)
    agf_state, token = agf_start(0, [])
    for l in range(nl):
        win, shg = _comm_wait_call(agf_state, [True] * 2, [xb], f"agf_wait{l}")
        agk_state, token = agk_start(l, [shg])
        full = shg[:, :10, :SH_W].transpose(1, 0, 2).reshape(10, LRU_WIDTH)
        vec, w4 = _lru_pack(full[0:4], conv_b[l], lru_w_a[l], full[4:6], lru_w_x[l], full[6:8], full[8:10])
        gnw8 = jnp.pad(ret_gn_w[l][None], ((0, SUB - 1), (0, 0)))
        btab = _na_bias_tables(na_rpb[l], rows_n)
        proj = _inproj_call(xb, win, token)
        y_lru = _lru_fwd_call(proj, vec, w4)
        y_ret = _ret_fwd_call(proj, tables, gnw8)
        y_na = _na_fwd_call(proj, btab)
        wg, wu, wd, wout = _comm_wait_call(agk_state, [True] * 4, [y_na], f"agk_wait{l}")
        gathered.append((win, wg, wu, wd, wout))
        if l + 1 < nl:
            agf_state, token = agf_start(l + 1, [wout])
        z1, x1, x1b, ycb = _outproj_ln_call(y_lru, y_ret, y_na, xs, wout, ln1_g[l][None], ln1_b[l][None], token)
        z2, x2, x2b = _ffn_ln_call(x1, x1b, wg, wu, wd, ln2_g[l][None], ln2_b[l][None])
        layers.append(dict(xb=xb, proj=proj, vec=vec, w4=w4, gnw8=gnw8, btab=btab,
                           z1=z1, x1=x1, x1b=x1b, ycb=ycb, z2=z2))
        xs, xb = x2, x2b

    dx, loss_blk = _loss_call(xs, target)
    loss = lax.psum(loss_blk[0, 0], ("x", "y", "c"))

    gxa_flags = [False] * 4
    gxb_flags = [False, False, True]
    gxa_state, gxb_state = [None] * nl, [None] * nl
    token = loss_blk
    for l in reversed(range(nl)):
        s = layers[l]
        win, wg, wu, wd, wout = gathered[l]
        dx1, dgp, dup, hid, dz2b, dln2 = _ffn_bwd_call(dx, s["z2"], s["x1"], s["x1b"], wg, wu, wd, ln2_g[l][None], token)
        dwg = _tn_cols_call(s["x1b"], dgp, "tn_cols")
        dwu = _tn_cols_call(s["x1b"], dup, "tn_cols")
        dwd = _tn_rows_call(hid, dz2b, W_BLK, "tn_rows_down")
        dz1b, dyc, dres, dln1 = _outproj_bwd_call(dx1, s["z1"], wout, ln1_g[l][None])
        dwout = _tn_rows_call(s["ycb"], dz1b, LANE, "tn_rows_out")
        gxa_state[l], token = _comm_start_call([dwg, dwu, dwd, dwout], gxa_flags, [], f"gxa_start{l}")
        dp_lru, dvec, dw4 = _lru_bwd_call(s["proj"], dyc, s["vec"], s["w4"], token)
        dp_ret, dgnw = _ret_bwd_call(s["proj"], dyc, tables, s["gnw8"])
        dp_na, dbias = _na_bwd_call(s["proj"], dyc, s["btab"])
        dp = jnp.concatenate([dp_lru, dp_ret, dp_na], axis=0)
        dwin = _tn_cols_call(s["xb"], dp, "tn_cols")
        dx = _inproj_bwd_call(dres, dp, win)
        dcw, dcb, dwa, dba, dwx, dbx, dlam = _lru_unpack(dvec, dw4)
        rep = _pack_rep(dcb, dwa, dwx, dgnw[0], _na_bias_grad(dbias, rows_n), dln1[0], dln1[1], dln2[0], dln2[1])
        sh = _pack_sh(dcw, dba, dbx, dlam).reshape(10, N_DEV, SH_W).transpose(1, 0, 2)
        gxb_state[l], token = _comm_start_call([dwin, _pad_sh(sh), rep], gxb_flags, [], f"gxb_start{l}")

    g_big = [[None] * nl for _ in range(5)]
    g_sh = [None] * nl
    g_rep = [None] * nl
    after = [dx, token]
    for l in reversed(range(nl)):
        ra = _comm_wait_call(gxa_state[l], gxa_flags, after, f"gxa_wait{l}")
        g_big[1][l] = _sum8_call(ra[0], TM, "sum8_cols")[:, :FF_BLK]
        g_big[2][l] = _sum8_call(ra[1], TM, "sum8_cols")[:, :FF_BLK]
        g_big[3][l] = _sum8_call(ra[2], W_BLK, "sum8_down")[:FF_BLK]
        g_big[4][l] = _sum8_call(ra[3], LANE, "sum8_out")
        rb = _comm_wait_call(gxb_state[l], gxb_flags, [g_big[4][l]], f"gxb_wait{l}")
        g_big[0][l] = _sum8_call(rb[0], TM, "sum8_cols")
        g_sh[l] = _sum8_call(rb[1], SH_ROWS, "sum8_sh")
        g_rep[l] = _sum8_call(rb[2], REP_ROWS, "sum8_rep")
        after = [g_rep[l]]

    g_w_in, g_w_gate, g_w_up, g_w_down, g_w_out = [jnp.stack(t) for t in g_big]
    big = {
        "w_in": _adamw_nd(w_in, g_w_in, m_w_in, v_w_in, TM, "adamw_in"),
        "w_gate": _adamw_nd(w_gate, g_w_gate, m_w_gate, v_w_gate, TM, "adamw_ff"),
        "w_up": _adamw_nd(w_up, g_w_up, m_w_up, v_w_up, TM, "adamw_ff"),
        "w_down": _adamw_nd(w_down, g_w_down, m_w_down, v_w_down, FF_BLK, "adamw_down"),
        "w_out": _adamw_nd(w_out, g_w_out, m_w_out, v_w_out, LANE, "adamw_out"),
    }
    g_shp = jnp.stack(g_sh)
    pack_sh = lambda cw, ba, bx, lam: _pad_sh(jnp.concatenate([cw, ba, bx, lam], axis=1))
    sh_out = _adamw_nd(pack_sh(conv_w, lru_b_a, lru_b_x, lru_lam), g_shp,
                       pack_sh(m_conv_w, m_lru_b_a, m_lru_b_x, m_lru_lam),
                       pack_sh(v_conv_w, v_lru_b_a, v_lru_b_x, v_lru_lam), SH_ROWS, "adamw_sh")

    def split_sh(p):
        p = p[:, :, :SH_W]
        return {"conv_w": p[:, 0:4], "lru_b_a": p[:, 4:6], "lru_b_x": p[:, 6:8], "lru_lam": p[:, 8:10]}

    g_repp = jnp.stack(g_rep)
    pack_rep = lambda *ps: jnp.stack([_pack_rep(*[p[l] for p in ps]) for l in range(nl)])
    rep_names = ("conv_b", "lru_w_a", "lru_w_x", "ret_gn_w", "na_rpb", "ln1_g", "ln1_b", "ln2_g", "ln2_b")
    rep_out = _adamw_nd(pack_rep(conv_b, lru_w_a, lru_w_x, ret_gn_w, na_rpb, ln1_g, ln1_b, ln2_g, ln2_b), g_repp,
                        pack_rep(m_conv_b, m_lru_w_a, m_lru_w_x, m_ret_gn_w, m_na_rpb, m_ln1_g, m_ln1_b, m_ln2_g, m_ln2_b),
                        pack_rep(v_conv_b, v_lru_w_a, v_lru_w_x, v_ret_gn_w, v_na_rpb, v_ln1_g, v_ln1_b, v_ln2_g, v_ln2_b),
                        REP_ROWS, "adamw_rep")

    grads = {"w_in": g_w_in, "w_gate": g_w_gate, "w_up": g_w_up, "w_down": g_w_down, "w_out": g_w_out}
    grads.update(split_sh(g_shp))
    grads.update(dict(zip(rep_names, _unpack_rep(g_repp))))
    kinds = []
    for k in range(3):
        d = {n: big[n][k] for n in big}
        d.update(split_sh(sh_out[k]))
        d.update(dict(zip(rep_names, _unpack_rep(rep_out[k]))))
        kinds.append(d)
    order = ("w_in", "conv_w", "conv_b", "lru_w_a", "lru_b_a", "lru_w_x", "lru_b_x", "lru_lam", "ret_gn_w", "na_rpb",
             "w_out", "ln1_g", "ln1_b", "w_gate", "w_up", "w_down", "ln2_g", "ln2_b")
    outs = [loss, dx[None]]
    for d in (grads, *kinds):
        outs.extend(d[n] for n in order)
    return tuple(outs)
```

```python
import functools
import math

import numpy as np
import jax
import jax.numpy as jnp
from jax import lax
from jax.experimental import pallas as pl
from jax.experimental.pallas import tpu as pltpu

F32 = jnp.float32
_BF = jnp.bfloat16

D_MODEL = 1024
DEPTH = 4
GRID_W = 64
HEAD_DIM = 64
LRU_WIDTH = 384
RET_WIDTH = 384
RET_HEADS = 6
NA_WIDTH = 256
NA_HEADS = 4
IN_WIDTH = 3072
CONV_WIDTH = 4
LRU_C = 8.0
RET_CHUNK = 128
ROPE_BASE = 10000.0
GN_EPS = 1e-6
NA_KH = 8
NA_KW = 16
D_FF = 2816
FF_BLK = 352
N_DEV = 8
ALPHA = (2 * DEPTH) ** 0.25
LN_EPS = 1e-5
ADAM_LR = 0.001
ADAM_B1 = 0.9
ADAM_B2 = 0.999
ADAM_EPS = 1e-08
ADAM_WD = 0.01
ADAM_STEP = 10

LANE = 128
SUB = 8
VMEM_MB = 56
NEG = -1e30

MESH = pl.DeviceIdType.MESH


def _cparams(sem=None, vmem_mb=VMEM_MB):
    return pltpu.CompilerParams(dimension_semantics=sem, vmem_limit_bytes=vmem_mb << 20)


def _mm(a, b):
    return jnp.dot(a.astype(_BF), b.astype(_BF), preferred_element_type=F32)


def _mm_nt(a, b):
    return lax.dot_general(a.astype(_BF), b.astype(_BF), (((1,), (1,)), ((), ())), preferred_element_type=F32)


def _mm_tn(a, b):
    return lax.dot_general(a.astype(_BF), b.astype(_BF), (((0,), (0,)), ((), ())), preferred_element_type=F32)


def _sigmoid(x):
    return jax.nn.sigmoid(x)


def _rows(start, size):
    return pl.ds(pl.multiple_of(start, SUB), size)


def _loop2(n, body, init):
    assert n % 2 == 0
    return lax.fori_loop(0, n // 2, lambda i, c: body(2 * i + 1, body(2 * i, c)), init)


def _strip(T, col):
    return pl.BlockSpec((T, LANE), lambda j: (0, col(j)), pipeline_mode=pl.Buffered(1))


LRU_CH = 256
_GELU_C0 = math.sqrt(2.0 / math.pi)
_GELU_C1 = 0.044715


def _gelu_parts(x):
    x2 = x * x
    t = jnp.tanh(_GELU_C0 * (x + _GELU_C1 * x * x2))
    val = 0.5 * x * (1.0 + t)
    der = 0.5 * (1.0 + t) + 0.5 * x * (1.0 - t * t) * _GELU_C0 * (1.0 + 3.0 * _GELU_C1 * x2)
    return val, der


def _softplus_neg(lam):
    e = jnp.exp(-jnp.abs(lam))
    w = 1.0 + e
    l1p = jnp.where(w == 1.0, e, jnp.log(w) * (e / jnp.where(w == 1.0, 1.0, w - 1.0)))
    return jnp.maximum(-lam, 0.0) + l1p


def _window(ref, t0, ch, T):
    prev = ref[_rows(jnp.maximum(t0 - SUB, 0), SUB), :].astype(F32)
    nxt = ref[_rows(jnp.minimum(t0 + ch, T - SUB), SUB), :].astype(F32)
    prev = jnp.where(t0 > 0, prev, 0.0)
    nxt = jnp.where(t0 + ch < T, nxt, 0.0)
    return jnp.concatenate([prev, ref[_rows(t0, ch), :].astype(F32), nxt], axis=0)


def _tap(win, shift, ch):
    n = win.shape[0]
    return pltpu.roll(win, (-shift) % n, 0)[SUB:SUB + ch]


def _lru_conv(xb_ref, vec, t0, T):
    win = _window(xb_ref, t0, LRU_CH, T)
    xc = jnp.broadcast_to(vec[4:5, :], (LRU_CH, LANE))
    for j in range(CONV_WIDTH):
        xc = xc + _tap(win, j - CONV_WIDTH // 2, LRU_CH) * vec[j:j + 1, :]
    return xc


def _lru_dir(pre_a, pre_x, sp):
    r = _sigmoid(pre_a)
    i = _sigmoid(pre_x)
    log_a = (-LRU_C) * r * sp
    a = jnp.exp(log_a)
    z = jnp.tanh(-log_a) * (a * a + 1.0)
    s = jnp.sqrt(z)
    return r, i, a, s


def _scan_tile(a, b, reverse, row):
    for k in (1, 2, 4):
        if not reverse:
            a_s, b_s, m = pltpu.roll(a, k, 0), pltpu.roll(b, k, 0), row >= k
        else:
            a_s, b_s, m = pltpu.roll(a, SUB - k, 0), pltpu.roll(b, SUB - k, 0), row < SUB - k
        b = jnp.where(m, a * b_s + b, b)
        a = jnp.where(m, a * a_s, a)
    return a, b


def _bcast_row(x, r):
    return jnp.broadcast_to(x[r:r + 1, :], (SUB, LANE))


def _lru_prepare(xb_ref, w4_ref, vec, xc_ref, af_ref, uf_ref, ab_ref, ub_ref, T):
    sp_f = _softplus_neg(vec[9:10, :])
    sp_b = _softplus_neg(vec[10:11, :])
    w4 = w4_ref[0]

    def body(c, carry):
        t0 = c * LRU_CH
        xc = _lru_conv(xb_ref, vec, t0, T)
        if xc_ref is not None:
            xc_ref[_rows(t0, LRU_CH), :] = xc
        pre = _mm(xc, w4)
        _, i, a, s = _lru_dir(pre[:, 0:128] + vec[5:6, :], pre[:, 128:256] + vec[6:7, :], sp_f)
        af_ref[_rows(t0, LRU_CH), :] = a
        uf_ref[_rows(t0, LRU_CH), :] = s * (i * xc)
        _, i, a, s = _lru_dir(pre[:, 256:384] + vec[7:8, :], pre[:, 384:512] + vec[8:9, :], sp_b)
        ab_ref[_rows(t0, LRU_CH), :] = a
        ub_ref[_rows(t0, LRU_CH), :] = s * (i * xc)
        return carry

    lax.fori_loop(0, T // LRU_CH, body, 0)


def _lru_scan(af_ref, uf_ref, ab_ref, ub_ref, T):
    nt = T // SUB
    row = lax.broadcasted_iota(jnp.int32, (SUB, LANE), 0)

    def body(j, carry):
        hf, hb = carry
        sf = _rows(j * SUB, SUB)
        sb = _rows((nt - 1 - j) * SUB, SUB)
        a, b = _scan_tile(af_ref[sf, :], uf_ref[sf, :], False, row)
        h = a * hf + b
        uf_ref[sf, :] = h
        hf = _bcast_row(h, SUB - 1)
        a, b = _scan_tile(ab_ref[sb, :], ub_ref[sb, :], True, row)
        h = a * hb + b
        ub_ref[sb, :] = h
        hb = _bcast_row(h, 0)
        return hf, hb

    z = jnp.zeros((SUB, LANE), F32)
    lax.fori_loop(0, nt, body, (z, z))


def _lru_fwd_call(proj, vec, w4):
    T = proj.shape[0]

    def body(xb_ref, gate_ref, vec_ref, w4_ref, y_ref, af_ref, uf_ref, ab_ref, ub_ref):
        vec = vec_ref[...]
        _lru_prepare(xb_ref, w4_ref, vec, None, af_ref, uf_ref, ab_ref, ub_ref, T)
        _lru_scan(af_ref, uf_ref, ab_ref, ub_ref, T)

        def out(c, carry):
            rows = _rows(c * LRU_CH, LRU_CH)
            gl, _ = _gelu_parts(gate_ref[rows, :])
            y_ref[rows, :] = (uf_ref[rows, :] + ub_ref[rows, :]) * gl
            return carry

        lax.fori_loop(0, T // LRU_CH, out, 0)

    return pl.pallas_call(
        body, name="lru_fwd", grid=(LRU_WIDTH // LANE,),
        in_specs=[_strip(T, lambda j: j), _strip(T, lambda j: j + 3),
                  pl.BlockSpec((16, LANE), lambda j: (0, j)),
                  pl.BlockSpec((1, LANE, 4 * LANE), lambda j: (j, 0, 0))],
        out_specs=_strip(T, lambda j: j),
        out_shape=jax.ShapeDtypeStruct((T, LRU_WIDTH), F32),
        scratch_shapes=[pltpu.VMEM((T, LANE), F32)] * 4,
        compiler_params=_cparams(("arbitrary",)),
    )(proj, proj, vec, w4)


def _store_strips(stage_ref, dp_ref, cols, sems):
    copies = [pltpu.make_async_copy(stage_ref.at[b], dp_ref.at[:, pl.ds(pl.multiple_of(c * LANE, LANE), LANE)], sems.at[b])
              for b, c in enumerate(cols)]
    for cp in copies:
        cp.start()
    for cp in copies:
        cp.wait()


def _lru_bwd_call(proj, dycat, vec, w4, after):
    T = proj.shape[0]
    nt = T // SUB
    nch = T // LRU_CH

    def body(xb_ref, gate_ref, dy_ref, vec_ref, w4_ref, after_ref, dp_ref, dvec_ref, dw4_ref,
             xc_ref, af_ref, hf_ref, ab_ref, hb_ref, dh_ref, stage_ref, sems):
        dxb_ref, dgate_ref = stage_ref.at[0], stage_ref.at[1]
        vec = vec_ref[...]
        _lru_prepare(xb_ref, w4_ref, vec, xc_ref, af_ref, hf_ref, ab_ref, hb_ref, T)
        _lru_scan(af_ref, hf_ref, ab_ref, hb_ref, T)

        def gate_bwd(c, carry):
            rows = _rows(c * LRU_CH, LRU_CH)
            gl, dgl = _gelu_parts(gate_ref[rows, :])
            dy = dy_ref[rows, :]
            dgate_ref[rows, :] = (dy * (hf_ref[rows, :] + hb_ref[rows, :]) * dgl).astype(dgate_ref.dtype)
            dh_ref[rows, :] = dy * gl
            return carry

        lax.fori_loop(0, nch, gate_bwd, 0)

        row = lax.broadcasted_iota(jnp.int32, (SUB, LANE), 0)

        def adj(j, carry):
            gf, a_next, gb, a_prev = carry
            tf = nt - 1 - j
            sf = _rows(tf * SUB, SUB)
            a_t = af_ref[sf, :]
            h_t = hf_ref[sf, :]
            coef = jnp.where(row == SUB - 1, a_next, pltpu.roll(a_t, SUB - 1, 0))
            ac, bc = _scan_tile(coef, dh_ref[sf, :], True, row)
            g = ac * gf + bc
            h_prev = hf_ref[_rows(jnp.maximum(tf - 1, 0) * SUB, SUB), :]
            h_prev = jnp.where(tf > 0, _bcast_row(h_prev, SUB - 1), 0.0)
            hs = jnp.where(row == 0, h_prev, pltpu.roll(h_t, 1, 0))
            af_ref[sf, :] = g * hs
            hf_ref[sf, :] = g
            gf = _bcast_row(g, 0)
            a_next = _bcast_row(a_t, 0)
            sb = _rows(j * SUB, SUB)
            a_t = ab_ref[sb, :]
            h_t = hb_ref[sb, :]
            coef = jnp.where(row == 0, a_prev, pltpu.roll(a_t, 1, 0))
            ac, bc = _scan_tile(coef, dh_ref[sb, :], False, row)
            g = ac * gb + bc
            h_next = hb_ref[_rows(jnp.minimum(j + 1, nt - 1) * SUB, SUB), :]
            h_next = jnp.where(j < nt - 1, _bcast_row(h_next, 0), 0.0)
            hs = jnp.where(row == SUB - 1, h_next, pltpu.roll(h_t, SUB - 1, 0))
            ab_ref[sb, :] = g * hs
            hb_ref[sb, :] = g
            gb = _bcast_row(g, SUB - 1)
            a_prev = _bcast_row(a_t, SUB - 1)
            return gf, a_next, gb, a_prev

        z = jnp.zeros((SUB, LANE), F32)
        lax.fori_loop(0, nt, adj, (z, z, z, z))

        sp_f = _softplus_neg(vec[9:10, :])
        sp_b = _softplus_neg(vec[10:11, :])
        w4 = w4_ref[0]
        dw4_ref[...] = jnp.zeros_like(dw4_ref)

        def one_dir(pre_a, pre_x, sp, xc, du, da):
            r, i, a, s = _lru_dir(pre_a, pre_x, sp)
            d_i = du * s * xc
            dxc = du * s * i
            d_s = du * i * xc
            d_log = da * a - d_s * (a * a) / s
            d_r = d_log * (-LRU_C) * sp
            d_sp = jnp.sum(d_log * (-LRU_C) * r, axis=0, keepdims=True)
            return d_r * r * (1.0 - r), d_i * i * (1.0 - i), dxc, d_sp

        def gates_bwd(c, carry):
            db, dspf, dspb = carry
            rows = _rows(c * LRU_CH, LRU_CH)
            xc = xc_ref[rows, :]
            pre = _mm(xc, w4)
            dpa_f, dpx_f, dxc_f, d_sp_f = one_dir(pre[:, 0:128] + vec[5:6, :], pre[:, 128:256] + vec[6:7, :],
                                                  sp_f, xc, hf_ref[rows, :], af_ref[rows, :])
            dpa_b, dpx_b, dxc_b, d_sp_b = one_dir(pre[:, 256:384] + vec[7:8, :], pre[:, 384:512] + vec[8:9, :],
                                                  sp_b, xc, hb_ref[rows, :], ab_ref[rows, :])
            dpre = jnp.concatenate([dpa_f, dpx_f, dpa_b, dpx_b], axis=1)
            dw4_ref[0] += _mm_tn(xc, dpre)
            dh_ref[rows, :] = dxc_f + dxc_b + _mm_nt(dpre, w4)
            return db + jnp.sum(dpre, axis=0, keepdims=True), dspf + d_sp_f, dspb + d_sp_b

        z1 = jnp.zeros((1, LANE), F32)
        db, dspf, dspb = lax.fori_loop(0, nch, gates_bwd, (jnp.zeros((1, 4 * LANE), F32), z1, z1))

        def conv_bwd(c, carry):
            t0 = c * LRU_CH
            rows = _rows(t0, LRU_CH)
            dwin = _window(dh_ref, t0, LRU_CH, T)
            xwin = _window(xb_ref, t0, LRU_CH, T)
            dxc = dh_ref[rows, :]
            dxb = jnp.zeros((LRU_CH, LANE), F32)
            out = []
            for j in range(CONV_WIDTH):
                off = j - CONV_WIDTH // 2
                dxb = dxb + _tap(dwin, -off, LRU_CH) * vec[j:j + 1, :]
                out.append(carry[j] + jnp.sum(dxc * _tap(xwin, off, LRU_CH), axis=0, keepdims=True))
            dxb_ref[rows, :] = dxb.astype(dxb_ref.dtype)
            out.append(carry[CONV_WIDTH] + jnp.sum(dxc, axis=0, keepdims=True))
            return tuple(out)

        dconv = lax.fori_loop(0, nch, conv_bwd, (z1,) * (CONV_WIDTH + 1))
        dlam_f = dspf * (-_sigmoid(-vec[9:10, :]))
        dlam_b = dspb * (-_sigmoid(-vec[10:11, :]))
        dvec_ref[...] = jnp.concatenate(
            list(dconv) + [db[:, 0:128], db[:, 128:256], db[:, 256:384], db[:, 384:512], dlam_f, dlam_b,
                           jnp.zeros((5, LANE), F32)], axis=0)
        j = pl.program_id(0)
        _store_strips(stage_ref, dp_ref, (j, j + 3), sems)

    ns = LRU_WIDTH // LANE
    return pl.pallas_call(
        body, name="lru_bwd", grid=(ns,),
        in_specs=[_strip(T, lambda j: j), _strip(T, lambda j: j + 3), _strip(T, lambda j: j),
                  pl.BlockSpec((16, LANE), lambda j: (0, j)),
                  pl.BlockSpec((1, LANE, 4 * LANE), lambda j: (j, 0, 0)),
                  pl.BlockSpec(memory_space=pl.ANY)],
        out_specs=[pl.BlockSpec(memory_space=pl.ANY),
                   pl.BlockSpec((16, LANE), lambda j: (0, j)),
                   pl.BlockSpec((1, LANE, 4 * LANE), lambda j: (j, 0, 0))],
        out_shape=[jax.ShapeDtypeStruct((T, IN_WIDTH), _BF),
                   jax.ShapeDtypeStruct((16, LRU_WIDTH), F32), jax.ShapeDtypeStruct((ns, LANE, 4 * LANE), F32)],
        scratch_shapes=[pltpu.VMEM((T, LANE), F32)] * 6 + [pltpu.VMEM((2, T, LANE), _BF), pltpu.SemaphoreType.DMA((2,))],
        compiler_params=_cparams(("arbitrary",)),
    )(proj, proj, dycat, vec, w4, after)


def _lru_pack(cw, cb, wa, ba, wx, bx, lam):
    vec = jnp.concatenate([cw, cb[None], ba[0:1], bx[0:1], ba[1:2], bx[1:2], lam, jnp.zeros((5, LRU_WIDTH), F32)], axis=0)
    eye = jnp.eye(2, dtype=F32)
    mats = []
    for w in (wa[0], wx[0], wa[1], wx[1]):
        bd = jnp.einsum("jsio,st->jsito", w.reshape(3, 2, 64, 64), eye)
        mats.append(bd.reshape(3, LANE, LANE))
    return vec, jnp.concatenate(mats, axis=2).astype(_BF)


def _lru_unpack(dvec, dw4):
    def blocks(m):
        m = m.reshape(3, 2, 64, 2, 64)
        return jnp.stack([m[:, 0, :, 0, :], m[:, 1, :, 1, :]], axis=1).reshape(6, 64, 64)
    parts = [blocks(dw4[:, :, k * LANE:(k + 1) * LANE]) for k in range(4)]
    dwa = jnp.stack([parts[0], parts[2]])
    dwx = jnp.stack([parts[1], parts[3]])
    dba = jnp.stack([dvec[5], dvec[7]])
    dbx = jnp.stack([dvec[6], dvec[8]])
    return dvec[0:4], dvec[4], dwa, dba, dwx, dbx, dvec[9:11]


RC = RET_CHUNK


def _ret_tables(T):
    half = HEAD_DIM // 2
    pos = jnp.arange(T, dtype=F32)
    inv_freq = ROPE_BASE ** (-jnp.arange(half, dtype=F32) / half)
    ang = pos[:, None] * inv_freq[None, :]
    cos = jnp.tile(jnp.cos(ang), (1, 4))
    sin = jnp.tile(jnp.concatenate([-jnp.sin(ang), jnp.sin(ang)], axis=1), (1, 2))
    log_g = jnp.log1p(-jnp.exp2(-5.0 - jnp.arange(RET_HEADS, dtype=F32)))
    idx = jnp.arange(RC, dtype=F32)
    dec = jnp.exp(jnp.abs(idx[:, None] - idx[None, :]) * log_g[:, None, None])
    lg = jnp.repeat(log_g, HEAD_DIM).reshape(3, 1, LANE)
    col = idx[None, :, None]
    rtab = jnp.stack([jnp.exp((RC - 1 - col) * lg), jnp.exp(col * lg),
                      jnp.exp((col + 1.0) * lg), jnp.exp((RC - col) * lg)], axis=1)
    gch = jnp.broadcast_to(jnp.exp(RC * lg), (3, SUB, LANE))
    return cos, sin, dec, rtab, gch


def _swap32(x, lane):
    return jnp.where((lane & 32) == 0, pltpu.roll(x, LANE - 32, 1), pltpu.roll(x, 32, 1))


def _head_mean(x, m0, m1):
    s0 = jnp.sum(x * m0, axis=-1, keepdims=True)
    s1 = jnp.sum(x * m1, axis=-1, keepdims=True)
    return (s0 * m0 + s1 * m1) * (1.0 / HEAD_DIM)


def _ret_masks():
    lane = lax.broadcasted_iota(jnp.int32, (RC, LANE), 1)
    m0 = (lane < HEAD_DIM).astype(F32)
    r = lax.broadcasted_iota(jnp.int32, (LANE, LANE), 0) // HEAD_DIM
    c = lax.broadcasted_iota(jnp.int32, (LANE, LANE), 1) // HEAD_DIM
    return lane, m0, 1.0 - m0, (r == c).astype(F32)


def _ret_specs(T):
    const = lambda shape, imap: pl.BlockSpec(shape, imap)
    return [_strip(T, lambda j: j + 6), _strip(T, lambda j: j + 9), _strip(T, lambda j: j + 12),
            _strip(T, lambda j: j + 15),
            pl.BlockSpec((T, LANE), lambda j: (0, 0), pipeline_mode=pl.Buffered(1)),
            pl.BlockSpec((T, LANE), lambda j: (0, 0), pipeline_mode=pl.Buffered(1)),
            const((2, RC, RC), lambda j: (j, 0, 0)),
            const((1, 4, RC, LANE), lambda j: (j, 0, 0, 0)),
            const((1, SUB, LANE), lambda j: (j, 0, 0)),
            const((SUB, LANE), lambda j: (0, j))]


def _ret_fwd_call(proj, tables, gnw8):
    T = proj.shape[0]
    nc = T // RC
    cos, sin, dec, rtab, gch = tables

    def body(q_ref, k_ref, v_ref, g_ref, cos_ref, sin_ref, dec_ref, rtab_ref, gch_ref, gnw_ref, y_ref, stf_ref):
        lane, m0, m1, bd = _ret_masks()
        gch_v = gch_ref[0][0:1, :]
        gnw = gnw_ref[0:1, :]
        dkf, dkb, dqf, dqb = rtab_ref[0, 0], rtab_ref[0, 1], rtab_ref[0, 2], rtab_ref[0, 3]

        def rope(x, rows):
            return x * cos_ref[rows, :] + _swap32(x, lane) * sin_ref[rows, :]

        def pass_a(n, st):
            rows = _rows(n * RC, RC)
            stf_ref[n] = st
            kr = rope(k_ref[rows, :], rows) * (HEAD_DIM ** -0.5)
            return gch_v * st + _mm_tn(kr * dkf, v_ref[rows, :]) * bd

        _loop2(nc, pass_a, jnp.zeros((LANE, LANE), F32))

        def pass_b(i, stb):
            n = nc - 1 - i
            rows = _rows(n * RC, RC)
            qr = rope(q_ref[rows, :], rows)
            kr = rope(k_ref[rows, :], rows) * (HEAD_DIM ** -0.5)
            v = v_ref[rows, :]
            o = _mm(qr * dqf, stf_ref[n]) + _mm(qr * dqb, stb)
            for h, m in ((0, m0), (1, m1)):
                s = _mm_nt(qr * m, kr) * dec_ref[h]
                o = o + _mm(s, v * m)
            oc = o - _head_mean(o, m0, m1)
            on = oc * lax.rsqrt(_head_mean(oc * oc, m0, m1) + GN_EPS)
            g = g_ref[rows, :]
            y_ref[rows, :] = (g * _sigmoid(g)) * (on * gnw)
            return gch_v * stb + _mm_tn(kr * dkb, v) * bd

        _loop2(nc, pass_b, jnp.zeros((LANE, LANE), F32))

    return pl.pallas_call(
        body, name="ret_fwd", grid=(RET_WIDTH // LANE,),
        in_specs=_ret_specs(T),
        out_specs=_strip(T, lambda j: j),
        out_shape=jax.ShapeDtypeStruct((T, RET_WIDTH), F32),
        scratch_shapes=[pltpu.VMEM((nc, LANE, LANE), F32)],
        compiler_params=_cparams(("arbitrary",)),
    )(proj, proj, proj, proj, cos, sin, dec, rtab, gch, gnw8)


def _ret_bwd_call(proj, dycat, tables, gnw8, dp):
    T = proj.shape[0]
    nc = T // RC
    cos, sin, dec, rtab, gch = tables

    def body(q_ref, k_ref, v_ref, g_ref, cos_ref, sin_ref, dec_ref, rtab_ref, gch_ref, gnw_ref, dy_ref, dp_in_ref,
             dp_out_ref, dgnw_ref, stf_ref, dstb_ref, dkr_ref, dv_ref, dp_ref, sems):
        lane, m0, m1, bd = _ret_masks()
        gch_v = gch_ref[0][0:1, :]
        gnw = gnw_ref[0:1, :]
        dkf, dkb, dqf, dqb = rtab_ref[0, 0], rtab_ref[0, 1], rtab_ref[0, 2], rtab_ref[0, 3]
        scale = HEAD_DIM ** -0.5
        zst = jnp.zeros((LANE, LANE), F32)

        def rope(x, rows):
            return x * cos_ref[rows, :] + _swap32(x, lane) * sin_ref[rows, :]

        def rope_t(d, rows):
            return d * cos_ref[rows, :] + _swap32(d * sin_ref[rows, :], lane)

        def pass_a(n, st):
            rows = _rows(n * RC, RC)
            stf_ref[n] = st
            kr = rope(k_ref[rows, :], rows) * scale
            return gch_v * st + _mm_tn(kr * dkf, v_ref[rows, :]) * bd

        _loop2(nc, pass_a, zst)

        def pass_b(i, carry):
            stb, d_f, dgnw = carry
            n = nc - 1 - i
            rows = _rows(n * RC, RC)
            qr = rope(q_ref[rows, :], rows)
            kr = rope(k_ref[rows, :], rows) * scale
            v = v_ref[rows, :]
            stf = stf_ref[n]
            qf = qr * dqf
            qb = qr * dqb
            o = _mm(qf, stf) + _mm(qb, stb)
            s_h = []
            for h, m in ((0, m0), (1, m1)):
                s = _mm_nt(qr * m, kr) * dec_ref[h]
                s_h.append(s)
                o = o + _mm(s, v * m)
            oc = o - _head_mean(o, m0, m1)
            rstd = lax.rsqrt(_head_mean(oc * oc, m0, m1) + GN_EPS)
            on = oc * rstd
            g = g_ref[rows, :]
            sg = _sigmoid(g)
            dy = dy_ref[rows, :]
            dp_ref[3, rows, :] = (dy * (on * gnw) * (sg * (1.0 + g * (1.0 - sg)))).astype(dp_ref.dtype)
            t = dy * (g * sg)
            dgnw = dgnw + jnp.sum(t * on, axis=0, keepdims=True)
            don = t * gnw
            do = rstd * (don - _head_mean(don, m0, m1) - on * _head_mean(don * on, m0, m1))
            dqr = _mm_nt(do, stf) * dqf + _mm_nt(do, stb) * dqb
            dkr = _mm_nt(v, d_f) * dkf
            dv = _mm(kr * dkf, d_f)
            for h, m in ((0, m0), (1, m1)):
                ds = _mm_nt(do * m, v) * dec_ref[h]
                dqr = dqr + _mm(ds, kr * m)
                dkr = dkr + _mm_tn(ds, qr * m)
                dv = dv + _mm_tn(s_h[h], do * m)
            dp_ref[0, rows, :] = rope_t(dqr, rows).astype(dp_ref.dtype)
            dkr_ref[rows, :] = dkr
            dv_ref[rows, :] = dv
            dstb_ref[n] = _mm_tn(qb, do) * bd
            d_f = _mm_tn(qf, do) * bd + gch_v * d_f
            stb = gch_v * stb + _mm_tn(kr * dkb, v) * bd
            return stb, d_f, dgnw

        _, _, dgnw = _loop2(nc, pass_b, (zst, zst, jnp.zeros((1, LANE), F32)))
        dgnw_ref[...] = jnp.concatenate([dgnw, jnp.zeros((SUB - 1, LANE), F32)], axis=0)

        def pass_c(n, d_b):
            rows = _rows(n * RC, RC)
            kr = rope(k_ref[rows, :], rows) * scale
            v = v_ref[rows, :]
            dkr = dkr_ref[rows, :] + _mm_nt(v, d_b) * dkb
            dp_ref[1, rows, :] = (rope_t(dkr, rows) * scale).astype(dp_ref.dtype)
            dp_ref[2, rows, :] = (dv_ref[rows, :] + _mm(kr * dkb, d_b)).astype(dp_ref.dtype)
            return dstb_ref[n] + gch_v * d_b

        _loop2(nc, pass_c, zst)
        j = pl.program_id(0)
        _store_strips(dp_ref, dp_out_ref, (j + 6, j + 9, j + 12, j + 15), sems)

    n_in = len(_ret_specs(T)) + 1
    return pl.pallas_call(
        body, name="ret_bwd", grid=(RET_WIDTH // LANE,),
        in_specs=_ret_specs(T) + [_strip(T, lambda j: j + 3), pl.BlockSpec(memory_space=pl.ANY)],
        out_specs=[pl.BlockSpec(memory_space=pl.ANY), pl.BlockSpec((SUB, LANE), lambda j: (0, j))],
        out_shape=[jax.ShapeDtypeStruct(dp.shape, dp.dtype), jax.ShapeDtypeStruct((SUB, RET_WIDTH), F32)],
        scratch_shapes=[pltpu.VMEM((nc, LANE, LANE), F32), pltpu.VMEM((nc, LANE, LANE), F32),
                        pltpu.VMEM((T, LANE), F32), pltpu.VMEM((T, LANE), F32),
                        pltpu.VMEM((4, T, LANE), _BF), pltpu.SemaphoreType.DMA((4,))],
        input_output_aliases={n_in: 0},
        compiler_params=_cparams(("arbitrary",)),
    )(proj, proj, proj, proj, cos, sin, dec, rtab, gch, gnw8, dycat, dp)


NA_Q = 2 * GRID_W
NA_WROWS = 10
NA_K = NA_WROWS * GRID_W
NA_CHUNKS = NA_K // LANE
NA_TYPES = 5


def _na_onehots(rows_n):
    reps = [(0, 0), (2, 0), (4, 0), (rows_n - 4, rows_n - NA_WROWS), (rows_n - 2, rows_n - NA_WROWS)]
    rm = np.zeros((NA_TYPES, 2, NA_WROWS, 2 * NA_KH - 1), np.float32)
    for t, (r, ws) in enumerate(reps):
        for qh in range(2):
            qrow = r + qh
            rstart = min(max(qrow - NA_KH // 2, 0), rows_n - NA_KH)
            for kh in range(NA_WROWS):
                krow = ws + kh
                if rstart <= krow < rstart + NA_KH:
                    rm[t, qh, kh, krow - qrow + NA_KH - 1] = 1.0
    cm = np.zeros((GRID_W, GRID_W, 2 * NA_KW - 1), np.float32)
    for qc in range(GRID_W):
        cstart = min(max(qc - NA_KW // 2, 0), GRID_W - NA_KW)
        for kc in range(cstart, cstart + NA_KW):
            cm[qc, kc, kc - qc + NA_KW - 1] = 1.0
    rm2 = rm.reshape(NA_TYPES, 2, NA_CHUNKS, 2, 2 * NA_KH - 1)
    cm2 = np.zeros((GRID_W, LANE, 2, 2 * NA_KW - 1), np.float32)
    for z in range(2):
        cm2[:, z * GRID_W:(z + 1) * GRID_W, z, :] = cm
    return rm2, cm2


def _na_bias_tables(rpb, rows_n):
    rm, cm = _na_onehots(rows_n)
    val = jnp.einsum("hab,tqpza,xkzb->htpqxk", rpb, rm, cm, precision=lax.Precision.HIGHEST)
    valid = np.einsum("tqpz,xkz->tpqxk", rm.sum(-1), cm.sum(-1)) > 0.5
    return jnp.where(valid[None], val, NEG).reshape(2, 2, NA_TYPES, NA_CHUNKS, NA_Q, LANE)


def _na_bias_grad(dtab, rows_n):
    rm, cm = _na_onehots(rows_n)
    d6 = dtab.reshape(NA_HEADS, NA_TYPES, NA_CHUNKS, 2, GRID_W, LANE)
    return jnp.einsum("htpqxk,tqpza,xkzb->hab", d6, rm, cm, precision=lax.Precision.HIGHEST)


def _na_bias(b_ref, h, typ):
    return jnp.concatenate([b_ref[0, h, typ, c] for c in range(NA_CHUNKS)], axis=1)


def _na_step(p, npairs, rows_n):
    ws = jnp.clip(2 * p - NA_KH // 2, 0, rows_n - NA_WROWS)
    koff = pl.multiple_of(ws * GRID_W, LANE)
    typ = jnp.where(p == 0, 0, jnp.where(p == 1, 1, jnp.where(p == npairs - 2, 3, jnp.where(p == npairs - 1, 4, 2))))
    return _rows(p * NA_Q, NA_Q), pl.ds(koff, NA_K), typ


def _na_fwd_call(proj, btab):
    T = proj.shape[0]
    npairs, rows_n = T // NA_Q, T // GRID_W

    def body(q_ref, k_ref, v_ref, b_ref, o_ref):
        lane = lax.broadcasted_iota(jnp.int32, (NA_Q, LANE), 1)
        m0 = (lane < HEAD_DIM).astype(F32)
        m1 = 1.0 - m0

        def step(p, carry):
            qrows, krows, typ = _na_step(p, npairs, rows_n)
            q = q_ref[qrows, :]
            kw = k_ref[krows, :]
            vw = v_ref[krows, :]
            o = jnp.zeros((NA_Q, LANE), F32)
            for h, m in ((0, m0), (1, m1)):
                s = _mm_nt(q * m, kw) * (HEAD_DIM ** -0.5) + _na_bias(b_ref, h, typ)
                e = jnp.exp(s - jnp.max(s, axis=-1, keepdims=True))
                pr = e / jnp.sum(e, axis=-1, keepdims=True)
                o = o + _mm(pr, vw) * m
            o_ref[qrows, :] = o
            return carry

        _loop2(npairs, step, 0)

    return pl.pallas_call(
        body, name="na_fwd", grid=(NA_WIDTH // LANE,),
        in_specs=[_strip(T, lambda j: j + 18), _strip(T, lambda j: j + 20), _strip(T, lambda j: j + 22),
                  pl.BlockSpec((1, 2, NA_TYPES, NA_CHUNKS, NA_Q, LANE), lambda j: (j, 0, 0, 0, 0, 0))],
        out_specs=_strip(T, lambda j: j),
        out_shape=jax.ShapeDtypeStruct((T, NA_WIDTH), F32),
        compiler_params=_cparams(("arbitrary",)),
    )(proj, proj, proj, btab)


def _na_bwd_call(proj, dycat, btab, dp):
    T = proj.shape[0]
    npairs, rows_n = T // NA_Q, T // GRID_W
    scale = HEAD_DIM ** -0.5

    def body(q_ref, k_ref, v_ref, do_ref, b_ref, dp_in_ref, dp_out_ref, db_ref, dka_ref, dva_ref, stage_ref, sems):
        dq_ref = stage_ref.at[0]
        lane = lax.broadcasted_iota(jnp.int32, (NA_Q, LANE), 1)
        m0 = (lane < HEAD_DIM).astype(F32)
        m1 = 1.0 - m0
        dka_ref[...] = jnp.zeros_like(dka_ref)
        dva_ref[...] = jnp.zeros_like(dva_ref)
        db_ref[...] = jnp.zeros_like(db_ref)

        def step(p, carry):
            qrows, krows, typ = _na_step(p, npairs, rows_n)
            q = q_ref[qrows, :]
            do = do_ref[qrows, :]
            kw = k_ref[krows, :]
            vw = v_ref[krows, :]
            dq = jnp.zeros((NA_Q, LANE), F32)
            dk = jnp.zeros((NA_K, LANE), F32)
            dv = jnp.zeros((NA_K, LANE), F32)
            for h, m in ((0, m0), (1, m1)):
                qm = q * m
                dom = do * m
                s = _mm_nt(qm, kw) * scale + _na_bias(b_ref, h, typ)
                e = jnp.exp(s - jnp.max(s, axis=-1, keepdims=True))
                pr = e / jnp.sum(e, axis=-1, keepdims=True)
                dpr = _mm_nt(dom, vw)
                ds = pr * (dpr - jnp.sum(pr * dpr, axis=-1, keepdims=True))
                for c in range(NA_CHUNKS):
                    db_ref[0, h, typ, c] += ds[:, c * LANE:(c + 1) * LANE]
                dsb = (ds * scale).astype(_BF)
                dq = dq + _mm(dsb, kw) * m
                dk = dk + _mm_tn(dsb, qm)
                dv = dv + _mm_tn(pr, dom)
            dq_ref[qrows, :] = dq.astype(dq_ref.dtype)
            dka_ref[krows, :] += dk
            dva_ref[krows, :] += dv
            return carry

        _loop2(npairs, step, 0)
        stage_ref[1] = dka_ref[...].astype(stage_ref.dtype)
        stage_ref[2] = dva_ref[...].astype(stage_ref.dtype)
        j = pl.program_id(0)
        _store_strips(stage_ref, dp_out_ref, (j + 18, j + 20, j + 22), sems)

    tab = pl.BlockSpec((1, 2, NA_TYPES, NA_CHUNKS, NA_Q, LANE), lambda j: (j, 0, 0, 0, 0, 0))
    return pl.pallas_call(
        body, name="na_bwd", grid=(NA_WIDTH // LANE,),
        in_specs=[_strip(T, lambda j: j + 18), _strip(T, lambda j: j + 20), _strip(T, lambda j: j + 22),
                  _strip(T, lambda j: j + 6), tab, pl.BlockSpec(memory_space=pl.ANY)],
        out_specs=[pl.BlockSpec(memory_space=pl.ANY), tab],
        out_shape=[jax.ShapeDtypeStruct(dp.shape, dp.dtype),
                   jax.ShapeDtypeStruct((2, 2, NA_TYPES, NA_CHUNKS, NA_Q, LANE), F32)],
        scratch_shapes=[pltpu.VMEM((T, LANE), F32), pltpu.VMEM((T, LANE), F32),
                        pltpu.VMEM((3, T, LANE), _BF), pltpu.SemaphoreType.DMA((3,))],
        input_output_aliases={5: 0},
        compiler_params=_cparams(("arbitrary",)),
    )(proj, proj, proj, dycat, btab, dp)


W_BLK = IN_WIDTH // N_DEV
N_BLK = 768
N_STEPS = IN_WIDTH // N_BLK
TM = 512


def _ln_fwd(z, g, b):
    zc = z - jnp.mean(z, axis=-1, keepdims=True)
    var = jnp.mean(zc * zc, axis=-1, keepdims=True)
    return zc * lax.rsqrt(var + LN_EPS) * g + b


def _ln_bwd(dy, z, g):
    zc = z - jnp.mean(z, axis=-1, keepdims=True)
    rstd = lax.rsqrt(jnp.mean(zc * zc, axis=-1, keepdims=True) + LN_EPS)
    xhat = zc * rstd
    dxh = dy * g
    dz = rstd * (dxh - jnp.mean(dxh, axis=-1, keepdims=True) - xhat * jnp.mean(dxh * xhat, axis=-1, keepdims=True))
    return dz, dy * xhat


def _row_tile(T):
    return 1024 if T % 1024 == 0 else TM


def _inproj_call(xb, w, after):
    T = xb.shape[0]
    tm = _row_tile(T)

    def body(x_ref, w_ref, after_ref, o_ref):
        o_ref[...] = _mm(x_ref[...], w_ref[...])

    return pl.pallas_call(
        body, name="inproj", grid=(T // tm, N_STEPS),
        in_specs=[pl.BlockSpec((tm, D_MODEL), lambda i, n: (i, 0)),
                  pl.BlockSpec((D_MODEL, N_BLK), lambda i, n: (0, n)),
                  pl.BlockSpec(memory_space=pl.ANY)],
        out_specs=pl.BlockSpec((tm, N_BLK), lambda i, n: (i, n)),
        out_shape=jax.ShapeDtypeStruct((T, IN_WIDTH), F32),
        compiler_params=_cparams(("parallel", "arbitrary")),
    )(xb, w, after)


def _vec_spec():
    return pl.BlockSpec((1, D_MODEL), lambda *_: (0, 0))


def _outproj_ln_call(y_lru, y_ret, y_na, x, w, g, b, after):
    T = x.shape[0]

    def body(yl_ref, yr_ref, yn_ref, x_ref, w_ref, g_ref, b_ref, after_ref, z_ref, x1_ref, x1b_ref, yc_ref):
        yc_ref[:, 0:LRU_WIDTH] = yl_ref[...].astype(yc_ref.dtype)
        yc_ref[:, LRU_WIDTH:LRU_WIDTH + RET_WIDTH] = yr_ref[...].astype(yc_ref.dtype)
        yc_ref[:, LRU_WIDTH + RET_WIDTH:] = yn_ref[...].astype(yc_ref.dtype)
        z = ALPHA * x_ref[...] + _mm(yc_ref[...], w_ref[...])
        z_ref[...] = z
        x1 = _ln_fwd(z, g_ref[...], b_ref[...])
        x1_ref[...] = x1
        x1b_ref[...] = x1.astype(x1b_ref.dtype)

    row = lambda w_: pl.BlockSpec((TM, w_), lambda i: (i, 0))
    return pl.pallas_call(
        body, name="outproj_ln", grid=(T // TM,),
        in_specs=[row(LRU_WIDTH), row(RET_WIDTH), row(NA_WIDTH), row(D_MODEL),
                  pl.BlockSpec((D_MODEL, D_MODEL), lambda i: (0, 0)), _vec_spec(), _vec_spec(),
                  pl.BlockSpec(memory_space=pl.ANY)],
        out_specs=[row(D_MODEL)] * 4,
        out_shape=[jax.ShapeDtypeStruct((T, D_MODEL), F32), jax.ShapeDtypeStruct((T, D_MODEL), F32),
                   jax.ShapeDtypeStruct((T, D_MODEL), _BF), jax.ShapeDtypeStruct((T, D_MODEL), _BF)],
        compiler_params=_cparams(("parallel",)),
    )(y_lru, y_ret, y_na, x, w, g, b, after)


def _ffn_ln_call(x1, x1b, wg, wu, wd, g, b):
    T = x1.shape[0]

    def body(x_ref, xb_ref, wg_ref, wu_ref, wd_ref, g_ref, b_ref, z_ref, x2_ref, x2b_ref, acc_ref):
        n = pl.program_id(1)

        @pl.when(n == 0)
        def _():
            acc_ref[...] = jnp.zeros_like(acc_ref)

        xb = xb_ref[...]
        gp = _mm(xb, wg_ref[...])
        hid = gp * _sigmoid(gp) * _mm(xb, wu_ref[...])
        acc_ref[...] += _mm(hid, wd_ref[...])

        @pl.when(n == N_STEPS - 1)
        def _():
            z = ALPHA * x_ref[...] + acc_ref[...]
            z_ref[...] = z
            x2 = _ln_fwd(z, g_ref[...], b_ref[...])
            x2_ref[...] = x2
            x2b_ref[...] = x2.astype(x2b_ref.dtype)

    row = pl.BlockSpec((TM, D_MODEL), lambda i, n: (i, 0))
    return pl.pallas_call(
        body, name="ffn_ln", grid=(T // TM, N_STEPS),
        in_specs=[row, row,
                  pl.BlockSpec((D_MODEL, N_BLK), lambda i, n: (0, n)),
                  pl.BlockSpec((D_MODEL, N_BLK), lambda i, n: (0, n)),
                  pl.BlockSpec((N_BLK, D_MODEL), lambda i, n: (n, 0)), _vec_spec(), _vec_spec()],
        out_specs=[row] * 3,
        out_shape=[jax.ShapeDtypeStruct((T, D_MODEL), F32), jax.ShapeDtypeStruct((T, D_MODEL), F32),
                   jax.ShapeDtypeStruct((T, D_MODEL), _BF)],
        scratch_shapes=[pltpu.VMEM((TM, D_MODEL), F32)],
        compiler_params=_cparams(("parallel", "arbitrary")),
    )(x1, x1b, wg, wu, wd, g, b)


def _loss_call(y, t):
    T = y.shape[0]

    def body(y_ref, t_ref, dy_ref, loss_ref):
        @pl.when(pl.program_id(0) == 0)
        def _():
            loss_ref[...] = jnp.zeros_like(loss_ref)

        err = y_ref[...] - t_ref[...]
        dy_ref[...] = err * (1.0 / D_MODEL)
        part = 0.5 * jnp.sum(jnp.mean(err * err, axis=-1, keepdims=True), axis=0, keepdims=True)
        loss_ref[...] += jnp.broadcast_to(part, loss_ref.shape)

    row = pl.BlockSpec((TM, D_MODEL), lambda i: (i, 0))
    return pl.pallas_call(
        body, name="loss", grid=(T // TM,),
        in_specs=[row, row],
        out_specs=[row, pl.BlockSpec((SUB, LANE), lambda i: (0, 0))],
        out_shape=[jax.ShapeDtypeStruct((T, D_MODEL), F32), jax.ShapeDtypeStruct((SUB, LANE), F32)],
        compiler_params=_cparams(("arbitrary",)),
    )(y, t)


def _ffn_bwd_call(dx2, z2, x1, x1b, wg, wu, wd, g, after):
    T = x1.shape[0]

    def body(dx2_ref, z_ref, x_ref, xb_ref, wg_ref, wu_ref, wd_ref, g_ref, after_ref,
             dx1_ref, dgp_ref, dup_ref, hid_ref, dzb_ref, dln_ref, acc_ref):
        i, n = pl.program_id(0), pl.program_id(1)

        @pl.when((i == 0) & (n == 0))
        def _():
            dln_ref[...] = jnp.zeros_like(dln_ref)

        @pl.when(n == 0)
        def _():
            dy = dx2_ref[...]
            dz, dg_rows = _ln_bwd(dy, z_ref[...], g_ref[...])
            dzb_ref[...] = dz.astype(dzb_ref.dtype)
            acc_ref[...] = ALPHA * dz
            dln_ref[0:1, :] += jnp.sum(dg_rows, axis=0, keepdims=True)
            dln_ref[1:2, :] += jnp.sum(dy, axis=0, keepdims=True)

        xb = xb_ref[...]
        gp = _mm(xb, wg_ref[...])
        up = _mm(xb, wu_ref[...])
        sg = _sigmoid(gp)
        act = gp * sg
        hid_ref[...] = (act * up).astype(hid_ref.dtype)
        dhid = _mm_nt(dzb_ref[...], wd_ref[...])
        dup = dhid * act
        dgp = dhid * up * (sg * (1.0 + gp * (1.0 - sg)))
        dgp_ref[...] = dgp.astype(dgp_ref.dtype)
        dup_ref[...] = dup.astype(dup_ref.dtype)
        acc_ref[...] += _mm_nt(dgp, wg_ref[...]) + _mm_nt(dup, wu_ref[...])

        @pl.when(n == N_STEPS - 1)
        def _():
            dx1_ref[...] = acc_ref[...]

    row = pl.BlockSpec((TM, D_MODEL), lambda i, n: (i, 0))
    blk = pl.BlockSpec((TM, N_BLK), lambda i, n: (i, n))
    return pl.pallas_call(
        body, name="ffn_bwd", grid=(T // TM, N_STEPS),
        in_specs=[row, row, row, row,
                  pl.BlockSpec((D_MODEL, N_BLK), lambda i, n: (0, n)),
                  pl.BlockSpec((D_MODEL, N_BLK), lambda i, n: (0, n)),
                  pl.BlockSpec((N_BLK, D_MODEL), lambda i, n: (n, 0)), _vec_spec(),
                  pl.BlockSpec(memory_space=pl.ANY)],
        out_specs=[row, blk, blk, blk, row, pl.BlockSpec((SUB, D_MODEL), lambda i, n: (0, 0))],
        out_shape=[jax.ShapeDtypeStruct((T, D_MODEL), F32),
                   jax.ShapeDtypeStruct((T, IN_WIDTH), _BF), jax.ShapeDtypeStruct((T, IN_WIDTH), _BF),
                   jax.ShapeDtypeStruct((T, IN_WIDTH), _BF), jax.ShapeDtypeStruct((T, D_MODEL), _BF),
                   jax.ShapeDtypeStruct((SUB, D_MODEL), F32)],
        scratch_shapes=[pltpu.VMEM((TM, D_MODEL), F32)],
        compiler_params=_cparams(("arbitrary", "arbitrary")),
    )(dx2, z2, x1, x1b, wg, wu, wd, g, after)


def _outproj_bwd_call(dx1, z1, w, g):
    T = dx1.shape[0]

    def body(dx_ref, z_ref, w_ref, g_ref, dzb_ref, dyc_ref, dres_ref, dln_ref):
        @pl.when(pl.program_id(0) == 0)
        def _():
            dln_ref[...] = jnp.zeros_like(dln_ref)

        dy = dx_ref[...]
        dz, dg_rows = _ln_bwd(dy, z_ref[...], g_ref[...])
        dzb_ref[...] = dz.astype(dzb_ref.dtype)
        dres_ref[...] = ALPHA * dz
        dyc_ref[...] = _mm_nt(dz, w_ref[...])
        dln_ref[0:1, :] += jnp.sum(dg_rows, axis=0, keepdims=True)
        dln_ref[1:2, :] += jnp.sum(dy, axis=0, keepdims=True)

    row = pl.BlockSpec((TM, D_MODEL), lambda i: (i, 0))
    return pl.pallas_call(
        body, name="outproj_bwd", grid=(T // TM,),
        in_specs=[row, row, pl.BlockSpec((D_MODEL, D_MODEL), lambda i: (0, 0)), _vec_spec()],
        out_specs=[row, row, row, pl.BlockSpec((SUB, D_MODEL), lambda i: (0, 0))],
        out_shape=[jax.ShapeDtypeStruct((T, D_MODEL), _BF), jax.ShapeDtypeStruct((T, D_MODEL), F32),
                   jax.ShapeDtypeStruct((T, D_MODEL), F32), jax.ShapeDtypeStruct((SUB, D_MODEL), F32)],
        compiler_params=_cparams(("arbitrary",)),
    )(dx1, z1, w, g)


def _inproj_bwd_call(dres, dp, w):
    T = dres.shape[0]

    def body(dres_ref, dp_ref, w_ref, dx_ref):
        dx_ref[...] = dres_ref[...] + _mm_nt(dp_ref[...], w_ref[...])

    row = pl.BlockSpec((TM, D_MODEL), lambda i: (i, 0))
    return pl.pallas_call(
        body, name="inproj_bwd", grid=(T // TM,),
        in_specs=[row, pl.BlockSpec((TM, IN_WIDTH), lambda i: (i, 0)),
                  pl.BlockSpec((D_MODEL, IN_WIDTH), lambda i: (0, 0), pipeline_mode=pl.Buffered(1))],
        out_specs=row,
        out_shape=jax.ShapeDtypeStruct((T, D_MODEL), F32),
        compiler_params=_cparams(("parallel",)),
    )(dres, dp, w)


def _tn_cols_call(a, b, name):
    T, ka = a.shape
    n = b.shape[1]

    def body(a_ref, b_ref, o_ref):
        o_ref[...] = _mm_tn(a_ref[...], b_ref[...]).astype(o_ref.dtype)

    return pl.pallas_call(
        body, name=name, grid=(n // N_BLK,),
        in_specs=[pl.BlockSpec((T, ka), lambda j: (0, 0), pipeline_mode=pl.Buffered(1)),
                  pl.BlockSpec((T, N_BLK), lambda j: (0, j))],
        out_specs=pl.BlockSpec((ka, N_BLK), lambda j: (0, j)),
        out_shape=jax.ShapeDtypeStruct((ka, n), _BF),
        compiler_params=_cparams(("parallel",)),
    )(a, b)


def _tn_rows_call(a, b, kb, name):
    T, ka = a.shape
    n = b.shape[1]

    def body(a_ref, b_ref, o_ref):
        o_ref[...] = _mm_tn(a_ref[...], b_ref[...]).astype(o_ref.dtype)

    return pl.pallas_call(
        body, name=name, grid=(ka // kb,),
        in_specs=[pl.BlockSpec((T, kb), lambda r: (0, r)),
                  pl.BlockSpec((T, n), lambda r: (0, 0), pipeline_mode=pl.Buffered(1))],
        out_specs=pl.BlockSpec((kb, n), lambda r: (r, 0)),
        out_shape=jax.ShapeDtypeStruct((ka, n), _BF),
        compiler_params=_cparams(("parallel",)),
    )(a, b)


def _me():
    return lax.axis_index("x"), lax.axis_index("y"), lax.axis_index("c")


def _flip(k):
    x, y, c = _me()
    return (1 - x if k & 4 else x, 1 - y if k & 2 else y, 1 - c if k & 1 else c)


def _dev_index(pos):
    return 4 * pos[0] + 2 * pos[1] + pos[2]


_HBM = pl.BlockSpec(memory_space=pltpu.HBM)
_SEM = pl.BlockSpec(memory_space=pltpu.SEMAPHORE)


def _land_shape(shape, mode):
    if mode == "all":
        return (N_DEV,) + shape
    if mode == "cols":
        return (shape[0], N_DEV * shape[1])
    if mode == "blk":
        return shape
    assert mode == "scols"
    return (N_DEV, shape[0], shape[1] // N_DEV)


def _comm_copies(ins, lands, modes, send_sems, recv_sems):
    me = _dev_index(_me())
    copies = []
    for k in range(N_DEV):
        peer = _flip(k)
        pidx = _dev_index(peer)
        for a, (src, land, mode) in enumerate(zip(ins, lands, modes)):
            if mode == "blk":
                src = src.at[pidx]
            elif mode == "scols":
                w = src.shape[1] // N_DEV
                src = src.at[:, pl.ds(pl.multiple_of(pidx * w, LANE), w)]
            if mode == "cols":
                w = src.shape[1]
                dst = land.at[:, pl.ds(pl.multiple_of(me * w, LANE), w)]
            else:
                dst = land.at[me]
            copies.append(pltpu.make_async_remote_copy(
                src_ref=src, dst_ref=dst, send_sem=send_sems.at[k * len(ins) + a], recv_sem=recv_sems.at[k * len(ins) + a],
                device_id=peer, device_id_type=MESH))
    return copies


def _comm_start_call(arrs, gather_flags, after, name):
    n = len(arrs)
    lands = [lax.empty(_land_shape(v.shape, mode), v.dtype) for v, mode in zip(arrs, gather_flags)]

    def body(*refs):
        ins, lnd = refs[:n], refs[n:2 * n]
        send_sems, recv_sems = refs[2 * n + len(after)], refs[2 * n + len(after) + 1]
        for cp in _comm_copies(ins, lnd, gather_flags, send_sems, recv_sems):
            cp.start()
        refs[-1][...] = jnp.zeros_like(refs[-1])

    hbm = [pltpu.with_memory_space_constraint(v, pltpu.HBM) for v in list(arrs) + lands]
    out = pl.pallas_call(
        body, name=name,
        out_shape=(pltpu.SemaphoreType.DMA((N_DEV * n,)), pltpu.SemaphoreType.DMA((N_DEV * n,)),
                   *[pltpu.HBM(v.shape, v.dtype) for v in hbm], jax.ShapeDtypeStruct((SUB, LANE), F32)),
        in_specs=[_HBM] * (2 * n) + [pl.BlockSpec(memory_space=pl.ANY)] * len(after),
        out_specs=(_SEM, _SEM, *[_HBM] * (2 * n), pl.BlockSpec(memory_space=pltpu.VMEM)),
        input_output_aliases={i: 2 + i for i in range(2 * n)},
        compiler_params=pltpu.CompilerParams(has_side_effects=pltpu.SideEffectType.DATAFLOW_SIDE_EFFECTING),
    )(*hbm, *after)
    return out[:-1], out[-1]


def _comm_wait_call(state, gather_flags, after, name):
    n = len(gather_flags)
    send_sems, recv_sems, thru = state[0], state[1], state[2:]

    def body(*refs):
        ins, lnd, ssem, rsem = refs[:n], refs[n:2 * n], refs[2 * n], refs[2 * n + 1]
        for cp in _comm_copies(ins, lnd, gather_flags, ssem, rsem):
            cp.wait_send()
            cp.wait_recv()

    out = pl.pallas_call(
        body, name=name,
        out_shape=tuple(pltpu.HBM(v.shape, v.dtype) for v in thru),
        in_specs=[_HBM] * (2 * n) + [_SEM, _SEM] + [pl.BlockSpec(memory_space=pl.ANY)] * len(after),
        out_specs=tuple([_HBM] * (2 * n)),
        input_output_aliases={i: i for i in range(2 * n)},
        compiler_params=pltpu.CompilerParams(has_side_effects=pltpu.SideEffectType.DATAFLOW_SIDE_EFFECTING),
    )(*thru, send_sems, recv_sems, *after)
    return out[n:]


def _sum8_call(recv, rows, name):
    _, r, c = recv.shape

    def body(x_ref, o_ref):
        acc = x_ref[0].astype(F32)
        for s in range(1, N_DEV):
            acc = acc + x_ref[s].astype(F32)
        o_ref[...] = acc

    return pl.pallas_call(
        body, name=name, grid=(r // rows,),
        in_specs=[pl.BlockSpec((N_DEV, rows, c), lambda i: (0, i, 0))],
        out_specs=pl.BlockSpec((rows, c), lambda i: (i, 0)),
        out_shape=jax.ShapeDtypeStruct((r, c), F32),
        compiler_params=_cparams(("parallel",)),
    )(recv)


def _adamw_call(w, g, m, v, rows, name):
    r, c = w.shape

    def body(w_ref, g_ref, m_ref, v_ref, d_ref, nm_ref, nv_ref):
        gr = g_ref[...]
        nm = ADAM_B1 * m_ref[...] + (1.0 - ADAM_B1) * gr
        nv = ADAM_B2 * v_ref[...] + (1.0 - ADAM_B2) * (gr * gr)
        m_hat = nm / (1.0 - ADAM_B1 ** ADAM_STEP)
        v_hat = nv / (1.0 - ADAM_B2 ** ADAM_STEP)
        d_ref[...] = -ADAM_LR * (m_hat / (jnp.sqrt(v_hat) + ADAM_EPS) + ADAM_WD * w_ref[...])
        nm_ref[...] = nm
        nv_ref[...] = nv

    spec = pl.BlockSpec((rows, c), lambda i: (i, 0))
    return pl.pallas_call(
        body, name=name, grid=(r // rows,),
        in_specs=[spec] * 4, out_specs=[spec] * 3,
        out_shape=[jax.ShapeDtypeStruct((r, c), F32)] * 3,
        compiler_params=_cparams(("parallel",)),
    )(w, g, m, v)


SH_ROWS = 16
SH_W = LRU_WIDTH // N_DEV
REP_ROWS = 824
_REP_SIZES = (LRU_WIDTH, 2 * 6 * 64 * 64, 2 * 6 * 64 * 64, RET_WIDTH, 1920, D_MODEL, D_MODEL, D_MODEL, D_MODEL)
_RPB_SIZE = NA_HEADS * (2 * NA_KH - 1) * (2 * NA_KW - 1)


def _pack_sh(cw, ba, bx, lam):
    return jnp.concatenate([cw, ba, bx, lam], axis=0)


def _pad_sh(p):
    pad = [(0, 0)] * (p.ndim - 2) + [(0, SH_ROWS - p.shape[-2]), (0, LANE - p.shape[-1])]
    return jnp.pad(p, pad)


def _pack_rep(cb, wa, wx, gnw, rpb, l1g, l1b, l2g, l2b):
    flat = jnp.concatenate([cb.reshape(-1), wa.reshape(-1), wx.reshape(-1), gnw.reshape(-1),
                            jnp.pad(rpb.reshape(-1), (0, 1920 - _RPB_SIZE)), l1g, l1b, l2g, l2b,
                            jnp.zeros((REP_ROWS * LANE - sum(_REP_SIZES),), F32)])
    return flat.reshape(REP_ROWS, LANE)


def _unpack_rep(p):
    nl = p.shape[0]
    flat = p.reshape(nl, -1)
    out, off = [], 0
    for size in _REP_SIZES:
        out.append(flat[:, off:off + size])
        off += size
    cb, wa, wx, gnw, rpb, l1g, l1b, l2g, l2b = out
    return (cb, wa.reshape(nl, 2, 6, 64, 64), wx.reshape(nl, 2, 6, 64, 64), gnw,
            rpb[:, :_RPB_SIZE].reshape(nl, NA_HEADS, 2 * NA_KH - 1, 2 * NA_KW - 1), l1g, l1b, l2g, l2b)


def _adamw_nd(w, g, m, v, rows, name):
    shp = w.shape
    f = lambda t: t.reshape(-1, shp[-1])
    return [t.reshape(shp) for t in _adamw_call(f(w), f(g), f(m), f(v), rows, name)]


def kernel(x, w_in, conv_w, conv_b, lru_w_a, lru_b_a, lru_w_x, lru_b_x, lru_lam, ret_gn_w, na_rpb, w_out, ln1_g, ln1_b, w_gate, w_up, w_down, ln2_g, ln2_b, loss_target, m_w_in, m_conv_w, m_conv_b, m_lru_w_a, m_lru_b_a, m_lru_w_x, m_lru_b_x, m_lru_lam, m_ret_gn_w, m_na_rpb, m_w_out, m_ln1_g, m_ln1_b, m_w_gate, m_w_up, m_w_down, m_ln2_g, m_ln2_b, v_w_in, v_conv_w, v_conv_b, v_lru_w_a, v_lru_b_a, v_lru_w_x, v_lru_b_x, v_lru_lam, v_ret_gn_w, v_na_rpb, v_w_out, v_ln1_g, v_ln1_b, v_w_gate, v_w_up, v_w_down, v_ln2_g, v_ln2_b):
    nl = w_in.shape[0]
    T = x.shape[1]
    rows_n = T // GRID_W
    x0, target = x[0], loss_target[0]
    ffpad = W_BLK - FF_BLK

    win_b = w_in.astype(_BF)
    wg_b = jnp.pad(w_gate, ((0, 0), (0, 0), (0, ffpad))).astype(_BF)
    wu_b = jnp.pad(w_up, ((0, 0), (0, 0), (0, ffpad))).astype(_BF)
    wd_b = jnp.pad(w_down, ((0, 0), (0, ffpad), (0, 0))).astype(_BF)
    wout_b = w_out.astype(_BF)
    agf_modes = ["cols", "all"]
    agk_modes = ["cols", "cols", "all", "all"]

    def agf_start(l, after):
        sh = _pad_sh(_pack_sh(conv_w[l], lru_b_a[l], lru_b_x[l], lru_lam[l]))
        return _comm_start_call([win_b[l], sh], agf_modes, after, f"agf_start{l}")

    def agk_start(l, after):
        return _comm_start_call([wg_b[l], wu_b[l], wd_b[l], wout_b[l]], agk_modes, after, f"agk_start{l}")

    tables = _ret_tables(T)
    layers = []
    gathered = []
    xs, xb = x0, x0.astype(_BF)
    agf_state, token = agf_start(0, [])
    for l in range(nl):
        win, shg = _comm_wait_call(agf_state, agf_modes, [xb], f"agf_wait{l}")
        agk_state, token = agk_start(l, [shg])
        full = shg[:, :10, :SH_W].transpose(1, 0, 2).reshape(10, LRU_WIDTH)
        vec, w4 = _lru_pack(full[0:4], conv_b[l], lru_w_a[l], full[4:6], lru_w_x[l], full[6:8], full[8:10])
        gnw8 = jnp.pad(ret_gn_w[l][None], ((0, SUB - 1), (0, 0)))
        btab = _na_bias_tables(na_rpb[l], rows_n)
        proj = _inproj_call(xb, win, token)
        y_lru = _lru_fwd_call(proj, vec, w4)
        y_ret = _ret_fwd_call(proj, tables, gnw8)
        y_na = _na_fwd_call(proj, btab)
        wg, wu, wd, wout = _comm_wait_call(agk_state, agk_modes, [y_na], f"agk_wait{l}")
        wd, wout = wd.reshape(IN_WIDTH, D_MODEL), wout.reshape(D_MODEL, D_MODEL)
        gathered.append((win, wg, wu, wd, wout))
        if l + 1 < nl:
            agf_state, token = agf_start(l + 1, [wout])
        z1, x1, x1b, ycb = _outproj_ln_call(y_lru, y_ret, y_na, xs, wout, ln1_g[l][None], ln1_b[l][None], token)
        z2, x2, x2b = _ffn_ln_call(x1, x1b, wg, wu, wd, ln2_g[l][None], ln2_b[l][None])
        layers.append(dict(xb=xb, proj=proj, vec=vec, w4=w4, gnw8=gnw8, btab=btab,
                           z1=z1, x1=x1, x1b=x1b, ycb=ycb, z2=z2))
        xs, xb = x2, x2b

    dx, loss_blk = _loss_call(xs, target)
    loss = lax.psum(loss_blk[0, 0], ("x", "y", "c"))

    gxa_flags = ["scols", "scols", "blk", "blk"]
    gxb_flags = ["scols", "blk", "all"]
    gxa_state, gxb_state = [None] * nl, [None] * nl
    token = loss_blk
    for l in reversed(range(nl)):
        s = layers[l]
        win, wg, wu, wd, wout = gathered[l]
        dx1, dgp, dup, hid, dz2b, dln2 = _ffn_bwd_call(dx, s["z2"], s["x1"], s["x1b"], wg, wu, wd, ln2_g[l][None], token)
        dwg = _tn_cols_call(s["x1b"], dgp, "tn_cols")
        dwu = _tn_cols_call(s["x1b"], dup, "tn_cols")
        dwd = _tn_rows_call(hid, dz2b, N_BLK, "tn_rows_down").reshape(N_DEV, W_BLK, D_MODEL)
        dz1b, dyc, dres, dln1 = _outproj_bwd_call(dx1, s["z1"], wout, ln1_g[l][None])
        dwout = _tn_rows_call(s["ycb"], dz1b, D_MODEL // 2, "tn_rows_out").reshape(N_DEV, LANE, D_MODEL)
        gxa_state[l], token = _comm_start_call([dwg, dwu, dwd, dwout], gxa_flags, [], f"gxa_start{l}")
        dp, dvec, dw4 = _lru_bwd_call(s["proj"], dyc, s["vec"], s["w4"], token)
        dp, dgnw = _ret_bwd_call(s["proj"], dyc, tables, s["gnw8"], dp)
        dp, dbias = _na_bwd_call(s["proj"], dyc, s["btab"], dp)
        dwin = _tn_cols_call(s["xb"], dp, "tn_cols")
        dx = _inproj_bwd_call(dres, dp, win)
        dcw, dcb, dwa, dba, dwx, dbx, dlam = _lru_unpack(dvec, dw4)
        rep = _pack_rep(dcb, dwa, dwx, dgnw[0], _na_bias_grad(dbias, rows_n), dln1[0], dln1[1], dln2[0], dln2[1])
        sh = _pack_sh(dcw, dba, dbx, dlam).reshape(10, N_DEV, SH_W).transpose(1, 0, 2)
        gxb_state[l], token = _comm_start_call([dwin, _pad_sh(sh), rep], gxb_flags, [], f"gxb_start{l}")

    g_big = [[None] * nl for _ in range(5)]
    g_sh = [None] * nl
    g_rep = [None] * nl
    after = [dx, token]
    for l in reversed(range(nl)):
        ra = _comm_wait_call(gxa_state[l], gxa_flags, after, f"gxa_wait{l}")
        g_big[1][l] = _sum8_call(ra[0], TM, "sum8_cols")[:, :FF_BLK]
        g_big[2][l] = _sum8_call(ra[1], TM, "sum8_cols")[:, :FF_BLK]
        g_big[3][l] = _sum8_call(ra[2], W_BLK, "sum8_down")[:FF_BLK]
        g_big[4][l] = _sum8_call(ra[3], LANE, "sum8_out")
        rb = _comm_wait_call(gxb_state[l], gxb_flags, [g_big[4][l]], f"gxb_wait{l}")
        g_big[0][l] = _sum8_call(rb[0], TM, "sum8_cols")
        g_sh[l] = _sum8_call(rb[1], SH_ROWS, "sum8_sh")
        g_rep[l] = _sum8_call(rb[2], REP_ROWS, "sum8_rep")
        after = [g_rep[l]]

    g_w_in, g_w_gate, g_w_up, g_w_down, g_w_out = [jnp.stack(t) for t in g_big]
    big = {
        "w_in": _adamw_nd(w_in, g_w_in, m_w_in, v_w_in, TM, "adamw_in"),
        "w_gate": _adamw_nd(w_gate, g_w_gate, m_w_gate, v_w_gate, TM, "adamw_ff"),
        "w_up": _adamw_nd(w_up, g_w_up, m_w_up, v_w_up, TM, "adamw_ff"),
        "w_down": _adamw_nd(w_down, g_w_down, m_w_down, v_w_down, FF_BLK, "adamw_down"),
        "w_out": _adamw_nd(w_out, g_w_out, m_w_out, v_w_out, LANE, "adamw_out"),
    }
    g_shp = jnp.stack(g_sh)
    pack_sh = lambda cw, ba, bx, lam: _pad_sh(jnp.concatenate([cw, ba, bx, lam], axis=1))
    sh_out = _adamw_nd(pack_sh(conv_w, lru_b_a, lru_b_x, lru_lam), g_shp,
                       pack_sh(m_conv_w, m_lru_b_a, m_lru_b_x, m_lru_lam),
                       pack_sh(v_conv_w, v_lru_b_a, v_lru_b_x, v_lru_lam), SH_ROWS, "adamw_sh")

    def split_sh(p):
        p = p[:, :, :SH_W]
        return {"conv_w": p[:, 0:4], "lru_b_a": p[:, 4:6], "lru_b_x": p[:, 6:8], "lru_lam": p[:, 8:10]}

    g_repp = jnp.stack(g_rep)
    pack_rep = lambda *ps: jnp.stack([_pack_rep(*[p[l] for p in ps]) for l in range(nl)])
    rep_names = ("conv_b", "lru_w_a", "lru_w_x", "ret_gn_w", "na_rpb", "ln1_g", "ln1_b", "ln2_g", "ln2_b")
    rep_out = _adamw_nd(pack_rep(conv_b, lru_w_a, lru_w_x, ret_gn_w, na_rpb, ln1_g, ln1_b, ln2_g, ln2_b), g_repp,
                        pack_rep(m_conv_b, m_lru_w_a, m_lru_w_x, m_ret_gn_w, m_na_rpb, m_ln1_g, m_ln1_b, m_ln2_g, m_ln2_b),
                        pack_rep(v_conv_b, v_lru_w_a, v_lru_w_x, v_ret_gn_w, v_na_rpb, v_ln1_g, v_ln1_b, v_ln2_g, v_ln2_b),
                        REP_ROWS, "adamw_rep")

    grads = {"w_in": g_w_in, "w_gate": g_w_gate, "w_up": g_w_up, "w_down": g_w_down, "w_out": g_w_out}
    grads.update(split_sh(g_shp))
    grads.update(dict(zip(rep_names, _unpack_rep(g_repp))))
    kinds = []
    for k in range(3):
        d = {n: big[n][k] for n in big}
        d.update(split_sh(sh_out[k]))
        d.update(dict(zip(rep_names, _unpack_rep(rep_out[k]))))
        kinds.append(d)
    order = ("w_in", "conv_w", "conv_b", "lru_w_a", "lru_b_a", "lru_w_x", "lru_b_x", "lru_lam", "ret_gn_w", "na_rpb",
             "w_out", "ln1_g", "ln1_b", "w_gate", "w_up", "w_down", "ln2_g", "ln2_b")
    outs = [loss, dx[None]]
    for d in (grads, *kinds):
        outs.extend(d[n] for n in order)
    return tuple(outs)
```

```python
import functools
import math

import numpy as np
import jax
import jax.numpy as jnp
from jax import lax
from jax.experimental import pallas as pl
from jax.experimental.pallas import tpu as pltpu

F32 = jnp.float32
_BF = jnp.bfloat16

D_MODEL = 1024
DEPTH = 4
GRID_W = 64
HEAD_DIM = 64
LRU_WIDTH = 384
RET_WIDTH = 384
RET_HEADS = 6
NA_WIDTH = 256
NA_HEADS = 4
IN_WIDTH = 3072
CONV_WIDTH = 4
LRU_C = 8.0
RET_CHUNK = 128
ROPE_BASE = 10000.0
GN_EPS = 1e-6
NA_KH = 8
NA_KW = 16
D_FF = 2816
FF_BLK = 352
N_DEV = 8
ALPHA = (2 * DEPTH) ** 0.25
LN_EPS = 1e-5
ADAM_LR = 0.001
ADAM_B1 = 0.9
ADAM_B2 = 0.999
ADAM_EPS = 1e-08
ADAM_WD = 0.01
ADAM_STEP = 10

LANE = 128
SUB = 8
VMEM_MB = 56
NEG = -1e30

MESH = pl.DeviceIdType.MESH


def _cparams(sem=None, vmem_mb=VMEM_MB):
    return pltpu.CompilerParams(dimension_semantics=sem, vmem_limit_bytes=vmem_mb << 20)


def _mm(a, b):
    return jnp.dot(a.astype(_BF), b.astype(_BF), preferred_element_type=F32)


def _mm_nt(a, b):
    return lax.dot_general(a.astype(_BF), b.astype(_BF), (((1,), (1,)), ((), ())), preferred_element_type=F32)


def _mm_tn(a, b):
    return lax.dot_general(a.astype(_BF), b.astype(_BF), (((0,), (0,)), ((), ())), preferred_element_type=F32)


def _sigmoid(x):
    return jax.nn.sigmoid(x)


def _rows(start, size):
    return pl.ds(pl.multiple_of(start, SUB), size)


def _loop2(n, body, init):
    assert n % 2 == 0
    return lax.fori_loop(0, n // 2, lambda i, c: body(2 * i + 1, body(2 * i, c)), init)


def _strip(T, col):
    return pl.BlockSpec((T, LANE), lambda j: (0, col(j)), pipeline_mode=pl.Buffered(1))


LRU_CH = 256
_GELU_C0 = math.sqrt(2.0 / math.pi)
_GELU_C1 = 0.044715


def _gelu_parts(x):
    x2 = x * x
    t = jnp.tanh(_GELU_C0 * (x + _GELU_C1 * x * x2))
    val = 0.5 * x * (1.0 + t)
    der = 0.5 * (1.0 + t) + 0.5 * x * (1.0 - t * t) * _GELU_C0 * (1.0 + 3.0 * _GELU_C1 * x2)
    return val, der


def _softplus_neg(lam):
    e = jnp.exp(-jnp.abs(lam))
    w = 1.0 + e
    l1p = jnp.where(w == 1.0, e, jnp.log(w) * (e / jnp.where(w == 1.0, 1.0, w - 1.0)))
    return jnp.maximum(-lam, 0.0) + l1p


def _window(ref, t0, ch, T):
    prev = ref[_rows(jnp.maximum(t0 - SUB, 0), SUB), :].astype(F32)
    nxt = ref[_rows(jnp.minimum(t0 + ch, T - SUB), SUB), :].astype(F32)
    prev = jnp.where(t0 > 0, prev, 0.0)
    nxt = jnp.where(t0 + ch < T, nxt, 0.0)
    return jnp.concatenate([prev, ref[_rows(t0, ch), :].astype(F32), nxt], axis=0)


def _tap(win, shift, ch):
    n = win.shape[0]
    return pltpu.roll(win, (-shift) % n, 0)[SUB:SUB + ch]


def _lru_conv(xb_ref, vec, t0, T):
    win = _window(xb_ref, t0, LRU_CH, T)
    xc = jnp.broadcast_to(vec[4:5, :], (LRU_CH, LANE))
    for j in range(CONV_WIDTH):
        xc = xc + _tap(win, j - CONV_WIDTH // 2, LRU_CH) * vec[j:j + 1, :]
    return xc


def _lru_dir(pre_a, pre_x, sp):
    r = _sigmoid(pre_a)
    i = _sigmoid(pre_x)
    log_a = (-LRU_C) * r * sp
    a = jnp.exp(log_a)
    z = jnp.tanh(-log_a) * (a * a + 1.0)
    s = jnp.sqrt(z)
    return r, i, a, s


def _scan_tile(a, b, reverse, row):
    for k in (1, 2, 4):
        if not reverse:
            a_s, b_s, m = pltpu.roll(a, k, 0), pltpu.roll(b, k, 0), row >= k
        else:
            a_s, b_s, m = pltpu.roll(a, SUB - k, 0), pltpu.roll(b, SUB - k, 0), row < SUB - k
        b = jnp.where(m, a * b_s + b, b)
        a = jnp.where(m, a * a_s, a)
    return a, b


def _bcast_row(x, r):
    return jnp.broadcast_to(x[r:r + 1, :], (SUB, LANE))


def _lru_prepare(xb_ref, w4_ref, vec, xc_ref, af_ref, uf_ref, ab_ref, ub_ref, T):
    sp_f = _softplus_neg(vec[9:10, :])
    sp_b = _softplus_neg(vec[10:11, :])
    w4 = w4_ref[0]

    def body(c, carry):
        t0 = c * LRU_CH
        xc = _lru_conv(xb_ref, vec, t0, T)
        if xc_ref is not None:
            xc_ref[_rows(t0, LRU_CH), :] = xc
        pre = _mm(xc, w4)
        _, i, a, s = _lru_dir(pre[:, 0:128] + vec[5:6, :], pre[:, 128:256] + vec[6:7, :], sp_f)
        af_ref[_rows(t0, LRU_CH), :] = a
        uf_ref[_rows(t0, LRU_CH), :] = s * (i * xc)
        _, i, a, s = _lru_dir(pre[:, 256:384] + vec[7:8, :], pre[:, 384:512] + vec[8:9, :], sp_b)
        ab_ref[_rows(t0, LRU_CH), :] = a
        ub_ref[_rows(t0, LRU_CH), :] = s * (i * xc)
        return carry

    lax.fori_loop(0, T // LRU_CH, body, 0)


def _lru_scan(af_ref, uf_ref, ab_ref, ub_ref, T):
    nt = T // SUB
    row = lax.broadcasted_iota(jnp.int32, (SUB, LANE), 0)

    def body(j, carry):
        hf, hb = carry
        sf = _rows(j * SUB, SUB)
        sb = _rows((nt - 1 - j) * SUB, SUB)
        a, b = _scan_tile(af_ref[sf, :], uf_ref[sf, :], False, row)
        h = a * hf + b
        uf_ref[sf, :] = h
        hf = _bcast_row(h, SUB - 1)
        a, b = _scan_tile(ab_ref[sb, :], ub_ref[sb, :], True, row)
        h = a * hb + b
        ub_ref[sb, :] = h
        hb = _bcast_row(h, 0)
        return hf, hb

    z = jnp.zeros((SUB, LANE), F32)
    lax.fori_loop(0, nt, body, (z, z))


def _lru_fwd_call(proj, vec, w4):
    T = proj.shape[0]

    def body(xb_ref, gate_ref, vec_ref, w4_ref, y_ref, af_ref, uf_ref, ab_ref, ub_ref):
        vec = vec_ref[...]
        _lru_prepare(xb_ref, w4_ref, vec, None, af_ref, uf_ref, ab_ref, ub_ref, T)
        _lru_scan(af_ref, uf_ref, ab_ref, ub_ref, T)

        def out(c, carry):
            rows = _rows(c * LRU_CH, LRU_CH)
            gl, _ = _gelu_parts(gate_ref[rows, :])
            y_ref[rows, :] = (uf_ref[rows, :] + ub_ref[rows, :]) * gl
            return carry

        lax.fori_loop(0, T // LRU_CH, out, 0)

    return pl.pallas_call(
        body, name="lru_fwd", grid=(LRU_WIDTH // LANE,),
        in_specs=[_strip(T, lambda j: j), _strip(T, lambda j: j + 3),
                  pl.BlockSpec((16, LANE), lambda j: (0, j)),
                  pl.BlockSpec((1, LANE, 4 * LANE), lambda j: (j, 0, 0))],
        out_specs=_strip(T, lambda j: j),
        out_shape=jax.ShapeDtypeStruct((T, LRU_WIDTH), F32),
        scratch_shapes=[pltpu.VMEM((T, LANE), F32)] * 4,
        compiler_params=_cparams(("arbitrary",)),
    )(proj, proj, vec, w4)


def _store_strips(stage_ref, dp_ref, cols, sems):
    copies = [pltpu.make_async_copy(stage_ref.at[b], dp_ref.at[:, pl.ds(pl.multiple_of(c * LANE, LANE), LANE)], sems.at[b])
              for b, c in enumerate(cols)]
    for cp in copies:
        cp.start()
    for cp in copies:
        cp.wait()


def _lru_bwd_call(proj, dycat, vec, w4, after):
    T = proj.shape[0]
    nt = T // SUB
    nch = T // LRU_CH

    def body(xb_ref, gate_ref, dy_ref, vec_ref, w4_ref, after_ref, dp_ref, dvec_ref, dw4_ref,
             xc_ref, af_ref, hf_ref, ab_ref, hb_ref, dh_ref, stage_ref, sems):
        dxb_ref, dgate_ref = stage_ref.at[0], stage_ref.at[1]
        vec = vec_ref[...]
        _lru_prepare(xb_ref, w4_ref, vec, xc_ref, af_ref, hf_ref, ab_ref, hb_ref, T)
        _lru_scan(af_ref, hf_ref, ab_ref, hb_ref, T)

        def gate_bwd(c, carry):
            rows = _rows(c * LRU_CH, LRU_CH)
            gl, dgl = _gelu_parts(gate_ref[rows, :])
            dy = dy_ref[rows, :]
            dgate_ref[rows, :] = (dy * (hf_ref[rows, :] + hb_ref[rows, :]) * dgl).astype(dgate_ref.dtype)
            dh_ref[rows, :] = dy * gl
            return carry

        lax.fori_loop(0, nch, gate_bwd, 0)

        row = lax.broadcasted_iota(jnp.int32, (SUB, LANE), 0)

        def adj(j, carry):
            gf, a_next, gb, a_prev = carry
            tf = nt - 1 - j
            sf = _rows(tf * SUB, SUB)
            a_t = af_ref[sf, :]
            h_t = hf_ref[sf, :]
            coef = jnp.where(row == SUB - 1, a_next, pltpu.roll(a_t, SUB - 1, 0))
            ac, bc = _scan_tile(coef, dh_ref[sf, :], True, row)
            g = ac * gf + bc
            h_prev = hf_ref[_rows(jnp.maximum(tf - 1, 0) * SUB, SUB), :]
            h_prev = jnp.where(tf > 0, _bcast_row(h_prev, SUB - 1), 0.0)
            hs = jnp.where(row == 0, h_prev, pltpu.roll(h_t, 1, 0))
            af_ref[sf, :] = g * hs
            hf_ref[sf, :] = g
            gf = _bcast_row(g, 0)
            a_next = _bcast_row(a_t, 0)
            sb = _rows(j * SUB, SUB)
            a_t = ab_ref[sb, :]
            h_t = hb_ref[sb, :]
            coef = jnp.where(row == 0, a_prev, pltpu.roll(a_t, 1, 0))
            ac, bc = _scan_tile(coef, dh_ref[sb, :], False, row)
            g = ac * gb + bc
            h_next = hb_ref[_rows(jnp.minimum(j + 1, nt - 1) * SUB, SUB), :]
            h_next = jnp.where(j < nt - 1, _bcast_row(h_next, 0), 0.0)
            hs = jnp.where(row == SUB - 1, h_next, pltpu.roll(h_t, SUB - 1, 0))
            ab_ref[sb, :] = g * hs
            hb_ref[sb, :] = g
            gb = _bcast_row(g, SUB - 1)
            a_prev = _bcast_row(a_t, SUB - 1)
            return gf, a_next, gb, a_prev

        z = jnp.zeros((SUB, LANE), F32)
        lax.fori_loop(0, nt, adj, (z, z, z, z))

        sp_f = _softplus_neg(vec[9:10, :])
        sp_b = _softplus_neg(vec[10:11, :])
        w4 = w4_ref[0]
        dw4_ref[...] = jnp.zeros_like(dw4_ref)

        def one_dir(pre_a, pre_x, sp, xc, du, da):
            r, i, a, s = _lru_dir(pre_a, pre_x, sp)
            d_i = du * s * xc
            dxc = du * s * i
            d_s = du * i * xc
            d_log = da * a - d_s * (a * a) / s
            d_r = d_log * (-LRU_C) * sp
            d_sp = jnp.sum(d_log * (-LRU_C) * r, axis=0, keepdims=True)
            return d_r * r * (1.0 - r), d_i * i * (1.0 - i), dxc, d_sp

        def gates_bwd(c, carry):
            db, dspf, dspb = carry
            rows = _rows(c * LRU_CH, LRU_CH)
            xc = xc_ref[rows, :]
            pre = _mm(xc, w4)
            dpa_f, dpx_f, dxc_f, d_sp_f = one_dir(pre[:, 0:128] + vec[5:6, :], pre[:, 128:256] + vec[6:7, :],
                                                  sp_f, xc, hf_ref[rows, :], af_ref[rows, :])
            dpa_b, dpx_b, dxc_b, d_sp_b = one_dir(pre[:, 256:384] + vec[7:8, :], pre[:, 384:512] + vec[8:9, :],
                                                  sp_b, xc, hb_ref[rows, :], ab_ref[rows, :])
            dpre = jnp.concatenate([dpa_f, dpx_f, dpa_b, dpx_b], axis=1)
            dw4_ref[0] += _mm_tn(xc, dpre)
            dh_ref[rows, :] = dxc_f + dxc_b + _mm_nt(dpre, w4)
            return db + jnp.sum(dpre, axis=0, keepdims=True), dspf + d_sp_f, dspb + d_sp_b

        z1 = jnp.zeros((1, LANE), F32)
        db, dspf, dspb = lax.fori_loop(0, nch, gates_bwd, (jnp.zeros((1, 4 * LANE), F32), z1, z1))

        def conv_bwd(c, carry):
            t0 = c * LRU_CH
            rows = _rows(t0, LRU_CH)
            dwin = _window(dh_ref, t0, LRU_CH, T)
            xwin = _window(xb_ref, t0, LRU_CH, T)
            dxc = dh_ref[rows, :]
            dxb = jnp.zeros((LRU_CH, LANE), F32)
            out = []
            for j in range(CONV_WIDTH):
                off = j - CONV_WIDTH // 2
                dxb = dxb + _tap(dwin, -off, LRU_CH) * vec[j:j + 1, :]
                out.append(carry[j] + jnp.sum(dxc * _tap(xwin, off, LRU_CH), axis=0, keepdims=True))
            dxb_ref[rows, :] = dxb.astype(dxb_ref.dtype)
            out.append(carry[CONV_WIDTH] + jnp.sum(dxc, axis=0, keepdims=True))
            return tuple(out)

        dconv = lax.fori_loop(0, nch, conv_bwd, (z1,) * (CONV_WIDTH + 1))
        dlam_f = dspf * (-_sigmoid(-vec[9:10, :]))
        dlam_b = dspb * (-_sigmoid(-vec[10:11, :]))
        dvec_ref[...] = jnp.concatenate(
            list(dconv) + [db[:, 0:128], db[:, 128:256], db[:, 256:384], db[:, 384:512], dlam_f, dlam_b,
                           jnp.zeros((5, LANE), F32)], axis=0)
        j = pl.program_id(0)
        _store_strips(stage_ref, dp_ref, (j, j + 3), sems)

    ns = LRU_WIDTH // LANE
    return pl.pallas_call(
        body, name="lru_bwd", grid=(ns,),
        in_specs=[_strip(T, lambda j: j), _strip(T, lambda j: j + 3), _strip(T, lambda j: j),
                  pl.BlockSpec((16, LANE), lambda j: (0, j)),
                  pl.BlockSpec((1, LANE, 4 * LANE), lambda j: (j, 0, 0)),
                  pl.BlockSpec(memory_space=pl.ANY)],
        out_specs=[pl.BlockSpec(memory_space=pl.ANY),
                   pl.BlockSpec((16, LANE), lambda j: (0, j)),
                   pl.BlockSpec((1, LANE, 4 * LANE), lambda j: (j, 0, 0))],
        out_shape=[jax.ShapeDtypeStruct((T, IN_WIDTH), _BF),
                   jax.ShapeDtypeStruct((16, LRU_WIDTH), F32), jax.ShapeDtypeStruct((ns, LANE, 4 * LANE), F32)],
        scratch_shapes=[pltpu.VMEM((T, LANE), F32)] * 6 + [pltpu.VMEM((2, T, LANE), _BF), pltpu.SemaphoreType.DMA((2,))],
        compiler_params=_cparams(("arbitrary",)),
    )(proj, proj, dycat, vec, w4, after)


def _lru_vec(cw, cb, ba, bx, lam):
    return jnp.concatenate([cw, cb[None], ba[0:1], bx[0:1], ba[1:2], bx[1:2], lam, jnp.zeros((5, LRU_WIDTH), F32)], axis=0)


def _lru_w4(wa, wx):
    nl = wa.shape[0]
    w = jnp.stack([wa[:, 0], wx[:, 0], wa[:, 1], wx[:, 1]], axis=1)
    w = w.reshape(nl, 4, 3, 2, 64, 64)
    eye = jnp.eye(2, dtype=w.dtype)
    bd = w[:, :, :, :, :, None, :] * eye[None, None, None, :, None, :, None]
    bd = bd.reshape(nl, 4, 3, LANE, LANE)
    return bd.transpose(0, 2, 3, 1, 4).reshape(nl, 3, LANE, 4 * LANE).astype(_BF)


def _lru_unpack(dvec, dw4):
    def blocks(m):
        m = m.reshape(3, 2, 64, 2, 64)
        return jnp.stack([m[:, 0, :, 0, :], m[:, 1, :, 1, :]], axis=1).reshape(6, 64, 64)
    parts = [blocks(dw4[:, :, k * LANE:(k + 1) * LANE]) for k in range(4)]
    dwa = jnp.stack([parts[0], parts[2]])
    dwx = jnp.stack([parts[1], parts[3]])
    dba = jnp.stack([dvec[5], dvec[7]])
    dbx = jnp.stack([dvec[6], dvec[8]])
    return dvec[0:4], dvec[4], dwa, dba, dwx, dbx, dvec[9:11]


RC = 2 * RET_CHUNK


def _ret_tables(T):
    half = HEAD_DIM // 2
    pos = jnp.arange(T, dtype=F32)
    inv_freq = ROPE_BASE ** (-jnp.arange(half, dtype=F32) / half)
    ang = pos[:, None] * inv_freq[None, :]
    cos = jnp.tile(jnp.cos(ang), (1, 4))
    sin = jnp.tile(jnp.concatenate([-jnp.sin(ang), jnp.sin(ang)], axis=1), (1, 2))
    log_g = jnp.log1p(-jnp.exp2(-5.0 - jnp.arange(RET_HEADS, dtype=F32)))
    idx = jnp.arange(RC, dtype=F32)
    dec = jnp.exp(jnp.abs(idx[:, None] - idx[None, :]) * log_g[:, None, None])
    lg = jnp.repeat(log_g, HEAD_DIM).reshape(3, 1, LANE)
    col = idx[None, :, None]
    rtab = jnp.stack([jnp.exp((RC - 1 - col) * lg), jnp.exp(col * lg),
                      jnp.exp((col + 1.0) * lg), jnp.exp((RC - col) * lg)], axis=1)
    gch = jnp.broadcast_to(jnp.exp(RC * lg), (3, SUB, LANE))
    return cos, sin, dec, rtab, gch


def _swap32(x, lane):
    return jnp.where((lane & 32) == 0, pltpu.roll(x, LANE - 32, 1), pltpu.roll(x, 32, 1))


def _head_mean(x, m0, m1):
    s0 = jnp.sum(x * m0, axis=-1, keepdims=True)
    s1 = jnp.sum(x * m1, axis=-1, keepdims=True)
    return (s0 * m0 + s1 * m1) * (1.0 / HEAD_DIM)


def _ret_masks():
    lane = lax.broadcasted_iota(jnp.int32, (RC, LANE), 1)
    m0 = (lane < HEAD_DIM).astype(F32)
    r = lax.broadcasted_iota(jnp.int32, (LANE, LANE), 0) // HEAD_DIM
    c = lax.broadcasted_iota(jnp.int32, (LANE, LANE), 1) // HEAD_DIM
    return lane, m0, 1.0 - m0, (r == c).astype(F32)


def _ret_specs(T):
    const = lambda shape, imap: pl.BlockSpec(shape, imap)
    return [_strip(T, lambda j: j + 6), _strip(T, lambda j: j + 9), _strip(T, lambda j: j + 12),
            _strip(T, lambda j: j + 15),
            pl.BlockSpec((T, LANE), lambda j: (0, 0), pipeline_mode=pl.Buffered(1)),
            pl.BlockSpec((T, LANE), lambda j: (0, 0), pipeline_mode=pl.Buffered(1)),
            const((2, RC, RC), lambda j: (j, 0, 0)),
            const((1, 4, RC, LANE), lambda j: (j, 0, 0, 0)),
            const((1, SUB, LANE), lambda j: (j, 0, 0)),
            const((SUB, LANE), lambda j: (0, j))]


def _ret_fwd_call(proj, tables, gnw8):
    T = proj.shape[0]
    nc = T // RC
    cos, sin, dec, rtab, gch = tables

    def body(q_ref, k_ref, v_ref, g_ref, cos_ref, sin_ref, dec_ref, rtab_ref, gch_ref, gnw_ref, y_ref, stf_ref):
        lane, m0, m1, bd = _ret_masks()
        gch_v = gch_ref[0][0:1, :]
        gnw = gnw_ref[0:1, :]
        dkf, dkb, dqf, dqb = rtab_ref[0, 0], rtab_ref[0, 1], rtab_ref[0, 2], rtab_ref[0, 3]

        def rope(x, rows):
            return x * cos_ref[rows, :] + _swap32(x, lane) * sin_ref[rows, :]

        def pass_a(n, st):
            rows = _rows(n * RC, RC)
            stf_ref[n] = st
            kr = rope(k_ref[rows, :], rows) * (HEAD_DIM ** -0.5)
            return gch_v * st + _mm_tn(kr * dkf, v_ref[rows, :]) * bd

        _loop2(nc, pass_a, jnp.zeros((LANE, LANE), F32))

        def pass_b(i, stb):
            n = nc - 1 - i
            rows = _rows(n * RC, RC)
            qr = rope(q_ref[rows, :], rows)
            kr = rope(k_ref[rows, :], rows) * (HEAD_DIM ** -0.5)
            v = v_ref[rows, :]
            o = _mm(qr * dqf, stf_ref[n]) + _mm(qr * dqb, stb)
            for h, m in ((0, m0), (1, m1)):
                s = _mm_nt(qr * m, kr) * dec_ref[h]
                o = o + _mm(s, v * m)
            oc = o - _head_mean(o, m0, m1)
            on = oc * lax.rsqrt(_head_mean(oc * oc, m0, m1) + GN_EPS)
            g = g_ref[rows, :]
            y_ref[rows, :] = (g * _sigmoid(g)) * (on * gnw)
            return gch_v * stb + _mm_tn(kr * dkb, v) * bd

        _loop2(nc, pass_b, jnp.zeros((LANE, LANE), F32))

    return pl.pallas_call(
        body, name="ret_fwd", grid=(RET_WIDTH // LANE,),
        in_specs=_ret_specs(T),
        out_specs=_strip(T, lambda j: j),
        out_shape=jax.ShapeDtypeStruct((T, RET_WIDTH), F32),
        scratch_shapes=[pltpu.VMEM((nc, LANE, LANE), F32)],
        compiler_params=_cparams(("arbitrary",)),
    )(proj, proj, proj, proj, cos, sin, dec, rtab, gch, gnw8)


def _ret_bwd_call(proj, dycat, tables, gnw8, dp):
    T = proj.shape[0]
    nc = T // RC
    cos, sin, dec, rtab, gch = tables

    def body(q_ref, k_ref, v_ref, g_ref, cos_ref, sin_ref, dec_ref, rtab_ref, gch_ref, gnw_ref, dy_ref, dp_in_ref,
             dp_out_ref, dgnw_ref, stf_ref, dstb_ref, dkr_ref, dv_ref, dp_ref, sems):
        lane, m0, m1, bd = _ret_masks()
        gch_v = gch_ref[0][0:1, :]
        gnw = gnw_ref[0:1, :]
        dkf, dkb, dqf, dqb = rtab_ref[0, 0], rtab_ref[0, 1], rtab_ref[0, 2], rtab_ref[0, 3]
        scale = HEAD_DIM ** -0.5
        zst = jnp.zeros((LANE, LANE), F32)

        def rope(x, rows):
            return x * cos_ref[rows, :] + _swap32(x, lane) * sin_ref[rows, :]

        def rope_t(d, rows):
            return d * cos_ref[rows, :] + _swap32(d * sin_ref[rows, :], lane)

        def pass_a(n, st):
            rows = _rows(n * RC, RC)
            stf_ref[n] = st
            kr = rope(k_ref[rows, :], rows) * scale
            return gch_v * st + _mm_tn(kr * dkf, v_ref[rows, :]) * bd

        _loop2(nc, pass_a, zst)

        def pass_b(i, carry):
            stb, d_f, dgnw = carry
            n = nc - 1 - i
            rows = _rows(n * RC, RC)
            qr = rope(q_ref[rows, :], rows)
            kr = rope(k_ref[rows, :], rows) * scale
            v = v_ref[rows, :]
            stf = stf_ref[n]
            qf = qr * dqf
            qb = qr * dqb
            o = _mm(qf, stf) + _mm(qb, stb)
            s_h = []
            for h, m in ((0, m0), (1, m1)):
                s = _mm_nt(qr * m, kr) * dec_ref[h]
                s_h.append(s)
                o = o + _mm(s, v * m)
            oc = o - _head_mean(o, m0, m1)
            rstd = lax.rsqrt(_head_mean(oc * oc, m0, m1) + GN_EPS)
            on = oc * rstd
            g = g_ref[rows, :]
            sg = _sigmoid(g)
            dy = dy_ref[rows, :]
            dp_ref[3, rows, :] = (dy * (on * gnw) * (sg * (1.0 + g * (1.0 - sg)))).astype(dp_ref.dtype)
            t = dy * (g * sg)
            dgnw = dgnw + jnp.sum(t * on, axis=0, keepdims=True)
            don = t * gnw
            do = rstd * (don - _head_mean(don, m0, m1) - on * _head_mean(don * on, m0, m1))
            dqr = _mm_nt(do, stf) * dqf + _mm_nt(do, stb) * dqb
            dkr = _mm_nt(v, d_f) * dkf
            dv = _mm(kr * dkf, d_f)
            for h, m in ((0, m0), (1, m1)):
                ds = _mm_nt(do * m, v) * dec_ref[h]
                dqr = dqr + _mm(ds, kr * m)
                dkr = dkr + _mm_tn(ds, qr * m)
                dv = dv + _mm_tn(s_h[h], do * m)
            dp_ref[0, rows, :] = rope_t(dqr, rows).astype(dp_ref.dtype)
            dkr_ref[rows, :] = dkr
            dv_ref[rows, :] = dv
            dstb_ref[n] = _mm_tn(qb, do) * bd
            d_f = _mm_tn(qf, do) * bd + gch_v * d_f
            stb = gch_v * stb + _mm_tn(kr * dkb, v) * bd
            return stb, d_f, dgnw

        _, _, dgnw = _loop2(nc, pass_b, (zst, zst, jnp.zeros((1, LANE), F32)))
        dgnw_ref[...] = jnp.concatenate([dgnw, jnp.zeros((SUB - 1, LANE), F32)], axis=0)

        def pass_c(n, d_b):
            rows = _rows(n * RC, RC)
            kr = rope(k_ref[rows, :], rows) * scale
            v = v_ref[rows, :]
            dkr = dkr_ref[rows, :] + _mm_nt(v, d_b) * dkb
            dp_ref[1, rows, :] = (rope_t(dkr, rows) * scale).astype(dp_ref.dtype)
            dp_ref[2, rows, :] = (dv_ref[rows, :] + _mm(kr * dkb, d_b)).astype(dp_ref.dtype)
            return dstb_ref[n] + gch_v * d_b

        _loop2(nc, pass_c, zst)
        j = pl.program_id(0)
        _store_strips(dp_ref, dp_out_ref, (j + 6, j + 9, j + 12, j + 15), sems)

    n_in = len(_ret_specs(T)) + 1
    return pl.pallas_call(
        body, name="ret_bwd", grid=(RET_WIDTH // LANE,),
        in_specs=_ret_specs(T) + [_strip(T, lambda j: j + 3), pl.BlockSpec(memory_space=pl.ANY)],
        out_specs=[pl.BlockSpec(memory_space=pl.ANY), pl.BlockSpec((SUB, LANE), lambda j: (0, j))],
        out_shape=[jax.ShapeDtypeStruct(dp.shape, dp.dtype), jax.ShapeDtypeStruct((SUB, RET_WIDTH), F32)],
        scratch_shapes=[pltpu.VMEM((nc, LANE, LANE), F32), pltpu.VMEM((nc, LANE, LANE), F32),
                        pltpu.VMEM((T, LANE), F32), pltpu.VMEM((T, LANE), F32),
                        pltpu.VMEM((4, T, LANE), _BF), pltpu.SemaphoreType.DMA((4,))],
        input_output_aliases={n_in: 0},
        compiler_params=_cparams(("arbitrary",)),
    )(proj, proj, proj, proj, cos, sin, dec, rtab, gch, gnw8, dycat, dp)


NA_Q = 2 * GRID_W
NA_WROWS = 10
NA_K = NA_WROWS * GRID_W
NA_CHUNKS = NA_K // LANE
NA_TYPES = 5
_ONEHOT_PRECISION = lax.Precision.HIGH


def _na_onehots(rows_n):
    reps = [(0, 0), (2, 0), (4, 0), (rows_n - 4, rows_n - NA_WROWS), (rows_n - 2, rows_n - NA_WROWS)]
    rm = np.zeros((NA_TYPES, 2, NA_WROWS, 2 * NA_KH - 1), np.float32)
    for t, (r, ws) in enumerate(reps):
        for qh in range(2):
            qrow = r + qh
            rstart = min(max(qrow - NA_KH // 2, 0), rows_n - NA_KH)
            for kh in range(NA_WROWS):
                krow = ws + kh
                if rstart <= krow < rstart + NA_KH:
                    rm[t, qh, kh, krow - qrow + NA_KH - 1] = 1.0
    cm = np.zeros((GRID_W, GRID_W, 2 * NA_KW - 1), np.float32)
    for qc in range(GRID_W):
        cstart = min(max(qc - NA_KW // 2, 0), GRID_W - NA_KW)
        for kc in range(cstart, cstart + NA_KW):
            cm[qc, kc, kc - qc + NA_KW - 1] = 1.0
    rm2 = rm.reshape(NA_TYPES, 2, NA_CHUNKS, 2, 2 * NA_KH - 1)
    cm2 = np.zeros((GRID_W, LANE, 2, 2 * NA_KW - 1), np.float32)
    for z in range(2):
        cm2[:, z * GRID_W:(z + 1) * GRID_W, z, :] = cm
    return rm2, cm2


def _na_bias_tables(rpb, rows_n):
    rm, cm = _na_onehots(rows_n)
    val = jnp.einsum("hab,tqpza,xkzb->htpqxk", rpb, rm, cm, precision=_ONEHOT_PRECISION)
    valid = np.einsum("tqpz,xkz->tpqxk", rm.sum(-1), cm.sum(-1)) > 0.5
    return jnp.where(valid[None], val, NEG).reshape(2, 2, NA_TYPES, NA_CHUNKS, NA_Q, LANE)


def _na_bias_grad(dtab, rows_n):
    rm, cm = _na_onehots(rows_n)
    d6 = dtab.reshape(NA_HEADS, NA_TYPES, NA_CHUNKS, 2, GRID_W, LANE)
    return jnp.einsum("htpqxk,tqpza,xkzb->hab", d6, rm, cm, precision=_ONEHOT_PRECISION)


def _na_bias(b_ref, h, typ):
    return jnp.concatenate([b_ref[0, h, typ, c] for c in range(NA_CHUNKS)], axis=1)


def _na_step(p, npairs, rows_n):
    ws = jnp.clip(2 * p - NA_KH // 2, 0, rows_n - NA_WROWS)
    koff = pl.multiple_of(ws * GRID_W, LANE)
    typ = jnp.where(p == 0, 0, jnp.where(p == 1, 1, jnp.where(p == npairs - 2, 3, jnp.where(p == npairs - 1, 4, 2))))
    return _rows(p * NA_Q, NA_Q), pl.ds(koff, NA_K), typ


def _na_fwd_call(proj, btab):
    T = proj.shape[0]
    npairs, rows_n = T // NA_Q, T // GRID_W

    def body(q_ref, k_ref, v_ref, b_ref, o_ref):
        lane = lax.broadcasted_iota(jnp.int32, (NA_Q, LANE), 1)
        m0 = (lane < HEAD_DIM).astype(F32)
        m1 = 1.0 - m0

        def step(p, carry):
            qrows, krows, typ = _na_step(p, npairs, rows_n)
            q = q_ref[qrows, :]
            kw = k_ref[krows, :]
            vw = v_ref[krows, :]
            o = jnp.zeros((NA_Q, LANE), F32)
            for h, m in ((0, m0), (1, m1)):
                s = _mm_nt(q * m, kw) * (HEAD_DIM ** -0.5) + _na_bias(b_ref, h, typ)
                e = jnp.exp(s - jnp.max(s, axis=-1, keepdims=True))
                pr = e / jnp.sum(e, axis=-1, keepdims=True)
                o = o + _mm(pr, vw) * m
            o_ref[qrows, :] = o
            return carry

        _loop2(npairs, step, 0)

    return pl.pallas_call(
        body, name="na_fwd", grid=(NA_WIDTH // LANE,),
        in_specs=[_strip(T, lambda j: j + 18), _strip(T, lambda j: j + 20), _strip(T, lambda j: j + 22),
                  pl.BlockSpec((1, 2, NA_TYPES, NA_CHUNKS, NA_Q, LANE), lambda j: (j, 0, 0, 0, 0, 0))],
        out_specs=_strip(T, lambda j: j),
        out_shape=jax.ShapeDtypeStruct((T, NA_WIDTH), F32),
        compiler_params=_cparams(("arbitrary",)),
    )(proj, proj, proj, btab)


def _na_bwd_call(proj, dycat, btab, dp):
    T = proj.shape[0]
    npairs, rows_n = T // NA_Q, T // GRID_W
    scale = HEAD_DIM ** -0.5

    def body(q_ref, k_ref, v_ref, do_ref, b_ref, dp_in_ref, dp_out_ref, db_ref, dka_ref, dva_ref, stage_ref, sems):
        dq_ref = stage_ref.at[0]
        lane = lax.broadcasted_iota(jnp.int32, (NA_Q, LANE), 1)
        m0 = (lane < HEAD_DIM).astype(F32)
        m1 = 1.0 - m0
        dka_ref[...] = jnp.zeros_like(dka_ref)
        dva_ref[...] = jnp.zeros_like(dva_ref)
        db_ref[...] = jnp.zeros_like(db_ref)

        def step(p, carry):
            qrows, krows, typ = _na_step(p, npairs, rows_n)
            q = q_ref[qrows, :]
            do = do_ref[qrows, :]
            kw = k_ref[krows, :]
            vw = v_ref[krows, :]
            dq = jnp.zeros((NA_Q, LANE), F32)
            dk = jnp.zeros((NA_K, LANE), F32)
            dv = jnp.zeros((NA_K, LANE), F32)
            for h, m in ((0, m0), (1, m1)):
                qm = q * m
                dom = do * m
                s = _mm_nt(qm, kw) * scale + _na_bias(b_ref, h, typ)
                e = jnp.exp(s - jnp.max(s, axis=-1, keepdims=True))
                pr = e / jnp.sum(e, axis=-1, keepdims=True)
                dpr = _mm_nt(dom, vw)
                ds = pr * (dpr - jnp.sum(pr * dpr, axis=-1, keepdims=True))
                for c in range(NA_CHUNKS):
                    db_ref[0, h, typ, c] += ds[:, c * LANE:(c + 1) * LANE]
                dsb = (ds * scale).astype(_BF)
                dq = dq + _mm(dsb, kw) * m
                dk = dk + _mm_tn(dsb, qm)
                dv = dv + _mm_tn(pr, dom)
            dq_ref[qrows, :] = dq.astype(dq_ref.dtype)
            dka_ref[krows, :] += dk
            dva_ref[krows, :] += dv
            return carry

        _loop2(npairs, step, 0)
        stage_ref[1] = dka_ref[...].astype(stage_ref.dtype)
        stage_ref[2] = dva_ref[...].astype(stage_ref.dtype)
        j = pl.program_id(0)
        _store_strips(stage_ref, dp_out_ref, (j + 18, j + 20, j + 22), sems)

    tab = pl.BlockSpec((1, 2, NA_TYPES, NA_CHUNKS, NA_Q, LANE), lambda j: (j, 0, 0, 0, 0, 0))
    return pl.pallas_call(
        body, name="na_bwd", grid=(NA_WIDTH // LANE,),
        in_specs=[_strip(T, lambda j: j + 18), _strip(T, lambda j: j + 20), _strip(T, lambda j: j + 22),
                  _strip(T, lambda j: j + 6), tab, pl.BlockSpec(memory_space=pl.ANY)],
        out_specs=[pl.BlockSpec(memory_space=pl.ANY), tab],
        out_shape=[jax.ShapeDtypeStruct(dp.shape, dp.dtype),
                   jax.ShapeDtypeStruct((2, 2, NA_TYPES, NA_CHUNKS, NA_Q, LANE), F32)],
        scratch_shapes=[pltpu.VMEM((T, LANE), F32), pltpu.VMEM((T, LANE), F32),
                        pltpu.VMEM((3, T, LANE), _BF), pltpu.SemaphoreType.DMA((3,))],
        input_output_aliases={5: 0},
        compiler_params=_cparams(("arbitrary",)),
    )(proj, proj, proj, dycat, btab, dp)


W_BLK = IN_WIDTH // N_DEV
N_BLK = 768
N_STEPS = IN_WIDTH // N_BLK
TM = 512


def _ln_fwd(z, g, b):
    zc = z - jnp.mean(z, axis=-1, keepdims=True)
    var = jnp.mean(zc * zc, axis=-1, keepdims=True)
    return zc * lax.rsqrt(var + LN_EPS) * g + b


def _ln_bwd(dy, z, g):
    zc = z - jnp.mean(z, axis=-1, keepdims=True)
    rstd = lax.rsqrt(jnp.mean(zc * zc, axis=-1, keepdims=True) + LN_EPS)
    xhat = zc * rstd
    dxh = dy * g
    dz = rstd * (dxh - jnp.mean(dxh, axis=-1, keepdims=True) - xhat * jnp.mean(dxh * xhat, axis=-1, keepdims=True))
    return dz, dy * xhat


def _row_tile(T):
    return 1024 if T % 1024 == 0 else TM


def _inproj_call(xb, w, after):
    T = xb.shape[0]
    tm = _row_tile(T)

    def body(x_ref, w_ref, after_ref, o_ref):
        o_ref[...] = _mm(x_ref[...], w_ref[...])

    return pl.pallas_call(
        body, name="inproj", grid=(T // tm, N_STEPS),
        in_specs=[pl.BlockSpec((tm, D_MODEL), lambda i, n: (i, 0)),
                  pl.BlockSpec((D_MODEL, N_BLK), lambda i, n: (0, n)),
                  pl.BlockSpec(memory_space=pl.ANY)],
        out_specs=pl.BlockSpec((tm, N_BLK), lambda i, n: (i, n)),
        out_shape=jax.ShapeDtypeStruct((T, IN_WIDTH), F32),
        compiler_params=_cparams(("parallel", "arbitrary")),
    )(xb, w, after)


def _vec_spec():
    return pl.BlockSpec((1, D_MODEL), lambda *_: (0, 0))


def _outproj_ln_call(y_lru, y_ret, y_na, x, w, g, b, after):
    T = x.shape[0]

    def body(yl_ref, yr_ref, yn_ref, x_ref, w_ref, g_ref, b_ref, after_ref, z_ref, x1_ref, x1b_ref, yc_ref):
        yc_ref[:, 0:LRU_WIDTH] = yl_ref[...].astype(yc_ref.dtype)
        yc_ref[:, LRU_WIDTH:LRU_WIDTH + RET_WIDTH] = yr_ref[...].astype(yc_ref.dtype)
        yc_ref[:, LRU_WIDTH + RET_WIDTH:] = yn_ref[...].astype(yc_ref.dtype)
        z = ALPHA * x_ref[...] + _mm(yc_ref[...], w_ref[...])
        z_ref[...] = z
        x1 = _ln_fwd(z, g_ref[...], b_ref[...])
        x1_ref[...] = x1
        x1b_ref[...] = x1.astype(x1b_ref.dtype)

    row = lambda w_: pl.BlockSpec((TM, w_), lambda i: (i, 0))
    return pl.pallas_call(
        body, name="outproj_ln", grid=(T // TM,),
        in_specs=[row(LRU_WIDTH), row(RET_WIDTH), row(NA_WIDTH), row(D_MODEL),
                  pl.BlockSpec((D_MODEL, D_MODEL), lambda i: (0, 0)), _vec_spec(), _vec_spec(),
                  pl.BlockSpec(memory_space=pl.ANY)],
        out_specs=[row(D_MODEL)] * 4,
        out_shape=[jax.ShapeDtypeStruct((T, D_MODEL), F32), jax.ShapeDtypeStruct((T, D_MODEL), F32),
                   jax.ShapeDtypeStruct((T, D_MODEL), _BF), jax.ShapeDtypeStruct((T, D_MODEL), _BF)],
        compiler_params=_cparams(("parallel",)),
    )(y_lru, y_ret, y_na, x, w, g, b, after)


def _ffn_ln_call(x1, x1b, wg, wu, wd, g, b):
    T = x1.shape[0]

    def body(x_ref, xb_ref, wg_ref, wu_ref, wd_ref, g_ref, b_ref, z_ref, x2_ref, x2b_ref, acc_ref):
        n = pl.program_id(1)

        @pl.when(n == 0)
        def _():
            acc_ref[...] = jnp.zeros_like(acc_ref)

        xb = xb_ref[...]
        gp = _mm(xb, wg_ref[...])
        hid = gp * _sigmoid(gp) * _mm(xb, wu_ref[...])
        acc_ref[...] += _mm(hid, wd_ref[...])

        @pl.when(n == N_STEPS - 1)
        def _():
            z = ALPHA * x_ref[...] + acc_ref[...]
            z_ref[...] = z
            x2 = _ln_fwd(z, g_ref[...], b_ref[...])
            x2_ref[...] = x2
            x2b_ref[...] = x2.astype(x2b_ref.dtype)

    row = pl.BlockSpec((TM, D_MODEL), lambda i, n: (i, 0))
    return pl.pallas_call(
        body, name="ffn_ln", grid=(T // TM, N_STEPS),
        in_specs=[row, row,
                  pl.BlockSpec((D_MODEL, N_BLK), lambda i, n: (0, n)),
                  pl.BlockSpec((D_MODEL, N_BLK), lambda i, n: (0, n)),
                  pl.BlockSpec((N_BLK, D_MODEL), lambda i, n: (n, 0)), _vec_spec(), _vec_spec()],
        out_specs=[row] * 3,
        out_shape=[jax.ShapeDtypeStruct((T, D_MODEL), F32), jax.ShapeDtypeStruct((T, D_MODEL), F32),
                   jax.ShapeDtypeStruct((T, D_MODEL), _BF)],
        scratch_shapes=[pltpu.VMEM((TM, D_MODEL), F32)],
        compiler_params=_cparams(("parallel", "arbitrary")),
    )(x1, x1b, wg, wu, wd, g, b)


def _loss_call(y, t):
    T = y.shape[0]

    def body(y_ref, t_ref, dy_ref, loss_ref):
        @pl.when(pl.program_id(0) == 0)
        def _():
            loss_ref[...] = jnp.zeros_like(loss_ref)

        err = y_ref[...] - t_ref[...]
        dy_ref[...] = err * (1.0 / D_MODEL)
        part = 0.5 * jnp.sum(jnp.mean(err * err, axis=-1, keepdims=True), axis=0, keepdims=True)
        loss_ref[...] += jnp.broadcast_to(part, loss_ref.shape)

    row = pl.BlockSpec((TM, D_MODEL), lambda i: (i, 0))
    return pl.pallas_call(
        body, name="loss", grid=(T // TM,),
        in_specs=[row, row],
        out_specs=[row, pl.BlockSpec((SUB, LANE), lambda i: (0, 0))],
        out_shape=[jax.ShapeDtypeStruct((T, D_MODEL), F32), jax.ShapeDtypeStruct((SUB, LANE), F32)],
        compiler_params=_cparams(("arbitrary",)),
    )(y, t)


def _ffn_bwd_call(dx2, z2, x1, x1b, wg, wu, wd, g, after):
    T = x1.shape[0]

    def body(dx2_ref, z_ref, x_ref, xb_ref, wg_ref, wu_ref, wd_ref, g_ref, after_ref,
             dx1_ref, dgp_ref, dup_ref, hid_ref, dzb_ref, dln_ref, acc_ref):
        i, n = pl.program_id(0), pl.program_id(1)

        @pl.when((i == 0) & (n == 0))
        def _():
            dln_ref[...] = jnp.zeros_like(dln_ref)

        @pl.when(n == 0)
        def _():
            dy = dx2_ref[...]
            dz, dg_rows = _ln_bwd(dy, z_ref[...], g_ref[...])
            dzb_ref[...] = dz.astype(dzb_ref.dtype)
            acc_ref[...] = ALPHA * dz
            dln_ref[0:1, :] += jnp.sum(dg_rows, axis=0, keepdims=True)
            dln_ref[1:2, :] += jnp.sum(dy, axis=0, keepdims=True)

        xb = xb_ref[...]
        gp = _mm(xb, wg_ref[...])
        up = _mm(xb, wu_ref[...])
        sg = _sigmoid(gp)
        act = gp * sg
        hid_ref[...] = (act * up).astype(hid_ref.dtype)
        dhid = _mm_nt(dzb_ref[...], wd_ref[...])
        dup = dhid * act
        dgp = dhid * up * (sg * (1.0 + gp * (1.0 - sg)))
        dgp_ref[...] = dgp.astype(dgp_ref.dtype)
        dup_ref[...] = dup.astype(dup_ref.dtype)
        acc_ref[...] += _mm_nt(dgp, wg_ref[...]) + _mm_nt(dup, wu_ref[...])

        @pl.when(n == N_STEPS - 1)
        def _():
            dx1_ref[...] = acc_ref[...]

    row = pl.BlockSpec((TM, D_MODEL), lambda i, n: (i, 0))
    blk = pl.BlockSpec((TM, N_BLK), lambda i, n: (i, n))
    return pl.pallas_call(
        body, name="ffn_bwd", grid=(T // TM, N_STEPS),
        in_specs=[row, row, row, row,
                  pl.BlockSpec((D_MODEL, N_BLK), lambda i, n: (0, n)),
                  pl.BlockSpec((D_MODEL, N_BLK), lambda i, n: (0, n)),
                  pl.BlockSpec((N_BLK, D_MODEL), lambda i, n: (n, 0)), _vec_spec(),
                  pl.BlockSpec(memory_space=pl.ANY)],
        out_specs=[row, blk, blk, blk, row, pl.BlockSpec((SUB, D_MODEL), lambda i, n: (0, 0))],
        out_shape=[jax.ShapeDtypeStruct((T, D_MODEL), F32),
                   jax.ShapeDtypeStruct((T, IN_WIDTH), _BF), jax.ShapeDtypeStruct((T, IN_WIDTH), _BF),
                   jax.ShapeDtypeStruct((T, IN_WIDTH), _BF), jax.ShapeDtypeStruct((T, D_MODEL), _BF),
                   jax.ShapeDtypeStruct((SUB, D_MODEL), F32)],
        scratch_shapes=[pltpu.VMEM((TM, D_MODEL), F32)],
        compiler_params=_cparams(("arbitrary", "arbitrary")),
    )(dx2, z2, x1, x1b, wg, wu, wd, g, after)


def _outproj_bwd_call(dx1, z1, w, g):
    T = dx1.shape[0]

    def body(dx_ref, z_ref, w_ref, g_ref, dzb_ref, dyc_ref, dres_ref, dln_ref):
        @pl.when(pl.program_id(0) == 0)
        def _():
            dln_ref[...] = jnp.zeros_like(dln_ref)

        dy = dx_ref[...]
        dz, dg_rows = _ln_bwd(dy, z_ref[...], g_ref[...])
        dzb_ref[...] = dz.astype(dzb_ref.dtype)
        dres_ref[...] = ALPHA * dz
        dyc_ref[...] = _mm_nt(dz, w_ref[...])
        dln_ref[0:1, :] += jnp.sum(dg_rows, axis=0, keepdims=True)
        dln_ref[1:2, :] += jnp.sum(dy, axis=0, keepdims=True)

    row = pl.BlockSpec((TM, D_MODEL), lambda i: (i, 0))
    return pl.pallas_call(
        body, name="outproj_bwd", grid=(T // TM,),
        in_specs=[row, row, pl.BlockSpec((D_MODEL, D_MODEL), lambda i: (0, 0)), _vec_spec()],
        out_specs=[row, row, row, pl.BlockSpec((SUB, D_MODEL), lambda i: (0, 0))],
        out_shape=[jax.ShapeDtypeStruct((T, D_MODEL), _BF), jax.ShapeDtypeStruct((T, D_MODEL), F32),
                   jax.ShapeDtypeStruct((T, D_MODEL), F32), jax.ShapeDtypeStruct((SUB, D_MODEL), F32)],
        compiler_params=_cparams(("arbitrary",)),
    )(dx1, z1, w, g)


def _inproj_bwd_call(dres, dp, w):
    T = dres.shape[0]

    def body(dres_ref, dp_ref, w_ref, dx_ref):
        dx_ref[...] = dres_ref[...] + _mm_nt(dp_ref[...], w_ref[...])

    row = pl.BlockSpec((TM, D_MODEL), lambda i: (i, 0))
    return pl.pallas_call(
        body, name="inproj_bwd", grid=(T // TM,),
        in_specs=[row, pl.BlockSpec((TM, IN_WIDTH), lambda i: (i, 0)),
                  pl.BlockSpec((D_MODEL, IN_WIDTH), lambda i: (0, 0), pipeline_mode=pl.Buffered(1))],
        out_specs=row,
        out_shape=jax.ShapeDtypeStruct((T, D_MODEL), F32),
        compiler_params=_cparams(("parallel",)),
    )(dres, dp, w)


def _tn_cols_call(a, b, name):
    T, ka = a.shape
    n = b.shape[1]

    def body(a_ref, b_ref, o_ref):
        o_ref[...] = _mm_tn(a_ref[...], b_ref[...]).astype(o_ref.dtype)

    return pl.pallas_call(
        body, name=name, grid=(n // N_BLK,),
        in_specs=[pl.BlockSpec((T, ka), lambda j: (0, 0), pipeline_mode=pl.Buffered(1)),
                  pl.BlockSpec((T, N_BLK), lambda j: (0, j))],
        out_specs=pl.BlockSpec((ka, N_BLK), lambda j: (0, j)),
        out_shape=jax.ShapeDtypeStruct((ka, n), _BF),
        compiler_params=_cparams(("parallel",)),
    )(a, b)


def _tn_rows_call(a, b, kb, name):
    T, ka = a.shape
    n = b.shape[1]

    def body(a_ref, b_ref, o_ref):
        o_ref[...] = _mm_tn(a_ref[...], b_ref[...]).astype(o_ref.dtype)

    return pl.pallas_call(
        body, name=name, grid=(ka // kb,),
        in_specs=[pl.BlockSpec((T, kb), lambda r: (0, r)),
                  pl.BlockSpec((T, n), lambda r: (0, 0), pipeline_mode=pl.Buffered(1))],
        out_specs=pl.BlockSpec((kb, n), lambda r: (r, 0)),
        out_shape=jax.ShapeDtypeStruct((ka, n), _BF),
        compiler_params=_cparams(("parallel",)),
    )(a, b)


def _me():
    return lax.axis_index("x"), lax.axis_index("y"), lax.axis_index("c")


def _flip(k):
    x, y, c = _me()
    return (1 - x if k & 4 else x, 1 - y if k & 2 else y, 1 - c if k & 1 else c)


def _dev_index(pos):
    return 4 * pos[0] + 2 * pos[1] + pos[2]


_HBM = pl.BlockSpec(memory_space=pltpu.HBM)
_SEM = pl.BlockSpec(memory_space=pltpu.SEMAPHORE)


def _land_shape(shape, mode):
    if mode == "all":
        return (N_DEV,) + shape
    if mode == "cols":
        return (shape[0], N_DEV * shape[1])
    if mode == "blk":
        return shape
    assert mode == "scols"
    return (N_DEV, shape[0], shape[1] // N_DEV)


def _comm_copies(ins, lands, modes, send_sems, recv_sems):
    me = _dev_index(_me())
    copies = []
    for k in range(N_DEV):
        peer = _flip(k)
        pidx = _dev_index(peer)
        for a, (src, land, mode) in enumerate(zip(ins, lands, modes)):
            if mode == "blk":
                src = src.at[pidx]
            elif mode == "scols":
                w = src.shape[1] // N_DEV
                src = src.at[:, pl.ds(pl.multiple_of(pidx * w, LANE), w)]
            if mode == "cols":
                w = src.shape[1]
                dst = land.at[:, pl.ds(pl.multiple_of(me * w, LANE), w)]
            else:
                dst = land.at[me]
            copies.append(pltpu.make_async_remote_copy(
                src_ref=src, dst_ref=dst, send_sem=send_sems.at[k * len(ins) + a], recv_sem=recv_sems.at[k * len(ins) + a],
                device_id=peer, device_id_type=MESH))
    return copies


def _comm_start_call(arrs, gather_flags, after, name):
    n = len(arrs)
    lands = [lax.empty(_land_shape(v.shape, mode), v.dtype) for v, mode in zip(arrs, gather_flags)]

    def body(*refs):
        ins, lnd = refs[:n], refs[n:2 * n]
        send_sems, recv_sems = refs[2 * n + len(after)], refs[2 * n + len(after) + 1]
        for cp in _comm_copies(ins, lnd, gather_flags, send_sems, recv_sems):
            cp.start()
        refs[-1][...] = jnp.zeros_like(refs[-1])

    hbm = [pltpu.with_memory_space_constraint(v, pltpu.HBM) for v in list(arrs) + lands]
    out = pl.pallas_call(
        body, name=name,
        out_shape=(pltpu.SemaphoreType.DMA((N_DEV * n,)), pltpu.SemaphoreType.DMA((N_DEV * n,)),
                   *[pltpu.HBM(v.shape, v.dtype) for v in hbm], jax.ShapeDtypeStruct((SUB, LANE), F32)),
        in_specs=[_HBM] * (2 * n) + [pl.BlockSpec(memory_space=pl.ANY)] * len(after),
        out_specs=(_SEM, _SEM, *[_HBM] * (2 * n), pl.BlockSpec(memory_space=pltpu.VMEM)),
        input_output_aliases={i: 2 + i for i in range(2 * n)},
        compiler_params=pltpu.CompilerParams(has_side_effects=pltpu.SideEffectType.DATAFLOW_SIDE_EFFECTING),
    )(*hbm, *after)
    return out[:-1], out[-1]


def _comm_wait_call(state, gather_flags, after, name):
    n = len(gather_flags)
    send_sems, recv_sems, thru = state[0], state[1], state[2:]

    def body(*refs):
        ins, lnd, ssem, rsem = refs[:n], refs[n:2 * n], refs[2 * n], refs[2 * n + 1]
        for cp in _comm_copies(ins, lnd, gather_flags, ssem, rsem):
            cp.wait_send()
            cp.wait_recv()

    out = pl.pallas_call(
        body, name=name,
        out_shape=tuple(pltpu.HBM(v.shape, v.dtype) for v in thru),
        in_specs=[_HBM] * (2 * n) + [_SEM, _SEM] + [pl.BlockSpec(memory_space=pl.ANY)] * len(after),
        out_specs=tuple([_HBM] * (2 * n)),
        input_output_aliases={i: i for i in range(2 * n)},
        compiler_params=pltpu.CompilerParams(has_side_effects=pltpu.SideEffectType.DATAFLOW_SIDE_EFFECTING),
    )(*thru, send_sems, recv_sems, *after)
    return out[n:]


def _sum8_call(recv, rows, name):
    _, r, c = recv.shape

    def body(x_ref, o_ref):
        acc = x_ref[0].astype(F32)
        for s in range(1, N_DEV):
            acc = acc + x_ref[s].astype(F32)
        o_ref[...] = acc

    return pl.pallas_call(
        body, name=name, grid=(r // rows,),
        in_specs=[pl.BlockSpec((N_DEV, rows, c), lambda i: (0, i, 0))],
        out_specs=pl.BlockSpec((rows, c), lambda i: (i, 0)),
        out_shape=jax.ShapeDtypeStruct((r, c), F32),
        compiler_params=_cparams(("parallel",)),
    )(recv)


def _adamw_call(w, g, m, v, rows, name):
    r, c = w.shape

    def body(w_ref, g_ref, m_ref, v_ref, d_ref, nm_ref, nv_ref):
        gr = g_ref[...]
        nm = ADAM_B1 * m_ref[...] + (1.0 - ADAM_B1) * gr
        nv = ADAM_B2 * v_ref[...] + (1.0 - ADAM_B2) * (gr * gr)
        m_hat = nm / (1.0 - ADAM_B1 ** ADAM_STEP)
        v_hat = nv / (1.0 - ADAM_B2 ** ADAM_STEP)
        d_ref[...] = -ADAM_LR * (m_hat / (jnp.sqrt(v_hat) + ADAM_EPS) + ADAM_WD * w_ref[...])
        nm_ref[...] = nm
        nv_ref[...] = nv

    spec = pl.BlockSpec((rows, c), lambda i: (i, 0))
    return pl.pallas_call(
        body, name=name, grid=(r // rows,),
        in_specs=[spec] * 4, out_specs=[spec] * 3,
        out_shape=[jax.ShapeDtypeStruct((r, c), F32)] * 3,
        compiler_params=_cparams(("parallel",)),
    )(w, g, m, v)


SH_ROWS = 16
SH_W = LRU_WIDTH // N_DEV
REP_ROWS = 824
_REP_SIZES = (LRU_WIDTH, 2 * 6 * 64 * 64, 2 * 6 * 64 * 64, RET_WIDTH, 1920, D_MODEL, D_MODEL, D_MODEL, D_MODEL)
_RPB_SIZE = NA_HEADS * (2 * NA_KH - 1) * (2 * NA_KW - 1)


def _pack_sh(cw, ba, bx, lam):
    return jnp.concatenate([cw, ba, bx, lam], axis=0)


def _pad_sh(p):
    pad = [(0, 0)] * (p.ndim - 2) + [(0, SH_ROWS - p.shape[-2]), (0, LANE - p.shape[-1])]
    return jnp.pad(p, pad)


def _pack_rep(cb, wa, wx, gnw, rpb, l1g, l1b, l2g, l2b):
    flat = jnp.concatenate([cb.reshape(-1), wa.reshape(-1), wx.reshape(-1), gnw.reshape(-1),
                            jnp.pad(rpb.reshape(-1), (0, 1920 - _RPB_SIZE)), l1g, l1b, l2g, l2b,
                            jnp.zeros((REP_ROWS * LANE - sum(_REP_SIZES),), F32)])
    return flat.reshape(REP_ROWS, LANE)


def _unpack_rep(p):
    nl = p.shape[0]
    flat = p.reshape(nl, -1)
    out, off = [], 0
    for size in _REP_SIZES:
        out.append(flat[:, off:off + size])
        off += size
    cb, wa, wx, gnw, rpb, l1g, l1b, l2g, l2b = out
    return (cb, wa.reshape(nl, 2, 6, 64, 64), wx.reshape(nl, 2, 6, 64, 64), gnw,
            rpb[:, :_RPB_SIZE].reshape(nl, NA_HEADS, 2 * NA_KH - 1, 2 * NA_KW - 1), l1g, l1b, l2g, l2b)


def _adamw_nd(w, g, m, v, rows, name):
    shp = w.shape
    f = lambda t: t.reshape(-1, shp[-1])
    rows = f(w).shape[0] if rows is None else rows
    return [t.reshape(shp) for t in _adamw_call(f(w), f(g), f(m), f(v), rows, name)]


def kernel(x, w_in, conv_w, conv_b, lru_w_a, lru_b_a, lru_w_x, lru_b_x, lru_lam, ret_gn_w, na_rpb, w_out, ln1_g, ln1_b, w_gate, w_up, w_down, ln2_g, ln2_b, loss_target, m_w_in, m_conv_w, m_conv_b, m_lru_w_a, m_lru_b_a, m_lru_w_x, m_lru_b_x, m_lru_lam, m_ret_gn_w, m_na_rpb, m_w_out, m_ln1_g, m_ln1_b, m_w_gate, m_w_up, m_w_down, m_ln2_g, m_ln2_b, v_w_in, v_conv_w, v_conv_b, v_lru_w_a, v_lru_b_a, v_lru_w_x, v_lru_b_x, v_lru_lam, v_ret_gn_w, v_na_rpb, v_w_out, v_ln1_g, v_ln1_b, v_w_gate, v_w_up, v_w_down, v_ln2_g, v_ln2_b):
    nl = w_in.shape[0]
    T = x.shape[1]
    rows_n = T // GRID_W
    x0, target = x[0], loss_target[0]
    ffpad = W_BLK - FF_BLK

    win_b = w_in.astype(_BF)
    wg_b = jnp.pad(w_gate, ((0, 0), (0, 0), (0, ffpad))).astype(_BF)
    wu_b = jnp.pad(w_up, ((0, 0), (0, 0), (0, ffpad))).astype(_BF)
    wd_b = jnp.pad(w_down, ((0, 0), (0, ffpad), (0, 0))).astype(_BF)
    wout_b = w_out.astype(_BF)
    agf_modes = ["cols", "all"]
    agk_modes = ["cols", "cols", "all", "all"]

    def agf_start(l, after):
        sh = _pad_sh(_pack_sh(conv_w[l], lru_b_a[l], lru_b_x[l], lru_lam[l]))
        return _comm_start_call([win_b[l], sh], agf_modes, after, f"agf_start{l}")

    def agk_start(l, after):
        return _comm_start_call([wg_b[l], wu_b[l], wd_b[l], wout_b[l]], agk_modes, after, f"agk_start{l}")

    tables = _ret_tables(T)
    w4_all = _lru_w4(lru_w_a, lru_w_x)
    layers = []
    gathered = []
    xs, xb = x0, x0.astype(_BF)
    agf_state, token = agf_start(0, [])
    for l in range(nl):
        win, shg = _comm_wait_call(agf_state, agf_modes, [xb], f"agf_wait{l}")
        agk_state, token = agk_start(l, [shg])
        full = shg[:, :10, :SH_W].transpose(1, 0, 2).reshape(10, LRU_WIDTH)
        vec, w4 = _lru_vec(full[0:4], conv_b[l], full[4:6], full[6:8], full[8:10]), w4_all[l]
        gnw8 = jnp.pad(ret_gn_w[l][None], ((0, SUB - 1), (0, 0)))
        btab = _na_bias_tables(na_rpb[l], rows_n)
        proj = _inproj_call(xb, win, token)
        y_lru = _lru_fwd_call(proj, vec, w4)
        y_ret = _ret_fwd_call(proj, tables, gnw8)
        y_na = _na_fwd_call(proj, btab)
        wg, wu, wd, wout = _comm_wait_call(agk_state, agk_modes, [y_na], f"agk_wait{l}")
        wd, wout = wd.reshape(IN_WIDTH, D_MODEL), wout.reshape(D_MODEL, D_MODEL)
        gathered.append((win, wg, wu, wd, wout))
        if l + 1 < nl:
            agf_state, token = agf_start(l + 1, [wout])
        z1, x1, x1b, ycb = _outproj_ln_call(y_lru, y_ret, y_na, xs, wout, ln1_g[l][None], ln1_b[l][None], token)
        z2, x2, x2b = _ffn_ln_call(x1, x1b, wg, wu, wd, ln2_g[l][None], ln2_b[l][None])
        layers.append(dict(xb=xb, proj=proj, vec=vec, w4=w4, gnw8=gnw8, btab=btab,
                           z1=z1, x1=x1, x1b=x1b, ycb=ycb, z2=z2))
        xs, xb = x2, x2b

    dx, loss_blk = _loss_call(xs, target)
    loss = lax.psum(loss_blk[0, 0], ("x", "y", "c"))

    gxa_flags = ["scols", "scols", "blk", "blk"]
    gxb_flags = ["scols", "blk", "all"]
    gxa_state, gxb_state = [None] * nl, [None] * nl
    token = loss_blk
    for l in reversed(range(nl)):
        s = layers[l]
        win, wg, wu, wd, wout = gathered[l]
        dx1, dgp, dup, hid, dz2b, dln2 = _ffn_bwd_call(dx, s["z2"], s["x1"], s["x1b"], wg, wu, wd, ln2_g[l][None], token)
        dwg = _tn_cols_call(s["x1b"], dgp, "tn_cols")
        dwu = _tn_cols_call(s["x1b"], dup, "tn_cols")
        dwd = _tn_rows_call(hid, dz2b, N_BLK, "tn_rows_down").reshape(N_DEV, W_BLK, D_MODEL)
        dz1b, dyc, dres, dln1 = _outproj_bwd_call(dx1, s["z1"], wout, ln1_g[l][None])
        dwout = _tn_rows_call(s["ycb"], dz1b, D_MODEL // 2, "tn_rows_out").reshape(N_DEV, LANE, D_MODEL)
        gxa_state[l], token = _comm_start_call([dwg, dwu, dwd, dwout], gxa_flags, [], f"gxa_start{l}")
        dp, dvec, dw4 = _lru_bwd_call(s["proj"], dyc, s["vec"], s["w4"], token)
        dp, dgnw = _ret_bwd_call(s["proj"], dyc, tables, s["gnw8"], dp)
        dp, dbias = _na_bwd_call(s["proj"], dyc, s["btab"], dp)
        dwin = _tn_cols_call(s["xb"], dp, "tn_cols")
        dx = _inproj_bwd_call(dres, dp, win)
        dcw, dcb, dwa, dba, dwx, dbx, dlam = _lru_unpack(dvec, dw4)
        rep = _pack_rep(dcb, dwa, dwx, dgnw[0], _na_bias_grad(dbias, rows_n), dln1[0], dln1[1], dln2[0], dln2[1])
        sh = _pack_sh(dcw, dba, dbx, dlam).reshape(10, N_DEV, SH_W).transpose(1, 0, 2)
        gxb_state[l], token = _comm_start_call([dwin, _pad_sh(sh), rep], gxb_flags, [], f"gxb_start{l}")

    g_big = [[None] * nl for _ in range(5)]
    g_sh = [None] * nl
    g_rep = [None] * nl
    after = [dx, token]
    for l in reversed(range(nl)):
        ra = _comm_wait_call(gxa_state[l], gxa_flags, after, f"gxa_wait{l}")
        g_big[1][l] = _sum8_call(ra[0], TM, "sum8_cols")[:, :FF_BLK]
        g_big[2][l] = _sum8_call(ra[1], TM, "sum8_cols")[:, :FF_BLK]
        g_big[3][l] = _sum8_call(ra[2], W_BLK, "sum8_down")[:FF_BLK]
        g_big[4][l] = _sum8_call(ra[3], LANE, "sum8_out")
        rb = _comm_wait_call(gxb_state[l], gxb_flags, [g_big[4][l]], f"gxb_wait{l}")
        g_big[0][l] = _sum8_call(rb[0], TM, "sum8_cols")
        g_sh[l] = _sum8_call(rb[1], SH_ROWS, "sum8_sh")
        g_rep[l] = _sum8_call(rb[2], REP_ROWS, "sum8_rep")
        after = [g_rep[l]]

    g_w_in, g_w_gate, g_w_up, g_w_down, g_w_out = [jnp.stack(t) for t in g_big]
    big = {
        "w_in": _adamw_nd(w_in, g_w_in, m_w_in, v_w_in, TM, "adamw_in"),
        "w_gate": _adamw_nd(w_gate, g_w_gate, m_w_gate, v_w_gate, TM, "adamw_ff"),
        "w_up": _adamw_nd(w_up, g_w_up, m_w_up, v_w_up, TM, "adamw_ff"),
        "w_down": _adamw_nd(w_down, g_w_down, m_w_down, v_w_down, FF_BLK, "adamw_down"),
        "w_out": _adamw_nd(w_out, g_w_out, m_w_out, v_w_out, LANE, "adamw_out"),
    }
    g_shp = jnp.stack(g_sh)[:, :, :SH_W]
    rep_names = ("conv_b", "lru_w_a", "lru_w_x", "ret_gn_w", "na_rpb", "ln1_g", "ln1_b", "ln2_g", "ln2_b")
    grads = {"w_in": g_w_in, "w_gate": g_w_gate, "w_up": g_w_up, "w_down": g_w_down, "w_out": g_w_out,
             "conv_w": g_shp[:, 0:4], "lru_b_a": g_shp[:, 4:6], "lru_b_x": g_shp[:, 6:8], "lru_lam": g_shp[:, 8:10]}
    grads.update(dict(zip(rep_names, _unpack_rep(jnp.stack(g_rep)))))
    small = {
        "conv_w": (conv_w, m_conv_w, v_conv_w), "conv_b": (conv_b, m_conv_b, v_conv_b),
        "lru_w_a": (lru_w_a, m_lru_w_a, v_lru_w_a), "lru_b_a": (lru_b_a, m_lru_b_a, v_lru_b_a),
        "lru_w_x": (lru_w_x, m_lru_w_x, v_lru_w_x), "lru_b_x": (lru_b_x, m_lru_b_x, v_lru_b_x),
        "lru_lam": (lru_lam, m_lru_lam, v_lru_lam), "ret_gn_w": (ret_gn_w, m_ret_gn_w, v_ret_gn_w),
        "na_rpb": (na_rpb, m_na_rpb, v_na_rpb), "ln1_g": (ln1_g, m_ln1_g, v_ln1_g), "ln1_b": (ln1_b, m_ln1_b, v_ln1_b),
        "ln2_g": (ln2_g, m_ln2_g, v_ln2_g), "ln2_b": (ln2_b, m_ln2_b, v_ln2_b),
    }
    for name, (w_, m_, v_) in small.items():
        big[name] = _adamw_nd(w_, grads[name], m_, v_, None, "adamw_small")
    kinds = [{n: big[n][k] for n in big} for k in range(3)]
    order = ("w_in", "conv_w", "conv_b", "lru_w_a", "lru_b_a", "lru_w_x", "lru_b_x", "lru_lam", "ret_gn_w", "na_rpb",
             "w_out", "ln1_g", "ln1_b", "w_gate", "w_up", "w_down", "ln2_g", "ln2_b")
    outs = [loss, dx[None]]
    for d in (grads, *kinds):
        outs.extend(d[n] for n in order)
    return tuple(outs)
```

```python
import functools
import math

import numpy as np
import jax
import jax.numpy as jnp
from jax import lax
from jax.experimental import pallas as pl
from jax.experimental.pallas import tpu as pltpu

F32 = jnp.float32
_BF = jnp.bfloat16

D_MODEL = 1024
DEPTH = 4
GRID_W = 64
HEAD_DIM = 64
LRU_WIDTH = 384
RET_WIDTH = 384
RET_HEADS = 6
NA_WIDTH = 256
NA_HEADS = 4
IN_WIDTH = 3072
CONV_WIDTH = 4
LRU_C = 8.0
RET_CHUNK = 128
ROPE_BASE = 10000.0
GN_EPS = 1e-6
NA_KH = 8
NA_KW = 16
D_FF = 2816
FF_BLK = 352
N_DEV = 8
ALPHA = (2 * DEPTH) ** 0.25
LN_EPS = 1e-5
ADAM_LR = 0.001
ADAM_B1 = 0.9
ADAM_B2 = 0.999
ADAM_EPS = 1e-08
ADAM_WD = 0.01
ADAM_STEP = 10

LANE = 128
SUB = 8
VMEM_MB = 56
NEG = -1e30

MESH = pl.DeviceIdType.MESH


def _cparams(sem=None, vmem_mb=VMEM_MB):
    return pltpu.CompilerParams(dimension_semantics=sem, vmem_limit_bytes=vmem_mb << 20)


def _mm(a, b):
    return jnp.dot(a.astype(_BF), b.astype(_BF), preferred_element_type=F32)


def _mm_nt(a, b):
    return lax.dot_general(a.astype(_BF), b.astype(_BF), (((1,), (1,)), ((), ())), preferred_element_type=F32)


def _mm_tn(a, b):
    return lax.dot_general(a.astype(_BF), b.astype(_BF), (((0,), (0,)), ((), ())), preferred_element_type=F32)


def _sigmoid(x):
    return jax.nn.sigmoid(x)


def _rows(start, size):
    return pl.ds(pl.multiple_of(start, SUB), size)


def _loop2(n, body, init):
    assert n % 2 == 0
    return lax.fori_loop(0, n // 2, lambda i, c: body(2 * i + 1, body(2 * i, c)), init)


def _strip(T, col, buffers=2):
    return pl.BlockSpec((T, LANE), lambda j: (0, col(j)), pipeline_mode=pl.Buffered(buffers))


LRU_CH = 256
_GELU_C0 = math.sqrt(2.0 / math.pi)
_GELU_C1 = 0.044715


def _gelu_parts(x):
    x2 = x * x
    t = jnp.tanh(_GELU_C0 * (x + _GELU_C1 * x * x2))
    val = 0.5 * x * (1.0 + t)
    der = 0.5 * (1.0 + t) + 0.5 * x * (1.0 - t * t) * _GELU_C0 * (1.0 + 3.0 * _GELU_C1 * x2)
    return val, der


def _softplus_neg(lam):
    e = jnp.exp(-jnp.abs(lam))
    w = 1.0 + e
    l1p = jnp.where(w == 1.0, e, jnp.log(w) * (e / jnp.where(w == 1.0, 1.0, w - 1.0)))
    return jnp.maximum(-lam, 0.0) + l1p


def _window(ref, t0, ch, T):
    prev = ref[_rows(jnp.maximum(t0 - SUB, 0), SUB), :].astype(F32)
    nxt = ref[_rows(jnp.minimum(t0 + ch, T - SUB), SUB), :].astype(F32)
    prev = jnp.where(t0 > 0, prev, 0.0)
    nxt = jnp.where(t0 + ch < T, nxt, 0.0)
    return jnp.concatenate([prev, ref[_rows(t0, ch), :].astype(F32), nxt], axis=0)


def _tap(win, shift, ch):
    n = win.shape[0]
    return pltpu.roll(win, (-shift) % n, 0)[SUB:SUB + ch]


def _lru_conv(xb_ref, vec, t0, T):
    win = _window(xb_ref, t0, LRU_CH, T)
    xc = jnp.broadcast_to(vec[4:5, :], (LRU_CH, LANE))
    for j in range(CONV_WIDTH):
        xc = xc + _tap(win, j - CONV_WIDTH // 2, LRU_CH) * vec[j:j + 1, :]
    return xc


def _lru_dir(pre_a, pre_x, sp):
    r = _sigmoid(pre_a)
    i = _sigmoid(pre_x)
    log_a = (-LRU_C) * r * sp
    a = jnp.exp(log_a)
    z = jnp.tanh(-log_a) * (a * a + 1.0)
    s = jnp.sqrt(z)
    return r, i, a, s


def _scan_tile(a, b, reverse, row):
    for k in (1, 2, 4):
        if not reverse:
            a_s, b_s, m = pltpu.roll(a, k, 0), pltpu.roll(b, k, 0), row >= k
        else:
            a_s, b_s, m = pltpu.roll(a, SUB - k, 0), pltpu.roll(b, SUB - k, 0), row < SUB - k
        b = jnp.where(m, a * b_s + b, b)
        a = jnp.where(m, a * a_s, a)
    return a, b


def _bcast_row(x, r):
    return jnp.broadcast_to(x[r:r + 1, :], (SUB, LANE))


def _lru_prepare(xb_ref, w4_ref, vec, xc_ref, af_ref, uf_ref, ab_ref, ub_ref, T):
    sp_f = _softplus_neg(vec[9:10, :])
    sp_b = _softplus_neg(vec[10:11, :])
    w4 = w4_ref[0]

    def body(c, carry):
        t0 = c * LRU_CH
        xc = _lru_conv(xb_ref, vec, t0, T)
        if xc_ref is not None:
            xc_ref[_rows(t0, LRU_CH), :] = xc
        pre = _mm(xc, w4)
        _, i, a, s = _lru_dir(pre[:, 0:128] + vec[5:6, :], pre[:, 128:256] + vec[6:7, :], sp_f)
        af_ref[_rows(t0, LRU_CH), :] = a
        uf_ref[_rows(t0, LRU_CH), :] = s * (i * xc)
        _, i, a, s = _lru_dir(pre[:, 256:384] + vec[7:8, :], pre[:, 384:512] + vec[8:9, :], sp_b)
        ab_ref[_rows(t0, LRU_CH), :] = a
        ub_ref[_rows(t0, LRU_CH), :] = s * (i * xc)
        return carry

    lax.fori_loop(0, T // LRU_CH, body, 0)


def _lru_scan(af_ref, uf_ref, ab_ref, ub_ref, T):
    nt = T // SUB
    row = lax.broadcasted_iota(jnp.int32, (SUB, LANE), 0)

    def body(j, carry):
        hf, hb = carry
        sf = _rows(j * SUB, SUB)
        sb = _rows((nt - 1 - j) * SUB, SUB)
        a, b = _scan_tile(af_ref[sf, :], uf_ref[sf, :], False, row)
        h = a * hf + b
        uf_ref[sf, :] = h
        hf = _bcast_row(h, SUB - 1)
        a, b = _scan_tile(ab_ref[sb, :], ub_ref[sb, :], True, row)
        h = a * hb + b
        ub_ref[sb, :] = h
        hb = _bcast_row(h, 0)
        return hf, hb

    z = jnp.zeros((SUB, LANE), F32)
    lax.fori_loop(0, nt, body, (z, z))


def _lru_fwd_call(proj, vec, w4):
    T = proj.shape[0]

    def body(xb_ref, gate_ref, vec_ref, w4_ref, y_ref, af_ref, uf_ref, ab_ref, ub_ref):
        vec = vec_ref[...]
        _lru_prepare(xb_ref, w4_ref, vec, None, af_ref, uf_ref, ab_ref, ub_ref, T)
        _lru_scan(af_ref, uf_ref, ab_ref, ub_ref, T)

        def out(c, carry):
            rows = _rows(c * LRU_CH, LRU_CH)
            gl, _ = _gelu_parts(gate_ref[rows, :])
            y_ref[rows, :] = (uf_ref[rows, :] + ub_ref[rows, :]) * gl
            return carry

        lax.fori_loop(0, T // LRU_CH, out, 0)

    return pl.pallas_call(
        body, name="lru_fwd", grid=(LRU_WIDTH // LANE,),
        in_specs=[_strip(T, lambda j: j), _strip(T, lambda j: j + 3),
                  pl.BlockSpec((16, LANE), lambda j: (0, j)),
                  pl.BlockSpec((1, LANE, 4 * LANE), lambda j: (j, 0, 0))],
        out_specs=_strip(T, lambda j: j, buffers=1),
        out_shape=jax.ShapeDtypeStruct((T, LRU_WIDTH), F32),
        scratch_shapes=[pltpu.VMEM((T, LANE), F32)] * 4,
        compiler_params=_cparams(("arbitrary",)),
    )(proj, proj, vec, w4)


def _store_strips(stage_ref, dp_ref, cols, sems):
    copies = [pltpu.make_async_copy(stage_ref.at[b], dp_ref.at[:, pl.ds(pl.multiple_of(c * LANE, LANE), LANE)], sems.at[b])
              for b, c in enumerate(cols)]
    for cp in copies:
        cp.start()
    for cp in copies:
        cp.wait()


def _lru_bwd_call(proj, dycat, vec, w4, after):
    T = proj.shape[0]
    nt = T // SUB
    nch = T // LRU_CH

    def body(xb_ref, gate_ref, dy_ref, vec_ref, w4_ref, after_ref, dp_ref, dvec_ref, dw4_ref,
             xc_ref, af_ref, hf_ref, ab_ref, hb_ref, dh_ref, stage_ref, sems):
        dxb_ref, dgate_ref = stage_ref.at[0], stage_ref.at[1]
        vec = vec_ref[...]
        _lru_prepare(xb_ref, w4_ref, vec, xc_ref, af_ref, hf_ref, ab_ref, hb_ref, T)
        _lru_scan(af_ref, hf_ref, ab_ref, hb_ref, T)

        def gate_bwd(c, carry):
            rows = _rows(c * LRU_CH, LRU_CH)
            gl, dgl = _gelu_parts(gate_ref[rows, :])
            dy = dy_ref[rows, :]
            dgate_ref[rows, :] = (dy * (hf_ref[rows, :] + hb_ref[rows, :]) * dgl).astype(dgate_ref.dtype)
            dh_ref[rows, :] = dy * gl
            return carry

        lax.fori_loop(0, nch, gate_bwd, 0)

        row = lax.broadcasted_iota(jnp.int32, (SUB, LANE), 0)

        def adj(j, carry):
            gf, a_next, gb, a_prev = carry
            tf = nt - 1 - j
            sf = _rows(tf * SUB, SUB)
            a_t = af_ref[sf, :]
            h_t = hf_ref[sf, :]
            coef = jnp.where(row == SUB - 1, a_next, pltpu.roll(a_t, SUB - 1, 0))
            ac, bc = _scan_tile(coef, dh_ref[sf, :], True, row)
            g = ac * gf + bc
            h_prev = hf_ref[_rows(jnp.maximum(tf - 1, 0) * SUB, SUB), :]
            h_prev = jnp.where(tf > 0, _bcast_row(h_prev, SUB - 1), 0.0)
            hs = jnp.where(row == 0, h_prev, pltpu.roll(h_t, 1, 0))
            af_ref[sf, :] = g * hs
            hf_ref[sf, :] = g
            gf = _bcast_row(g, 0)
            a_next = _bcast_row(a_t, 0)
            sb = _rows(j * SUB, SUB)
            a_t = ab_ref[sb, :]
            h_t = hb_ref[sb, :]
            coef = jnp.where(row == 0, a_prev, pltpu.roll(a_t, 1, 0))
            ac, bc = _scan_tile(coef, dh_ref[sb, :], False, row)
            g = ac * gb + bc
            h_next = hb_ref[_rows(jnp.minimum(j + 1, nt - 1) * SUB, SUB), :]
            h_next = jnp.where(j < nt - 1, _bcast_row(h_next, 0), 0.0)
            hs = jnp.where(row == SUB - 1, h_next, pltpu.roll(h_t, SUB - 1, 0))
            ab_ref[sb, :] = g * hs
            hb_ref[sb, :] = g
            gb = _bcast_row(g, SUB - 1)
            a_prev = _bcast_row(a_t, SUB - 1)
            return gf, a_next, gb, a_prev

        z = jnp.zeros((SUB, LANE), F32)
        lax.fori_loop(0, nt, adj, (z, z, z, z))

        sp_f = _softplus_neg(vec[9:10, :])
        sp_b = _softplus_neg(vec[10:11, :])
        w4 = w4_ref[0]
        dw4_ref[...] = jnp.zeros_like(dw4_ref)

        def one_dir(pre_a, pre_x, sp, xc, du, da):
            r, i, a, s = _lru_dir(pre_a, pre_x, sp)
            d_i = du * s * xc
            dxc = du * s * i
            d_s = du * i * xc
            d_log = da * a - d_s * (a * a) / s
            d_r = d_log * (-LRU_C) * sp
            d_sp = jnp.sum(d_log * (-LRU_C) * r, axis=0, keepdims=True)
            return d_r * r * (1.0 - r), d_i * i * (1.0 - i), dxc, d_sp

        def gates_bwd(c, carry):
            db, dspf, dspb = carry
            rows = _rows(c * LRU_CH, LRU_CH)
            xc = xc_ref[rows, :]
            pre = _mm(xc, w4)
            dpa_f, dpx_f, dxc_f, d_sp_f = one_dir(pre[:, 0:128] + vec[5:6, :], pre[:, 128:256] + vec[6:7, :],
                                                  sp_f, xc, hf_ref[rows, :], af_ref[rows, :])
            dpa_b, dpx_b, dxc_b, d_sp_b = one_dir(pre[:, 256:384] + vec[7:8, :], pre[:, 384:512] + vec[8:9, :],
                                                  sp_b, xc, hb_ref[rows, :], ab_ref[rows, :])
            dpre = jnp.concatenate([dpa_f, dpx_f, dpa_b, dpx_b], axis=1)
            dw4_ref[0] += _mm_tn(xc, dpre)
            dh_ref[rows, :] = dxc_f + dxc_b + _mm_nt(dpre, w4)
            return db + jnp.sum(dpre, axis=0, keepdims=True), dspf + d_sp_f, dspb + d_sp_b

        z1 = jnp.zeros((1, LANE), F32)
        db, dspf, dspb = lax.fori_loop(0, nch, gates_bwd, (jnp.zeros((1, 4 * LANE), F32), z1, z1))

        def conv_bwd(c, carry):
            t0 = c * LRU_CH
            rows = _rows(t0, LRU_CH)
            dwin = _window(dh_ref, t0, LRU_CH, T)
            xwin = _window(xb_ref, t0, LRU_CH, T)
            dxc = dh_ref[rows, :]
            dxb = jnp.zeros((LRU_CH, LANE), F32)
            out = []
            for j in range(CONV_WIDTH):
                off = j - CONV_WIDTH // 2
                dxb = dxb + _tap(dwin, -off, LRU_CH) * vec[j:j + 1, :]
                out.append(carry[j] + jnp.sum(dxc * _tap(xwin, off, LRU_CH), axis=0, keepdims=True))
            dxb_ref[rows, :] = dxb.astype(dxb_ref.dtype)
            out.append(carry[CONV_WIDTH] + jnp.sum(dxc, axis=0, keepdims=True))
            return tuple(out)

        dconv = lax.fori_loop(0, nch, conv_bwd, (z1,) * (CONV_WIDTH + 1))
        dlam_f = dspf * (-_sigmoid(-vec[9:10, :]))
        dlam_b = dspb * (-_sigmoid(-vec[10:11, :]))
        dvec_ref[...] = jnp.concatenate(
            list(dconv) + [db[:, 0:128], db[:, 128:256], db[:, 256:384], db[:, 384:512], dlam_f, dlam_b,
                           jnp.zeros((5, LANE), F32)], axis=0)
        j = pl.program_id(0)
        _store_strips(stage_ref, dp_ref, (j, j + 3), sems)

    ns = LRU_WIDTH // LANE
    return pl.pallas_call(
        body, name="lru_bwd", grid=(ns,),
        in_specs=[_strip(T, lambda j: j), _strip(T, lambda j: j + 3), _strip(T, lambda j: j),
                  pl.BlockSpec((16, LANE), lambda j: (0, j)),
                  pl.BlockSpec((1, LANE, 4 * LANE), lambda j: (j, 0, 0)),
                  pl.BlockSpec(memory_space=pl.ANY)],
        out_specs=[pl.BlockSpec(memory_space=pl.ANY),
                   pl.BlockSpec((16, LANE), lambda j: (0, j)),
                   pl.BlockSpec((1, LANE, 4 * LANE), lambda j: (j, 0, 0))],
        out_shape=[jax.ShapeDtypeStruct((T, IN_WIDTH), _BF),
                   jax.ShapeDtypeStruct((16, LRU_WIDTH), F32), jax.ShapeDtypeStruct((ns, LANE, 4 * LANE), F32)],
        scratch_shapes=[pltpu.VMEM((T, LANE), F32)] * 6 + [pltpu.VMEM((2, T, LANE), _BF), pltpu.SemaphoreType.DMA((2,))],
        compiler_params=_cparams(("arbitrary",)),
    )(proj, proj, dycat, vec, w4, after)


def _lru_vec(cw, cb, ba, bx, lam):
    return jnp.concatenate([cw, cb[None], ba[0:1], bx[0:1], ba[1:2], bx[1:2], lam, jnp.zeros((5, LRU_WIDTH), F32)], axis=0)


def _lru_w4(wa, wx):
    nl = wa.shape[0]
    w = jnp.stack([wa[:, 0], wx[:, 0], wa[:, 1], wx[:, 1]], axis=1)
    w = w.reshape(nl, 4, 3, 2, 64, 64)
    eye = jnp.eye(2, dtype=w.dtype)
    bd = w[:, :, :, :, :, None, :] * eye[None, None, None, :, None, :, None]
    bd = bd.reshape(nl, 4, 3, LANE, LANE)
    return bd.transpose(0, 2, 3, 1, 4).reshape(nl, 3, LANE, 4 * LANE).astype(_BF)


def _lru_unpack(dvec, dw4):
    def blocks(m):
        m = m.reshape(3, 2, 64, 2, 64)
        return jnp.stack([m[:, 0, :, 0, :], m[:, 1, :, 1, :]], axis=1).reshape(6, 64, 64)
    parts = [blocks(dw4[:, :, k * LANE:(k + 1) * LANE]) for k in range(4)]
    dwa = jnp.stack([parts[0], parts[2]])
    dwx = jnp.stack([parts[1], parts[3]])
    dba = jnp.stack([dvec[5], dvec[7]])
    dbx = jnp.stack([dvec[6], dvec[8]])
    return dvec[0:4], dvec[4], dwa, dba, dwx, dbx, dvec[9:11]


RC = 2 * RET_CHUNK


def _ret_tables(T):
    half = HEAD_DIM // 2
    pos = jnp.arange(T, dtype=F32)
    inv_freq = ROPE_BASE ** (-jnp.arange(half, dtype=F32) / half)
    ang = pos[:, None] * inv_freq[None, :]
    cos = jnp.tile(jnp.cos(ang), (1, 4))
    sin = jnp.tile(jnp.concatenate([-jnp.sin(ang), jnp.sin(ang)], axis=1), (1, 2))
    log_g = jnp.log1p(-jnp.exp2(-5.0 - jnp.arange(RET_HEADS, dtype=F32)))
    idx = jnp.arange(RC, dtype=F32)
    dec = jnp.exp(jnp.abs(idx[:, None] - idx[None, :]) * log_g[:, None, None])
    lg = jnp.repeat(log_g, HEAD_DIM).reshape(3, 1, LANE)
    col = idx[None, :, None]
    rtab = jnp.stack([jnp.exp((RC - 1 - col) * lg), jnp.exp(col * lg),
                      jnp.exp((col + 1.0) * lg), jnp.exp((RC - col) * lg)], axis=1)
    gch = jnp.broadcast_to(jnp.exp(RC * lg), (3, SUB, LANE))
    return cos, sin, dec, rtab, gch


def _swap32(x, lane):
    return jnp.where((lane & 32) == 0, pltpu.roll(x, LANE - 32, 1), pltpu.roll(x, 32, 1))


def _head_mean(x, m0, m1):
    s0 = jnp.sum(x * m0, axis=-1, keepdims=True)
    s1 = jnp.sum(x * m1, axis=-1, keepdims=True)
    return (s0 * m0 + s1 * m1) * (1.0 / HEAD_DIM)


def _ret_masks():
    lane = lax.broadcasted_iota(jnp.int32, (RC, LANE), 1)
    m0 = (lane < HEAD_DIM).astype(F32)
    r = lax.broadcasted_iota(jnp.int32, (LANE, LANE), 0) // HEAD_DIM
    c = lax.broadcasted_iota(jnp.int32, (LANE, LANE), 1) // HEAD_DIM
    return lane, m0, 1.0 - m0, (r == c).astype(F32)


def _ret_specs(T):
    const = lambda shape, imap: pl.BlockSpec(shape, imap)
    return [_strip(T, lambda j: j + 6), _strip(T, lambda j: j + 9), _strip(T, lambda j: j + 12),
            _strip(T, lambda j: j + 15),
            pl.BlockSpec((T, LANE), lambda j: (0, 0), pipeline_mode=pl.Buffered(1)),
            pl.BlockSpec((T, LANE), lambda j: (0, 0), pipeline_mode=pl.Buffered(1)),
            const((2, RC, RC), lambda j: (j, 0, 0)),
            const((1, 4, RC, LANE), lambda j: (j, 0, 0, 0)),
            const((1, SUB, LANE), lambda j: (j, 0, 0)),
            const((SUB, LANE), lambda j: (0, j))]


def _ret_fwd_call(proj, tables, gnw8):
    T = proj.shape[0]
    nc = T // RC
    cos, sin, dec, rtab, gch = tables

    def body(q_ref, k_ref, v_ref, g_ref, cos_ref, sin_ref, dec_ref, rtab_ref, gch_ref, gnw_ref, y_ref, stf_ref):
        lane, m0, m1, bd = _ret_masks()
        gch_v = gch_ref[0][0:1, :]
        gnw = gnw_ref[0:1, :]
        dkf, dkb, dqf, dqb = rtab_ref[0, 0], rtab_ref[0, 1], rtab_ref[0, 2], rtab_ref[0, 3]

        def rope(x, rows):
            return x * cos_ref[rows, :] + _swap32(x, lane) * sin_ref[rows, :]

        def pass_a(n, st):
            rows = _rows(n * RC, RC)
            stf_ref[n] = st
            kr = rope(k_ref[rows, :], rows) * (HEAD_DIM ** -0.5)
            return gch_v * st + _mm_tn(kr * dkf, v_ref[rows, :]) * bd

        _loop2(nc, pass_a, jnp.zeros((LANE, LANE), F32))

        def pass_b(i, stb):
            n = nc - 1 - i
            rows = _rows(n * RC, RC)
            qr = rope(q_ref[rows, :], rows)
            kr = rope(k_ref[rows, :], rows) * (HEAD_DIM ** -0.5)
            v = v_ref[rows, :]
            o = _mm(qr * dqf, stf_ref[n]) + _mm(qr * dqb, stb)
            for h, m in ((0, m0), (1, m1)):
                s = _mm_nt(qr * m, kr) * dec_ref[h]
                o = o + _mm(s, v * m)
            oc = o - _head_mean(o, m0, m1)
            on = oc * lax.rsqrt(_head_mean(oc * oc, m0, m1) + GN_EPS)
            g = g_ref[rows, :]
            y_ref[rows, :] = (g * _sigmoid(g)) * (on * gnw)
            return gch_v * stb + _mm_tn(kr * dkb, v) * bd

        _loop2(nc, pass_b, jnp.zeros((LANE, LANE), F32))

    return pl.pallas_call(
        body, name="ret_fwd", grid=(RET_WIDTH // LANE,),
        in_specs=_ret_specs(T),
        out_specs=_strip(T, lambda j: j, buffers=1),
        out_shape=jax.ShapeDtypeStruct((T, RET_WIDTH), F32),
        scratch_shapes=[pltpu.VMEM((nc, LANE, LANE), F32)],
        compiler_params=_cparams(("arbitrary",)),
    )(proj, proj, proj, proj, cos, sin, dec, rtab, gch, gnw8)


def _ret_bwd_call(proj, dycat, tables, gnw8, dp):
    T = proj.shape[0]
    nc = T // RC
    cos, sin, dec, rtab, gch = tables

    def body(q_ref, k_ref, v_ref, g_ref, cos_ref, sin_ref, dec_ref, rtab_ref, gch_ref, gnw_ref, dy_ref, dp_in_ref,
             dp_out_ref, dgnw_ref, stf_ref, dstb_ref, dkr_ref, dv_ref, dp_ref, sems):
        lane, m0, m1, bd = _ret_masks()
        gch_v = gch_ref[0][0:1, :]
        gnw = gnw_ref[0:1, :]
        dkf, dkb, dqf, dqb = rtab_ref[0, 0], rtab_ref[0, 1], rtab_ref[0, 2], rtab_ref[0, 3]
        scale = HEAD_DIM ** -0.5
        zst = jnp.zeros((LANE, LANE), F32)

        def rope(x, rows):
            return x * cos_ref[rows, :] + _swap32(x, lane) * sin_ref[rows, :]

        def rope_t(d, rows):
            return d * cos_ref[rows, :] + _swap32(d * sin_ref[rows, :], lane)

        def pass_a(n, st):
            rows = _rows(n * RC, RC)
            stf_ref[n] = st
            kr = rope(k_ref[rows, :], rows) * scale
            return gch_v * st + _mm_tn(kr * dkf, v_ref[rows, :]) * bd

        _loop2(nc, pass_a, zst)

        def pass_b(i, carry):
            stb, d_f, dgnw = carry
            n = nc - 1 - i
            rows = _rows(n * RC, RC)
            qr = rope(q_ref[rows, :], rows)
            kr = rope(k_ref[rows, :], rows) * scale
            v = v_ref[rows, :]
            stf = stf_ref[n]
            qf = qr * dqf
            qb = qr * dqb
            o = _mm(qf, stf) + _mm(qb, stb)
            s_h = []
            for h, m in ((0, m0), (1, m1)):
                s = _mm_nt(qr * m, kr) * dec_ref[h]
                s_h.append(s)
                o = o + _mm(s, v * m)
            oc = o - _head_mean(o, m0, m1)
            rstd = lax.rsqrt(_head_mean(oc * oc, m0, m1) + GN_EPS)
            on = oc * rstd
            g = g_ref[rows, :]
            sg = _sigmoid(g)
            dy = dy_ref[rows, :]
            dp_ref[3, rows, :] = (dy * (on * gnw) * (sg * (1.0 + g * (1.0 - sg)))).astype(dp_ref.dtype)
            t = dy * (g * sg)
            dgnw = dgnw + jnp.sum(t * on, axis=0, keepdims=True)
            don = t * gnw
            do = rstd * (don - _head_mean(don, m0, m1) - on * _head_mean(don * on, m0, m1))
            dqr = _mm_nt(do, stf) * dqf + _mm_nt(do, stb) * dqb
            dkr = _mm_nt(v, d_f) * dkf
            dv = _mm(kr * dkf, d_f)
            for h, m in ((0, m0), (1, m1)):
                ds = _mm_nt(do * m, v) * dec_ref[h]
                dqr = dqr + _mm(ds, kr * m)
                dkr = dkr + _mm_tn(ds, qr * m)
                dv = dv + _mm_tn(s_h[h], do * m)
            dp_ref[0, rows, :] = rope_t(dqr, rows).astype(dp_ref.dtype)
            dkr_ref[rows, :] = dkr
            dv_ref[rows, :] = dv
            dstb_ref[n] = _mm_tn(qb, do) * bd
            d_f = _mm_tn(qf, do) * bd + gch_v * d_f
            stb = gch_v * stb + _mm_tn(kr * dkb, v) * bd
            return stb, d_f, dgnw

        _, _, dgnw = _loop2(nc, pass_b, (zst, zst, jnp.zeros((1, LANE), F32)))
        dgnw_ref[...] = jnp.concatenate([dgnw, jnp.zeros((SUB - 1, LANE), F32)], axis=0)

        def pass_c(n, d_b):
            rows = _rows(n * RC, RC)
            kr = rope(k_ref[rows, :], rows) * scale
            v = v_ref[rows, :]
            dkr = dkr_ref[rows, :] + _mm_nt(v, d_b) * dkb
            dp_ref[1, rows, :] = (rope_t(dkr, rows) * scale).astype(dp_ref.dtype)
            dp_ref[2, rows, :] = (dv_ref[rows, :] + _mm(kr * dkb, d_b)).astype(dp_ref.dtype)
            return dstb_ref[n] + gch_v * d_b

        _loop2(nc, pass_c, zst)
        j = pl.program_id(0)
        _store_strips(dp_ref, dp_out_ref, (j + 6, j + 9, j + 12, j + 15), sems)

    n_in = len(_ret_specs(T)) + 1
    return pl.pallas_call(
        body, name="ret_bwd", grid=(RET_WIDTH // LANE,),
        in_specs=_ret_specs(T) + [_strip(T, lambda j: j + 3), pl.BlockSpec(memory_space=pl.ANY)],
        out_specs=[pl.BlockSpec(memory_space=pl.ANY), pl.BlockSpec((SUB, LANE), lambda j: (0, j))],
        out_shape=[jax.ShapeDtypeStruct(dp.shape, dp.dtype), jax.ShapeDtypeStruct((SUB, RET_WIDTH), F32)],
        scratch_shapes=[pltpu.VMEM((nc, LANE, LANE), F32), pltpu.VMEM((nc, LANE, LANE), F32),
                        pltpu.VMEM((T, LANE), F32), pltpu.VMEM((T, LANE), F32),
                        pltpu.VMEM((4, T, LANE), _BF), pltpu.SemaphoreType.DMA((4,))],
        input_output_aliases={n_in: 0},
        compiler_params=_cparams(("arbitrary",)),
    )(proj, proj, proj, proj, cos, sin, dec, rtab, gch, gnw8, dycat, dp)


NA_Q = 2 * GRID_W
NA_WROWS = 10
NA_K = NA_WROWS * GRID_W
NA_CHUNKS = NA_K // LANE
NA_TYPES = 5
_ONEHOT_PRECISION = lax.Precision.HIGH


def _na_onehots(rows_n):
    reps = [(0, 0), (2, 0), (4, 0), (rows_n - 4, rows_n - NA_WROWS), (rows_n - 2, rows_n - NA_WROWS)]
    rm = np.zeros((NA_TYPES, 2, NA_WROWS, 2 * NA_KH - 1), np.float32)
    for t, (r, ws) in enumerate(reps):
        for qh in range(2):
            qrow = r + qh
            rstart = min(max(qrow - NA_KH // 2, 0), rows_n - NA_KH)
            for kh in range(NA_WROWS):
                krow = ws + kh
                if rstart <= krow < rstart + NA_KH:
                    rm[t, qh, kh, krow - qrow + NA_KH - 1] = 1.0
    cm = np.zeros((GRID_W, GRID_W, 2 * NA_KW - 1), np.float32)
    for qc in range(GRID_W):
        cstart = min(max(qc - NA_KW // 2, 0), GRID_W - NA_KW)
        for kc in range(cstart, cstart + NA_KW):
            cm[qc, kc, kc - qc + NA_KW - 1] = 1.0
    rm2 = rm.reshape(NA_TYPES, 2, NA_CHUNKS, 2, 2 * NA_KH - 1)
    cm2 = np.zeros((GRID_W, LANE, 2, 2 * NA_KW - 1), np.float32)
    for z in range(2):
        cm2[:, z * GRID_W:(z + 1) * GRID_W, z, :] = cm
    return rm2, cm2


def _na_bias_tables(rpb, rows_n):
    rm, cm = _na_onehots(rows_n)
    val = jnp.einsum("hab,tqpza,xkzb->htpqxk", rpb, rm, cm, precision=_ONEHOT_PRECISION)
    valid = np.einsum("tqpz,xkz->tpqxk", rm.sum(-1), cm.sum(-1)) > 0.5
    return jnp.where(valid[None], val, NEG).reshape(2, 2, NA_TYPES, NA_CHUNKS, NA_Q, LANE)


def _na_bias_grad(dtab, rows_n):
    rm, cm = _na_onehots(rows_n)
    d6 = dtab.reshape(NA_HEADS, NA_TYPES, NA_CHUNKS, 2, GRID_W, LANE)
    return jnp.einsum("htpqxk,tqpza,xkzb->hab", d6, rm, cm, precision=_ONEHOT_PRECISION)


def _na_bias(b_ref, h, typ):
    return jnp.concatenate([b_ref[0, h, typ, c] for c in range(NA_CHUNKS)], axis=1)


def _na_step(p, npairs, rows_n):
    ws = jnp.clip(2 * p - NA_KH // 2, 0, rows_n - NA_WROWS)
    koff = pl.multiple_of(ws * GRID_W, LANE)
    typ = jnp.where(p == 0, 0, jnp.where(p == 1, 1, jnp.where(p == npairs - 2, 3, jnp.where(p == npairs - 1, 4, 2))))
    return _rows(p * NA_Q, NA_Q), pl.ds(koff, NA_K), typ


def _na_fwd_call(proj, btab):
    T = proj.shape[0]
    npairs, rows_n = T // NA_Q, T // GRID_W

    def body(q_ref, k_ref, v_ref, b_ref, o_ref):
        lane = lax.broadcasted_iota(jnp.int32, (NA_Q, LANE), 1)
        m0 = (lane < HEAD_DIM).astype(F32)
        m1 = 1.0 - m0

        def step(p, carry):
            qrows, krows, typ = _na_step(p, npairs, rows_n)
            q = q_ref[qrows, :]
            kw = k_ref[krows, :]
            vw = v_ref[krows, :]
            o = jnp.zeros((NA_Q, LANE), F32)
            for h, m in ((0, m0), (1, m1)):
                s = _mm_nt(q * m, kw) * (HEAD_DIM ** -0.5) + _na_bias(b_ref, h, typ)
                e = jnp.exp(s - jnp.max(s, axis=-1, keepdims=True))
                pr = e / jnp.sum(e, axis=-1, keepdims=True)
                o = o + _mm(pr, vw) * m
            o_ref[qrows, :] = o
            return carry

        _loop2(npairs, step, 0)

    return pl.pallas_call(
        body, name="na_fwd", grid=(NA_WIDTH // LANE,),
        in_specs=[_strip(T, lambda j: j + 18), _strip(T, lambda j: j + 20), _strip(T, lambda j: j + 22),
                  pl.BlockSpec((1, 2, NA_TYPES, NA_CHUNKS, NA_Q, LANE), lambda j: (j, 0, 0, 0, 0, 0))],
        out_specs=_strip(T, lambda j: j, buffers=1),
        out_shape=jax.ShapeDtypeStruct((T, NA_WIDTH), F32),
        compiler_params=_cparams(("arbitrary",)),
    )(proj, proj, proj, btab)


def _na_bwd_call(proj, dycat, btab, dp):
    T = proj.shape[0]
    npairs, rows_n = T // NA_Q, T // GRID_W
    scale = HEAD_DIM ** -0.5

    def body(q_ref, k_ref, v_ref, do_ref, b_ref, dp_in_ref, dp_out_ref, db_ref, dka_ref, dva_ref, stage_ref, sems):
        dq_ref = stage_ref.at[0]
        lane = lax.broadcasted_iota(jnp.int32, (NA_Q, LANE), 1)
        m0 = (lane < HEAD_DIM).astype(F32)
        m1 = 1.0 - m0
        dka_ref[...] = jnp.zeros_like(dka_ref)
        dva_ref[...] = jnp.zeros_like(dva_ref)
        db_ref[...] = jnp.zeros_like(db_ref)

        def step(p, carry):
            qrows, krows, typ = _na_step(p, npairs, rows_n)
            q = q_ref[qrows, :]
            do = do_ref[qrows, :]
            kw = k_ref[krows, :]
            vw = v_ref[krows, :]
            dq = jnp.zeros((NA_Q, LANE), F32)
            dk = jnp.zeros((NA_K, LANE), F32)
            dv = jnp.zeros((NA_K, LANE), F32)
            for h, m in ((0, m0), (1, m1)):
                qm = q * m
                dom = do * m
                s = _mm_nt(qm, kw) * scale + _na_bias(b_ref, h, typ)
                e = jnp.exp(s - jnp.max(s, axis=-1, keepdims=True))
                pr = e / jnp.sum(e, axis=-1, keepdims=True)
                dpr = _mm_nt(dom, vw)
                ds = pr * (dpr - jnp.sum(pr * dpr, axis=-1, keepdims=True))
                for c in range(NA_CHUNKS):
                    db_ref[0, h, typ, c] += ds[:, c * LANE:(c + 1) * LANE]
                dsb = (ds * scale).astype(_BF)
                dq = dq + _mm(dsb, kw) * m
                dk = dk + _mm_tn(dsb, qm)
                dv = dv + _mm_tn(pr, dom)
            dq_ref[qrows, :] = dq.astype(dq_ref.dtype)
            dka_ref[krows, :] += dk
            dva_ref[krows, :] += dv
            return carry

        _loop2(npairs, step, 0)
        stage_ref[1] = dka_ref[...].astype(stage_ref.dtype)
        stage_ref[2] = dva_ref[...].astype(stage_ref.dtype)
        j = pl.program_id(0)
        _store_strips(stage_ref, dp_out_ref, (j + 18, j + 20, j + 22), sems)

    tab = pl.BlockSpec((1, 2, NA_TYPES, NA_CHUNKS, NA_Q, LANE), lambda j: (j, 0, 0, 0, 0, 0))
    return pl.pallas_call(
        body, name="na_bwd", grid=(NA_WIDTH // LANE,),
        in_specs=[_strip(T, lambda j: j + 18), _strip(T, lambda j: j + 20), _strip(T, lambda j: j + 22),
                  _strip(T, lambda j: j + 6), tab, pl.BlockSpec(memory_space=pl.ANY)],
        out_specs=[pl.BlockSpec(memory_space=pl.ANY), tab],
        out_shape=[jax.ShapeDtypeStruct(dp.shape, dp.dtype),
                   jax.ShapeDtypeStruct((2, 2, NA_TYPES, NA_CHUNKS, NA_Q, LANE), F32)],
        scratch_shapes=[pltpu.VMEM((T, LANE), F32), pltpu.VMEM((T, LANE), F32),
                        pltpu.VMEM((3, T, LANE), _BF), pltpu.SemaphoreType.DMA((3,))],
        input_output_aliases={5: 0},
        compiler_params=_cparams(("arbitrary",)),
    )(proj, proj, proj, dycat, btab, dp)


W_BLK = IN_WIDTH // N_DEV
N_BLK = 768
N_STEPS = IN_WIDTH // N_BLK
TM = 512


def _ln_fwd(z, g, b):
    zc = z - jnp.mean(z, axis=-1, keepdims=True)
    var = jnp.mean(zc * zc, axis=-1, keepdims=True)
    return zc * lax.rsqrt(var + LN_EPS) * g + b


def _ln_bwd(dy, z, g):
    zc = z - jnp.mean(z, axis=-1, keepdims=True)
    rstd = lax.rsqrt(jnp.mean(zc * zc, axis=-1, keepdims=True) + LN_EPS)
    xhat = zc * rstd
    dxh = dy * g
    dz = rstd * (dxh - jnp.mean(dxh, axis=-1, keepdims=True) - xhat * jnp.mean(dxh * xhat, axis=-1, keepdims=True))
    return dz, dy * xhat


def _row_tile(T):
    return 1024 if T % 1024 == 0 else TM


def _halves(n):
    return (pl.ds(0, n // 2), pl.ds(n // 2, n // 2))


def _inproj_call(xb, w, after):
    T = xb.shape[0]
    tm = _row_tile(T)

    def body(x_ref, w_ref, after_ref, o_ref):
        o_ref[...] = _mm(x_ref[...], w_ref[...])

    return pl.pallas_call(
        body, name="inproj", grid=(T // tm, N_STEPS),
        in_specs=[pl.BlockSpec((tm, D_MODEL), lambda i, n: (i, 0)),
                  pl.BlockSpec((D_MODEL, N_BLK), lambda i, n: (0, n)),
                  pl.BlockSpec(memory_space=pl.ANY)],
        out_specs=pl.BlockSpec((tm, N_BLK), lambda i, n: (i, n)),
        out_shape=jax.ShapeDtypeStruct((T, IN_WIDTH), F32),
        compiler_params=_cparams(("parallel", "arbitrary")),
    )(xb, w, after)


def _vec_spec():
    return pl.BlockSpec((1, D_MODEL), lambda *_: (0, 0))


def _outproj_ln_call(y_lru, y_ret, y_na, x, w, g, b, after):
    T = x.shape[0]

    def body(yl_ref, yr_ref, yn_ref, x_ref, w_ref, g_ref, b_ref, after_ref, z_ref, x1_ref, x1b_ref, yc_ref):
        yc_ref[:, 0:LRU_WIDTH] = yl_ref[...].astype(yc_ref.dtype)
        yc_ref[:, LRU_WIDTH:LRU_WIDTH + RET_WIDTH] = yr_ref[...].astype(yc_ref.dtype)
        yc_ref[:, LRU_WIDTH + RET_WIDTH:] = yn_ref[...].astype(yc_ref.dtype)
        z = ALPHA * x_ref[...] + _mm(yc_ref[...], w_ref[...])
        z_ref[...] = z
        x1 = _ln_fwd(z, g_ref[...], b_ref[...])
        x1_ref[...] = x1
        x1b_ref[...] = x1.astype(x1b_ref.dtype)

    row = lambda w_: pl.BlockSpec((TM, w_), lambda i: (i, 0))
    return pl.pallas_call(
        body, name="outproj_ln", grid=(T // TM,),
        in_specs=[row(LRU_WIDTH), row(RET_WIDTH), row(NA_WIDTH), row(D_MODEL),
                  pl.BlockSpec((D_MODEL, D_MODEL), lambda i: (0, 0)), _vec_spec(), _vec_spec(),
                  pl.BlockSpec(memory_space=pl.ANY)],
        out_specs=[row(D_MODEL)] * 4,
        out_shape=[jax.ShapeDtypeStruct((T, D_MODEL), F32), jax.ShapeDtypeStruct((T, D_MODEL), F32),
                   jax.ShapeDtypeStruct((T, D_MODEL), _BF), jax.ShapeDtypeStruct((T, D_MODEL), _BF)],
        compiler_params=_cparams(("parallel",)),
    )(y_lru, y_ret, y_na, x, w, g, b, after)


def _ffn_ln_call(x1, x1b, wg, wu, wd, g, b):
    T = x1.shape[0]

    def body(x_ref, xb_ref, wg_ref, wu_ref, wd_ref, g_ref, b_ref, z_ref, x2_ref, x2b_ref, gp_ref, up_ref, acc_ref):
        n = pl.program_id(1)

        @pl.when(n == 0)
        def _():
            acc_ref[...] = jnp.zeros_like(acc_ref)

        for rows in _halves(TM):
            xb = xb_ref[rows, :]
            gp = _mm(xb, wg_ref[...])
            up = _mm(xb, wu_ref[...])
            gp_ref[rows, :] = gp.astype(gp_ref.dtype)
            up_ref[rows, :] = up.astype(up_ref.dtype)
            acc_ref[rows, :] += _mm(gp * _sigmoid(gp) * up, wd_ref[...])

        @pl.when(n == N_STEPS - 1)
        def _():
            z = ALPHA * x_ref[...] + acc_ref[...]
            z_ref[...] = z
            x2 = _ln_fwd(z, g_ref[...], b_ref[...])
            x2_ref[...] = x2
            x2b_ref[...] = x2.astype(x2b_ref.dtype)

    row = pl.BlockSpec((TM, D_MODEL), lambda i, n: (i, 0))
    return pl.pallas_call(
        body, name="ffn_ln", grid=(T // TM, N_STEPS),
        in_specs=[row, row,
                  pl.BlockSpec((D_MODEL, N_BLK), lambda i, n: (0, n)),
                  pl.BlockSpec((D_MODEL, N_BLK), lambda i, n: (0, n)),
                  pl.BlockSpec((N_BLK, D_MODEL), lambda i, n: (n, 0)), _vec_spec(), _vec_spec()],
        out_specs=[row] * 3 + [pl.BlockSpec((TM, N_BLK), lambda i, n: (i, n))] * 2,
        out_shape=[jax.ShapeDtypeStruct((T, D_MODEL), F32), jax.ShapeDtypeStruct((T, D_MODEL), F32),
                   jax.ShapeDtypeStruct((T, D_MODEL), _BF),
                   jax.ShapeDtypeStruct((T, IN_WIDTH), _BF), jax.ShapeDtypeStruct((T, IN_WIDTH), _BF)],
        scratch_shapes=[pltpu.VMEM((TM, D_MODEL), F32)],
        compiler_params=_cparams(("parallel", "arbitrary")),
    )(x1, x1b, wg, wu, wd, g, b)


def _loss_call(y, t):
    T = y.shape[0]

    def body(y_ref, t_ref, dy_ref, loss_ref):
        @pl.when(pl.program_id(0) == 0)
        def _():
            loss_ref[...] = jnp.zeros_like(loss_ref)

        err = y_ref[...] - t_ref[...]
        dy_ref[...] = err * (1.0 / D_MODEL)
        part = 0.5 * jnp.sum(jnp.mean(err * err, axis=-1, keepdims=True), axis=0, keepdims=True)
        loss_ref[...] += jnp.broadcast_to(part, loss_ref.shape)

    row = pl.BlockSpec((TM, D_MODEL), lambda i: (i, 0))
    return pl.pallas_call(
        body, name="loss", grid=(T // TM,),
        in_specs=[row, row],
        out_specs=[row, pl.BlockSpec((SUB, LANE), lambda i: (0, 0))],
        out_shape=[jax.ShapeDtypeStruct((T, D_MODEL), F32), jax.ShapeDtypeStruct((SUB, LANE), F32)],
        compiler_params=_cparams(("arbitrary",)),
    )(y, t)


def _ffn_bwd_call(dx2, z2, gpb, upb, wg, wu, wd, g, after):
    T = dx2.shape[0]

    def body(dx2_ref, z_ref, gp_ref, up_ref, wg_ref, wu_ref, wd_ref, g_ref, after_ref,
             dx1_ref, dgp_ref, dup_ref, hid_ref, dzb_ref, dln_ref, acc_ref):
        i, n = pl.program_id(0), pl.program_id(1)

        @pl.when((i == 0) & (n == 0))
        def _():
            dln_ref[...] = jnp.zeros_like(dln_ref)

        @pl.when(n == 0)
        def _():
            dy = dx2_ref[...]
            dz, dg_rows = _ln_bwd(dy, z_ref[...], g_ref[...])
            dzb_ref[...] = dz.astype(dzb_ref.dtype)
            acc_ref[...] = ALPHA * dz
            dln_ref[0:1, :] += jnp.sum(dg_rows, axis=0, keepdims=True)
            dln_ref[1:2, :] += jnp.sum(dy, axis=0, keepdims=True)

        for rows in _halves(TM):
            gp = gp_ref[rows, :].astype(F32)
            up = up_ref[rows, :].astype(F32)
            sg = _sigmoid(gp)
            act = gp * sg
            hid_ref[rows, :] = (act * up).astype(hid_ref.dtype)
            dhid = _mm_nt(dzb_ref[rows, :], wd_ref[...])
            dup = dhid * act
            dgp = dhid * up * (sg * (1.0 + gp * (1.0 - sg)))
            dgp_ref[rows, :] = dgp.astype(dgp_ref.dtype)
            dup_ref[rows, :] = dup.astype(dup_ref.dtype)
            acc_ref[rows, :] += _mm_nt(dgp, wg_ref[...]) + _mm_nt(dup, wu_ref[...])

        @pl.when(n == N_STEPS - 1)
        def _():
            dx1_ref[...] = acc_ref[...]

    row = pl.BlockSpec((TM, D_MODEL), lambda i, n: (i, 0))
    blk = pl.BlockSpec((TM, N_BLK), lambda i, n: (i, n))
    return pl.pallas_call(
        body, name="ffn_bwd", grid=(T // TM, N_STEPS),
        in_specs=[row, row, blk, blk,
                  pl.BlockSpec((D_MODEL, N_BLK), lambda i, n: (0, n)),
                  pl.BlockSpec((D_MODEL, N_BLK), lambda i, n: (0, n)),
                  pl.BlockSpec((N_BLK, D_MODEL), lambda i, n: (n, 0)), _vec_spec(),
                  pl.BlockSpec(memory_space=pl.ANY)],
        out_specs=[row, blk, blk, blk, row, pl.BlockSpec((SUB, D_MODEL), lambda i, n: (0, 0))],
        out_shape=[jax.ShapeDtypeStruct((T, D_MODEL), F32),
                   jax.ShapeDtypeStruct((T, IN_WIDTH), _BF), jax.ShapeDtypeStruct((T, IN_WIDTH), _BF),
                   jax.ShapeDtypeStruct((T, IN_WIDTH), _BF), jax.ShapeDtypeStruct((T, D_MODEL), _BF),
                   jax.ShapeDtypeStruct((SUB, D_MODEL), F32)],
        scratch_shapes=[pltpu.VMEM((TM, D_MODEL), F32)],
        compiler_params=_cparams(("arbitrary", "arbitrary")),
    )(dx2, z2, gpb, upb, wg, wu, wd, g, after)


def _outproj_bwd_call(dx1, z1, w, g):
    T = dx1.shape[0]

    def body(dx_ref, z_ref, w_ref, g_ref, dzb_ref, dyc_ref, dres_ref, dln_ref):
        @pl.when(pl.program_id(0) == 0)
        def _():
            dln_ref[...] = jnp.zeros_like(dln_ref)

        dy = dx_ref[...]
        dz, dg_rows = _ln_bwd(dy, z_ref[...], g_ref[...])
        dzb_ref[...] = dz.astype(dzb_ref.dtype)
        dres_ref[...] = ALPHA * dz
        dyc_ref[...] = _mm_nt(dz, w_ref[...])
        dln_ref[0:1, :] += jnp.sum(dg_rows, axis=0, keepdims=True)
        dln_ref[1:2, :] += jnp.sum(dy, axis=0, keepdims=True)

    row = pl.BlockSpec((TM, D_MODEL), lambda i: (i, 0))
    return pl.pallas_call(
        body, name="outproj_bwd", grid=(T // TM,),
        in_specs=[row, row, pl.BlockSpec((D_MODEL, D_MODEL), lambda i: (0, 0)), _vec_spec()],
        out_specs=[row, row, row, pl.BlockSpec((SUB, D_MODEL), lambda i: (0, 0))],
        out_shape=[jax.ShapeDtypeStruct((T, D_MODEL), _BF), jax.ShapeDtypeStruct((T, D_MODEL), F32),
                   jax.ShapeDtypeStruct((T, D_MODEL), F32), jax.ShapeDtypeStruct((SUB, D_MODEL), F32)],
        compiler_params=_cparams(("arbitrary",)),
    )(dx1, z1, w, g)


def _inproj_bwd_call(dres, dp, w):
    T = dres.shape[0]

    def body(dres_ref, dp_ref, w_ref, dx_ref):
        dx_ref[...] = dres_ref[...] + _mm_nt(dp_ref[...], w_ref[...])

    row = pl.BlockSpec((TM, D_MODEL), lambda i: (i, 0))
    return pl.pallas_call(
        body, name="inproj_bwd", grid=(T // TM,),
        in_specs=[row, pl.BlockSpec((TM, IN_WIDTH), lambda i: (i, 0)),
                  pl.BlockSpec((D_MODEL, IN_WIDTH), lambda i: (0, 0), pipeline_mode=pl.Buffered(1))],
        out_specs=row,
        out_shape=jax.ShapeDtypeStruct((T, D_MODEL), F32),
        compiler_params=_cparams(("parallel",)),
    )(dres, dp, w)


def _tn_cols_call(a, b, name):
    T, ka = a.shape
    n = b.shape[1]

    def body(a_ref, b_ref, o_ref):
        o_ref[...] = _mm_tn(a_ref[...], b_ref[...]).astype(o_ref.dtype)

    return pl.pallas_call(
        body, name=name, grid=(n // N_BLK,),
        in_specs=[pl.BlockSpec((T, ka), lambda j: (0, 0), pipeline_mode=pl.Buffered(1)),
                  pl.BlockSpec((T, N_BLK), lambda j: (0, j))],
        out_specs=pl.BlockSpec((ka, N_BLK), lambda j: (0, j)),
        out_shape=jax.ShapeDtypeStruct((ka, n), _BF),
        compiler_params=_cparams(("parallel",)),
    )(a, b)


def _tn_rows_call(a, b, kb, name):
    T, ka = a.shape
    n = b.shape[1]

    def body(a_ref, b_ref, o_ref):
        o_ref[...] = _mm_tn(a_ref[...], b_ref[...]).astype(o_ref.dtype)

    return pl.pallas_call(
        body, name=name, grid=(ka // kb,),
        in_specs=[pl.BlockSpec((T, kb), lambda r: (0, r)),
                  pl.BlockSpec((T, n), lambda r: (0, 0), pipeline_mode=pl.Buffered(1))],
        out_specs=pl.BlockSpec((kb, n), lambda r: (r, 0)),
        out_shape=jax.ShapeDtypeStruct((ka, n), _BF),
        compiler_params=_cparams(("parallel",)),
    )(a, b)


def _me():
    return lax.axis_index("x"), lax.axis_index("y"), lax.axis_index("c")


def _flip(k):
    x, y, c = _me()
    return (1 - x if k & 4 else x, 1 - y if k & 2 else y, 1 - c if k & 1 else c)


def _dev_index(pos):
    return 4 * pos[0] + 2 * pos[1] + pos[2]


_HBM = pl.BlockSpec(memory_space=pltpu.HBM)
_SEM = pl.BlockSpec(memory_space=pltpu.SEMAPHORE)


def _land_shape(shape, mode):
    if mode == "all":
        return (N_DEV,) + shape
    if mode == "cols":
        return (shape[0], N_DEV * shape[1])
    if mode == "blk":
        return shape
    assert mode == "scols"
    return (N_DEV, shape[0], shape[1] // N_DEV)


def _comm_copies(ins, lands, modes, send_sems, recv_sems):
    me = _dev_index(_me())
    copies = []
    for k in range(N_DEV):
        peer = _flip(k)
        pidx = _dev_index(peer)
        for a, (src, land, mode) in enumerate(zip(ins, lands, modes)):
            if mode == "blk":
                src = src.at[pidx]
            elif mode == "scols":
                w = src.shape[1] // N_DEV
                src = src.at[:, pl.ds(pl.multiple_of(pidx * w, LANE), w)]
            if mode == "cols":
                w = src.shape[1]
                dst = land.at[:, pl.ds(pl.multiple_of(me * w, LANE), w)]
            else:
                dst = land.at[me]
            copies.append(pltpu.make_async_remote_copy(
                src_ref=src, dst_ref=dst, send_sem=send_sems.at[k * len(ins) + a], recv_sem=recv_sems.at[k * len(ins) + a],
                device_id=peer, device_id_type=MESH))
    return copies


def _comm_start_call(arrs, gather_flags, after, name):
    n = len(arrs)
    lands = [lax.empty(_land_shape(v.shape, mode), v.dtype) for v, mode in zip(arrs, gather_flags)]

    def body(*refs):
        ins, lnd = refs[:n], refs[n:2 * n]
        send_sems, recv_sems = refs[2 * n + len(after)], refs[2 * n + len(after) + 1]
        for cp in _comm_copies(ins, lnd, gather_flags, send_sems, recv_sems):
            cp.start()
        refs[-1][...] = jnp.zeros_like(refs[-1])

    hbm = [pltpu.with_memory_space_constraint(v, pltpu.HBM) for v in list(arrs) + lands]
    out = pl.pallas_call(
        body, name=name,
        out_shape=(pltpu.SemaphoreType.DMA((N_DEV * n,)), pltpu.SemaphoreType.DMA((N_DEV * n,)),
                   *[pltpu.HBM(v.shape, v.dtype) for v in hbm], jax.ShapeDtypeStruct((SUB, LANE), F32)),
        in_specs=[_HBM] * (2 * n) + [pl.BlockSpec(memory_space=pl.ANY)] * len(after),
        out_specs=(_SEM, _SEM, *[_HBM] * (2 * n), pl.BlockSpec(memory_space=pltpu.VMEM)),
        input_output_aliases={i: 2 + i for i in range(2 * n)},
        compiler_params=pltpu.CompilerParams(has_side_effects=pltpu.SideEffectType.DATAFLOW_SIDE_EFFECTING),
    )(*hbm, *after)
    return out[:-1], out[-1]


def _comm_wait_call(state, gather_flags, after, name):
    n = len(gather_flags)
    send_sems, recv_sems, thru = state[0], state[1], state[2:]

    def body(*refs):
        ins, lnd, ssem, rsem = refs[:n], refs[n:2 * n], refs[2 * n], refs[2 * n + 1]
        for cp in _comm_copies(ins, lnd, gather_flags, ssem, rsem):
            cp.wait_send()
            cp.wait_recv()

    out = pl.pallas_call(
        body, name=name,
        out_shape=tuple(pltpu.HBM(v.shape, v.dtype) for v in thru),
        in_specs=[_HBM] * (2 * n) + [_SEM, _SEM] + [pl.BlockSpec(memory_space=pl.ANY)] * len(after),
        out_specs=tuple([_HBM] * (2 * n)),
        input_output_aliases={i: i for i in range(2 * n)},
        compiler_params=pltpu.CompilerParams(has_side_effects=pltpu.SideEffectType.DATAFLOW_SIDE_EFFECTING),
    )(*thru, send_sems, recv_sems, *after)
    return out[n:]


def _sum8_call(recv, rows, name):
    _, r, c = recv.shape

    def body(x_ref, o_ref):
        acc = x_ref[0].astype(F32)
        for s in range(1, N_DEV):
            acc = acc + x_ref[s].astype(F32)
        o_ref[...] = acc

    return pl.pallas_call(
        body, name=name, grid=(r // rows,),
        in_specs=[pl.BlockSpec((N_DEV, rows, c), lambda i: (0, i, 0))],
        out_specs=pl.BlockSpec((rows, c), lambda i: (i, 0)),
        out_shape=jax.ShapeDtypeStruct((r, c), F32),
        compiler_params=_cparams(("parallel",)),
    )(recv)


def _adamw_call(w, g, m, v, rows, name):
    r, c = w.shape

    def body(w_ref, g_ref, m_ref, v_ref, d_ref, nm_ref, nv_ref):
        gr = g_ref[...]
        nm = ADAM_B1 * m_ref[...] + (1.0 - ADAM_B1) * gr
        nv = ADAM_B2 * v_ref[...] + (1.0 - ADAM_B2) * (gr * gr)
        m_hat = nm / (1.0 - ADAM_B1 ** ADAM_STEP)
        v_hat = nv / (1.0 - ADAM_B2 ** ADAM_STEP)
        d_ref[...] = -ADAM_LR * (m_hat / (jnp.sqrt(v_hat) + ADAM_EPS) + ADAM_WD * w_ref[...])
        nm_ref[...] = nm
        nv_ref[...] = nv

    spec = pl.BlockSpec((rows, c), lambda i: (i, 0))
    return pl.pallas_call(
        body, name=name, grid=(r // rows,),
        in_specs=[spec] * 4, out_specs=[spec] * 3,
        out_shape=[jax.ShapeDtypeStruct((r, c), F32)] * 3,
        compiler_params=_cparams(("parallel",)),
    )(w, g, m, v)


SH_ROWS = 16
SH_W = LRU_WIDTH // N_DEV
REP_ROWS = 824
_REP_SIZES = (LRU_WIDTH, 2 * 6 * 64 * 64, 2 * 6 * 64 * 64, RET_WIDTH, 1920, D_MODEL, D_MODEL, D_MODEL, D_MODEL)
_RPB_SIZE = NA_HEADS * (2 * NA_KH - 1) * (2 * NA_KW - 1)


def _pack_sh(cw, ba, bx, lam):
    return jnp.concatenate([cw, ba, bx, lam], axis=0)


def _pad_sh(p):
    pad = [(0, 0)] * (p.ndim - 2) + [(0, SH_ROWS - p.shape[-2]), (0, LANE - p.shape[-1])]
    return jnp.pad(p, pad)


def _pack_rep(cb, wa, wx, gnw, rpb, l1g, l1b, l2g, l2b):
    flat = jnp.concatenate([cb.reshape(-1), wa.reshape(-1), wx.reshape(-1), gnw.reshape(-1),
                            jnp.pad(rpb.reshape(-1), (0, 1920 - _RPB_SIZE)), l1g, l1b, l2g, l2b,
                            jnp.zeros((REP_ROWS * LANE - sum(_REP_SIZES),), F32)])
    return flat.reshape(REP_ROWS, LANE)


def _unpack_rep(p):
    nl = p.shape[0]
    flat = p.reshape(nl, -1)
    out, off = [], 0
    for size in _REP_SIZES:
        out.append(flat[:, off:off + size])
        off += size
    cb, wa, wx, gnw, rpb, l1g, l1b, l2g, l2b = out
    return (cb, wa.reshape(nl, 2, 6, 64, 64), wx.reshape(nl, 2, 6, 64, 64), gnw,
            rpb[:, :_RPB_SIZE].reshape(nl, NA_HEADS, 2 * NA_KH - 1, 2 * NA_KW - 1), l1g, l1b, l2g, l2b)


def _adamw_nd(w, g, m, v, rows, name):
    shp = w.shape
    f = lambda t: t.reshape(-1, shp[-1])
    rows = f(w).shape[0] if rows is None else rows
    return [t.reshape(shp) for t in _adamw_call(f(w), f(g), f(m), f(v), rows, name)]


def kernel(x, w_in, conv_w, conv_b, lru_w_a, lru_b_a, lru_w_x, lru_b_x, lru_lam, ret_gn_w, na_rpb, w_out, ln1_g, ln1_b, w_gate, w_up, w_down, ln2_g, ln2_b, loss_target, m_w_in, m_conv_w, m_conv_b, m_lru_w_a, m_lru_b_a, m_lru_w_x, m_lru_b_x, m_lru_lam, m_ret_gn_w, m_na_rpb, m_w_out, m_ln1_g, m_ln1_b, m_w_gate, m_w_up, m_w_down, m_ln2_g, m_ln2_b, v_w_in, v_conv_w, v_conv_b, v_lru_w_a, v_lru_b_a, v_lru_w_x, v_lru_b_x, v_lru_lam, v_ret_gn_w, v_na_rpb, v_w_out, v_ln1_g, v_ln1_b, v_w_gate, v_w_up, v_w_down, v_ln2_g, v_ln2_b):
    nl = w_in.shape[0]
    T = x.shape[1]
    rows_n = T // GRID_W
    x0, target = x[0], loss_target[0]
    ffpad = W_BLK - FF_BLK

    win_b = w_in.astype(_BF)
    wg_b = jnp.pad(w_gate, ((0, 0), (0, 0), (0, ffpad))).astype(_BF)
    wu_b = jnp.pad(w_up, ((0, 0), (0, 0), (0, ffpad))).astype(_BF)
    wd_b = jnp.pad(w_down, ((0, 0), (0, ffpad), (0, 0))).astype(_BF)
    wout_b = w_out.astype(_BF)
    agf_modes = ["cols", "all"]
    agk_modes = ["cols", "cols", "all", "all"]

    def agf_start(l, after):
        sh = _pad_sh(_pack_sh(conv_w[l], lru_b_a[l], lru_b_x[l], lru_lam[l]))
        return _comm_start_call([win_b[l], sh], agf_modes, after, f"agf_start{l}")

    def agk_start(l, after):
        return _comm_start_call([wg_b[l], wu_b[l], wd_b[l], wout_b[l]], agk_modes, after, f"agk_start{l}")

    tables = _ret_tables(T)
    w4_all = _lru_w4(lru_w_a, lru_w_x)
    layers = []
    gathered = []
    xs, xb = x0, x0.astype(_BF)
    agf_state, token = agf_start(0, [])
    for l in range(nl):
        win, shg = _comm_wait_call(agf_state, agf_modes, [xb], f"agf_wait{l}")
        agk_state, token = agk_start(l, [shg])
        full = shg[:, :10, :SH_W].transpose(1, 0, 2).reshape(10, LRU_WIDTH)
        vec, w4 = _lru_vec(full[0:4], conv_b[l], full[4:6], full[6:8], full[8:10]), w4_all[l]
        gnw8 = jnp.pad(ret_gn_w[l][None], ((0, SUB - 1), (0, 0)))
        btab = _na_bias_tables(na_rpb[l], rows_n)
        proj = _inproj_call(xb, win, token)
        y_lru = _lru_fwd_call(proj, vec, w4)
        y_ret = _ret_fwd_call(proj, tables, gnw8)
        y_na = _na_fwd_call(proj, btab)
        wg, wu, wd, wout = _comm_wait_call(agk_state, agk_modes, [y_na], f"agk_wait{l}")
        wd, wout = wd.reshape(IN_WIDTH, D_MODEL), wout.reshape(D_MODEL, D_MODEL)
        gathered.append((win, wg, wu, wd, wout))
        if l + 1 < nl:
            agf_state, token = agf_start(l + 1, [wout])
        z1, x1, x1b, ycb = _outproj_ln_call(y_lru, y_ret, y_na, xs, wout, ln1_g[l][None], ln1_b[l][None], token)
        z2, x2, x2b, gpb, upb = _ffn_ln_call(x1, x1b, wg, wu, wd, ln2_g[l][None], ln2_b[l][None])
        layers.append(dict(xb=xb, proj=proj, vec=vec, w4=w4, gnw8=gnw8, btab=btab,
                           z1=z1, x1b=x1b, ycb=ycb, z2=z2, gpb=gpb, upb=upb))
        xs, xb = x2, x2b

    dx, loss_blk = _loss_call(xs, target)
    loss = lax.psum(loss_blk[0, 0], ("x", "y", "c"))

    gxa_flags = ["scols", "scols", "blk", "blk"]
    gxb_flags = ["scols", "blk", "all"]
    gxa_state, gxb_state = [None] * nl, [None] * nl
    token = loss_blk
    for l in reversed(range(nl)):
        s = layers[l]
        win, wg, wu, wd, wout = gathered[l]
        dx1, dgp, dup, hid, dz2b, dln2 = _ffn_bwd_call(dx, s["z2"], s["gpb"], s["upb"], wg, wu, wd, ln2_g[l][None], token)
        dwg = _tn_cols_call(s["x1b"], dgp, "tn_cols")
        dwu = _tn_cols_call(s["x1b"], dup, "tn_cols")
        dwd = _tn_rows_call(hid, dz2b, N_BLK, "tn_rows_down").reshape(N_DEV, W_BLK, D_MODEL)
        dz1b, dyc, dres, dln1 = _outproj_bwd_call(dx1, s["z1"], wout, ln1_g[l][None])
        dwout = _tn_rows_call(s["ycb"], dz1b, D_MODEL // 2, "tn_rows_out").reshape(N_DEV, LANE, D_MODEL)
        gxa_state[l], token = _comm_start_call([dwg, dwu, dwd, dwout], gxa_flags, [], f"gxa_start{l}")
        dp, dvec, dw4 = _lru_bwd_call(s["proj"], dyc, s["vec"], s["w4"], token)
        dp, dgnw = _ret_bwd_call(s["proj"], dyc, tables, s["gnw8"], dp)
        dp, dbias = _na_bwd_call(s["proj"], dyc, s["btab"], dp)
        dwin = _tn_cols_call(s["xb"], dp, "tn_cols")
        dx = _inproj_bwd_call(dres, dp, win)
        dcw, dcb, dwa, dba, dwx, dbx, dlam = _lru_unpack(dvec, dw4)
        rep = _pack_rep(dcb, dwa, dwx, dgnw[0], _na_bias_grad(dbias, rows_n), dln1[0], dln1[1], dln2[0], dln2[1])
        sh = _pack_sh(dcw, dba, dbx, dlam).reshape(10, N_DEV, SH_W).transpose(1, 0, 2)
        gxb_state[l], token = _comm_start_call([dwin, _pad_sh(sh), rep], gxb_flags, [], f"gxb_start{l}")

    g_big = [[None] * nl for _ in range(5)]
    g_sh = [None] * nl
    g_rep = [None] * nl
    after = [dx, token]
    for l in reversed(range(nl)):
        ra = _comm_wait_call(gxa_state[l], gxa_flags, after, f"gxa_wait{l}")
        g_big[1][l] = _sum8_call(ra[0], TM, "sum8_cols")[:, :FF_BLK]
        g_big[2][l] = _sum8_call(ra[1], TM, "sum8_cols")[:, :FF_BLK]
        g_big[3][l] = _sum8_call(ra[2], W_BLK, "sum8_down")[:FF_BLK]
        g_big[4][l] = _sum8_call(ra[3], LANE, "sum8_out")
        rb = _comm_wait_call(gxb_state[l], gxb_flags, [g_big[4][l]], f"gxb_wait{l}")
        g_big[0][l] = _sum8_call(rb[0], TM, "sum8_cols")
        g_sh[l] = _sum8_call(rb[1], SH_ROWS, "sum8_sh")
        g_rep[l] = _sum8_call(rb[2], REP_ROWS, "sum8_rep")
        after = [g_rep[l]]

    g_w_in, g_w_gate, g_w_up, g_w_down, g_w_out = [jnp.stack(t) for t in g_big]
    big = {
        "w_in": _adamw_nd(w_in, g_w_in, m_w_in, v_w_in, TM, "adamw_in"),
        "w_gate": _adamw_nd(w_gate, g_w_gate, m_w_gate, v_w_gate, TM, "adamw_ff"),
        "w_up": _adamw_nd(w_up, g_w_up, m_w_up, v_w_up, TM, "adamw_ff"),
        "w_down": _adamw_nd(w_down, g_w_down, m_w_down, v_w_down, FF_BLK, "adamw_down"),
        "w_out": _adamw_nd(w_out, g_w_out, m_w_out, v_w_out, LANE, "adamw_out"),
    }
    g_shp = jnp.stack(g_sh)[:, :, :SH_W]
    rep_names = ("conv_b", "lru_w_a", "lru_w_x", "ret_gn_w", "na_rpb", "ln1_g", "ln1_b", "ln2_g", "ln2_b")
    grads = {"w_in": g_w_in, "w_gate": g_w_gate, "w_up": g_w_up, "w_down": g_w_down, "w_out": g_w_out,
             "conv_w": g_shp[:, 0:4], "lru_b_a": g_shp[:, 4:6], "lru_b_x": g_shp[:, 6:8], "lru_lam": g_shp[:, 8:10]}
    grads.update(dict(zip(rep_names, _unpack_rep(jnp.stack(g_rep)))))
    small = {
        "conv_w": (conv_w, m_conv_w, v_conv_w), "conv_b": (conv_b, m_conv_b, v_conv_b),
        "lru_w_a": (lru_w_a, m_lru_w_a, v_lru_w_a), "lru_b_a": (lru_b_a, m_lru_b_a, v_lru_b_a),
        "lru_w_x": (lru_w_x, m_lru_w_x, v_lru_w_x), "lru_b_x": (lru_b_x, m_lru_b_x, v_lru_b_x),
        "lru_lam": (lru_lam, m_lru_lam, v_lru_lam), "ret_gn_w": (ret_gn_w, m_ret_gn_w, v_ret_gn_w),
        "na_rpb": (na_rpb, m_na_rpb, v_na_rpb), "ln1_g": (ln1_g, m_ln1_g, v_ln1_g), "ln1_b": (ln1_b, m_ln1_b, v_ln1_b),
        "ln2_g": (ln2_g, m_ln2_g, v_ln2_g), "ln2_b": (ln2_b, m_ln2_b, v_ln2_b),
    }
    for name, (w_, m_, v_) in small.items():
        big[name] = _adamw_nd(w_, grads[name], m_, v_, None, "adamw_small")
    kinds = [{n: big[n][k] for n in big} for k in range(3)]
    order = ("w_in", "conv_w", "conv_b", "lru_w_a", "lru_b_a", "lru_w_x", "lru_b_x", "lru_lam", "ret_gn_w", "na_rpb",
             "w_out", "ln1_g", "ln1_b", "w_gate", "w_up", "w_down", "ln2_g", "ln2_b")
    outs = [loss, dx[None]]
    for d in (grads, *kinds):
        outs.extend(d[n] for n in order)
    return tuple(outs)
```

```python
import functools
import math

import numpy as np
import jax
import jax.numpy as jnp
from jax import lax
from jax.experimental import pallas as pl
from jax.experimental.pallas import tpu as pltpu

F32 = jnp.float32
_BF = jnp.bfloat16

D_MODEL = 1024
DEPTH = 4
GRID_W = 64
HEAD_DIM = 64
LRU_WIDTH = 384
RET_WIDTH = 384
RET_HEADS = 6
NA_WIDTH = 256
NA_HEADS = 4
IN_WIDTH = 3072
CONV_WIDTH = 4
LRU_C = 8.0
RET_CHUNK = 128
ROPE_BASE = 10000.0
GN_EPS = 1e-6
NA_KH = 8
NA_KW = 16
D_FF = 2816
FF_BLK = 352
N_DEV = 8
ALPHA = (2 * DEPTH) ** 0.25
LN_EPS = 1e-5
ADAM_LR = 0.001
ADAM_B1 = 0.9
ADAM_B2 = 0.999
ADAM_EPS = 1e-08
ADAM_WD = 0.01
ADAM_STEP = 10

LANE = 128
SUB = 8
VMEM_MB = 56
NEG = -1e30

MESH = pl.DeviceIdType.MESH


def _cparams(sem=None, vmem_mb=VMEM_MB):
    return pltpu.CompilerParams(dimension_semantics=sem, vmem_limit_bytes=vmem_mb << 20)


def _mm(a, b):
    return jnp.dot(a.astype(_BF), b.astype(_BF), preferred_element_type=F32)


def _mm_nt(a, b):
    return lax.dot_general(a.astype(_BF), b.astype(_BF), (((1,), (1,)), ((), ())), preferred_element_type=F32)


def _mm_tn(a, b):
    return lax.dot_general(a.astype(_BF), b.astype(_BF), (((0,), (0,)), ((), ())), preferred_element_type=F32)


def _sigmoid(x):
    return jax.nn.sigmoid(x)


def _rows(start, size):
    return pl.ds(pl.multiple_of(start, SUB), size)


def _loop2(n, body, init):
    assert n % 2 == 0
    return lax.fori_loop(0, n // 2, lambda i, c: body(2 * i + 1, body(2 * i, c)), init)


def _strip(T, col, buffers=2):
    return pl.BlockSpec((T, LANE), lambda j: (0, col(j)), pipeline_mode=pl.Buffered(buffers))


LRU_CH = 256
_GELU_C0 = math.sqrt(2.0 / math.pi)
_GELU_C1 = 0.044715


def _gelu_parts(x):
    x2 = x * x
    t = jnp.tanh(_GELU_C0 * (x + _GELU_C1 * x * x2))
    val = 0.5 * x * (1.0 + t)
    der = 0.5 * (1.0 + t) + 0.5 * x * (1.0 - t * t) * _GELU_C0 * (1.0 + 3.0 * _GELU_C1 * x2)
    return val, der


def _softplus_neg(lam):
    e = jnp.exp(-jnp.abs(lam))
    w = 1.0 + e
    l1p = jnp.where(w == 1.0, e, jnp.log(w) * (e / jnp.where(w == 1.0, 1.0, w - 1.0)))
    return jnp.maximum(-lam, 0.0) + l1p


def _window(ref, t0, ch, T):
    prev = ref[_rows(jnp.maximum(t0 - SUB, 0), SUB), :].astype(F32)
    nxt = ref[_rows(jnp.minimum(t0 + ch, T - SUB), SUB), :].astype(F32)
    prev = jnp.where(t0 > 0, prev, 0.0)
    nxt = jnp.where(t0 + ch < T, nxt, 0.0)
    return jnp.concatenate([prev, ref[_rows(t0, ch), :].astype(F32), nxt], axis=0)


def _tap(win, shift, ch):
    n = win.shape[0]
    return pltpu.roll(win, (-shift) % n, 0)[SUB:SUB + ch]


def _lru_conv(xb_ref, vec, t0, T):
    win = _window(xb_ref, t0, LRU_CH, T)
    xc = jnp.broadcast_to(vec[4:5, :], (LRU_CH, LANE))
    for j in range(CONV_WIDTH):
        xc = xc + _tap(win, j - CONV_WIDTH // 2, LRU_CH) * vec[j:j + 1, :]
    return xc


def _lru_dir(pre_a, pre_x, sp):
    r = _sigmoid(pre_a)
    i = _sigmoid(pre_x)
    log_a = (-LRU_C) * r * sp
    a = jnp.exp(log_a)
    z = jnp.tanh(-log_a) * (a * a + 1.0)
    s = jnp.sqrt(z)
    return r, i, a, s


def _scan_tile(a, b, reverse, row):
    for k in (1, 2, 4):
        if not reverse:
            a_s, b_s, m = pltpu.roll(a, k, 0), pltpu.roll(b, k, 0), row >= k
        else:
            a_s, b_s, m = pltpu.roll(a, SUB - k, 0), pltpu.roll(b, SUB - k, 0), row < SUB - k
        b = jnp.where(m, a * b_s + b, b)
        a = jnp.where(m, a * a_s, a)
    return a, b


def _bcast_row(x, r):
    return jnp.broadcast_to(x[r:r + 1, :], (SUB, LANE))


def _lru_prepare(xb_ref, w4_ref, vec, xc_ref, af_ref, uf_ref, ab_ref, ub_ref, T):
    sp_f = _softplus_neg(vec[9:10, :])
    sp_b = _softplus_neg(vec[10:11, :])
    w4 = w4_ref[0]

    def body(c, carry):
        t0 = c * LRU_CH
        xc = _lru_conv(xb_ref, vec, t0, T)
        if xc_ref is not None:
            xc_ref[_rows(t0, LRU_CH), :] = xc
        pre = _mm(xc, w4)
        _, i, a, s = _lru_dir(pre[:, 0:128] + vec[5:6, :], pre[:, 128:256] + vec[6:7, :], sp_f)
        af_ref[_rows(t0, LRU_CH), :] = a
        uf_ref[_rows(t0, LRU_CH), :] = s * (i * xc)
        _, i, a, s = _lru_dir(pre[:, 256:384] + vec[7:8, :], pre[:, 384:512] + vec[8:9, :], sp_b)
        ab_ref[_rows(t0, LRU_CH), :] = a
        ub_ref[_rows(t0, LRU_CH), :] = s * (i * xc)
        return carry

    lax.fori_loop(0, T // LRU_CH, body, 0)


def _lru_scan(af_ref, uf_ref, ab_ref, ub_ref, T):
    nt = T // SUB
    row = lax.broadcasted_iota(jnp.int32, (SUB, LANE), 0)

    def body(j, carry):
        hf, hb = carry
        sf = _rows(j * SUB, SUB)
        sb = _rows((nt - 1 - j) * SUB, SUB)
        a, b = _scan_tile(af_ref[sf, :], uf_ref[sf, :], False, row)
        h = a * hf + b
        uf_ref[sf, :] = h
        hf = _bcast_row(h, SUB - 1)
        a, b = _scan_tile(ab_ref[sb, :], ub_ref[sb, :], True, row)
        h = a * hb + b
        ub_ref[sb, :] = h
        hb = _bcast_row(h, 0)
        return hf, hb

    z = jnp.zeros((SUB, LANE), F32)
    lax.fori_loop(0, nt, body, (z, z))


def _lru_fwd_call(proj, vec, w4):
    T = proj.shape[0]

    def body(xb_ref, gate_ref, vec_ref, w4_ref, y_ref, af_ref, uf_ref, ab_ref, ub_ref):
        vec = vec_ref[...]
        _lru_prepare(xb_ref, w4_ref, vec, None, af_ref, uf_ref, ab_ref, ub_ref, T)
        _lru_scan(af_ref, uf_ref, ab_ref, ub_ref, T)

        def out(c, carry):
            rows = _rows(c * LRU_CH, LRU_CH)
            gl, _ = _gelu_parts(gate_ref[rows, :])
            y_ref[rows, :] = (uf_ref[rows, :] + ub_ref[rows, :]) * gl
            return carry

        lax.fori_loop(0, T // LRU_CH, out, 0)

    return pl.pallas_call(
        body, name="lru_fwd", grid=(LRU_WIDTH // LANE,),
        in_specs=[_strip(T, lambda j: j), _strip(T, lambda j: j + 3),
                  pl.BlockSpec((16, LANE), lambda j: (0, j)),
                  pl.BlockSpec((1, LANE, 4 * LANE), lambda j: (j, 0, 0))],
        out_specs=_strip(T, lambda j: j, buffers=1),
        out_shape=jax.ShapeDtypeStruct((T, LRU_WIDTH), F32),
        scratch_shapes=[pltpu.VMEM((T, LANE), F32)] * 4,
        compiler_params=_cparams(("arbitrary",)),
    )(proj, proj, vec, w4)


def _store_strips(stage_ref, dp_ref, cols, sems):
    copies = [pltpu.make_async_copy(stage_ref.at[b], dp_ref.at[:, pl.ds(pl.multiple_of(c * LANE, LANE), LANE)], sems.at[b])
              for b, c in enumerate(cols)]
    for cp in copies:
        cp.start()
    for cp in copies:
        cp.wait()


def _lru_bwd_call(proj, dycat, vec, w4, after):
    T = proj.shape[0]
    nt = T // SUB
    nch = T // LRU_CH

    def body(xb_ref, gate_ref, dy_ref, vec_ref, w4_ref, after_ref, dp_ref, dvec_ref, dw4_ref,
             xc_ref, af_ref, hf_ref, ab_ref, hb_ref, dh_ref, stage_ref, sems):
        dxb_ref, dgate_ref = stage_ref.at[0], stage_ref.at[1]
        vec = vec_ref[...]
        _lru_prepare(xb_ref, w4_ref, vec, xc_ref, af_ref, hf_ref, ab_ref, hb_ref, T)
        _lru_scan(af_ref, hf_ref, ab_ref, hb_ref, T)

        def gate_bwd(c, carry):
            rows = _rows(c * LRU_CH, LRU_CH)
            gl, dgl = _gelu_parts(gate_ref[rows, :])
            dy = dy_ref[rows, :]
            dgate_ref[rows, :] = (dy * (hf_ref[rows, :] + hb_ref[rows, :]) * dgl).astype(dgate_ref.dtype)
            dh_ref[rows, :] = dy * gl
            return carry

        lax.fori_loop(0, nch, gate_bwd, 0)

        row = lax.broadcasted_iota(jnp.int32, (SUB, LANE), 0)

        def adj(j, carry):
            gf, a_next, gb, a_prev = carry
            tf = nt - 1 - j
            sf = _rows(tf * SUB, SUB)
            a_t = af_ref[sf, :]
            h_t = hf_ref[sf, :]
            coef = jnp.where(row == SUB - 1, a_next, pltpu.roll(a_t, SUB - 1, 0))
            ac, bc = _scan_tile(coef, dh_ref[sf, :], True, row)
            g = ac * gf + bc
            h_prev = hf_ref[_rows(jnp.maximum(tf - 1, 0) * SUB, SUB), :]
            h_prev = jnp.where(tf > 0, _bcast_row(h_prev, SUB - 1), 0.0)
            hs = jnp.where(row == 0, h_prev, pltpu.roll(h_t, 1, 0))
            af_ref[sf, :] = g * hs
            hf_ref[sf, :] = g
            gf = _bcast_row(g, 0)
            a_next = _bcast_row(a_t, 0)
            sb = _rows(j * SUB, SUB)
            a_t = ab_ref[sb, :]
            h_t = hb_ref[sb, :]
            coef = jnp.where(row == 0, a_prev, pltpu.roll(a_t, 1, 0))
            ac, bc = _scan_tile(coef, dh_ref[sb, :], False, row)
            g = ac * gb + bc
            h_next = hb_ref[_rows(jnp.minimum(j + 1, nt - 1) * SUB, SUB), :]
            h_next = jnp.where(j < nt - 1, _bcast_row(h_next, 0), 0.0)
            hs = jnp.where(row == SUB - 1, h_next, pltpu.roll(h_t, SUB - 1, 0))
            ab_ref[sb, :] = g * hs
            hb_ref[sb, :] = g
            gb = _bcast_row(g, SUB - 1)
            a_prev = _bcast_row(a_t, SUB - 1)
            return gf, a_next, gb, a_prev

        z = jnp.zeros((SUB, LANE), F32)
        lax.fori_loop(0, nt, adj, (z, z, z, z))

        sp_f = _softplus_neg(vec[9:10, :])
        sp_b = _softplus_neg(vec[10:11, :])
        w4 = w4_ref[0]
        dw4_ref[...] = jnp.zeros_like(dw4_ref)

        def one_dir(pre_a, pre_x, sp, xc, du, da):
            r, i, a, s = _lru_dir(pre_a, pre_x, sp)
            d_i = du * s * xc
            dxc = du * s * i
            d_s = du * i * xc
            d_log = da * a - d_s * (a * a) / s
            d_r = d_log * (-LRU_C) * sp
            d_sp = jnp.sum(d_log * (-LRU_C) * r, axis=0, keepdims=True)
            return d_r * r * (1.0 - r), d_i * i * (1.0 - i), dxc, d_sp

        def gates_bwd(c, carry):
            db, dspf, dspb = carry
            rows = _rows(c * LRU_CH, LRU_CH)
            xc = xc_ref[rows, :]
            pre = _mm(xc, w4)
            dpa_f, dpx_f, dxc_f, d_sp_f = one_dir(pre[:, 0:128] + vec[5:6, :], pre[:, 128:256] + vec[6:7, :],
                                                  sp_f, xc, hf_ref[rows, :], af_ref[rows, :])
            dpa_b, dpx_b, dxc_b, d_sp_b = one_dir(pre[:, 256:384] + vec[7:8, :], pre[:, 384:512] + vec[8:9, :],
                                                  sp_b, xc, hb_ref[rows, :], ab_ref[rows, :])
            dpre = jnp.concatenate([dpa_f, dpx_f, dpa_b, dpx_b], axis=1)
            dw4_ref[0] += _mm_tn(xc, dpre)
            dh_ref[rows, :] = dxc_f + dxc_b + _mm_nt(dpre, w4)
            return db + jnp.sum(dpre, axis=0, keepdims=True), dspf + d_sp_f, dspb + d_sp_b

        z1 = jnp.zeros((1, LANE), F32)
        db, dspf, dspb = lax.fori_loop(0, nch, gates_bwd, (jnp.zeros((1, 4 * LANE), F32), z1, z1))

        def conv_bwd(c, carry):
            t0 = c * LRU_CH
            rows = _rows(t0, LRU_CH)
            dwin = _window(dh_ref, t0, LRU_CH, T)
            xwin = _window(xb_ref, t0, LRU_CH, T)
            dxc = dh_ref[rows, :]
            dxb = jnp.zeros((LRU_CH, LANE), F32)
            out = []
            for j in range(CONV_WIDTH):
                off = j - CONV_WIDTH // 2
                dxb = dxb + _tap(dwin, -off, LRU_CH) * vec[j:j + 1, :]
                out.append(carry[j] + jnp.sum(dxc * _tap(xwin, off, LRU_CH), axis=0, keepdims=True))
            dxb_ref[rows, :] = dxb.astype(dxb_ref.dtype)
            out.append(carry[CONV_WIDTH] + jnp.sum(dxc, axis=0, keepdims=True))
            return tuple(out)

        dconv = lax.fori_loop(0, nch, conv_bwd, (z1,) * (CONV_WIDTH + 1))
        dlam_f = dspf * (-_sigmoid(-vec[9:10, :]))
        dlam_b = dspb * (-_sigmoid(-vec[10:11, :]))
        dvec_ref[...] = jnp.concatenate(
            list(dconv) + [db[:, 0:128], db[:, 128:256], db[:, 256:384], db[:, 384:512], dlam_f, dlam_b,
                           jnp.zeros((5, LANE), F32)], axis=0)
        j = pl.program_id(0)
        _store_strips(stage_ref, dp_ref, (j, j + 3), sems)

    ns = LRU_WIDTH // LANE
    return pl.pallas_call(
        body, name="lru_bwd", grid=(ns,),
        in_specs=[_strip(T, lambda j: j), _strip(T, lambda j: j + 3), _strip(T, lambda j: j),
                  pl.BlockSpec((16, LANE), lambda j: (0, j)),
                  pl.BlockSpec((1, LANE, 4 * LANE), lambda j: (j, 0, 0)),
                  pl.BlockSpec(memory_space=pl.ANY)],
        out_specs=[pl.BlockSpec(memory_space=pl.ANY),
                   pl.BlockSpec((16, LANE), lambda j: (0, j)),
                   pl.BlockSpec((1, LANE, 4 * LANE), lambda j: (j, 0, 0))],
        out_shape=[jax.ShapeDtypeStruct((T, IN_WIDTH), _BF),
                   jax.ShapeDtypeStruct((16, LRU_WIDTH), F32), jax.ShapeDtypeStruct((ns, LANE, 4 * LANE), F32)],
        scratch_shapes=[pltpu.VMEM((T, LANE), F32)] * 6 + [pltpu.VMEM((2, T, LANE), _BF), pltpu.SemaphoreType.DMA((2,))],
        compiler_params=_cparams(("arbitrary",)),
    )(proj, proj, dycat, vec, w4, after)


def _lru_vec(cw, cb, ba, bx, lam):
    return jnp.concatenate([cw, cb[None], ba[0:1], bx[0:1], ba[1:2], bx[1:2], lam, jnp.zeros((5, LRU_WIDTH), F32)], axis=0)


def _lru_w4(wa, wx):
    nl = wa.shape[0]
    w = jnp.stack([wa[:, 0], wx[:, 0], wa[:, 1], wx[:, 1]], axis=1)
    w = w.reshape(nl, 4, 3, 2, 64, 64)
    eye = jnp.eye(2, dtype=w.dtype)
    bd = w[:, :, :, :, :, None, :] * eye[None, None, None, :, None, :, None]
    bd = bd.reshape(nl, 4, 3, LANE, LANE)
    return bd.transpose(0, 2, 3, 1, 4).reshape(nl, 3, LANE, 4 * LANE).astype(_BF)


def _lru_unpack(dvec, dw4):
    def blocks(m):
        m = m.reshape(3, 2, 64, 2, 64)
        return jnp.stack([m[:, 0, :, 0, :], m[:, 1, :, 1, :]], axis=1).reshape(6, 64, 64)
    parts = [blocks(dw4[:, :, k * LANE:(k + 1) * LANE]) for k in range(4)]
    dwa = jnp.stack([parts[0], parts[2]])
    dwx = jnp.stack([parts[1], parts[3]])
    dba = jnp.stack([dvec[5], dvec[7]])
    dbx = jnp.stack([dvec[6], dvec[8]])
    return dvec[0:4], dvec[4], dwa, dba, dwx, dbx, dvec[9:11]


RC = 2 * RET_CHUNK


def _ret_tables(T):
    half = HEAD_DIM // 2
    pos = jnp.arange(T, dtype=F32)
    inv_freq = ROPE_BASE ** (-jnp.arange(half, dtype=F32) / half)
    ang = pos[:, None] * inv_freq[None, :]
    cos = jnp.tile(jnp.cos(ang), (1, 4))
    sin = jnp.tile(jnp.concatenate([-jnp.sin(ang), jnp.sin(ang)], axis=1), (1, 2))
    log_g = jnp.log1p(-jnp.exp2(-5.0 - jnp.arange(RET_HEADS, dtype=F32)))
    idx = jnp.arange(RC, dtype=F32)
    dec = jnp.exp(jnp.abs(idx[:, None] - idx[None, :]) * log_g[:, None, None])
    lg = jnp.repeat(log_g, HEAD_DIM).reshape(3, 1, LANE)
    col = idx[None, :, None]
    rtab = jnp.stack([jnp.exp((RC - 1 - col) * lg), jnp.exp(col * lg),
                      jnp.exp((col + 1.0) * lg), jnp.exp((RC - col) * lg)], axis=1)
    gch = jnp.broadcast_to(jnp.exp(RC * lg), (3, SUB, LANE))
    return cos, sin, dec, rtab, gch


def _swap32(x, lane):
    return jnp.where((lane & 32) == 0, pltpu.roll(x, LANE - 32, 1), pltpu.roll(x, 32, 1))


def _head_mean(x, m0, m1):
    s0 = jnp.sum(x * m0, axis=-1, keepdims=True)
    s1 = jnp.sum(x * m1, axis=-1, keepdims=True)
    return (s0 * m0 + s1 * m1) * (1.0 / HEAD_DIM)


def _ret_masks():
    lane = lax.broadcasted_iota(jnp.int32, (RC, LANE), 1)
    m0 = (lane < HEAD_DIM).astype(F32)
    r = lax.broadcasted_iota(jnp.int32, (LANE, LANE), 0) // HEAD_DIM
    c = lax.broadcasted_iota(jnp.int32, (LANE, LANE), 1) // HEAD_DIM
    return lane, m0, 1.0 - m0, (r == c).astype(F32)


def _ret_specs(T):
    const = lambda shape, imap: pl.BlockSpec(shape, imap)
    return [_strip(T, lambda j: j + 6), _strip(T, lambda j: j + 9), _strip(T, lambda j: j + 12),
            _strip(T, lambda j: j + 15),
            pl.BlockSpec((T, LANE), lambda j: (0, 0), pipeline_mode=pl.Buffered(1)),
            pl.BlockSpec((T, LANE), lambda j: (0, 0), pipeline_mode=pl.Buffered(1)),
            const((2, RC, RC), lambda j: (j, 0, 0)),
            const((1, 4, RC, LANE), lambda j: (j, 0, 0, 0)),
            const((1, SUB, LANE), lambda j: (j, 0, 0)),
            const((SUB, LANE), lambda j: (0, j))]


def _ret_fwd_call(proj, tables, gnw8):
    T = proj.shape[0]
    nc = T // RC
    cos, sin, dec, rtab, gch = tables

    def body(q_ref, k_ref, v_ref, g_ref, cos_ref, sin_ref, dec_ref, rtab_ref, gch_ref, gnw_ref, y_ref, stf_ref):
        lane, m0, m1, bd = _ret_masks()
        gch_v = gch_ref[0][0:1, :]
        gnw = gnw_ref[0:1, :]
        dkf, dkb, dqf, dqb = rtab_ref[0, 0], rtab_ref[0, 1], rtab_ref[0, 2], rtab_ref[0, 3]

        def rope(x, rows):
            return x * cos_ref[rows, :] + _swap32(x, lane) * sin_ref[rows, :]

        def pass_a(n, st):
            rows = _rows(n * RC, RC)
            stf_ref[n] = st
            kr = rope(k_ref[rows, :], rows) * (HEAD_DIM ** -0.5)
            return gch_v * st + _mm_tn(kr * dkf, v_ref[rows, :]) * bd

        _loop2(nc, pass_a, jnp.zeros((LANE, LANE), F32))

        def pass_b(i, stb):
            ns = [nc - 1 - 2 * i, nc - 2 - 2 * i]
            rows = [_rows(n * RC, RC) for n in ns]
            heads = ((0, m0), (1, m1))
            qr = [rope(q_ref[r, :], r) for r in rows]
            kr = [rope(k_ref[r, :], r) * (HEAD_DIM ** -0.5) for r in rows]
            v = [v_ref[r, :] for r in rows]
            kv = [_mm_tn(kr[c] * dkb, v[c]) * bd for c in range(2)]
            stbs = [stb, gch_v * stb + kv[0]]
            s = [[_mm_nt(qr[c] * m, kr[c]) * dec_ref[h] for h, m in heads] for c in range(2)]
            o = [_mm(qr[c] * dqf, stf_ref[ns[c]]) + _mm(qr[c] * dqb, stbs[c]) for c in range(2)]
            o = [o[c] + _mm(s[c][0], v[c] * m0) + _mm(s[c][1], v[c] * m1) for c in range(2)]
            oc = [o_ - _head_mean(o_, m0, m1) for o_ in o]
            on = [oc_ * lax.rsqrt(_head_mean(oc_ * oc_, m0, m1) + GN_EPS) for oc_ in oc]
            for c in range(2):
                g = g_ref[rows[c], :]
                y_ref[rows[c], :] = (g * _sigmoid(g)) * (on[c] * gnw)
            return gch_v * stbs[1] + kv[1]

        assert nc % 2 == 0
        lax.fori_loop(0, nc // 2, pass_b, jnp.zeros((LANE, LANE), F32))

    return pl.pallas_call(
        body, name="ret_fwd", grid=(RET_WIDTH // LANE,),
        in_specs=_ret_specs(T),
        out_specs=_strip(T, lambda j: j, buffers=1),
        out_shape=jax.ShapeDtypeStruct((T, RET_WIDTH), F32),
        scratch_shapes=[pltpu.VMEM((nc, LANE, LANE), F32)],
        compiler_params=_cparams(("arbitrary",)),
    )(proj, proj, proj, proj, cos, sin, dec, rtab, gch, gnw8)


def _ret_bwd_call(proj, dycat, tables, gnw8, dp):
    T = proj.shape[0]
    nc = T // RC
    cos, sin, dec, rtab, gch = tables

    def body(q_ref, k_ref, v_ref, g_ref, cos_ref, sin_ref, dec_ref, rtab_ref, gch_ref, gnw_ref, dy_ref, dp_in_ref,
             dp_out_ref, dgnw_ref, stf_ref, dstb_ref, dkr_ref, dv_ref, dp_ref, sems):
        lane, m0, m1, bd = _ret_masks()
        gch_v = gch_ref[0][0:1, :]
        gnw = gnw_ref[0:1, :]
        dkf, dkb, dqf, dqb = rtab_ref[0, 0], rtab_ref[0, 1], rtab_ref[0, 2], rtab_ref[0, 3]
        scale = HEAD_DIM ** -0.5
        zst = jnp.zeros((LANE, LANE), F32)

        def rope(x, rows):
            return x * cos_ref[rows, :] + _swap32(x, lane) * sin_ref[rows, :]

        def rope_t(d, rows):
            return d * cos_ref[rows, :] + _swap32(d * sin_ref[rows, :], lane)

        def pass_a(n, st):
            rows = _rows(n * RC, RC)
            stf_ref[n] = st
            kr = rope(k_ref[rows, :], rows) * scale
            return gch_v * st + _mm_tn(kr * dkf, v_ref[rows, :]) * bd

        _loop2(nc, pass_a, zst)

        def pass_b(i, carry):
            stb, d_f, dgnw = carry
            two = range(2)
            heads = ((0, m0), (1, m1))
            ns = [nc - 1 - 2 * i, nc - 2 - 2 * i]
            rows = [_rows(n * RC, RC) for n in ns]
            qr = [rope(q_ref[r, :], r) for r in rows]
            kr = [rope(k_ref[r, :], r) * scale for r in rows]
            v = [v_ref[r, :] for r in rows]
            stf = [stf_ref[n] for n in ns]
            kvb = [_mm_tn(kr[c] * dkb, v[c]) * bd for c in two]
            stbs = [stb, gch_v * stb + kvb[0]]
            qf = [qr[c] * dqf for c in two]
            qb = [qr[c] * dqb for c in two]
            s = [[_mm_nt(qr[c] * m, kr[c]) * dec_ref[h] for h, m in heads] for c in two]
            o = [_mm(qf[c], stf[c]) + _mm(qb[c], stbs[c]) for c in two]
            o = [o[c] + _mm(s[c][0], v[c] * m0) + _mm(s[c][1], v[c] * m1) for c in two]
            oc = [o_ - _head_mean(o_, m0, m1) for o_ in o]
            rstd = [lax.rsqrt(_head_mean(oc_ * oc_, m0, m1) + GN_EPS) for oc_ in oc]
            on = [oc[c] * rstd[c] for c in two]
            do = []
            for c in two:
                g = g_ref[rows[c], :]
                sg = _sigmoid(g)
                dy = dy_ref[rows[c], :]
                dp_ref[3, rows[c], :] = (dy * (on[c] * gnw) * (sg * (1.0 + g * (1.0 - sg)))).astype(dp_ref.dtype)
                t = dy * (g * sg)
                dgnw = dgnw + jnp.sum(t * on[c], axis=0, keepdims=True)
                don = t * gnw
                do.append(rstd[c] * (don - _head_mean(don, m0, m1) - on[c] * _head_mean(don * on[c], m0, m1)))
            dstf = [_mm_tn(qf[c], do[c]) * bd for c in two]
            dfs = [d_f, dstf[0] + gch_v * d_f]
            ds = [[_mm_nt(do[c] * m, v[c]) * dec_ref[h] for h, m in heads] for c in two]
            dqr = [_mm_nt(do[c], stf[c]) * dqf + _mm_nt(do[c], stbs[c]) * dqb
                   + _mm(ds[c][0], kr[c] * m0) + _mm(ds[c][1], kr[c] * m1) for c in two]
            dkr = [_mm_nt(v[c], dfs[c]) * dkf + _mm_tn(ds[c][0], qr[c] * m0) + _mm_tn(ds[c][1], qr[c] * m1) for c in two]
            dv = [_mm(kr[c] * dkf, dfs[c]) + _mm_tn(s[c][0], do[c] * m0) + _mm_tn(s[c][1], do[c] * m1) for c in two]
            for c in two:
                dp_ref[0, rows[c], :] = rope_t(dqr[c], rows[c]).astype(dp_ref.dtype)
                dkr_ref[rows[c], :] = dkr[c]
                dv_ref[rows[c], :] = dv[c]
                dstb_ref[ns[c]] = _mm_tn(qb[c], do[c]) * bd
            return gch_v * stbs[1] + kvb[1], dstf[1] + gch_v * dfs[1], dgnw

        assert nc % 2 == 0
        _, _, dgnw = lax.fori_loop(0, nc // 2, pass_b, (zst, zst, jnp.zeros((1, LANE), F32)))
        dgnw_ref[...] = jnp.concatenate([dgnw, jnp.zeros((SUB - 1, LANE), F32)], axis=0)

        def pass_c(n, d_b):
            rows = _rows(n * RC, RC)
            kr = rope(k_ref[rows, :], rows) * scale
            v = v_ref[rows, :]
            dkr = dkr_ref[rows, :] + _mm_nt(v, d_b) * dkb
            dp_ref[1, rows, :] = (rope_t(dkr, rows) * scale).astype(dp_ref.dtype)
            dp_ref[2, rows, :] = (dv_ref[rows, :] + _mm(kr * dkb, d_b)).astype(dp_ref.dtype)
            return dstb_ref[n] + gch_v * d_b

        _loop2(nc, pass_c, zst)
        j = pl.program_id(0)
        _store_strips(dp_ref, dp_out_ref, (j + 6, j + 9, j + 12, j + 15), sems)

    n_in = len(_ret_specs(T)) + 1
    return pl.pallas_call(
        body, name="ret_bwd", grid=(RET_WIDTH // LANE,),
        in_specs=_ret_specs(T) + [_strip(T, lambda j: j + 3), pl.BlockSpec(memory_space=pl.ANY)],
        out_specs=[pl.BlockSpec(memory_space=pl.ANY), pl.BlockSpec((SUB, LANE), lambda j: (0, j))],
        out_shape=[jax.ShapeDtypeStruct(dp.shape, dp.dtype), jax.ShapeDtypeStruct((SUB, RET_WIDTH), F32)],
        scratch_shapes=[pltpu.VMEM((nc, LANE, LANE), F32), pltpu.VMEM((nc, LANE, LANE), F32),
                        pltpu.VMEM((T, LANE), F32), pltpu.VMEM((T, LANE), F32),
                        pltpu.VMEM((4, T, LANE), _BF), pltpu.SemaphoreType.DMA((4,))],
        input_output_aliases={n_in: 0},
        compiler_params=_cparams(("arbitrary",)),
    )(proj, proj, proj, proj, cos, sin, dec, rtab, gch, gnw8, dycat, dp)


NA_Q = 2 * GRID_W
NA_WROWS = 10
NA_K = NA_WROWS * GRID_W
NA_CHUNKS = NA_K // LANE
NA_UNROLL = 2
NA_TYPES = 5
_ONEHOT_PRECISION = lax.Precision.HIGH


def _na_onehots(rows_n):
    reps = [(0, 0), (2, 0), (4, 0), (rows_n - 4, rows_n - NA_WROWS), (rows_n - 2, rows_n - NA_WROWS)]
    rm = np.zeros((NA_TYPES, 2, NA_WROWS, 2 * NA_KH - 1), np.float32)
    for t, (r, ws) in enumerate(reps):
        for qh in range(2):
            qrow = r + qh
            rstart = min(max(qrow - NA_KH // 2, 0), rows_n - NA_KH)
            for kh in range(NA_WROWS):
                krow = ws + kh
                if rstart <= krow < rstart + NA_KH:
                    rm[t, qh, kh, krow - qrow + NA_KH - 1] = 1.0
    cm = np.zeros((GRID_W, GRID_W, 2 * NA_KW - 1), np.float32)
    for qc in range(GRID_W):
        cstart = min(max(qc - NA_KW // 2, 0), GRID_W - NA_KW)
        for kc in range(cstart, cstart + NA_KW):
            cm[qc, kc, kc - qc + NA_KW - 1] = 1.0
    rm2 = rm.reshape(NA_TYPES, 2, NA_CHUNKS, 2, 2 * NA_KH - 1)
    cm2 = np.zeros((GRID_W, LANE, 2, 2 * NA_KW - 1), np.float32)
    for z in range(2):
        cm2[:, z * GRID_W:(z + 1) * GRID_W, z, :] = cm
    return rm2, cm2


def _na_bias_tables(rpb, rows_n):
    rm, cm = _na_onehots(rows_n)
    val = jnp.einsum("hab,tqpza,xkzb->htpqxk", rpb, rm, cm, precision=_ONEHOT_PRECISION)
    valid = np.einsum("tqpz,xkz->tpqxk", rm.sum(-1), cm.sum(-1)) > 0.5
    return jnp.where(valid[None], val, NEG).reshape(2, 2, NA_TYPES, NA_CHUNKS, NA_Q, LANE)


def _na_bias_grad(dtab, rows_n):
    rm, cm = _na_onehots(rows_n)
    d6 = dtab.reshape(NA_HEADS, NA_TYPES, NA_CHUNKS, 2, GRID_W, LANE)
    return jnp.einsum("htpqxk,tqpza,xkzb->hab", d6, rm, cm, precision=_ONEHOT_PRECISION)


def _na_bias(b_ref, h, typ):
    return jnp.concatenate([b_ref[0, h, typ, c] for c in range(NA_CHUNKS)], axis=1)


def _na_step(p, npairs, rows_n):
    ws = jnp.clip(2 * p - NA_KH // 2, 0, rows_n - NA_WROWS)
    koff = pl.multiple_of(ws * GRID_W, LANE)
    typ = jnp.where(p == 0, 0, jnp.where(p == 1, 1, jnp.where(p == npairs - 2, 3, jnp.where(p == npairs - 1, 4, 2))))
    return _rows(p * NA_Q, NA_Q), pl.ds(koff, NA_K), typ


def _na_fwd_call(proj, btab):
    T = proj.shape[0]
    npairs, rows_n = T // NA_Q, T // GRID_W

    def body(q_ref, k_ref, v_ref, b_ref, o_ref):
        lane = lax.broadcasted_iota(jnp.int32, (NA_Q, LANE), 1)
        m0 = (lane < HEAD_DIM).astype(F32)
        m1 = 1.0 - m0

        def steps(i, carry):
            idx = [_na_step(NA_UNROLL * i + u, npairs, rows_n) for u in range(NA_UNROLL)]
            chains = [(u, h, m) for u in range(NA_UNROLL) for h, m in ((0, m0), (1, m1))]
            kws = [k_ref[krows, :].astype(_BF) for _, krows, _ in idx]
            vws = [v_ref[krows, :].astype(_BF) for _, krows, _ in idx]
            s = [_mm_nt(q_ref[idx[u][0], :] * m, kws[u]) for u, h, m in chains]
            s = [s_ * (HEAD_DIM ** -0.5) + _na_bias(b_ref, h, idx[u][2]) for s_, (u, h, m) in zip(s, chains)]
            e = [jnp.exp(s_ - jnp.max(s_, axis=-1, keepdims=True)) for s_ in s]
            pr = [e_ / jnp.sum(e_, axis=-1, keepdims=True) for e_ in e]
            ov = [_mm(pr_, vws[u]) * m for pr_, (u, h, m) in zip(pr, chains)]
            for u in range(NA_UNROLL):
                o_ref[idx[u][0], :] = ov[2 * u] + ov[2 * u + 1]
            return carry

        lax.fori_loop(0, npairs // NA_UNROLL, steps, 0)

    return pl.pallas_call(
        body, name="na_fwd", grid=(NA_WIDTH // LANE,),
        in_specs=[_strip(T, lambda j: j + 18), _strip(T, lambda j: j + 20), _strip(T, lambda j: j + 22),
                  pl.BlockSpec((1, 2, NA_TYPES, NA_CHUNKS, NA_Q, LANE), lambda j: (j, 0, 0, 0, 0, 0))],
        out_specs=_strip(T, lambda j: j, buffers=1),
        out_shape=jax.ShapeDtypeStruct((T, NA_WIDTH), F32),
        compiler_params=_cparams(("arbitrary",)),
    )(proj, proj, proj, btab)


def _na_bwd_call(proj, dycat, btab, dp):
    T = proj.shape[0]
    npairs, rows_n = T // NA_Q, T // GRID_W
    scale = HEAD_DIM ** -0.5

    def body(q_ref, k_ref, v_ref, do_ref, b_ref, dp_in_ref, dp_out_ref, db_ref, dka_ref, dva_ref, stage_ref, sems):
        dq_ref = stage_ref.at[0]
        lane = lax.broadcasted_iota(jnp.int32, (NA_Q, LANE), 1)
        m0 = (lane < HEAD_DIM).astype(F32)
        m1 = 1.0 - m0
        dka_ref[...] = jnp.zeros_like(dka_ref)
        dva_ref[...] = jnp.zeros_like(dva_ref)
        db_ref[...] = jnp.zeros_like(db_ref)

        def steps(i, carry):
            idx = [_na_step(NA_UNROLL * i + u, npairs, rows_n) for u in range(NA_UNROLL)]
            chains = [(u, h, m) for u in range(NA_UNROLL) for h, m in ((0, m0), (1, m1))]
            kws = [k_ref[krows, :].astype(_BF) for _, krows, _ in idx]
            vws = [v_ref[krows, :].astype(_BF) for _, krows, _ in idx]
            qm = [(q_ref[idx[u][0], :] * m).astype(_BF) for u, h, m in chains]
            dom = [(do_ref[idx[u][0], :] * m).astype(_BF) for u, h, m in chains]
            s = [_mm_nt(qm_, kws[u]) for qm_, (u, h, m) in zip(qm, chains)]
            dpr = [_mm_nt(dom_, vws[u]) for dom_, (u, h, m) in zip(dom, chains)]
            s = [s_ * scale + _na_bias(b_ref, h, idx[u][2]) for s_, (u, h, m) in zip(s, chains)]
            e = [jnp.exp(s_ - jnp.max(s_, axis=-1, keepdims=True)) for s_ in s]
            pr = [e_ / jnp.sum(e_, axis=-1, keepdims=True) for e_ in e]
            ds = [pr_ * (dpr_ - jnp.sum(pr_ * dpr_, axis=-1, keepdims=True)) for pr_, dpr_ in zip(pr, dpr)]
            dsb = [(ds_ * scale).astype(_BF) for ds_ in ds]
            dq = [_mm(dsb_, kws[u]) * m for dsb_, (u, h, m) in zip(dsb, chains)]
            dk = [_mm_tn(dsb_, qm_) for dsb_, qm_ in zip(dsb, qm)]
            dv = [_mm_tn(pr_, dom_) for pr_, dom_ in zip(pr, dom)]
            for ds_, (u, h, m) in zip(ds, chains):
                for c in range(NA_CHUNKS):
                    db_ref[0, h, idx[u][2], c] += ds_[:, c * LANE:(c + 1) * LANE]
            for u in range(NA_UNROLL):
                qrows, krows, _ = idx[u]
                dq_ref[qrows, :] = (dq[2 * u] + dq[2 * u + 1]).astype(dq_ref.dtype)
                dka_ref[krows, :] += dk[2 * u] + dk[2 * u + 1]
                dva_ref[krows, :] += dv[2 * u] + dv[2 * u + 1]
            return carry

        lax.fori_loop(0, npairs // NA_UNROLL, steps, 0)
        stage_ref[1] = dka_ref[...].astype(stage_ref.dtype)
        stage_ref[2] = dva_ref[...].astype(stage_ref.dtype)
        j = pl.program_id(0)
        _store_strips(stage_ref, dp_out_ref, (j + 18, j + 20, j + 22), sems)

    tab = pl.BlockSpec((1, 2, NA_TYPES, NA_CHUNKS, NA_Q, LANE), lambda j: (j, 0, 0, 0, 0, 0))
    return pl.pallas_call(
        body, name="na_bwd", grid=(NA_WIDTH // LANE,),
        in_specs=[_strip(T, lambda j: j + 18), _strip(T, lambda j: j + 20), _strip(T, lambda j: j + 22),
                  _strip(T, lambda j: j + 6), tab, pl.BlockSpec(memory_space=pl.ANY)],
        out_specs=[pl.BlockSpec(memory_space=pl.ANY), tab],
        out_shape=[jax.ShapeDtypeStruct(dp.shape, dp.dtype),
                   jax.ShapeDtypeStruct((2, 2, NA_TYPES, NA_CHUNKS, NA_Q, LANE), F32)],
        scratch_shapes=[pltpu.VMEM((T, LANE), F32), pltpu.VMEM((T, LANE), F32),
                        pltpu.VMEM((3, T, LANE), _BF), pltpu.SemaphoreType.DMA((3,))],
        input_output_aliases={5: 0},
        compiler_params=_cparams(("arbitrary",)),
    )(proj, proj, proj, dycat, btab, dp)


W_BLK = IN_WIDTH // N_DEV
MXU_W = 256
N_BLK = 3 * MXU_W
N_STEPS = IN_WIDTH // N_BLK
TM = 512


def _ln_fwd(z, g, b):
    zc = z - jnp.mean(z, axis=-1, keepdims=True)
    var = jnp.mean(zc * zc, axis=-1, keepdims=True)
    return zc * lax.rsqrt(var + LN_EPS) * g + b


def _ln_bwd(dy, z, g):
    zc = z - jnp.mean(z, axis=-1, keepdims=True)
    rstd = lax.rsqrt(jnp.mean(zc * zc, axis=-1, keepdims=True) + LN_EPS)
    xhat = zc * rstd
    dxh = dy * g
    dz = rstd * (dxh - jnp.mean(dxh, axis=-1, keepdims=True) - xhat * jnp.mean(dxh * xhat, axis=-1, keepdims=True))
    return dz, dy * xhat


def _row_tile(T):
    return 1024 if T % 1024 == 0 else TM


def _halves(n):
    return (pl.ds(0, n // 2), pl.ds(n // 2, n // 2))


def _inproj_call(xb, w, after):
    T = xb.shape[0]
    tm = _row_tile(T)

    def body(x_ref, w_ref, after_ref, o_ref):
        o_ref[...] = _mm(x_ref[...], w_ref[...])

    return pl.pallas_call(
        body, name="inproj", grid=(T // tm, N_STEPS),
        in_specs=[pl.BlockSpec((tm, D_MODEL), lambda i, n: (i, 0)),
                  pl.BlockSpec((D_MODEL, N_BLK), lambda i, n: (0, n)),
                  pl.BlockSpec(memory_space=pl.ANY)],
        out_specs=pl.BlockSpec((tm, N_BLK), lambda i, n: (i, n)),
        out_shape=jax.ShapeDtypeStruct((T, IN_WIDTH), F32),
        compiler_params=_cparams(("parallel", "arbitrary")),
    )(xb, w, after)


def _vec_spec():
    return pl.BlockSpec((1, D_MODEL), lambda *_: (0, 0))


def _outproj_ln_call(y_lru, y_ret, y_na, x, w, g, b, after):
    T = x.shape[0]

    def body(yl_ref, yr_ref, yn_ref, x_ref, w_ref, g_ref, b_ref, after_ref, z_ref, x1_ref, x1b_ref, yc_ref):
        yc_ref[:, 0:LRU_WIDTH] = yl_ref[...].astype(yc_ref.dtype)
        yc_ref[:, LRU_WIDTH:LRU_WIDTH + RET_WIDTH] = yr_ref[...].astype(yc_ref.dtype)
        yc_ref[:, LRU_WIDTH + RET_WIDTH:] = yn_ref[...].astype(yc_ref.dtype)
        z = ALPHA * x_ref[...] + _mm(yc_ref[...], w_ref[...])
        z_ref[...] = z
        x1 = _ln_fwd(z, g_ref[...], b_ref[...])
        x1_ref[...] = x1
        x1b_ref[...] = x1.astype(x1b_ref.dtype)

    row = lambda w_: pl.BlockSpec((TM, w_), lambda i: (i, 0))
    return pl.pallas_call(
        body, name="outproj_ln", grid=(T // TM,),
        in_specs=[row(LRU_WIDTH), row(RET_WIDTH), row(NA_WIDTH), row(D_MODEL),
                  pl.BlockSpec((D_MODEL, D_MODEL), lambda i: (0, 0)), _vec_spec(), _vec_spec(),
                  pl.BlockSpec(memory_space=pl.ANY)],
        out_specs=[row(D_MODEL)] * 4,
        out_shape=[jax.ShapeDtypeStruct((T, D_MODEL), F32), jax.ShapeDtypeStruct((T, D_MODEL), F32),
                   jax.ShapeDtypeStruct((T, D_MODEL), _BF), jax.ShapeDtypeStruct((T, D_MODEL), _BF)],
        compiler_params=_cparams(("parallel",)),
    )(y_lru, y_ret, y_na, x, w, g, b, after)


def _ffn_ln_call(x1, x1b, wg, wu, wd, g, b):
    T = x1.shape[0]

    def body(x_ref, xb_ref, wg_ref, wu_ref, wd_ref, g_ref, b_ref, z_ref, x2_ref, x2b_ref, gp_ref, up_ref, acc_ref):
        n = pl.program_id(1)

        @pl.when(n == 0)
        def _():
            acc_ref[...] = jnp.zeros_like(acc_ref)

        xb = xb_ref[...]
        gp = _mm(xb, wg_ref[...])
        up = _mm(xb, wu_ref[...])
        gp_ref[...] = gp.astype(gp_ref.dtype)
        up_ref[...] = up.astype(up_ref.dtype)
        acc_ref[...] += _mm(gp * _sigmoid(gp) * up, wd_ref[...])

        @pl.when(n == N_STEPS - 1)
        def _():
            z = ALPHA * x_ref[...] + acc_ref[...]
            z_ref[...] = z
            x2 = _ln_fwd(z, g_ref[...], b_ref[...])
            x2_ref[...] = x2
            x2b_ref[...] = x2.astype(x2b_ref.dtype)

    tm = _row_tile(T)
    row = pl.BlockSpec((tm, D_MODEL), lambda i, n: (i, 0))
    row1 = pl.BlockSpec((tm, D_MODEL), lambda i, n: (i, 0), pipeline_mode=pl.Buffered(1))
    return pl.pallas_call(
        body, name="ffn_ln", grid=(T // tm, N_STEPS),
        in_specs=[row1, row,
                  pl.BlockSpec((D_MODEL, N_BLK), lambda i, n: (0, n)),
                  pl.BlockSpec((D_MODEL, N_BLK), lambda i, n: (0, n)),
                  pl.BlockSpec((N_BLK, D_MODEL), lambda i, n: (n, 0)), _vec_spec(), _vec_spec()],
        out_specs=[row1] * 3 + [pl.BlockSpec((tm, N_BLK), lambda i, n: (i, n))] * 2,
        out_shape=[jax.ShapeDtypeStruct((T, D_MODEL), F32), jax.ShapeDtypeStruct((T, D_MODEL), F32),
                   jax.ShapeDtypeStruct((T, D_MODEL), _BF),
                   jax.ShapeDtypeStruct((T, IN_WIDTH), _BF), jax.ShapeDtypeStruct((T, IN_WIDTH), _BF)],
        scratch_shapes=[pltpu.VMEM((tm, D_MODEL), F32)],
        compiler_params=_cparams(("parallel", "arbitrary")),
    )(x1, x1b, wg, wu, wd, g, b)


def _loss_call(y, t):
    T = y.shape[0]

    def body(y_ref, t_ref, dy_ref, loss_ref):
        @pl.when(pl.program_id(0) == 0)
        def _():
            loss_ref[...] = jnp.zeros_like(loss_ref)

        err = y_ref[...] - t_ref[...]
        dy_ref[...] = err * (1.0 / D_MODEL)
        part = 0.5 * jnp.sum(jnp.mean(err * err, axis=-1, keepdims=True), axis=0, keepdims=True)
        loss_ref[...] += jnp.broadcast_to(part, loss_ref.shape)

    row = pl.BlockSpec((TM, D_MODEL), lambda i: (i, 0))
    return pl.pallas_call(
        body, name="loss", grid=(T // TM,),
        in_specs=[row, row],
        out_specs=[row, pl.BlockSpec((SUB, LANE), lambda i: (0, 0))],
        out_shape=[jax.ShapeDtypeStruct((T, D_MODEL), F32), jax.ShapeDtypeStruct((SUB, LANE), F32)],
        compiler_params=_cparams(("arbitrary",)),
    )(y, t)


def _ffn_bwd_call(dx2, z2, gpb, upb, wg, wu, wd, g, after):
    T = dx2.shape[0]

    def body(dx2_ref, z_ref, gp_ref, up_ref, wg_ref, wu_ref, wd_ref, g_ref, after_ref,
             dx1_ref, dgp_ref, dup_ref, hid_ref, dzb_ref, dln_ref, acc_ref):
        i, n = pl.program_id(0), pl.program_id(1)

        @pl.when((i == 0) & (n == 0))
        def _():
            dln_ref[...] = jnp.zeros_like(dln_ref)

        @pl.when(n == 0)
        def _():
            dy = dx2_ref[...]
            dz, dg_rows = _ln_bwd(dy, z_ref[...], g_ref[...])
            dzb_ref[...] = dz.astype(dzb_ref.dtype)
            acc_ref[...] = ALPHA * dz
            dln_ref[0:1, :] += jnp.sum(dg_rows, axis=0, keepdims=True)
            dln_ref[1:2, :] += jnp.sum(dy, axis=0, keepdims=True)

        for rows in _halves(TM):
            gp = gp_ref[rows, :].astype(F32)
            up = up_ref[rows, :].astype(F32)
            sg = _sigmoid(gp)
            act = gp * sg
            hid_ref[rows, :] = (act * up).astype(hid_ref.dtype)
            dhid = _mm_nt(dzb_ref[rows, :], wd_ref[...])
            dup = dhid * act
            dgp = dhid * up * (sg * (1.0 + gp * (1.0 - sg)))
            dgp_ref[rows, :] = dgp.astype(dgp_ref.dtype)
            dup_ref[rows, :] = dup.astype(dup_ref.dtype)
            acc_ref[rows, :] += _mm_nt(dgp, wg_ref[...]) + _mm_nt(dup, wu_ref[...])

        @pl.when(n == N_STEPS - 1)
        def _():
            dx1_ref[...] = acc_ref[...]

    row = pl.BlockSpec((TM, D_MODEL), lambda i, n: (i, 0))
    blk = pl.BlockSpec((TM, N_BLK), lambda i, n: (i, n))
    return pl.pallas_call(
        body, name="ffn_bwd", grid=(T // TM, N_STEPS),
        in_specs=[row, row, blk, blk,
                  pl.BlockSpec((D_MODEL, N_BLK), lambda i, n: (0, n)),
                  pl.BlockSpec((D_MODEL, N_BLK), lambda i, n: (0, n)),
                  pl.BlockSpec((N_BLK, D_MODEL), lambda i, n: (n, 0)), _vec_spec(),
                  pl.BlockSpec(memory_space=pl.ANY)],
        out_specs=[row, blk, blk, blk, row, pl.BlockSpec((SUB, D_MODEL), lambda i, n: (0, 0))],
        out_shape=[jax.ShapeDtypeStruct((T, D_MODEL), F32),
                   jax.ShapeDtypeStruct((T, IN_WIDTH), _BF), jax.ShapeDtypeStruct((T, IN_WIDTH), _BF),
                   jax.ShapeDtypeStruct((T, IN_WIDTH), _BF), jax.ShapeDtypeStruct((T, D_MODEL), _BF),
                   jax.ShapeDtypeStruct((SUB, D_MODEL), F32)],
        scratch_shapes=[pltpu.VMEM((TM, D_MODEL), F32)],
        compiler_params=_cparams(("arbitrary", "arbitrary")),
    )(dx2, z2, gpb, upb, wg, wu, wd, g, after)


def _outproj_bwd_call(dx1, z1, w, g):
    T = dx1.shape[0]

    def body(dx_ref, z_ref, w_ref, g_ref, dzb_ref, dyc_ref, dres_ref, dln_ref):
        @pl.when(pl.program_id(0) == 0)
        def _():
            dln_ref[...] = jnp.zeros_like(dln_ref)

        dy = dx_ref[...]
        dz, dg_rows = _ln_bwd(dy, z_ref[...], g_ref[...])
        dzb_ref[...] = dz.astype(dzb_ref.dtype)
        dres_ref[...] = ALPHA * dz
        dyc_ref[...] = _mm_nt(dz, w_ref[...])
        dln_ref[0:1, :] += jnp.sum(dg_rows, axis=0, keepdims=True)
        dln_ref[1:2, :] += jnp.sum(dy, axis=0, keepdims=True)

    row = pl.BlockSpec((TM, D_MODEL), lambda i: (i, 0))
    return pl.pallas_call(
        body, name="outproj_bwd", grid=(T // TM,),
        in_specs=[row, row, pl.BlockSpec((D_MODEL, D_MODEL), lambda i: (0, 0)), _vec_spec()],
        out_specs=[row, row, row, pl.BlockSpec((SUB, D_MODEL), lambda i: (0, 0))],
        out_shape=[jax.ShapeDtypeStruct((T, D_MODEL), _BF), jax.ShapeDtypeStruct((T, D_MODEL), F32),
                   jax.ShapeDtypeStruct((T, D_MODEL), F32), jax.ShapeDtypeStruct((SUB, D_MODEL), F32)],
        compiler_params=_cparams(("arbitrary",)),
    )(dx1, z1, w, g)


def _inproj_bwd_call(dres, dp, w):
    T = dres.shape[0]

    def body(dres_ref, dp_ref, w_ref, dx_ref):
        dx_ref[...] = dres_ref[...] + _mm_nt(dp_ref[...], w_ref[...])

    row = pl.BlockSpec((TM, D_MODEL), lambda i: (i, 0))
    return pl.pallas_call(
        body, name="inproj_bwd", grid=(T // TM,),
        in_specs=[row, pl.BlockSpec((TM, IN_WIDTH), lambda i: (i, 0)),
                  pl.BlockSpec((D_MODEL, IN_WIDTH), lambda i: (0, 0), pipeline_mode=pl.Buffered(1))],
        out_specs=row,
        out_shape=jax.ShapeDtypeStruct((T, D_MODEL), F32),
        compiler_params=_cparams(("parallel",)),
    )(dres, dp, w)


def _tn_cols_call(a, b, name):
    T, ka = a.shape
    n = b.shape[1]

    def body(a_ref, b_ref, o_ref):
        o_ref[...] = _mm_tn(a_ref[...], b_ref[...]).astype(o_ref.dtype)

    return pl.pallas_call(
        body, name=name, grid=(n // N_BLK,),
        in_specs=[pl.BlockSpec((T, ka), lambda j: (0, 0), pipeline_mode=pl.Buffered(1)),
                  pl.BlockSpec((T, N_BLK), lambda j: (0, j))],
        out_specs=pl.BlockSpec((ka, N_BLK), lambda j: (0, j)),
        out_shape=jax.ShapeDtypeStruct((ka, n), _BF),
        compiler_params=_cparams(("parallel",)),
    )(a, b)


def _tn_rows_call(a, b, kb, name):
    T, ka = a.shape
    n = b.shape[1]

    def body(a_ref, b_ref, o_ref):
        o_ref[...] = _mm_tn(a_ref[...], b_ref[...]).astype(o_ref.dtype)

    return pl.pallas_call(
        body, name=name, grid=(ka // kb,),
        in_specs=[pl.BlockSpec((T, kb), lambda r: (0, r)),
                  pl.BlockSpec((T, n), lambda r: (0, 0), pipeline_mode=pl.Buffered(1))],
        out_specs=pl.BlockSpec((kb, n), lambda r: (r, 0)),
        out_shape=jax.ShapeDtypeStruct((ka, n), _BF),
        compiler_params=_cparams(("parallel",)),
    )(a, b)


def _me():
    return lax.axis_index("x"), lax.axis_index("y"), lax.axis_index("c")


def _flip(k):
    x, y, c = _me()
    return (1 - x if k & 4 else x, 1 - y if k & 2 else y, 1 - c if k & 1 else c)


def _dev_index(pos):
    return 4 * pos[0] + 2 * pos[1] + pos[2]


_HBM = pl.BlockSpec(memory_space=pltpu.HBM)
_SEM = pl.BlockSpec(memory_space=pltpu.SEMAPHORE)


def _land_shape(shape, mode):
    if mode == "all":
        return (N_DEV,) + shape
    if mode == "cols":
        return (shape[0], N_DEV * shape[1])
    if mode == "blk":
        return shape
    assert mode == "scols"
    return (N_DEV, shape[0], shape[1] // N_DEV)


def _comm_copies(ins, lands, modes, send_sems, recv_sems):
    me = _dev_index(_me())
    copies = []
    for k in range(N_DEV):
        peer = _flip(k)
        pidx = _dev_index(peer)
        for a, (src, land, mode) in enumerate(zip(ins, lands, modes)):
            if mode == "blk":
                src = src.at[pidx]
            elif mode == "scols":
                w = src.shape[1] // N_DEV
                src = src.at[:, pl.ds(pl.multiple_of(pidx * w, LANE), w)]
            if mode == "cols":
                w = src.shape[1]
                dst = land.at[:, pl.ds(pl.multiple_of(me * w, LANE), w)]
            else:
                dst = land.at[me]
            copies.append(pltpu.make_async_remote_copy(
                src_ref=src, dst_ref=dst, send_sem=send_sems.at[k * len(ins) + a], recv_sem=recv_sems.at[k * len(ins) + a],
                device_id=peer, device_id_type=MESH))
    return copies


def _comm_start_call(arrs, gather_flags, after, name):
    n = len(arrs)
    lands = [lax.empty(_land_shape(v.shape, mode), v.dtype) for v, mode in zip(arrs, gather_flags)]

    def body(*refs):
        ins, lnd = refs[:n], refs[n:2 * n]
        send_sems, recv_sems = refs[2 * n + len(after)], refs[2 * n + len(after) + 1]
        for cp in _comm_copies(ins, lnd, gather_flags, send_sems, recv_sems):
            cp.start()
        refs[-1][...] = jnp.zeros_like(refs[-1])

    hbm = [pltpu.with_memory_space_constraint(v, pltpu.HBM) for v in list(arrs) + lands]
    out = pl.pallas_call(
        body, name=name,
        out_shape=(pltpu.SemaphoreType.DMA((N_DEV * n,)), pltpu.SemaphoreType.DMA((N_DEV * n,)),
                   *[pltpu.HBM(v.shape, v.dtype) for v in hbm], jax.ShapeDtypeStruct((SUB, LANE), F32)),
        in_specs=[_HBM] * (2 * n) + [pl.BlockSpec(memory_space=pl.ANY)] * len(after),
        out_specs=(_SEM, _SEM, *[_HBM] * (2 * n), pl.BlockSpec(memory_space=pltpu.VMEM)),
        input_output_aliases={i: 2 + i for i in range(2 * n)},
        compiler_params=pltpu.CompilerParams(has_side_effects=pltpu.SideEffectType.DATAFLOW_SIDE_EFFECTING),
    )(*hbm, *after)
    return out[:-1], out[-1]


def _comm_wait_call(state, gather_flags, after, name):
    n = len(gather_flags)
    send_sems, recv_sems, thru = state[0], state[1], state[2:]

    def body(*refs):
        ins, lnd, ssem, rsem = refs[:n], refs[n:2 * n], refs[2 * n], refs[2 * n + 1]
        for cp in _comm_copies(ins, lnd, gather_flags, ssem, rsem):
            cp.wait_send()
            cp.wait_recv()

    out = pl.pallas_call(
        body, name=name,
        out_shape=tuple(pltpu.HBM(v.shape, v.dtype) for v in thru),
        in_specs=[_HBM] * (2 * n) + [_SEM, _SEM] + [pl.BlockSpec(memory_space=pl.ANY)] * len(after),
        out_specs=tuple([_HBM] * (2 * n)),
        input_output_aliases={i: i for i in range(2 * n)},
        compiler_params=pltpu.CompilerParams(has_side_effects=pltpu.SideEffectType.DATAFLOW_SIDE_EFFECTING),
    )(*thru, send_sems, recv_sems, *after)
    return out[n:]


def _sum8_call(recv, rows, name):
    _, r, c = recv.shape

    def body(x_ref, o_ref):
        acc = x_ref[0].astype(F32)
        for s in range(1, N_DEV):
            acc = acc + x_ref[s].astype(F32)
        o_ref[...] = acc

    return pl.pallas_call(
        body, name=name, grid=(r // rows,),
        in_specs=[pl.BlockSpec((N_DEV, rows, c), lambda i: (0, i, 0))],
        out_specs=pl.BlockSpec((rows, c), lambda i: (i, 0)),
        out_shape=jax.ShapeDtypeStruct((r, c), F32),
        compiler_params=_cparams(("parallel",)),
    )(recv)


def _adamw_call(w, g, m, v, rows, name):
    r, c = w.shape

    def body(w_ref, g_ref, m_ref, v_ref, d_ref, nm_ref, nv_ref):
        gr = g_ref[...]
        nm = ADAM_B1 * m_ref[...] + (1.0 - ADAM_B1) * gr
        nv = ADAM_B2 * v_ref[...] + (1.0 - ADAM_B2) * (gr * gr)
        m_hat = nm / (1.0 - ADAM_B1 ** ADAM_STEP)
        v_hat = nv / (1.0 - ADAM_B2 ** ADAM_STEP)
        d_ref[...] = -ADAM_LR * (m_hat / (jnp.sqrt(v_hat) + ADAM_EPS) + ADAM_WD * w_ref[...])
        nm_ref[...] = nm
        nv_ref[...] = nv

    spec = pl.BlockSpec((rows, c), lambda i: (i, 0))
    return pl.pallas_call(
        body, name=name, grid=(r // rows,),
        in_specs=[spec] * 4, out_specs=[spec] * 3,
        out_shape=[jax.ShapeDtypeStruct((r, c), F32)] * 3,
        compiler_params=_cparams(("parallel",)),
    )(w, g, m, v)


SH_ROWS = 16
SH_W = LRU_WIDTH // N_DEV
REP_ROWS = 824
_REP_SIZES = (LRU_WIDTH, 2 * 6 * 64 * 64, 2 * 6 * 64 * 64, RET_WIDTH, 1920, D_MODEL, D_MODEL, D_MODEL, D_MODEL)
_RPB_SIZE = NA_HEADS * (2 * NA_KH - 1) * (2 * NA_KW - 1)


def _pack_sh(cw, ba, bx, lam):
    return jnp.concatenate([cw, ba, bx, lam], axis=0)


def _pad_sh(p):
    pad = [(0, 0)] * (p.ndim - 2) + [(0, SH_ROWS - p.shape[-2]), (0, LANE - p.shape[-1])]
    return jnp.pad(p, pad)


def _pack_rep(cb, wa, wx, gnw, rpb, l1g, l1b, l2g, l2b):
    flat = jnp.concatenate([cb.reshape(-1), wa.reshape(-1), wx.reshape(-1), gnw.reshape(-1),
                            jnp.pad(rpb.reshape(-1), (0, 1920 - _RPB_SIZE)), l1g, l1b, l2g, l2b,
                            jnp.zeros((REP_ROWS * LANE - sum(_REP_SIZES),), F32)])
    return flat.reshape(REP_ROWS, LANE)


def _unpack_rep(p):
    nl = p.shape[0]
    flat = p.reshape(nl, -1)
    out, off = [], 0
    for size in _REP_SIZES:
        out.append(flat[:, off:off + size])
        off += size
    cb, wa, wx, gnw, rpb, l1g, l1b, l2g, l2b = out
    return (cb, wa.reshape(nl, 2, 6, 64, 64), wx.reshape(nl, 2, 6, 64, 64), gnw,
            rpb[:, :_RPB_SIZE].reshape(nl, NA_HEADS, 2 * NA_KH - 1, 2 * NA_KW - 1), l1g, l1b, l2g, l2b)


def _adamw_nd(w, g, m, v, rows, name):
    shp = w.shape
    f = lambda t: t.reshape(-1, shp[-1])
    rows = f(w).shape[0] if rows is None else rows
    return [t.reshape(shp) for t in _adamw_call(f(w), f(g), f(m), f(v), rows, name)]


def kernel(x, w_in, conv_w, conv_b, lru_w_a, lru_b_a, lru_w_x, lru_b_x, lru_lam, ret_gn_w, na_rpb, w_out, ln1_g, ln1_b, w_gate, w_up, w_down, ln2_g, ln2_b, loss_target, m_w_in, m_conv_w, m_conv_b, m_lru_w_a, m_lru_b_a, m_lru_w_x, m_lru_b_x, m_lru_lam, m_ret_gn_w, m_na_rpb, m_w_out, m_ln1_g, m_ln1_b, m_w_gate, m_w_up, m_w_down, m_ln2_g, m_ln2_b, v_w_in, v_conv_w, v_conv_b, v_lru_w_a, v_lru_b_a, v_lru_w_x, v_lru_b_x, v_lru_lam, v_ret_gn_w, v_na_rpb, v_w_out, v_ln1_g, v_ln1_b, v_w_gate, v_w_up, v_w_down, v_ln2_g, v_ln2_b):
    nl = w_in.shape[0]
    T = x.shape[1]
    rows_n = T // GRID_W
    x0, target = x[0], loss_target[0]
    ffpad = W_BLK - FF_BLK

    win_b = w_in.astype(_BF)
    wg_b = jnp.pad(w_gate, ((0, 0), (0, 0), (0, ffpad))).astype(_BF)
    wu_b = jnp.pad(w_up, ((0, 0), (0, 0), (0, ffpad))).astype(_BF)
    wd_b = jnp.pad(w_down, ((0, 0), (0, ffpad), (0, 0))).astype(_BF)
    wout_b = w_out.astype(_BF)
    agf_modes = ["cols", "all"]
    agk_modes = ["cols", "cols", "all", "all"]

    def agf_start(l, after):
        sh = _pad_sh(_pack_sh(conv_w[l], lru_b_a[l], lru_b_x[l], lru_lam[l]))
        return _comm_start_call([win_b[l], sh], agf_modes, after, f"agf_start{l}")

    def agk_start(l, after):
        return _comm_start_call([wg_b[l], wu_b[l], wd_b[l], wout_b[l]], agk_modes, after, f"agk_start{l}")

    tables = _ret_tables(T)
    w4_all = _lru_w4(lru_w_a, lru_w_x)
    layers = []
    gathered = []
    xs, xb = x0, x0.astype(_BF)
    agf_state, token = agf_start(0, [])
    for l in range(nl):
        win, shg = _comm_wait_call(agf_state, agf_modes, [xb], f"agf_wait{l}")
        agk_state, token = agk_start(l, [shg])
        full = shg[:, :10, :SH_W].transpose(1, 0, 2).reshape(10, LRU_WIDTH)
        vec, w4 = _lru_vec(full[0:4], conv_b[l], full[4:6], full[6:8], full[8:10]), w4_all[l]
        gnw8 = jnp.pad(ret_gn_w[l][None], ((0, SUB - 1), (0, 0)))
        btab = _na_bias_tables(na_rpb[l], rows_n)
        proj = _inproj_call(xb, win, token)
        y_lru = _lru_fwd_call(proj, vec, w4)
        y_ret = _ret_fwd_call(proj, tables, gnw8)
        y_na = _na_fwd_call(proj, btab)
        wg, wu, wd, wout = _comm_wait_call(agk_state, agk_modes, [y_na], f"agk_wait{l}")
        wd, wout = wd.reshape(IN_WIDTH, D_MODEL), wout.reshape(D_MODEL, D_MODEL)
        gathered.append((win, wg, wu, wd, wout))
        if l + 1 < nl:
            agf_state, token = agf_start(l + 1, [wout])
        z1, x1, x1b, ycb = _outproj_ln_call(y_lru, y_ret, y_na, xs, wout, ln1_g[l][None], ln1_b[l][None], token)
        z2, x2, x2b, gpb, upb = _ffn_ln_call(x1, x1b, wg, wu, wd, ln2_g[l][None], ln2_b[l][None])
        layers.append(dict(xb=xb, proj=proj, vec=vec, w4=w4, gnw8=gnw8, btab=btab,
                           z1=z1, x1b=x1b, ycb=ycb, z2=z2, gpb=gpb, upb=upb))
        xs, xb = x2, x2b

    dx, loss_blk = _loss_call(xs, target)
    loss = lax.psum(loss_blk[0, 0], ("x", "y", "c"))

    gxa_flags = ["scols", "scols", "blk", "blk"]
    gxb_flags = ["scols", "blk", "all"]
    gxa_state, gxb_state = [None] * nl, [None] * nl
    token = loss_blk
    for l in reversed(range(nl)):
        s = layers[l]
        win, wg, wu, wd, wout = gathered[l]
        dx1, dgp, dup, hid, dz2b, dln2 = _ffn_bwd_call(dx, s["z2"], s["gpb"], s["upb"], wg, wu, wd, ln2_g[l][None], token)
        dwg = _tn_cols_call(s["x1b"], dgp, "tn_cols")
        dwu = _tn_cols_call(s["x1b"], dup, "tn_cols")
        dwd = _tn_rows_call(hid, dz2b, N_BLK, "tn_rows_down").reshape(N_DEV, W_BLK, D_MODEL)
        dz1b, dyc, dres, dln1 = _outproj_bwd_call(dx1, s["z1"], wout, ln1_g[l][None])
        dwout = _tn_rows_call(s["ycb"], dz1b, D_MODEL // 2, "tn_rows_out").reshape(N_DEV, LANE, D_MODEL)
        gxa_state[l], token = _comm_start_call([dwg, dwu, dwd, dwout], gxa_flags, [], f"gxa_start{l}")
        dp, dvec, dw4 = _lru_bwd_call(s["proj"], dyc, s["vec"], s["w4"], token)
        dp, dgnw = _ret_bwd_call(s["proj"], dyc, tables, s["gnw8"], dp)
        dp, dbias = _na_bwd_call(s["proj"], dyc, s["btab"], dp)
        dwin = _tn_cols_call(s["xb"], dp, "tn_cols")
        dx = _inproj_bwd_call(dres, dp, win)
        dcw, dcb, dwa, dba, dwx, dbx, dlam = _lru_unpack(dvec, dw4)
        rep = _pack_rep(dcb, dwa, dwx, dgnw[0], _na_bias_grad(dbias, rows_n), dln1[0], dln1[1], dln2[0], dln2[1])
        sh = _pack_sh(dcw, dba, dbx, dlam).reshape(10, N_DEV, SH_W).transpose(1, 0, 2)
        gxb_state[l], token = _comm_start_call([dwin, _pad_sh(sh), rep], gxb_flags, [], f"gxb_start{l}")

    g_big = [[None] * nl for _ in range(5)]
    g_sh = [None] * nl
    g_rep = [None] * nl
    after = [dx, token]
    for l in reversed(range(nl)):
        ra = _comm_wait_call(gxa_state[l], gxa_flags, after, f"gxa_wait{l}")
        g_big[1][l] = _sum8_call(ra[0], TM, "sum8_cols")[:, :FF_BLK]
        g_big[2][l] = _sum8_call(ra[1], TM, "sum8_cols")[:, :FF_BLK]
        g_big[3][l] = _sum8_call(ra[2], W_BLK, "sum8_down")[:FF_BLK]
        g_big[4][l] = _sum8_call(ra[3], LANE, "sum8_out")
        rb = _comm_wait_call(gxb_state[l], gxb_flags, [g_big[4][l]], f"gxb_wait{l}")
        g_big[0][l] = _sum8_call(rb[0], TM, "sum8_cols")
        g_sh[l] = _sum8_call(rb[1], SH_ROWS, "sum8_sh")
        g_rep[l] = _sum8_call(rb[2], REP_ROWS, "sum8_rep")
        after = [g_rep[l]]

    g_w_in, g_w_gate, g_w_up, g_w_down, g_w_out = [jnp.stack(t) for t in g_big]
    big = {
        "w_in": _adamw_nd(w_in, g_w_in, m_w_in, v_w_in, TM, "adamw_in"),
        "w_gate": _adamw_nd(w_gate, g_w_gate, m_w_gate, v_w_gate, TM, "adamw_ff"),
        "w_up": _adamw_nd(w_up, g_w_up, m_w_up, v_w_up, TM, "adamw_ff"),
        "w_down": _adamw_nd(w_down, g_w_down, m_w_down, v_w_down, FF_BLK, "adamw_down"),
        "w_out": _adamw_nd(w_out, g_w_out, m_w_out, v_w_out, LANE, "adamw_out"),
    }
    g_shp = jnp.stack(g_sh)[:, :, :SH_W]
    rep_names = ("conv_b", "lru_w_a", "lru_w_x", "ret_gn_w", "na_rpb", "ln1_g", "ln1_b", "ln2_g", "ln2_b")
    grads = {"w_in": g_w_in, "w_gate": g_w_gate, "w_up": g_w_up, "w_down": g_w_down, "w_out": g_w_out,
             "conv_w": g_shp[:, 0:4], "lru_b_a": g_shp[:, 4:6], "lru_b_x": g_shp[:, 6:8], "lru_lam": g_shp[:, 8:10]}
    grads.update(dict(zip(rep_names, _unpack_rep(jnp.stack(g_rep)))))
    small = {
        "conv_w": (conv_w, m_conv_w, v_conv_w), "conv_b": (conv_b, m_conv_b, v_conv_b),
        "lru_w_a": (lru_w_a, m_lru_w_a, v_lru_w_a), "lru_b_a": (lru_b_a, m_lru_b_a, v_lru_b_a),
        "lru_w_x": (lru_w_x, m_lru_w_x, v_lru_w_x), "lru_b_x": (lru_b_x, m_lru_b_x, v_lru_b_x),
        "lru_lam": (lru_lam, m_lru_lam, v_lru_lam), "ret_gn_w": (ret_gn_w, m_ret_gn_w, v_ret_gn_w),
        "na_rpb": (na_rpb, m_na_rpb, v_na_rpb), "ln1_g": (ln1_g, m_ln1_g, v_ln1_g), "ln1_b": (ln1_b, m_ln1_b, v_ln1_b),
        "ln2_g": (ln2_g, m_ln2_g, v_ln2_g), "ln2_b": (ln2_b, m_ln2_b, v_ln2_b),
    }
    for name, (w_, m_, v_) in small.items():
        big[name] = _adamw_nd(w_, grads[name], m_, v_, None, "adamw_small")
    kinds = [{n: big[n][k] for n in big} for k in range(3)]
    order = ("w_in", "conv_w", "conv_b", "lru_w_a", "lru_b_a", "lru_w_x", "lru_b_x", "lru_lam", "ret_gn_w", "na_rpb",
             "w_out", "ln1_g", "ln1_b", "w_gate", "w_up", "w_down", "ln2_g", "ln2_b")
    outs = [loss, dx[None]]
    for d in (grads, *kinds):
        outs.extend(d[n] for n in order)
    return tuple(outs)
```

```python
import functools
import math

import numpy as np
import jax
import jax.numpy as jnp
from jax import lax
from jax.experimental import pallas as pl
from jax.experimental.pallas import tpu as pltpu

F32 = jnp.float32
_BF = jnp.bfloat16

D_MODEL = 1024
DEPTH = 4
GRID_W = 64
HEAD_DIM = 64
LRU_WIDTH = 384
RET_WIDTH = 384
RET_HEADS = 6
NA_WIDTH = 256
NA_HEADS = 4
IN_WIDTH = 3072
CONV_WIDTH = 4
LRU_C = 8.0
RET_CHUNK = 128
ROPE_BASE = 10000.0
GN_EPS = 1e-6
NA_KH = 8
NA_KW = 16
D_FF = 2816
FF_BLK = 352
N_DEV = 8
ALPHA = (2 * DEPTH) ** 0.25
LN_EPS = 1e-5
ADAM_LR = 0.001
ADAM_B1 = 0.9
ADAM_B2 = 0.999
ADAM_EPS = 1e-08
ADAM_WD = 0.01
ADAM_STEP = 10

LANE = 128
SUB = 8
VMEM_MB = 56
NEG = -1e30

MESH = pl.DeviceIdType.MESH


def _cparams(sem=None, vmem_mb=VMEM_MB):
    return pltpu.CompilerParams(dimension_semantics=sem, vmem_limit_bytes=vmem_mb << 20)


def _mm(a, b):
    return jnp.dot(a.astype(_BF), b.astype(_BF), preferred_element_type=F32)


def _mm_nt(a, b):
    return lax.dot_general(a.astype(_BF), b.astype(_BF), (((1,), (1,)), ((), ())), preferred_element_type=F32)


def _mm_tn(a, b):
    return lax.dot_general(a.astype(_BF), b.astype(_BF), (((0,), (0,)), ((), ())), preferred_element_type=F32)


def _sigmoid(x):
    return jax.nn.sigmoid(x)


def _rows(start, size):
    return pl.ds(pl.multiple_of(start, SUB), size)


def _loop2(n, body, init):
    assert n % 2 == 0
    return lax.fori_loop(0, n // 2, lambda i, c: body(2 * i + 1, body(2 * i, c)), init)


def _strip(T, col, buffers=2):
    return pl.BlockSpec((T, LANE), lambda j: (0, col(j)), pipeline_mode=pl.Buffered(buffers))


LRU_CH = 256
_GELU_C0 = math.sqrt(2.0 / math.pi)
_GELU_C1 = 0.044715


def _gelu_parts(x):
    x2 = x * x
    t = jnp.tanh(_GELU_C0 * (x + _GELU_C1 * x * x2))
    val = 0.5 * x * (1.0 + t)
    der = 0.5 * (1.0 + t) + 0.5 * x * (1.0 - t * t) * _GELU_C0 * (1.0 + 3.0 * _GELU_C1 * x2)
    return val, der


def _softplus_neg(lam):
    e = jnp.exp(-jnp.abs(lam))
    w = 1.0 + e
    l1p = jnp.where(w == 1.0, e, jnp.log(w) * (e / jnp.where(w == 1.0, 1.0, w - 1.0)))
    return jnp.maximum(-lam, 0.0) + l1p


def _window(ref, t0, ch, T):
    prev = ref[_rows(jnp.maximum(t0 - SUB, 0), SUB), :].astype(F32)
    nxt = ref[_rows(jnp.minimum(t0 + ch, T - SUB), SUB), :].astype(F32)
    prev = jnp.where(t0 > 0, prev, 0.0)
    nxt = jnp.where(t0 + ch < T, nxt, 0.0)
    return jnp.concatenate([prev, ref[_rows(t0, ch), :].astype(F32), nxt], axis=0)


def _tap(win, shift, ch):
    n = win.shape[0]
    return pltpu.roll(win, (-shift) % n, 0)[SUB:SUB + ch]


def _lru_conv(xb_ref, vec, t0, T):
    win = _window(xb_ref, t0, LRU_CH, T)
    xc = jnp.broadcast_to(vec[4:5, :], (LRU_CH, LANE))
    for j in range(CONV_WIDTH):
        xc = xc + _tap(win, j - CONV_WIDTH // 2, LRU_CH) * vec[j:j + 1, :]
    return xc


def _lru_dir(pre_a, pre_x, sp):
    r = _sigmoid(pre_a)
    i = _sigmoid(pre_x)
    log_a = (-LRU_C) * r * sp
    a = jnp.exp(log_a)
    z = jnp.tanh(-log_a) * (a * a + 1.0)
    s = jnp.sqrt(z)
    return r, i, a, s


def _scan_tile(a, b, reverse, row):
    for k in (1, 2, 4):
        if not reverse:
            a_s, b_s, m = pltpu.roll(a, k, 0), pltpu.roll(b, k, 0), row >= k
        else:
            a_s, b_s, m = pltpu.roll(a, SUB - k, 0), pltpu.roll(b, SUB - k, 0), row < SUB - k
        b = jnp.where(m, a * b_s + b, b)
        a = jnp.where(m, a * a_s, a)
    return a, b


def _bcast_row(x, r):
    return jnp.broadcast_to(x[r:r + 1, :], (SUB, LANE))


def _lru_prepare(xb_ref, w4_ref, vec, xc_ref, af_ref, uf_ref, ab_ref, ub_ref, T):
    sp_f = _softplus_neg(vec[9:10, :])
    sp_b = _softplus_neg(vec[10:11, :])
    w4 = w4_ref[0]

    def body(c, carry):
        t0 = c * LRU_CH
        xc = _lru_conv(xb_ref, vec, t0, T)
        if xc_ref is not None:
            xc_ref[_rows(t0, LRU_CH), :] = xc
        pre = _mm(xc, w4)
        _, i, a, s = _lru_dir(pre[:, 0:128] + vec[5:6, :], pre[:, 128:256] + vec[6:7, :], sp_f)
        af_ref[_rows(t0, LRU_CH), :] = a
        uf_ref[_rows(t0, LRU_CH), :] = s * (i * xc)
        _, i, a, s = _lru_dir(pre[:, 256:384] + vec[7:8, :], pre[:, 384:512] + vec[8:9, :], sp_b)
        ab_ref[_rows(t0, LRU_CH), :] = a
        ub_ref[_rows(t0, LRU_CH), :] = s * (i * xc)
        return carry

    lax.fori_loop(0, T // LRU_CH, body, 0)


def _lru_scan(af_ref, uf_ref, ab_ref, ub_ref, T):
    nt = T // SUB
    row = lax.broadcasted_iota(jnp.int32, (SUB, LANE), 0)

    def body(j, carry):
        hf, hb = carry
        sf = _rows(j * SUB, SUB)
        sb = _rows((nt - 1 - j) * SUB, SUB)
        a, b = _scan_tile(af_ref[sf, :], uf_ref[sf, :], False, row)
        h = a * hf + b
        uf_ref[sf, :] = h
        hf = _bcast_row(h, SUB - 1)
        a, b = _scan_tile(ab_ref[sb, :], ub_ref[sb, :], True, row)
        h = a * hb + b
        ub_ref[sb, :] = h
        hb = _bcast_row(h, 0)
        return hf, hb

    z = jnp.zeros((SUB, LANE), F32)
    lax.fori_loop(0, nt, body, (z, z))


def _lru_fwd_call(proj, vec, w4):
    T = proj.shape[0]

    def body(xb_ref, gate_ref, vec_ref, w4_ref, y_ref, af_ref, uf_ref, ab_ref, ub_ref):
        vec = vec_ref[...]
        _lru_prepare(xb_ref, w4_ref, vec, None, af_ref, uf_ref, ab_ref, ub_ref, T)
        _lru_scan(af_ref, uf_ref, ab_ref, ub_ref, T)

        def out(c, carry):
            rows = _rows(c * LRU_CH, LRU_CH)
            gl, _ = _gelu_parts(gate_ref[rows, :])
            y_ref[rows, :] = (uf_ref[rows, :] + ub_ref[rows, :]) * gl
            return carry

        lax.fori_loop(0, T // LRU_CH, out, 0)

    return pl.pallas_call(
        body, name="lru_fwd", grid=(LRU_WIDTH // LANE,),
        in_specs=[_strip(T, lambda j: j), _strip(T, lambda j: j + 3),
                  pl.BlockSpec((16, LANE), lambda j: (0, j)),
                  pl.BlockSpec((1, LANE, 4 * LANE), lambda j: (j, 0, 0))],
        out_specs=_strip(T, lambda j: j, buffers=1),
        out_shape=jax.ShapeDtypeStruct((T, LRU_WIDTH), F32),
        scratch_shapes=[pltpu.VMEM((T, LANE), F32)] * 4,
        compiler_params=_cparams(("arbitrary",)),
    )(proj, proj, vec, w4)


def _store_strips(stage_ref, dp_ref, cols, sems):
    copies = [pltpu.make_async_copy(stage_ref.at[b], dp_ref.at[:, pl.ds(pl.multiple_of(c * LANE, LANE), LANE)], sems.at[b])
              for b, c in enumerate(cols)]
    for cp in copies:
        cp.start()
    for cp in copies:
        cp.wait()


def _lru_bwd_call(proj, dycat, vec, w4, after):
    T = proj.shape[0]
    nt = T // SUB
    nch = T // LRU_CH

    def body(xb_ref, gate_ref, dy_ref, vec_ref, w4_ref, after_ref, dp_ref, dvec_ref, dw4_ref,
             xc_ref, af_ref, hf_ref, ab_ref, hb_ref, dh_ref, stage_ref, sems):
        dxb_ref, dgate_ref = stage_ref.at[0], stage_ref.at[1]
        vec = vec_ref[...]
        _lru_prepare(xb_ref, w4_ref, vec, xc_ref, af_ref, hf_ref, ab_ref, hb_ref, T)
        _lru_scan(af_ref, hf_ref, ab_ref, hb_ref, T)

        def gate_bwd(c, carry):
            rows = _rows(c * LRU_CH, LRU_CH)
            gl, dgl = _gelu_parts(gate_ref[rows, :])
            dy = dy_ref[rows, :]
            dgate_ref[rows, :] = (dy * (hf_ref[rows, :] + hb_ref[rows, :]) * dgl).astype(dgate_ref.dtype)
            dh_ref[rows, :] = dy * gl
            return carry

        lax.fori_loop(0, nch, gate_bwd, 0)

        row = lax.broadcasted_iota(jnp.int32, (SUB, LANE), 0)

        def adj(j, carry):
            gf, a_next, gb, a_prev = carry
            tf = nt - 1 - j
            sf = _rows(tf * SUB, SUB)
            a_t = af_ref[sf, :]
            h_t = hf_ref[sf, :]
            coef = jnp.where(row == SUB - 1, a_next, pltpu.roll(a_t, SUB - 1, 0))
            ac, bc = _scan_tile(coef, dh_ref[sf, :], True, row)
            g = ac * gf + bc
            h_prev = hf_ref[_rows(jnp.maximum(tf - 1, 0) * SUB, SUB), :]
            h_prev = jnp.where(tf > 0, _bcast_row(h_prev, SUB - 1), 0.0)
            hs = jnp.where(row == 0, h_prev, pltpu.roll(h_t, 1, 0))
            af_ref[sf, :] = g * hs
            hf_ref[sf, :] = g
            gf = _bcast_row(g, 0)
            a_next = _bcast_row(a_t, 0)
            sb = _rows(j * SUB, SUB)
            a_t = ab_ref[sb, :]
            h_t = hb_ref[sb, :]
            coef = jnp.where(row == 0, a_prev, pltpu.roll(a_t, 1, 0))
            ac, bc = _scan_tile(coef, dh_ref[sb, :], False, row)
            g = ac * gb + bc
            h_next = hb_ref[_rows(jnp.minimum(j + 1, nt - 1) * SUB, SUB), :]
            h_next = jnp.where(j < nt - 1, _bcast_row(h_next, 0), 0.0)
            hs = jnp.where(row == SUB - 1, h_next, pltpu.roll(h_t, SUB - 1, 0))
            ab_ref[sb, :] = g * hs
            hb_ref[sb, :] = g
            gb = _bcast_row(g, SUB - 1)
            a_prev = _bcast_row(a_t, SUB - 1)
            return gf, a_next, gb, a_prev

        z = jnp.zeros((SUB, LANE), F32)
        lax.fori_loop(0, nt, adj, (z, z, z, z))

        sp_f = _softplus_neg(vec[9:10, :])
        sp_b = _softplus_neg(vec[10:11, :])
        w4 = w4_ref[0]
        dw4_ref[...] = jnp.zeros_like(dw4_ref)

        def one_dir(pre_a, pre_x, sp, xc, du, da):
            r, i, a, s = _lru_dir(pre_a, pre_x, sp)
            d_i = du * s * xc
            dxc = du * s * i
            d_s = du * i * xc
            d_log = da * a - d_s * (a * a) / s
            d_r = d_log * (-LRU_C) * sp
            d_sp = jnp.sum(d_log * (-LRU_C) * r, axis=0, keepdims=True)
            return d_r * r * (1.0 - r), d_i * i * (1.0 - i), dxc, d_sp

        def gates_bwd(c, carry):
            db, dspf, dspb = carry
            rows = _rows(c * LRU_CH, LRU_CH)
            xc = xc_ref[rows, :]
            pre = _mm(xc, w4)
            dpa_f, dpx_f, dxc_f, d_sp_f = one_dir(pre[:, 0:128] + vec[5:6, :], pre[:, 128:256] + vec[6:7, :],
                                                  sp_f, xc, hf_ref[rows, :], af_ref[rows, :])
            dpa_b, dpx_b, dxc_b, d_sp_b = one_dir(pre[:, 256:384] + vec[7:8, :], pre[:, 384:512] + vec[8:9, :],
                                                  sp_b, xc, hb_ref[rows, :], ab_ref[rows, :])
            dpre = jnp.concatenate([dpa_f, dpx_f, dpa_b, dpx_b], axis=1)
            dw4_ref[0] += _mm_tn(xc, dpre)
            dh_ref[rows, :] = dxc_f + dxc_b + _mm_nt(dpre, w4)
            return db + jnp.sum(dpre, axis=0, keepdims=True), dspf + d_sp_f, dspb + d_sp_b

        z1 = jnp.zeros((1, LANE), F32)
        db, dspf, dspb = lax.fori_loop(0, nch, gates_bwd, (jnp.zeros((1, 4 * LANE), F32), z1, z1))

        def conv_bwd(c, carry):
            t0 = c * LRU_CH
            rows = _rows(t0, LRU_CH)
            dwin = _window(dh_ref, t0, LRU_CH, T)
            xwin = _window(xb_ref, t0, LRU_CH, T)
            dxc = dh_ref[rows, :]
            dxb = jnp.zeros((LRU_CH, LANE), F32)
            out = []
            for j in range(CONV_WIDTH):
                off = j - CONV_WIDTH // 2
                dxb = dxb + _tap(dwin, -off, LRU_CH) * vec[j:j + 1, :]
                out.append(carry[j] + jnp.sum(dxc * _tap(xwin, off, LRU_CH), axis=0, keepdims=True))
            dxb_ref[rows, :] = dxb.astype(dxb_ref.dtype)
            out.append(carry[CONV_WIDTH] + jnp.sum(dxc, axis=0, keepdims=True))
            return tuple(out)

        dconv = lax.fori_loop(0, nch, conv_bwd, (z1,) * (CONV_WIDTH + 1))
        dlam_f = dspf * (-_sigmoid(-vec[9:10, :]))
        dlam_b = dspb * (-_sigmoid(-vec[10:11, :]))
        dvec_ref[...] = jnp.concatenate(
            list(dconv) + [db[:, 0:128], db[:, 128:256], db[:, 256:384], db[:, 384:512], dlam_f, dlam_b,
                           jnp.zeros((5, LANE), F32)], axis=0)
        j = pl.program_id(0)
        _store_strips(stage_ref, dp_ref, (j, j + 3), sems)

    ns = LRU_WIDTH // LANE
    return pl.pallas_call(
        body, name="lru_bwd", grid=(ns,),
        in_specs=[_strip(T, lambda j: j), _strip(T, lambda j: j + 3), _strip(T, lambda j: j),
                  pl.BlockSpec((16, LANE), lambda j: (0, j)),
                  pl.BlockSpec((1, LANE, 4 * LANE), lambda j: (j, 0, 0)),
                  pl.BlockSpec(memory_space=pl.ANY)],
        out_specs=[pl.BlockSpec(memory_space=pl.ANY),
                   pl.BlockSpec((16, LANE), lambda j: (0, j)),
                   pl.BlockSpec((1, LANE, 4 * LANE), lambda j: (j, 0, 0))],
        out_shape=[jax.ShapeDtypeStruct((T, IN_WIDTH), _BF),
                   jax.ShapeDtypeStruct((16, LRU_WIDTH), F32), jax.ShapeDtypeStruct((ns, LANE, 4 * LANE), F32)],
        scratch_shapes=[pltpu.VMEM((T, LANE), F32)] * 6 + [pltpu.VMEM((2, T, LANE), _BF), pltpu.SemaphoreType.DMA((2,))],
        compiler_params=_cparams(("arbitrary",)),
    )(proj, proj, dycat, vec, w4, after)


def _lru_vec(cw, cb, ba, bx, lam):
    return jnp.concatenate([cw, cb[None], ba[0:1], bx[0:1], ba[1:2], bx[1:2], lam, jnp.zeros((5, LRU_WIDTH), F32)], axis=0)


def _lru_w4(wa, wx):
    nl = wa.shape[0]
    w = jnp.stack([wa[:, 0], wx[:, 0], wa[:, 1], wx[:, 1]], axis=1)
    w = w.reshape(nl, 4, 3, 2, 64, 64)
    eye = jnp.eye(2, dtype=w.dtype)
    bd = w[:, :, :, :, :, None, :] * eye[None, None, None, :, None, :, None]
    bd = bd.reshape(nl, 4, 3, LANE, LANE)
    return bd.transpose(0, 2, 3, 1, 4).reshape(nl, 3, LANE, 4 * LANE).astype(_BF)


def _lru_unpack(dvec, dw4):
    def blocks(m):
        m = m.reshape(3, 2, 64, 2, 64)
        return jnp.stack([m[:, 0, :, 0, :], m[:, 1, :, 1, :]], axis=1).reshape(6, 64, 64)
    parts = [blocks(dw4[:, :, k * LANE:(k + 1) * LANE]) for k in range(4)]
    dwa = jnp.stack([parts[0], parts[2]])
    dwx = jnp.stack([parts[1], parts[3]])
    dba = jnp.stack([dvec[5], dvec[7]])
    dbx = jnp.stack([dvec[6], dvec[8]])
    return dvec[0:4], dvec[4], dwa, dba, dwx, dbx, dvec[9:11]


RC = 2 * RET_CHUNK


def _ret_tables(T):
    half = HEAD_DIM // 2
    pos = jnp.arange(T, dtype=F32)
    inv_freq = ROPE_BASE ** (-jnp.arange(half, dtype=F32) / half)
    ang = pos[:, None] * inv_freq[None, :]
    cos = jnp.tile(jnp.cos(ang), (1, 4))
    sin = jnp.tile(jnp.concatenate([-jnp.sin(ang), jnp.sin(ang)], axis=1), (1, 2))
    log_g = jnp.log1p(-jnp.exp2(-5.0 - jnp.arange(RET_HEADS, dtype=F32)))
    idx = jnp.arange(RC, dtype=F32)
    dec = jnp.exp(jnp.abs(idx[:, None] - idx[None, :]) * log_g[:, None, None])
    lg = jnp.repeat(log_g, HEAD_DIM).reshape(3, 1, LANE)
    col = idx[None, :, None]
    rtab = jnp.stack([jnp.exp((RC - 1 - col) * lg), jnp.exp(col * lg),
                      jnp.exp((col + 1.0) * lg), jnp.exp((RC - col) * lg)], axis=1)
    gch = jnp.broadcast_to(jnp.exp(RC * lg), (3, SUB, LANE))
    return cos, sin, dec, rtab, gch


def _swap32(x, lane):
    return jnp.where((lane & 32) == 0, pltpu.roll(x, LANE - 32, 1), pltpu.roll(x, 32, 1))


def _head_mean(x, m0, m1):
    s0 = jnp.sum(x * m0, axis=-1, keepdims=True)
    s1 = jnp.sum(x * m1, axis=-1, keepdims=True)
    return (s0 * m0 + s1 * m1) * (1.0 / HEAD_DIM)


def _ret_masks():
    lane = lax.broadcasted_iota(jnp.int32, (RC, LANE), 1)
    m0 = (lane < HEAD_DIM).astype(F32)
    r = lax.broadcasted_iota(jnp.int32, (LANE, LANE), 0) // HEAD_DIM
    c = lax.broadcasted_iota(jnp.int32, (LANE, LANE), 1) // HEAD_DIM
    return lane, m0, 1.0 - m0, (r == c).astype(F32)


def _ret_specs(T):
    const = lambda shape, imap: pl.BlockSpec(shape, imap)
    return [_strip(T, lambda j: j + 6), _strip(T, lambda j: j + 9), _strip(T, lambda j: j + 12),
            _strip(T, lambda j: j + 15),
            pl.BlockSpec((T, LANE), lambda j: (0, 0), pipeline_mode=pl.Buffered(1)),
            pl.BlockSpec((T, LANE), lambda j: (0, 0), pipeline_mode=pl.Buffered(1)),
            const((2, RC, RC), lambda j: (j, 0, 0)),
            const((1, 4, RC, LANE), lambda j: (j, 0, 0, 0)),
            const((1, SUB, LANE), lambda j: (j, 0, 0)),
            const((SUB, LANE), lambda j: (0, j))]


def _ret_fwd_call(proj, tables, gnw8):
    T = proj.shape[0]
    nc = T // RC
    cos, sin, dec, rtab, gch = tables

    def body(q_ref, k_ref, v_ref, g_ref, cos_ref, sin_ref, dec_ref, rtab_ref, gch_ref, gnw_ref, y_ref, stf_ref, kr_ref):
        lane, m0, m1, bd = _ret_masks()
        gch_v = gch_ref[0][0:1, :]
        gnw = gnw_ref[0:1, :]
        dkf, dkb, dqf, dqb = rtab_ref[0, 0], rtab_ref[0, 1], rtab_ref[0, 2], rtab_ref[0, 3]

        def rope(x, rows):
            return x * cos_ref[rows, :] + _swap32(x, lane) * sin_ref[rows, :]

        def pass_a(n, st):
            rows = _rows(n * RC, RC)
            stf_ref[n] = st
            kr = rope(k_ref[rows, :], rows) * (HEAD_DIM ** -0.5)
            kr_ref[rows, :] = kr
            return gch_v * st + _mm_tn(kr * dkf, v_ref[rows, :]) * bd

        _loop2(nc, pass_a, jnp.zeros((LANE, LANE), F32))

        def pass_b(i, stb):
            ns = [nc - 1 - 2 * i, nc - 2 - 2 * i]
            rows = [_rows(n * RC, RC) for n in ns]
            heads = ((0, m0), (1, m1))
            qr = [rope(q_ref[r, :], r) for r in rows]
            kr = [kr_ref[r, :] for r in rows]
            v = [v_ref[r, :] for r in rows]
            kv = [_mm_tn(kr[c] * dkb, v[c]) * bd for c in range(2)]
            stbs = [stb, gch_v * stb + kv[0]]
            s = [[_mm_nt(qr[c] * m, kr[c]) * dec_ref[h] for h, m in heads] for c in range(2)]
            o = [_mm(qr[c] * dqf, stf_ref[ns[c]]) + _mm(qr[c] * dqb, stbs[c]) for c in range(2)]
            o = [o[c] + _mm(s[c][0], v[c] * m0) + _mm(s[c][1], v[c] * m1) for c in range(2)]
            oc = [o_ - _head_mean(o_, m0, m1) for o_ in o]
            on = [oc_ * lax.rsqrt(_head_mean(oc_ * oc_, m0, m1) + GN_EPS) for oc_ in oc]
            for c in range(2):
                g = g_ref[rows[c], :]
                y_ref[rows[c], :] = (g * _sigmoid(g)) * (on[c] * gnw)
            return gch_v * stbs[1] + kv[1]

        assert nc % 2 == 0
        lax.fori_loop(0, nc // 2, pass_b, jnp.zeros((LANE, LANE), F32))

    return pl.pallas_call(
        body, name="ret_fwd", grid=(RET_WIDTH // LANE,),
        in_specs=_ret_specs(T),
        out_specs=_strip(T, lambda j: j, buffers=1),
        out_shape=jax.ShapeDtypeStruct((T, RET_WIDTH), F32),
        scratch_shapes=[pltpu.VMEM((nc, LANE, LANE), F32), pltpu.VMEM((T, LANE), F32)],
        compiler_params=_cparams(("arbitrary",)),
    )(proj, proj, proj, proj, cos, sin, dec, rtab, gch, gnw8)


def _ret_bwd_call(proj, dycat, tables, gnw8, dp):
    T = proj.shape[0]
    nc = T // RC
    cos, sin, dec, rtab, gch = tables

    def body(q_ref, k_ref, v_ref, g_ref, cos_ref, sin_ref, dec_ref, rtab_ref, gch_ref, gnw_ref, dy_ref, dp_in_ref,
             dp_out_ref, dgnw_ref, stf_ref, dstb_ref, dkr_ref, dv_ref, dp_ref, kr_ref, sems):
        lane, m0, m1, bd = _ret_masks()
        gch_v = gch_ref[0][0:1, :]
        gnw = gnw_ref[0:1, :]
        dkf, dkb, dqf, dqb = rtab_ref[0, 0], rtab_ref[0, 1], rtab_ref[0, 2], rtab_ref[0, 3]
        scale = HEAD_DIM ** -0.5
        zst = jnp.zeros((LANE, LANE), F32)

        def rope(x, rows):
            return x * cos_ref[rows, :] + _swap32(x, lane) * sin_ref[rows, :]

        def rope_t(d, rows):
            return d * cos_ref[rows, :] + _swap32(d * sin_ref[rows, :], lane)

        def pass_a(n, st):
            rows = _rows(n * RC, RC)
            stf_ref[n] = st
            kr = rope(k_ref[rows, :], rows) * scale
            kr_ref[rows, :] = kr
            return gch_v * st + _mm_tn(kr * dkf, v_ref[rows, :]) * bd

        _loop2(nc, pass_a, zst)

        def pass_b(i, carry):
            stb, d_f, dgnw = carry
            two = range(2)
            heads = ((0, m0), (1, m1))
            ns = [nc - 1 - 2 * i, nc - 2 - 2 * i]
            rows = [_rows(n * RC, RC) for n in ns]
            qr = [rope(q_ref[r, :], r) for r in rows]
            kr = [kr_ref[r, :] for r in rows]
            v = [v_ref[r, :] for r in rows]
            stf = [stf_ref[n] for n in ns]
            kvb = [_mm_tn(kr[c] * dkb, v[c]) * bd for c in two]
            stbs = [stb, gch_v * stb + kvb[0]]
            qf = [qr[c] * dqf for c in two]
            qb = [qr[c] * dqb for c in two]
            s = [[_mm_nt(qr[c] * m, kr[c]) * dec_ref[h] for h, m in heads] for c in two]
            o = [_mm(qf[c], stf[c]) + _mm(qb[c], stbs[c]) for c in two]
            o = [o[c] + _mm(s[c][0], v[c] * m0) + _mm(s[c][1], v[c] * m1) for c in two]
            oc = [o_ - _head_mean(o_, m0, m1) for o_ in o]
            rstd = [lax.rsqrt(_head_mean(oc_ * oc_, m0, m1) + GN_EPS) for oc_ in oc]
            on = [oc[c] * rstd[c] for c in two]
            do = []
            for c in two:
                g = g_ref[rows[c], :]
                sg = _sigmoid(g)
                dy = dy_ref[rows[c], :]
                dp_ref[3, rows[c], :] = (dy * (on[c] * gnw) * (sg * (1.0 + g * (1.0 - sg)))).astype(dp_ref.dtype)
                t = dy * (g * sg)
                dgnw = dgnw + jnp.sum(t * on[c], axis=0, keepdims=True)
                don = t * gnw
                do.append(rstd[c] * (don - _head_mean(don, m0, m1) - on[c] * _head_mean(don * on[c], m0, m1)))
            dstf = [_mm_tn(qf[c], do[c]) * bd for c in two]
            dfs = [d_f, dstf[0] + gch_v * d_f]
            ds = [[_mm_nt(do[c] * m, v[c]) * dec_ref[h] for h, m in heads] for c in two]
            dqr = [_mm_nt(do[c], stf[c]) * dqf + _mm_nt(do[c], stbs[c]) * dqb
                   + _mm(ds[c][0], kr[c] * m0) + _mm(ds[c][1], kr[c] * m1) for c in two]
            dkr = [_mm_nt(v[c], dfs[c]) * dkf + _mm_tn(ds[c][0], qr[c] * m0) + _mm_tn(ds[c][1], qr[c] * m1) for c in two]
            dv = [_mm(kr[c] * dkf, dfs[c]) + _mm_tn(s[c][0], do[c] * m0) + _mm_tn(s[c][1], do[c] * m1) for c in two]
            for c in two:
                dp_ref[0, rows[c], :] = rope_t(dqr[c], rows[c]).astype(dp_ref.dtype)
                dkr_ref[rows[c], :] = dkr[c]
                dv_ref[rows[c], :] = dv[c]
                dstb_ref[ns[c]] = _mm_tn(qb[c], do[c]) * bd
            return gch_v * stbs[1] + kvb[1], dstf[1] + gch_v * dfs[1], dgnw

        assert nc % 2 == 0
        _, _, dgnw = lax.fori_loop(0, nc // 2, pass_b, (zst, zst, jnp.zeros((1, LANE), F32)))
        dgnw_ref[...] = jnp.concatenate([dgnw, jnp.zeros((SUB - 1, LANE), F32)], axis=0)

        def pass_c(n, d_b):
            rows = _rows(n * RC, RC)
            kr = kr_ref[rows, :]
            v = v_ref[rows, :]
            dkr = dkr_ref[rows, :] + _mm_nt(v, d_b) * dkb
            dp_ref[1, rows, :] = (rope_t(dkr, rows) * scale).astype(dp_ref.dtype)
            dp_ref[2, rows, :] = (dv_ref[rows, :] + _mm(kr * dkb, d_b)).astype(dp_ref.dtype)
            return dstb_ref[n] + gch_v * d_b

        _loop2(nc, pass_c, zst)
        j = pl.program_id(0)
        _store_strips(dp_ref, dp_out_ref, (j + 6, j + 9, j + 12, j + 15), sems)

    n_in = len(_ret_specs(T)) + 1
    return pl.pallas_call(
        body, name="ret_bwd", grid=(RET_WIDTH // LANE,),
        in_specs=_ret_specs(T) + [_strip(T, lambda j: j + 3), pl.BlockSpec(memory_space=pl.ANY)],
        out_specs=[pl.BlockSpec(memory_space=pl.ANY), pl.BlockSpec((SUB, LANE), lambda j: (0, j))],
        out_shape=[jax.ShapeDtypeStruct(dp.shape, dp.dtype), jax.ShapeDtypeStruct((SUB, RET_WIDTH), F32)],
        scratch_shapes=[pltpu.VMEM((nc, LANE, LANE), F32), pltpu.VMEM((nc, LANE, LANE), F32),
                        pltpu.VMEM((T, LANE), F32), pltpu.VMEM((T, LANE), F32),
                        pltpu.VMEM((4, T, LANE), _BF), pltpu.VMEM((T, LANE), F32), pltpu.SemaphoreType.DMA((4,))],
        input_output_aliases={n_in: 0},
        compiler_params=_cparams(("arbitrary",)),
    )(proj, proj, proj, proj, cos, sin, dec, rtab, gch, gnw8, dycat, dp)


NA_Q = 2 * GRID_W
NA_WROWS = 10
NA_K = NA_WROWS * GRID_W
NA_CHUNKS = NA_K // LANE
NA_UNROLL = 2
NA_TYPES = 5
_ONEHOT_PRECISION = lax.Precision.HIGH


def _na_onehots(rows_n):
    reps = [(0, 0), (2, 0), (4, 0), (rows_n - 4, rows_n - NA_WROWS), (rows_n - 2, rows_n - NA_WROWS)]
    rm = np.zeros((NA_TYPES, 2, NA_WROWS, 2 * NA_KH - 1), np.float32)
    for t, (r, ws) in enumerate(reps):
        for qh in range(2):
            qrow = r + qh
            rstart = min(max(qrow - NA_KH // 2, 0), rows_n - NA_KH)
            for kh in range(NA_WROWS):
                krow = ws + kh
                if rstart <= krow < rstart + NA_KH:
                    rm[t, qh, kh, krow - qrow + NA_KH - 1] = 1.0
    cm = np.zeros((GRID_W, GRID_W, 2 * NA_KW - 1), np.float32)
    for qc in range(GRID_W):
        cstart = min(max(qc - NA_KW // 2, 0), GRID_W - NA_KW)
        for kc in range(cstart, cstart + NA_KW):
            cm[qc, kc, kc - qc + NA_KW - 1] = 1.0
    rm2 = rm.reshape(NA_TYPES, 2, NA_CHUNKS, 2, 2 * NA_KH - 1)
    cm2 = np.zeros((GRID_W, LANE, 2, 2 * NA_KW - 1), np.float32)
    for z in range(2):
        cm2[:, z * GRID_W:(z + 1) * GRID_W, z, :] = cm
    return rm2, cm2


def _na_bias_tables(rpb, rows_n):
    rm, cm = _na_onehots(rows_n)
    val = jnp.einsum("hab,tqpza,xkzb->htpqxk", rpb, rm, cm, precision=_ONEHOT_PRECISION)
    valid = np.einsum("tqpz,xkz->tpqxk", rm.sum(-1), cm.sum(-1)) > 0.5
    return jnp.where(valid[None], val, NEG).reshape(2, 2, NA_TYPES, NA_CHUNKS, NA_Q, LANE)


def _na_bias_grad(dtab, rows_n):
    rm, cm = _na_onehots(rows_n)
    d6 = dtab.reshape(NA_HEADS, NA_TYPES, NA_CHUNKS, 2, GRID_W, LANE)
    return jnp.einsum("htpqxk,tqpza,xkzb->hab", d6, rm, cm, precision=_ONEHOT_PRECISION)


def _na_bias(b_ref, h, typ):
    return jnp.concatenate([b_ref[0, h, typ, c] for c in range(NA_CHUNKS)], axis=1)


def _na_step(p, npairs, rows_n):
    ws = jnp.clip(2 * p - NA_KH // 2, 0, rows_n - NA_WROWS)
    koff = pl.multiple_of(ws * GRID_W, LANE)
    typ = jnp.where(p == 0, 0, jnp.where(p == 1, 1, jnp.where(p == npairs - 2, 3, jnp.where(p == npairs - 1, 4, 2))))
    return _rows(p * NA_Q, NA_Q), pl.ds(koff, NA_K), typ


def _na_fwd_call(proj, btab):
    T = proj.shape[0]
    npairs, rows_n = T // NA_Q, T // GRID_W

    def body(q_ref, k_ref, v_ref, b_ref, o_ref):
        lane = lax.broadcasted_iota(jnp.int32, (NA_Q, LANE), 1)
        m0 = (lane < HEAD_DIM).astype(F32)
        m1 = 1.0 - m0

        def steps(i, carry):
            idx = [_na_step(NA_UNROLL * i + u, npairs, rows_n) for u in range(NA_UNROLL)]
            chains = [(u, h, m) for u in range(NA_UNROLL) for h, m in ((0, m0), (1, m1))]
            kws = [k_ref[krows, :].astype(_BF) for _, krows, _ in idx]
            vws = [v_ref[krows, :].astype(_BF) for _, krows, _ in idx]
            s = [_mm_nt(q_ref[idx[u][0], :] * m, kws[u]) for u, h, m in chains]
            s = [s_ * (HEAD_DIM ** -0.5) + _na_bias(b_ref, h, idx[u][2]) for s_, (u, h, m) in zip(s, chains)]
            e = [jnp.exp(s_ - jnp.max(s_, axis=-1, keepdims=True)) for s_ in s]
            pr = [e_ / jnp.sum(e_, axis=-1, keepdims=True) for e_ in e]
            ov = [_mm(pr_, vws[u]) * m for pr_, (u, h, m) in zip(pr, chains)]
            for u in range(NA_UNROLL):
                o_ref[idx[u][0], :] = ov[2 * u] + ov[2 * u + 1]
            return carry

        lax.fori_loop(0, npairs // NA_UNROLL, steps, 0)

    return pl.pallas_call(
        body, name="na_fwd", grid=(NA_WIDTH // LANE,),
        in_specs=[_strip(T, lambda j: j + 18), _strip(T, lambda j: j + 20), _strip(T, lambda j: j + 22),
                  pl.BlockSpec((1, 2, NA_TYPES, NA_CHUNKS, NA_Q, LANE), lambda j: (j, 0, 0, 0, 0, 0))],
        out_specs=_strip(T, lambda j: j, buffers=1),
        out_shape=jax.ShapeDtypeStruct((T, NA_WIDTH), F32),
        compiler_params=_cparams(("arbitrary",)),
    )(proj, proj, proj, btab)


def _na_bwd_call(proj, dycat, btab, dp):
    T = proj.shape[0]
    npairs, rows_n = T // NA_Q, T // GRID_W
    scale = HEAD_DIM ** -0.5

    def body(q_ref, k_ref, v_ref, do_ref, b_ref, dp_in_ref, dp_out_ref, db_ref, dka_ref, dva_ref, stage_ref, sems):
        dq_ref = stage_ref.at[0]
        lane = lax.broadcasted_iota(jnp.int32, (NA_Q, LANE), 1)
        m0 = (lane < HEAD_DIM).astype(F32)
        m1 = 1.0 - m0
        dka_ref[...] = jnp.zeros_like(dka_ref)
        dva_ref[...] = jnp.zeros_like(dva_ref)
        db_ref[...] = jnp.zeros_like(db_ref)

        def steps(i, carry):
            idx = [_na_step(NA_UNROLL * i + u, npairs, rows_n) for u in range(NA_UNROLL)]
            chains = [(u, h, m) for u in range(NA_UNROLL) for h, m in ((0, m0), (1, m1))]
            kws = [k_ref[krows, :].astype(_BF) for _, krows, _ in idx]
            vws = [v_ref[krows, :].astype(_BF) for _, krows, _ in idx]
            qm = [(q_ref[idx[u][0], :] * m).astype(_BF) for u, h, m in chains]
            dom = [(do_ref[idx[u][0], :] * m).astype(_BF) for u, h, m in chains]
            s = [_mm_nt(qm_, kws[u]) for qm_, (u, h, m) in zip(qm, chains)]
            dpr = [_mm_nt(dom_, vws[u]) for dom_, (u, h, m) in zip(dom, chains)]
            s = [s_ * scale + _na_bias(b_ref, h, idx[u][2]) for s_, (u, h, m) in zip(s, chains)]
            e = [jnp.exp(s_ - jnp.max(s_, axis=-1, keepdims=True)) for s_ in s]
            pr = [e_ / jnp.sum(e_, axis=-1, keepdims=True) for e_ in e]
            ds = [pr_ * (dpr_ - jnp.sum(pr_ * dpr_, axis=-1, keepdims=True)) for pr_, dpr_ in zip(pr, dpr)]
            dsb = [(ds_ * scale).astype(_BF) for ds_ in ds]
            dq = [_mm(dsb_, kws[u]) * m for dsb_, (u, h, m) in zip(dsb, chains)]
            dk = [_mm_tn(dsb_, qm_) for dsb_, qm_ in zip(dsb, qm)]
            dv = [_mm_tn(pr_, dom_) for pr_, dom_ in zip(pr, dom)]
            for ds_, (u, h, m) in zip(ds, chains):
                for c in range(NA_CHUNKS):
                    db_ref[0, h, idx[u][2], c] += ds_[:, c * LANE:(c + 1) * LANE]
            for u in range(NA_UNROLL):
                qrows, krows, _ = idx[u]
                dq_ref[qrows, :] = (dq[2 * u] + dq[2 * u + 1]).astype(dq_ref.dtype)
                dka_ref[krows, :] += dk[2 * u] + dk[2 * u + 1]
                dva_ref[krows, :] += dv[2 * u] + dv[2 * u + 1]
            return carry

        lax.fori_loop(0, npairs // NA_UNROLL, steps, 0)
        stage_ref[1] = dka_ref[...].astype(stage_ref.dtype)
        stage_ref[2] = dva_ref[...].astype(stage_ref.dtype)
        j = pl.program_id(0)
        _store_strips(stage_ref, dp_out_ref, (j + 18, j + 20, j + 22), sems)

    tab = pl.BlockSpec((1, 2, NA_TYPES, NA_CHUNKS, NA_Q, LANE), lambda j: (j, 0, 0, 0, 0, 0))
    return pl.pallas_call(
        body, name="na_bwd", grid=(NA_WIDTH // LANE,),
        in_specs=[_strip(T, lambda j: j + 18), _strip(T, lambda j: j + 20), _strip(T, lambda j: j + 22),
                  _strip(T, lambda j: j + 6), tab, pl.BlockSpec(memory_space=pl.ANY)],
        out_specs=[pl.BlockSpec(memory_space=pl.ANY), tab],
        out_shape=[jax.ShapeDtypeStruct(dp.shape, dp.dtype),
                   jax.ShapeDtypeStruct((2, 2, NA_TYPES, NA_CHUNKS, NA_Q, LANE), F32)],
        scratch_shapes=[pltpu.VMEM((T, LANE), F32), pltpu.VMEM((T, LANE), F32),
                        pltpu.VMEM((3, T, LANE), _BF), pltpu.SemaphoreType.DMA((3,))],
        input_output_aliases={5: 0},
        compiler_params=_cparams(("arbitrary",)),
    )(proj, proj, proj, dycat, btab, dp)


W_BLK = IN_WIDTH // N_DEV
MXU_W = 256
N_BLK = 3 * MXU_W
N_STEPS = IN_WIDTH // N_BLK
TM = 512


def _ln_fwd(z, g, b):
    zc = z - jnp.mean(z, axis=-1, keepdims=True)
    var = jnp.mean(zc * zc, axis=-1, keepdims=True)
    return zc * lax.rsqrt(var + LN_EPS) * g + b


def _ln_bwd(dy, z, g):
    zc = z - jnp.mean(z, axis=-1, keepdims=True)
    rstd = lax.rsqrt(jnp.mean(zc * zc, axis=-1, keepdims=True) + LN_EPS)
    xhat = zc * rstd
    dxh = dy * g
    dz = rstd * (dxh - jnp.mean(dxh, axis=-1, keepdims=True) - xhat * jnp.mean(dxh * xhat, axis=-1, keepdims=True))
    return dz, dy * xhat


def _row_tile(T):
    return 1024 if T % 1024 == 0 else TM


def _halves(n):
    return (pl.ds(0, n // 2), pl.ds(n // 2, n // 2))


def _inproj_call(xb, w, after):
    T = xb.shape[0]
    tm = _row_tile(T)

    def body(x_ref, w_ref, after_ref, o_ref):
        o_ref[...] = _mm(x_ref[...], w_ref[...])

    return pl.pallas_call(
        body, name="inproj", grid=(T // tm, N_STEPS),
        in_specs=[pl.BlockSpec((tm, D_MODEL), lambda i, n: (i, 0)),
                  pl.BlockSpec((D_MODEL, N_BLK), lambda i, n: (0, n)),
                  pl.BlockSpec(memory_space=pl.ANY)],
        out_specs=pl.BlockSpec((tm, N_BLK), lambda i, n: (i, n)),
        out_shape=jax.ShapeDtypeStruct((T, IN_WIDTH), F32),
        compiler_params=_cparams(("parallel", "arbitrary")),
    )(xb, w, after)


def _vec_spec():
    return pl.BlockSpec((1, D_MODEL), lambda *_: (0, 0))


def _outproj_ln_call(y_lru, y_ret, y_na, x, w, g, b, after):
    T = x.shape[0]

    def body(yl_ref, yr_ref, yn_ref, x_ref, w_ref, g_ref, b_ref, after_ref, z_ref, x1_ref, x1b_ref, yc_ref):
        yc_ref[:, 0:LRU_WIDTH] = yl_ref[...].astype(yc_ref.dtype)
        yc_ref[:, LRU_WIDTH:LRU_WIDTH + RET_WIDTH] = yr_ref[...].astype(yc_ref.dtype)
        yc_ref[:, LRU_WIDTH + RET_WIDTH:] = yn_ref[...].astype(yc_ref.dtype)
        z = ALPHA * x_ref[...] + _mm(yc_ref[...], w_ref[...])
        z_ref[...] = z
        x1 = _ln_fwd(z, g_ref[...], b_ref[...])
        x1_ref[...] = x1
        x1b_ref[...] = x1.astype(x1b_ref.dtype)

    row = lambda w_: pl.BlockSpec((TM, w_), lambda i: (i, 0))
    return pl.pallas_call(
        body, name="outproj_ln", grid=(T // TM,),
        in_specs=[row(LRU_WIDTH), row(RET_WIDTH), row(NA_WIDTH), row(D_MODEL),
                  pl.BlockSpec((D_MODEL, D_MODEL), lambda i: (0, 0)), _vec_spec(), _vec_spec(),
                  pl.BlockSpec(memory_space=pl.ANY)],
        out_specs=[row(D_MODEL)] * 4,
        out_shape=[jax.ShapeDtypeStruct((T, D_MODEL), F32), jax.ShapeDtypeStruct((T, D_MODEL), F32),
                   jax.ShapeDtypeStruct((T, D_MODEL), _BF), jax.ShapeDtypeStruct((T, D_MODEL), _BF)],
        compiler_params=_cparams(("parallel",)),
    )(y_lru, y_ret, y_na, x, w, g, b, after)


def _ffn_ln_call(x1, x1b, wg, wu, wd, g, b):
    T = x1.shape[0]

    def body(x_ref, xb_ref, wg_ref, wu_ref, wd_ref, g_ref, b_ref, z_ref, x2_ref, x2b_ref, gp_ref, up_ref, acc_ref):
        n = pl.program_id(1)

        @pl.when(n == 0)
        def _():
            acc_ref[...] = jnp.zeros_like(acc_ref)

        for rows in _halves(TM):
            xb = xb_ref[rows, :]
            gp = _mm(xb, wg_ref[...])
            up = _mm(xb, wu_ref[...])
            gp_ref[rows, :] = gp.astype(gp_ref.dtype)
            up_ref[rows, :] = up.astype(up_ref.dtype)
            acc_ref[rows, :] += _mm(gp * _sigmoid(gp) * up, wd_ref[...])

        @pl.when(n == N_STEPS - 1)
        def _():
            z = ALPHA * x_ref[...] + acc_ref[...]
            z_ref[...] = z
            x2 = _ln_fwd(z, g_ref[...], b_ref[...])
            x2_ref[...] = x2
            x2b_ref[...] = x2.astype(x2b_ref.dtype)

    row = pl.BlockSpec((TM, D_MODEL), lambda i, n: (i, 0))
    return pl.pallas_call(
        body, name="ffn_ln", grid=(T // TM, N_STEPS),
        in_specs=[row, row,
                  pl.BlockSpec((D_MODEL, N_BLK), lambda i, n: (0, n)),
                  pl.BlockSpec((D_MODEL, N_BLK), lambda i, n: (0, n)),
                  pl.BlockSpec((N_BLK, D_MODEL), lambda i, n: (n, 0)), _vec_spec(), _vec_spec()],
        out_specs=[row] * 3 + [pl.BlockSpec((TM, N_BLK), lambda i, n: (i, n))] * 2,
        out_shape=[jax.ShapeDtypeStruct((T, D_MODEL), F32), jax.ShapeDtypeStruct((T, D_MODEL), F32),
                   jax.ShapeDtypeStruct((T, D_MODEL), _BF),
                   jax.ShapeDtypeStruct((T, IN_WIDTH), _BF), jax.ShapeDtypeStruct((T, IN_WIDTH), _BF)],
        scratch_shapes=[pltpu.VMEM((TM, D_MODEL), F32)],
        compiler_params=_cparams(("parallel", "arbitrary")),
    )(x1, x1b, wg, wu, wd, g, b)


def _loss_call(y, t):
    T = y.shape[0]

    def body(y_ref, t_ref, dy_ref, loss_ref):
        @pl.when(pl.program_id(0) == 0)
        def _():
            loss_ref[...] = jnp.zeros_like(loss_ref)

        err = y_ref[...] - t_ref[...]
        dy_ref[...] = err * (1.0 / D_MODEL)
        part = 0.5 * jnp.sum(jnp.mean(err * err, axis=-1, keepdims=True), axis=0, keepdims=True)
        loss_ref[...] += jnp.broadcast_to(part, loss_ref.shape)

    row = pl.BlockSpec((TM, D_MODEL), lambda i: (i, 0))
    return pl.pallas_call(
        body, name="loss", grid=(T // TM,),
        in_specs=[row, row],
        out_specs=[row, pl.BlockSpec((SUB, LANE), lambda i: (0, 0))],
        out_shape=[jax.ShapeDtypeStruct((T, D_MODEL), F32), jax.ShapeDtypeStruct((SUB, LANE), F32)],
        compiler_params=_cparams(("arbitrary",)),
    )(y, t)


def _ffn_bwd_call(dx2, z2, gpb, upb, wg, wu, wd, g, after):
    T = dx2.shape[0]

    def body(dx2_ref, z_ref, gp_ref, up_ref, wg_ref, wu_ref, wd_ref, g_ref, after_ref,
             dx1_ref, dgp_ref, dup_ref, hid_ref, dzb_ref, dln_ref, acc_ref):
        i, n = pl.program_id(0), pl.program_id(1)

        @pl.when((i == 0) & (n == 0))
        def _():
            dln_ref[...] = jnp.zeros_like(dln_ref)

        @pl.when(n == 0)
        def _():
            dy = dx2_ref[...]
            dz, dg_rows = _ln_bwd(dy, z_ref[...], g_ref[...])
            dzb_ref[...] = dz.astype(dzb_ref.dtype)
            acc_ref[...] = ALPHA * dz
            dln_ref[0:1, :] += jnp.sum(dg_rows, axis=0, keepdims=True)
            dln_ref[1:2, :] += jnp.sum(dy, axis=0, keepdims=True)

        for rows in _halves(TM):
            gp = gp_ref[rows, :].astype(F32)
            up = up_ref[rows, :].astype(F32)
            sg = _sigmoid(gp)
            act = gp * sg
            hid_ref[rows, :] = (act * up).astype(hid_ref.dtype)
            dhid = _mm_nt(dzb_ref[rows, :], wd_ref[...])
            dup = dhid * act
            dgp = dhid * up * (sg * (1.0 + gp * (1.0 - sg)))
            dgp_ref[rows, :] = dgp.astype(dgp_ref.dtype)
            dup_ref[rows, :] = dup.astype(dup_ref.dtype)
            acc_ref[rows, :] += _mm_nt(dgp, wg_ref[...]) + _mm_nt(dup, wu_ref[...])

        @pl.when(n == N_STEPS - 1)
        def _():
            dx1_ref[...] = acc_ref[...]

    row = pl.BlockSpec((TM, D_MODEL), lambda i, n: (i, 0))
    blk = pl.BlockSpec((TM, N_BLK), lambda i, n: (i, n))
    return pl.pallas_call(
        body, name="ffn_bwd", grid=(T // TM, N_STEPS),
        in_specs=[row, row, blk, blk,
                  pl.BlockSpec((D_MODEL, N_BLK), lambda i, n: (0, n)),
                  pl.BlockSpec((D_MODEL, N_BLK), lambda i, n: (0, n)),
                  pl.BlockSpec((N_BLK, D_MODEL), lambda i, n: (n, 0)), _vec_spec(),
                  pl.BlockSpec(memory_space=pl.ANY)],
        out_specs=[row, blk, blk, blk, row, pl.BlockSpec((SUB, D_MODEL), lambda i, n: (0, 0))],
        out_shape=[jax.ShapeDtypeStruct((T, D_MODEL), F32),
                   jax.ShapeDtypeStruct((T, IN_WIDTH), _BF), jax.ShapeDtypeStruct((T, IN_WIDTH), _BF),
                   jax.ShapeDtypeStruct((T, IN_WIDTH), _BF), jax.ShapeDtypeStruct((T, D_MODEL), _BF),
                   jax.ShapeDtypeStruct((SUB, D_MODEL), F32)],
        scratch_shapes=[pltpu.VMEM((TM, D_MODEL), F32)],
        compiler_params=_cparams(("arbitrary", "arbitrary")),
    )(dx2, z2, gpb, upb, wg, wu, wd, g, after)


def _outproj_bwd_call(dx1, z1, w, g):
    T = dx1.shape[0]

    def body(dx_ref, z_ref, w_ref, g_ref, dzb_ref, dyc_ref, dres_ref, dln_ref):
        @pl.when(pl.program_id(0) == 0)
        def _():
            dln_ref[...] = jnp.zeros_like(dln_ref)

        dy = dx_ref[...]
        dz, dg_rows = _ln_bwd(dy, z_ref[...], g_ref[...])
        dzb_ref[...] = dz.astype(dzb_ref.dtype)
        dres_ref[...] = ALPHA * dz
        dyc_ref[...] = _mm_nt(dz, w_ref[...])
        dln_ref[0:1, :] += jnp.sum(dg_rows, axis=0, keepdims=True)
        dln_ref[1:2, :] += jnp.sum(dy, axis=0, keepdims=True)

    row = pl.BlockSpec((TM, D_MODEL), lambda i: (i, 0))
    return pl.pallas_call(
        body, name="outproj_bwd", grid=(T // TM,),
        in_specs=[row, row, pl.BlockSpec((D_MODEL, D_MODEL), lambda i: (0, 0)), _vec_spec()],
        out_specs=[row, row, row, pl.BlockSpec((SUB, D_MODEL), lambda i: (0, 0))],
        out_shape=[jax.ShapeDtypeStruct((T, D_MODEL), _BF), jax.ShapeDtypeStruct((T, D_MODEL), F32),
                   jax.ShapeDtypeStruct((T, D_MODEL), F32), jax.ShapeDtypeStruct((SUB, D_MODEL), F32)],
        compiler_params=_cparams(("arbitrary",)),
    )(dx1, z1, w, g)


def _inproj_bwd_call(dres, dp, w):
    T = dres.shape[0]

    def body(dres_ref, dp_ref, w_ref, dx_ref):
        dx_ref[...] = dres_ref[...] + _mm_nt(dp_ref[...], w_ref[...])

    row = pl.BlockSpec((TM, D_MODEL), lambda i: (i, 0))
    return pl.pallas_call(
        body, name="inproj_bwd", grid=(T // TM,),
        in_specs=[row, pl.BlockSpec((TM, IN_WIDTH), lambda i: (i, 0)),
                  pl.BlockSpec((D_MODEL, IN_WIDTH), lambda i: (0, 0), pipeline_mode=pl.Buffered(1))],
        out_specs=row,
        out_shape=jax.ShapeDtypeStruct((T, D_MODEL), F32),
        compiler_params=_cparams(("parallel",)),
    )(dres, dp, w)


def _tn_cols_call(a, b, name):
    T, ka = a.shape
    n = b.shape[1]

    def body(a_ref, b_ref, o_ref):
        o_ref[...] = _mm_tn(a_ref[...], b_ref[...]).astype(o_ref.dtype)

    return pl.pallas_call(
        body, name=name, grid=(n // N_BLK,),
        in_specs=[pl.BlockSpec((T, ka), lambda j: (0, 0), pipeline_mode=pl.Buffered(1)),
                  pl.BlockSpec((T, N_BLK), lambda j: (0, j))],
        out_specs=pl.BlockSpec((ka, N_BLK), lambda j: (0, j)),
        out_shape=jax.ShapeDtypeStruct((ka, n), _BF),
        compiler_params=_cparams(("parallel",)),
    )(a, b)


def _tn_rows_call(a, b, kb, name):
    T, ka = a.shape
    n = b.shape[1]

    def body(a_ref, b_ref, o_ref):
        o_ref[...] = _mm_tn(a_ref[...], b_ref[...]).astype(o_ref.dtype)

    return pl.pallas_call(
        body, name=name, grid=(ka // kb,),
        in_specs=[pl.BlockSpec((T, kb), lambda r: (0, r)),
                  pl.BlockSpec((T, n), lambda r: (0, 0), pipeline_mode=pl.Buffered(1))],
        out_specs=pl.BlockSpec((kb, n), lambda r: (r, 0)),
        out_shape=jax.ShapeDtypeStruct((ka, n), _BF),
        compiler_params=_cparams(("parallel",)),
    )(a, b)


def _me():
    return lax.axis_index("x"), lax.axis_index("y"), lax.axis_index("c")


def _flip(k):
    x, y, c = _me()
    return (1 - x if k & 4 else x, 1 - y if k & 2 else y, 1 - c if k & 1 else c)


def _dev_index(pos):
    return 4 * pos[0] + 2 * pos[1] + pos[2]


_HBM = pl.BlockSpec(memory_space=pltpu.HBM)
_SEM = pl.BlockSpec(memory_space=pltpu.SEMAPHORE)


def _land_shape(shape, mode):
    if mode == "all":
        return (N_DEV,) + shape
    if mode == "cols":
        return (shape[0], N_DEV * shape[1])
    if mode == "blk":
        return shape
    assert mode == "scols"
    return (N_DEV, shape[0], shape[1] // N_DEV)


def _comm_copies(ins, lands, modes, send_sems, recv_sems):
    me = _dev_index(_me())
    copies = []
    for k in range(N_DEV):
        peer = _flip(k)
        pidx = _dev_index(peer)
        for a, (src, land, mode) in enumerate(zip(ins, lands, modes)):
            if mode == "blk":
                src = src.at[pidx]
            elif mode == "scols":
                w = src.shape[1] // N_DEV
                src = src.at[:, pl.ds(pl.multiple_of(pidx * w, LANE), w)]
            if mode == "cols":
                w = src.shape[1]
                dst = land.at[:, pl.ds(pl.multiple_of(me * w, LANE), w)]
            else:
                dst = land.at[me]
            copies.append(pltpu.make_async_remote_copy(
                src_ref=src, dst_ref=dst, send_sem=send_sems.at[k * len(ins) + a], recv_sem=recv_sems.at[k * len(ins) + a],
                device_id=peer, device_id_type=MESH))
    return copies


def _comm_start_call(arrs, gather_flags, after, name):
    n = len(arrs)
    lands = [lax.empty(_land_shape(v.shape, mode), v.dtype) for v, mode in zip(arrs, gather_flags)]

    def body(*refs):
        ins, lnd = refs[:n], refs[n:2 * n]
        send_sems, recv_sems = refs[2 * n + len(after)], refs[2 * n + len(after) + 1]
        for cp in _comm_copies(ins, lnd, gather_flags, send_sems, recv_sems):
            cp.start()
        refs[-1][...] = jnp.zeros_like(refs[-1])

    hbm = [pltpu.with_memory_space_constraint(v, pltpu.HBM) for v in list(arrs) + lands]
    out = pl.pallas_call(
        body, name=name,
        out_shape=(pltpu.SemaphoreType.DMA((N_DEV * n,)), pltpu.SemaphoreType.DMA((N_DEV * n,)),
                   *[pltpu.HBM(v.shape, v.dtype) for v in hbm], jax.ShapeDtypeStruct((SUB, LANE), F32)),
        in_specs=[_HBM] * (2 * n) + [pl.BlockSpec(memory_space=pl.ANY)] * len(after),
        out_specs=(_SEM, _SEM, *[_HBM] * (2 * n), pl.BlockSpec(memory_space=pltpu.VMEM)),
        input_output_aliases={i: 2 + i for i in range(2 * n)},
        compiler_params=pltpu.CompilerParams(has_side_effects=pltpu.SideEffectType.DATAFLOW_SIDE_EFFECTING),
    )(*hbm, *after)
    return out[:-1], out[-1]


def _comm_wait_call(state, gather_flags, after, name):
    n = len(gather_flags)
    send_sems, recv_sems, thru = state[0], state[1], state[2:]

    def body(*refs):
        ins, lnd, ssem, rsem = refs[:n], refs[n:2 * n], refs[2 * n], refs[2 * n + 1]
        for cp in _comm_copies(ins, lnd, gather_flags, ssem, rsem):
            cp.wait_send()
            cp.wait_recv()

    out = pl.pallas_call(
        body, name=name,
        out_shape=tuple(pltpu.HBM(v.shape, v.dtype) for v in thru),
        in_specs=[_HBM] * (2 * n) + [_SEM, _SEM] + [pl.BlockSpec(memory_space=pl.ANY)] * len(after),
        out_specs=tuple([_HBM] * (2 * n)),
        input_output_aliases={i: i for i in range(2 * n)},
        compiler_params=pltpu.CompilerParams(has_side_effects=pltpu.SideEffectType.DATAFLOW_SIDE_EFFECTING),
    )(*thru, send_sems, recv_sems, *after)
    return out[n:]


def _sum8_call(recv, rows, name):
    _, r, c = recv.shape

    def body(x_ref, o_ref):
        acc = x_ref[0].astype(F32)
        for s in range(1, N_DEV):
            acc = acc + x_ref[s].astype(F32)
        o_ref[...] = acc

    return pl.pallas_call(
        body, name=name, grid=(r // rows,),
        in_specs=[pl.BlockSpec((N_DEV, rows, c), lambda i: (0, i, 0))],
        out_specs=pl.BlockSpec((rows, c), lambda i: (i, 0)),
        out_shape=jax.ShapeDtypeStruct((r, c), F32),
        compiler_params=_cparams(("parallel",)),
    )(recv)


def _adamw_call(w, g, m, v, rows, name):
    r, c = w.shape

    def body(w_ref, g_ref, m_ref, v_ref, d_ref, nm_ref, nv_ref):
        gr = g_ref[...]
        nm = ADAM_B1 * m_ref[...] + (1.0 - ADAM_B1) * gr
        nv = ADAM_B2 * v_ref[...] + (1.0 - ADAM_B2) * (gr * gr)
        m_hat = nm / (1.0 - ADAM_B1 ** ADAM_STEP)
        v_hat = nv / (1.0 - ADAM_B2 ** ADAM_STEP)
        d_ref[...] = -ADAM_LR * (m_hat / (jnp.sqrt(v_hat) + ADAM_EPS) + ADAM_WD * w_ref[...])
        nm_ref[...] = nm
        nv_ref[...] = nv

    spec = pl.BlockSpec((rows, c), lambda i: (i, 0))
    return pl.pallas_call(
        body, name=name, grid=(r // rows,),
        in_specs=[spec] * 4, out_specs=[spec] * 3,
        out_shape=[jax.ShapeDtypeStruct((r, c), F32)] * 3,
        compiler_params=_cparams(("parallel",)),
    )(w, g, m, v)


SH_ROWS = 16
SH_W = LRU_WIDTH // N_DEV
REP_ROWS = 824
_REP_SIZES = (LRU_WIDTH, 2 * 6 * 64 * 64, 2 * 6 * 64 * 64, RET_WIDTH, 1920, D_MODEL, D_MODEL, D_MODEL, D_MODEL)
_RPB_SIZE = NA_HEADS * (2 * NA_KH - 1) * (2 * NA_KW - 1)


def _pack_sh(cw, ba, bx, lam):
    return jnp.concatenate([cw, ba, bx, lam], axis=0)


def _pad_sh(p):
    pad = [(0, 0)] * (p.ndim - 2) + [(0, SH_ROWS - p.shape[-2]), (0, LANE - p.shape[-1])]
    return jnp.pad(p, pad)


def _pack_rep(cb, wa, wx, gnw, rpb, l1g, l1b, l2g, l2b):
    flat = jnp.concatenate([cb.reshape(-1), wa.reshape(-1), wx.reshape(-1), gnw.reshape(-1),
                            jnp.pad(rpb.reshape(-1), (0, 1920 - _RPB_SIZE)), l1g, l1b, l2g, l2b,
                            jnp.zeros((REP_ROWS * LANE - sum(_REP_SIZES),), F32)])
    return flat.reshape(REP_ROWS, LANE)


def _unpack_rep(p):
    nl = p.shape[0]
    flat = p.reshape(nl, -1)
    out, off = [], 0
    for size in _REP_SIZES:
        out.append(flat[:, off:off + size])
        off += size
    cb, wa, wx, gnw, rpb, l1g, l1b, l2g, l2b = out
    return (cb, wa.reshape(nl, 2, 6, 64, 64), wx.reshape(nl, 2, 6, 64, 64), gnw,
            rpb[:, :_RPB_SIZE].reshape(nl, NA_HEADS, 2 * NA_KH - 1, 2 * NA_KW - 1), l1g, l1b, l2g, l2b)


def _adamw_nd(w, g, m, v, rows, name):
    shp = w.shape
    f = lambda t: t.reshape(-1, shp[-1])
    rows = f(w).shape[0] if rows is None else rows
    return [t.reshape(shp) for t in _adamw_call(f(w), f(g), f(m), f(v), rows, name)]


def kernel(x, w_in, conv_w, conv_b, lru_w_a, lru_b_a, lru_w_x, lru_b_x, lru_lam, ret_gn_w, na_rpb, w_out, ln1_g, ln1_b, w_gate, w_up, w_down, ln2_g, ln2_b, loss_target, m_w_in, m_conv_w, m_conv_b, m_lru_w_a, m_lru_b_a, m_lru_w_x, m_lru_b_x, m_lru_lam, m_ret_gn_w, m_na_rpb, m_w_out, m_ln1_g, m_ln1_b, m_w_gate, m_w_up, m_w_down, m_ln2_g, m_ln2_b, v_w_in, v_conv_w, v_conv_b, v_lru_w_a, v_lru_b_a, v_lru_w_x, v_lru_b_x, v_lru_lam, v_ret_gn_w, v_na_rpb, v_w_out, v_ln1_g, v_ln1_b, v_w_gate, v_w_up, v_w_down, v_ln2_g, v_ln2_b):
    nl = w_in.shape[0]
    T = x.shape[1]
    rows_n = T // GRID_W
    x0, target = x[0], loss_target[0]
    ffpad = W_BLK - FF_BLK

    win_b = w_in.astype(_BF)
    wg_b = jnp.pad(w_gate, ((0, 0), (0, 0), (0, ffpad))).astype(_BF)
    wu_b = jnp.pad(w_up, ((0, 0), (0, 0), (0, ffpad))).astype(_BF)
    wd_b = jnp.pad(w_down, ((0, 0), (0, ffpad), (0, 0))).astype(_BF)
    wout_b = w_out.astype(_BF)
    def agf_start(l, after):
        sh = _pad_sh(_pack_sh(conv_w[l], lru_b_a[l], lru_b_x[l], lru_lam[l]))
        arrs, modes = [win_b[l], sh], ["cols", "all"]
        if l > 0:
            arrs, modes = arrs + [wd_b[l]], modes + ["all"]
        return _comm_start_call(arrs, modes, after, f"agf_start{l}"), modes

    def agk_start(l, after):
        arrs, modes = [wg_b[l], wu_b[l], wout_b[l]], ["cols", "cols", "all"]
        if l == 0:
            arrs, modes = arrs + [wd_b[l]], modes + ["all"]
        return _comm_start_call(arrs, modes, after, f"agk_start{l}"), modes

    tables = _ret_tables(T)
    w4_all = _lru_w4(lru_w_a, lru_w_x)
    layers = []
    gathered = []
    xs, xb = x0, x0.astype(_BF)
    (agf_state, token), agf_modes = agf_start(0, [])
    tie = 0.0 * token[0, 0]
    btabs = [_na_bias_tables(na_rpb[l] + tie, rows_n) for l in range(nl)]
    for l in range(nl):
        front = _comm_wait_call(agf_state, agf_modes, [xb] + (btabs if l == 0 else []), f"agf_wait{l}")
        win, shg = front[0], front[1]
        (agk_state, token), agk_modes = agk_start(l, [shg])
        full = shg[:, :10, :SH_W].transpose(1, 0, 2).reshape(10, LRU_WIDTH)
        vec, w4 = _lru_vec(full[0:4], conv_b[l], full[4:6], full[6:8], full[8:10]), w4_all[l]
        gnw8 = jnp.pad(ret_gn_w[l][None], ((0, SUB - 1), (0, 0)))
        btab = btabs[l]
        proj = _inproj_call(xb, win, token)
        y_lru = _lru_fwd_call(proj, vec, w4)
        y_ret = _ret_fwd_call(proj, tables, gnw8)
        y_na = _na_fwd_call(proj, btab)
        back = _comm_wait_call(agk_state, agk_modes, [y_na], f"agk_wait{l}")
        wg, wu, wout = back[0], back[1], back[2]
        wd = (back[3] if l == 0 else front[2]).reshape(IN_WIDTH, D_MODEL)
        wout = wout.reshape(D_MODEL, D_MODEL)
        gathered.append((win, wg, wu, wd, wout))
        if l + 1 < nl:
            (agf_state, token), agf_modes = agf_start(l + 1, [wout])
        z1, x1, x1b, ycb = _outproj_ln_call(y_lru, y_ret, y_na, xs, wout, ln1_g[l][None], ln1_b[l][None], token)
        z2, x2, x2b, gpb, upb = _ffn_ln_call(x1, x1b, wg, wu, wd, ln2_g[l][None], ln2_b[l][None])
        layers.append(dict(xb=xb, proj=proj, vec=vec, w4=w4, gnw8=gnw8, btab=btab,
                           z1=z1, x1b=x1b, ycb=ycb, z2=z2, gpb=gpb, upb=upb))
        xs, xb = x2, x2b

    dx, loss_blk = _loss_call(xs, target)
    loss = lax.psum(loss_blk[0, 0], ("x", "y", "c"))

    gxa_flags = ["scols", "scols", "blk", "blk"]
    gxb_flags = ["scols", "blk", "all"]
    gxa_state, gxb_state = [None] * nl, [None] * nl
    token = loss_blk
    for l in reversed(range(nl)):
        s = layers[l]
        win, wg, wu, wd, wout = gathered[l]
        dx1, dgp, dup, hid, dz2b, dln2 = _ffn_bwd_call(dx, s["z2"], s["gpb"], s["upb"], wg, wu, wd, ln2_g[l][None], token)
        dwg = _tn_cols_call(s["x1b"], dgp, "tn_cols")
        dwu = _tn_cols_call(s["x1b"], dup, "tn_cols")
        dwd = _tn_rows_call(hid, dz2b, N_BLK, "tn_rows_down").reshape(N_DEV, W_BLK, D_MODEL)
        dz1b, dyc, dres, dln1 = _outproj_bwd_call(dx1, s["z1"], wout, ln1_g[l][None])
        dwout = _tn_rows_call(s["ycb"], dz1b, D_MODEL // 2, "tn_rows_out").reshape(N_DEV, LANE, D_MODEL)
        gxa_state[l], token = _comm_start_call([dwg, dwu, dwd, dwout], gxa_flags, [], f"gxa_start{l}")
        dp, dvec, dw4 = _lru_bwd_call(s["proj"], dyc, s["vec"], s["w4"], token)
        dp, dgnw = _ret_bwd_call(s["proj"], dyc, tables, s["gnw8"], dp)
        dp, dbias = _na_bwd_call(s["proj"], dyc, s["btab"], dp)
        dwin = _tn_cols_call(s["xb"], dp, "tn_cols")
        dx = _inproj_bwd_call(dres, dp, win)
        dcw, dcb, dwa, dba, dwx, dbx, dlam = _lru_unpack(dvec, dw4)
        rep = _pack_rep(dcb, dwa, dwx, dgnw[0], _na_bias_grad(dbias, rows_n), dln1[0], dln1[1], dln2[0], dln2[1])
        sh = _pack_sh(dcw, dba, dbx, dlam).reshape(10, N_DEV, SH_W).transpose(1, 0, 2)
        gxb_state[l], token = _comm_start_call([dwin, _pad_sh(sh), rep], gxb_flags, [], f"gxb_start{l}")

    g_big = [[None] * nl for _ in range(5)]
    g_sh = [None] * nl
    g_rep = [None] * nl
    after = [dx, token]
    for l in reversed(range(nl)):
        ra = _comm_wait_call(gxa_state[l], gxa_flags, after, f"gxa_wait{l}")
        g_big[1][l] = _sum8_call(ra[0], TM, "sum8_cols")[:, :FF_BLK]
        g_big[2][l] = _sum8_call(ra[1], TM, "sum8_cols")[:, :FF_BLK]
        g_big[3][l] = _sum8_call(ra[2], W_BLK, "sum8_down")[:FF_BLK]
        g_big[4][l] = _sum8_call(ra[3], LANE, "sum8_out")
        rb = _comm_wait_call(gxb_state[l], gxb_flags, [g_big[4][l]], f"gxb_wait{l}")
        g_big[0][l] = _sum8_call(rb[0], TM, "sum8_cols")
        g_sh[l] = _sum8_call(rb[1], SH_ROWS, "sum8_sh")
        g_rep[l] = _sum8_call(rb[2], REP_ROWS, "sum8_rep")
        after = [g_rep[l]]

    g_w_in, g_w_gate, g_w_up, g_w_down, g_w_out = [jnp.stack(t) for t in g_big]
    big = {
        "w_in": _adamw_nd(w_in, g_w_in, m_w_in, v_w_in, TM, "adamw_in"),
        "w_gate": _adamw_nd(w_gate, g_w_gate, m_w_gate, v_w_gate, TM, "adamw_ff"),
        "w_up": _adamw_nd(w_up, g_w_up, m_w_up, v_w_up, TM, "adamw_ff"),
        "w_down": _adamw_nd(w_down, g_w_down, m_w_down, v_w_down, FF_BLK, "adamw_down"),
        "w_out": _adamw_nd(w_out, g_w_out, m_w_out, v_w_out, LANE, "adamw_out"),
    }
    g_shp = jnp.stack(g_sh)[:, :, :SH_W]
    rep_names = ("conv_b", "lru_w_a", "lru_w_x", "ret_gn_w", "na_rpb", "ln1_g", "ln1_b", "ln2_g", "ln2_b")
    grads = {"w_in": g_w_in, "w_gate": g_w_gate, "w_up": g_w_up, "w_down": g_w_down, "w_out": g_w_out,
             "conv_w": g_shp[:, 0:4], "lru_b_a": g_shp[:, 4:6], "lru_b_x": g_shp[:, 6:8], "lru_lam": g_shp[:, 8:10]}
    grads.update(dict(zip(rep_names, _unpack_rep(jnp.stack(g_rep)))))
    small = {
        "conv_w": (conv_w, m_conv_w, v_conv_w), "conv_b": (conv_b, m_conv_b, v_conv_b),
        "lru_w_a": (lru_w_a, m_lru_w_a, v_lru_w_a), "lru_b_a": (lru_b_a, m_lru_b_a, v_lru_b_a),
        "lru_w_x": (lru_w_x, m_lru_w_x, v_lru_w_x), "lru_b_x": (lru_b_x, m_lru_b_x, v_lru_b_x),
        "lru_lam": (lru_lam, m_lru_lam, v_lru_lam), "ret_gn_w": (ret_gn_w, m_ret_gn_w, v_ret_gn_w),
        "na_rpb": (na_rpb, m_na_rpb, v_na_rpb), "ln1_g": (ln1_g, m_ln1_g, v_ln1_g), "ln1_b": (ln1_b, m_ln1_b, v_ln1_b),
        "ln2_g": (ln2_g, m_ln2_g, v_ln2_g), "ln2_b": (ln2_b, m_ln2_b, v_ln2_b),
    }
    for name, (w_, m_, v_) in small.items():
        big[name] = _adamw_nd(w_, grads[name], m_, v_, None, "adamw_small")
    kinds = [{n: big[n][k] for n in big} for k in range(3)]
    order = ("w_in", "conv_w", "conv_b", "lru_w_a", "lru_b_a", "lru_w_x", "lru_b_x", "lru_lam", "ret_gn_w", "na_rpb",
             "w_out", "ln1_g", "ln1_b", "w_gate", "w_up", "w_down", "ln2_g", "ln2_b")
    outs = [loss, dx[None]]
    for d in (grads, *kinds):
        outs.extend(d[n] for n in order)
    return tuple(outs)
```

```python
import functools
import math

import numpy as np
import jax
import jax.numpy as jnp
from jax import lax
from jax.experimental import pallas as pl
from jax.experimental.pallas import tpu as pltpu

F32 = jnp.float32
_BF = jnp.bfloat16

D_MODEL = 1024
DEPTH = 4
GRID_W = 64
HEAD_DIM = 64
LRU_WIDTH = 384
RET_WIDTH = 384
RET_HEADS = 6
NA_WIDTH = 256
NA_HEADS = 4
IN_WIDTH = 3072
CONV_WIDTH = 4
LRU_C = 8.0
RET_CHUNK = 128
ROPE_BASE = 10000.0
GN_EPS = 1e-6
NA_KH = 8
NA_KW = 16
D_FF = 2816
FF_BLK = 352
N_DEV = 8
ALPHA = (2 * DEPTH) ** 0.25
LN_EPS = 1e-5
ADAM_LR = 0.001
ADAM_B1 = 0.9
ADAM_B2 = 0.999
ADAM_EPS = 1e-08
ADAM_WD = 0.01
ADAM_STEP = 10

LANE = 128
SUB = 8
VMEM_MB = 56
NEG = -1e30

MESH = pl.DeviceIdType.MESH


def _cparams(sem=None, vmem_mb=VMEM_MB):
    return pltpu.CompilerParams(dimension_semantics=sem, vmem_limit_bytes=vmem_mb << 20)


def _mm(a, b):
    return jnp.dot(a.astype(_BF), b.astype(_BF), preferred_element_type=F32)


def _mm_nt(a, b):
    return lax.dot_general(a.astype(_BF), b.astype(_BF), (((1,), (1,)), ((), ())), preferred_element_type=F32)


def _mm_tn(a, b):
    return lax.dot_general(a.astype(_BF), b.astype(_BF), (((0,), (0,)), ((), ())), preferred_element_type=F32)


def _sigmoid(x):
    return jax.nn.sigmoid(x)


def _rows(start, size):
    return pl.ds(pl.multiple_of(start, SUB), size)


def _loop2(n, body, init):
    assert n % 2 == 0
    return lax.fori_loop(0, n // 2, lambda i, c: body(2 * i + 1, body(2 * i, c)), init)


def _strip(T, col, buffers=2):
    return pl.BlockSpec((T, LANE), lambda j: (0, col(j)), pipeline_mode=pl.Buffered(buffers))


LRU_CH = 256
_GELU_C0 = math.sqrt(2.0 / math.pi)
_GELU_C1 = 0.044715


def _gelu_parts(x):
    x2 = x * x
    t = jnp.tanh(_GELU_C0 * (x + _GELU_C1 * x * x2))
    val = 0.5 * x * (1.0 + t)
    der = 0.5 * (1.0 + t) + 0.5 * x * (1.0 - t * t) * _GELU_C0 * (1.0 + 3.0 * _GELU_C1 * x2)
    return val, der


def _softplus_neg(lam):
    e = jnp.exp(-jnp.abs(lam))
    w = 1.0 + e
    l1p = jnp.where(w == 1.0, e, jnp.log(w) * (e / jnp.where(w == 1.0, 1.0, w - 1.0)))
    return jnp.maximum(-lam, 0.0) + l1p


def _window(ref, t0, ch, T):
    prev = ref[_rows(jnp.maximum(t0 - SUB, 0), SUB), :].astype(F32)
    nxt = ref[_rows(jnp.minimum(t0 + ch, T - SUB), SUB), :].astype(F32)
    prev = jnp.where(t0 > 0, prev, 0.0)
    nxt = jnp.where(t0 + ch < T, nxt, 0.0)
    return jnp.concatenate([prev, ref[_rows(t0, ch), :].astype(F32), nxt], axis=0)


def _tap(win, shift, ch):
    n = win.shape[0]
    return pltpu.roll(win, (-shift) % n, 0)[SUB:SUB + ch]


def _lru_conv(xb_ref, vec, t0, T):
    win = _window(xb_ref, t0, LRU_CH, T)
    xc = jnp.broadcast_to(vec[4:5, :], (LRU_CH, LANE))
    for j in range(CONV_WIDTH):
        xc = xc + _tap(win, j - CONV_WIDTH // 2, LRU_CH) * vec[j:j + 1, :]
    return xc


def _lru_dir(pre_a, pre_x, sp):
    r = _sigmoid(pre_a)
    i = _sigmoid(pre_x)
    log_a = (-LRU_C) * r * sp
    a = jnp.exp(log_a)
    z = jnp.tanh(-log_a) * (a * a + 1.0)
    s = jnp.sqrt(z)
    return r, i, a, s


def _scan_tile(a, b, reverse, row):
    for k in (1, 2, 4):
        if not reverse:
            a_s, b_s, m = pltpu.roll(a, k, 0), pltpu.roll(b, k, 0), row >= k
        else:
            a_s, b_s, m = pltpu.roll(a, SUB - k, 0), pltpu.roll(b, SUB - k, 0), row < SUB - k
        b = jnp.where(m, a * b_s + b, b)
        a = jnp.where(m, a * a_s, a)
    return a, b


def _bcast_row(x, r):
    return jnp.broadcast_to(x[r:r + 1, :], (SUB, LANE))


def _lru_prepare(xb_ref, w4_ref, vec, xc_ref, af_ref, uf_ref, ab_ref, ub_ref, T):
    sp_f = _softplus_neg(vec[9:10, :])
    sp_b = _softplus_neg(vec[10:11, :])
    w4 = w4_ref[0]

    def body(c, carry):
        t0 = c * LRU_CH
        xc = _lru_conv(xb_ref, vec, t0, T)
        if xc_ref is not None:
            xc_ref[_rows(t0, LRU_CH), :] = xc
        pre = _mm(xc, w4)
        _, i, a, s = _lru_dir(pre[:, 0:128] + vec[5:6, :], pre[:, 128:256] + vec[6:7, :], sp_f)
        af_ref[_rows(t0, LRU_CH), :] = a
        uf_ref[_rows(t0, LRU_CH), :] = s * (i * xc)
        _, i, a, s = _lru_dir(pre[:, 256:384] + vec[7:8, :], pre[:, 384:512] + vec[8:9, :], sp_b)
        ab_ref[_rows(t0, LRU_CH), :] = a
        ub_ref[_rows(t0, LRU_CH), :] = s * (i * xc)
        return carry

    lax.fori_loop(0, T // LRU_CH, body, 0)


def _lru_scan(af_ref, uf_ref, ab_ref, ub_ref, T):
    nt = T // SUB
    row = lax.broadcasted_iota(jnp.int32, (SUB, LANE), 0)

    def body(j, carry):
        hf, hb = carry
        sf = _rows(j * SUB, SUB)
        sb = _rows((nt - 1 - j) * SUB, SUB)
        a, b = _scan_tile(af_ref[sf, :], uf_ref[sf, :], False, row)
        h = a * hf + b
        uf_ref[sf, :] = h
        hf = _bcast_row(h, SUB - 1)
        a, b = _scan_tile(ab_ref[sb, :], ub_ref[sb, :], True, row)
        h = a * hb + b
        ub_ref[sb, :] = h
        hb = _bcast_row(h, 0)
        return hf, hb

    z = jnp.zeros((SUB, LANE), F32)
    lax.fori_loop(0, nt, body, (z, z))


def _lru_fwd_call(proj, vec, w4):
    T = proj.shape[0]

    def body(xb_ref, gate_ref, vec_ref, w4_ref, y_ref, af_ref, uf_ref, ab_ref, ub_ref):
        vec = vec_ref[...]
        _lru_prepare(xb_ref, w4_ref, vec, None, af_ref, uf_ref, ab_ref, ub_ref, T)
        _lru_scan(af_ref, uf_ref, ab_ref, ub_ref, T)

        def out(c, carry):
            rows = _rows(c * LRU_CH, LRU_CH)
            gl, _ = _gelu_parts(gate_ref[rows, :])
            y_ref[rows, :] = (uf_ref[rows, :] + ub_ref[rows, :]) * gl
            return carry

        lax.fori_loop(0, T // LRU_CH, out, 0)

    return pl.pallas_call(
        body, name="lru_fwd", grid=(LRU_WIDTH // LANE,),
        in_specs=[_strip(T, lambda j: j), _strip(T, lambda j: j + 3),
                  pl.BlockSpec((16, LANE), lambda j: (0, j)),
                  pl.BlockSpec((1, LANE, 4 * LANE), lambda j: (j, 0, 0))],
        out_specs=_strip(T, lambda j: j, buffers=1),
        out_shape=jax.ShapeDtypeStruct((T, LRU_WIDTH), F32),
        scratch_shapes=[pltpu.VMEM((T, LANE), F32)] * 4,
        compiler_params=_cparams(("arbitrary",)),
    )(proj, proj, vec, w4)


def _store_strips(stage_ref, dp_ref, cols, sems):
    copies = [pltpu.make_async_copy(stage_ref.at[b], dp_ref.at[:, pl.ds(pl.multiple_of(c * LANE, LANE), LANE)], sems.at[b])
              for b, c in enumerate(cols)]
    for cp in copies:
        cp.start()
    for cp in copies:
        cp.wait()


def _lru_bwd_call(proj, dycat, vec, w4, after):
    T = proj.shape[0]
    nt = T // SUB
    nch = T // LRU_CH

    def body(xb_ref, gate_ref, dy_ref, vec_ref, w4_ref, after_ref, dp_ref, dvec_ref, dw4_ref,
             xc_ref, af_ref, hf_ref, ab_ref, hb_ref, dh_ref, stage_ref, sems):
        dxb_ref, dgate_ref = stage_ref.at[0], stage_ref.at[1]
        vec = vec_ref[...]
        _lru_prepare(xb_ref, w4_ref, vec, xc_ref, af_ref, hf_ref, ab_ref, hb_ref, T)
        _lru_scan(af_ref, hf_ref, ab_ref, hb_ref, T)

        def gate_bwd(c, carry):
            rows = _rows(c * LRU_CH, LRU_CH)
            gl, dgl = _gelu_parts(gate_ref[rows, :])
            dy = dy_ref[rows, :]
            dgate_ref[rows, :] = (dy * (hf_ref[rows, :] + hb_ref[rows, :]) * dgl).astype(dgate_ref.dtype)
            dh_ref[rows, :] = dy * gl
            return carry

        lax.fori_loop(0, nch, gate_bwd, 0)

        row = lax.broadcasted_iota(jnp.int32, (SUB, LANE), 0)

        def adj(j, carry):
            gf, a_next, gb, a_prev = carry
            tf = nt - 1 - j
            sf = _rows(tf * SUB, SUB)
            a_t = af_ref[sf, :]
            h_t = hf_ref[sf, :]
            coef = jnp.where(row == SUB - 1, a_next, pltpu.roll(a_t, SUB - 1, 0))
            ac, bc = _scan_tile(coef, dh_ref[sf, :], True, row)
            g = ac * gf + bc
            h_prev = hf_ref[_rows(jnp.maximum(tf - 1, 0) * SUB, SUB), :]
            h_prev = jnp.where(tf > 0, _bcast_row(h_prev, SUB - 1), 0.0)
            hs = jnp.where(row == 0, h_prev, pltpu.roll(h_t, 1, 0))
            af_ref[sf, :] = g * hs
            hf_ref[sf, :] = g
            gf = _bcast_row(g, 0)
            a_next = _bcast_row(a_t, 0)
            sb = _rows(j * SUB, SUB)
            a_t = ab_ref[sb, :]
            h_t = hb_ref[sb, :]
            coef = jnp.where(row == 0, a_prev, pltpu.roll(a_t, 1, 0))
            ac, bc = _scan_tile(coef, dh_ref[sb, :], False, row)
            g = ac * gb + bc
            h_next = hb_ref[_rows(jnp.minimum(j + 1, nt - 1) * SUB, SUB), :]
            h_next = jnp.where(j < nt - 1, _bcast_row(h_next, 0), 0.0)
            hs = jnp.where(row == SUB - 1, h_next, pltpu.roll(h_t, SUB - 1, 0))
            ab_ref[sb, :] = g * hs
            hb_ref[sb, :] = g
            gb = _bcast_row(g, SUB - 1)
            a_prev = _bcast_row(a_t, SUB - 1)
            return gf, a_next, gb, a_prev

        z = jnp.zeros((SUB, LANE), F32)
        lax.fori_loop(0, nt, adj, (z, z, z, z))

        sp_f = _softplus_neg(vec[9:10, :])
        sp_b = _softplus_neg(vec[10:11, :])
        w4 = w4_ref[0]
        dw4_ref[...] = jnp.zeros_like(dw4_ref)

        def one_dir(pre_a, pre_x, sp, xc, du, da):
            r, i, a, s = _lru_dir(pre_a, pre_x, sp)
            d_i = du * s * xc
            dxc = du * s * i
            d_s = du * i * xc
            d_log = da * a - d_s * (a * a) / s
            d_r = d_log * (-LRU_C) * sp
            d_sp = jnp.sum(d_log * (-LRU_C) * r, axis=0, keepdims=True)
            return d_r * r * (1.0 - r), d_i * i * (1.0 - i), dxc, d_sp

        def gates_bwd(c, carry):
            db, dspf, dspb = carry
            rows = _rows(c * LRU_CH, LRU_CH)
            xc = xc_ref[rows, :]
            pre = _mm(xc, w4)
            dpa_f, dpx_f, dxc_f, d_sp_f = one_dir(pre[:, 0:128] + vec[5:6, :], pre[:, 128:256] + vec[6:7, :],
                                                  sp_f, xc, hf_ref[rows, :], af_ref[rows, :])
            dpa_b, dpx_b, dxc_b, d_sp_b = one_dir(pre[:, 256:384] + vec[7:8, :], pre[:, 384:512] + vec[8:9, :],
                                                  sp_b, xc, hb_ref[rows, :], ab_ref[rows, :])
            dpre = jnp.concatenate([dpa_f, dpx_f, dpa_b, dpx_b], axis=1)
            dw4_ref[0] += _mm_tn(xc, dpre)
            dh_ref[rows, :] = dxc_f + dxc_b + _mm_nt(dpre, w4)
            return db + jnp.sum(dpre, axis=0, keepdims=True), dspf + d_sp_f, dspb + d_sp_b

        z1 = jnp.zeros((1, LANE), F32)
        db, dspf, dspb = lax.fori_loop(0, nch, gates_bwd, (jnp.zeros((1, 4 * LANE), F32), z1, z1))

        def conv_bwd(c, carry):
            t0 = c * LRU_CH
            rows = _rows(t0, LRU_CH)
            dwin = _window(dh_ref, t0, LRU_CH, T)
            xwin = _window(xb_ref, t0, LRU_CH, T)
            dxc = dh_ref[rows, :]
            dxb = jnp.zeros((LRU_CH, LANE), F32)
            out = []
            for j in range(CONV_WIDTH):
                off = j - CONV_WIDTH // 2
                dxb = dxb + _tap(dwin, -off, LRU_CH) * vec[j:j + 1, :]
                out.append(carry[j] + jnp.sum(dxc * _tap(xwin, off, LRU_CH), axis=0, keepdims=True))
            dxb_ref[rows, :] = dxb.astype(dxb_ref.dtype)
            out.append(carry[CONV_WIDTH] + jnp.sum(dxc, axis=0, keepdims=True))
            return tuple(out)

        dconv = lax.fori_loop(0, nch, conv_bwd, (z1,) * (CONV_WIDTH + 1))
        dlam_f = dspf * (-_sigmoid(-vec[9:10, :]))
        dlam_b = dspb * (-_sigmoid(-vec[10:11, :]))
        dvec_ref[...] = jnp.concatenate(
            list(dconv) + [db[:, 0:128], db[:, 128:256], db[:, 256:384], db[:, 384:512], dlam_f, dlam_b,
                           jnp.zeros((5, LANE), F32)], axis=0)
        j = pl.program_id(0)
        _store_strips(stage_ref, dp_ref, (j, j + 3), sems)

    ns = LRU_WIDTH // LANE
    return pl.pallas_call(
        body, name="lru_bwd", grid=(ns,),
        in_specs=[_strip(T, lambda j: j), _strip(T, lambda j: j + 3), _strip(T, lambda j: j),
                  pl.BlockSpec((16, LANE), lambda j: (0, j)),
                  pl.BlockSpec((1, LANE, 4 * LANE), lambda j: (j, 0, 0)),
                  pl.BlockSpec(memory_space=pl.ANY)],
        out_specs=[pl.BlockSpec(memory_space=pl.ANY),
                   pl.BlockSpec((16, LANE), lambda j: (0, j)),
                   pl.BlockSpec((1, LANE, 4 * LANE), lambda j: (j, 0, 0))],
        out_shape=[jax.ShapeDtypeStruct((T, IN_WIDTH), _BF),
                   jax.ShapeDtypeStruct((16, LRU_WIDTH), F32), jax.ShapeDtypeStruct((ns, LANE, 4 * LANE), F32)],
        scratch_shapes=[pltpu.VMEM((T, LANE), F32)] * 6 + [pltpu.VMEM((2, T, LANE), _BF), pltpu.SemaphoreType.DMA((2,))],
        compiler_params=_cparams(("arbitrary",)),
    )(proj, proj, dycat, vec, w4, after)


def _lru_vec(cw, cb, ba, bx, lam):
    return jnp.concatenate([cw, cb[None], ba[0:1], bx[0:1], ba[1:2], bx[1:2], lam, jnp.zeros((5, LRU_WIDTH), F32)], axis=0)


def _lru_w4(wa, wx):
    nl = wa.shape[0]
    w = jnp.stack([wa[:, 0], wx[:, 0], wa[:, 1], wx[:, 1]], axis=1)
    w = w.reshape(nl, 4, 3, 2, 64, 64)
    eye = jnp.eye(2, dtype=w.dtype)
    bd = w[:, :, :, :, :, None, :] * eye[None, None, None, :, None, :, None]
    bd = bd.reshape(nl, 4, 3, LANE, LANE)
    return bd.transpose(0, 2, 3, 1, 4).reshape(nl, 3, LANE, 4 * LANE).astype(_BF)


def _lru_unpack(dvec, dw4):
    def blocks(m):
        m = m.reshape(3, 2, 64, 2, 64)
        return jnp.stack([m[:, 0, :, 0, :], m[:, 1, :, 1, :]], axis=1).reshape(6, 64, 64)
    parts = [blocks(dw4[:, :, k * LANE:(k + 1) * LANE]) for k in range(4)]
    dwa = jnp.stack([parts[0], parts[2]])
    dwx = jnp.stack([parts[1], parts[3]])
    dba = jnp.stack([dvec[5], dvec[7]])
    dbx = jnp.stack([dvec[6], dvec[8]])
    return dvec[0:4], dvec[4], dwa, dba, dwx, dbx, dvec[9:11]


RC = 2 * RET_CHUNK


def _ret_tables(T):
    half = HEAD_DIM // 2
    pos = jnp.arange(T, dtype=F32)
    inv_freq = ROPE_BASE ** (-jnp.arange(half, dtype=F32) / half)
    ang = pos[:, None] * inv_freq[None, :]
    cos = jnp.tile(jnp.cos(ang), (1, 4))
    sin = jnp.tile(jnp.concatenate([-jnp.sin(ang), jnp.sin(ang)], axis=1), (1, 2))
    log_g = jnp.log1p(-jnp.exp2(-5.0 - jnp.arange(RET_HEADS, dtype=F32)))
    idx = jnp.arange(RC, dtype=F32)
    dec = jnp.exp(jnp.abs(idx[:, None] - idx[None, :]) * log_g[:, None, None])
    lg = jnp.repeat(log_g, HEAD_DIM).reshape(3, 1, LANE)
    col = idx[None, :, None]
    rtab = jnp.stack([jnp.exp((RC - 1 - col) * lg), jnp.exp(col * lg),
                      jnp.exp((col + 1.0) * lg), jnp.exp((RC - col) * lg)], axis=1)
    gch = jnp.broadcast_to(jnp.exp(RC * lg), (3, SUB, LANE))
    return cos, sin, dec, rtab, gch


def _swap32(x, lane):
    return jnp.where((lane & 32) == 0, pltpu.roll(x, LANE - 32, 1), pltpu.roll(x, 32, 1))


def _head_mean(x, m0, m1):
    s0 = jnp.sum(x * m0, axis=-1, keepdims=True)
    s1 = jnp.sum(x * m1, axis=-1, keepdims=True)
    return (s0 * m0 + s1 * m1) * (1.0 / HEAD_DIM)


def _ret_masks():
    lane = lax.broadcasted_iota(jnp.int32, (RC, LANE), 1)
    m0 = (lane < HEAD_DIM).astype(F32)
    r = lax.broadcasted_iota(jnp.int32, (LANE, LANE), 0) // HEAD_DIM
    c = lax.broadcasted_iota(jnp.int32, (LANE, LANE), 1) // HEAD_DIM
    return lane, m0, 1.0 - m0, (r == c).astype(F32)


def _ret_specs(T):
    const = lambda shape, imap: pl.BlockSpec(shape, imap)
    return [_strip(T, lambda j: j + 6), _strip(T, lambda j: j + 9), _strip(T, lambda j: j + 12),
            _strip(T, lambda j: j + 15),
            pl.BlockSpec((T, LANE), lambda j: (0, 0), pipeline_mode=pl.Buffered(1)),
            pl.BlockSpec((T, LANE), lambda j: (0, 0), pipeline_mode=pl.Buffered(1)),
            const((2, RC, RC), lambda j: (j, 0, 0)),
            const((1, 4, RC, LANE), lambda j: (j, 0, 0, 0)),
            const((1, SUB, LANE), lambda j: (j, 0, 0)),
            const((SUB, LANE), lambda j: (0, j))]


def _ret_fwd_call(proj, tables, gnw8):
    T = proj.shape[0]
    nc = T // RC
    cos, sin, dec, rtab, gch = tables

    def body(q_ref, k_ref, v_ref, g_ref, cos_ref, sin_ref, dec_ref, rtab_ref, gch_ref, gnw_ref, y_ref, stf_ref, kr_ref):
        lane, m0, m1, bd = _ret_masks()
        gch_v = gch_ref[0][0:1, :]
        gnw = gnw_ref[0:1, :]
        dkf, dkb, dqf, dqb = rtab_ref[0, 0], rtab_ref[0, 1], rtab_ref[0, 2], rtab_ref[0, 3]

        def rope(x, rows):
            return x * cos_ref[rows, :] + _swap32(x, lane) * sin_ref[rows, :]

        def pass_a(n, st):
            rows = _rows(n * RC, RC)
            stf_ref[n] = st
            kr = rope(k_ref[rows, :], rows) * (HEAD_DIM ** -0.5)
            kr_ref[rows, :] = kr
            return gch_v * st + _mm_tn(kr * dkf, v_ref[rows, :]) * bd

        _loop2(nc, pass_a, jnp.zeros((LANE, LANE), F32))

        def pass_b(i, stb):
            ns = [nc - 1 - 2 * i, nc - 2 - 2 * i]
            rows = [_rows(n * RC, RC) for n in ns]
            heads = ((0, m0), (1, m1))
            qr = [rope(q_ref[r, :], r) for r in rows]
            kr = [kr_ref[r, :] for r in rows]
            v = [v_ref[r, :] for r in rows]
            kv = [_mm_tn(kr[c] * dkb, v[c]) * bd for c in range(2)]
            stbs = [stb, gch_v * stb + kv[0]]
            s = [[_mm_nt(qr[c] * m, kr[c]) * dec_ref[h] for h, m in heads] for c in range(2)]
            o = [_mm(qr[c] * dqf, stf_ref[ns[c]]) + _mm(qr[c] * dqb, stbs[c]) for c in range(2)]
            o = [o[c] + _mm(s[c][0], v[c] * m0) + _mm(s[c][1], v[c] * m1) for c in range(2)]
            oc = [o_ - _head_mean(o_, m0, m1) for o_ in o]
            on = [oc_ * lax.rsqrt(_head_mean(oc_ * oc_, m0, m1) + GN_EPS) for oc_ in oc]
            for c in range(2):
                g = g_ref[rows[c], :]
                y_ref[rows[c], :] = (g * _sigmoid(g)) * (on[c] * gnw)
            return gch_v * stbs[1] + kv[1]

        assert nc % 2 == 0
        lax.fori_loop(0, nc // 2, pass_b, jnp.zeros((LANE, LANE), F32))

    return pl.pallas_call(
        body, name="ret_fwd", grid=(RET_WIDTH // LANE,),
        in_specs=_ret_specs(T),
        out_specs=_strip(T, lambda j: j, buffers=1),
        out_shape=jax.ShapeDtypeStruct((T, RET_WIDTH), F32),
        scratch_shapes=[pltpu.VMEM((nc, LANE, LANE), F32), pltpu.VMEM((T, LANE), F32)],
        compiler_params=_cparams(("arbitrary",)),
    )(proj, proj, proj, proj, cos, sin, dec, rtab, gch, gnw8)


def _ret_bwd_call(proj, dycat, tables, gnw8, dp):
    T = proj.shape[0]
    nc = T // RC
    cos, sin, dec, rtab, gch = tables

    def body(q_ref, k_ref, v_ref, g_ref, cos_ref, sin_ref, dec_ref, rtab_ref, gch_ref, gnw_ref, dy_ref, dp_in_ref,
             dp_out_ref, dgnw_ref, stf_ref, dstb_ref, dkr_ref, dv_ref, dp_ref, kr_ref, sems):
        lane, m0, m1, bd = _ret_masks()
        gch_v = gch_ref[0][0:1, :]
        gnw = gnw_ref[0:1, :]
        dkf, dkb, dqf, dqb = rtab_ref[0, 0], rtab_ref[0, 1], rtab_ref[0, 2], rtab_ref[0, 3]
        scale = HEAD_DIM ** -0.5
        zst = jnp.zeros((LANE, LANE), F32)

        def rope(x, rows):
            return x * cos_ref[rows, :] + _swap32(x, lane) * sin_ref[rows, :]

        def rope_t(d, rows):
            return d * cos_ref[rows, :] + _swap32(d * sin_ref[rows, :], lane)

        def pass_a(n, st):
            rows = _rows(n * RC, RC)
            stf_ref[n] = st
            kr = rope(k_ref[rows, :], rows) * scale
            kr_ref[rows, :] = kr
            return gch_v * st + _mm_tn(kr * dkf, v_ref[rows, :]) * bd

        _loop2(nc, pass_a, zst)

        def pass_b(i, carry):
            stb, d_f, dgnw = carry
            two = range(2)
            heads = ((0, m0), (1, m1))
            ns = [nc - 1 - 2 * i, nc - 2 - 2 * i]
            rows = [_rows(n * RC, RC) for n in ns]
            qr = [rope(q_ref[r, :], r) for r in rows]
            kr = [kr_ref[r, :] for r in rows]
            v = [v_ref[r, :] for r in rows]
            stf = [stf_ref[n] for n in ns]
            kvb = [_mm_tn(kr[c] * dkb, v[c]) * bd for c in two]
            stbs = [stb, gch_v * stb + kvb[0]]
            qf = [qr[c] * dqf for c in two]
            qb = [qr[c] * dqb for c in two]
            s = [[_mm_nt(qr[c] * m, kr[c]) * dec_ref[h] for h, m in heads] for c in two]
            o = [_mm(qf[c], stf[c]) + _mm(qb[c], stbs[c]) for c in two]
            o = [o[c] + _mm(s[c][0], v[c] * m0) + _mm(s[c][1], v[c] * m1) for c in two]
            oc = [o_ - _head_mean(o_, m0, m1) for o_ in o]
            rstd = [lax.rsqrt(_head_mean(oc_ * oc_, m0, m1) + GN_EPS) for oc_ in oc]
            on = [oc[c] * rstd[c] for c in two]
            do = []
            for c in two:
                g = g_ref[rows[c], :]
                sg = _sigmoid(g)
                dy = dy_ref[rows[c], :]
                dp_ref[3, rows[c], :] = (dy * (on[c] * gnw) * (sg * (1.0 + g * (1.0 - sg)))).astype(dp_ref.dtype)
                t = dy * (g * sg)
                dgnw = dgnw + jnp.sum(t * on[c], axis=0, keepdims=True)
                don = t * gnw
                do.append(rstd[c] * (don - _head_mean(don, m0, m1) - on[c] * _head_mean(don * on[c], m0, m1)))
            dstf = [_mm_tn(qf[c], do[c]) * bd for c in two]
            dfs = [d_f, dstf[0] + gch_v * d_f]
            ds = [[_mm_nt(do[c] * m, v[c]) * dec_ref[h] for h, m in heads] for c in two]
            dqr = [_mm_nt(do[c], stf[c]) * dqf + _mm_nt(do[c], stbs[c]) * dqb
                   + _mm(ds[c][0], kr[c] * m0) + _mm(ds[c][1], kr[c] * m1) for c in two]
            dkr = [_mm_nt(v[c], dfs[c]) * dkf + _mm_tn(ds[c][0], qr[c] * m0) + _mm_tn(ds[c][1], qr[c] * m1) for c in two]
            dv = [_mm(kr[c] * dkf, dfs[c]) + _mm_tn(s[c][0], do[c] * m0) + _mm_tn(s[c][1], do[c] * m1) for c in two]
            for c in two:
                dp_ref[0, rows[c], :] = rope_t(dqr[c], rows[c]).astype(dp_ref.dtype)
                dkr_ref[rows[c], :] = dkr[c]
                dv_ref[rows[c], :] = dv[c]
                dstb_ref[ns[c]] = _mm_tn(qb[c], do[c]) * bd
            return gch_v * stbs[1] + kvb[1], dstf[1] + gch_v * dfs[1], dgnw

        assert nc % 2 == 0
        _, _, dgnw = lax.fori_loop(0, nc // 2, pass_b, (zst, zst, jnp.zeros((1, LANE), F32)))
        dgnw_ref[...] = jnp.concatenate([dgnw, jnp.zeros((SUB - 1, LANE), F32)], axis=0)

        def pass_c(n, d_b):
            rows = _rows(n * RC, RC)
            kr = kr_ref[rows, :]
            v = v_ref[rows, :]
            dkr = dkr_ref[rows, :] + _mm_nt(v, d_b) * dkb
            dp_ref[1, rows, :] = (rope_t(dkr, rows) * scale).astype(dp_ref.dtype)
            dp_ref[2, rows, :] = (dv_ref[rows, :] + _mm(kr * dkb, d_b)).astype(dp_ref.dtype)
            return dstb_ref[n] + gch_v * d_b

        _loop2(nc, pass_c, zst)
        j = pl.program_id(0)
        _store_strips(dp_ref, dp_out_ref, (j + 6, j + 9, j + 12, j + 15), sems)

    n_in = len(_ret_specs(T)) + 1
    return pl.pallas_call(
        body, name="ret_bwd", grid=(RET_WIDTH // LANE,),
        in_specs=_ret_specs(T) + [_strip(T, lambda j: j + 3), pl.BlockSpec(memory_space=pl.ANY)],
        out_specs=[pl.BlockSpec(memory_space=pl.ANY), pl.BlockSpec((SUB, LANE), lambda j: (0, j))],
        out_shape=[jax.ShapeDtypeStruct(dp.shape, dp.dtype), jax.ShapeDtypeStruct((SUB, RET_WIDTH), F32)],
        scratch_shapes=[pltpu.VMEM((nc, LANE, LANE), F32), pltpu.VMEM((nc, LANE, LANE), F32),
                        pltpu.VMEM((T, LANE), F32), pltpu.VMEM((T, LANE), F32),
                        pltpu.VMEM((4, T, LANE), _BF), pltpu.VMEM((T, LANE), F32), pltpu.SemaphoreType.DMA((4,))],
        input_output_aliases={n_in: 0},
        compiler_params=_cparams(("arbitrary",)),
    )(proj, proj, proj, proj, cos, sin, dec, rtab, gch, gnw8, dycat, dp)


NA_Q = 2 * GRID_W
NA_WROWS = 10
NA_K = NA_WROWS * GRID_W
NA_CHUNKS = NA_K // LANE
NA_UNROLL = 2
NA_TYPES = 5
_ONEHOT_PRECISION = lax.Precision.HIGH


def _na_onehots(rows_n):
    reps = [(0, 0), (2, 0), (4, 0), (rows_n - 4, rows_n - NA_WROWS), (rows_n - 2, rows_n - NA_WROWS)]
    rm = np.zeros((NA_TYPES, 2, NA_WROWS, 2 * NA_KH - 1), np.float32)
    for t, (r, ws) in enumerate(reps):
        for qh in range(2):
            qrow = r + qh
            rstart = min(max(qrow - NA_KH // 2, 0), rows_n - NA_KH)
            for kh in range(NA_WROWS):
                krow = ws + kh
                if rstart <= krow < rstart + NA_KH:
                    rm[t, qh, kh, krow - qrow + NA_KH - 1] = 1.0
    cm = np.zeros((GRID_W, GRID_W, 2 * NA_KW - 1), np.float32)
    for qc in range(GRID_W):
        cstart = min(max(qc - NA_KW // 2, 0), GRID_W - NA_KW)
        for kc in range(cstart, cstart + NA_KW):
            cm[qc, kc, kc - qc + NA_KW - 1] = 1.0
    rm2 = rm.reshape(NA_TYPES, 2, NA_CHUNKS, 2, 2 * NA_KH - 1)
    cm2 = np.zeros((GRID_W, LANE, 2, 2 * NA_KW - 1), np.float32)
    for z in range(2):
        cm2[:, z * GRID_W:(z + 1) * GRID_W, z, :] = cm
    return rm2, cm2


def _na_bias_tables(rpb, rows_n):
    rm, cm = _na_onehots(rows_n)
    val = jnp.einsum("hab,tqpza,xkzb->htpqxk", rpb, rm, cm, precision=_ONEHOT_PRECISION)
    valid = np.einsum("tqpz,xkz->tpqxk", rm.sum(-1), cm.sum(-1)) > 0.5
    return jnp.where(valid[None], val, NEG).reshape(2, 2, NA_TYPES, NA_CHUNKS, NA_Q, LANE)


def _na_bias_grad(dtab, rows_n):
    rm, cm = _na_onehots(rows_n)
    d6 = dtab.reshape(NA_HEADS, NA_TYPES, NA_CHUNKS, 2, GRID_W, LANE)
    return jnp.einsum("htpqxk,tqpza,xkzb->hab", d6, rm, cm, precision=_ONEHOT_PRECISION)


def _na_bias(b_ref, h, typ):
    return jnp.concatenate([b_ref[0, h, typ, c] for c in range(NA_CHUNKS)], axis=1)


def _na_step(p, npairs, rows_n):
    ws = jnp.clip(2 * p - NA_KH // 2, 0, rows_n - NA_WROWS)
    koff = pl.multiple_of(ws * GRID_W, LANE)
    typ = jnp.where(p == 0, 0, jnp.where(p == 1, 1, jnp.where(p == npairs - 2, 3, jnp.where(p == npairs - 1, 4, 2))))
    return _rows(p * NA_Q, NA_Q), pl.ds(koff, NA_K), typ


def _na_fwd_call(proj, btab):
    T = proj.shape[0]
    npairs, rows_n = T // NA_Q, T // GRID_W

    def body(q_ref, k_ref, v_ref, b_ref, o_ref):
        lane = lax.broadcasted_iota(jnp.int32, (NA_Q, LANE), 1)
        m0 = (lane < HEAD_DIM).astype(F32)
        m1 = 1.0 - m0

        def steps(i, carry):
            idx = [_na_step(NA_UNROLL * i + u, npairs, rows_n) for u in range(NA_UNROLL)]
            chains = [(u, h, m) for u in range(NA_UNROLL) for h, m in ((0, m0), (1, m1))]
            kws = [k_ref[krows, :].astype(_BF) for _, krows, _ in idx]
            vws = [v_ref[krows, :].astype(_BF) for _, krows, _ in idx]
            s = [_mm_nt(q_ref[idx[u][0], :] * m, kws[u]) for u, h, m in chains]
            s = [s_ * (HEAD_DIM ** -0.5) + _na_bias(b_ref, h, idx[u][2]) for s_, (u, h, m) in zip(s, chains)]
            e = [jnp.exp(s_ - jnp.max(s_, axis=-1, keepdims=True)) for s_ in s]
            pr = [e_ / jnp.sum(e_, axis=-1, keepdims=True) for e_ in e]
            ov = [_mm(pr_, vws[u]) * m for pr_, (u, h, m) in zip(pr, chains)]
            for u in range(NA_UNROLL):
                o_ref[idx[u][0], :] = ov[2 * u] + ov[2 * u + 1]
            return carry

        lax.fori_loop(0, npairs // NA_UNROLL, steps, 0)

    return pl.pallas_call(
        body, name="na_fwd", grid=(NA_WIDTH // LANE,),
        in_specs=[_strip(T, lambda j: j + 18), _strip(T, lambda j: j + 20), _strip(T, lambda j: j + 22),
                  pl.BlockSpec((1, 2, NA_TYPES, NA_CHUNKS, NA_Q, LANE), lambda j: (j, 0, 0, 0, 0, 0))],
        out_specs=_strip(T, lambda j: j, buffers=1),
        out_shape=jax.ShapeDtypeStruct((T, NA_WIDTH), F32),
        compiler_params=_cparams(("arbitrary",)),
    )(proj, proj, proj, btab)


def _na_bwd_call(proj, dycat, btab, dp):
    T = proj.shape[0]
    npairs, rows_n = T // NA_Q, T // GRID_W
    scale = HEAD_DIM ** -0.5

    def body(q_ref, k_ref, v_ref, do_ref, b_ref, dp_in_ref, dp_out_ref, db_ref, dka_ref, dva_ref, stage_ref, sems):
        dq_ref = stage_ref.at[0]
        lane = lax.broadcasted_iota(jnp.int32, (NA_Q, LANE), 1)
        m0 = (lane < HEAD_DIM).astype(F32)
        m1 = 1.0 - m0
        dka_ref[...] = jnp.zeros_like(dka_ref)
        dva_ref[...] = jnp.zeros_like(dva_ref)
        db_ref[...] = jnp.zeros_like(db_ref)

        def steps(i, carry):
            idx = [_na_step(NA_UNROLL * i + u, npairs, rows_n) for u in range(NA_UNROLL)]
            chains = [(u, h, m) for u in range(NA_UNROLL) for h, m in ((0, m0), (1, m1))]
            kws = [k_ref[krows, :].astype(_BF) for _, krows, _ in idx]
            vws = [v_ref[krows, :].astype(_BF) for _, krows, _ in idx]
            qm = [(q_ref[idx[u][0], :] * m).astype(_BF) for u, h, m in chains]
            dom = [(do_ref[idx[u][0], :] * m).astype(_BF) for u, h, m in chains]
            s = [_mm_nt(qm_, kws[u]) for qm_, (u, h, m) in zip(qm, chains)]
            dpr = [_mm_nt(dom_, vws[u]) for dom_, (u, h, m) in zip(dom, chains)]
            s = [s_ * scale + _na_bias(b_ref, h, idx[u][2]) for s_, (u, h, m) in zip(s, chains)]
            e = [jnp.exp(s_ - jnp.max(s_, axis=-1, keepdims=True)) for s_ in s]
            pr = [e_ / jnp.sum(e_, axis=-1, keepdims=True) for e_ in e]
            ds = [pr_ * (dpr_ - jnp.sum(pr_ * dpr_, axis=-1, keepdims=True)) for pr_, dpr_ in zip(pr, dpr)]
            dsb = [(ds_ * scale).astype(_BF) for ds_ in ds]
            dq = [_mm(dsb_, kws[u]) * m for dsb_, (u, h, m) in zip(dsb, chains)]
            dk = [_mm_tn(dsb_, qm_) for dsb_, qm_ in zip(dsb, qm)]
            dv = [_mm_tn(pr_, dom_) for pr_, dom_ in zip(pr, dom)]
            for ds_, (u, h, m) in zip(ds, chains):
                for c in range(NA_CHUNKS):
                    db_ref[0, h, idx[u][2], c] += ds_[:, c * LANE:(c + 1) * LANE]
            for u in range(NA_UNROLL):
                qrows, krows, _ = idx[u]
                dq_ref[qrows, :] = (dq[2 * u] + dq[2 * u + 1]).astype(dq_ref.dtype)
                dka_ref[krows, :] += dk[2 * u] + dk[2 * u + 1]
                dva_ref[krows, :] += dv[2 * u] + dv[2 * u + 1]
            return carry

        lax.fori_loop(0, npairs // NA_UNROLL, steps, 0)
        stage_ref[1] = dka_ref[...].astype(stage_ref.dtype)
        stage_ref[2] = dva_ref[...].astype(stage_ref.dtype)
        j = pl.program_id(0)
        _store_strips(stage_ref, dp_out_ref, (j + 18, j + 20, j + 22), sems)

    tab = pl.BlockSpec((1, 2, NA_TYPES, NA_CHUNKS, NA_Q, LANE), lambda j: (j, 0, 0, 0, 0, 0))
    return pl.pallas_call(
        body, name="na_bwd", grid=(NA_WIDTH // LANE,),
        in_specs=[_strip(T, lambda j: j + 18), _strip(T, lambda j: j + 20), _strip(T, lambda j: j + 22),
                  _strip(T, lambda j: j + 6), tab, pl.BlockSpec(memory_space=pl.ANY)],
        out_specs=[pl.BlockSpec(memory_space=pl.ANY), tab],
        out_shape=[jax.ShapeDtypeStruct(dp.shape, dp.dtype),
                   jax.ShapeDtypeStruct((2, 2, NA_TYPES, NA_CHUNKS, NA_Q, LANE), F32)],
        scratch_shapes=[pltpu.VMEM((T, LANE), F32), pltpu.VMEM((T, LANE), F32),
                        pltpu.VMEM((3, T, LANE), _BF), pltpu.SemaphoreType.DMA((3,))],
        input_output_aliases={5: 0},
        compiler_params=_cparams(("arbitrary",)),
    )(proj, proj, proj, dycat, btab, dp)


W_BLK = IN_WIDTH // N_DEV
MXU_W = 256
N_BLK = 3 * MXU_W
N_STEPS = IN_WIDTH // N_BLK
TM = 512


def _ln_fwd(z, g, b):
    zc = z - jnp.mean(z, axis=-1, keepdims=True)
    var = jnp.mean(zc * zc, axis=-1, keepdims=True)
    return zc * lax.rsqrt(var + LN_EPS) * g + b


def _ln_bwd(dy, z, g):
    zc = z - jnp.mean(z, axis=-1, keepdims=True)
    rstd = lax.rsqrt(jnp.mean(zc * zc, axis=-1, keepdims=True) + LN_EPS)
    xhat = zc * rstd
    dxh = dy * g
    dz = rstd * (dxh - jnp.mean(dxh, axis=-1, keepdims=True) - xhat * jnp.mean(dxh * xhat, axis=-1, keepdims=True))
    return dz, dy * xhat


def _row_tile(T):
    return 1024 if T % 1024 == 0 else TM


def _halves(n):
    return (pl.ds(0, n // 2), pl.ds(n // 2, n // 2))


def _inproj_call(xb, w, after):
    T = xb.shape[0]
    tm = _row_tile(T)

    def body(x_ref, w_ref, after_ref, o_ref):
        o_ref[...] = _mm(x_ref[...], w_ref[...])

    return pl.pallas_call(
        body, name="inproj", grid=(T // tm, N_STEPS),
        in_specs=[pl.BlockSpec((tm, D_MODEL), lambda i, n: (i, 0)),
                  pl.BlockSpec((D_MODEL, N_BLK), lambda i, n: (0, n)),
                  pl.BlockSpec(memory_space=pl.ANY)],
        out_specs=pl.BlockSpec((tm, N_BLK), lambda i, n: (i, n)),
        out_shape=jax.ShapeDtypeStruct((T, IN_WIDTH), F32),
        compiler_params=_cparams(("parallel", "arbitrary")),
    )(xb, w, after)


def _vec_spec():
    return pl.BlockSpec((1, D_MODEL), lambda *_: (0, 0))


def _outproj_ln_call(y_lru, y_ret, y_na, x, w, g, b, after):
    T = x.shape[0]

    def body(yl_ref, yr_ref, yn_ref, x_ref, w_ref, g_ref, b_ref, after_ref, z_ref, x1_ref, x1b_ref, yc_ref):
        yc_ref[:, 0:LRU_WIDTH] = yl_ref[...].astype(yc_ref.dtype)
        yc_ref[:, LRU_WIDTH:LRU_WIDTH + RET_WIDTH] = yr_ref[...].astype(yc_ref.dtype)
        yc_ref[:, LRU_WIDTH + RET_WIDTH:] = yn_ref[...].astype(yc_ref.dtype)
        z = ALPHA * x_ref[...] + _mm(yc_ref[...], w_ref[...])
        z_ref[...] = z
        x1 = _ln_fwd(z, g_ref[...], b_ref[...])
        x1_ref[...] = x1
        x1b_ref[...] = x1.astype(x1b_ref.dtype)

    row = lambda w_: pl.BlockSpec((TM, w_), lambda i: (i, 0))
    return pl.pallas_call(
        body, name="outproj_ln", grid=(T // TM,),
        in_specs=[row(LRU_WIDTH), row(RET_WIDTH), row(NA_WIDTH), row(D_MODEL),
                  pl.BlockSpec((D_MODEL, D_MODEL), lambda i: (0, 0)), _vec_spec(), _vec_spec(),
                  pl.BlockSpec(memory_space=pl.ANY)],
        out_specs=[row(D_MODEL)] * 4,
        out_shape=[jax.ShapeDtypeStruct((T, D_MODEL), F32), jax.ShapeDtypeStruct((T, D_MODEL), F32),
                   jax.ShapeDtypeStruct((T, D_MODEL), _BF), jax.ShapeDtypeStruct((T, D_MODEL), _BF)],
        compiler_params=_cparams(("parallel",)),
    )(y_lru, y_ret, y_na, x, w, g, b, after)


def _ffn_ln_call(x1, x1b, wg, wu, wd, g, b):
    T = x1.shape[0]

    def body(x_ref, xb_ref, wg_ref, wu_ref, wd_ref, g_ref, b_ref, z_ref, x2_ref, x2b_ref, gp_ref, up_ref, acc_ref):
        n = pl.program_id(1)

        @pl.when(n == 0)
        def _():
            acc_ref[...] = jnp.zeros_like(acc_ref)

        r0, r1 = _halves(TM)

        def pre(rows):
            xb = xb_ref[rows, :]
            return _mm(xb, wg_ref[...]), _mm(xb, wu_ref[...])

        def act(rows, gp, up):
            gp_ref[rows, :] = gp.astype(gp_ref.dtype)
            up_ref[rows, :] = up.astype(up_ref.dtype)
            return (gp * _sigmoid(gp) * up).astype(_BF)

        gp0, up0 = pre(r0)
        hid0 = act(r0, gp0, up0)
        gp1, up1 = pre(r1)
        acc_ref[r0, :] += _mm(hid0, wd_ref[...])
        hid1 = act(r1, gp1, up1)
        acc_ref[r1, :] += _mm(hid1, wd_ref[...])

        @pl.when(n == N_STEPS - 1)
        def _():
            z = ALPHA * x_ref[...] + acc_ref[...]
            z_ref[...] = z
            x2 = _ln_fwd(z, g_ref[...], b_ref[...])
            x2_ref[...] = x2
            x2b_ref[...] = x2.astype(x2b_ref.dtype)

    row = pl.BlockSpec((TM, D_MODEL), lambda i, n: (i, 0))
    return pl.pallas_call(
        body, name="ffn_ln", grid=(T // TM, N_STEPS),
        in_specs=[row, row,
                  pl.BlockSpec((D_MODEL, N_BLK), lambda i, n: (0, n)),
                  pl.BlockSpec((D_MODEL, N_BLK), lambda i, n: (0, n)),
                  pl.BlockSpec((N_BLK, D_MODEL), lambda i, n: (n, 0)), _vec_spec(), _vec_spec()],
        out_specs=[row] * 3 + [pl.BlockSpec((TM, N_BLK), lambda i, n: (i, n))] * 2,
        out_shape=[jax.ShapeDtypeStruct((T, D_MODEL), F32), jax.ShapeDtypeStruct((T, D_MODEL), F32),
                   jax.ShapeDtypeStruct((T, D_MODEL), _BF),
                   jax.ShapeDtypeStruct((T, IN_WIDTH), _BF), jax.ShapeDtypeStruct((T, IN_WIDTH), _BF)],
        scratch_shapes=[pltpu.VMEM((TM, D_MODEL), F32)],
        compiler_params=_cparams(("parallel", "arbitrary")),
    )(x1, x1b, wg, wu, wd, g, b)


def _loss_call(y, t):
    T = y.shape[0]

    def body(y_ref, t_ref, dy_ref, loss_ref):
        @pl.when(pl.program_id(0) == 0)
        def _():
            loss_ref[...] = jnp.zeros_like(loss_ref)

        err = y_ref[...] - t_ref[...]
        dy_ref[...] = err * (1.0 / D_MODEL)
        part = 0.5 * jnp.sum(jnp.mean(err * err, axis=-1, keepdims=True), axis=0, keepdims=True)
        loss_ref[...] += jnp.broadcast_to(part, loss_ref.shape)

    row = pl.BlockSpec((TM, D_MODEL), lambda i: (i, 0))
    return pl.pallas_call(
        body, name="loss", grid=(T // TM,),
        in_specs=[row, row],
        out_specs=[row, pl.BlockSpec((SUB, LANE), lambda i: (0, 0))],
        out_shape=[jax.ShapeDtypeStruct((T, D_MODEL), F32), jax.ShapeDtypeStruct((SUB, LANE), F32)],
        compiler_params=_cparams(("arbitrary",)),
    )(y, t)


def _ffn_bwd_call(dx2, z2, gpb, upb, wg, wu, wd, g, after):
    T = dx2.shape[0]

    def body(dx2_ref, z_ref, gp_ref, up_ref, wg_ref, wu_ref, wd_ref, g_ref, after_ref,
             dx1_ref, dgp_ref, dup_ref, hid_ref, dzb_ref, dln_ref, acc_ref):
        i, n = pl.program_id(0), pl.program_id(1)

        @pl.when((i == 0) & (n == 0))
        def _():
            dln_ref[...] = jnp.zeros_like(dln_ref)

        @pl.when(n == 0)
        def _():
            dy = dx2_ref[...]
            dz, dg_rows = _ln_bwd(dy, z_ref[...], g_ref[...])
            dzb_ref[...] = dz.astype(dzb_ref.dtype)
            acc_ref[...] = ALPHA * dz
            dln_ref[0:1, :] += jnp.sum(dg_rows, axis=0, keepdims=True)
            dln_ref[1:2, :] += jnp.sum(dy, axis=0, keepdims=True)

        r0, r1 = _halves(TM)

        def grads(rows, dhid):
            gp = gp_ref[rows, :].astype(F32)
            up = up_ref[rows, :].astype(F32)
            sg = _sigmoid(gp)
            act = gp * sg
            hid_ref[rows, :] = (act * up).astype(hid_ref.dtype)
            dup = (dhid * act).astype(_BF)
            dgp = (dhid * up * (sg * (1.0 + gp * (1.0 - sg)))).astype(_BF)
            dgp_ref[rows, :] = dgp.astype(dgp_ref.dtype)
            dup_ref[rows, :] = dup.astype(dup_ref.dtype)
            return dgp, dup

        dhid0 = _mm_nt(dzb_ref[r0, :], wd_ref[...])
        dhid1 = _mm_nt(dzb_ref[r1, :], wd_ref[...])
        dgp0, dup0 = grads(r0, dhid0)
        acc_ref[r0, :] += _mm_nt(dgp0, wg_ref[...]) + _mm_nt(dup0, wu_ref[...])
        dgp1, dup1 = grads(r1, dhid1)
        acc_ref[r1, :] += _mm_nt(dgp1, wg_ref[...]) + _mm_nt(dup1, wu_ref[...])

        @pl.when(n == N_STEPS - 1)
        def _():
            dx1_ref[...] = acc_ref[...]

    row = pl.BlockSpec((TM, D_MODEL), lambda i, n: (i, 0))
    blk = pl.BlockSpec((TM, N_BLK), lambda i, n: (i, n))
    return pl.pallas_call(
        body, name="ffn_bwd", grid=(T // TM, N_STEPS),
        in_specs=[row, row, blk, blk,
                  pl.BlockSpec((D_MODEL, N_BLK), lambda i, n: (0, n)),
                  pl.BlockSpec((D_MODEL, N_BLK), lambda i, n: (0, n)),
                  pl.BlockSpec((N_BLK, D_MODEL), lambda i, n: (n, 0)), _vec_spec(),
                  pl.BlockSpec(memory_space=pl.ANY)],
        out_specs=[row, blk, blk, blk, row, pl.BlockSpec((SUB, D_MODEL), lambda i, n: (0, 0))],
        out_shape=[jax.ShapeDtypeStruct((T, D_MODEL), F32),
                   jax.ShapeDtypeStruct((T, IN_WIDTH), _BF), jax.ShapeDtypeStruct((T, IN_WIDTH), _BF),
                   jax.ShapeDtypeStruct((T, IN_WIDTH), _BF), jax.ShapeDtypeStruct((T, D_MODEL), _BF),
                   jax.ShapeDtypeStruct((SUB, D_MODEL), F32)],
        scratch_shapes=[pltpu.VMEM((TM, D_MODEL), F32)],
        compiler_params=_cparams(("arbitrary", "arbitrary")),
    )(dx2, z2, gpb, upb, wg, wu, wd, g, after)


def _outproj_bwd_call(dx1, z1, w, g):
    T = dx1.shape[0]

    def body(dx_ref, z_ref, w_ref, g_ref, dzb_ref, dyc_ref, dres_ref, dln_ref):
        @pl.when(pl.program_id(0) == 0)
        def _():
            dln_ref[...] = jnp.zeros_like(dln_ref)

        dy = dx_ref[...]
        dz, dg_rows = _ln_bwd(dy, z_ref[...], g_ref[...])
        dzb_ref[...] = dz.astype(dzb_ref.dtype)
        dres_ref[...] = ALPHA * dz
        dyc_ref[...] = _mm_nt(dz, w_ref[...])
        dln_ref[0:1, :] += jnp.sum(dg_rows, axis=0, keepdims=True)
        dln_ref[1:2, :] += jnp.sum(dy, axis=0, keepdims=True)

    row = pl.BlockSpec((TM, D_MODEL), lambda i: (i, 0))
    return pl.pallas_call(
        body, name="outproj_bwd", grid=(T // TM,),
        in_specs=[row, row, pl.BlockSpec((D_MODEL, D_MODEL), lambda i: (0, 0)), _vec_spec()],
        out_specs=[row, row, row, pl.BlockSpec((SUB, D_MODEL), lambda i: (0, 0))],
        out_shape=[jax.ShapeDtypeStruct((T, D_MODEL), _BF), jax.ShapeDtypeStruct((T, D_MODEL), F32),
                   jax.ShapeDtypeStruct((T, D_MODEL), F32), jax.ShapeDtypeStruct((SUB, D_MODEL), F32)],
        compiler_params=_cparams(("arbitrary",)),
    )(dx1, z1, w, g)


def _inproj_bwd_call(dres, dp, w):
    T = dres.shape[0]

    def body(dres_ref, dp_ref, w_ref, dx_ref):
        dx_ref[...] = dres_ref[...] + _mm_nt(dp_ref[...], w_ref[...])

    row = pl.BlockSpec((TM, D_MODEL), lambda i: (i, 0))
    return pl.pallas_call(
        body, name="inproj_bwd", grid=(T // TM,),
        in_specs=[row, pl.BlockSpec((TM, IN_WIDTH), lambda i: (i, 0)),
                  pl.BlockSpec((D_MODEL, IN_WIDTH), lambda i: (0, 0), pipeline_mode=pl.Buffered(1))],
        out_specs=row,
        out_shape=jax.ShapeDtypeStruct((T, D_MODEL), F32),
        compiler_params=_cparams(("parallel",)),
    )(dres, dp, w)


def _tn_cols_call(a, b, name):
    T, ka = a.shape
    n = b.shape[1]

    def body(a_ref, b_ref, o_ref):
        o_ref[...] = _mm_tn(a_ref[...], b_ref[...]).astype(o_ref.dtype)

    return pl.pallas_call(
        body, name=name, grid=(n // N_BLK,),
        in_specs=[pl.BlockSpec((T, ka), lambda j: (0, 0), pipeline_mode=pl.Buffered(1)),
                  pl.BlockSpec((T, N_BLK), lambda j: (0, j))],
        out_specs=pl.BlockSpec((ka, N_BLK), lambda j: (0, j)),
        out_shape=jax.ShapeDtypeStruct((ka, n), _BF),
        compiler_params=_cparams(("parallel",)),
    )(a, b)


def _tn_rows_call(a, b, kb, name):
    T, ka = a.shape
    n = b.shape[1]

    def body(a_ref, b_ref, o_ref):
        o_ref[...] = _mm_tn(a_ref[...], b_ref[...]).astype(o_ref.dtype)

    return pl.pallas_call(
        body, name=name, grid=(ka // kb,),
        in_specs=[pl.BlockSpec((T, kb), lambda r: (0, r)),
                  pl.BlockSpec((T, n), lambda r: (0, 0), pipeline_mode=pl.Buffered(1))],
        out_specs=pl.BlockSpec((kb, n), lambda r: (r, 0)),
        out_shape=jax.ShapeDtypeStruct((ka, n), _BF),
        compiler_params=_cparams(("parallel",)),
    )(a, b)


def _me():
    return lax.axis_index("x"), lax.axis_index("y"), lax.axis_index("c")


def _flip(k):
    x, y, c = _me()
    return (1 - x if k & 4 else x, 1 - y if k & 2 else y, 1 - c if k & 1 else c)


def _dev_index(pos):
    return 4 * pos[0] + 2 * pos[1] + pos[2]


_HBM = pl.BlockSpec(memory_space=pltpu.HBM)
_SEM = pl.BlockSpec(memory_space=pltpu.SEMAPHORE)


def _land_shape(shape, mode):
    if mode == "all":
        return (N_DEV,) + shape
    if mode == "cols":
        return (shape[0], N_DEV * shape[1])
    if mode == "blk":
        return shape
    assert mode == "scols"
    return (N_DEV, shape[0], shape[1] // N_DEV)


def _comm_copies(ins, lands, modes, send_sems, recv_sems):
    me = _dev_index(_me())
    copies = []
    for k in range(N_DEV):
        peer = _flip(k)
        pidx = _dev_index(peer)
        for a, (src, land, mode) in enumerate(zip(ins, lands, modes)):
            if mode == "blk":
                src = src.at[pidx]
            elif mode == "scols":
                w = src.shape[1] // N_DEV
                src = src.at[:, pl.ds(pl.multiple_of(pidx * w, LANE), w)]
            if mode == "cols":
                w = src.shape[1]
                dst = land.at[:, pl.ds(pl.multiple_of(me * w, LANE), w)]
            else:
                dst = land.at[me]
            copies.append(pltpu.make_async_remote_copy(
                src_ref=src, dst_ref=dst, send_sem=send_sems.at[k * len(ins) + a], recv_sem=recv_sems.at[k * len(ins) + a],
                device_id=peer, device_id_type=MESH))
    return copies


def _comm_start_call(arrs, gather_flags, after, name):
    n = len(arrs)
    lands = [lax.empty(_land_shape(v.shape, mode), v.dtype) for v, mode in zip(arrs, gather_flags)]

    def body(*refs):
        ins, lnd = refs[:n], refs[n:2 * n]
        send_sems, recv_sems = refs[2 * n + len(after)], refs[2 * n + len(after) + 1]
        for cp in _comm_copies(ins, lnd, gather_flags, send_sems, recv_sems):
            cp.start()
        refs[-1][...] = jnp.zeros_like(refs[-1])

    hbm = [pltpu.with_memory_space_constraint(v, pltpu.HBM) for v in list(arrs) + lands]
    out = pl.pallas_call(
        body, name=name,
        out_shape=(pltpu.SemaphoreType.DMA((N_DEV * n,)), pltpu.SemaphoreType.DMA((N_DEV * n,)),
                   *[pltpu.HBM(v.shape, v.dtype) for v in hbm], jax.ShapeDtypeStruct((SUB, LANE), F32)),
        in_specs=[_HBM] * (2 * n) + [pl.BlockSpec(memory_space=pl.ANY)] * len(after),
        out_specs=(_SEM, _SEM, *[_HBM] * (2 * n), pl.BlockSpec(memory_space=pltpu.VMEM)),
        input_output_aliases={i: 2 + i for i in range(2 * n)},
        compiler_params=pltpu.CompilerParams(has_side_effects=pltpu.SideEffectType.DATAFLOW_SIDE_EFFECTING),
    )(*hbm, *after)
    return out[:-1], out[-1]


def _comm_wait_call(state, gather_flags, after, name):
    n = len(gather_flags)
    send_sems, recv_sems, thru = state[0], state[1], state[2:]

    def body(*refs):
        ins, lnd, ssem, rsem = refs[:n], refs[n:2 * n], refs[2 * n], refs[2 * n + 1]
        for cp in _comm_copies(ins, lnd, gather_flags, ssem, rsem):
            cp.wait_send()
            cp.wait_recv()

    out = pl.pallas_call(
        body, name=name,
        out_shape=tuple(pltpu.HBM(v.shape, v.dtype) for v in thru),
        in_specs=[_HBM] * (2 * n) + [_SEM, _SEM] + [pl.BlockSpec(memory_space=pl.ANY)] * len(after),
        out_specs=tuple([_HBM] * (2 * n)),
        input_output_aliases={i: i for i in range(2 * n)},
        compiler_params=pltpu.CompilerParams(has_side_effects=pltpu.SideEffectType.DATAFLOW_SIDE_EFFECTING),
    )(*thru, send_sems, recv_sems, *after)
    return out[n:]


def _sum8_call(recv, rows, name):
    _, r, c = recv.shape

    def body(x_ref, o_ref):
        acc = x_ref[0].astype(F32)
        for s in range(1, N_DEV):
            acc = acc + x_ref[s].astype(F32)
        o_ref[...] = acc

    return pl.pallas_call(
        body, name=name, grid=(r // rows,),
        in_specs=[pl.BlockSpec((N_DEV, rows, c), lambda i: (0, i, 0))],
        out_specs=pl.BlockSpec((rows, c), lambda i: (i, 0)),
        out_shape=jax.ShapeDtypeStruct((r, c), F32),
        compiler_params=_cparams(("parallel",)),
    )(recv)


def _adamw_call(w, g, m, v, rows, name):
    nl, r, c = w.shape

    def body(w_ref, g_ref, m_ref, v_ref, d_ref, nm_ref, nv_ref):
        gr = g_ref[...]
        nm = ADAM_B1 * m_ref[...] + (1.0 - ADAM_B1) * gr
        nv = ADAM_B2 * v_ref[...] + (1.0 - ADAM_B2) * (gr * gr)
        m_hat = nm / (1.0 - ADAM_B1 ** ADAM_STEP)
        v_hat = nv / (1.0 - ADAM_B2 ** ADAM_STEP)
        d_ref[...] = -ADAM_LR * (m_hat / (jnp.sqrt(v_hat) + ADAM_EPS) + ADAM_WD * w_ref[...])
        nm_ref[...] = nm
        nv_ref[...] = nv

    spec = pl.BlockSpec((None, rows, c), lambda l, i: (l, i, 0))
    return pl.pallas_call(
        body, name=name, grid=(nl, r // rows),
        in_specs=[spec] * 4, out_specs=[spec] * 3,
        out_shape=[jax.ShapeDtypeStruct((nl, r, c), F32)] * 3,
        compiler_params=_cparams(("parallel", "parallel")),
    )(w, g, m, v)


SH_ROWS = 16
SH_W = LRU_WIDTH // N_DEV
REP_ROWS = 824
_REP_SIZES = (LRU_WIDTH, 2 * 6 * 64 * 64, 2 * 6 * 64 * 64, RET_WIDTH, 1920, D_MODEL, D_MODEL, D_MODEL, D_MODEL)
_RPB_SIZE = NA_HEADS * (2 * NA_KH - 1) * (2 * NA_KW - 1)


def _pack_sh(cw, ba, bx, lam):
    return jnp.concatenate([cw, ba, bx, lam], axis=0)


def _pad_sh(p):
    pad = [(0, 0)] * (p.ndim - 2) + [(0, SH_ROWS - p.shape[-2]), (0, LANE - p.shape[-1])]
    return jnp.pad(p, pad)


def _pack_rep(cb, wa, wx, gnw, rpb, l1g, l1b, l2g, l2b):
    flat = jnp.concatenate([cb.reshape(-1), wa.reshape(-1), wx.reshape(-1), gnw.reshape(-1),
                            jnp.pad(rpb.reshape(-1), (0, 1920 - _RPB_SIZE)), l1g, l1b, l2g, l2b,
                            jnp.zeros((REP_ROWS * LANE - sum(_REP_SIZES),), F32)])
    return flat.reshape(REP_ROWS, LANE)


def _unpack_rep(p):
    nl = p.shape[0]
    flat = p.reshape(nl, -1)
    out, off = [], 0
    for size in _REP_SIZES:
        out.append(flat[:, off:off + size])
        off += size
    cb, wa, wx, gnw, rpb, l1g, l1b, l2g, l2b = out
    return (cb, wa.reshape(nl, 2, 6, 64, 64), wx.reshape(nl, 2, 6, 64, 64), gnw,
            rpb[:, :_RPB_SIZE].reshape(nl, NA_HEADS, 2 * NA_KH - 1, 2 * NA_KW - 1), l1g, l1b, l2g, l2b)


def _adamw_nd(w, g, m, v, rows, name):
    shp = w.shape
    if rows is None:
        f = lambda t: t.reshape(1, -1, shp[-1])
        rows = f(w).shape[1]
    else:
        f = lambda t: t
    return [t.reshape(shp) for t in _adamw_call(f(w), f(g), f(m), f(v), rows, name)]


def kernel(x, w_in, conv_w, conv_b, lru_w_a, lru_b_a, lru_w_x, lru_b_x, lru_lam, ret_gn_w, na_rpb, w_out, ln1_g, ln1_b, w_gate, w_up, w_down, ln2_g, ln2_b, loss_target, m_w_in, m_conv_w, m_conv_b, m_lru_w_a, m_lru_b_a, m_lru_w_x, m_lru_b_x, m_lru_lam, m_ret_gn_w, m_na_rpb, m_w_out, m_ln1_g, m_ln1_b, m_w_gate, m_w_up, m_w_down, m_ln2_g, m_ln2_b, v_w_in, v_conv_w, v_conv_b, v_lru_w_a, v_lru_b_a, v_lru_w_x, v_lru_b_x, v_lru_lam, v_ret_gn_w, v_na_rpb, v_w_out, v_ln1_g, v_ln1_b, v_w_gate, v_w_up, v_w_down, v_ln2_g, v_ln2_b):
    nl = w_in.shape[0]
    T = x.shape[1]
    rows_n = T // GRID_W
    x0, target = x[0], loss_target[0]
    ffpad = W_BLK - FF_BLK

    win_b = w_in.astype(_BF)
    wg_b = jnp.pad(w_gate, ((0, 0), (0, 0), (0, ffpad))).astype(_BF)
    wu_b = jnp.pad(w_up, ((0, 0), (0, 0), (0, ffpad))).astype(_BF)
    wd_b = jnp.pad(w_down, ((0, 0), (0, ffpad), (0, 0))).astype(_BF)
    wout_b = w_out.astype(_BF)
    def agf_start(l, after):
        sh = _pad_sh(_pack_sh(conv_w[l], lru_b_a[l], lru_b_x[l], lru_lam[l]))
        arrs, modes = [win_b[l], sh], ["cols", "all"]
        if l > 0:
            arrs, modes = arrs + [wd_b[l]], modes + ["all"]
        return _comm_start_call(arrs, modes, after, f"agf_start{l}"), modes

    def agk_start(l, after):
        arrs, modes = [wg_b[l], wu_b[l], wout_b[l]], ["cols", "cols", "all"]
        if l == 0:
            arrs, modes = arrs + [wd_b[l]], modes + ["all"]
        return _comm_start_call(arrs, modes, after, f"agk_start{l}"), modes

    tables = _ret_tables(T)
    w4_all = _lru_w4(lru_w_a, lru_w_x)
    layers = []
    gathered = []
    xs, xb = x0, x0.astype(_BF)
    (agf_state, token), agf_modes = agf_start(0, [])
    tie = 0.0 * token[0, 0]
    btabs = [_na_bias_tables(na_rpb[l] + tie, rows_n) for l in range(nl)]
    for l in range(nl):
        front = _comm_wait_call(agf_state, agf_modes, [xb] + (btabs if l == 0 else []), f"agf_wait{l}")
        win, shg = front[0], front[1]
        (agk_state, token), agk_modes = agk_start(l, [shg])
        full = shg[:, :10, :SH_W].transpose(1, 0, 2).reshape(10, LRU_WIDTH)
        vec, w4 = _lru_vec(full[0:4], conv_b[l], full[4:6], full[6:8], full[8:10]), w4_all[l]
        gnw8 = jnp.pad(ret_gn_w[l][None], ((0, SUB - 1), (0, 0)))
        btab = btabs[l]
        proj = _inproj_call(xb, win, token)
        y_lru = _lru_fwd_call(proj, vec, w4)
        y_ret = _ret_fwd_call(proj, tables, gnw8)
        y_na = _na_fwd_call(proj, btab)
        back = _comm_wait_call(agk_state, agk_modes, [y_na], f"agk_wait{l}")
        wg, wu, wout = back[0], back[1], back[2]
        wd = (back[3] if l == 0 else front[2]).reshape(IN_WIDTH, D_MODEL)
        wout = wout.reshape(D_MODEL, D_MODEL)
        gathered.append((win, wg, wu, wd, wout))
        if l + 1 < nl:
            (agf_state, token), agf_modes = agf_start(l + 1, [wout])
        z1, x1, x1b, ycb = _outproj_ln_call(y_lru, y_ret, y_na, xs, wout, ln1_g[l][None], ln1_b[l][None], token)
        z2, x2, x2b, gpb, upb = _ffn_ln_call(x1, x1b, wg, wu, wd, ln2_g[l][None], ln2_b[l][None])
        layers.append(dict(xb=xb, proj=proj, vec=vec, w4=w4, gnw8=gnw8, btab=btab,
                           z1=z1, x1b=x1b, ycb=ycb, z2=z2, gpb=gpb, upb=upb))
        xs, xb = x2, x2b

    dx, loss_blk = _loss_call(xs, target)
    loss = lax.psum(loss_blk[0, 0], ("x", "y", "c"))

    gxa_flags = ["scols", "scols", "blk", "blk"]
    gxb_flags = ["scols", "blk", "all"]
    gxa_state, gxb_state = [None] * nl, [None] * nl
    token = loss_blk
    for l in reversed(range(nl)):
        s = layers[l]
        win, wg, wu, wd, wout = gathered[l]
        dx1, dgp, dup, hid, dz2b, dln2 = _ffn_bwd_call(dx, s["z2"], s["gpb"], s["upb"], wg, wu, wd, ln2_g[l][None], token)
        dwg = _tn_cols_call(s["x1b"], dgp, "tn_cols")
        dwu = _tn_cols_call(s["x1b"], dup, "tn_cols")
        dwd = _tn_rows_call(hid, dz2b, N_BLK, "tn_rows_down").reshape(N_DEV, W_BLK, D_MODEL)
        dz1b, dyc, dres, dln1 = _outproj_bwd_call(dx1, s["z1"], wout, ln1_g[l][None])
        dwout = _tn_rows_call(s["ycb"], dz1b, D_MODEL // 2, "tn_rows_out").reshape(N_DEV, LANE, D_MODEL)
        gxa_state[l], token = _comm_start_call([dwg, dwu, dwd, dwout], gxa_flags, [], f"gxa_start{l}")
        dp, dvec, dw4 = _lru_bwd_call(s["proj"], dyc, s["vec"], s["w4"], token)
        dp, dgnw = _ret_bwd_call(s["proj"], dyc, tables, s["gnw8"], dp)
        dp, dbias = _na_bwd_call(s["proj"], dyc, s["btab"], dp)
        dwin = _tn_cols_call(s["xb"], dp, "tn_cols")
        dx = _inproj_bwd_call(dres, dp, win)
        dcw, dcb, dwa, dba, dwx, dbx, dlam = _lru_unpack(dvec, dw4)
        rep = _pack_rep(dcb, dwa, dwx, dgnw[0], _na_bias_grad(dbias, rows_n), dln1[0], dln1[1], dln2[0], dln2[1])
        sh = _pack_sh(dcw, dba, dbx, dlam).reshape(10, N_DEV, SH_W).transpose(1, 0, 2)
        gxb_state[l], token = _comm_start_call([dwin, _pad_sh(sh), rep], gxb_flags, [], f"gxb_start{l}")

    g_big = [[None] * nl for _ in range(5)]
    g_sh = [None] * nl
    g_rep = [None] * nl
    after = [dx, token]
    for l in reversed(range(nl)):
        ra = _comm_wait_call(gxa_state[l], gxa_flags, after, f"gxa_wait{l}")
        g_big[1][l] = _sum8_call(ra[0], TM, "sum8_cols")[:, :FF_BLK]
        g_big[2][l] = _sum8_call(ra[1], TM, "sum8_cols")[:, :FF_BLK]
        g_big[3][l] = _sum8_call(ra[2], W_BLK, "sum8_down")[:FF_BLK]
        g_big[4][l] = _sum8_call(ra[3], LANE, "sum8_out")
        rb = _comm_wait_call(gxb_state[l], gxb_flags, [g_big[4][l]], f"gxb_wait{l}")
        g_big[0][l] = _sum8_call(rb[0], TM, "sum8_cols")
        g_sh[l] = _sum8_call(rb[1], SH_ROWS, "sum8_sh")
        g_rep[l] = _sum8_call(rb[2], REP_ROWS, "sum8_rep")
        after = [g_rep[l]]

    g_w_in, g_w_gate, g_w_up, g_w_down, g_w_out = [jnp.stack(t) for t in g_big]
    big = {
        "w_in": _adamw_nd(w_in, g_w_in, m_w_in, v_w_in, TM, "adamw_in"),
        "w_gate": _adamw_nd(w_gate, g_w_gate, m_w_gate, v_w_gate, TM, "adamw_ff"),
        "w_up": _adamw_nd(w_up, g_w_up, m_w_up, v_w_up, TM, "adamw_ff"),
        "w_down": _adamw_nd(w_down, g_w_down, m_w_down, v_w_down, FF_BLK, "adamw_down"),
        "w_out": _adamw_nd(w_out, g_w_out, m_w_out, v_w_out, LANE, "adamw_out"),
    }
    g_shp = jnp.stack(g_sh)[:, :, :SH_W]
    rep_names = ("conv_b", "lru_w_a", "lru_w_x", "ret_gn_w", "na_rpb", "ln1_g", "ln1_b", "ln2_g", "ln2_b")
    grads = {"w_in": g_w_in, "w_gate": g_w_gate, "w_up": g_w_up, "w_down": g_w_down, "w_out": g_w_out,
             "conv_w": g_shp[:, 0:4], "lru_b_a": g_shp[:, 4:6], "lru_b_x": g_shp[:, 6:8], "lru_lam": g_shp[:, 8:10]}
    grads.update(dict(zip(rep_names, _unpack_rep(jnp.stack(g_rep)))))
    small = {
        "conv_w": (conv_w, m_conv_w, v_conv_w), "conv_b": (conv_b, m_conv_b, v_conv_b),
        "lru_w_a": (lru_w_a, m_lru_w_a, v_lru_w_a), "lru_b_a": (lru_b_a, m_lru_b_a, v_lru_b_a),
        "lru_w_x": (lru_w_x, m_lru_w_x, v_lru_w_x), "lru_b_x": (lru_b_x, m_lru_b_x, v_lru_b_x),
        "lru_lam": (lru_lam, m_lru_lam, v_lru_lam), "ret_gn_w": (ret_gn_w, m_ret_gn_w, v_ret_gn_w),
        "na_rpb": (na_rpb, m_na_rpb, v_na_rpb), "ln1_g": (ln1_g, m_ln1_g, v_ln1_g), "ln1_b": (ln1_b, m_ln1_b, v_ln1_b),
        "ln2_g": (ln2_g, m_ln2_g, v_ln2_g), "ln2_b": (ln2_b, m_ln2_b, v_ln2_b),
    }
    for name, (w_, m_, v_) in small.items():
        big[name] = _adamw_nd(w_, grads[name], m_, v_, None, "adamw_small")
    kinds = [{n: big[n][k] for n in big} for k in range(3)]
    order = ("w_in", "conv_w", "conv_b", "lru_w_a", "lru_b_a", "lru_w_x", "lru_b_x", "lru_lam", "ret_gn_w", "na_rpb",
             "w_out", "ln1_g", "ln1_b", "w_gate", "w_up", "w_down", "ln2_g", "ln2_b")
    outs = [loss, dx[None]]
    for d in (grads, *kinds):
        outs.extend(d[n] for n in order)
    return tuple(outs)
```

```python
import functools
import math

import numpy as np
import jax
import jax.numpy as jnp
from jax import lax
from jax.experimental import pallas as pl
from jax.experimental.pallas import tpu as pltpu

F32 = jnp.float32
_BF = jnp.bfloat16

D_MODEL = 1024
DEPTH = 4
GRID_W = 64
HEAD_DIM = 64
LRU_WIDTH = 384
RET_WIDTH = 384
RET_HEADS = 6
NA_WIDTH = 256
NA_HEADS = 4
IN_WIDTH = 3072
CONV_WIDTH = 4
LRU_C = 8.0
RET_CHUNK = 128
ROPE_BASE = 10000.0
GN_EPS = 1e-6
NA_KH = 8
NA_KW = 16
D_FF = 2816
FF_BLK = 352
N_DEV = 8
ALPHA = (2 * DEPTH) ** 0.25
LN_EPS = 1e-5
ADAM_LR = 0.001
ADAM_B1 = 0.9
ADAM_B2 = 0.999
ADAM_EPS = 1e-08
ADAM_WD = 0.01
ADAM_STEP = 10

LANE = 128
SUB = 8
VMEM_MB = 56
NEG = -1e30

MESH = pl.DeviceIdType.MESH


def _cparams(sem=None, vmem_mb=VMEM_MB):
    return pltpu.CompilerParams(dimension_semantics=sem, vmem_limit_bytes=vmem_mb << 20)


def _mm(a, b):
    return jnp.dot(a.astype(_BF), b.astype(_BF), preferred_element_type=F32)


def _mm_nt(a, b):
    return lax.dot_general(a.astype(_BF), b.astype(_BF), (((1,), (1,)), ((), ())), preferred_element_type=F32)


def _mm_tn(a, b):
    return lax.dot_general(a.astype(_BF), b.astype(_BF), (((0,), (0,)), ((), ())), preferred_element_type=F32)


def _sigmoid(x):
    return jax.nn.sigmoid(x)


def _rows(start, size):
    return pl.ds(pl.multiple_of(start, SUB), size)


def _loop2(n, body, init):
    assert n % 2 == 0
    return lax.fori_loop(0, n // 2, lambda i, c: body(2 * i + 1, body(2 * i, c)), init)


def _strip(T, col, buffers=2):
    return pl.BlockSpec((T, LANE), lambda j: (0, col(j)), pipeline_mode=pl.Buffered(buffers))


LRU_CH = 256
_GELU_C0 = math.sqrt(2.0 / math.pi)
_GELU_C1 = 0.044715


def _gelu_parts(x):
    x2 = x * x
    t = jnp.tanh(_GELU_C0 * (x + _GELU_C1 * x * x2))
    val = 0.5 * x * (1.0 + t)
    der = 0.5 * (1.0 + t) + 0.5 * x * (1.0 - t * t) * _GELU_C0 * (1.0 + 3.0 * _GELU_C1 * x2)
    return val, der


def _softplus_neg(lam):
    e = jnp.exp(-jnp.abs(lam))
    w = 1.0 + e
    l1p = jnp.where(w == 1.0, e, jnp.log(w) * (e / jnp.where(w == 1.0, 1.0, w - 1.0)))
    return jnp.maximum(-lam, 0.0) + l1p


def _window(ref, t0, ch, T):
    prev = ref[_rows(jnp.maximum(t0 - SUB, 0), SUB), :].astype(F32)
    nxt = ref[_rows(jnp.minimum(t0 + ch, T - SUB), SUB), :].astype(F32)
    prev = jnp.where(t0 > 0, prev, 0.0)
    nxt = jnp.where(t0 + ch < T, nxt, 0.0)
    return jnp.concatenate([prev, ref[_rows(t0, ch), :].astype(F32), nxt], axis=0)


def _tap(win, shift, ch):
    n = win.shape[0]
    return pltpu.roll(win, (-shift) % n, 0)[SUB:SUB + ch]


def _lru_conv(xb_ref, vec, t0, T):
    win = _window(xb_ref, t0, LRU_CH, T)
    xc = jnp.broadcast_to(vec[4:5, :], (LRU_CH, LANE))
    for j in range(CONV_WIDTH):
        xc = xc + _tap(win, j - CONV_WIDTH // 2, LRU_CH) * vec[j:j + 1, :]
    return xc


def _lru_dir(pre_a, pre_x, sp):
    r = _sigmoid(pre_a)
    i = _sigmoid(pre_x)
    log_a = (-LRU_C) * r * sp
    a = jnp.exp(log_a)
    z = jnp.tanh(-log_a) * (a * a + 1.0)
    s = jnp.sqrt(z)
    return r, i, a, s


def _scan_tile(a, b, reverse, row):
    for k in (1, 2, 4):
        if not reverse:
            a_s, b_s, m = pltpu.roll(a, k, 0), pltpu.roll(b, k, 0), row >= k
        else:
            a_s, b_s, m = pltpu.roll(a, SUB - k, 0), pltpu.roll(b, SUB - k, 0), row < SUB - k
        b = jnp.where(m, a * b_s + b, b)
        a = jnp.where(m, a * a_s, a)
    return a, b


def _bcast_row(x, r):
    return jnp.broadcast_to(x[r:r + 1, :], (SUB, LANE))


def _lru_prepare(xb_ref, w4_ref, vec, xc_ref, af_ref, uf_ref, ab_ref, ub_ref, T):
    sp_f = _softplus_neg(vec[9:10, :])
    sp_b = _softplus_neg(vec[10:11, :])
    w4 = w4_ref[0]

    def body(c, carry):
        t0 = c * LRU_CH
        xc = _lru_conv(xb_ref, vec, t0, T)
        if xc_ref is not None:
            xc_ref[_rows(t0, LRU_CH), :] = xc
        pre = _mm(xc, w4)
        _, i, a, s = _lru_dir(pre[:, 0:128] + vec[5:6, :], pre[:, 128:256] + vec[6:7, :], sp_f)
        af_ref[_rows(t0, LRU_CH), :] = a
        uf_ref[_rows(t0, LRU_CH), :] = s * (i * xc)
        _, i, a, s = _lru_dir(pre[:, 256:384] + vec[7:8, :], pre[:, 384:512] + vec[8:9, :], sp_b)
        ab_ref[_rows(t0, LRU_CH), :] = a
        ub_ref[_rows(t0, LRU_CH), :] = s * (i * xc)
        return carry

    lax.fori_loop(0, T // LRU_CH, body, 0)


def _lru_scan(af_ref, uf_ref, ab_ref, ub_ref, T):
    nt = T // SUB
    row = lax.broadcasted_iota(jnp.int32, (SUB, LANE), 0)

    def body(j, carry):
        hf, hb = carry
        sf = _rows(j * SUB, SUB)
        sb = _rows((nt - 1 - j) * SUB, SUB)
        a, b = _scan_tile(af_ref[sf, :], uf_ref[sf, :], False, row)
        h = a * hf + b
        uf_ref[sf, :] = h
        hf = _bcast_row(h, SUB - 1)
        a, b = _scan_tile(ab_ref[sb, :], ub_ref[sb, :], True, row)
        h = a * hb + b
        ub_ref[sb, :] = h
        hb = _bcast_row(h, 0)
        return hf, hb

    z = jnp.zeros((SUB, LANE), F32)
    lax.fori_loop(0, nt, body, (z, z))


def _lru_fwd_call(proj, vec, w4):
    T = proj.shape[0]

    def body(xb_ref, gate_ref, vec_ref, w4_ref, y_ref, af_ref, uf_ref, ab_ref, ub_ref):
        vec = vec_ref[...]
        _lru_prepare(xb_ref, w4_ref, vec, None, af_ref, uf_ref, ab_ref, ub_ref, T)
        _lru_scan(af_ref, uf_ref, ab_ref, ub_ref, T)

        def out(c, carry):
            rows = _rows(c * LRU_CH, LRU_CH)
            gl, _ = _gelu_parts(gate_ref[rows, :])
            y_ref[rows, :] = (uf_ref[rows, :] + ub_ref[rows, :]) * gl
            return carry

        lax.fori_loop(0, T // LRU_CH, out, 0)

    return pl.pallas_call(
        body, name="lru_fwd", grid=(LRU_WIDTH // LANE,),
        in_specs=[_strip(T, lambda j: j), _strip(T, lambda j: j + 3),
                  pl.BlockSpec((16, LANE), lambda j: (0, j)),
                  pl.BlockSpec((1, LANE, 4 * LANE), lambda j: (j, 0, 0))],
        out_specs=_strip(T, lambda j: j, buffers=1),
        out_shape=jax.ShapeDtypeStruct((T, LRU_WIDTH), F32),
        scratch_shapes=[pltpu.VMEM((T, LANE), F32)] * 4,
        compiler_params=_cparams(("arbitrary",)),
    )(proj, proj, vec, w4)


def _store_strips(stage_ref, dp_ref, cols, sems):
    copies = [pltpu.make_async_copy(stage_ref.at[b], dp_ref.at[:, pl.ds(pl.multiple_of(c * LANE, LANE), LANE)], sems.at[b])
              for b, c in enumerate(cols)]
    for cp in copies:
        cp.start()
    for cp in copies:
        cp.wait()


def _lru_bwd_call(proj, dycat, vec, w4, after):
    T = proj.shape[0]
    nt = T // SUB
    nch = T // LRU_CH

    def body(xb_ref, gate_ref, dy_ref, vec_ref, w4_ref, after_ref, dp_ref, dvec_ref, dw4_ref,
             xc_ref, af_ref, hf_ref, ab_ref, hb_ref, dh_ref, stage_ref, sems):
        dxb_ref, dgate_ref = stage_ref.at[0], stage_ref.at[1]
        vec = vec_ref[...]
        _lru_prepare(xb_ref, w4_ref, vec, xc_ref, af_ref, hf_ref, ab_ref, hb_ref, T)
        _lru_scan(af_ref, hf_ref, ab_ref, hb_ref, T)

        def gate_bwd(c, carry):
            rows = _rows(c * LRU_CH, LRU_CH)
            gl, dgl = _gelu_parts(gate_ref[rows, :])
            dy = dy_ref[rows, :]
            dgate_ref[rows, :] = (dy * (hf_ref[rows, :] + hb_ref[rows, :]) * dgl).astype(dgate_ref.dtype)
            dh_ref[rows, :] = dy * gl
            return carry

        lax.fori_loop(0, nch, gate_bwd, 0)

        row = lax.broadcasted_iota(jnp.int32, (SUB, LANE), 0)

        def adj(j, carry):
            gf, a_next, gb, a_prev = carry
            tf = nt - 1 - j
            sf = _rows(tf * SUB, SUB)
            a_t = af_ref[sf, :]
            h_t = hf_ref[sf, :]
            coef = jnp.where(row == SUB - 1, a_next, pltpu.roll(a_t, SUB - 1, 0))
            ac, bc = _scan_tile(coef, dh_ref[sf, :], True, row)
            g = ac * gf + bc
            h_prev = hf_ref[_rows(jnp.maximum(tf - 1, 0) * SUB, SUB), :]
            h_prev = jnp.where(tf > 0, _bcast_row(h_prev, SUB - 1), 0.0)
            hs = jnp.where(row == 0, h_prev, pltpu.roll(h_t, 1, 0))
            af_ref[sf, :] = g * hs
            hf_ref[sf, :] = g
            gf = _bcast_row(g, 0)
            a_next = _bcast_row(a_t, 0)
            sb = _rows(j * SUB, SUB)
            a_t = ab_ref[sb, :]
            h_t = hb_ref[sb, :]
            coef = jnp.where(row == 0, a_prev, pltpu.roll(a_t, 1, 0))
            ac, bc = _scan_tile(coef, dh_ref[sb, :], False, row)
            g = ac * gb + bc
            h_next = hb_ref[_rows(jnp.minimum(j + 1, nt - 1) * SUB, SUB), :]
            h_next = jnp.where(j < nt - 1, _bcast_row(h_next, 0), 0.0)
            hs = jnp.where(row == SUB - 1, h_next, pltpu.roll(h_t, SUB - 1, 0))
            ab_ref[sb, :] = g * hs
            hb_ref[sb, :] = g
            gb = _bcast_row(g, SUB - 1)
            a_prev = _bcast_row(a_t, SUB - 1)
            return gf, a_next, gb, a_prev

        z = jnp.zeros((SUB, LANE), F32)
        lax.fori_loop(0, nt, adj, (z, z, z, z))

        sp_f = _softplus_neg(vec[9:10, :])
        sp_b = _softplus_neg(vec[10:11, :])
        w4 = w4_ref[0]
        dw4_ref[...] = jnp.zeros_like(dw4_ref)

        def one_dir(pre_a, pre_x, sp, xc, du, da):
            r, i, a, s = _lru_dir(pre_a, pre_x, sp)
            d_i = du * s * xc
            dxc = du * s * i
            d_s = du * i * xc
            d_log = da * a - d_s * (a * a) / s
            d_r = d_log * (-LRU_C) * sp
            d_sp = jnp.sum(d_log * (-LRU_C) * r, axis=0, keepdims=True)
            return d_r * r * (1.0 - r), d_i * i * (1.0 - i), dxc, d_sp

        def gates_bwd(c, carry):
            db, dspf, dspb = carry
            rows = _rows(c * LRU_CH, LRU_CH)
            xc = xc_ref[rows, :]
            pre = _mm(xc, w4)
            dpa_f, dpx_f, dxc_f, d_sp_f = one_dir(pre[:, 0:128] + vec[5:6, :], pre[:, 128:256] + vec[6:7, :],
                                                  sp_f, xc, hf_ref[rows, :], af_ref[rows, :])
            dpa_b, dpx_b, dxc_b, d_sp_b = one_dir(pre[:, 256:384] + vec[7:8, :], pre[:, 384:512] + vec[8:9, :],
                                                  sp_b, xc, hb_ref[rows, :], ab_ref[rows, :])
            dpre = jnp.concatenate([dpa_f, dpx_f, dpa_b, dpx_b], axis=1)
            dw4_ref[0] += _mm_tn(xc, dpre)
            dh_ref[rows, :] = dxc_f + dxc_b + _mm_nt(dpre, w4)
            return db + jnp.sum(dpre, axis=0, keepdims=True), dspf + d_sp_f, dspb + d_sp_b

        z1 = jnp.zeros((1, LANE), F32)
        db, dspf, dspb = lax.fori_loop(0, nch, gates_bwd, (jnp.zeros((1, 4 * LANE), F32), z1, z1))

        def conv_bwd(c, carry):
            t0 = c * LRU_CH
            rows = _rows(t0, LRU_CH)
            dwin = _window(dh_ref, t0, LRU_CH, T)
            xwin = _window(xb_ref, t0, LRU_CH, T)
            dxc = dh_ref[rows, :]
            dxb = jnp.zeros((LRU_CH, LANE), F32)
            out = []
            for j in range(CONV_WIDTH):
                off = j - CONV_WIDTH // 2
                dxb = dxb + _tap(dwin, -off, LRU_CH) * vec[j:j + 1, :]
                out.append(carry[j] + jnp.sum(dxc * _tap(xwin, off, LRU_CH), axis=0, keepdims=True))
            dxb_ref[rows, :] = dxb.astype(dxb_ref.dtype)
            out.append(carry[CONV_WIDTH] + jnp.sum(dxc, axis=0, keepdims=True))
            return tuple(out)

        dconv = lax.fori_loop(0, nch, conv_bwd, (z1,) * (CONV_WIDTH + 1))
        dlam_f = dspf * (-_sigmoid(-vec[9:10, :]))
        dlam_b = dspb * (-_sigmoid(-vec[10:11, :]))
        dvec_ref[...] = jnp.concatenate(
            list(dconv) + [db[:, 0:128], db[:, 128:256], db[:, 256:384], db[:, 384:512], dlam_f, dlam_b,
                           jnp.zeros((5, LANE), F32)], axis=0)
        j = pl.program_id(0)
        _store_strips(stage_ref, dp_ref, (j, j + 3), sems)

    ns = LRU_WIDTH // LANE
    return pl.pallas_call(
        body, name="lru_bwd", grid=(ns,),
        in_specs=[_strip(T, lambda j: j), _strip(T, lambda j: j + 3), _strip(T, lambda j: j),
                  pl.BlockSpec((16, LANE), lambda j: (0, j)),
                  pl.BlockSpec((1, LANE, 4 * LANE), lambda j: (j, 0, 0)),
                  pl.BlockSpec(memory_space=pl.ANY)],
        out_specs=[pl.BlockSpec(memory_space=pl.ANY),
                   pl.BlockSpec((16, LANE), lambda j: (0, j)),
                   pl.BlockSpec((1, LANE, 4 * LANE), lambda j: (j, 0, 0))],
        out_shape=[jax.ShapeDtypeStruct((T, IN_WIDTH), _BF),
                   jax.ShapeDtypeStruct((16, LRU_WIDTH), F32), jax.ShapeDtypeStruct((ns, LANE, 4 * LANE), F32)],
        scratch_shapes=[pltpu.VMEM((T, LANE), F32)] * 6 + [pltpu.VMEM((2, T, LANE), _BF), pltpu.SemaphoreType.DMA((2,))],
        compiler_params=_cparams(("arbitrary",)),
    )(proj, proj, dycat, vec, w4, after)


def _lru_vec(cw, cb, ba, bx, lam):
    return jnp.concatenate([cw, cb[None], ba[0:1], bx[0:1], ba[1:2], bx[1:2], lam, jnp.zeros((5, LRU_WIDTH), F32)], axis=0)


def _lru_w4(wa, wx):
    nl = wa.shape[0]
    w = jnp.stack([wa[:, 0], wx[:, 0], wa[:, 1], wx[:, 1]], axis=1)
    w = w.reshape(nl, 4, 3, 2, 64, 64)
    eye = jnp.eye(2, dtype=w.dtype)
    bd = w[:, :, :, :, :, None, :] * eye[None, None, None, :, None, :, None]
    bd = bd.reshape(nl, 4, 3, LANE, LANE)
    return bd.transpose(0, 2, 3, 1, 4).reshape(nl, 3, LANE, 4 * LANE).astype(_BF)


def _lru_unpack(dvec, dw4):
    def blocks(m):
        m = m.reshape(3, 2, 64, 2, 64)
        return jnp.stack([m[:, 0, :, 0, :], m[:, 1, :, 1, :]], axis=1).reshape(6, 64, 64)
    parts = [blocks(dw4[:, :, k * LANE:(k + 1) * LANE]) for k in range(4)]
    dwa = jnp.stack([parts[0], parts[2]])
    dwx = jnp.stack([parts[1], parts[3]])
    dba = jnp.stack([dvec[5], dvec[7]])
    dbx = jnp.stack([dvec[6], dvec[8]])
    return dvec[0:4], dvec[4], dwa, dba, dwx, dbx, dvec[9:11]


RC = 2 * RET_CHUNK


def _ret_tables(T):
    half = HEAD_DIM // 2
    pos = jnp.arange(T, dtype=F32)
    inv_freq = ROPE_BASE ** (-jnp.arange(half, dtype=F32) / half)
    ang = pos[:, None] * inv_freq[None, :]
    cos = jnp.tile(jnp.cos(ang), (1, 4))
    sin = jnp.tile(jnp.concatenate([-jnp.sin(ang), jnp.sin(ang)], axis=1), (1, 2))
    log_g = jnp.log1p(-jnp.exp2(-5.0 - jnp.arange(RET_HEADS, dtype=F32)))
    idx = jnp.arange(RC, dtype=F32)
    dec = jnp.exp(jnp.abs(idx[:, None] - idx[None, :]) * log_g[:, None, None])
    lg = jnp.repeat(log_g, HEAD_DIM).reshape(3, 1, LANE)
    col = idx[None, :, None]
    rtab = jnp.stack([jnp.exp((RC - 1 - col) * lg), jnp.exp(col * lg),
                      jnp.exp((col + 1.0) * lg), jnp.exp((RC - col) * lg)], axis=1)
    gch = jnp.broadcast_to(jnp.exp(RC * lg), (3, SUB, LANE))
    return cos, sin, dec, rtab, gch


def _swap32(x, lane):
    return jnp.where((lane & 32) == 0, pltpu.roll(x, LANE - 32, 1), pltpu.roll(x, 32, 1))


def _head_mean(x, m0, m1):
    s0 = jnp.sum(x * m0, axis=-1, keepdims=True)
    s1 = jnp.sum(x * m1, axis=-1, keepdims=True)
    return (s0 * m0 + s1 * m1) * (1.0 / HEAD_DIM)


def _ret_masks():
    lane = lax.broadcasted_iota(jnp.int32, (RC, LANE), 1)
    m0 = (lane < HEAD_DIM).astype(F32)
    r = lax.broadcasted_iota(jnp.int32, (LANE, LANE), 0) // HEAD_DIM
    c = lax.broadcasted_iota(jnp.int32, (LANE, LANE), 1) // HEAD_DIM
    return lane, m0, 1.0 - m0, (r == c).astype(F32)


def _ret_specs(T):
    const = lambda shape, imap: pl.BlockSpec(shape, imap)
    return [_strip(T, lambda j: j + 6), _strip(T, lambda j: j + 9), _strip(T, lambda j: j + 12),
            _strip(T, lambda j: j + 15),
            pl.BlockSpec((T, LANE), lambda j: (0, 0), pipeline_mode=pl.Buffered(1)),
            pl.BlockSpec((T, LANE), lambda j: (0, 0), pipeline_mode=pl.Buffered(1)),
            const((2, RC, RC), lambda j: (j, 0, 0)),
            const((1, 4, RC, LANE), lambda j: (j, 0, 0, 0)),
            const((1, SUB, LANE), lambda j: (j, 0, 0)),
            const((SUB, LANE), lambda j: (0, j))]


def _ret_fwd_call(proj, tables, gnw8):
    T = proj.shape[0]
    nc = T // RC
    cos, sin, dec, rtab, gch = tables

    def body(q_ref, k_ref, v_ref, g_ref, cos_ref, sin_ref, dec_ref, rtab_ref, gch_ref, gnw_ref, y_ref, stf_ref, kr_ref):
        lane, m0, m1, bd = _ret_masks()
        gch_v = gch_ref[0][0:1, :]
        gnw = gnw_ref[0:1, :]
        dkf, dkb, dqf, dqb = rtab_ref[0, 0], rtab_ref[0, 1], rtab_ref[0, 2], rtab_ref[0, 3]

        def rope(x, rows):
            return x * cos_ref[rows, :] + _swap32(x, lane) * sin_ref[rows, :]

        def pass_a(n, st):
            rows = _rows(n * RC, RC)
            stf_ref[n] = st
            kr = rope(k_ref[rows, :], rows) * (HEAD_DIM ** -0.5)
            kr_ref[rows, :] = kr
            return gch_v * st + _mm_tn(kr * dkf, v_ref[rows, :]) * bd

        _loop2(nc, pass_a, jnp.zeros((LANE, LANE), F32))

        def pass_b(i, stb):
            ns = [nc - 1 - 2 * i, nc - 2 - 2 * i]
            rows = [_rows(n * RC, RC) for n in ns]
            heads = ((0, m0), (1, m1))
            qr = [rope(q_ref[r, :], r) for r in rows]
            kr = [kr_ref[r, :] for r in rows]
            v = [v_ref[r, :] for r in rows]
            kv = [_mm_tn(kr[c] * dkb, v[c]) * bd for c in range(2)]
            stbs = [stb, gch_v * stb + kv[0]]
            s = [[_mm_nt(qr[c] * m, kr[c]) * dec_ref[h] for h, m in heads] for c in range(2)]
            o = [_mm(qr[c] * dqf, stf_ref[ns[c]]) + _mm(qr[c] * dqb, stbs[c]) for c in range(2)]
            o = [o[c] + _mm(s[c][0], v[c] * m0) + _mm(s[c][1], v[c] * m1) for c in range(2)]
            oc = [o_ - _head_mean(o_, m0, m1) for o_ in o]
            on = [oc_ * lax.rsqrt(_head_mean(oc_ * oc_, m0, m1) + GN_EPS) for oc_ in oc]
            for c in range(2):
                g = g_ref[rows[c], :]
                y_ref[rows[c], :] = (g * _sigmoid(g)) * (on[c] * gnw)
            return gch_v * stbs[1] + kv[1]

        assert nc % 2 == 0
        lax.fori_loop(0, nc // 2, pass_b, jnp.zeros((LANE, LANE), F32))

    return pl.pallas_call(
        body, name="ret_fwd", grid=(RET_WIDTH // LANE,),
        in_specs=_ret_specs(T),
        out_specs=_strip(T, lambda j: j, buffers=1),
        out_shape=jax.ShapeDtypeStruct((T, RET_WIDTH), F32),
        scratch_shapes=[pltpu.VMEM((nc, LANE, LANE), F32), pltpu.VMEM((T, LANE), F32)],
        compiler_params=_cparams(("arbitrary",)),
    )(proj, proj, proj, proj, cos, sin, dec, rtab, gch, gnw8)


def _ret_bwd_call(proj, dycat, tables, gnw8, dp):
    T = proj.shape[0]
    nc = T // RC
    cos, sin, dec, rtab, gch = tables

    def body(q_ref, k_ref, v_ref, g_ref, cos_ref, sin_ref, dec_ref, rtab_ref, gch_ref, gnw_ref, dy_ref, dp_in_ref,
             dp_out_ref, dgnw_ref, stf_ref, dstb_ref, dkr_ref, dv_ref, dp_ref, kr_ref, sems):
        lane, m0, m1, bd = _ret_masks()
        gch_v = gch_ref[0][0:1, :]
        gnw = gnw_ref[0:1, :]
        dkf, dkb, dqf, dqb = rtab_ref[0, 0], rtab_ref[0, 1], rtab_ref[0, 2], rtab_ref[0, 3]
        scale = HEAD_DIM ** -0.5
        zst = jnp.zeros((LANE, LANE), F32)

        def rope(x, rows):
            return x * cos_ref[rows, :] + _swap32(x, lane) * sin_ref[rows, :]

        def rope_t(d, rows):
            return d * cos_ref[rows, :] + _swap32(d * sin_ref[rows, :], lane)

        def pass_a(n, st):
            rows = _rows(n * RC, RC)
            stf_ref[n] = st
            kr = rope(k_ref[rows, :], rows) * scale
            kr_ref[rows, :] = kr
            return gch_v * st + _mm_tn(kr * dkf, v_ref[rows, :]) * bd

        _loop2(nc, pass_a, zst)

        def pass_b(i, carry):
            stb, d_f, dgnw = carry
            two = range(2)
            heads = ((0, m0), (1, m1))
            ns = [nc - 1 - 2 * i, nc - 2 - 2 * i]
            rows = [_rows(n * RC, RC) for n in ns]
            qr = [rope(q_ref[r, :], r) for r in rows]
            kr = [kr_ref[r, :] for r in rows]
            v = [v_ref[r, :] for r in rows]
            stf = [stf_ref[n] for n in ns]
            kvb = [_mm_tn(kr[c] * dkb, v[c]) * bd for c in two]
            stbs = [stb, gch_v * stb + kvb[0]]
            qf = [qr[c] * dqf for c in two]
            qb = [qr[c] * dqb for c in two]
            s = [[_mm_nt(qr[c] * m, kr[c]) * dec_ref[h] for h, m in heads] for c in two]
            o = [_mm(qf[c], stf[c]) + _mm(qb[c], stbs[c]) for c in two]
            o = [o[c] + _mm(s[c][0], v[c] * m0) + _mm(s[c][1], v[c] * m1) for c in two]
            oc = [o_ - _head_mean(o_, m0, m1) for o_ in o]
            rstd = [lax.rsqrt(_head_mean(oc_ * oc_, m0, m1) + GN_EPS) for oc_ in oc]
            on = [oc[c] * rstd[c] for c in two]
            do = []
            for c in two:
                g = g_ref[rows[c], :]
                sg = _sigmoid(g)
                dy = dy_ref[rows[c], :]
                dp_ref[3, rows[c], :] = (dy * (on[c] * gnw) * (sg * (1.0 + g * (1.0 - sg)))).astype(dp_ref.dtype)
                t = dy * (g * sg)
                dgnw = dgnw + jnp.sum(t * on[c], axis=0, keepdims=True)
                don = t * gnw
                do.append(rstd[c] * (don - _head_mean(don, m0, m1) - on[c] * _head_mean(don * on[c], m0, m1)))
            dstf = [_mm_tn(qf[c], do[c]) * bd for c in two]
            dfs = [d_f, dstf[0] + gch_v * d_f]
            ds = [[_mm_nt(do[c] * m, v[c]) * dec_ref[h] for h, m in heads] for c in two]
            dqr = [_mm_nt(do[c], stf[c]) * dqf + _mm_nt(do[c], stbs[c]) * dqb
                   + _mm(ds[c][0], kr[c] * m0) + _mm(ds[c][1], kr[c] * m1) for c in two]
            dkr = [_mm_nt(v[c], dfs[c]) * dkf + _mm_tn(ds[c][0], qr[c] * m0) + _mm_tn(ds[c][1], qr[c] * m1) for c in two]
            dv = [_mm(kr[c] * dkf, dfs[c]) + _mm_tn(s[c][0], do[c] * m0) + _mm_tn(s[c][1], do[c] * m1) for c in two]
            for c in two:
                dp_ref[0, rows[c], :] = rope_t(dqr[c], rows[c]).astype(dp_ref.dtype)
                dkr_ref[rows[c], :] = dkr[c]
                dv_ref[rows[c], :] = dv[c]
                dstb_ref[ns[c]] = _mm_tn(qb[c], do[c]) * bd
            return gch_v * stbs[1] + kvb[1], dstf[1] + gch_v * dfs[1], dgnw

        assert nc % 2 == 0
        _, _, dgnw = lax.fori_loop(0, nc // 2, pass_b, (zst, zst, jnp.zeros((1, LANE), F32)))
        dgnw_ref[...] = jnp.concatenate([dgnw, jnp.zeros((SUB - 1, LANE), F32)], axis=0)

        def pass_c(n, d_b):
            rows = _rows(n * RC, RC)
            kr = kr_ref[rows, :]
            v = v_ref[rows, :]
            dkr = dkr_ref[rows, :] + _mm_nt(v, d_b) * dkb
            dp_ref[1, rows, :] = (rope_t(dkr, rows) * scale).astype(dp_ref.dtype)
            dp_ref[2, rows, :] = (dv_ref[rows, :] + _mm(kr * dkb, d_b)).astype(dp_ref.dtype)
            return dstb_ref[n] + gch_v * d_b

        _loop2(nc, pass_c, zst)
        j = pl.program_id(0)
        _store_strips(dp_ref, dp_out_ref, (j + 6, j + 9, j + 12, j + 15), sems)

    n_in = len(_ret_specs(T)) + 1
    return pl.pallas_call(
        body, name="ret_bwd", grid=(RET_WIDTH // LANE,),
        in_specs=_ret_specs(T) + [_strip(T, lambda j: j + 3), pl.BlockSpec(memory_space=pl.ANY)],
        out_specs=[pl.BlockSpec(memory_space=pl.ANY), pl.BlockSpec((SUB, LANE), lambda j: (0, j))],
        out_shape=[jax.ShapeDtypeStruct(dp.shape, dp.dtype), jax.ShapeDtypeStruct((SUB, RET_WIDTH), F32)],
        scratch_shapes=[pltpu.VMEM((nc, LANE, LANE), F32), pltpu.VMEM((nc, LANE, LANE), F32),
                        pltpu.VMEM((T, LANE), F32), pltpu.VMEM((T, LANE), F32),
                        pltpu.VMEM((4, T, LANE), _BF), pltpu.VMEM((T, LANE), F32), pltpu.SemaphoreType.DMA((4,))],
        input_output_aliases={n_in: 0},
        compiler_params=_cparams(("arbitrary",)),
    )(proj, proj, proj, proj, cos, sin, dec, rtab, gch, gnw8, dycat, dp)


NA_Q = 2 * GRID_W
NA_WROWS = 10
NA_K = NA_WROWS * GRID_W
NA_CHUNKS = NA_K // LANE
NA_UNROLL = 2
NA_TYPES = 5
_ONEHOT_PRECISION = lax.Precision.HIGH


def _na_onehots(rows_n):
    reps = [(0, 0), (2, 0), (4, 0), (rows_n - 4, rows_n - NA_WROWS), (rows_n - 2, rows_n - NA_WROWS)]
    rm = np.zeros((NA_TYPES, 2, NA_WROWS, 2 * NA_KH - 1), np.float32)
    for t, (r, ws) in enumerate(reps):
        for qh in range(2):
            qrow = r + qh
            rstart = min(max(qrow - NA_KH // 2, 0), rows_n - NA_KH)
            for kh in range(NA_WROWS):
                krow = ws + kh
                if rstart <= krow < rstart + NA_KH:
                    rm[t, qh, kh, krow - qrow + NA_KH - 1] = 1.0
    cm = np.zeros((GRID_W, GRID_W, 2 * NA_KW - 1), np.float32)
    for qc in range(GRID_W):
        cstart = min(max(qc - NA_KW // 2, 0), GRID_W - NA_KW)
        for kc in range(cstart, cstart + NA_KW):
            cm[qc, kc, kc - qc + NA_KW - 1] = 1.0
    rm2 = rm.reshape(NA_TYPES, 2, NA_CHUNKS, 2, 2 * NA_KH - 1)
    cm2 = np.zeros((GRID_W, LANE, 2, 2 * NA_KW - 1), np.float32)
    for z in range(2):
        cm2[:, z * GRID_W:(z + 1) * GRID_W, z, :] = cm
    return rm2, cm2


def _na_bias_tables(rpb, rows_n):
    rm, cm = _na_onehots(rows_n)
    val = jnp.einsum("hab,tqpza,xkzb->htpqxk", rpb, rm, cm, precision=_ONEHOT_PRECISION)
    valid = np.einsum("tqpz,xkz->tpqxk", rm.sum(-1), cm.sum(-1)) > 0.5
    return jnp.where(valid[None], val, NEG).reshape(2, 2, NA_TYPES, NA_CHUNKS, NA_Q, LANE)


def _na_bias_grad(dtab, rows_n):
    rm, cm = _na_onehots(rows_n)
    d6 = dtab.reshape(NA_HEADS, NA_TYPES, NA_CHUNKS, 2, GRID_W, LANE)
    return jnp.einsum("htpqxk,tqpza,xkzb->hab", d6, rm, cm, precision=_ONEHOT_PRECISION)


def _na_bias(b_ref, h, typ):
    return jnp.concatenate([b_ref[0, h, typ, c] for c in range(NA_CHUNKS)], axis=1)


def _na_step(p, npairs, rows_n):
    ws = jnp.clip(2 * p - NA_KH // 2, 0, rows_n - NA_WROWS)
    koff = pl.multiple_of(ws * GRID_W, LANE)
    typ = jnp.where(p == 0, 0, jnp.where(p == 1, 1, jnp.where(p == npairs - 2, 3, jnp.where(p == npairs - 1, 4, 2))))
    return _rows(p * NA_Q, NA_Q), pl.ds(koff, NA_K), typ


def _na_fwd_call(proj, btab):
    T = proj.shape[0]
    npairs, rows_n = T // NA_Q, T // GRID_W

    def body(q_ref, k_ref, v_ref, b_ref, o_ref):
        lane = lax.broadcasted_iota(jnp.int32, (NA_Q, LANE), 1)
        m0 = (lane < HEAD_DIM).astype(F32)
        m1 = 1.0 - m0

        def steps(i, carry):
            idx = [_na_step(NA_UNROLL * i + u, npairs, rows_n) for u in range(NA_UNROLL)]
            chains = [(u, h, m) for u in range(NA_UNROLL) for h, m in ((0, m0), (1, m1))]
            kws = [k_ref[krows, :].astype(_BF) for _, krows, _ in idx]
            vws = [v_ref[krows, :].astype(_BF) for _, krows, _ in idx]
            s = [_mm_nt(q_ref[idx[u][0], :] * m, kws[u]) for u, h, m in chains]
            s = [s_ * (HEAD_DIM ** -0.5) + _na_bias(b_ref, h, idx[u][2]) for s_, (u, h, m) in zip(s, chains)]
            e = [jnp.exp(s_ - jnp.max(s_, axis=-1, keepdims=True)) for s_ in s]
            pr = [e_ / jnp.sum(e_, axis=-1, keepdims=True) for e_ in e]
            ov = [_mm(pr_, vws[u]) * m for pr_, (u, h, m) in zip(pr, chains)]
            for u in range(NA_UNROLL):
                o_ref[idx[u][0], :] = ov[2 * u] + ov[2 * u + 1]
            return carry

        lax.fori_loop(0, npairs // NA_UNROLL, steps, 0)

    return pl.pallas_call(
        body, name="na_fwd", grid=(NA_WIDTH // LANE,),
        in_specs=[_strip(T, lambda j: j + 18), _strip(T, lambda j: j + 20), _strip(T, lambda j: j + 22),
                  pl.BlockSpec((1, 2, NA_TYPES, NA_CHUNKS, NA_Q, LANE), lambda j: (j, 0, 0, 0, 0, 0))],
        out_specs=_strip(T, lambda j: j, buffers=1),
        out_shape=jax.ShapeDtypeStruct((T, NA_WIDTH), F32),
        compiler_params=_cparams(("arbitrary",)),
    )(proj, proj, proj, btab)


def _na_bwd_call(proj, dycat, btab, dp):
    T = proj.shape[0]
    npairs, rows_n = T // NA_Q, T // GRID_W
    scale = HEAD_DIM ** -0.5

    def body(q_ref, k_ref, v_ref, do_ref, b_ref, dp_in_ref, dp_out_ref, db_ref, dka_ref, dva_ref, stage_ref, sems):
        dq_ref = stage_ref.at[0]
        lane = lax.broadcasted_iota(jnp.int32, (NA_Q, LANE), 1)
        m0 = (lane < HEAD_DIM).astype(F32)
        m1 = 1.0 - m0
        dka_ref[...] = jnp.zeros_like(dka_ref)
        dva_ref[...] = jnp.zeros_like(dva_ref)
        db_ref[...] = jnp.zeros_like(db_ref)

        def steps(i, carry):
            idx = [_na_step(NA_UNROLL * i + u, npairs, rows_n) for u in range(NA_UNROLL)]
            chains = [(u, h, m) for u in range(NA_UNROLL) for h, m in ((0, m0), (1, m1))]
            kws = [k_ref[krows, :].astype(_BF) for _, krows, _ in idx]
            vws = [v_ref[krows, :].astype(_BF) for _, krows, _ in idx]
            qm = [(q_ref[idx[u][0], :] * m).astype(_BF) for u, h, m in chains]
            dom = [(do_ref[idx[u][0], :] * m).astype(_BF) for u, h, m in chains]
            s = [_mm_nt(qm_, kws[u]) for qm_, (u, h, m) in zip(qm, chains)]
            dpr = [_mm_nt(dom_, vws[u]) for dom_, (u, h, m) in zip(dom, chains)]
            s = [s_ * scale + _na_bias(b_ref, h, idx[u][2]) for s_, (u, h, m) in zip(s, chains)]
            e = [jnp.exp(s_ - jnp.max(s_, axis=-1, keepdims=True)) for s_ in s]
            pr = [e_ / jnp.sum(e_, axis=-1, keepdims=True) for e_ in e]
            ds = [pr_ * (dpr_ - jnp.sum(pr_ * dpr_, axis=-1, keepdims=True)) for pr_, dpr_ in zip(pr, dpr)]
            dsb = [(ds_ * scale).astype(_BF) for ds_ in ds]
            dq = [_mm(dsb_, kws[u]) * m for dsb_, (u, h, m) in zip(dsb, chains)]
            dk = [_mm_tn(dsb_, qm_) for dsb_, qm_ in zip(dsb, qm)]
            dv = [_mm_tn(pr_, dom_) for pr_, dom_ in zip(pr, dom)]
            for ds_, (u, h, m) in zip(ds, chains):
                for c in range(NA_CHUNKS):
                    db_ref[0, h, idx[u][2], c] += ds_[:, c * LANE:(c + 1) * LANE]
            for u in range(NA_UNROLL):
                qrows, krows, _ = idx[u]
                dq_ref[qrows, :] = (dq[2 * u] + dq[2 * u + 1]).astype(dq_ref.dtype)
                dka_ref[krows, :] += dk[2 * u] + dk[2 * u + 1]
                dva_ref[krows, :] += dv[2 * u] + dv[2 * u + 1]
            return carry

        lax.fori_loop(0, npairs // NA_UNROLL, steps, 0)
        stage_ref[1] = dka_ref[...].astype(stage_ref.dtype)
        stage_ref[2] = dva_ref[...].astype(stage_ref.dtype)
        j = pl.program_id(0)
        _store_strips(stage_ref, dp_out_ref, (j + 18, j + 20, j + 22), sems)

    tab = pl.BlockSpec((1, 2, NA_TYPES, NA_CHUNKS, NA_Q, LANE), lambda j: (j, 0, 0, 0, 0, 0))
    return pl.pallas_call(
        body, name="na_bwd", grid=(NA_WIDTH // LANE,),
        in_specs=[_strip(T, lambda j: j + 18), _strip(T, lambda j: j + 20), _strip(T, lambda j: j + 22),
                  _strip(T, lambda j: j + 6), tab, pl.BlockSpec(memory_space=pl.ANY)],
        out_specs=[pl.BlockSpec(memory_space=pl.ANY), tab],
        out_shape=[jax.ShapeDtypeStruct(dp.shape, dp.dtype),
                   jax.ShapeDtypeStruct((2, 2, NA_TYPES, NA_CHUNKS, NA_Q, LANE), F32)],
        scratch_shapes=[pltpu.VMEM((T, LANE), F32), pltpu.VMEM((T, LANE), F32),
                        pltpu.VMEM((3, T, LANE), _BF), pltpu.SemaphoreType.DMA((3,))],
        input_output_aliases={5: 0},
        compiler_params=_cparams(("arbitrary",)),
    )(proj, proj, proj, dycat, btab, dp)


W_BLK = IN_WIDTH // N_DEV
MXU_W = 256
N_BLK = 3 * MXU_W
N_STEPS = IN_WIDTH // N_BLK
TM = 512


def _ln_fwd(z, g, b):
    zc = z - jnp.mean(z, axis=-1, keepdims=True)
    var = jnp.mean(zc * zc, axis=-1, keepdims=True)
    return zc * lax.rsqrt(var + LN_EPS) * g + b


def _ln_bwd(dy, z, g):
    zc = z - jnp.mean(z, axis=-1, keepdims=True)
    rstd = lax.rsqrt(jnp.mean(zc * zc, axis=-1, keepdims=True) + LN_EPS)
    xhat = zc * rstd
    dxh = dy * g
    dz = rstd * (dxh - jnp.mean(dxh, axis=-1, keepdims=True) - xhat * jnp.mean(dxh * xhat, axis=-1, keepdims=True))
    return dz, dy * xhat


def _row_tile(T):
    return 1024 if T % 1024 == 0 else TM


def _halves(n):
    return (pl.ds(0, n // 2), pl.ds(n // 2, n // 2))


def _inproj_call(xb, w, after):
    T = xb.shape[0]
    tm = _row_tile(T)

    def body(x_ref, w_ref, after_ref, o_ref):
        o_ref[...] = _mm(x_ref[...], w_ref[...])

    return pl.pallas_call(
        body, name="inproj", grid=(T // tm, N_STEPS),
        in_specs=[pl.BlockSpec((tm, D_MODEL), lambda i, n: (i, 0)),
                  pl.BlockSpec((D_MODEL, N_BLK), lambda i, n: (0, n)),
                  pl.BlockSpec(memory_space=pl.ANY)],
        out_specs=pl.BlockSpec((tm, N_BLK), lambda i, n: (i, n)),
        out_shape=jax.ShapeDtypeStruct((T, IN_WIDTH), F32),
        compiler_params=_cparams(("parallel", "arbitrary")),
    )(xb, w, after)


def _vec_spec():
    return pl.BlockSpec((1, D_MODEL), lambda *_: (0, 0))


def _outproj_ln_call(y_lru, y_ret, y_na, x, w, g, b, after):
    T = x.shape[0]

    def body(yl_ref, yr_ref, yn_ref, x_ref, w_ref, g_ref, b_ref, after_ref, z_ref, x1_ref, x1b_ref, yc_ref):
        yc_ref[:, 0:LRU_WIDTH] = yl_ref[...].astype(yc_ref.dtype)
        yc_ref[:, LRU_WIDTH:LRU_WIDTH + RET_WIDTH] = yr_ref[...].astype(yc_ref.dtype)
        yc_ref[:, LRU_WIDTH + RET_WIDTH:] = yn_ref[...].astype(yc_ref.dtype)
        z = ALPHA * x_ref[...] + _mm(yc_ref[...], w_ref[...])
        z_ref[...] = z
        x1 = _ln_fwd(z, g_ref[...], b_ref[...])
        x1_ref[...] = x1
        x1b_ref[...] = x1.astype(x1b_ref.dtype)

    row = lambda w_: pl.BlockSpec((TM, w_), lambda i: (i, 0))
    return pl.pallas_call(
        body, name="outproj_ln", grid=(T // TM,),
        in_specs=[row(LRU_WIDTH), row(RET_WIDTH), row(NA_WIDTH), row(D_MODEL),
                  pl.BlockSpec((D_MODEL, D_MODEL), lambda i: (0, 0)), _vec_spec(), _vec_spec(),
                  pl.BlockSpec(memory_space=pl.ANY)],
        out_specs=[row(D_MODEL)] * 4,
        out_shape=[jax.ShapeDtypeStruct((T, D_MODEL), F32), jax.ShapeDtypeStruct((T, D_MODEL), F32),
                   jax.ShapeDtypeStruct((T, D_MODEL), _BF), jax.ShapeDtypeStruct((T, D_MODEL), _BF)],
        compiler_params=_cparams(("parallel",)),
    )(y_lru, y_ret, y_na, x, w, g, b, after)


def _ffn_ln_call(x1, x1b, wg, wu, wd, g, b):
    T = x1.shape[0]

    def body(x_ref, xb_ref, wg_ref, wu_ref, wd_ref, g_ref, b_ref, z_ref, x2_ref, x2b_ref, gp_ref, up_ref, acc_ref):
        n = pl.program_id(1)

        @pl.when(n == 0)
        def _():
            acc_ref[...] = jnp.zeros_like(acc_ref)

        r0, r1 = _halves(TM)

        def pre(rows):
            xb = xb_ref[rows, :]
            return _mm(xb, wg_ref[...]), _mm(xb, wu_ref[...])

        def act(rows, gp, up):
            gp_ref[rows, :] = gp.astype(gp_ref.dtype)
            up_ref[rows, :] = up.astype(up_ref.dtype)
            return (gp * _sigmoid(gp) * up).astype(_BF)

        gp0, up0 = pre(r0)
        hid0 = act(r0, gp0, up0)
        gp1, up1 = pre(r1)
        acc_ref[r0, :] += _mm(hid0, wd_ref[...])
        hid1 = act(r1, gp1, up1)
        acc_ref[r1, :] += _mm(hid1, wd_ref[...])

        @pl.when(n == N_STEPS - 1)
        def _():
            z = ALPHA * x_ref[...] + acc_ref[...]
            z_ref[...] = z
            x2 = _ln_fwd(z, g_ref[...], b_ref[...])
            x2_ref[...] = x2
            x2b_ref[...] = x2.astype(x2b_ref.dtype)

    row = pl.BlockSpec((TM, D_MODEL), lambda i, n: (i, 0))
    return pl.pallas_call(
        body, name="ffn_ln", grid=(T // TM, N_STEPS),
        in_specs=[row, row,
                  pl.BlockSpec((D_MODEL, N_BLK), lambda i, n: (0, n)),
                  pl.BlockSpec((D_MODEL, N_BLK), lambda i, n: (0, n)),
                  pl.BlockSpec((N_BLK, D_MODEL), lambda i, n: (n, 0)), _vec_spec(), _vec_spec()],
        out_specs=[row] * 3 + [pl.BlockSpec((TM, N_BLK), lambda i, n: (i, n))] * 2,
        out_shape=[jax.ShapeDtypeStruct((T, D_MODEL), F32), jax.ShapeDtypeStruct((T, D_MODEL), F32),
                   jax.ShapeDtypeStruct((T, D_MODEL), _BF),
                   jax.ShapeDtypeStruct((T, IN_WIDTH), _BF), jax.ShapeDtypeStruct((T, IN_WIDTH), _BF)],
        scratch_shapes=[pltpu.VMEM((TM, D_MODEL), F32)],
        compiler_params=_cparams(("parallel", "arbitrary")),
    )(x1, x1b, wg, wu, wd, g, b)


def _loss_call(y, t):
    T = y.shape[0]

    def body(y_ref, t_ref, dy_ref, loss_ref):
        @pl.when(pl.program_id(0) == 0)
        def _():
            loss_ref[...] = jnp.zeros_like(loss_ref)

        err = y_ref[...] - t_ref[...]
        dy_ref[...] = err * (1.0 / D_MODEL)
        part = 0.5 * jnp.sum(jnp.mean(err * err, axis=-1, keepdims=True), axis=0, keepdims=True)
        loss_ref[...] += jnp.broadcast_to(part, loss_ref.shape)

    row = pl.BlockSpec((TM, D_MODEL), lambda i: (i, 0))
    return pl.pallas_call(
        body, name="loss", grid=(T // TM,),
        in_specs=[row, row],
        out_specs=[row, pl.BlockSpec((SUB, LANE), lambda i: (0, 0))],
        out_shape=[jax.ShapeDtypeStruct((T, D_MODEL), F32), jax.ShapeDtypeStruct((SUB, LANE), F32)],
        compiler_params=_cparams(("arbitrary",)),
    )(y, t)


def _ffn_bwd_call(dx2, z2, gpb, upb, wg, wu, wd, g, after):
    T = dx2.shape[0]

    def body(dx2_ref, z_ref, gp_ref, up_ref, wg_ref, wu_ref, wd_ref, g_ref, after_ref,
             dx1_ref, dgp_ref, dup_ref, hid_ref, dzb_ref, dln_ref, acc_ref):
        i, n = pl.program_id(0), pl.program_id(1)

        @pl.when((i == 0) & (n == 0))
        def _():
            dln_ref[...] = jnp.zeros_like(dln_ref)

        @pl.when(n == 0)
        def _():
            dy = dx2_ref[...]
            dz, dg_rows = _ln_bwd(dy, z_ref[...], g_ref[...])
            dzb_ref[...] = dz.astype(dzb_ref.dtype)
            acc_ref[...] = ALPHA * dz
            dln_ref[0:1, :] += jnp.sum(dg_rows, axis=0, keepdims=True)
            dln_ref[1:2, :] += jnp.sum(dy, axis=0, keepdims=True)

        r0, r1 = _halves(TM)

        def grads(rows, dhid):
            gp = gp_ref[rows, :].astype(F32)
            up = up_ref[rows, :].astype(F32)
            sg = _sigmoid(gp)
            act = gp * sg
            hid_ref[rows, :] = (act * up).astype(hid_ref.dtype)
            dup = (dhid * act).astype(_BF)
            dgp = (dhid * up * (sg * (1.0 + gp * (1.0 - sg)))).astype(_BF)
            dgp_ref[rows, :] = dgp.astype(dgp_ref.dtype)
            dup_ref[rows, :] = dup.astype(dup_ref.dtype)
            return dgp, dup

        dhid0 = _mm_nt(dzb_ref[r0, :], wd_ref[...])
        dhid1 = _mm_nt(dzb_ref[r1, :], wd_ref[...])
        dgp0, dup0 = grads(r0, dhid0)
        acc_ref[r0, :] += _mm_nt(dgp0, wg_ref[...]) + _mm_nt(dup0, wu_ref[...])
        dgp1, dup1 = grads(r1, dhid1)
        acc_ref[r1, :] += _mm_nt(dgp1, wg_ref[...]) + _mm_nt(dup1, wu_ref[...])

        @pl.when(n == N_STEPS - 1)
        def _():
            dx1_ref[...] = acc_ref[...]

    row = pl.BlockSpec((TM, D_MODEL), lambda i, n: (i, 0))
    blk = pl.BlockSpec((TM, N_BLK), lambda i, n: (i, n))
    return pl.pallas_call(
        body, name="ffn_bwd", grid=(T // TM, N_STEPS),
        in_specs=[row, row, blk, blk,
                  pl.BlockSpec((D_MODEL, N_BLK), lambda i, n: (0, n)),
                  pl.BlockSpec((D_MODEL, N_BLK), lambda i, n: (0, n)),
                  pl.BlockSpec((N_BLK, D_MODEL), lambda i, n: (n, 0)), _vec_spec(),
                  pl.BlockSpec(memory_space=pl.ANY)],
        out_specs=[row, blk, blk, blk, row, pl.BlockSpec((SUB, D_MODEL), lambda i, n: (0, 0))],
        out_shape=[jax.ShapeDtypeStruct((T, D_MODEL), F32),
                   jax.ShapeDtypeStruct((T, IN_WIDTH), _BF), jax.ShapeDtypeStruct((T, IN_WIDTH), _BF),
                   jax.ShapeDtypeStruct((T, IN_WIDTH), _BF), jax.ShapeDtypeStruct((T, D_MODEL), _BF),
                   jax.ShapeDtypeStruct((SUB, D_MODEL), F32)],
        scratch_shapes=[pltpu.VMEM((TM, D_MODEL), F32)],
        compiler_params=_cparams(("arbitrary", "arbitrary")),
    )(dx2, z2, gpb, upb, wg, wu, wd, g, after)


def _outproj_bwd_call(dx1, z1, w, g):
    T = dx1.shape[0]

    def body(dx_ref, z_ref, w_ref, g_ref, dzb_ref, dyc_ref, dres_ref, dln_ref):
        @pl.when(pl.program_id(0) == 0)
        def _():
            dln_ref[...] = jnp.zeros_like(dln_ref)

        dy = dx_ref[...]
        dz, dg_rows = _ln_bwd(dy, z_ref[...], g_ref[...])
        dzb_ref[...] = dz.astype(dzb_ref.dtype)
        dres_ref[...] = ALPHA * dz
        dyc_ref[...] = _mm_nt(dz, w_ref[...])
        dln_ref[0:1, :] += jnp.sum(dg_rows, axis=0, keepdims=True)
        dln_ref[1:2, :] += jnp.sum(dy, axis=0, keepdims=True)

    row = pl.BlockSpec((TM, D_MODEL), lambda i: (i, 0))
    return pl.pallas_call(
        body, name="outproj_bwd", grid=(T // TM,),
        in_specs=[row, row, pl.BlockSpec((D_MODEL, D_MODEL), lambda i: (0, 0)), _vec_spec()],
        out_specs=[row, row, row, pl.BlockSpec((SUB, D_MODEL), lambda i: (0, 0))],
        out_shape=[jax.ShapeDtypeStruct((T, D_MODEL), _BF), jax.ShapeDtypeStruct((T, D_MODEL), F32),
                   jax.ShapeDtypeStruct((T, D_MODEL), F32), jax.ShapeDtypeStruct((SUB, D_MODEL), F32)],
        compiler_params=_cparams(("arbitrary",)),
    )(dx1, z1, w, g)


def _inproj_bwd_call(dres, dp, w):
    T = dres.shape[0]

    def body(dres_ref, dp_ref, w_ref, dx_ref):
        dx_ref[...] = dres_ref[...] + _mm_nt(dp_ref[...], w_ref[...])

    row = pl.BlockSpec((TM, D_MODEL), lambda i: (i, 0))
    return pl.pallas_call(
        body, name="inproj_bwd", grid=(T // TM,),
        in_specs=[row, pl.BlockSpec((TM, IN_WIDTH), lambda i: (i, 0)),
                  pl.BlockSpec((D_MODEL, IN_WIDTH), lambda i: (0, 0), pipeline_mode=pl.Buffered(1))],
        out_specs=row,
        out_shape=jax.ShapeDtypeStruct((T, D_MODEL), F32),
        compiler_params=_cparams(("parallel",)),
    )(dres, dp, w)


def _tn_cols_call(a, b, name):
    T, ka = a.shape
    n = b.shape[1]

    def body(a_ref, b_ref, o_ref):
        o_ref[...] = _mm_tn(a_ref[...], b_ref[...]).astype(o_ref.dtype)

    return pl.pallas_call(
        body, name=name, grid=(n // N_BLK,),
        in_specs=[pl.BlockSpec((T, ka), lambda j: (0, 0), pipeline_mode=pl.Buffered(1)),
                  pl.BlockSpec((T, N_BLK), lambda j: (0, j))],
        out_specs=pl.BlockSpec((ka, N_BLK), lambda j: (0, j)),
        out_shape=jax.ShapeDtypeStruct((ka, n), _BF),
        compiler_params=_cparams(("parallel",)),
    )(a, b)


def _tn_rows_call(a, b, kb, name):
    T, ka = a.shape
    n = b.shape[1]

    def body(a_ref, b_ref, o_ref):
        o_ref[...] = _mm_tn(a_ref[...], b_ref[...]).astype(o_ref.dtype)

    return pl.pallas_call(
        body, name=name, grid=(ka // kb,),
        in_specs=[pl.BlockSpec((T, kb), lambda r: (0, r)),
                  pl.BlockSpec((T, n), lambda r: (0, 0), pipeline_mode=pl.Buffered(1))],
        out_specs=pl.BlockSpec((kb, n), lambda r: (r, 0)),
        out_shape=jax.ShapeDtypeStruct((ka, n), _BF),
        compiler_params=_cparams(("parallel",)),
    )(a, b)


def _me():
    return lax.axis_index("x"), lax.axis_index("y"), lax.axis_index("c")


def _flip(k):
    x, y, c = _me()
    return (1 - x if k & 4 else x, 1 - y if k & 2 else y, 1 - c if k & 1 else c)


def _dev_index(pos):
    return 4 * pos[0] + 2 * pos[1] + pos[2]


_HBM = pl.BlockSpec(memory_space=pltpu.HBM)
_SEM = pl.BlockSpec(memory_space=pltpu.SEMAPHORE)


def _land_shape(shape, mode):
    if mode == "all":
        return (N_DEV,) + shape
    if mode == "cols":
        return (shape[0], N_DEV * shape[1])
    if mode == "blk":
        return shape
    assert mode == "scols"
    return (N_DEV, shape[0], shape[1] // N_DEV)


def _comm_copies(ins, lands, modes, send_sems, recv_sems):
    me = _dev_index(_me())
    copies = []
    for k in range(N_DEV):
        peer = _flip(k)
        pidx = _dev_index(peer)
        for a, (src, land, mode) in enumerate(zip(ins, lands, modes)):
            if mode == "blk":
                src = src.at[pidx]
            elif mode == "scols":
                w = src.shape[1] // N_DEV
                src = src.at[:, pl.ds(pl.multiple_of(pidx * w, LANE), w)]
            if mode == "cols":
                w = src.shape[1]
                dst = land.at[:, pl.ds(pl.multiple_of(me * w, LANE), w)]
            else:
                dst = land.at[me]
            copies.append(pltpu.make_async_remote_copy(
                src_ref=src, dst_ref=dst, send_sem=send_sems.at[k * len(ins) + a], recv_sem=recv_sems.at[k * len(ins) + a],
                device_id=peer, device_id_type=MESH))
    return copies


def _comm_start_call(arrs, gather_flags, after, name):
    n = len(arrs)
    lands = [lax.empty(_land_shape(v.shape, mode), v.dtype) for v, mode in zip(arrs, gather_flags)]

    def body(*refs):
        ins, lnd = refs[:n], refs[n:2 * n]
        send_sems, recv_sems = refs[2 * n + len(after)], refs[2 * n + len(after) + 1]
        for cp in _comm_copies(ins, lnd, gather_flags, send_sems, recv_sems):
            cp.start()
        refs[-1][...] = jnp.zeros_like(refs[-1])

    hbm = [pltpu.with_memory_space_constraint(v, pltpu.HBM) for v in list(arrs) + lands]
    out = pl.pallas_call(
        body, name=name,
        out_shape=(pltpu.SemaphoreType.DMA((N_DEV * n,)), pltpu.SemaphoreType.DMA((N_DEV * n,)),
                   *[pltpu.HBM(v.shape, v.dtype) for v in hbm], jax.ShapeDtypeStruct((SUB, LANE), F32)),
        in_specs=[_HBM] * (2 * n) + [pl.BlockSpec(memory_space=pl.ANY)] * len(after),
        out_specs=(_SEM, _SEM, *[_HBM] * (2 * n), pl.BlockSpec(memory_space=pltpu.VMEM)),
        input_output_aliases={i: 2 + i for i in range(2 * n)},
        compiler_params=pltpu.CompilerParams(has_side_effects=pltpu.SideEffectType.DATAFLOW_SIDE_EFFECTING),
    )(*hbm, *after)
    return out[:-1], out[-1]


def _comm_wait_call(state, gather_flags, after, name):
    n = len(gather_flags)
    send_sems, recv_sems, thru = state[0], state[1], state[2:]

    def body(*refs):
        ins, lnd, ssem, rsem = refs[:n], refs[n:2 * n], refs[2 * n], refs[2 * n + 1]
        for cp in _comm_copies(ins, lnd, gather_flags, ssem, rsem):
            cp.wait_send()
            cp.wait_recv()

    out = pl.pallas_call(
        body, name=name,
        out_shape=tuple(pltpu.HBM(v.shape, v.dtype) for v in thru),
        in_specs=[_HBM] * (2 * n) + [_SEM, _SEM] + [pl.BlockSpec(memory_space=pl.ANY)] * len(after),
        out_specs=tuple([_HBM] * (2 * n)),
        input_output_aliases={i: i for i in range(2 * n)},
        compiler_params=pltpu.CompilerParams(has_side_effects=pltpu.SideEffectType.DATAFLOW_SIDE_EFFECTING),
    )(*thru, send_sems, recv_sems, *after)
    return out[n:]


def _sum8_call(recv, rows, name):
    _, r, c = recv.shape

    def body(x_ref, o_ref):
        acc = x_ref[0].astype(F32)
        for s in range(1, N_DEV):
            acc = acc + x_ref[s].astype(F32)
        o_ref[...] = acc

    return pl.pallas_call(
        body, name=name, grid=(r // rows,),
        in_specs=[pl.BlockSpec((N_DEV, rows, c), lambda i: (0, i, 0))],
        out_specs=pl.BlockSpec((rows, c), lambda i: (i, 0)),
        out_shape=jax.ShapeDtypeStruct((r, c), F32),
        compiler_params=_cparams(("parallel",)),
    )(recv)


def _adamw_call(w, g, m, v, rows, name):
    r, c = w.shape

    def body(w_ref, g_ref, m_ref, v_ref, d_ref, nm_ref, nv_ref):
        gr = g_ref[...]
        nm = ADAM_B1 * m_ref[...] + (1.0 - ADAM_B1) * gr
        nv = ADAM_B2 * v_ref[...] + (1.0 - ADAM_B2) * (gr * gr)
        m_hat = nm / (1.0 - ADAM_B1 ** ADAM_STEP)
        v_hat = nv / (1.0 - ADAM_B2 ** ADAM_STEP)
        d_ref[...] = -ADAM_LR * (m_hat / (jnp.sqrt(v_hat) + ADAM_EPS) + ADAM_WD * w_ref[...])
        nm_ref[...] = nm
        nv_ref[...] = nv

    spec = pl.BlockSpec((rows, c), lambda i: (i, 0))
    return pl.pallas_call(
        body, name=name, grid=(r // rows,),
        in_specs=[spec] * 4, out_specs=[spec] * 3,
        out_shape=[jax.ShapeDtypeStruct((r, c), F32)] * 3,
        compiler_params=_cparams(("parallel",)),
    )(w, g, m, v)


SH_ROWS = 16
SH_W = LRU_WIDTH // N_DEV
REP_ROWS = 824
_REP_SIZES = (LRU_WIDTH, 2 * 6 * 64 * 64, 2 * 6 * 64 * 64, RET_WIDTH, 1920, D_MODEL, D_MODEL, D_MODEL, D_MODEL)
_RPB_SIZE = NA_HEADS * (2 * NA_KH - 1) * (2 * NA_KW - 1)


def _pack_sh(cw, ba, bx, lam):
    return jnp.concatenate([cw, ba, bx, lam], axis=0)


def _pad_sh(p):
    pad = [(0, 0)] * (p.ndim - 2) + [(0, SH_ROWS - p.shape[-2]), (0, LANE - p.shape[-1])]
    return jnp.pad(p, pad)


def _pack_rep(cb, wa, wx, gnw, rpb, l1g, l1b, l2g, l2b):
    flat = jnp.concatenate([cb.reshape(-1), wa.reshape(-1), wx.reshape(-1), gnw.reshape(-1),
                            jnp.pad(rpb.reshape(-1), (0, 1920 - _RPB_SIZE)), l1g, l1b, l2g, l2b,
                            jnp.zeros((REP_ROWS * LANE - sum(_REP_SIZES),), F32)])
    return flat.reshape(REP_ROWS, LANE)


def _unpack_rep(p):
    nl = p.shape[0]
    flat = p.reshape(nl, -1)
    out, off = [], 0
    for size in _REP_SIZES:
        out.append(flat[:, off:off + size])
        off += size
    cb, wa, wx, gnw, rpb, l1g, l1b, l2g, l2b = out
    return (cb, wa.reshape(nl, 2, 6, 64, 64), wx.reshape(nl, 2, 6, 64, 64), gnw,
            rpb[:, :_RPB_SIZE].reshape(nl, NA_HEADS, 2 * NA_KH - 1, 2 * NA_KW - 1), l1g, l1b, l2g, l2b)


def _adamw_nd(w, g, m, v, rows, name):
    shp = w.shape
    f = lambda t: t.reshape(-1, shp[-1])
    rows = f(w).shape[0] if rows is None else rows
    return [t.reshape(shp) for t in _adamw_call(f(w), f(g), f(m), f(v), rows, name)]


def kernel(x, w_in, conv_w, conv_b, lru_w_a, lru_b_a, lru_w_x, lru_b_x, lru_lam, ret_gn_w, na_rpb, w_out, ln1_g, ln1_b, w_gate, w_up, w_down, ln2_g, ln2_b, loss_target, m_w_in, m_conv_w, m_conv_b, m_lru_w_a, m_lru_b_a, m_lru_w_x, m_lru_b_x, m_lru_lam, m_ret_gn_w, m_na_rpb, m_w_out, m_ln1_g, m_ln1_b, m_w_gate, m_w_up, m_w_down, m_ln2_g, m_ln2_b, v_w_in, v_conv_w, v_conv_b, v_lru_w_a, v_lru_b_a, v_lru_w_x, v_lru_b_x, v_lru_lam, v_ret_gn_w, v_na_rpb, v_w_out, v_ln1_g, v_ln1_b, v_w_gate, v_w_up, v_w_down, v_ln2_g, v_ln2_b):
    nl = w_in.shape[0]
    T = x.shape[1]
    rows_n = T // GRID_W
    x0, target = x[0], loss_target[0]
    ffpad = W_BLK - FF_BLK

    win_b = w_in.astype(_BF)
    wg_b = jnp.pad(w_gate, ((0, 0), (0, 0), (0, ffpad))).astype(_BF)
    wu_b = jnp.pad(w_up, ((0, 0), (0, 0), (0, ffpad))).astype(_BF)
    wd_b = jnp.pad(w_down, ((0, 0), (0, ffpad), (0, 0))).astype(_BF)
    wout_b = w_out.astype(_BF)
    def agf_start(l, after):
        sh = _pad_sh(_pack_sh(conv_w[l], lru_b_a[l], lru_b_x[l], lru_lam[l]))
        arrs, modes = [win_b[l], sh], ["cols", "all"]
        if l > 0:
            arrs, modes = arrs + [wd_b[l]], modes + ["all"]
        return _comm_start_call(arrs, modes, after, f"agf_start{l}"), modes

    def agk_start(l, after):
        arrs, modes = [wg_b[l], wu_b[l], wout_b[l]], ["cols", "cols", "all"]
        if l == 0:
            arrs, modes = arrs + [wd_b[l]], modes + ["all"]
        return _comm_start_call(arrs, modes, after, f"agk_start{l}"), modes

    tables = _ret_tables(T)
    w4_all = _lru_w4(lru_w_a, lru_w_x)
    layers = []
    gathered = []
    xs, xb = x0, x0.astype(_BF)
    (agf_state, token), agf_modes = agf_start(0, [])
    tie = 0.0 * token[0, 0]
    btabs = [_na_bias_tables(na_rpb[l] + tie, rows_n) for l in range(nl)]
    for l in range(nl):
        front = _comm_wait_call(agf_state, agf_modes, [xb] + (btabs if l == 0 else []), f"agf_wait{l}")
        win, shg = front[0], front[1]
        (agk_state, token), agk_modes = agk_start(l, [shg])
        full = shg[:, :10, :SH_W].transpose(1, 0, 2).reshape(10, LRU_WIDTH)
        vec, w4 = _lru_vec(full[0:4], conv_b[l], full[4:6], full[6:8], full[8:10]), w4_all[l]
        gnw8 = jnp.pad(ret_gn_w[l][None], ((0, SUB - 1), (0, 0)))
        btab = btabs[l]
        proj = _inproj_call(xb, win, token)
        y_lru = _lru_fwd_call(proj, vec, w4)
        y_ret = _ret_fwd_call(proj, tables, gnw8)
        y_na = _na_fwd_call(proj, btab)
        back = _comm_wait_call(agk_state, agk_modes, [y_na], f"agk_wait{l}")
        wg, wu, wout = back[0], back[1], back[2]
        wd = (back[3] if l == 0 else front[2]).reshape(IN_WIDTH, D_MODEL)
        wout = wout.reshape(D_MODEL, D_MODEL)
        gathered.append((win, wg, wu, wd, wout))
        if l + 1 < nl:
            (agf_state, token), agf_modes = agf_start(l + 1, [wout])
        z1, x1, x1b, ycb = _outproj_ln_call(y_lru, y_ret, y_na, xs, wout, ln1_g[l][None], ln1_b[l][None], token)
        z2, x2, x2b, gpb, upb = _ffn_ln_call(x1, x1b, wg, wu, wd, ln2_g[l][None], ln2_b[l][None])
        layers.append(dict(xb=xb, proj=proj, vec=vec, w4=w4, gnw8=gnw8, btab=btab,
                           z1=z1, x1b=x1b, ycb=ycb, z2=z2, gpb=gpb, upb=upb))
        xs, xb = x2, x2b

    dx, loss_blk = _loss_call(xs, target)
    loss = lax.psum(loss_blk[0, 0], ("x", "y", "c"))

    gxa_flags = ["scols", "scols", "blk", "blk"]
    gxb_flags = ["scols", "blk", "all"]
    gxa_state, gxb_state = [None] * nl, [None] * nl
    token = loss_blk
    for l in reversed(range(nl)):
        s = layers[l]
        win, wg, wu, wd, wout = gathered[l]
        dx1, dgp, dup, hid, dz2b, dln2 = _ffn_bwd_call(dx, s["z2"], s["gpb"], s["upb"], wg, wu, wd, ln2_g[l][None], token)
        dwg = _tn_cols_call(s["x1b"], dgp, "tn_cols")
        dwu = _tn_cols_call(s["x1b"], dup, "tn_cols")
        dwd = _tn_rows_call(hid, dz2b, N_BLK, "tn_rows_down").reshape(N_DEV, W_BLK, D_MODEL)
        dz1b, dyc, dres, dln1 = _outproj_bwd_call(dx1, s["z1"], wout, ln1_g[l][None])
        dwout = _tn_rows_call(s["ycb"], dz1b, D_MODEL // 2, "tn_rows_out").reshape(N_DEV, LANE, D_MODEL)
        gxa_state[l], token = _comm_start_call([dwg, dwu, dwd, dwout], gxa_flags, [], f"gxa_start{l}")
        dp, dvec, dw4 = _lru_bwd_call(s["proj"], dyc, s["vec"], s["w4"], token)
        dp, dgnw = _ret_bwd_call(s["proj"], dyc, tables, s["gnw8"], dp)
        dp, dbias = _na_bwd_call(s["proj"], dyc, s["btab"], dp)
        dwin = _tn_cols_call(s["xb"], dp, "tn_cols")
        dx = _inproj_bwd_call(dres, dp, win)
        dcw, dcb, dwa, dba, dwx, dbx, dlam = _lru_unpack(dvec, dw4)
        rep = _pack_rep(dcb, dwa, dwx, dgnw[0], _na_bias_grad(dbias, rows_n), dln1[0], dln1[1], dln2[0], dln2[1])
        sh = _pack_sh(dcw, dba, dbx, dlam).reshape(10, N_DEV, SH_W).transpose(1, 0, 2)
        gxb_state[l], token = _comm_start_call([dwin, _pad_sh(sh), rep], gxb_flags, [], f"gxb_start{l}")

    g_big = [[None] * nl for _ in range(5)]
    g_sh = [None] * nl
    g_rep = [None] * nl
    after = [dx, token]
    big = {}
    for l in reversed(range(nl)):
        ra = _comm_wait_call(gxa_state[l], gxa_flags, after, f"gxa_wait{l}")
        g_big[1][l] = _sum8_call(ra[0], TM, "sum8_cols")[:, :FF_BLK]
        g_big[2][l] = _sum8_call(ra[1], TM, "sum8_cols")[:, :FF_BLK]
        g_big[3][l] = _sum8_call(ra[2], W_BLK, "sum8_down")[:FF_BLK]
        g_big[4][l] = _sum8_call(ra[3], LANE, "sum8_out")
        after = [g_big[4][l]]
        if l == 0:
            g_w_gate, g_w_up, g_w_down, g_w_out = [jnp.stack(t) for t in g_big[1:]]
            big["w_gate"] = _adamw_nd(w_gate, g_w_gate, m_w_gate, v_w_gate, TM, "adamw_ff")
            big["w_up"] = _adamw_nd(w_up, g_w_up, m_w_up, v_w_up, TM, "adamw_ff")
            big["w_down"] = _adamw_nd(w_down, g_w_down, m_w_down, v_w_down, FF_BLK, "adamw_down")
            big["w_out"] = _adamw_nd(w_out, g_w_out, m_w_out, v_w_out, LANE, "adamw_out")
            after = [big[n][k] for n in ("w_gate", "w_up", "w_down", "w_out") for k in range(3)]
        rb = _comm_wait_call(gxb_state[l], gxb_flags, after, f"gxb_wait{l}")
        g_big[0][l] = _sum8_call(rb[0], TM, "sum8_cols")
        g_sh[l] = _sum8_call(rb[1], SH_ROWS, "sum8_sh")
        g_rep[l] = _sum8_call(rb[2], REP_ROWS, "sum8_rep")
        after = [g_rep[l]]

    g_w_in = jnp.stack(g_big[0])
    big["w_in"] = _adamw_nd(w_in, g_w_in, m_w_in, v_w_in, TM, "adamw_in")
    g_shp = jnp.stack(g_sh)[:, :, :SH_W]
    rep_names = ("conv_b", "lru_w_a", "lru_w_x", "ret_gn_w", "na_rpb", "ln1_g", "ln1_b", "ln2_g", "ln2_b")
    grads = {"w_in": g_w_in, "w_gate": g_w_gate, "w_up": g_w_up, "w_down": g_w_down, "w_out": g_w_out,
             "conv_w": g_shp[:, 0:4], "lru_b_a": g_shp[:, 4:6], "lru_b_x": g_shp[:, 6:8], "lru_lam": g_shp[:, 8:10]}
    grads.update(dict(zip(rep_names, _unpack_rep(jnp.stack(g_rep)))))
    small = {
        "conv_w": (conv_w, m_conv_w, v_conv_w), "conv_b": (conv_b, m_conv_b, v_conv_b),
        "lru_w_a": (lru_w_a, m_lru_w_a, v_lru_w_a), "lru_b_a": (lru_b_a, m_lru_b_a, v_lru_b_a),
        "lru_w_x": (lru_w_x, m_lru_w_x, v_lru_w_x), "lru_b_x": (lru_b_x, m_lru_b_x, v_lru_b_x),
        "lru_lam": (lru_lam, m_lru_lam, v_lru_lam), "ret_gn_w": (ret_gn_w, m_ret_gn_w, v_ret_gn_w),
        "na_rpb": (na_rpb, m_na_rpb, v_na_rpb), "ln1_g": (ln1_g, m_ln1_g, v_ln1_g), "ln1_b": (ln1_b, m_ln1_b, v_ln1_b),
        "ln2_g": (ln2_g, m_ln2_g, v_ln2_g), "ln2_b": (ln2_b, m_ln2_b, v_ln2_b),
    }
    for name, (w_, m_, v_) in small.items():
        big[name] = _adamw_nd(w_, grads[name], m_, v_, None, "adamw_small")
    kinds = [{n: big[n][k] for n in big} for k in range(3)]
    order = ("w_in", "conv_w", "conv_b", "lru_w_a", "lru_b_a", "lru_w_x", "lru_b_x", "lru_lam", "ret_gn_w", "na_rpb",
             "w_out", "ln1_g", "ln1_b", "w_gate", "w_up", "w_down", "ln2_g", "ln2_b")
    outs = [loss, dx[None]]
    for d in (grads, *kinds):
        outs.extend(d[n] for n in order)
    return tuple(outs)
```

```python
import functools
import math

import numpy as np
import jax
import jax.numpy as jnp
from jax import lax
from jax.experimental import pallas as pl
from jax.experimental.pallas import tpu as pltpu

F32 = jnp.float32
_BF = jnp.bfloat16

D_MODEL = 1024
DEPTH = 4
GRID_W = 64
HEAD_DIM = 64
LRU_WIDTH = 384
RET_WIDTH = 384
RET_HEADS = 6
NA_WIDTH = 256
NA_HEADS = 4
IN_WIDTH = 3072
CONV_WIDTH = 4
LRU_C = 8.0
RET_CHUNK = 128
ROPE_BASE = 10000.0
GN_EPS = 1e-6
NA_KH = 8
NA_KW = 16
D_FF = 2816
FF_BLK = 352
N_DEV = 8
ALPHA = (2 * DEPTH) ** 0.25
LN_EPS = 1e-5
ADAM_LR = 0.001
ADAM_B1 = 0.9
ADAM_B2 = 0.999
ADAM_EPS = 1e-08
ADAM_WD = 0.01
ADAM_STEP = 10

LANE = 128
SUB = 8
VMEM_MB = 56
NEG = -1e30

MESH = pl.DeviceIdType.MESH


def _cparams(sem=None, vmem_mb=VMEM_MB):
    return pltpu.CompilerParams(dimension_semantics=sem, vmem_limit_bytes=vmem_mb << 20)


def _mm(a, b):
    return jnp.dot(a.astype(_BF), b.astype(_BF), preferred_element_type=F32)


def _mm_nt(a, b):
    return lax.dot_general(a.astype(_BF), b.astype(_BF), (((1,), (1,)), ((), ())), preferred_element_type=F32)


def _mm_tn(a, b):
    return lax.dot_general(a.astype(_BF), b.astype(_BF), (((0,), (0,)), ((), ())), preferred_element_type=F32)


def _sigmoid(x):
    return jax.nn.sigmoid(x)


def _rows(start, size):
    return pl.ds(pl.multiple_of(start, SUB), size)


def _loop2(n, body, init):
    assert n % 2 == 0
    return lax.fori_loop(0, n // 2, lambda i, c: body(2 * i + 1, body(2 * i, c)), init)


def _strip(T, col, buffers=2):
    return pl.BlockSpec((T, LANE), lambda j: (0, col(j)), pipeline_mode=pl.Buffered(buffers))


LRU_CH = 256
_GELU_C0 = math.sqrt(2.0 / math.pi)
_GELU_C1 = 0.044715


def _gelu_parts(x):
    x2 = x * x
    t = jnp.tanh(_GELU_C0 * (x + _GELU_C1 * x * x2))
    val = 0.5 * x * (1.0 + t)
    der = 0.5 * (1.0 + t) + 0.5 * x * (1.0 - t * t) * _GELU_C0 * (1.0 + 3.0 * _GELU_C1 * x2)
    return val, der


def _softplus_neg(lam):
    e = jnp.exp(-jnp.abs(lam))
    w = 1.0 + e
    l1p = jnp.where(w == 1.0, e, jnp.log(w) * (e / jnp.where(w == 1.0, 1.0, w - 1.0)))
    return jnp.maximum(-lam, 0.0) + l1p


def _window(ref, t0, ch, T):
    prev = ref[_rows(jnp.maximum(t0 - SUB, 0), SUB), :].astype(F32)
    nxt = ref[_rows(jnp.minimum(t0 + ch, T - SUB), SUB), :].astype(F32)
    prev = jnp.where(t0 > 0, prev, 0.0)
    nxt = jnp.where(t0 + ch < T, nxt, 0.0)
    return jnp.concatenate([prev, ref[_rows(t0, ch), :].astype(F32), nxt], axis=0)


def _tap(win, shift, ch):
    n = win.shape[0]
    return pltpu.roll(win, (-shift) % n, 0)[SUB:SUB + ch]


def _lru_conv(xb_ref, vec, t0, T):
    win = _window(xb_ref, t0, LRU_CH, T)
    xc = jnp.broadcast_to(vec[4:5, :], (LRU_CH, LANE))
    for j in range(CONV_WIDTH):
        xc = xc + _tap(win, j - CONV_WIDTH // 2, LRU_CH) * vec[j:j + 1, :]
    return xc


def _lru_dir(pre_a, pre_x, sp):
    r = _sigmoid(pre_a)
    i = _sigmoid(pre_x)
    log_a = (-LRU_C) * r * sp
    a = jnp.exp(log_a)
    z = jnp.tanh(-log_a) * (a * a + 1.0)
    s = jnp.sqrt(z)
    return r, i, a, s


def _scan_tile(a, b, reverse, row):
    for k in (1, 2, 4):
        if not reverse:
            a_s, b_s, m = pltpu.roll(a, k, 0), pltpu.roll(b, k, 0), row >= k
        else:
            a_s, b_s, m = pltpu.roll(a, SUB - k, 0), pltpu.roll(b, SUB - k, 0), row < SUB - k
        b = jnp.where(m, a * b_s + b, b)
        a = jnp.where(m, a * a_s, a)
    return a, b


def _bcast_row(x, r):
    return jnp.broadcast_to(x[r:r + 1, :], (SUB, LANE))


def _lru_prepare(xb_ref, w4_ref, vec, xc_ref, af_ref, uf_ref, ab_ref, ub_ref, T):
    sp_f = _softplus_neg(vec[9:10, :])
    sp_b = _softplus_neg(vec[10:11, :])
    w4 = w4_ref[0]

    def body(c, carry):
        t0 = c * LRU_CH
        xc = _lru_conv(xb_ref, vec, t0, T)
        if xc_ref is not None:
            xc_ref[_rows(t0, LRU_CH), :] = xc
        pre = _mm(xc, w4)
        _, i, a, s = _lru_dir(pre[:, 0:128] + vec[5:6, :], pre[:, 128:256] + vec[6:7, :], sp_f)
        af_ref[_rows(t0, LRU_CH), :] = a
        uf_ref[_rows(t0, LRU_CH), :] = s * (i * xc)
        _, i, a, s = _lru_dir(pre[:, 256:384] + vec[7:8, :], pre[:, 384:512] + vec[8:9, :], sp_b)
        ab_ref[_rows(t0, LRU_CH), :] = a
        ub_ref[_rows(t0, LRU_CH), :] = s * (i * xc)
        return carry

    lax.fori_loop(0, T // LRU_CH, body, 0)


def _lru_scan(af_ref, uf_ref, ab_ref, ub_ref, T):
    nt = T // SUB
    row = lax.broadcasted_iota(jnp.int32, (SUB, LANE), 0)

    def body(j, carry):
        hf, hb = carry
        sf = _rows(j * SUB, SUB)
        sb = _rows((nt - 1 - j) * SUB, SUB)
        a, b = _scan_tile(af_ref[sf, :], uf_ref[sf, :], False, row)
        h = a * hf + b
        uf_ref[sf, :] = h
        hf = _bcast_row(h, SUB - 1)
        a, b = _scan_tile(ab_ref[sb, :], ub_ref[sb, :], True, row)
        h = a * hb + b
        ub_ref[sb, :] = h
        hb = _bcast_row(h, 0)
        return hf, hb

    z = jnp.zeros((SUB, LANE), F32)
    lax.fori_loop(0, nt, body, (z, z))


def _lru_fwd_call(proj, vec, w4):
    T = proj.shape[0]

    def body(xb_ref, gate_ref, vec_ref, w4_ref, y_ref, af_ref, uf_ref, ab_ref, ub_ref):
        vec = vec_ref[...]
        _lru_prepare(xb_ref, w4_ref, vec, None, af_ref, uf_ref, ab_ref, ub_ref, T)
        _lru_scan(af_ref, uf_ref, ab_ref, ub_ref, T)

        def out(c, carry):
            rows = _rows(c * LRU_CH, LRU_CH)
            gl, _ = _gelu_parts(gate_ref[rows, :])
            y_ref[rows, :] = (uf_ref[rows, :] + ub_ref[rows, :]) * gl
            return carry

        lax.fori_loop(0, T // LRU_CH, out, 0)

    return pl.pallas_call(
        body, name="lru_fwd", grid=(LRU_WIDTH // LANE,),
        in_specs=[_strip(T, lambda j: j), _strip(T, lambda j: j + 3),
                  pl.BlockSpec((16, LANE), lambda j: (0, j)),
                  pl.BlockSpec((1, LANE, 4 * LANE), lambda j: (j, 0, 0))],
        out_specs=_strip(T, lambda j: j, buffers=1),
        out_shape=jax.ShapeDtypeStruct((T, LRU_WIDTH), F32),
        scratch_shapes=[pltpu.VMEM((T, LANE), F32)] * 4,
        compiler_params=_cparams(("arbitrary",)),
    )(proj, proj, vec, w4)


def _store_strips(stage_ref, dp_ref, cols, sems):
    copies = [pltpu.make_async_copy(stage_ref.at[b], dp_ref.at[:, pl.ds(pl.multiple_of(c * LANE, LANE), LANE)], sems.at[b])
              for b, c in enumerate(cols)]
    for cp in copies:
        cp.start()
    for cp in copies:
        cp.wait()


def _lru_bwd_call(proj, dycat, vec, w4, after):
    T = proj.shape[0]
    nt = T // SUB
    nch = T // LRU_CH

    def body(xb_ref, gate_ref, dy_ref, vec_ref, w4_ref, after_ref, dp_ref, dvec_ref, dw4_ref,
             xc_ref, af_ref, hf_ref, ab_ref, hb_ref, dh_ref, stage_ref, sems):
        dxb_ref, dgate_ref = stage_ref.at[0], stage_ref.at[1]
        vec = vec_ref[...]
        _lru_prepare(xb_ref, w4_ref, vec, xc_ref, af_ref, hf_ref, ab_ref, hb_ref, T)
        _lru_scan(af_ref, hf_ref, ab_ref, hb_ref, T)

        def gate_bwd(c, carry):
            rows = _rows(c * LRU_CH, LRU_CH)
            gl, dgl = _gelu_parts(gate_ref[rows, :])
            dy = dy_ref[rows, :]
            dgate_ref[rows, :] = (dy * (hf_ref[rows, :] + hb_ref[rows, :]) * dgl).astype(dgate_ref.dtype)
            dh_ref[rows, :] = dy * gl
            return carry

        lax.fori_loop(0, nch, gate_bwd, 0)

        row = lax.broadcasted_iota(jnp.int32, (SUB, LANE), 0)

        def adj(j, carry):
            gf, a_next, gb, a_prev = carry
            tf = nt - 1 - j
            sf = _rows(tf * SUB, SUB)
            a_t = af_ref[sf, :]
            h_t = hf_ref[sf, :]
            coef = jnp.where(row == SUB - 1, a_next, pltpu.roll(a_t, SUB - 1, 0))
            ac, bc = _scan_tile(coef, dh_ref[sf, :], True, row)
            g = ac * gf + bc
            h_prev = hf_ref[_rows(jnp.maximum(tf - 1, 0) * SUB, SUB), :]
            h_prev = jnp.where(tf > 0, _bcast_row(h_prev, SUB - 1), 0.0)
            hs = jnp.where(row == 0, h_prev, pltpu.roll(h_t, 1, 0))
            af_ref[sf, :] = g * hs
            hf_ref[sf, :] = g
            gf = _bcast_row(g, 0)
            a_next = _bcast_row(a_t, 0)
            sb = _rows(j * SUB, SUB)
            a_t = ab_ref[sb, :]
            h_t = hb_ref[sb, :]
            coef = jnp.where(row == 0, a_prev, pltpu.roll(a_t, 1, 0))
            ac, bc = _scan_tile(coef, dh_ref[sb, :], False, row)
            g = ac * gb + bc
            h_next = hb_ref[_rows(jnp.minimum(j + 1, nt - 1) * SUB, SUB), :]
            h_next = jnp.where(j < nt - 1, _bcast_row(h_next, 0), 0.0)
            hs = jnp.where(row == SUB - 1, h_next, pltpu.roll(h_t, SUB - 1, 0))
            ab_ref[sb, :] = g * hs
            hb_ref[sb, :] = g
            gb = _bcast_row(g, SUB - 1)
            a_prev = _bcast_row(a_t, SUB - 1)
            return gf, a_next, gb, a_prev

        z = jnp.zeros((SUB, LANE), F32)
        lax.fori_loop(0, nt, adj, (z, z, z, z))

        sp_f = _softplus_neg(vec[9:10, :])
        sp_b = _softplus_neg(vec[10:11, :])
        w4 = w4_ref[0]
        dw4_ref[...] = jnp.zeros_like(dw4_ref)

        def one_dir(pre_a, pre_x, sp, xc, du, da):
            r, i, a, s = _lru_dir(pre_a, pre_x, sp)
            d_i = du * s * xc
            dxc = du * s * i
            d_s = du * i * xc
            d_log = da * a - d_s * (a * a) / s
            d_r = d_log * (-LRU_C) * sp
            d_sp = jnp.sum(d_log * (-LRU_C) * r, axis=0, keepdims=True)
            return d_r * r * (1.0 - r), d_i * i * (1.0 - i), dxc, d_sp

        def gates_bwd(c, carry):
            db, dspf, dspb = carry
            rows = _rows(c * LRU_CH, LRU_CH)
            xc = xc_ref[rows, :]
            pre = _mm(xc, w4)
            dpa_f, dpx_f, dxc_f, d_sp_f = one_dir(pre[:, 0:128] + vec[5:6, :], pre[:, 128:256] + vec[6:7, :],
                                                  sp_f, xc, hf_ref[rows, :], af_ref[rows, :])
            dpa_b, dpx_b, dxc_b, d_sp_b = one_dir(pre[:, 256:384] + vec[7:8, :], pre[:, 384:512] + vec[8:9, :],
                                                  sp_b, xc, hb_ref[rows, :], ab_ref[rows, :])
            dpre = jnp.concatenate([dpa_f, dpx_f, dpa_b, dpx_b], axis=1)
            dw4_ref[0] += _mm_tn(xc, dpre)
            dh_ref[rows, :] = dxc_f + dxc_b + _mm_nt(dpre, w4)
            return db + jnp.sum(dpre, axis=0, keepdims=True), dspf + d_sp_f, dspb + d_sp_b

        z1 = jnp.zeros((1, LANE), F32)
        db, dspf, dspb = lax.fori_loop(0, nch, gates_bwd, (jnp.zeros((1, 4 * LANE), F32), z1, z1))

        def conv_bwd(c, carry):
            t0 = c * LRU_CH
            rows = _rows(t0, LRU_CH)
            dwin = _window(dh_ref, t0, LRU_CH, T)
            xwin = _window(xb_ref, t0, LRU_CH, T)
            dxc = dh_ref[rows, :]
            dxb = jnp.zeros((LRU_CH, LANE), F32)
            out = []
            for j in range(CONV_WIDTH):
                off = j - CONV_WIDTH // 2
                dxb = dxb + _tap(dwin, -off, LRU_CH) * vec[j:j + 1, :]
                out.append(carry[j] + jnp.sum(dxc * _tap(xwin, off, LRU_CH), axis=0, keepdims=True))
            dxb_ref[rows, :] = dxb.astype(dxb_ref.dtype)
            out.append(carry[CONV_WIDTH] + jnp.sum(dxc, axis=0, keepdims=True))
            return tuple(out)

        dconv = lax.fori_loop(0, nch, conv_bwd, (z1,) * (CONV_WIDTH + 1))
        dlam_f = dspf * (-_sigmoid(-vec[9:10, :]))
        dlam_b = dspb * (-_sigmoid(-vec[10:11, :]))
        dvec_ref[...] = jnp.concatenate(
            list(dconv) + [db[:, 0:128], db[:, 128:256], db[:, 256:384], db[:, 384:512], dlam_f, dlam_b,
                           jnp.zeros((5, LANE), F32)], axis=0)
        j = pl.program_id(0)
        _store_strips(stage_ref, dp_ref, (j, j + 3), sems)

    ns = LRU_WIDTH // LANE
    return pl.pallas_call(
        body, name="lru_bwd", grid=(ns,),
        in_specs=[_strip(T, lambda j: j), _strip(T, lambda j: j + 3), _strip(T, lambda j: j),
                  pl.BlockSpec((16, LANE), lambda j: (0, j)),
                  pl.BlockSpec((1, LANE, 4 * LANE), lambda j: (j, 0, 0)),
                  pl.BlockSpec(memory_space=pl.ANY)],
        out_specs=[pl.BlockSpec(memory_space=pl.ANY),
                   pl.BlockSpec((16, LANE), lambda j: (0, j)),
                   pl.BlockSpec((1, LANE, 4 * LANE), lambda j: (j, 0, 0))],
        out_shape=[jax.ShapeDtypeStruct((T, IN_WIDTH), _BF),
                   jax.ShapeDtypeStruct((16, LRU_WIDTH), F32), jax.ShapeDtypeStruct((ns, LANE, 4 * LANE), F32)],
        scratch_shapes=[pltpu.VMEM((T, LANE), F32)] * 6 + [pltpu.VMEM((2, T, LANE), _BF), pltpu.SemaphoreType.DMA((2,))],
        compiler_params=_cparams(("arbitrary",)),
    )(proj, proj, dycat, vec, w4, after)


def _lru_vec(cw, cb, ba, bx, lam):
    return jnp.concatenate([cw, cb[None], ba[0:1], bx[0:1], ba[1:2], bx[1:2], lam, jnp.zeros((5, LRU_WIDTH), F32)], axis=0)


def _lru_w4(wa, wx):
    nl = wa.shape[0]
    w = jnp.stack([wa[:, 0], wx[:, 0], wa[:, 1], wx[:, 1]], axis=1)
    w = w.reshape(nl, 4, 3, 2, 64, 64)
    eye = jnp.eye(2, dtype=w.dtype)
    bd = w[:, :, :, :, :, None, :] * eye[None, None, None, :, None, :, None]
    bd = bd.reshape(nl, 4, 3, LANE, LANE)
    return bd.transpose(0, 2, 3, 1, 4).reshape(nl, 3, LANE, 4 * LANE).astype(_BF)


def _lru_unpack(dvec, dw4):
    def blocks(m):
        m = m.reshape(3, 2, 64, 2, 64)
        return jnp.stack([m[:, 0, :, 0, :], m[:, 1, :, 1, :]], axis=1).reshape(6, 64, 64)
    parts = [blocks(dw4[:, :, k * LANE:(k + 1) * LANE]) for k in range(4)]
    dwa = jnp.stack([parts[0], parts[2]])
    dwx = jnp.stack([parts[1], parts[3]])
    dba = jnp.stack([dvec[5], dvec[7]])
    dbx = jnp.stack([dvec[6], dvec[8]])
    return dvec[0:4], dvec[4], dwa, dba, dwx, dbx, dvec[9:11]


RC = 2 * RET_CHUNK


def _ret_tables(T):
    half = HEAD_DIM // 2
    pos = jnp.arange(T, dtype=F32)
    inv_freq = ROPE_BASE ** (-jnp.arange(half, dtype=F32) / half)
    ang = pos[:, None] * inv_freq[None, :]
    cos = jnp.tile(jnp.cos(ang), (1, 4))
    sin = jnp.tile(jnp.concatenate([-jnp.sin(ang), jnp.sin(ang)], axis=1), (1, 2))
    log_g = jnp.log1p(-jnp.exp2(-5.0 - jnp.arange(RET_HEADS, dtype=F32)))
    idx = jnp.arange(RC, dtype=F32)
    dec = jnp.exp(jnp.abs(idx[:, None] - idx[None, :]) * log_g[:, None, None])
    lg = jnp.repeat(log_g, HEAD_DIM).reshape(3, 1, LANE)
    col = idx[None, :, None]
    rtab = jnp.stack([jnp.exp((RC - 1 - col) * lg), jnp.exp(col * lg),
                      jnp.exp((col + 1.0) * lg), jnp.exp((RC - col) * lg)], axis=1)
    gch = jnp.broadcast_to(jnp.exp(RC * lg), (3, SUB, LANE))
    return cos, sin, dec, rtab, gch


def _swap32(x, lane):
    return jnp.where((lane & 32) == 0, pltpu.roll(x, LANE - 32, 1), pltpu.roll(x, 32, 1))


def _head_mean(x, m0, m1):
    s0 = jnp.sum(x * m0, axis=-1, keepdims=True)
    s1 = jnp.sum(x * m1, axis=-1, keepdims=True)
    return (s0 * m0 + s1 * m1) * (1.0 / HEAD_DIM)


def _ret_masks():
    lane = lax.broadcasted_iota(jnp.int32, (RC, LANE), 1)
    m0 = (lane < HEAD_DIM).astype(F32)
    r = lax.broadcasted_iota(jnp.int32, (LANE, LANE), 0) // HEAD_DIM
    c = lax.broadcasted_iota(jnp.int32, (LANE, LANE), 1) // HEAD_DIM
    return lane, m0, 1.0 - m0, (r == c).astype(F32)


def _ret_specs(T):
    const = lambda shape, imap: pl.BlockSpec(shape, imap)
    return [_strip(T, lambda j: j + 6), _strip(T, lambda j: j + 9), _strip(T, lambda j: j + 12),
            _strip(T, lambda j: j + 15),
            pl.BlockSpec((T, LANE), lambda j: (0, 0), pipeline_mode=pl.Buffered(1)),
            pl.BlockSpec((T, LANE), lambda j: (0, 0), pipeline_mode=pl.Buffered(1)),
            const((2, RC, RC), lambda j: (j, 0, 0)),
            const((1, 4, RC, LANE), lambda j: (j, 0, 0, 0)),
            const((1, SUB, LANE), lambda j: (j, 0, 0)),
            const((SUB, LANE), lambda j: (0, j))]


def _ret_fwd_call(proj, tables, gnw8):
    T = proj.shape[0]
    nc = T // RC
    cos, sin, dec, rtab, gch = tables

    def body(q_ref, k_ref, v_ref, g_ref, cos_ref, sin_ref, dec_ref, rtab_ref, gch_ref, gnw_ref, y_ref, stf_ref, kr_ref):
        lane, m0, m1, bd = _ret_masks()
        gch_v = gch_ref[0][0:1, :]
        gnw = gnw_ref[0:1, :]
        dkf, dkb, dqf, dqb = rtab_ref[0, 0], rtab_ref[0, 1], rtab_ref[0, 2], rtab_ref[0, 3]

        def rope(x, rows):
            return x * cos_ref[rows, :] + _swap32(x, lane) * sin_ref[rows, :]

        def pass_a(n, st):
            rows = _rows(n * RC, RC)
            stf_ref[n] = st
            kr = rope(k_ref[rows, :], rows) * (HEAD_DIM ** -0.5)
            kr_ref[rows, :] = kr
            return gch_v * st + _mm_tn(kr * dkf, v_ref[rows, :]) * bd

        _loop2(nc, pass_a, jnp.zeros((LANE, LANE), F32))

        def pass_b(i, stb):
            ns = [nc - 1 - 2 * i, nc - 2 - 2 * i]
            rows = [_rows(n * RC, RC) for n in ns]
            heads = ((0, m0), (1, m1))
            qr = [rope(q_ref[r, :], r) for r in rows]
            kr = [kr_ref[r, :] for r in rows]
            v = [v_ref[r, :] for r in rows]
            kv = [_mm_tn(kr[c] * dkb, v[c]) * bd for c in range(2)]
            stbs = [stb, gch_v * stb + kv[0]]
            s = [[_mm_nt(qr[c] * m, kr[c]) * dec_ref[h] for h, m in heads] for c in range(2)]
            o = [_mm(qr[c] * dqf, stf_ref[ns[c]]) + _mm(qr[c] * dqb, stbs[c]) for c in range(2)]
            o = [o[c] + _mm(s[c][0], v[c] * m0) + _mm(s[c][1], v[c] * m1) for c in range(2)]
            oc = [o_ - _head_mean(o_, m0, m1) for o_ in o]
            on = [oc_ * lax.rsqrt(_head_mean(oc_ * oc_, m0, m1) + GN_EPS) for oc_ in oc]
            for c in range(2):
                g = g_ref[rows[c], :]
                y_ref[rows[c], :] = (g * _sigmoid(g)) * (on[c] * gnw)
            return gch_v * stbs[1] + kv[1]

        assert nc % 2 == 0
        lax.fori_loop(0, nc // 2, pass_b, jnp.zeros((LANE, LANE), F32))

    return pl.pallas_call(
        body, name="ret_fwd", grid=(RET_WIDTH // LANE,),
        in_specs=_ret_specs(T),
        out_specs=_strip(T, lambda j: j, buffers=1),
        out_shape=jax.ShapeDtypeStruct((T, RET_WIDTH), F32),
        scratch_shapes=[pltpu.VMEM((nc, LANE, LANE), F32), pltpu.VMEM((T, LANE), F32)],
        compiler_params=_cparams(("arbitrary",)),
    )(proj, proj, proj, proj, cos, sin, dec, rtab, gch, gnw8)


def _ret_bwd_call(proj, dycat, tables, gnw8, dp):
    T = proj.shape[0]
    nc = T // RC
    cos, sin, dec, rtab, gch = tables

    def body(q_ref, k_ref, v_ref, g_ref, cos_ref, sin_ref, dec_ref, rtab_ref, gch_ref, gnw_ref, dy_ref, dp_in_ref,
             dp_out_ref, dgnw_ref, stf_ref, dstb_ref, dkr_ref, dv_ref, dp_ref, kr_ref, sems):
        lane, m0, m1, bd = _ret_masks()
        gch_v = gch_ref[0][0:1, :]
        gnw = gnw_ref[0:1, :]
        dkf, dkb, dqf, dqb = rtab_ref[0, 0], rtab_ref[0, 1], rtab_ref[0, 2], rtab_ref[0, 3]
        scale = HEAD_DIM ** -0.5
        zst = jnp.zeros((LANE, LANE), F32)

        def rope(x, rows):
            return x * cos_ref[rows, :] + _swap32(x, lane) * sin_ref[rows, :]

        def rope_t(d, rows):
            return d * cos_ref[rows, :] + _swap32(d * sin_ref[rows, :], lane)

        def pass_a(n, st):
            rows = _rows(n * RC, RC)
            stf_ref[n] = st
            kr = rope(k_ref[rows, :], rows) * scale
            kr_ref[rows, :] = kr
            return gch_v * st + _mm_tn(kr * dkf, v_ref[rows, :]) * bd

        _loop2(nc, pass_a, zst)

        def pass_b(i, carry):
            stb, d_f, dgnw = carry
            two = range(2)
            heads = ((0, m0), (1, m1))
            ns = [nc - 1 - 2 * i, nc - 2 - 2 * i]
            rows = [_rows(n * RC, RC) for n in ns]
            qr = [rope(q_ref[r, :], r) for r in rows]
            kr = [kr_ref[r, :] for r in rows]
            v = [v_ref[r, :] for r in rows]
            stf = [stf_ref[n] for n in ns]
            kvb = [_mm_tn(kr[c] * dkb, v[c]) * bd for c in two]
            stbs = [stb, gch_v * stb + kvb[0]]
            qf = [qr[c] * dqf for c in two]
            qb = [qr[c] * dqb for c in two]
            s = [[_mm_nt(qr[c] * m, kr[c]) * dec_ref[h] for h, m in heads] for c in two]
            o = [_mm(qf[c], stf[c]) + _mm(qb[c], stbs[c]) for c in two]
            o = [o[c] + _mm(s[c][0], v[c] * m0) + _mm(s[c][1], v[c] * m1) for c in two]
            oc = [o_ - _head_mean(o_, m0, m1) for o_ in o]
            rstd = [lax.rsqrt(_head_mean(oc_ * oc_, m0, m1) + GN_EPS) for oc_ in oc]
            on = [oc[c] * rstd[c] for c in two]
            do = []
            for c in two:
                g = g_ref[rows[c], :]
                sg = _sigmoid(g)
                dy = dy_ref[rows[c], :]
                dp_ref[3, rows[c], :] = (dy * (on[c] * gnw) * (sg * (1.0 + g * (1.0 - sg)))).astype(dp_ref.dtype)
                t = dy * (g * sg)
                dgnw = dgnw + jnp.sum(t * on[c], axis=0, keepdims=True)
                don = t * gnw
                do.append(rstd[c] * (don - _head_mean(don, m0, m1) - on[c] * _head_mean(don * on[c], m0, m1)))
            dstf = [_mm_tn(qf[c], do[c]) * bd for c in two]
            dfs = [d_f, dstf[0] + gch_v * d_f]
            ds = [[_mm_nt(do[c] * m, v[c]) * dec_ref[h] for h, m in heads] for c in two]
            dqr = [_mm_nt(do[c], stf[c]) * dqf + _mm_nt(do[c], stbs[c]) * dqb
                   + _mm(ds[c][0], kr[c] * m0) + _mm(ds[c][1], kr[c] * m1) for c in two]
            dkr = [_mm_nt(v[c], dfs[c]) * dkf + _mm_tn(ds[c][0], qr[c] * m0) + _mm_tn(ds[c][1], qr[c] * m1) for c in two]
            dv = [_mm(kr[c] * dkf, dfs[c]) + _mm_tn(s[c][0], do[c] * m0) + _mm_tn(s[c][1], do[c] * m1) for c in two]
            for c in two:
                dp_ref[0, rows[c], :] = rope_t(dqr[c], rows[c]).astype(dp_ref.dtype)
                dkr_ref[rows[c], :] = dkr[c]
                dv_ref[rows[c], :] = dv[c]
                dstb_ref[ns[c]] = _mm_tn(qb[c], do[c]) * bd
            return gch_v * stbs[1] + kvb[1], dstf[1] + gch_v * dfs[1], dgnw

        assert nc % 2 == 0
        _, _, dgnw = lax.fori_loop(0, nc // 2, pass_b, (zst, zst, jnp.zeros((1, LANE), F32)))
        dgnw_ref[...] = jnp.concatenate([dgnw, jnp.zeros((SUB - 1, LANE), F32)], axis=0)

        def pass_c(n, d_b):
            rows = _rows(n * RC, RC)
            kr = kr_ref[rows, :]
            v = v_ref[rows, :]
            dkr = dkr_ref[rows, :] + _mm_nt(v, d_b) * dkb
            dp_ref[1, rows, :] = (rope_t(dkr, rows) * scale).astype(dp_ref.dtype)
            dp_ref[2, rows, :] = (dv_ref[rows, :] + _mm(kr * dkb, d_b)).astype(dp_ref.dtype)
            return dstb_ref[n] + gch_v * d_b

        _loop2(nc, pass_c, zst)
        j = pl.program_id(0)
        _store_strips(dp_ref, dp_out_ref, (j + 6, j + 9, j + 12, j + 15), sems)

    n_in = len(_ret_specs(T)) + 1
    return pl.pallas_call(
        body, name="ret_bwd", grid=(RET_WIDTH // LANE,),
        in_specs=_ret_specs(T) + [_strip(T, lambda j: j + 3), pl.BlockSpec(memory_space=pl.ANY)],
        out_specs=[pl.BlockSpec(memory_space=pl.ANY), pl.BlockSpec((SUB, LANE), lambda j: (0, j))],
        out_shape=[jax.ShapeDtypeStruct(dp.shape, dp.dtype), jax.ShapeDtypeStruct((SUB, RET_WIDTH), F32)],
        scratch_shapes=[pltpu.VMEM((nc, LANE, LANE), F32), pltpu.VMEM((nc, LANE, LANE), F32),
                        pltpu.VMEM((T, LANE), F32), pltpu.VMEM((T, LANE), F32),
                        pltpu.VMEM((4, T, LANE), _BF), pltpu.VMEM((T, LANE), F32), pltpu.SemaphoreType.DMA((4,))],
        input_output_aliases={n_in: 0},
        compiler_params=_cparams(("arbitrary",)),
    )(proj, proj, proj, proj, cos, sin, dec, rtab, gch, gnw8, dycat, dp)


NA_Q = 2 * GRID_W
NA_WROWS = 10
NA_K = NA_WROWS * GRID_W
NA_CHUNKS = NA_K // LANE
NA_UNROLL = 4
NA_TYPES = 5
_ONEHOT_PRECISION = lax.Precision.HIGH


def _na_onehots(rows_n):
    reps = [(0, 0), (2, 0), (4, 0), (rows_n - 4, rows_n - NA_WROWS), (rows_n - 2, rows_n - NA_WROWS)]
    rm = np.zeros((NA_TYPES, 2, NA_WROWS, 2 * NA_KH - 1), np.float32)
    for t, (r, ws) in enumerate(reps):
        for qh in range(2):
            qrow = r + qh
            rstart = min(max(qrow - NA_KH // 2, 0), rows_n - NA_KH)
            for kh in range(NA_WROWS):
                krow = ws + kh
                if rstart <= krow < rstart + NA_KH:
                    rm[t, qh, kh, krow - qrow + NA_KH - 1] = 1.0
    cm = np.zeros((GRID_W, GRID_W, 2 * NA_KW - 1), np.float32)
    for qc in range(GRID_W):
        cstart = min(max(qc - NA_KW // 2, 0), GRID_W - NA_KW)
        for kc in range(cstart, cstart + NA_KW):
            cm[qc, kc, kc - qc + NA_KW - 1] = 1.0
    rm2 = rm.reshape(NA_TYPES, 2, NA_CHUNKS, 2, 2 * NA_KH - 1)
    cm2 = np.zeros((GRID_W, LANE, 2, 2 * NA_KW - 1), np.float32)
    for z in range(2):
        cm2[:, z * GRID_W:(z + 1) * GRID_W, z, :] = cm
    return rm2, cm2


def _na_bias_tables(rpb, rows_n):
    rm, cm = _na_onehots(rows_n)
    val = jnp.einsum("hab,tqpza,xkzb->htpqxk", rpb, rm, cm, precision=_ONEHOT_PRECISION)
    valid = np.einsum("tqpz,xkz->tpqxk", rm.sum(-1), cm.sum(-1)) > 0.5
    return jnp.where(valid[None], val, NEG).reshape(2, 2, NA_TYPES, NA_CHUNKS, NA_Q, LANE)


def _na_bias_grad(dtab, rows_n):
    rm, cm = _na_onehots(rows_n)
    d6 = dtab.reshape(NA_HEADS, NA_TYPES, NA_CHUNKS, 2, GRID_W, LANE)
    return jnp.einsum("htpqxk,tqpza,xkzb->hab", d6, rm, cm, precision=_ONEHOT_PRECISION)


def _na_bias(b_ref, h, typ):
    return jnp.concatenate([b_ref[0, h, typ, c] for c in range(NA_CHUNKS)], axis=1)


def _na_step(p, npairs, rows_n):
    ws = jnp.clip(2 * p - NA_KH // 2, 0, rows_n - NA_WROWS)
    koff = pl.multiple_of(ws * GRID_W, LANE)
    typ = jnp.where(p == 0, 0, jnp.where(p == 1, 1, jnp.where(p == npairs - 2, 3, jnp.where(p == npairs - 1, 4, 2))))
    return _rows(p * NA_Q, NA_Q), pl.ds(koff, NA_K), typ


def _na_fwd_call(proj, btab):
    T = proj.shape[0]
    npairs, rows_n = T // NA_Q, T // GRID_W

    def body(q_ref, k_ref, v_ref, b_ref, o_ref):
        lane = lax.broadcasted_iota(jnp.int32, (NA_Q, LANE), 1)
        m0 = (lane < HEAD_DIM).astype(F32)
        m1 = 1.0 - m0

        def steps(i, carry):
            idx = [_na_step(NA_UNROLL * i + u, npairs, rows_n) for u in range(NA_UNROLL)]
            chains = [(u, h, m) for u in range(NA_UNROLL) for h, m in ((0, m0), (1, m1))]
            kws = [k_ref[krows, :].astype(_BF) for _, krows, _ in idx]
            vws = [v_ref[krows, :].astype(_BF) for _, krows, _ in idx]
            s = [_mm_nt(q_ref[idx[u][0], :] * m, kws[u]) for u, h, m in chains]
            s = [s_ * (HEAD_DIM ** -0.5) + _na_bias(b_ref, h, idx[u][2]) for s_, (u, h, m) in zip(s, chains)]
            e = [jnp.exp(s_ - jnp.max(s_, axis=-1, keepdims=True)) for s_ in s]
            pr = [e_ / jnp.sum(e_, axis=-1, keepdims=True) for e_ in e]
            ov = [_mm(pr_, vws[u]) * m for pr_, (u, h, m) in zip(pr, chains)]
            for u in range(NA_UNROLL):
                o_ref[idx[u][0], :] = ov[2 * u] + ov[2 * u + 1]
            return carry

        lax.fori_loop(0, npairs // NA_UNROLL, steps, 0)

    return pl.pallas_call(
        body, name="na_fwd", grid=(NA_WIDTH // LANE,),
        in_specs=[_strip(T, lambda j: j + 18), _strip(T, lambda j: j + 20), _strip(T, lambda j: j + 22),
                  pl.BlockSpec((1, 2, NA_TYPES, NA_CHUNKS, NA_Q, LANE), lambda j: (j, 0, 0, 0, 0, 0))],
        out_specs=_strip(T, lambda j: j, buffers=1),
        out_shape=jax.ShapeDtypeStruct((T, NA_WIDTH), F32),
        compiler_params=_cparams(("arbitrary",)),
    )(proj, proj, proj, btab)


def _na_bwd_call(proj, dycat, btab, dp):
    T = proj.shape[0]
    npairs, rows_n = T // NA_Q, T // GRID_W
    scale = HEAD_DIM ** -0.5

    def body(q_ref, k_ref, v_ref, do_ref, b_ref, dp_in_ref, dp_out_ref, db_ref, dka_ref, dva_ref, stage_ref, sems):
        dq_ref = stage_ref.at[0]
        lane = lax.broadcasted_iota(jnp.int32, (NA_Q, LANE), 1)
        m0 = (lane < HEAD_DIM).astype(F32)
        m1 = 1.0 - m0
        dka_ref[...] = jnp.zeros_like(dka_ref)
        dva_ref[...] = jnp.zeros_like(dva_ref)
        db_ref[...] = jnp.zeros_like(db_ref)

        def steps(i, carry):
            idx = [_na_step(NA_UNROLL * i + u, npairs, rows_n) for u in range(NA_UNROLL)]
            chains = [(u, h, m) for u in range(NA_UNROLL) for h, m in ((0, m0), (1, m1))]
            kws = [k_ref[krows, :].astype(_BF) for _, krows, _ in idx]
            vws = [v_ref[krows, :].astype(_BF) for _, krows, _ in idx]
            qm = [(q_ref[idx[u][0], :] * m).astype(_BF) for u, h, m in chains]
            dom = [(do_ref[idx[u][0], :] * m).astype(_BF) for u, h, m in chains]
            s = [_mm_nt(qm_, kws[u]) for qm_, (u, h, m) in zip(qm, chains)]
            dpr = [_mm_nt(dom_, vws[u]) for dom_, (u, h, m) in zip(dom, chains)]
            s = [s_ * scale + _na_bias(b_ref, h, idx[u][2]) for s_, (u, h, m) in zip(s, chains)]
            e = [jnp.exp(s_ - jnp.max(s_, axis=-1, keepdims=True)) for s_ in s]
            pr = [e_ / jnp.sum(e_, axis=-1, keepdims=True) for e_ in e]
            ds = [pr_ * (dpr_ - jnp.sum(pr_ * dpr_, axis=-1, keepdims=True)) for pr_, dpr_ in zip(pr, dpr)]
            dsb = [(ds_ * scale).astype(_BF) for ds_ in ds]
            dq = [_mm(dsb_, kws[u]) * m for dsb_, (u, h, m) in zip(dsb, chains)]
            dk = [_mm_tn(dsb_, qm_) for dsb_, qm_ in zip(dsb, qm)]
            dv = [_mm_tn(pr_, dom_) for pr_, dom_ in zip(pr, dom)]
            for ds_, (u, h, m) in zip(ds, chains):
                for c in range(NA_CHUNKS):
                    db_ref[0, h, idx[u][2], c] += ds_[:, c * LANE:(c + 1) * LANE]
            for u in range(NA_UNROLL):
                qrows, krows, _ = idx[u]
                dq_ref[qrows, :] = (dq[2 * u] + dq[2 * u + 1]).astype(dq_ref.dtype)
                dka_ref[krows, :] += dk[2 * u] + dk[2 * u + 1]
                dva_ref[krows, :] += dv[2 * u] + dv[2 * u + 1]
            return carry

        lax.fori_loop(0, npairs // NA_UNROLL, steps, 0)
        stage_ref[1] = dka_ref[...].astype(stage_ref.dtype)
        stage_ref[2] = dva_ref[...].astype(stage_ref.dtype)
        j = pl.program_id(0)
        _store_strips(stage_ref, dp_out_ref, (j + 18, j + 20, j + 22), sems)

    tab = pl.BlockSpec((1, 2, NA_TYPES, NA_CHUNKS, NA_Q, LANE), lambda j: (j, 0, 0, 0, 0, 0))
    return pl.pallas_call(
        body, name="na_bwd", grid=(NA_WIDTH // LANE,),
        in_specs=[_strip(T, lambda j: j + 18), _strip(T, lambda j: j + 20), _strip(T, lambda j: j + 22),
                  _strip(T, lambda j: j + 6), tab, pl.BlockSpec(memory_space=pl.ANY)],
        out_specs=[pl.BlockSpec(memory_space=pl.ANY), tab],
        out_shape=[jax.ShapeDtypeStruct(dp.shape, dp.dtype),
                   jax.ShapeDtypeStruct((2, 2, NA_TYPES, NA_CHUNKS, NA_Q, LANE), F32)],
        scratch_shapes=[pltpu.VMEM((T, LANE), F32), pltpu.VMEM((T, LANE), F32),
                        pltpu.VMEM((3, T, LANE), _BF), pltpu.SemaphoreType.DMA((3,))],
        input_output_aliases={5: 0},
        compiler_params=_cparams(("arbitrary",)),
    )(proj, proj, proj, dycat, btab, dp)


W_BLK = IN_WIDTH // N_DEV
MXU_W = 256
N_BLK = 3 * MXU_W
N_STEPS = IN_WIDTH // N_BLK
TM = 512


def _ln_fwd(z, g, b):
    zc = z - jnp.mean(z, axis=-1, keepdims=True)
    var = jnp.mean(zc * zc, axis=-1, keepdims=True)
    return zc * lax.rsqrt(var + LN_EPS) * g + b


def _ln_bwd(dy, z, g):
    zc = z - jnp.mean(z, axis=-1, keepdims=True)
    rstd = lax.rsqrt(jnp.mean(zc * zc, axis=-1, keepdims=True) + LN_EPS)
    xhat = zc * rstd
    dxh = dy * g
    dz = rstd * (dxh - jnp.mean(dxh, axis=-1, keepdims=True) - xhat * jnp.mean(dxh * xhat, axis=-1, keepdims=True))
    return dz, dy * xhat


def _row_tile(T):
    return 1024 if T % 1024 == 0 else TM


def _halves(n):
    return (pl.ds(0, n // 2), pl.ds(n // 2, n // 2))


def _inproj_call(xb, w, after):
    T = xb.shape[0]
    tm = _row_tile(T)

    def body(x_ref, w_ref, after_ref, o_ref):
        o_ref[...] = _mm(x_ref[...], w_ref[...])

    return pl.pallas_call(
        body, name="inproj", grid=(T // tm, N_STEPS),
        in_specs=[pl.BlockSpec((tm, D_MODEL), lambda i, n: (i, 0)),
                  pl.BlockSpec((D_MODEL, N_BLK), lambda i, n: (0, n)),
                  pl.BlockSpec(memory_space=pl.ANY)],
        out_specs=pl.BlockSpec((tm, N_BLK), lambda i, n: (i, n)),
        out_shape=jax.ShapeDtypeStruct((T, IN_WIDTH), F32),
        compiler_params=_cparams(("parallel", "arbitrary")),
    )(xb, w, after)


def _vec_spec():
    return pl.BlockSpec((1, D_MODEL), lambda *_: (0, 0))


def _outproj_ln_call(y_lru, y_ret, y_na, x, w, g, b, after):
    T = x.shape[0]

    def body(yl_ref, yr_ref, yn_ref, x_ref, w_ref, g_ref, b_ref, after_ref, z_ref, x1_ref, x1b_ref, yc_ref):
        yc_ref[:, 0:LRU_WIDTH] = yl_ref[...].astype(yc_ref.dtype)
        yc_ref[:, LRU_WIDTH:LRU_WIDTH + RET_WIDTH] = yr_ref[...].astype(yc_ref.dtype)
        yc_ref[:, LRU_WIDTH + RET_WIDTH:] = yn_ref[...].astype(yc_ref.dtype)
        z = ALPHA * x_ref[...] + _mm(yc_ref[...], w_ref[...])
        z_ref[...] = z
        x1 = _ln_fwd(z, g_ref[...], b_ref[...])
        x1_ref[...] = x1
        x1b_ref[...] = x1.astype(x1b_ref.dtype)

    row = lambda w_: pl.BlockSpec((TM, w_), lambda i: (i, 0))
    return pl.pallas_call(
        body, name="outproj_ln", grid=(T // TM,),
        in_specs=[row(LRU_WIDTH), row(RET_WIDTH), row(NA_WIDTH), row(D_MODEL),
                  pl.BlockSpec((D_MODEL, D_MODEL), lambda i: (0, 0)), _vec_spec(), _vec_spec(),
                  pl.BlockSpec(memory_space=pl.ANY)],
        out_specs=[row(D_MODEL)] * 4,
        out_shape=[jax.ShapeDtypeStruct((T, D_MODEL), F32), jax.ShapeDtypeStruct((T, D_MODEL), F32),
                   jax.ShapeDtypeStruct((T, D_MODEL), _BF), jax.ShapeDtypeStruct((T, D_MODEL), _BF)],
        compiler_params=_cparams(("parallel",)),
    )(y_lru, y_ret, y_na, x, w, g, b, after)


def _ffn_ln_call(x1, x1b, wg, wu, wd, g, b):
    T = x1.shape[0]

    def body(x_ref, xb_ref, wg_ref, wu_ref, wd_ref, g_ref, b_ref, z_ref, x2_ref, x2b_ref, gp_ref, up_ref, acc_ref):
        n = pl.program_id(1)

        @pl.when(n == 0)
        def _():
            acc_ref[...] = jnp.zeros_like(acc_ref)

        r0, r1 = _halves(TM)

        def pre(rows):
            xb = xb_ref[rows, :]
            return _mm(xb, wg_ref[...]), _mm(xb, wu_ref[...])

        def act(rows, gp, up):
            gp_ref[rows, :] = gp.astype(gp_ref.dtype)
            up_ref[rows, :] = up.astype(up_ref.dtype)
            return (gp * _sigmoid(gp) * up).astype(_BF)

        gp0, up0 = pre(r0)
        hid0 = act(r0, gp0, up0)
        gp1, up1 = pre(r1)
        acc_ref[r0, :] += _mm(hid0, wd_ref[...])
        hid1 = act(r1, gp1, up1)
        acc_ref[r1, :] += _mm(hid1, wd_ref[...])

        @pl.when(n == N_STEPS - 1)
        def _():
            z = ALPHA * x_ref[...] + acc_ref[...]
            z_ref[...] = z
            x2 = _ln_fwd(z, g_ref[...], b_ref[...])
            x2_ref[...] = x2
            x2b_ref[...] = x2.astype(x2b_ref.dtype)

    row = pl.BlockSpec((TM, D_MODEL), lambda i, n: (i, 0))
    return pl.pallas_call(
        body, name="ffn_ln", grid=(T // TM, N_STEPS),
        in_specs=[row, row,
                  pl.BlockSpec((D_MODEL, N_BLK), lambda i, n: (0, n)),
                  pl.BlockSpec((D_MODEL, N_BLK), lambda i, n: (0, n)),
                  pl.BlockSpec((N_BLK, D_MODEL), lambda i, n: (n, 0)), _vec_spec(), _vec_spec()],
        out_specs=[row] * 3 + [pl.BlockSpec((TM, N_BLK), lambda i, n: (i, n))] * 2,
        out_shape=[jax.ShapeDtypeStruct((T, D_MODEL), F32), jax.ShapeDtypeStruct((T, D_MODEL), F32),
                   jax.ShapeDtypeStruct((T, D_MODEL), _BF),
                   jax.ShapeDtypeStruct((T, IN_WIDTH), _BF), jax.ShapeDtypeStruct((T, IN_WIDTH), _BF)],
        scratch_shapes=[pltpu.VMEM((TM, D_MODEL), F32)],
        compiler_params=_cparams(("parallel", "arbitrary")),
    )(x1, x1b, wg, wu, wd, g, b)


def _loss_call(y, t):
    T = y.shape[0]

    def body(y_ref, t_ref, dy_ref, loss_ref):
        @pl.when(pl.program_id(0) == 0)
        def _():
            loss_ref[...] = jnp.zeros_like(loss_ref)

        err = y_ref[...] - t_ref[...]
        dy_ref[...] = err * (1.0 / D_MODEL)
        part = 0.5 * jnp.sum(jnp.mean(err * err, axis=-1, keepdims=True), axis=0, keepdims=True)
        loss_ref[...] += jnp.broadcast_to(part, loss_ref.shape)

    row = pl.BlockSpec((TM, D_MODEL), lambda i: (i, 0))
    return pl.pallas_call(
        body, name="loss", grid=(T // TM,),
        in_specs=[row, row],
        out_specs=[row, pl.BlockSpec((SUB, LANE), lambda i: (0, 0))],
        out_shape=[jax.ShapeDtypeStruct((T, D_MODEL), F32), jax.ShapeDtypeStruct((SUB, LANE), F32)],
        compiler_params=_cparams(("arbitrary",)),
    )(y, t)


def _ffn_bwd_call(dx2, z2, gpb, upb, wg, wu, wd, g, after):
    T = dx2.shape[0]

    def body(dx2_ref, z_ref, gp_ref, up_ref, wg_ref, wu_ref, wd_ref, g_ref, after_ref,
             dx1_ref, dgp_ref, dup_ref, hid_ref, dzb_ref, dln_ref, acc_ref):
        i, n = pl.program_id(0), pl.program_id(1)

        @pl.when((i == 0) & (n == 0))
        def _():
            dln_ref[...] = jnp.zeros_like(dln_ref)

        @pl.when(n == 0)
        def _():
            dy = dx2_ref[...]
            dz, dg_rows = _ln_bwd(dy, z_ref[...], g_ref[...])
            dzb_ref[...] = dz.astype(dzb_ref.dtype)
            acc_ref[...] = ALPHA * dz
            dln_ref[0:1, :] += jnp.sum(dg_rows, axis=0, keepdims=True)
            dln_ref[1:2, :] += jnp.sum(dy, axis=0, keepdims=True)

        r0, r1 = _halves(TM)

        def grads(rows, dhid):
            gp = gp_ref[rows, :].astype(F32)
            up = up_ref[rows, :].astype(F32)
            sg = _sigmoid(gp)
            act = gp * sg
            hid_ref[rows, :] = (act * up).astype(hid_ref.dtype)
            dup = (dhid * act).astype(_BF)
            dgp = (dhid * up * (sg * (1.0 + gp * (1.0 - sg)))).astype(_BF)
            dgp_ref[rows, :] = dgp.astype(dgp_ref.dtype)
            dup_ref[rows, :] = dup.astype(dup_ref.dtype)
            return dgp, dup

        dhid0 = _mm_nt(dzb_ref[r0, :], wd_ref[...])
        dhid1 = _mm_nt(dzb_ref[r1, :], wd_ref[...])
        dgp0, dup0 = grads(r0, dhid0)
        acc_ref[r0, :] += _mm_nt(dgp0, wg_ref[...]) + _mm_nt(dup0, wu_ref[...])
        dgp1, dup1 = grads(r1, dhid1)
        acc_ref[r1, :] += _mm_nt(dgp1, wg_ref[...]) + _mm_nt(dup1, wu_ref[...])

        @pl.when(n == N_STEPS - 1)
        def _():
            dx1_ref[...] = acc_ref[...]

    row = pl.BlockSpec((TM, D_MODEL), lambda i, n: (i, 0))
    blk = pl.BlockSpec((TM, N_BLK), lambda i, n: (i, n))
    return pl.pallas_call(
        body, name="ffn_bwd", grid=(T // TM, N_STEPS),
        in_specs=[row, row, blk, blk,
                  pl.BlockSpec((D_MODEL, N_BLK), lambda i, n: (0, n)),
                  pl.BlockSpec((D_MODEL, N_BLK), lambda i, n: (0, n)),
                  pl.BlockSpec((N_BLK, D_MODEL), lambda i, n: (n, 0)), _vec_spec(),
                  pl.BlockSpec(memory_space=pl.ANY)],
        out_specs=[row, blk, blk, blk, row, pl.BlockSpec((SUB, D_MODEL), lambda i, n: (0, 0))],
        out_shape=[jax.ShapeDtypeStruct((T, D_MODEL), F32),
                   jax.ShapeDtypeStruct((T, IN_WIDTH), _BF), jax.ShapeDtypeStruct((T, IN_WIDTH), _BF),
                   jax.ShapeDtypeStruct((T, IN_WIDTH), _BF), jax.ShapeDtypeStruct((T, D_MODEL), _BF),
                   jax.ShapeDtypeStruct((SUB, D_MODEL), F32)],
        scratch_shapes=[pltpu.VMEM((TM, D_MODEL), F32)],
        compiler_params=_cparams(("arbitrary", "arbitrary")),
    )(dx2, z2, gpb, upb, wg, wu, wd, g, after)


def _outproj_bwd_call(dx1, z1, w, g):
    T = dx1.shape[0]

    def body(dx_ref, z_ref, w_ref, g_ref, dzb_ref, dyc_ref, dres_ref, dln_ref):
        @pl.when(pl.program_id(0) == 0)
        def _():
            dln_ref[...] = jnp.zeros_like(dln_ref)

        dy = dx_ref[...]
        dz, dg_rows = _ln_bwd(dy, z_ref[...], g_ref[...])
        dzb_ref[...] = dz.astype(dzb_ref.dtype)
        dres_ref[...] = ALPHA * dz
        dyc_ref[...] = _mm_nt(dz, w_ref[...])
        dln_ref[0:1, :] += jnp.sum(dg_rows, axis=0, keepdims=True)
        dln_ref[1:2, :] += jnp.sum(dy, axis=0, keepdims=True)

    row = pl.BlockSpec((TM, D_MODEL), lambda i: (i, 0))
    return pl.pallas_call(
        body, name="outproj_bwd", grid=(T // TM,),
        in_specs=[row, row, pl.BlockSpec((D_MODEL, D_MODEL), lambda i: (0, 0)), _vec_spec()],
        out_specs=[row, row, row, pl.BlockSpec((SUB, D_MODEL), lambda i: (0, 0))],
        out_shape=[jax.ShapeDtypeStruct((T, D_MODEL), _BF), jax.ShapeDtypeStruct((T, D_MODEL), F32),
                   jax.ShapeDtypeStruct((T, D_MODEL), F32), jax.ShapeDtypeStruct((SUB, D_MODEL), F32)],
        compiler_params=_cparams(("arbitrary",)),
    )(dx1, z1, w, g)


def _inproj_bwd_call(dres, dp, w):
    T = dres.shape[0]

    def body(dres_ref, dp_ref, w_ref, dx_ref):
        dx_ref[...] = dres_ref[...] + _mm_nt(dp_ref[...], w_ref[...])

    row = pl.BlockSpec((TM, D_MODEL), lambda i: (i, 0))
    return pl.pallas_call(
        body, name="inproj_bwd", grid=(T // TM,),
        in_specs=[row, pl.BlockSpec((TM, IN_WIDTH), lambda i: (i, 0)),
                  pl.BlockSpec((D_MODEL, IN_WIDTH), lambda i: (0, 0), pipeline_mode=pl.Buffered(1))],
        out_specs=row,
        out_shape=jax.ShapeDtypeStruct((T, D_MODEL), F32),
        compiler_params=_cparams(("parallel",)),
    )(dres, dp, w)


def _tn_cols_call(a, b, name):
    T, ka = a.shape
    n = b.shape[1]

    def body(a_ref, b_ref, o_ref):
        o_ref[...] = _mm_tn(a_ref[...], b_ref[...]).astype(o_ref.dtype)

    return pl.pallas_call(
        body, name=name, grid=(n // N_BLK,),
        in_specs=[pl.BlockSpec((T, ka), lambda j: (0, 0), pipeline_mode=pl.Buffered(1)),
                  pl.BlockSpec((T, N_BLK), lambda j: (0, j))],
        out_specs=pl.BlockSpec((ka, N_BLK), lambda j: (0, j)),
        out_shape=jax.ShapeDtypeStruct((ka, n), _BF),
        compiler_params=_cparams(("parallel",)),
    )(a, b)


def _tn_rows_call(a, b, kb, name):
    T, ka = a.shape
    n = b.shape[1]

    def body(a_ref, b_ref, o_ref):
        o_ref[...] = _mm_tn(a_ref[...], b_ref[...]).astype(o_ref.dtype)

    return pl.pallas_call(
        body, name=name, grid=(ka // kb,),
        in_specs=[pl.BlockSpec((T, kb), lambda r: (0, r)),
                  pl.BlockSpec((T, n), lambda r: (0, 0), pipeline_mode=pl.Buffered(1))],
        out_specs=pl.BlockSpec((kb, n), lambda r: (r, 0)),
        out_shape=jax.ShapeDtypeStruct((ka, n), _BF),
        compiler_params=_cparams(("parallel",)),
    )(a, b)


def _me():
    return lax.axis_index("x"), lax.axis_index("y"), lax.axis_index("c")


def _flip(k):
    x, y, c = _me()
    return (1 - x if k & 4 else x, 1 - y if k & 2 else y, 1 - c if k & 1 else c)


def _dev_index(pos):
    return 4 * pos[0] + 2 * pos[1] + pos[2]


_HBM = pl.BlockSpec(memory_space=pltpu.HBM)
_SEM = pl.BlockSpec(memory_space=pltpu.SEMAPHORE)


def _land_shape(shape, mode):
    if mode == "all":
        return (N_DEV,) + shape
    if mode == "cols":
        return (shape[0], N_DEV * shape[1])
    if mode == "blk":
        return shape
    assert mode == "scols"
    return (N_DEV, shape[0], shape[1] // N_DEV)


def _comm_copies(ins, lands, modes, send_sems, recv_sems):
    me = _dev_index(_me())
    copies = []
    for k in range(N_DEV):
        peer = _flip(k)
        pidx = _dev_index(peer)
        for a, (src, land, mode) in enumerate(zip(ins, lands, modes)):
            if mode == "blk":
                src = src.at[pidx]
            elif mode == "scols":
                w = src.shape[1] // N_DEV
                src = src.at[:, pl.ds(pl.multiple_of(pidx * w, LANE), w)]
            if mode == "cols":
                w = src.shape[1]
                dst = land.at[:, pl.ds(pl.multiple_of(me * w, LANE), w)]
            else:
                dst = land.at[me]
            copies.append(pltpu.make_async_remote_copy(
                src_ref=src, dst_ref=dst, send_sem=send_sems.at[k * len(ins) + a], recv_sem=recv_sems.at[k * len(ins) + a],
                device_id=peer, device_id_type=MESH))
    return copies


def _comm_start_call(arrs, gather_flags, after, name):
    n = len(arrs)
    lands = [lax.empty(_land_shape(v.shape, mode), v.dtype) for v, mode in zip(arrs, gather_flags)]

    def body(*refs):
        ins, lnd = refs[:n], refs[n:2 * n]
        send_sems, recv_sems = refs[2 * n + len(after)], refs[2 * n + len(after) + 1]
        for cp in _comm_copies(ins, lnd, gather_flags, send_sems, recv_sems):
            cp.start()
        refs[-1][...] = jnp.zeros_like(refs[-1])

    hbm = [pltpu.with_memory_space_constraint(v, pltpu.HBM) for v in list(arrs) + lands]
    out = pl.pallas_call(
        body, name=name,
        out_shape=(pltpu.SemaphoreType.DMA((N_DEV * n,)), pltpu.SemaphoreType.DMA((N_DEV * n,)),
                   *[pltpu.HBM(v.shape, v.dtype) for v in hbm], jax.ShapeDtypeStruct((SUB, LANE), F32)),
        in_specs=[_HBM] * (2 * n) + [pl.BlockSpec(memory_space=pl.ANY)] * len(after),
        out_specs=(_SEM, _SEM, *[_HBM] * (2 * n), pl.BlockSpec(memory_space=pltpu.VMEM)),
        input_output_aliases={i: 2 + i for i in range(2 * n)},
        compiler_params=pltpu.CompilerParams(has_side_effects=pltpu.SideEffectType.DATAFLOW_SIDE_EFFECTING),
    )(*hbm, *after)
    return out[:-1], out[-1]


def _comm_wait_call(state, gather_flags, after, name):
    n = len(gather_flags)
    send_sems, recv_sems, thru = state[0], state[1], state[2:]

    def body(*refs):
        ins, lnd, ssem, rsem = refs[:n], refs[n:2 * n], refs[2 * n], refs[2 * n + 1]
        for cp in _comm_copies(ins, lnd, gather_flags, ssem, rsem):
            cp.wait_send()
            cp.wait_recv()

    out = pl.pallas_call(
        body, name=name,
        out_shape=tuple(pltpu.HBM(v.shape, v.dtype) for v in thru),
        in_specs=[_HBM] * (2 * n) + [_SEM, _SEM] + [pl.BlockSpec(memory_space=pl.ANY)] * len(after),
        out_specs=tuple([_HBM] * (2 * n)),
        input_output_aliases={i: i for i in range(2 * n)},
        compiler_params=pltpu.CompilerParams(has_side_effects=pltpu.SideEffectType.DATAFLOW_SIDE_EFFECTING),
    )(*thru, send_sems, recv_sems, *after)
    return out[n:]


def _sum8_call(recv, stacked, layer, nl, rows, r_out, c_out, name):
    c = recv.shape[2]

    def body(x_ref, *rest):
        o_ref = rest[-1]
        acc = x_ref[0, :, :c_out].astype(F32)
        for s in range(1, N_DEV):
            acc = acc + x_ref[s, :, :c_out].astype(F32)
        o_ref[...] = acc

    prev = [] if stacked is None else [stacked]
    return pl.pallas_call(
        body, name=name, grid=(r_out // rows,),
        in_specs=[pl.BlockSpec((N_DEV, rows, c), lambda i: (0, i, 0))] + [pl.BlockSpec(memory_space=pl.ANY)] * len(prev),
        out_specs=pl.BlockSpec((None, rows, c_out), lambda i: (layer, i, 0)),
        out_shape=jax.ShapeDtypeStruct((nl, r_out, c_out), F32),
        input_output_aliases={1: 0} if prev else {},
        compiler_params=_cparams(("parallel",)),
    )(recv, *prev)


def _adamw_call(w, g, m, v, rows, name):
    r, c = w.shape

    def body(w_ref, g_ref, m_ref, v_ref, d_ref, nm_ref, nv_ref):
        gr = g_ref[...]
        nm = ADAM_B1 * m_ref[...] + (1.0 - ADAM_B1) * gr
        nv = ADAM_B2 * v_ref[...] + (1.0 - ADAM_B2) * (gr * gr)
        m_hat = nm / (1.0 - ADAM_B1 ** ADAM_STEP)
        v_hat = nv / (1.0 - ADAM_B2 ** ADAM_STEP)
        d_ref[...] = -ADAM_LR * (m_hat / (jnp.sqrt(v_hat) + ADAM_EPS) + ADAM_WD * w_ref[...])
        nm_ref[...] = nm
        nv_ref[...] = nv

    spec = pl.BlockSpec((rows, c), lambda i: (i, 0))
    return pl.pallas_call(
        body, name=name, grid=(r // rows,),
        in_specs=[spec] * 4, out_specs=[spec] * 3,
        out_shape=[jax.ShapeDtypeStruct((r, c), F32)] * 3,
        compiler_params=_cparams(("parallel",)),
    )(w, g, m, v)


SH_ROWS = 16
SH_W = LRU_WIDTH // N_DEV
REP_ROWS = 824
_REP_SIZES = (LRU_WIDTH, 2 * 6 * 64 * 64, 2 * 6 * 64 * 64, RET_WIDTH, 1920, D_MODEL, D_MODEL, D_MODEL, D_MODEL)
_RPB_SIZE = NA_HEADS * (2 * NA_KH - 1) * (2 * NA_KW - 1)


def _pack_sh(cw, ba, bx, lam):
    return jnp.concatenate([cw, ba, bx, lam], axis=0)


def _pad_sh(p):
    pad = [(0, 0)] * (p.ndim - 2) + [(0, SH_ROWS - p.shape[-2]), (0, LANE - p.shape[-1])]
    return jnp.pad(p, pad)


def _pack_rep(cb, wa, wx, gnw, rpb, l1g, l1b, l2g, l2b):
    flat = jnp.concatenate([cb.reshape(-1), wa.reshape(-1), wx.reshape(-1), gnw.reshape(-1),
                            jnp.pad(rpb.reshape(-1), (0, 1920 - _RPB_SIZE)), l1g, l1b, l2g, l2b,
                            jnp.zeros((REP_ROWS * LANE - sum(_REP_SIZES),), F32)])
    return flat.reshape(REP_ROWS, LANE)


def _unpack_rep(p):
    nl = p.shape[0]
    flat = p.reshape(nl, -1)
    out, off = [], 0
    for size in _REP_SIZES:
        out.append(flat[:, off:off + size])
        off += size
    cb, wa, wx, gnw, rpb, l1g, l1b, l2g, l2b = out
    return (cb, wa.reshape(nl, 2, 6, 64, 64), wx.reshape(nl, 2, 6, 64, 64), gnw,
            rpb[:, :_RPB_SIZE].reshape(nl, NA_HEADS, 2 * NA_KH - 1, 2 * NA_KW - 1), l1g, l1b, l2g, l2b)


def _adamw_nd(w, g, m, v, rows, name):
    shp = w.shape
    f = lambda t: t.reshape(-1, shp[-1])
    rows = f(w).shape[0] if rows is None else rows
    return [t.reshape(shp) for t in _adamw_call(f(w), f(g), f(m), f(v), rows, name)]


def kernel(x, w_in, conv_w, conv_b, lru_w_a, lru_b_a, lru_w_x, lru_b_x, lru_lam, ret_gn_w, na_rpb, w_out, ln1_g, ln1_b, w_gate, w_up, w_down, ln2_g, ln2_b, loss_target, m_w_in, m_conv_w, m_conv_b, m_lru_w_a, m_lru_b_a, m_lru_w_x, m_lru_b_x, m_lru_lam, m_ret_gn_w, m_na_rpb, m_w_out, m_ln1_g, m_ln1_b, m_w_gate, m_w_up, m_w_down, m_ln2_g, m_ln2_b, v_w_in, v_conv_w, v_conv_b, v_lru_w_a, v_lru_b_a, v_lru_w_x, v_lru_b_x, v_lru_lam, v_ret_gn_w, v_na_rpb, v_w_out, v_ln1_g, v_ln1_b, v_w_gate, v_w_up, v_w_down, v_ln2_g, v_ln2_b):
    nl = w_in.shape[0]
    T = x.shape[1]
    rows_n = T // GRID_W
    x0, target = x[0], loss_target[0]
    ffpad = W_BLK - FF_BLK

    win_b = w_in.astype(_BF)
    wg_b = jnp.pad(w_gate, ((0, 0), (0, 0), (0, ffpad))).astype(_BF)
    wu_b = jnp.pad(w_up, ((0, 0), (0, 0), (0, ffpad))).astype(_BF)
    wd_b = jnp.pad(w_down, ((0, 0), (0, ffpad), (0, 0))).astype(_BF)
    wout_b = w_out.astype(_BF)
    def agf_start(l, after):
        sh = _pad_sh(_pack_sh(conv_w[l], lru_b_a[l], lru_b_x[l], lru_lam[l]))
        arrs, modes = [win_b[l], sh], ["cols", "all"]
        if l > 0:
            arrs, modes = arrs + [wd_b[l]], modes + ["all"]
        return _comm_start_call(arrs, modes, after, f"agf_start{l}"), modes

    def agk_start(l, after):
        arrs, modes = [wg_b[l], wu_b[l], wout_b[l]], ["cols", "cols", "all"]
        if l == 0:
            arrs, modes = arrs + [wd_b[l]], modes + ["all"]
        return _comm_start_call(arrs, modes, after, f"agk_start{l}"), modes

    tables = _ret_tables(T)
    w4_all = _lru_w4(lru_w_a, lru_w_x)
    layers = []
    gathered = []
    xs, xb = x0, x0.astype(_BF)
    (agf_state, token), agf_modes = agf_start(0, [])
    tie = 0.0 * token[0, 0]
    btabs = [_na_bias_tables(na_rpb[l] + tie, rows_n) for l in range(nl)]
    for l in range(nl):
        front = _comm_wait_call(agf_state, agf_modes, [xb] + (btabs if l == 0 else []), f"agf_wait{l}")
        win, shg = front[0], front[1]
        (agk_state, token), agk_modes = agk_start(l, [shg])
        full = shg[:, :10, :SH_W].transpose(1, 0, 2).reshape(10, LRU_WIDTH)
        vec, w4 = _lru_vec(full[0:4], conv_b[l], full[4:6], full[6:8], full[8:10]), w4_all[l]
        gnw8 = jnp.pad(ret_gn_w[l][None], ((0, SUB - 1), (0, 0)))
        btab = btabs[l]
        proj = _inproj_call(xb, win, token)
        y_lru = _lru_fwd_call(proj, vec, w4)
        y_ret = _ret_fwd_call(proj, tables, gnw8)
        y_na = _na_fwd_call(proj, btab)
        back = _comm_wait_call(agk_state, agk_modes, [y_na], f"agk_wait{l}")
        wg, wu, wout = back[0], back[1], back[2]
        wd = (back[3] if l == 0 else front[2]).reshape(IN_WIDTH, D_MODEL)
        wout = wout.reshape(D_MODEL, D_MODEL)
        gathered.append((win, wg, wu, wd, wout))
        if l + 1 < nl:
            (agf_state, token), agf_modes = agf_start(l + 1, [wout])
        z1, x1, x1b, ycb = _outproj_ln_call(y_lru, y_ret, y_na, xs, wout, ln1_g[l][None], ln1_b[l][None], token)
        z2, x2, x2b, gpb, upb = _ffn_ln_call(x1, x1b, wg, wu, wd, ln2_g[l][None], ln2_b[l][None])
        layers.append(dict(xb=xb, proj=proj, vec=vec, w4=w4, gnw8=gnw8, btab=btab,
                           z1=z1, x1b=x1b, ycb=ycb, z2=z2, gpb=gpb, upb=upb))
        xs, xb = x2, x2b

    dx, loss_blk = _loss_call(xs, target)
    loss = lax.psum(loss_blk[0, 0], ("x", "y", "c"))

    gxa_flags = ["scols", "scols", "blk", "blk"]
    gxb_flags = ["scols", "blk", "all"]
    gxa_state, gxb_state = [None] * nl, [None] * nl
    token = loss_blk
    for l in reversed(range(nl)):
        s = layers[l]
        win, wg, wu, wd, wout = gathered[l]
        dx1, dgp, dup, hid, dz2b, dln2 = _ffn_bwd_call(dx, s["z2"], s["gpb"], s["upb"], wg, wu, wd, ln2_g[l][None], token)
        dwg = _tn_cols_call(s["x1b"], dgp, "tn_cols")
        dwu = _tn_cols_call(s["x1b"], dup, "tn_cols")
        dwd = _tn_rows_call(hid, dz2b, N_BLK, "tn_rows_down").reshape(N_DEV, W_BLK, D_MODEL)
        dz1b, dyc, dres, dln1 = _outproj_bwd_call(dx1, s["z1"], wout, ln1_g[l][None])
        dwout = _tn_rows_call(s["ycb"], dz1b, D_MODEL // 2, "tn_rows_out").reshape(N_DEV, LANE, D_MODEL)
        gxa_state[l], token = _comm_start_call([dwg, dwu, dwd, dwout], gxa_flags, [], f"gxa_start{l}")
        dp, dvec, dw4 = _lru_bwd_call(s["proj"], dyc, s["vec"], s["w4"], token)
        dp, dgnw = _ret_bwd_call(s["proj"], dyc, tables, s["gnw8"], dp)
        dp, dbias = _na_bwd_call(s["proj"], dyc, s["btab"], dp)
        dwin = _tn_cols_call(s["xb"], dp, "tn_cols")
        dx = _inproj_bwd_call(dres, dp, win)
        dcw, dcb, dwa, dba, dwx, dbx, dlam = _lru_unpack(dvec, dw4)
        rep = _pack_rep(dcb, dwa, dwx, dgnw[0], _na_bias_grad(dbias, rows_n), dln1[0], dln1[1], dln2[0], dln2[1])
        sh = _pack_sh(dcw, dba, dbx, dlam).reshape(10, N_DEV, SH_W).transpose(1, 0, 2)
        gxb_state[l], token = _comm_start_call([dwin, _pad_sh(sh), rep], gxb_flags, [], f"gxb_start{l}")

    g_w_in = g_w_gate = g_w_up = g_w_down = g_w_out = g_shp = g_repp = None
    after = [dx, token]
    big = {}
    for l in reversed(range(nl)):
        ra = _comm_wait_call(gxa_state[l], gxa_flags, after, f"gxa_wait{l}")
        g_w_gate = _sum8_call(ra[0], g_w_gate, l, nl, TM, D_MODEL, FF_BLK, "sum8_ff")
        g_w_up = _sum8_call(ra[1], g_w_up, l, nl, TM, D_MODEL, FF_BLK, "sum8_ff")
        g_w_down = _sum8_call(ra[2], g_w_down, l, nl, FF_BLK, FF_BLK, D_MODEL, "sum8_down")
        g_w_out = _sum8_call(ra[3], g_w_out, l, nl, LANE, LANE, D_MODEL, "sum8_out")
        after = [g_w_out]
        if l == 0:
            big["w_gate"] = _adamw_nd(w_gate, g_w_gate, m_w_gate, v_w_gate, TM, "adamw_ff")
            big["w_up"] = _adamw_nd(w_up, g_w_up, m_w_up, v_w_up, TM, "adamw_ff")
            big["w_down"] = _adamw_nd(w_down, g_w_down, m_w_down, v_w_down, FF_BLK, "adamw_down")
            big["w_out"] = _adamw_nd(w_out, g_w_out, m_w_out, v_w_out, LANE, "adamw_out")
            after = [big[n][k] for n in ("w_gate", "w_up", "w_down", "w_out") for k in range(3)]
        rb = _comm_wait_call(gxb_state[l], gxb_flags, after, f"gxb_wait{l}")
        g_w_in = _sum8_call(rb[0], g_w_in, l, nl, TM, D_MODEL, W_BLK, "sum8_in")
        g_shp = _sum8_call(rb[1], g_shp, l, nl, SH_ROWS, SH_ROWS, LANE, "sum8_sh")
        g_repp = _sum8_call(rb[2], g_repp, l, nl, REP_ROWS, REP_ROWS, LANE, "sum8_rep")
        after = [g_repp]

    big["w_in"] = _adamw_nd(w_in, g_w_in, m_w_in, v_w_in, TM, "adamw_in")
    g_shp = g_shp[:, :, :SH_W]
    rep_names = ("conv_b", "lru_w_a", "lru_w_x", "ret_gn_w", "na_rpb", "ln1_g", "ln1_b", "ln2_g", "ln2_b")
    grads = {"w_in": g_w_in, "w_gate": g_w_gate, "w_up": g_w_up, "w_down": g_w_down, "w_out": g_w_out,
             "conv_w": g_shp[:, 0:4], "lru_b_a": g_shp[:, 4:6], "lru_b_x": g_shp[:, 6:8], "lru_lam": g_shp[:, 8:10]}
    grads.update(dict(zip(rep_names, _unpack_rep(g_repp))))
    small = {
        "conv_w": (conv_w, m_conv_w, v_conv_w), "conv_b": (conv_b, m_conv_b, v_conv_b),
        "lru_w_a": (lru_w_a, m_lru_w_a, v_lru_w_a), "lru_b_a": (lru_b_a, m_lru_b_a, v_lru_b_a),
        "lru_w_x": (lru_w_x, m_lru_w_x, v_lru_w_x), "lru_b_x": (lru_b_x, m_lru_b_x, v_lru_b_x),
        "lru_lam": (lru_lam, m_lru_lam, v_lru_lam), "ret_gn_w": (ret_gn_w, m_ret_gn_w, v_ret_gn_w),
        "na_rpb": (na_rpb, m_na_rpb, v_na_rpb), "ln1_g": (ln1_g, m_ln1_g, v_ln1_g), "ln1_b": (ln1_b, m_ln1_b, v_ln1_b),
        "ln2_g": (ln2_g, m_ln2_g, v_ln2_g), "ln2_b": (ln2_b, m_ln2_b, v_ln2_b),
    }
    for name, (w_, m_, v_) in small.items():
        big[name] = _adamw_nd(w_, grads[name], m_, v_, None, "adamw_small")
    kinds = [{n: big[n][k] for n in big} for k in range(3)]
    order = ("w_in", "conv_w", "conv_b", "lru_w_a", "lru_b_a", "lru_w_x", "lru_b_x", "lru_lam", "ret_gn_w", "na_rpb",
             "w_out", "ln1_g", "ln1_b", "w_gate", "w_up", "w_down", "ln2_g", "ln2_b")
    outs = [loss, dx[None]]
    for d in (grads, *kinds):
        outs.extend(d[n] for n in order)
    return tuple(outs)
```

```python
import functools
import math

import numpy as np
import jax
import jax.numpy as jnp
from jax import lax
from jax.experimental import pallas as pl
from jax.experimental.pallas import tpu as pltpu

F32 = jnp.float32
_BF = jnp.bfloat16

D_MODEL = 1024
DEPTH = 4
GRID_W = 64
HEAD_DIM = 64
LRU_WIDTH = 384
RET_WIDTH = 384
RET_HEADS = 6
NA_WIDTH = 256
NA_HEADS = 4
IN_WIDTH = 3072
CONV_WIDTH = 4
LRU_C = 8.0
RET_CHUNK = 128
ROPE_BASE = 10000.0
GN_EPS = 1e-6
NA_KH = 8
NA_KW = 16
D_FF = 2816
FF_BLK = 352
N_DEV = 8
ALPHA = (2 * DEPTH) ** 0.25
LN_EPS = 1e-5
ADAM_LR = 0.001
ADAM_B1 = 0.9
ADAM_B2 = 0.999
ADAM_EPS = 1e-08
ADAM_WD = 0.01
ADAM_STEP = 10

LANE = 128
SUB = 8
VMEM_MB = 56
NEG = -1e30

MESH = pl.DeviceIdType.MESH


def _cparams(sem=None, vmem_mb=VMEM_MB):
    return pltpu.CompilerParams(dimension_semantics=sem, vmem_limit_bytes=vmem_mb << 20)


def _mm(a, b):
    return jnp.dot(a.astype(_BF), b.astype(_BF), preferred_element_type=F32)


def _mm_nt(a, b):
    return lax.dot_general(a.astype(_BF), b.astype(_BF), (((1,), (1,)), ((), ())), preferred_element_type=F32)


def _mm_tn(a, b):
    return lax.dot_general(a.astype(_BF), b.astype(_BF), (((0,), (0,)), ((), ())), preferred_element_type=F32)


def _sigmoid(x):
    return jax.nn.sigmoid(x)


def _rows(start, size):
    return pl.ds(pl.multiple_of(start, SUB), size)


def _loop2(n, body, init):
    assert n % 2 == 0
    return lax.fori_loop(0, n // 2, lambda i, c: body(2 * i + 1, body(2 * i, c)), init)


def _strip(T, col, buffers=2):
    return pl.BlockSpec((T, LANE), lambda j: (0, col(j)), pipeline_mode=pl.Buffered(buffers))


LRU_CH = 1024
_GELU_C0 = math.sqrt(2.0 / math.pi)
_GELU_C1 = 0.044715


def _gelu_parts(x):
    x2 = x * x
    t = jnp.tanh(_GELU_C0 * (x + _GELU_C1 * x * x2))
    val = 0.5 * x * (1.0 + t)
    der = 0.5 * (1.0 + t) + 0.5 * x * (1.0 - t * t) * _GELU_C0 * (1.0 + 3.0 * _GELU_C1 * x2)
    return val, der


def _softplus_neg(lam):
    e = jnp.exp(-jnp.abs(lam))
    w = 1.0 + e
    l1p = jnp.where(w == 1.0, e, jnp.log(w) * (e / jnp.where(w == 1.0, 1.0, w - 1.0)))
    return jnp.maximum(-lam, 0.0) + l1p


def _window(ref, t0, ch, T):
    prev = ref[_rows(jnp.maximum(t0 - SUB, 0), SUB), :].astype(F32)
    nxt = ref[_rows(jnp.minimum(t0 + ch, T - SUB), SUB), :].astype(F32)
    prev = jnp.where(t0 > 0, prev, 0.0)
    nxt = jnp.where(t0 + ch < T, nxt, 0.0)
    return jnp.concatenate([prev, ref[_rows(t0, ch), :].astype(F32), nxt], axis=0)


def _tap(win, shift, ch):
    n = win.shape[0]
    return pltpu.roll(win, (-shift) % n, 0)[SUB:SUB + ch]


def _lru_conv(xb_ref, vec, t0, T):
    win = _window(xb_ref, t0, LRU_CH, T)
    xc = jnp.broadcast_to(vec[4:5, :], (LRU_CH, LANE))
    for j in range(CONV_WIDTH):
        xc = xc + _tap(win, j - CONV_WIDTH // 2, LRU_CH) * vec[j:j + 1, :]
    return xc


def _lru_dir(pre_a, pre_x, sp):
    r = _sigmoid(pre_a)
    i = _sigmoid(pre_x)
    log_a = (-LRU_C) * r * sp
    a = jnp.exp(log_a)
    z = jnp.tanh(-log_a) * (a * a + 1.0)
    s = jnp.sqrt(z)
    return r, i, a, s


def _scan_tile(a, b, reverse, row):
    for k in (1, 2, 4):
        if not reverse:
            a_s, b_s, m = pltpu.roll(a, k, 0), pltpu.roll(b, k, 0), row >= k
        else:
            a_s, b_s, m = pltpu.roll(a, SUB - k, 0), pltpu.roll(b, SUB - k, 0), row < SUB - k
        b = jnp.where(m, a * b_s + b, b)
        a = jnp.where(m, a * a_s, a)
    return a, b


def _bcast_row(x, r):
    return jnp.broadcast_to(x[r:r + 1, :], (SUB, LANE))


def _lru_prepare(xb_ref, w4_ref, vec, xc_ref, af_ref, uf_ref, ab_ref, ub_ref, T):
    sp_f = _softplus_neg(vec[9:10, :])
    sp_b = _softplus_neg(vec[10:11, :])
    w4 = w4_ref[0]

    def body(c, carry):
        t0 = c * LRU_CH
        xc = _lru_conv(xb_ref, vec, t0, T)
        if xc_ref is not None:
            xc_ref[_rows(t0, LRU_CH), :] = xc
        pre = _mm(xc, w4)
        _, i, a, s = _lru_dir(pre[:, 0:128] + vec[5:6, :], pre[:, 128:256] + vec[6:7, :], sp_f)
        af_ref[_rows(t0, LRU_CH), :] = a
        uf_ref[_rows(t0, LRU_CH), :] = s * (i * xc)
        _, i, a, s = _lru_dir(pre[:, 256:384] + vec[7:8, :], pre[:, 384:512] + vec[8:9, :], sp_b)
        ab_ref[_rows(t0, LRU_CH), :] = a
        ub_ref[_rows(t0, LRU_CH), :] = s * (i * xc)
        return carry

    lax.fori_loop(0, T // LRU_CH, body, 0)


def _lru_scan(af_ref, uf_ref, ab_ref, ub_ref, T):
    nt = T // SUB
    row = lax.broadcasted_iota(jnp.int32, (SUB, LANE), 0)

    def body(j, carry):
        hf, hb = carry
        sf = _rows(j * SUB, SUB)
        sb = _rows((nt - 1 - j) * SUB, SUB)
        a, b = _scan_tile(af_ref[sf, :], uf_ref[sf, :], False, row)
        h = a * hf + b
        uf_ref[sf, :] = h
        hf = _bcast_row(h, SUB - 1)
        a, b = _scan_tile(ab_ref[sb, :], ub_ref[sb, :], True, row)
        h = a * hb + b
        ub_ref[sb, :] = h
        hb = _bcast_row(h, 0)
        return hf, hb

    z = jnp.zeros((SUB, LANE), F32)
    lax.fori_loop(0, nt, body, (z, z))


def _lru_fwd_call(proj, vec, w4):
    T = proj.shape[0]

    def body(xb_ref, gate_ref, vec_ref, w4_ref, y_ref, af_ref, uf_ref, ab_ref, ub_ref):
        vec = vec_ref[...]
        _lru_prepare(xb_ref, w4_ref, vec, None, af_ref, uf_ref, ab_ref, ub_ref, T)
        _lru_scan(af_ref, uf_ref, ab_ref, ub_ref, T)

        def out(c, carry):
            rows = _rows(c * LRU_CH, LRU_CH)
            gl, _ = _gelu_parts(gate_ref[rows, :])
            y_ref[rows, :] = (uf_ref[rows, :] + ub_ref[rows, :]) * gl
            return carry

        lax.fori_loop(0, T // LRU_CH, out, 0)

    return pl.pallas_call(
        body, name="lru_fwd", grid=(LRU_WIDTH // LANE,),
        in_specs=[_strip(T, lambda j: j), _strip(T, lambda j: j + 3),
                  pl.BlockSpec((16, LANE), lambda j: (0, j)),
                  pl.BlockSpec((1, LANE, 4 * LANE), lambda j: (j, 0, 0))],
        out_specs=_strip(T, lambda j: j, buffers=1),
        out_shape=jax.ShapeDtypeStruct((T, LRU_WIDTH), F32),
        scratch_shapes=[pltpu.VMEM((T, LANE), F32)] * 4,
        compiler_params=_cparams(("arbitrary",)),
    )(proj, proj, vec, w4)


def _store_strips(stage_ref, dp_ref, cols, sems):
    copies = [pltpu.make_async_copy(stage_ref.at[b], dp_ref.at[:, pl.ds(pl.multiple_of(c * LANE, LANE), LANE)], sems.at[b])
              for b, c in enumerate(cols)]
    for cp in copies:
        cp.start()
    for cp in copies:
        cp.wait()


def _lru_bwd_call(proj, dycat, vec, w4, after):
    T = proj.shape[0]
    nt = T // SUB
    nch = T // LRU_CH

    def body(xb_ref, gate_ref, dy_ref, vec_ref, w4_ref, after_ref, dp_ref, dvec_ref, dw4_ref,
             xc_ref, af_ref, hf_ref, ab_ref, hb_ref, dh_ref, stage_ref, sems):
        dxb_ref, dgate_ref = stage_ref.at[0], stage_ref.at[1]
        vec = vec_ref[...]
        _lru_prepare(xb_ref, w4_ref, vec, xc_ref, af_ref, hf_ref, ab_ref, hb_ref, T)
        _lru_scan(af_ref, hf_ref, ab_ref, hb_ref, T)

        def gate_bwd(c, carry):
            rows = _rows(c * LRU_CH, LRU_CH)
            gl, dgl = _gelu_parts(gate_ref[rows, :])
            dy = dy_ref[rows, :]
            dgate_ref[rows, :] = (dy * (hf_ref[rows, :] + hb_ref[rows, :]) * dgl).astype(dgate_ref.dtype)
            dh_ref[rows, :] = dy * gl
            return carry

        lax.fori_loop(0, nch, gate_bwd, 0)

        row = lax.broadcasted_iota(jnp.int32, (SUB, LANE), 0)

        def adj(j, carry):
            gf, a_next, gb, a_prev = carry
            tf = nt - 1 - j
            sf = _rows(tf * SUB, SUB)
            a_t = af_ref[sf, :]
            h_t = hf_ref[sf, :]
            coef = jnp.where(row == SUB - 1, a_next, pltpu.roll(a_t, SUB - 1, 0))
            ac, bc = _scan_tile(coef, dh_ref[sf, :], True, row)
            g = ac * gf + bc
            h_prev = hf_ref[_rows(jnp.maximum(tf - 1, 0) * SUB, SUB), :]
            h_prev = jnp.where(tf > 0, _bcast_row(h_prev, SUB - 1), 0.0)
            hs = jnp.where(row == 0, h_prev, pltpu.roll(h_t, 1, 0))
            af_ref[sf, :] = g * hs
            hf_ref[sf, :] = g
            gf = _bcast_row(g, 0)
            a_next = _bcast_row(a_t, 0)
            sb = _rows(j * SUB, SUB)
            a_t = ab_ref[sb, :]
            h_t = hb_ref[sb, :]
            coef = jnp.where(row == 0, a_prev, pltpu.roll(a_t, 1, 0))
            ac, bc = _scan_tile(coef, dh_ref[sb, :], False, row)
            g = ac * gb + bc
            h_next = hb_ref[_rows(jnp.minimum(j + 1, nt - 1) * SUB, SUB), :]
            h_next = jnp.where(j < nt - 1, _bcast_row(h_next, 0), 0.0)
            hs = jnp.where(row == SUB - 1, h_next, pltpu.roll(h_t, SUB - 1, 0))
            ab_ref[sb, :] = g * hs
            hb_ref[sb, :] = g
            gb = _bcast_row(g, SUB - 1)
            a_prev = _bcast_row(a_t, SUB - 1)
            return gf, a_next, gb, a_prev

        z = jnp.zeros((SUB, LANE), F32)
        lax.fori_loop(0, nt, adj, (z, z, z, z))

        sp_f = _softplus_neg(vec[9:10, :])
        sp_b = _softplus_neg(vec[10:11, :])
        w4 = w4_ref[0]
        dw4_ref[...] = jnp.zeros_like(dw4_ref)

        def one_dir(pre_a, pre_x, sp, xc, du, da):
            r, i, a, s = _lru_dir(pre_a, pre_x, sp)
            d_i = du * s * xc
            dxc = du * s * i
            d_s = du * i * xc
            d_log = da * a - d_s * (a * a) / s
            d_r = d_log * (-LRU_C) * sp
            d_sp = jnp.sum(d_log * (-LRU_C) * r, axis=0, keepdims=True)
            return d_r * r * (1.0 - r), d_i * i * (1.0 - i), dxc, d_sp

        def gates_bwd(c, carry):
            db, dspf, dspb = carry
            rows = _rows(c * LRU_CH, LRU_CH)
            xc = xc_ref[rows, :]
            pre = _mm(xc, w4)
            dpa_f, dpx_f, dxc_f, d_sp_f = one_dir(pre[:, 0:128] + vec[5:6, :], pre[:, 128:256] + vec[6:7, :],
                                                  sp_f, xc, hf_ref[rows, :], af_ref[rows, :])
            dpa_b, dpx_b, dxc_b, d_sp_b = one_dir(pre[:, 256:384] + vec[7:8, :], pre[:, 384:512] + vec[8:9, :],
                                                  sp_b, xc, hb_ref[rows, :], ab_ref[rows, :])
            dpre = jnp.concatenate([dpa_f, dpx_f, dpa_b, dpx_b], axis=1)
            dw4_ref[0] += _mm_tn(xc, dpre)
            dh_ref[rows, :] = dxc_f + dxc_b + _mm_nt(dpre, w4)
            return db + jnp.sum(dpre, axis=0, keepdims=True), dspf + d_sp_f, dspb + d_sp_b

        z1 = jnp.zeros((1, LANE), F32)
        db, dspf, dspb = lax.fori_loop(0, nch, gates_bwd, (jnp.zeros((1, 4 * LANE), F32), z1, z1))

        def conv_bwd(c, carry):
            t0 = c * LRU_CH
            rows = _rows(t0, LRU_CH)
            dwin = _window(dh_ref, t0, LRU_CH, T)
            xwin = _window(xb_ref, t0, LRU_CH, T)
            dxc = dh_ref[rows, :]
            dxb = jnp.zeros((LRU_CH, LANE), F32)
            out = []
            for j in range(CONV_WIDTH):
                off = j - CONV_WIDTH // 2
                dxb = dxb + _tap(dwin, -off, LRU_CH) * vec[j:j + 1, :]
                out.append(carry[j] + jnp.sum(dxc * _tap(xwin, off, LRU_CH), axis=0, keepdims=True))
            dxb_ref[rows, :] = dxb.astype(dxb_ref.dtype)
            out.append(carry[CONV_WIDTH] + jnp.sum(dxc, axis=0, keepdims=True))
            return tuple(out)

        dconv = lax.fori_loop(0, nch, conv_bwd, (z1,) * (CONV_WIDTH + 1))
        dlam_f = dspf * (-_sigmoid(-vec[9:10, :]))
        dlam_b = dspb * (-_sigmoid(-vec[10:11, :]))
        dvec_ref[...] = jnp.concatenate(
            list(dconv) + [db[:, 0:128], db[:, 128:256], db[:, 256:384], db[:, 384:512], dlam_f, dlam_b,
                           jnp.zeros((5, LANE), F32)], axis=0)
        j = pl.program_id(0)
        _store_strips(stage_ref, dp_ref, (j, j + 3), sems)

    ns = LRU_WIDTH // LANE
    return pl.pallas_call(
        body, name="lru_bwd", grid=(ns,),
        in_specs=[_strip(T, lambda j: j), _strip(T, lambda j: j + 3), _strip(T, lambda j: j),
                  pl.BlockSpec((16, LANE), lambda j: (0, j)),
                  pl.BlockSpec((1, LANE, 4 * LANE), lambda j: (j, 0, 0)),
                  pl.BlockSpec(memory_space=pl.ANY)],
        out_specs=[pl.BlockSpec(memory_space=pl.ANY),
                   pl.BlockSpec((16, LANE), lambda j: (0, j)),
                   pl.BlockSpec((1, LANE, 4 * LANE), lambda j: (j, 0, 0))],
        out_shape=[jax.ShapeDtypeStruct((T, IN_WIDTH), _BF),
                   jax.ShapeDtypeStruct((16, LRU_WIDTH), F32), jax.ShapeDtypeStruct((ns, LANE, 4 * LANE), F32)],
        scratch_shapes=[pltpu.VMEM((T, LANE), F32)] * 6 + [pltpu.VMEM((2, T, LANE), _BF), pltpu.SemaphoreType.DMA((2,))],
        compiler_params=_cparams(("arbitrary",)),
    )(proj, proj, dycat, vec, w4, after)


def _lru_vec(cw, cb, ba, bx, lam):
    return jnp.concatenate([cw, cb[None], ba[0:1], bx[0:1], ba[1:2], bx[1:2], lam, jnp.zeros((5, LRU_WIDTH), F32)], axis=0)


def _lru_w4(wa, wx):
    nl = wa.shape[0]
    w = jnp.stack([wa[:, 0], wx[:, 0], wa[:, 1], wx[:, 1]], axis=1)
    w = w.reshape(nl, 4, 3, 2, 64, 64)
    eye = jnp.eye(2, dtype=w.dtype)
    bd = w[:, :, :, :, :, None, :] * eye[None, None, None, :, None, :, None]
    bd = bd.reshape(nl, 4, 3, LANE, LANE)
    return bd.transpose(0, 2, 3, 1, 4).reshape(nl, 3, LANE, 4 * LANE).astype(_BF)


def _lru_unpack(dvec, dw4):
    def blocks(m):
        m = m.reshape(3, 2, 64, 2, 64)
        return jnp.stack([m[:, 0, :, 0, :], m[:, 1, :, 1, :]], axis=1).reshape(6, 64, 64)
    parts = [blocks(dw4[:, :, k * LANE:(k + 1) * LANE]) for k in range(4)]
    dwa = jnp.stack([parts[0], parts[2]])
    dwx = jnp.stack([parts[1], parts[3]])
    dba = jnp.stack([dvec[5], dvec[7]])
    dbx = jnp.stack([dvec[6], dvec[8]])
    return dvec[0:4], dvec[4], dwa, dba, dwx, dbx, dvec[9:11]


RC = 2 * RET_CHUNK


def _ret_tables(T):
    half = HEAD_DIM // 2
    pos = jnp.arange(T, dtype=F32)
    inv_freq = ROPE_BASE ** (-jnp.arange(half, dtype=F32) / half)
    ang = pos[:, None] * inv_freq[None, :]
    cos = jnp.tile(jnp.cos(ang), (1, 4))
    sin = jnp.tile(jnp.concatenate([-jnp.sin(ang), jnp.sin(ang)], axis=1), (1, 2))
    log_g = jnp.log1p(-jnp.exp2(-5.0 - jnp.arange(RET_HEADS, dtype=F32)))
    idx = jnp.arange(RC, dtype=F32)
    dec = jnp.exp(jnp.abs(idx[:, None] - idx[None, :]) * log_g[:, None, None])
    lg = jnp.repeat(log_g, HEAD_DIM).reshape(3, 1, LANE)
    col = idx[None, :, None]
    rtab = jnp.stack([jnp.exp((RC - 1 - col) * lg), jnp.exp(col * lg),
                      jnp.exp((col + 1.0) * lg), jnp.exp((RC - col) * lg)], axis=1)
    gch = jnp.broadcast_to(jnp.exp(RC * lg), (3, SUB, LANE))
    return cos, sin, dec, rtab, gch


def _swap32(x, lane):
    return jnp.where((lane & 32) == 0, pltpu.roll(x, LANE - 32, 1), pltpu.roll(x, 32, 1))


def _head_mean(x, m0, m1):
    s0 = jnp.sum(x * m0, axis=-1, keepdims=True)
    s1 = jnp.sum(x * m1, axis=-1, keepdims=True)
    return (s0 * m0 + s1 * m1) * (1.0 / HEAD_DIM)


def _ret_masks():
    lane = lax.broadcasted_iota(jnp.int32, (RC, LANE), 1)
    m0 = (lane < HEAD_DIM).astype(F32)
    r = lax.broadcasted_iota(jnp.int32, (LANE, LANE), 0) // HEAD_DIM
    c = lax.broadcasted_iota(jnp.int32, (LANE, LANE), 1) // HEAD_DIM
    return lane, m0, 1.0 - m0, (r == c).astype(F32)


def _ret_specs(T):
    const = lambda shape, imap: pl.BlockSpec(shape, imap)
    return [_strip(T, lambda j: j + 6), _strip(T, lambda j: j + 9), _strip(T, lambda j: j + 12),
            _strip(T, lambda j: j + 15),
            pl.BlockSpec((T, LANE), lambda j: (0, 0), pipeline_mode=pl.Buffered(1)),
            pl.BlockSpec((T, LANE), lambda j: (0, 0), pipeline_mode=pl.Buffered(1)),
            const((2, RC, RC), lambda j: (j, 0, 0)),
            const((1, 4, RC, LANE), lambda j: (j, 0, 0, 0)),
            const((1, SUB, LANE), lambda j: (j, 0, 0)),
            const((SUB, LANE), lambda j: (0, j))]


def _ret_fwd_call(proj, tables, gnw8):
    T = proj.shape[0]
    nc = T // RC
    cos, sin, dec, rtab, gch = tables

    def body(q_ref, k_ref, v_ref, g_ref, cos_ref, sin_ref, dec_ref, rtab_ref, gch_ref, gnw_ref, y_ref, stf_ref, kr_ref):
        lane, m0, m1, bd = _ret_masks()
        gch_v = gch_ref[0][0:1, :]
        gnw = gnw_ref[0:1, :]
        dkf, dkb, dqf, dqb = rtab_ref[0, 0], rtab_ref[0, 1], rtab_ref[0, 2], rtab_ref[0, 3]

        def rope(x, rows):
            return x * cos_ref[rows, :] + _swap32(x, lane) * sin_ref[rows, :]

        def pass_a(n, st):
            rows = _rows(n * RC, RC)
            stf_ref[n] = st
            kr = rope(k_ref[rows, :], rows) * (HEAD_DIM ** -0.5)
            kr_ref[rows, :] = kr
            return gch_v * st + _mm_tn(kr * dkf, v_ref[rows, :]) * bd

        _loop2(nc, pass_a, jnp.zeros((LANE, LANE), F32))

        def pass_b(i, stb):
            ns = [nc - 1 - 2 * i, nc - 2 - 2 * i]
            rows = [_rows(n * RC, RC) for n in ns]
            heads = ((0, m0), (1, m1))
            qr = [rope(q_ref[r, :], r) for r in rows]
            kr = [kr_ref[r, :] for r in rows]
            v = [v_ref[r, :] for r in rows]
            kv = [_mm_tn(kr[c] * dkb, v[c]) * bd for c in range(2)]
            stbs = [stb, gch_v * stb + kv[0]]
            s = [[_mm_nt(qr[c] * m, kr[c]) * dec_ref[h] for h, m in heads] for c in range(2)]
            o = [_mm(qr[c] * dqf, stf_ref[ns[c]]) + _mm(qr[c] * dqb, stbs[c]) for c in range(2)]
            o = [o[c] + _mm(s[c][0], v[c] * m0) + _mm(s[c][1], v[c] * m1) for c in range(2)]
            oc = [o_ - _head_mean(o_, m0, m1) for o_ in o]
            on = [oc_ * lax.rsqrt(_head_mean(oc_ * oc_, m0, m1) + GN_EPS) for oc_ in oc]
            for c in range(2):
                g = g_ref[rows[c], :]
                y_ref[rows[c], :] = (g * _sigmoid(g)) * (on[c] * gnw)
            return gch_v * stbs[1] + kv[1]

        assert nc % 2 == 0
        lax.fori_loop(0, nc // 2, pass_b, jnp.zeros((LANE, LANE), F32))

    return pl.pallas_call(
        body, name="ret_fwd", grid=(RET_WIDTH // LANE,),
        in_specs=_ret_specs(T),
        out_specs=_strip(T, lambda j: j, buffers=1),
        out_shape=jax.ShapeDtypeStruct((T, RET_WIDTH), F32),
        scratch_shapes=[pltpu.VMEM((nc, LANE, LANE), F32), pltpu.VMEM((T, LANE), F32)],
        compiler_params=_cparams(("arbitrary",)),
    )(proj, proj, proj, proj, cos, sin, dec, rtab, gch, gnw8)


def _ret_bwd_call(proj, dycat, tables, gnw8, dp):
    T = proj.shape[0]
    nc = T // RC
    cos, sin, dec, rtab, gch = tables

    def body(q_ref, k_ref, v_ref, g_ref, cos_ref, sin_ref, dec_ref, rtab_ref, gch_ref, gnw_ref, dy_ref, dp_in_ref,
             dp_out_ref, dgnw_ref, stf_ref, dstb_ref, dkr_ref, dv_ref, dp_ref, kr_ref, sems):
        lane, m0, m1, bd = _ret_masks()
        gch_v = gch_ref[0][0:1, :]
        gnw = gnw_ref[0:1, :]
        dkf, dkb, dqf, dqb = rtab_ref[0, 0], rtab_ref[0, 1], rtab_ref[0, 2], rtab_ref[0, 3]
        scale = HEAD_DIM ** -0.5
        zst = jnp.zeros((LANE, LANE), F32)

        def rope(x, rows):
            return x * cos_ref[rows, :] + _swap32(x, lane) * sin_ref[rows, :]

        def rope_t(d, rows):
            return d * cos_ref[rows, :] + _swap32(d * sin_ref[rows, :], lane)

        def pass_a(n, st):
            rows = _rows(n * RC, RC)
            stf_ref[n] = st
            kr = rope(k_ref[rows, :], rows) * scale
            kr_ref[rows, :] = kr
            return gch_v * st + _mm_tn(kr * dkf, v_ref[rows, :]) * bd

        _loop2(nc, pass_a, zst)

        def pass_b(i, carry):
            stb, d_f, dgnw = carry
            two = range(2)
            heads = ((0, m0), (1, m1))
            ns = [nc - 1 - 2 * i, nc - 2 - 2 * i]
            rows = [_rows(n * RC, RC) for n in ns]
            qr = [rope(q_ref[r, :], r) for r in rows]
            kr = [kr_ref[r, :] for r in rows]
            v = [v_ref[r, :] for r in rows]
            stf = [stf_ref[n] for n in ns]
            kvb = [_mm_tn(kr[c] * dkb, v[c]) * bd for c in two]
            stbs = [stb, gch_v * stb + kvb[0]]
            qf = [qr[c] * dqf for c in two]
            qb = [qr[c] * dqb for c in two]
            s = [[_mm_nt(qr[c] * m, kr[c]) * dec_ref[h] for h, m in heads] for c in two]
            o = [_mm(qf[c], stf[c]) + _mm(qb[c], stbs[c]) for c in two]
            o = [o[c] + _mm(s[c][0], v[c] * m0) + _mm(s[c][1], v[c] * m1) for c in two]
            oc = [o_ - _head_mean(o_, m0, m1) for o_ in o]
            rstd = [lax.rsqrt(_head_mean(oc_ * oc_, m0, m1) + GN_EPS) for oc_ in oc]
            on = [oc[c] * rstd[c] for c in two]
            do = []
            for c in two:
                g = g_ref[rows[c], :]
                sg = _sigmoid(g)
                dy = dy_ref[rows[c], :]
                dp_ref[3, rows[c], :] = (dy * (on[c] * gnw) * (sg * (1.0 + g * (1.0 - sg)))).astype(dp_ref.dtype)
                t = dy * (g * sg)
                dgnw = dgnw + jnp.sum(t * on[c], axis=0, keepdims=True)
                don = t * gnw
                do.append(rstd[c] * (don - _head_mean(don, m0, m1) - on[c] * _head_mean(don * on[c], m0, m1)))
            dstf = [_mm_tn(qf[c], do[c]) * bd for c in two]
            dfs = [d_f, dstf[0] + gch_v * d_f]
            ds = [[_mm_nt(do[c] * m, v[c]) * dec_ref[h] for h, m in heads] for c in two]
            dqr = [_mm_nt(do[c], stf[c]) * dqf + _mm_nt(do[c], stbs[c]) * dqb
                   + _mm(ds[c][0], kr[c] * m0) + _mm(ds[c][1], kr[c] * m1) for c in two]
            dkr = [_mm_nt(v[c], dfs[c]) * dkf + _mm_tn(ds[c][0], qr[c] * m0) + _mm_tn(ds[c][1], qr[c] * m1) for c in two]
            dv = [_mm(kr[c] * dkf, dfs[c]) + _mm_tn(s[c][0], do[c] * m0) + _mm_tn(s[c][1], do[c] * m1) for c in two]
            for c in two:
                dp_ref[0, rows[c], :] = rope_t(dqr[c], rows[c]).astype(dp_ref.dtype)
                dkr_ref[rows[c], :] = dkr[c]
                dv_ref[rows[c], :] = dv[c]
                dstb_ref[ns[c]] = _mm_tn(qb[c], do[c]) * bd
            return gch_v * stbs[1] + kvb[1], dstf[1] + gch_v * dfs[1], dgnw

        assert nc % 2 == 0
        _, _, dgnw = lax.fori_loop(0, nc // 2, pass_b, (zst, zst, jnp.zeros((1, LANE), F32)))
        dgnw_ref[...] = jnp.concatenate([dgnw, jnp.zeros((SUB - 1, LANE), F32)], axis=0)

        def pass_c(n, d_b):
            rows = _rows(n * RC, RC)
            kr = kr_ref[rows, :]
            v = v_ref[rows, :]
            dkr = dkr_ref[rows, :] + _mm_nt(v, d_b) * dkb
            dp_ref[1, rows, :] = (rope_t(dkr, rows) * scale).astype(dp_ref.dtype)
            dp_ref[2, rows, :] = (dv_ref[rows, :] + _mm(kr * dkb, d_b)).astype(dp_ref.dtype)
            return dstb_ref[n] + gch_v * d_b

        _loop2(nc, pass_c, zst)
        j = pl.program_id(0)
        _store_strips(dp_ref, dp_out_ref, (j + 6, j + 9, j + 12, j + 15), sems)

    n_in = len(_ret_specs(T)) + 1
    return pl.pallas_call(
        body, name="ret_bwd", grid=(RET_WIDTH // LANE,),
        in_specs=_ret_specs(T) + [_strip(T, lambda j: j + 3), pl.BlockSpec(memory_space=pl.ANY)],
        out_specs=[pl.BlockSpec(memory_space=pl.ANY), pl.BlockSpec((SUB, LANE), lambda j: (0, j))],
        out_shape=[jax.ShapeDtypeStruct(dp.shape, dp.dtype), jax.ShapeDtypeStruct((SUB, RET_WIDTH), F32)],
        scratch_shapes=[pltpu.VMEM((nc, LANE, LANE), F32), pltpu.VMEM((nc, LANE, LANE), F32),
                        pltpu.VMEM((T, LANE), F32), pltpu.VMEM((T, LANE), F32),
                        pltpu.VMEM((4, T, LANE), _BF), pltpu.VMEM((T, LANE), F32), pltpu.SemaphoreType.DMA((4,))],
        input_output_aliases={n_in: 0},
        compiler_params=_cparams(("arbitrary",)),
    )(proj, proj, proj, proj, cos, sin, dec, rtab, gch, gnw8, dycat, dp)


NA_Q = 2 * GRID_W
NA_WROWS = 10
NA_K = NA_WROWS * GRID_W
NA_CHUNKS = NA_K // LANE
NA_UNROLL = 4
NA_TYPES = 5
_ONEHOT_PRECISION = lax.Precision.HIGH


def _na_onehots(rows_n):
    reps = [(0, 0), (2, 0), (4, 0), (rows_n - 4, rows_n - NA_WROWS), (rows_n - 2, rows_n - NA_WROWS)]
    rm = np.zeros((NA_TYPES, 2, NA_WROWS, 2 * NA_KH - 1), np.float32)
    for t, (r, ws) in enumerate(reps):
        for qh in range(2):
            qrow = r + qh
            rstart = min(max(qrow - NA_KH // 2, 0), rows_n - NA_KH)
            for kh in range(NA_WROWS):
                krow = ws + kh
                if rstart <= krow < rstart + NA_KH:
                    rm[t, qh, kh, krow - qrow + NA_KH - 1] = 1.0
    cm = np.zeros((GRID_W, GRID_W, 2 * NA_KW - 1), np.float32)
    for qc in range(GRID_W):
        cstart = min(max(qc - NA_KW // 2, 0), GRID_W - NA_KW)
        for kc in range(cstart, cstart + NA_KW):
            cm[qc, kc, kc - qc + NA_KW - 1] = 1.0
    rm2 = rm.reshape(NA_TYPES, 2, NA_CHUNKS, 2, 2 * NA_KH - 1)
    cm2 = np.zeros((GRID_W, LANE, 2, 2 * NA_KW - 1), np.float32)
    for z in range(2):
        cm2[:, z * GRID_W:(z + 1) * GRID_W, z, :] = cm
    return rm2, cm2


def _na_bias_tables(rpb, rows_n):
    rm, cm = _na_onehots(rows_n)
    val = jnp.einsum("hab,tqpza,xkzb->htpqxk", rpb, rm, cm, precision=_ONEHOT_PRECISION)
    valid = np.einsum("tqpz,xkz->tpqxk", rm.sum(-1), cm.sum(-1)) > 0.5
    return jnp.where(valid[None], val, NEG).reshape(2, 2, NA_TYPES, NA_CHUNKS, NA_Q, LANE)


def _na_bias_grad(dtab, rows_n):
    rm, cm = _na_onehots(rows_n)
    d6 = dtab.reshape(NA_HEADS, NA_TYPES, NA_CHUNKS, 2, GRID_W, LANE)
    return jnp.einsum("htpqxk,tqpza,xkzb->hab", d6, rm, cm, precision=_ONEHOT_PRECISION)


def _na_bias(b_ref, h, typ):
    return jnp.concatenate([b_ref[0, h, typ, c] for c in range(NA_CHUNKS)], axis=1)


def _na_step(p, npairs, rows_n):
    ws = jnp.clip(2 * p - NA_KH // 2, 0, rows_n - NA_WROWS)
    koff = pl.multiple_of(ws * GRID_W, LANE)
    typ = jnp.where(p == 0, 0, jnp.where(p == 1, 1, jnp.where(p == npairs - 2, 3, jnp.where(p == npairs - 1, 4, 2))))
    return _rows(p * NA_Q, NA_Q), pl.ds(koff, NA_K), typ


def _na_fwd_call(proj, btab):
    T = proj.shape[0]
    npairs, rows_n = T // NA_Q, T // GRID_W

    def body(q_ref, k_ref, v_ref, b_ref, o_ref):
        lane = lax.broadcasted_iota(jnp.int32, (NA_Q, LANE), 1)
        m0 = (lane < HEAD_DIM).astype(F32)
        m1 = 1.0 - m0

        def steps(i, carry):
            idx = [_na_step(NA_UNROLL * i + u, npairs, rows_n) for u in range(NA_UNROLL)]
            chains = [(u, h, m) for u in range(NA_UNROLL) for h, m in ((0, m0), (1, m1))]
            kws = [k_ref[krows, :].astype(_BF) for _, krows, _ in idx]
            vws = [v_ref[krows, :].astype(_BF) for _, krows, _ in idx]
            s = [_mm_nt(q_ref[idx[u][0], :] * m, kws[u]) for u, h, m in chains]
            s = [s_ * (HEAD_DIM ** -0.5) + _na_bias(b_ref, h, idx[u][2]) for s_, (u, h, m) in zip(s, chains)]
            e = [jnp.exp(s_ - jnp.max(s_, axis=-1, keepdims=True)) for s_ in s]
            pr = [e_ / jnp.sum(e_, axis=-1, keepdims=True) for e_ in e]
            ov = [_mm(pr_, vws[u]) * m for pr_, (u, h, m) in zip(pr, chains)]
            for u in range(NA_UNROLL):
                o_ref[idx[u][0], :] = ov[2 * u] + ov[2 * u + 1]
            return carry

        lax.fori_loop(0, npairs // NA_UNROLL, steps, 0)

    return pl.pallas_call(
        body, name="na_fwd", grid=(NA_WIDTH // LANE,),
        in_specs=[_strip(T, lambda j: j + 18), _strip(T, lambda j: j + 20), _strip(T, lambda j: j + 22),
                  pl.BlockSpec((1, 2, NA_TYPES, NA_CHUNKS, NA_Q, LANE), lambda j: (j, 0, 0, 0, 0, 0))],
        out_specs=_strip(T, lambda j: j, buffers=1),
        out_shape=jax.ShapeDtypeStruct((T, NA_WIDTH), F32),
        compiler_params=_cparams(("arbitrary",)),
    )(proj, proj, proj, btab)


def _na_bwd_call(proj, dycat, btab, dp):
    T = proj.shape[0]
    npairs, rows_n = T // NA_Q, T // GRID_W
    scale = HEAD_DIM ** -0.5

    def body(q_ref, k_ref, v_ref, do_ref, b_ref, dp_in_ref, dp_out_ref, db_ref, dka_ref, dva_ref, stage_ref, sems):
        dq_ref = stage_ref.at[0]
        lane = lax.broadcasted_iota(jnp.int32, (NA_Q, LANE), 1)
        m0 = (lane < HEAD_DIM).astype(F32)
        m1 = 1.0 - m0
        dka_ref[...] = jnp.zeros_like(dka_ref)
        dva_ref[...] = jnp.zeros_like(dva_ref)
        db_ref[...] = jnp.zeros_like(db_ref)

        def steps(i, carry):
            idx = [_na_step(NA_UNROLL * i + u, npairs, rows_n) for u in range(NA_UNROLL)]
            chains = [(u, h, m) for u in range(NA_UNROLL) for h, m in ((0, m0), (1, m1))]
            kws = [k_ref[krows, :].astype(_BF) for _, krows, _ in idx]
            vws = [v_ref[krows, :].astype(_BF) for _, krows, _ in idx]
            qm = [(q_ref[idx[u][0], :] * m).astype(_BF) for u, h, m in chains]
            dom = [(do_ref[idx[u][0], :] * m).astype(_BF) for u, h, m in chains]
            s = [_mm_nt(qm_, kws[u]) for qm_, (u, h, m) in zip(qm, chains)]
            dpr = [_mm_nt(dom_, vws[u]) for dom_, (u, h, m) in zip(dom, chains)]
            s = [s_ * scale + _na_bias(b_ref, h, idx[u][2]) for s_, (u, h, m) in zip(s, chains)]
            e = [jnp.exp(s_ - jnp.max(s_, axis=-1, keepdims=True)) for s_ in s]
            pr = [e_ / jnp.sum(e_, axis=-1, keepdims=True) for e_ in e]
            ds = [pr_ * (dpr_ - jnp.sum(pr_ * dpr_, axis=-1, keepdims=True)) for pr_, dpr_ in zip(pr, dpr)]
            dsb = [(ds_ * scale).astype(_BF) for ds_ in ds]
            dq = [_mm(dsb_, kws[u]) * m for dsb_, (u, h, m) in zip(dsb, chains)]
            dk = [_mm_tn(dsb_, qm_) for dsb_, qm_ in zip(dsb, qm)]
            dv = [_mm_tn(pr_, dom_) for pr_, dom_ in zip(pr, dom)]
            for ds_, (u, h, m) in zip(ds, chains):
                for c in range(NA_CHUNKS):
                    db_ref[0, h, idx[u][2], c] += ds_[:, c * LANE:(c + 1) * LANE]
            for u in range(NA_UNROLL):
                qrows, krows, _ = idx[u]
                dq_ref[qrows, :] = (dq[2 * u] + dq[2 * u + 1]).astype(dq_ref.dtype)
                dka_ref[krows, :] += dk[2 * u] + dk[2 * u + 1]
                dva_ref[krows, :] += dv[2 * u] + dv[2 * u + 1]
            return carry

        lax.fori_loop(0, npairs // NA_UNROLL, steps, 0)
        stage_ref[1] = dka_ref[...].astype(stage_ref.dtype)
        stage_ref[2] = dva_ref[...].astype(stage_ref.dtype)
        j = pl.program_id(0)
        _store_strips(stage_ref, dp_out_ref, (j + 18, j + 20, j + 22), sems)

    tab = pl.BlockSpec((1, 2, NA_TYPES, NA_CHUNKS, NA_Q, LANE), lambda j: (j, 0, 0, 0, 0, 0))
    return pl.pallas_call(
        body, name="na_bwd", grid=(NA_WIDTH // LANE,),
        in_specs=[_strip(T, lambda j: j + 18), _strip(T, lambda j: j + 20), _strip(T, lambda j: j + 22),
                  _strip(T, lambda j: j + 6), tab, pl.BlockSpec(memory_space=pl.ANY)],
        out_specs=[pl.BlockSpec(memory_space=pl.ANY), tab],
        out_shape=[jax.ShapeDtypeStruct(dp.shape, dp.dtype),
                   jax.ShapeDtypeStruct((2, 2, NA_TYPES, NA_CHUNKS, NA_Q, LANE), F32)],
        scratch_shapes=[pltpu.VMEM((T, LANE), F32), pltpu.VMEM((T, LANE), F32),
                        pltpu.VMEM((3, T, LANE), _BF), pltpu.SemaphoreType.DMA((3,))],
        input_output_aliases={5: 0},
        compiler_params=_cparams(("arbitrary",)),
    )(proj, proj, proj, dycat, btab, dp)


W_BLK = IN_WIDTH // N_DEV
MXU_W = 256
N_BLK = 3 * MXU_W
N_STEPS = IN_WIDTH // N_BLK
TM = 512


def _ln_fwd(z, g, b):
    zc = z - jnp.mean(z, axis=-1, keepdims=True)
    var = jnp.mean(zc * zc, axis=-1, keepdims=True)
    return zc * lax.rsqrt(var + LN_EPS) * g + b


def _ln_bwd(dy, z, g):
    zc = z - jnp.mean(z, axis=-1, keepdims=True)
    rstd = lax.rsqrt(jnp.mean(zc * zc, axis=-1, keepdims=True) + LN_EPS)
    xhat = zc * rstd
    dxh = dy * g
    dz = rstd * (dxh - jnp.mean(dxh, axis=-1, keepdims=True) - xhat * jnp.mean(dxh * xhat, axis=-1, keepdims=True))
    return dz, dy * xhat


def _row_tile(T):
    return 1024 if T % 1024 == 0 else TM


def _halves(n):
    return (pl.ds(0, n // 2), pl.ds(n // 2, n // 2))


def _inproj_call(xb, w, after):
    T = xb.shape[0]
    tm = _row_tile(T)

    def body(x_ref, w_ref, after_ref, o_ref):
        o_ref[...] = _mm(x_ref[...], w_ref[...])

    return pl.pallas_call(
        body, name="inproj", grid=(T // tm, N_STEPS),
        in_specs=[pl.BlockSpec((tm, D_MODEL), lambda i, n: (i, 0)),
                  pl.BlockSpec((D_MODEL, N_BLK), lambda i, n: (0, n)),
                  pl.BlockSpec(memory_space=pl.ANY)],
        out_specs=pl.BlockSpec((tm, N_BLK), lambda i, n: (i, n)),
        out_shape=jax.ShapeDtypeStruct((T, IN_WIDTH), F32),
        compiler_params=_cparams(("parallel", "arbitrary")),
    )(xb, w, after)


def _vec_spec():
    return pl.BlockSpec((1, D_MODEL), lambda *_: (0, 0))


def _outproj_ln_call(y_lru, y_ret, y_na, x, w, g, b, after):
    T = x.shape[0]

    def body(yl_ref, yr_ref, yn_ref, x_ref, w_ref, g_ref, b_ref, after_ref, z_ref, x1_ref, x1b_ref, yc_ref):
        yc_ref[:, 0:LRU_WIDTH] = yl_ref[...].astype(yc_ref.dtype)
        yc_ref[:, LRU_WIDTH:LRU_WIDTH + RET_WIDTH] = yr_ref[...].astype(yc_ref.dtype)
        yc_ref[:, LRU_WIDTH + RET_WIDTH:] = yn_ref[...].astype(yc_ref.dtype)
        z = ALPHA * x_ref[...] + _mm(yc_ref[...], w_ref[...])
        z_ref[...] = z
        x1 = _ln_fwd(z, g_ref[...], b_ref[...])
        x1_ref[...] = x1
        x1b_ref[...] = x1.astype(x1b_ref.dtype)

    row = lambda w_: pl.BlockSpec((TM, w_), lambda i: (i, 0))
    return pl.pallas_call(
        body, name="outproj_ln", grid=(T // TM,),
        in_specs=[row(LRU_WIDTH), row(RET_WIDTH), row(NA_WIDTH), row(D_MODEL),
                  pl.BlockSpec((D_MODEL, D_MODEL), lambda i: (0, 0)), _vec_spec(), _vec_spec(),
                  pl.BlockSpec(memory_space=pl.ANY)],
        out_specs=[row(D_MODEL)] * 4,
        out_shape=[jax.ShapeDtypeStruct((T, D_MODEL), F32), jax.ShapeDtypeStruct((T, D_MODEL), F32),
                   jax.ShapeDtypeStruct((T, D_MODEL), _BF), jax.ShapeDtypeStruct((T, D_MODEL), _BF)],
        compiler_params=_cparams(("parallel",)),
    )(y_lru, y_ret, y_na, x, w, g, b, after)


def _ffn_ln_call(x1, x1b, wg, wu, wd, g, b):
    T = x1.shape[0]

    def body(x_ref, xb_ref, wg_ref, wu_ref, wd_ref, g_ref, b_ref, z_ref, x2_ref, x2b_ref, gp_ref, up_ref, acc_ref):
        n = pl.program_id(1)

        @pl.when(n == 0)
        def _():
            acc_ref[...] = jnp.zeros_like(acc_ref)

        r0, r1 = _halves(TM)

        def pre(rows):
            xb = xb_ref[rows, :]
            return _mm(xb, wg_ref[...]), _mm(xb, wu_ref[...])

        def act(rows, gp, up):
            gp_ref[rows, :] = gp.astype(gp_ref.dtype)
            up_ref[rows, :] = up.astype(up_ref.dtype)
            return (gp * _sigmoid(gp) * up).astype(_BF)

        gp0, up0 = pre(r0)
        hid0 = act(r0, gp0, up0)
        gp1, up1 = pre(r1)
        acc_ref[r0, :] += _mm(hid0, wd_ref[...])
        hid1 = act(r1, gp1, up1)
        acc_ref[r1, :] += _mm(hid1, wd_ref[...])

        @pl.when(n == N_STEPS - 1)
        def _():
            z = ALPHA * x_ref[...] + acc_ref[...]
            z_ref[...] = z
            x2 = _ln_fwd(z, g_ref[...], b_ref[...])
            x2_ref[...] = x2
            x2b_ref[...] = x2.astype(x2b_ref.dtype)

    row = pl.BlockSpec((TM, D_MODEL), lambda i, n: (i, 0))
    return pl.pallas_call(
        body, name="ffn_ln", grid=(T // TM, N_STEPS),
        in_specs=[row, row,
                  pl.BlockSpec((D_MODEL, N_BLK), lambda i, n: (0, n)),
                  pl.BlockSpec((D_MODEL, N_BLK), lambda i, n: (0, n)),
                  pl.BlockSpec((N_BLK, D_MODEL), lambda i, n: (n, 0)), _vec_spec(), _vec_spec()],
        out_specs=[row] * 3 + [pl.BlockSpec((TM, N_BLK), lambda i, n: (i, n))] * 2,
        out_shape=[jax.ShapeDtypeStruct((T, D_MODEL), F32), jax.ShapeDtypeStruct((T, D_MODEL), F32),
                   jax.ShapeDtypeStruct((T, D_MODEL), _BF),
                   jax.ShapeDtypeStruct((T, IN_WIDTH), _BF), jax.ShapeDtypeStruct((T, IN_WIDTH), _BF)],
        scratch_shapes=[pltpu.VMEM((TM, D_MODEL), F32)],
        compiler_params=_cparams(("parallel", "arbitrary")),
    )(x1, x1b, wg, wu, wd, g, b)


def _loss_call(y, t):
    T = y.shape[0]

    def body(y_ref, t_ref, dy_ref, loss_ref):
        @pl.when(pl.program_id(0) == 0)
        def _():
            loss_ref[...] = jnp.zeros_like(loss_ref)

        err = y_ref[...] - t_ref[...]
        dy_ref[...] = err * (1.0 / D_MODEL)
        part = 0.5 * jnp.sum(jnp.mean(err * err, axis=-1, keepdims=True), axis=0, keepdims=True)
        loss_ref[...] += jnp.broadcast_to(part, loss_ref.shape)

    row = pl.BlockSpec((TM, D_MODEL), lambda i: (i, 0))
    return pl.pallas_call(
        body, name="loss", grid=(T // TM,),
        in_specs=[row, row],
        out_specs=[row, pl.BlockSpec((SUB, LANE), lambda i: (0, 0))],
        out_shape=[jax.ShapeDtypeStruct((T, D_MODEL), F32), jax.ShapeDtypeStruct((SUB, LANE), F32)],
        compiler_params=_cparams(("arbitrary",)),
    )(y, t)


def _ffn_bwd_call(dx2, z2, gpb, upb, wg, wu, wd, g, after):
    T = dx2.shape[0]

    def body(dx2_ref, z_ref, gp_ref, up_ref, wg_ref, wu_ref, wd_ref, g_ref, after_ref,
             dx1_ref, dgp_ref, dup_ref, hid_ref, dzb_ref, dln_ref, acc_ref):
        i, n = pl.program_id(0), pl.program_id(1)

        @pl.when((i == 0) & (n == 0))
        def _():
            dln_ref[...] = jnp.zeros_like(dln_ref)

        @pl.when(n == 0)
        def _():
            dy = dx2_ref[...]
            dz, dg_rows = _ln_bwd(dy, z_ref[...], g_ref[...])
            dzb_ref[...] = dz.astype(dzb_ref.dtype)
            acc_ref[...] = ALPHA * dz
            dln_ref[0:1, :] += jnp.sum(dg_rows, axis=0, keepdims=True)
            dln_ref[1:2, :] += jnp.sum(dy, axis=0, keepdims=True)

        r0, r1 = _halves(TM)

        def grads(rows, dhid):
            gp = gp_ref[rows, :].astype(F32)
            up = up_ref[rows, :].astype(F32)
            sg = _sigmoid(gp)
            act = gp * sg
            hid_ref[rows, :] = (act * up).astype(hid_ref.dtype)
            dup = (dhid * act).astype(_BF)
            dgp = (dhid * up * (sg * (1.0 + gp * (1.0 - sg)))).astype(_BF)
            dgp_ref[rows, :] = dgp.astype(dgp_ref.dtype)
            dup_ref[rows, :] = dup.astype(dup_ref.dtype)
            return dgp, dup

        dhid0 = _mm_nt(dzb_ref[r0, :], wd_ref[...])
        dhid1 = _mm_nt(dzb_ref[r1, :], wd_ref[...])
        dgp0, dup0 = grads(r0, dhid0)
        acc_ref[r0, :] += _mm_nt(dgp0, wg_ref[...]) + _mm_nt(dup0, wu_ref[...])
        dgp1, dup1 = grads(r1, dhid1)
        acc_ref[r1, :] += _mm_nt(dgp1, wg_ref[...]) + _mm_nt(dup1, wu_ref[...])

        @pl.when(n == N_STEPS - 1)
        def _():
            dx1_ref[...] = acc_ref[...]

    row = pl.BlockSpec((TM, D_MODEL), lambda i, n: (i, 0))
    blk = pl.BlockSpec((TM, N_BLK), lambda i, n: (i, n))
    return pl.pallas_call(
        body, name="ffn_bwd", grid=(T // TM, N_STEPS),
        in_specs=[row, row, blk, blk,
                  pl.BlockSpec((D_MODEL, N_BLK), lambda i, n: (0, n)),
                  pl.BlockSpec((D_MODEL, N_BLK), lambda i, n: (0, n)),
                  pl.BlockSpec((N_BLK, D_MODEL), lambda i, n: (n, 0)), _vec_spec(),
                  pl.BlockSpec(memory_space=pl.ANY)],
        out_specs=[row, blk, blk, blk, row, pl.BlockSpec((SUB, D_MODEL), lambda i, n: (0, 0))],
        out_shape=[jax.ShapeDtypeStruct((T, D_MODEL), F32),
                   jax.ShapeDtypeStruct((T, IN_WIDTH), _BF), jax.ShapeDtypeStruct((T, IN_WIDTH), _BF),
                   jax.ShapeDtypeStruct((T, IN_WIDTH), _BF), jax.ShapeDtypeStruct((T, D_MODEL), _BF),
                   jax.ShapeDtypeStruct((SUB, D_MODEL), F32)],
        scratch_shapes=[pltpu.VMEM((TM, D_MODEL), F32)],
        compiler_params=_cparams(("arbitrary", "arbitrary")),
    )(dx2, z2, gpb, upb, wg, wu, wd, g, after)


def _outproj_bwd_call(dx1, z1, w, g):
    T = dx1.shape[0]

    def body(dx_ref, z_ref, w_ref, g_ref, dzb_ref, dyc_ref, dres_ref, dln_ref):
        @pl.when(pl.program_id(0) == 0)
        def _():
            dln_ref[...] = jnp.zeros_like(dln_ref)

        dy = dx_ref[...]
        dz, dg_rows = _ln_bwd(dy, z_ref[...], g_ref[...])
        dzb_ref[...] = dz.astype(dzb_ref.dtype)
        dres_ref[...] = ALPHA * dz
        dyc_ref[...] = _mm_nt(dz, w_ref[...])
        dln_ref[0:1, :] += jnp.sum(dg_rows, axis=0, keepdims=True)
        dln_ref[1:2, :] += jnp.sum(dy, axis=0, keepdims=True)

    row = pl.BlockSpec((TM, D_MODEL), lambda i: (i, 0))
    return pl.pallas_call(
        body, name="outproj_bwd", grid=(T // TM,),
        in_specs=[row, row, pl.BlockSpec((D_MODEL, D_MODEL), lambda i: (0, 0)), _vec_spec()],
        out_specs=[row, row, row, pl.BlockSpec((SUB, D_MODEL), lambda i: (0, 0))],
        out_shape=[jax.ShapeDtypeStruct((T, D_MODEL), _BF), jax.ShapeDtypeStruct((T, D_MODEL), F32),
                   jax.ShapeDtypeStruct((T, D_MODEL), F32), jax.ShapeDtypeStruct((SUB, D_MODEL), F32)],
        compiler_params=_cparams(("arbitrary",)),
    )(dx1, z1, w, g)


def _inproj_bwd_call(dres, dp, w):
    T = dres.shape[0]

    def body(dres_ref, dp_ref, w_ref, dx_ref):
        dx_ref[...] = dres_ref[...] + _mm_nt(dp_ref[...], w_ref[...])

    row = pl.BlockSpec((TM, D_MODEL), lambda i: (i, 0))
    return pl.pallas_call(
        body, name="inproj_bwd", grid=(T // TM,),
        in_specs=[row, pl.BlockSpec((TM, IN_WIDTH), lambda i: (i, 0)),
                  pl.BlockSpec((D_MODEL, IN_WIDTH), lambda i: (0, 0), pipeline_mode=pl.Buffered(1))],
        out_specs=row,
        out_shape=jax.ShapeDtypeStruct((T, D_MODEL), F32),
        compiler_params=_cparams(("parallel",)),
    )(dres, dp, w)


def _tn_cols_call(a, b, name):
    T, ka = a.shape
    n = b.shape[1]

    def body(a_ref, b_ref, o_ref):
        o_ref[...] = _mm_tn(a_ref[...], b_ref[...]).astype(o_ref.dtype)

    return pl.pallas_call(
        body, name=name, grid=(n // N_BLK,),
        in_specs=[pl.BlockSpec((T, ka), lambda j: (0, 0), pipeline_mode=pl.Buffered(1)),
                  pl.BlockSpec((T, N_BLK), lambda j: (0, j))],
        out_specs=pl.BlockSpec((ka, N_BLK), lambda j: (0, j)),
        out_shape=jax.ShapeDtypeStruct((ka, n), _BF),
        compiler_params=_cparams(("parallel",)),
    )(a, b)


def _tn_rows_call(a, b, kb, name):
    T, ka = a.shape
    n = b.shape[1]

    def body(a_ref, b_ref, o_ref):
        o_ref[...] = _mm_tn(a_ref[...], b_ref[...]).astype(o_ref.dtype)

    return pl.pallas_call(
        body, name=name, grid=(ka // kb,),
        in_specs=[pl.BlockSpec((T, kb), lambda r: (0, r)),
                  pl.BlockSpec((T, n), lambda r: (0, 0), pipeline_mode=pl.Buffered(1))],
        out_specs=pl.BlockSpec((kb, n), lambda r: (r, 0)),
        out_shape=jax.ShapeDtypeStruct((ka, n), _BF),
        compiler_params=_cparams(("parallel",)),
    )(a, b)


def _me():
    return lax.axis_index("x"), lax.axis_index("y"), lax.axis_index("c")


def _flip(k):
    x, y, c = _me()
    return (1 - x if k & 4 else x, 1 - y if k & 2 else y, 1 - c if k & 1 else c)


def _dev_index(pos):
    return 4 * pos[0] + 2 * pos[1] + pos[2]


_HBM = pl.BlockSpec(memory_space=pltpu.HBM)
_SEM = pl.BlockSpec(memory_space=pltpu.SEMAPHORE)


def _land_shape(shape, mode):
    if mode == "all":
        return (N_DEV,) + shape
    if mode == "cols":
        return (shape[0], N_DEV * shape[1])
    if mode == "blk":
        return shape
    assert mode == "scols"
    return (N_DEV, shape[0], shape[1] // N_DEV)


def _comm_copies(ins, lands, modes, send_sems, recv_sems):
    me = _dev_index(_me())
    copies = []
    for k in range(N_DEV):
        peer = _flip(k)
        pidx = _dev_index(peer)
        for a, (src, land, mode) in enumerate(zip(ins, lands, modes)):
            if mode == "blk":
                src = src.at[pidx]
            elif mode == "scols":
                w = src.shape[1] // N_DEV
                src = src.at[:, pl.ds(pl.multiple_of(pidx * w, LANE), w)]
            if mode == "cols":
                w = src.shape[1]
                dst = land.at[:, pl.ds(pl.multiple_of(me * w, LANE), w)]
            else:
                dst = land.at[me]
            copies.append(pltpu.make_async_remote_copy(
                src_ref=src, dst_ref=dst, send_sem=send_sems.at[k * len(ins) + a], recv_sem=recv_sems.at[k * len(ins) + a],
                device_id=peer, device_id_type=MESH))
    return copies


def _comm_start_call(arrs, gather_flags, after, name):
    n = len(arrs)
    lands = [lax.empty(_land_shape(v.shape, mode), v.dtype) for v, mode in zip(arrs, gather_flags)]

    def body(*refs):
        ins, lnd = refs[:n], refs[n:2 * n]
        send_sems, recv_sems = refs[2 * n + len(after)], refs[2 * n + len(after) + 1]
        for cp in _comm_copies(ins, lnd, gather_flags, send_sems, recv_sems):
            cp.start()
        refs[-1][...] = jnp.zeros_like(refs[-1])

    hbm = [pltpu.with_memory_space_constraint(v, pltpu.HBM) for v in list(arrs) + lands]
    out = pl.pallas_call(
        body, name=name,
        out_shape=(pltpu.SemaphoreType.DMA((N_DEV * n,)), pltpu.SemaphoreType.DMA((N_DEV * n,)),
                   *[pltpu.HBM(v.shape, v.dtype) for v in hbm], jax.ShapeDtypeStruct((SUB, LANE), F32)),
        in_specs=[_HBM] * (2 * n) + [pl.BlockSpec(memory_space=pl.ANY)] * len(after),
        out_specs=(_SEM, _SEM, *[_HBM] * (2 * n), pl.BlockSpec(memory_space=pltpu.VMEM)),
        input_output_aliases={i: 2 + i for i in range(2 * n)},
        compiler_params=pltpu.CompilerParams(has_side_effects=pltpu.SideEffectType.DATAFLOW_SIDE_EFFECTING),
    )(*hbm, *after)
    return out[:-1], out[-1]


def _comm_wait_call(state, gather_flags, after, name):
    n = len(gather_flags)
    send_sems, recv_sems, thru = state[0], state[1], state[2:]

    def body(*refs):
        ins, lnd, ssem, rsem = refs[:n], refs[n:2 * n], refs[2 * n], refs[2 * n + 1]
        for cp in _comm_copies(ins, lnd, gather_flags, ssem, rsem):
            cp.wait_send()
            cp.wait_recv()

    out = pl.pallas_call(
        body, name=name,
        out_shape=tuple(pltpu.HBM(v.shape, v.dtype) for v in thru),
        in_specs=[_HBM] * (2 * n) + [_SEM, _SEM] + [pl.BlockSpec(memory_space=pl.ANY)] * len(after),
        out_specs=tuple([_HBM] * (2 * n)),
        input_output_aliases={i: i for i in range(2 * n)},
        compiler_params=pltpu.CompilerParams(has_side_effects=pltpu.SideEffectType.DATAFLOW_SIDE_EFFECTING),
    )(*thru, send_sems, recv_sems, *after)
    return out[n:]


def _sum8_call(recv, stacked, layer, nl, rows, r_out, c_out, name):
    c = recv.shape[2]

    def body(x_ref, *rest):
        o_ref = rest[-1]
        acc = x_ref[0, :, :c_out].astype(F32)
        for s in range(1, N_DEV):
            acc = acc + x_ref[s, :, :c_out].astype(F32)
        o_ref[...] = acc

    prev = [] if stacked is None else [stacked]
    return pl.pallas_call(
        body, name=name, grid=(r_out // rows,),
        in_specs=[pl.BlockSpec((N_DEV, rows, c), lambda i: (0, i, 0))] + [pl.BlockSpec(memory_space=pl.ANY)] * len(prev),
        out_specs=pl.BlockSpec((None, rows, c_out), lambda i: (layer, i, 0)),
        out_shape=jax.ShapeDtypeStruct((nl, r_out, c_out), F32),
        input_output_aliases={1: 0} if prev else {},
        compiler_params=_cparams(("parallel",)),
    )(recv, *prev)


def _adamw_call(w, g, m, v, rows, name):
    r, c = w.shape

    def body(w_ref, g_ref, m_ref, v_ref, d_ref, nm_ref, nv_ref):
        gr = g_ref[...]
        nm = ADAM_B1 * m_ref[...] + (1.0 - ADAM_B1) * gr
        nv = ADAM_B2 * v_ref[...] + (1.0 - ADAM_B2) * (gr * gr)
        m_hat = nm / (1.0 - ADAM_B1 ** ADAM_STEP)
        v_hat = nv / (1.0 - ADAM_B2 ** ADAM_STEP)
        d_ref[...] = -ADAM_LR * (m_hat / (jnp.sqrt(v_hat) + ADAM_EPS) + ADAM_WD * w_ref[...])
        nm_ref[...] = nm
        nv_ref[...] = nv

    spec = pl.BlockSpec((rows, c), lambda i: (i, 0))
    return pl.pallas_call(
        body, name=name, grid=(r // rows,),
        in_specs=[spec] * 4, out_specs=[spec] * 3,
        out_shape=[jax.ShapeDtypeStruct((r, c), F32)] * 3,
        compiler_params=_cparams(("parallel",)),
    )(w, g, m, v)


SH_ROWS = 16
SH_W = LRU_WIDTH // N_DEV
REP_ROWS = 824
_REP_SIZES = (LRU_WIDTH, 2 * 6 * 64 * 64, 2 * 6 * 64 * 64, RET_WIDTH, 1920, D_MODEL, D_MODEL, D_MODEL, D_MODEL)
_RPB_SIZE = NA_HEADS * (2 * NA_KH - 1) * (2 * NA_KW - 1)


def _pack_sh(cw, ba, bx, lam):
    return jnp.concatenate([cw, ba, bx, lam], axis=0)


def _pad_sh(p):
    pad = [(0, 0)] * (p.ndim - 2) + [(0, SH_ROWS - p.shape[-2]), (0, LANE - p.shape[-1])]
    return jnp.pad(p, pad)


def _pack_rep(cb, wa, wx, gnw, rpb, l1g, l1b, l2g, l2b):
    flat = jnp.concatenate([cb.reshape(-1), wa.reshape(-1), wx.reshape(-1), gnw.reshape(-1),
                            jnp.pad(rpb.reshape(-1), (0, 1920 - _RPB_SIZE)), l1g, l1b, l2g, l2b,
                            jnp.zeros((REP_ROWS * LANE - sum(_REP_SIZES),), F32)])
    return flat.reshape(REP_ROWS, LANE)


def _unpack_rep(p):
    nl = p.shape[0]
    flat = p.reshape(nl, -1)
    out, off = [], 0
    for size in _REP_SIZES:
        out.append(flat[:, off:off + size])
        off += size
    cb, wa, wx, gnw, rpb, l1g, l1b, l2g, l2b = out
    return (cb, wa.reshape(nl, 2, 6, 64, 64), wx.reshape(nl, 2, 6, 64, 64), gnw,
            rpb[:, :_RPB_SIZE].reshape(nl, NA_HEADS, 2 * NA_KH - 1, 2 * NA_KW - 1), l1g, l1b, l2g, l2b)


def _adamw_nd(w, g, m, v, rows, name):
    shp = w.shape
    f = lambda t: t.reshape(-1, shp[-1])
    rows = f(w).shape[0] if rows is None else rows
    return [t.reshape(shp) for t in _adamw_call(f(w), f(g), f(m), f(v), rows, name)]


def kernel(x, w_in, conv_w, conv_b, lru_w_a, lru_b_a, lru_w_x, lru_b_x, lru_lam, ret_gn_w, na_rpb, w_out, ln1_g, ln1_b, w_gate, w_up, w_down, ln2_g, ln2_b, loss_target, m_w_in, m_conv_w, m_conv_b, m_lru_w_a, m_lru_b_a, m_lru_w_x, m_lru_b_x, m_lru_lam, m_ret_gn_w, m_na_rpb, m_w_out, m_ln1_g, m_ln1_b, m_w_gate, m_w_up, m_w_down, m_ln2_g, m_ln2_b, v_w_in, v_conv_w, v_conv_b, v_lru_w_a, v_lru_b_a, v_lru_w_x, v_lru_b_x, v_lru_lam, v_ret_gn_w, v_na_rpb, v_w_out, v_ln1_g, v_ln1_b, v_w_gate, v_w_up, v_w_down, v_ln2_g, v_ln2_b):
    nl = w_in.shape[0]
    T = x.shape[1]
    rows_n = T // GRID_W
    x0, target = x[0], loss_target[0]
    ffpad = W_BLK - FF_BLK

    win_b = w_in.astype(_BF)
    wg_b = jnp.pad(w_gate, ((0, 0), (0, 0), (0, ffpad))).astype(_BF)
    wu_b = jnp.pad(w_up, ((0, 0), (0, 0), (0, ffpad))).astype(_BF)
    wd_b = jnp.pad(w_down, ((0, 0), (0, ffpad), (0, 0))).astype(_BF)
    wout_b = w_out.astype(_BF)
    def agf_start(l, after):
        sh = _pad_sh(_pack_sh(conv_w[l], lru_b_a[l], lru_b_x[l], lru_lam[l]))
        arrs, modes = [win_b[l], sh], ["cols", "all"]
        if l > 0:
            arrs, modes = arrs + [wd_b[l]], modes + ["all"]
        return _comm_start_call(arrs, modes, after, f"agf_start{l}"), modes

    def agk_start(l, after):
        arrs, modes = [wg_b[l], wu_b[l], wout_b[l]], ["cols", "cols", "all"]
        if l == 0:
            arrs, modes = arrs + [wd_b[l]], modes + ["all"]
        return _comm_start_call(arrs, modes, after, f"agk_start{l}"), modes

    tables = _ret_tables(T)
    w4_all = _lru_w4(lru_w_a, lru_w_x)
    layers = []
    gathered = []
    xs, xb = x0, x0.astype(_BF)
    (agf_state, token), agf_modes = agf_start(0, [])
    tie = 0.0 * token[0, 0]
    btabs = [_na_bias_tables(na_rpb[l] + tie, rows_n) for l in range(nl)]
    for l in range(nl):
        front = _comm_wait_call(agf_state, agf_modes, [xb] + (btabs if l == 0 else []), f"agf_wait{l}")
        win, shg = front[0], front[1]
        (agk_state, token), agk_modes = agk_start(l, [shg])
        full = shg[:, :10, :SH_W].transpose(1, 0, 2).reshape(10, LRU_WIDTH)
        vec, w4 = _lru_vec(full[0:4], conv_b[l], full[4:6], full[6:8], full[8:10]), w4_all[l]
        gnw8 = jnp.pad(ret_gn_w[l][None], ((0, SUB - 1), (0, 0)))
        btab = btabs[l]
        proj = _inproj_call(xb, win, token)
        y_lru = _lru_fwd_call(proj, vec, w4)
        y_ret = _ret_fwd_call(proj, tables, gnw8)
        y_na = _na_fwd_call(proj, btab)
        back = _comm_wait_call(agk_state, agk_modes, [y_na], f"agk_wait{l}")
        wg, wu, wout = back[0], back[1], back[2]
        wd = (back[3] if l == 0 else front[2]).reshape(IN_WIDTH, D_MODEL)
        wout = wout.reshape(D_MODEL, D_MODEL)
        gathered.append((win, wg, wu, wd, wout))
        if l + 1 < nl:
            (agf_state, token), agf_modes = agf_start(l + 1, [wout])
        z1, x1, x1b, ycb = _outproj_ln_call(y_lru, y_ret, y_na, xs, wout, ln1_g[l][None], ln1_b[l][None], token)
        z2, x2, x2b, gpb, upb = _ffn_ln_call(x1, x1b, wg, wu, wd, ln2_g[l][None], ln2_b[l][None])
        layers.append(dict(xb=xb, proj=proj, vec=vec, w4=w4, gnw8=gnw8, btab=btab,
                           z1=z1, x1b=x1b, ycb=ycb, z2=z2, gpb=gpb, upb=upb))
        xs, xb = x2, x2b

    dx, loss_blk = _loss_call(xs, target)
    loss = lax.psum(loss_blk[0, 0], ("x", "y", "c"))

    gxa_flags = ["scols", "scols", "blk", "blk"]
    gxb_flags = ["scols", "blk", "all"]
    gxa_state, gxb_state = [None] * nl, [None] * nl
    token = loss_blk
    for l in reversed(range(nl)):
        s = layers[l]
        win, wg, wu, wd, wout = gathered[l]
        dx1, dgp, dup, hid, dz2b, dln2 = _ffn_bwd_call(dx, s["z2"], s["gpb"], s["upb"], wg, wu, wd, ln2_g[l][None], token)
        dwg = _tn_cols_call(s["x1b"], dgp, "tn_cols")
        dwu = _tn_cols_call(s["x1b"], dup, "tn_cols")
        dwd = _tn_rows_call(hid, dz2b, N_BLK, "tn_rows_down").reshape(N_DEV, W_BLK, D_MODEL)
        dz1b, dyc, dres, dln1 = _outproj_bwd_call(dx1, s["z1"], wout, ln1_g[l][None])
        dwout = _tn_rows_call(s["ycb"], dz1b, D_MODEL // 2, "tn_rows_out").reshape(N_DEV, LANE, D_MODEL)
        gxa_state[l], token = _comm_start_call([dwg, dwu, dwd, dwout], gxa_flags, [], f"gxa_start{l}")
        dp, dvec, dw4 = _lru_bwd_call(s["proj"], dyc, s["vec"], s["w4"], token)
        dp, dgnw = _ret_bwd_call(s["proj"], dyc, tables, s["gnw8"], dp)
        dp, dbias = _na_bwd_call(s["proj"], dyc, s["btab"], dp)
        dwin = _tn_cols_call(s["xb"], dp, "tn_cols")
        dx = _inproj_bwd_call(dres, dp, win)
        dcw, dcb, dwa, dba, dwx, dbx, dlam = _lru_unpack(dvec, dw4)
        rep = _pack_rep(dcb, dwa, dwx, dgnw[0], _na_bias_grad(dbias, rows_n), dln1[0], dln1[1], dln2[0], dln2[1])
        sh = _pack_sh(dcw, dba, dbx, dlam).reshape(10, N_DEV, SH_W).transpose(1, 0, 2)
        gxb_state[l], token = _comm_start_call([dwin, _pad_sh(sh), rep], gxb_flags, [], f"gxb_start{l}")

    g_w_in = g_w_gate = g_w_up = g_w_down = g_w_out = g_shp = g_repp = None
    after = [dx, token]
    big = {}
    for l in reversed(range(nl)):
        ra = _comm_wait_call(gxa_state[l], gxa_flags, after, f"gxa_wait{l}")
        g_w_gate = _sum8_call(ra[0], g_w_gate, l, nl, TM, D_MODEL, FF_BLK, "sum8_ff")
        g_w_up = _sum8_call(ra[1], g_w_up, l, nl, TM, D_MODEL, FF_BLK, "sum8_ff")
        g_w_down = _sum8_call(ra[2], g_w_down, l, nl, FF_BLK, FF_BLK, D_MODEL, "sum8_down")
        g_w_out = _sum8_call(ra[3], g_w_out, l, nl, LANE, LANE, D_MODEL, "sum8_out")
        after = [g_w_out]
        if l == 0:
            big["w_gate"] = _adamw_nd(w_gate, g_w_gate, m_w_gate, v_w_gate, TM, "adamw_ff")
            big["w_up"] = _adamw_nd(w_up, g_w_up, m_w_up, v_w_up, TM, "adamw_ff")
            big["w_down"] = _adamw_nd(w_down, g_w_down, m_w_down, v_w_down, FF_BLK, "adamw_down")
            big["w_out"] = _adamw_nd(w_out, g_w_out, m_w_out, v_w_out, LANE, "adamw_out")
            after = [big[n][k] for n in ("w_gate", "w_up", "w_down", "w_out") for k in range(3)]
        rb = _comm_wait_call(gxb_state[l], gxb_flags, after, f"gxb_wait{l}")
        g_w_in = _sum8_call(rb[0], g_w_in, l, nl, TM, D_MODEL, W_BLK, "sum8_in")
        g_shp = _sum8_call(rb[1], g_shp, l, nl, SH_ROWS, SH_ROWS, LANE, "sum8_sh")
        g_repp = _sum8_call(rb[2], g_repp, l, nl, REP_ROWS, REP_ROWS, LANE, "sum8_rep")
        after = [g_repp]

    big["w_in"] = _adamw_nd(w_in, g_w_in, m_w_in, v_w_in, TM, "adamw_in")
    g_shp = g_shp[:, :, :SH_W]
    rep_names = ("conv_b", "lru_w_a", "lru_w_x", "ret_gn_w", "na_rpb", "ln1_g", "ln1_b", "ln2_g", "ln2_b")
    grads = {"w_in": g_w_in, "w_gate": g_w_gate, "w_up": g_w_up, "w_down": g_w_down, "w_out": g_w_out,
             "conv_w": g_shp[:, 0:4], "lru_b_a": g_shp[:, 4:6], "lru_b_x": g_shp[:, 6:8], "lru_lam": g_shp[:, 8:10]}
    grads.update(dict(zip(rep_names, _unpack_rep(g_repp))))
    small = {
        "conv_w": (conv_w, m_conv_w, v_conv_w), "conv_b": (conv_b, m_conv_b, v_conv_b),
        "lru_w_a": (lru_w_a, m_lru_w_a, v_lru_w_a), "lru_b_a": (lru_b_a, m_lru_b_a, v_lru_b_a),
        "lru_w_x": (lru_w_x, m_lru_w_x, v_lru_w_x), "lru_b_x": (lru_b_x, m_lru_b_x, v_lru_b_x),
        "lru_lam": (lru_lam, m_lru_lam, v_lru_lam), "ret_gn_w": (ret_gn_w, m_ret_gn_w, v_ret_gn_w),
        "na_rpb": (na_rpb, m_na_rpb, v_na_rpb), "ln1_g": (ln1_g, m_ln1_g, v_ln1_g), "ln1_b": (ln1_b, m_ln1_b, v_ln1_b),
        "ln2_g": (ln2_g, m_ln2_g, v_ln2_g), "ln2_b": (ln2_b, m_ln2_b, v_ln2_b),
    }
    for name, (w_, m_, v_) in small.items():
        big[name] = _adamw_nd(w_, grads[name], m_, v_, None, "adamw_small")
    kinds = [{n: big[n][k] for n in big} for k in range(3)]
    order = ("w_in", "conv_w", "conv_b", "lru_w_a", "lru_b_a", "lru_w_x", "lru_b_x", "lru_lam", "ret_gn_w", "na_rpb",
             "w_out", "ln1_g", "ln1_b", "w_gate", "w_up", "w_down", "ln2_g", "ln2_b")
    outs = [loss, dx[None]]
    for d in (grads, *kinds):
        outs.extend(d[n] for n in order)
    return tuple(outs)
```

```python
import functools
import math

import numpy as np
import jax
import jax.numpy as jnp
from jax import lax
from jax.experimental import pallas as pl
from jax.experimental.pallas import tpu as pltpu

F32 = jnp.float32
_BF = jnp.bfloat16

D_MODEL = 1024
DEPTH = 4
GRID_W = 64
HEAD_DIM = 64
LRU_WIDTH = 384
RET_WIDTH = 384
RET_HEADS = 6
NA_WIDTH = 256
NA_HEADS = 4
IN_WIDTH = 3072
CONV_WIDTH = 4
LRU_C = 8.0
RET_CHUNK = 128
ROPE_BASE = 10000.0
GN_EPS = 1e-6
NA_KH = 8
NA_KW = 16
D_FF = 2816
FF_BLK = 352
N_DEV = 8
ALPHA = (2 * DEPTH) ** 0.25
LN_EPS = 1e-5
ADAM_LR = 0.001
ADAM_B1 = 0.9
ADAM_B2 = 0.999
ADAM_EPS = 1e-08
ADAM_WD = 0.01
ADAM_STEP = 10

LANE = 128
SUB = 8
VMEM_MB = 56
NEG = -1e30

MESH = pl.DeviceIdType.MESH


def _cparams(sem=None, vmem_mb=VMEM_MB):
    return pltpu.CompilerParams(dimension_semantics=sem, vmem_limit_bytes=vmem_mb << 20)


def _mm(a, b):
    return jnp.dot(a.astype(_BF), b.astype(_BF), preferred_element_type=F32)


def _mm_nt(a, b):
    return lax.dot_general(a.astype(_BF), b.astype(_BF), (((1,), (1,)), ((), ())), preferred_element_type=F32)


def _mm_tn(a, b):
    return lax.dot_general(a.astype(_BF), b.astype(_BF), (((0,), (0,)), ((), ())), preferred_element_type=F32)


def _sigmoid(x):
    return jax.nn.sigmoid(x)


def _rows(start, size):
    return pl.ds(pl.multiple_of(start, SUB), size)


def _loop2(n, body, init):
    assert n % 2 == 0
    return lax.fori_loop(0, n // 2, lambda i, c: body(2 * i + 1, body(2 * i, c)), init)


def _strip(T, col, buffers=2):
    return pl.BlockSpec((T, LANE), lambda j: (0, col(j)), pipeline_mode=pl.Buffered(buffers))


LRU_CH = 1024
_GELU_C0 = math.sqrt(2.0 / math.pi)
_GELU_C1 = 0.044715


def _gelu_parts(x):
    x2 = x * x
    t = jnp.tanh(_GELU_C0 * (x + _GELU_C1 * x * x2))
    val = 0.5 * x * (1.0 + t)
    der = 0.5 * (1.0 + t) + 0.5 * x * (1.0 - t * t) * _GELU_C0 * (1.0 + 3.0 * _GELU_C1 * x2)
    return val, der


def _softplus_neg(lam):
    e = jnp.exp(-jnp.abs(lam))
    w = 1.0 + e
    l1p = jnp.where(w == 1.0, e, jnp.log(w) * (e / jnp.where(w == 1.0, 1.0, w - 1.0)))
    return jnp.maximum(-lam, 0.0) + l1p


def _window(ref, t0, ch, T):
    prev = ref[_rows(jnp.maximum(t0 - SUB, 0), SUB), :].astype(F32)
    nxt = ref[_rows(jnp.minimum(t0 + ch, T - SUB), SUB), :].astype(F32)
    prev = jnp.where(t0 > 0, prev, 0.0)
    nxt = jnp.where(t0 + ch < T, nxt, 0.0)
    return jnp.concatenate([prev, ref[_rows(t0, ch), :].astype(F32), nxt], axis=0)


def _tap(win, shift, ch):
    n = win.shape[0]
    return pltpu.roll(win, (-shift) % n, 0)[SUB:SUB + ch]


def _lru_conv(xb_ref, vec, t0, T):
    win = _window(xb_ref, t0, LRU_CH, T)
    xc = jnp.broadcast_to(vec[4:5, :], (LRU_CH, LANE))
    for j in range(CONV_WIDTH):
        xc = xc + _tap(win, j - CONV_WIDTH // 2, LRU_CH) * vec[j:j + 1, :]
    return xc


def _lru_dir(pre_a, pre_x, sp):
    r = _sigmoid(pre_a)
    i = _sigmoid(pre_x)
    log_a = (-LRU_C) * r * sp
    a = jnp.exp(log_a)
    z = jnp.tanh(-log_a) * (a * a + 1.0)
    s = jnp.sqrt(z)
    return r, i, a, s


def _scan_tile(a, b, reverse, row):
    for k in (1, 2, 4):
        if not reverse:
            a_s, b_s, m = pltpu.roll(a, k, 0), pltpu.roll(b, k, 0), row >= k
        else:
            a_s, b_s, m = pltpu.roll(a, SUB - k, 0), pltpu.roll(b, SUB - k, 0), row < SUB - k
        b = jnp.where(m, a * b_s + b, b)
        a = jnp.where(m, a * a_s, a)
    return a, b


def _bcast_row(x, r):
    return jnp.broadcast_to(x[r:r + 1, :], (SUB, LANE))


def _lru_prepare(xb_ref, w4_ref, vec, xc_ref, af_ref, uf_ref, ab_ref, ub_ref, T):
    sp_f = _softplus_neg(vec[9:10, :])
    sp_b = _softplus_neg(vec[10:11, :])
    w4 = w4_ref[0]

    def body(c, carry):
        t0 = c * LRU_CH
        xc = _lru_conv(xb_ref, vec, t0, T)
        if xc_ref is not None:
            xc_ref[_rows(t0, LRU_CH), :] = xc
        pre = _mm(xc, w4)
        _, i, a, s = _lru_dir(pre[:, 0:128] + vec[5:6, :], pre[:, 128:256] + vec[6:7, :], sp_f)
        af_ref[_rows(t0, LRU_CH), :] = a
        uf_ref[_rows(t0, LRU_CH), :] = s * (i * xc)
        _, i, a, s = _lru_dir(pre[:, 256:384] + vec[7:8, :], pre[:, 384:512] + vec[8:9, :], sp_b)
        ab_ref[_rows(t0, LRU_CH), :] = a
        ub_ref[_rows(t0, LRU_CH), :] = s * (i * xc)
        return carry

    lax.fori_loop(0, T // LRU_CH, body, 0)


def _lru_scan(af_ref, uf_ref, ab_ref, ub_ref, T):
    nt = T // SUB
    row = lax.broadcasted_iota(jnp.int32, (SUB, LANE), 0)

    def body(j, carry):
        hf, hb = carry
        sf = _rows(j * SUB, SUB)
        sb = _rows((nt - 1 - j) * SUB, SUB)
        a, b = _scan_tile(af_ref[sf, :], uf_ref[sf, :], False, row)
        h = a * hf + b
        uf_ref[sf, :] = h
        hf = _bcast_row(h, SUB - 1)
        a, b = _scan_tile(ab_ref[sb, :], ub_ref[sb, :], True, row)
        h = a * hb + b
        ub_ref[sb, :] = h
        hb = _bcast_row(h, 0)
        return hf, hb

    z = jnp.zeros((SUB, LANE), F32)
    lax.fori_loop(0, nt, body, (z, z))


def _lru_fwd_call(proj, vec, w4):
    T = proj.shape[0]

    def body(xb_ref, gate_ref, vec_ref, w4_ref, y_ref, af_ref, uf_ref, ab_ref, ub_ref):
        vec = vec_ref[...]
        _lru_prepare(xb_ref, w4_ref, vec, None, af_ref, uf_ref, ab_ref, ub_ref, T)
        _lru_scan(af_ref, uf_ref, ab_ref, ub_ref, T)

        def out(c, carry):
            rows = _rows(c * LRU_CH, LRU_CH)
            gl, _ = _gelu_parts(gate_ref[rows, :])
            y_ref[rows, :] = (uf_ref[rows, :] + ub_ref[rows, :]) * gl
            return carry

        lax.fori_loop(0, T // LRU_CH, out, 0)

    return pl.pallas_call(
        body, name="lru_fwd", grid=(LRU_WIDTH // LANE,),
        in_specs=[_strip(T, lambda j: j), _strip(T, lambda j: j + 3),
                  pl.BlockSpec((16, LANE), lambda j: (0, j)),
                  pl.BlockSpec((1, LANE, 4 * LANE), lambda j: (j, 0, 0))],
        out_specs=_strip(T, lambda j: j, buffers=1),
        out_shape=jax.ShapeDtypeStruct((T, LRU_WIDTH), F32),
        scratch_shapes=[pltpu.VMEM((T, LANE), F32)] * 4,
        compiler_params=_cparams(("arbitrary",)),
    )(proj, proj, vec, w4)


def _store_strips(stage_ref, dp_ref, cols, sems):
    copies = [pltpu.make_async_copy(stage_ref.at[b], dp_ref.at[:, pl.ds(pl.multiple_of(c * LANE, LANE), LANE)], sems.at[b])
              for b, c in enumerate(cols)]
    for cp in copies:
        cp.start()
    for cp in copies:
        cp.wait()


def _lru_bwd_call(proj, dycat, vec, w4, after):
    T = proj.shape[0]
    nt = T // SUB
    nch = T // LRU_CH

    def body(xb_ref, gate_ref, dy_ref, vec_ref, w4_ref, after_ref, dp_ref, dvec_ref, dw4_ref,
             xc_ref, af_ref, hf_ref, ab_ref, hb_ref, dh_ref, stage_ref, sems):
        dxb_ref, dgate_ref = stage_ref.at[0], stage_ref.at[1]
        vec = vec_ref[...]
        _lru_prepare(xb_ref, w4_ref, vec, xc_ref, af_ref, hf_ref, ab_ref, hb_ref, T)
        _lru_scan(af_ref, hf_ref, ab_ref, hb_ref, T)

        def gate_bwd(c, carry):
            rows = _rows(c * LRU_CH, LRU_CH)
            gl, dgl = _gelu_parts(gate_ref[rows, :])
            dy = dy_ref[rows, :]
            dgate_ref[rows, :] = (dy * (hf_ref[rows, :] + hb_ref[rows, :]) * dgl).astype(dgate_ref.dtype)
            dh_ref[rows, :] = dy * gl
            return carry

        lax.fori_loop(0, nch, gate_bwd, 0)

        row = lax.broadcasted_iota(jnp.int32, (SUB, LANE), 0)

        def adj(j, carry):
            gf, a_next, gb, a_prev = carry
            tf = nt - 1 - j
            sf = _rows(tf * SUB, SUB)
            a_t = af_ref[sf, :]
            h_t = hf_ref[sf, :]
            coef = jnp.where(row == SUB - 1, a_next, pltpu.roll(a_t, SUB - 1, 0))
            ac, bc = _scan_tile(coef, dh_ref[sf, :], True, row)
            g = ac * gf + bc
            h_prev = hf_ref[_rows(jnp.maximum(tf - 1, 0) * SUB, SUB), :]
            h_prev = jnp.where(tf > 0, _bcast_row(h_prev, SUB - 1), 0.0)
            hs = jnp.where(row == 0, h_prev, pltpu.roll(h_t, 1, 0))
            af_ref[sf, :] = g * hs
            hf_ref[sf, :] = g
            gf = _bcast_row(g, 0)
            a_next = _bcast_row(a_t, 0)
            sb = _rows(j * SUB, SUB)
            a_t = ab_ref[sb, :]
            h_t = hb_ref[sb, :]
            coef = jnp.where(row == 0, a_prev, pltpu.roll(a_t, 1, 0))
            ac, bc = _scan_tile(coef, dh_ref[sb, :], False, row)
            g = ac * gb + bc
            h_next = hb_ref[_rows(jnp.minimum(j + 1, nt - 1) * SUB, SUB), :]
            h_next = jnp.where(j < nt - 1, _bcast_row(h_next, 0), 0.0)
            hs = jnp.where(row == SUB - 1, h_next, pltpu.roll(h_t, SUB - 1, 0))
            ab_ref[sb, :] = g * hs
            hb_ref[sb, :] = g
            gb = _bcast_row(g, SUB - 1)
            a_prev = _bcast_row(a_t, SUB - 1)
            return gf, a_next, gb, a_prev

        z = jnp.zeros((SUB, LANE), F32)
        lax.fori_loop(0, nt, adj, (z, z, z, z))

        sp_f = _softplus_neg(vec[9:10, :])
        sp_b = _softplus_neg(vec[10:11, :])
        w4 = w4_ref[0]
        dw4_ref[...] = jnp.zeros_like(dw4_ref)

        def one_dir(pre_a, pre_x, sp, xc, du, da):
            r, i, a, s = _lru_dir(pre_a, pre_x, sp)
            d_i = du * s * xc
            dxc = du * s * i
            d_s = du * i * xc
            d_log = da * a - d_s * (a * a) / s
            d_r = d_log * (-LRU_C) * sp
            d_sp = jnp.sum(d_log * (-LRU_C) * r, axis=0, keepdims=True)
            return d_r * r * (1.0 - r), d_i * i * (1.0 - i), dxc, d_sp

        def gates_bwd(c, carry):
            db, dspf, dspb = carry
            rows = _rows(c * LRU_CH, LRU_CH)
            xc = xc_ref[rows, :]
            pre = _mm(xc, w4)
            dpa_f, dpx_f, dxc_f, d_sp_f = one_dir(pre[:, 0:128] + vec[5:6, :], pre[:, 128:256] + vec[6:7, :],
                                                  sp_f, xc, hf_ref[rows, :], af_ref[rows, :])
            dpa_b, dpx_b, dxc_b, d_sp_b = one_dir(pre[:, 256:384] + vec[7:8, :], pre[:, 384:512] + vec[8:9, :],
                                                  sp_b, xc, hb_ref[rows, :], ab_ref[rows, :])
            dpre = jnp.concatenate([dpa_f, dpx_f, dpa_b, dpx_b], axis=1)
            dw4_ref[0] += _mm_tn(xc, dpre)
            dh_ref[rows, :] = dxc_f + dxc_b + _mm_nt(dpre, w4)
            return db + jnp.sum(dpre, axis=0, keepdims=True), dspf + d_sp_f, dspb + d_sp_b

        z1 = jnp.zeros((1, LANE), F32)
        db, dspf, dspb = lax.fori_loop(0, nch, gates_bwd, (jnp.zeros((1, 4 * LANE), F32), z1, z1))

        def conv_bwd(c, carry):
            t0 = c * LRU_CH
            rows = _rows(t0, LRU_CH)
            dwin = _window(dh_ref, t0, LRU_CH, T)
            xwin = _window(xb_ref, t0, LRU_CH, T)
            dxc = dh_ref[rows, :]
            dxb = jnp.zeros((LRU_CH, LANE), F32)
            out = []
            for j in range(CONV_WIDTH):
                off = j - CONV_WIDTH // 2
                dxb = dxb + _tap(dwin, -off, LRU_CH) * vec[j:j + 1, :]
                out.append(carry[j] + jnp.sum(dxc * _tap(xwin, off, LRU_CH), axis=0, keepdims=True))
            dxb_ref[rows, :] = dxb.astype(dxb_ref.dtype)
            out.append(carry[CONV_WIDTH] + jnp.sum(dxc, axis=0, keepdims=True))
            return tuple(out)

        dconv = lax.fori_loop(0, nch, conv_bwd, (z1,) * (CONV_WIDTH + 1))
        dlam_f = dspf * (-_sigmoid(-vec[9:10, :]))
        dlam_b = dspb * (-_sigmoid(-vec[10:11, :]))
        dvec_ref[...] = jnp.concatenate(
            list(dconv) + [db[:, 0:128], db[:, 128:256], db[:, 256:384], db[:, 384:512], dlam_f, dlam_b,
                           jnp.zeros((5, LANE), F32)], axis=0)
        j = pl.program_id(0)
        _store_strips(stage_ref, dp_ref, (j, j + 3), sems)

    ns = LRU_WIDTH // LANE
    return pl.pallas_call(
        body, name="lru_bwd", grid=(ns,),
        in_specs=[_strip(T, lambda j: j), _strip(T, lambda j: j + 3), _strip(T, lambda j: j),
                  pl.BlockSpec((16, LANE), lambda j: (0, j)),
                  pl.BlockSpec((1, LANE, 4 * LANE), lambda j: (j, 0, 0)),
                  pl.BlockSpec(memory_space=pl.ANY)],
        out_specs=[pl.BlockSpec(memory_space=pl.ANY),
                   pl.BlockSpec((16, LANE), lambda j: (0, j)),
                   pl.BlockSpec((1, LANE, 4 * LANE), lambda j: (j, 0, 0))],
        out_shape=[jax.ShapeDtypeStruct((T, IN_WIDTH), _BF),
                   jax.ShapeDtypeStruct((16, LRU_WIDTH), F32), jax.ShapeDtypeStruct((ns, LANE, 4 * LANE), F32)],
        scratch_shapes=[pltpu.VMEM((T, LANE), F32)] * 6 + [pltpu.VMEM((2, T, LANE), _BF), pltpu.SemaphoreType.DMA((2,))],
        compiler_params=_cparams(("arbitrary",)),
    )(proj, proj, dycat, vec, w4, after)


def _lru_vec(cw, cb, ba, bx, lam):
    return jnp.concatenate([cw, cb[None], ba[0:1], bx[0:1], ba[1:2], bx[1:2], lam, jnp.zeros((5, LRU_WIDTH), F32)], axis=0)


def _lru_w4(wa, wx):
    nl = wa.shape[0]
    w = jnp.stack([wa[:, 0], wx[:, 0], wa[:, 1], wx[:, 1]], axis=1)
    w = w.reshape(nl, 4, 3, 2, 64, 64)
    eye = jnp.eye(2, dtype=w.dtype)
    bd = w[:, :, :, :, :, None, :] * eye[None, None, None, :, None, :, None]
    bd = bd.reshape(nl, 4, 3, LANE, LANE)
    return bd.transpose(0, 2, 3, 1, 4).reshape(nl, 3, LANE, 4 * LANE).astype(_BF)


def _lru_unpack(dvec, dw4):
    def blocks(m):
        m = m.reshape(3, 2, 64, 2, 64)
        return jnp.stack([m[:, 0, :, 0, :], m[:, 1, :, 1, :]], axis=1).reshape(6, 64, 64)
    parts = [blocks(dw4[:, :, k * LANE:(k + 1) * LANE]) for k in range(4)]
    dwa = jnp.stack([parts[0], parts[2]])
    dwx = jnp.stack([parts[1], parts[3]])
    dba = jnp.stack([dvec[5], dvec[7]])
    dbx = jnp.stack([dvec[6], dvec[8]])
    return dvec[0:4], dvec[4], dwa, dba, dwx, dbx, dvec[9:11]


RC = 2 * RET_CHUNK


def _ret_tables(T):
    half = HEAD_DIM // 2
    pos = jnp.arange(T, dtype=F32)
    inv_freq = ROPE_BASE ** (-jnp.arange(half, dtype=F32) / half)
    ang = pos[:, None] * inv_freq[None, :]
    cos = jnp.tile(jnp.cos(ang), (1, 4))
    sin = jnp.tile(jnp.concatenate([-jnp.sin(ang), jnp.sin(ang)], axis=1), (1, 2))
    log_g = jnp.log1p(-jnp.exp2(-5.0 - jnp.arange(RET_HEADS, dtype=F32)))
    idx = jnp.arange(RC, dtype=F32)
    dec = jnp.exp(jnp.abs(idx[:, None] - idx[None, :]) * log_g[:, None, None])
    lg = jnp.repeat(log_g, HEAD_DIM).reshape(3, 1, LANE)
    col = idx[None, :, None]
    rtab = jnp.stack([jnp.exp((RC - 1 - col) * lg), jnp.exp(col * lg),
                      jnp.exp((col + 1.0) * lg), jnp.exp((RC - col) * lg)], axis=1)
    gch = jnp.broadcast_to(jnp.exp(RC * lg), (3, SUB, LANE))
    return cos, sin, dec, rtab, gch


def _swap32(x, lane):
    return jnp.where((lane & 32) == 0, pltpu.roll(x, LANE - 32, 1), pltpu.roll(x, 32, 1))


def _head_mean(x, m0, m1):
    s0 = jnp.sum(x * m0, axis=-1, keepdims=True)
    s1 = jnp.sum(x * m1, axis=-1, keepdims=True)
    return (s0 * m0 + s1 * m1) * (1.0 / HEAD_DIM)


def _ret_masks():
    lane = lax.broadcasted_iota(jnp.int32, (RC, LANE), 1)
    m0 = (lane < HEAD_DIM).astype(F32)
    r = lax.broadcasted_iota(jnp.int32, (LANE, LANE), 0) // HEAD_DIM
    c = lax.broadcasted_iota(jnp.int32, (LANE, LANE), 1) // HEAD_DIM
    return lane, m0, 1.0 - m0, (r == c).astype(F32)


def _ret_specs(T):
    const = lambda shape, imap: pl.BlockSpec(shape, imap)
    return [_strip(T, lambda j: j + 6), _strip(T, lambda j: j + 9), _strip(T, lambda j: j + 12),
            _strip(T, lambda j: j + 15),
            pl.BlockSpec((T, LANE), lambda j: (0, 0), pipeline_mode=pl.Buffered(1)),
            pl.BlockSpec((T, LANE), lambda j: (0, 0), pipeline_mode=pl.Buffered(1)),
            const((2, RC, RC), lambda j: (j, 0, 0)),
            const((1, 4, RC, LANE), lambda j: (j, 0, 0, 0)),
            const((1, SUB, LANE), lambda j: (j, 0, 0)),
            const((SUB, LANE), lambda j: (0, j))]


def _ret_fwd_call(proj, tables, gnw8):
    T = proj.shape[0]
    nc = T // RC
    cos, sin, dec, rtab, gch = tables

    def body(q_ref, k_ref, v_ref, g_ref, cos_ref, sin_ref, dec_ref, rtab_ref, gch_ref, gnw_ref, y_ref, stf_ref, kr_ref):
        lane, m0, m1, bd = _ret_masks()
        gch_v = gch_ref[0][0:1, :]
        gnw = gnw_ref[0:1, :]
        dkf, dkb, dqf, dqb = rtab_ref[0, 0], rtab_ref[0, 1], rtab_ref[0, 2], rtab_ref[0, 3]

        def rope(x, rows):
            return x * cos_ref[rows, :] + _swap32(x, lane) * sin_ref[rows, :]

        def pass_a(n, st):
            rows = _rows(n * RC, RC)
            stf_ref[n] = st
            kr = rope(k_ref[rows, :], rows) * (HEAD_DIM ** -0.5)
            kr_ref[rows, :] = kr
            return gch_v * st + _mm_tn(kr * dkf, v_ref[rows, :]) * bd

        _loop2(nc, pass_a, jnp.zeros((LANE, LANE), F32))

        def pass_b(i, stb):
            ns = [nc - 1 - 2 * i, nc - 2 - 2 * i]
            rows = [_rows(n * RC, RC) for n in ns]
            heads = ((0, m0), (1, m1))
            qr = [rope(q_ref[r, :], r) for r in rows]
            kr = [kr_ref[r, :] for r in rows]
            v = [v_ref[r, :] for r in rows]
            kv = [_mm_tn(kr[c] * dkb, v[c]) * bd for c in range(2)]
            stbs = [stb, gch_v * stb + kv[0]]
            s = [[_mm_nt(qr[c] * m, kr[c]) * dec_ref[h] for h, m in heads] for c in range(2)]
            o = [_mm(qr[c] * dqf, stf_ref[ns[c]]) + _mm(qr[c] * dqb, stbs[c]) for c in range(2)]
            o = [o[c] + _mm(s[c][0], v[c] * m0) + _mm(s[c][1], v[c] * m1) for c in range(2)]
            oc = [o_ - _head_mean(o_, m0, m1) for o_ in o]
            on = [oc_ * lax.rsqrt(_head_mean(oc_ * oc_, m0, m1) + GN_EPS) for oc_ in oc]
            for c in range(2):
                g = g_ref[rows[c], :]
                y_ref[rows[c], :] = (g * _sigmoid(g)) * (on[c] * gnw)
            return gch_v * stbs[1] + kv[1]

        assert nc % 2 == 0
        lax.fori_loop(0, nc // 2, pass_b, jnp.zeros((LANE, LANE), F32))

    return pl.pallas_call(
        body, name="ret_fwd", grid=(RET_WIDTH // LANE,),
        in_specs=_ret_specs(T),
        out_specs=_strip(T, lambda j: j, buffers=1),
        out_shape=jax.ShapeDtypeStruct((T, RET_WIDTH), F32),
        scratch_shapes=[pltpu.VMEM((nc, LANE, LANE), F32), pltpu.VMEM((T, LANE), F32)],
        compiler_params=_cparams(("arbitrary",)),
    )(proj, proj, proj, proj, cos, sin, dec, rtab, gch, gnw8)


def _ret_bwd_call(proj, dycat, tables, gnw8, dp):
    T = proj.shape[0]
    nc = T // RC
    cos, sin, dec, rtab, gch = tables

    def body(q_ref, k_ref, v_ref, g_ref, cos_ref, sin_ref, dec_ref, rtab_ref, gch_ref, gnw_ref, dy_ref, dp_in_ref,
             dp_out_ref, dgnw_ref, stf_ref, dstb_ref, dkr_ref, dv_ref, dp_ref, kr_ref, sems):
        lane, m0, m1, bd = _ret_masks()
        gch_v = gch_ref[0][0:1, :]
        gnw = gnw_ref[0:1, :]
        dkf, dkb, dqf, dqb = rtab_ref[0, 0], rtab_ref[0, 1], rtab_ref[0, 2], rtab_ref[0, 3]
        scale = HEAD_DIM ** -0.5
        zst = jnp.zeros((LANE, LANE), F32)

        def rope(x, rows):
            return x * cos_ref[rows, :] + _swap32(x, lane) * sin_ref[rows, :]

        def rope_t(d, rows):
            return d * cos_ref[rows, :] + _swap32(d * sin_ref[rows, :], lane)

        def pass_a(n, st):
            rows = _rows(n * RC, RC)
            stf_ref[n] = st
            kr = rope(k_ref[rows, :], rows) * scale
            kr_ref[rows, :] = kr
            return gch_v * st + _mm_tn(kr * dkf, v_ref[rows, :]) * bd

        _loop2(nc, pass_a, zst)

        def pass_b(i, carry):
            stb, d_f, dgnw = carry
            two = range(2)
            heads = ((0, m0), (1, m1))
            ns = [nc - 1 - 2 * i, nc - 2 - 2 * i]
            rows = [_rows(n * RC, RC) for n in ns]
            qr = [rope(q_ref[r, :], r) for r in rows]
            kr = [kr_ref[r, :] for r in rows]
            v = [v_ref[r, :] for r in rows]
            stf = [stf_ref[n] for n in ns]
            kvb = [_mm_tn(kr[c] * dkb, v[c]) * bd for c in two]
            stbs = [stb, gch_v * stb + kvb[0]]
            qf = [qr[c] * dqf for c in two]
            qb = [qr[c] * dqb for c in two]
            s = [[_mm_nt(qr[c] * m, kr[c]) * dec_ref[h] for h, m in heads] for c in two]
            o = [_mm(qf[c], stf[c]) + _mm(qb[c], stbs[c]) for c in two]
            o = [o[c] + _mm(s[c][0], v[c] * m0) + _mm(s[c][1], v[c] * m1) for c in two]
            oc = [o_ - _head_mean(o_, m0, m1) for o_ in o]
            rstd = [lax.rsqrt(_head_mean(oc_ * oc_, m0, m1) + GN_EPS) for oc_ in oc]
            on = [oc[c] * rstd[c] for c in two]
            do = []
            for c in two:
                g = g_ref[rows[c], :]
                sg = _sigmoid(g)
                dy = dy_ref[rows[c], :]
                dp_ref[3, rows[c], :] = (dy * (on[c] * gnw) * (sg * (1.0 + g * (1.0 - sg)))).astype(dp_ref.dtype)
                t = dy * (g * sg)
                dgnw = dgnw + jnp.sum(t * on[c], axis=0, keepdims=True)
                don = t * gnw
                do.append(rstd[c] * (don - _head_mean(don, m0, m1) - on[c] * _head_mean(don * on[c], m0, m1)))
            dstf = [_mm_tn(qf[c], do[c]) * bd for c in two]
            dfs = [d_f, dstf[0] + gch_v * d_f]
            ds = [[_mm_nt(do[c] * m, v[c]) * dec_ref[h] for h, m in heads] for c in two]
            dqr = [_mm_nt(do[c], stf[c]) * dqf + _mm_nt(do[c], stbs[c]) * dqb
                   + _mm(ds[c][0], kr[c] * m0) + _mm(ds[c][1], kr[c] * m1) for c in two]
            dkr = [_mm_nt(v[c], dfs[c]) * dkf + _mm_tn(ds[c][0], qr[c] * m0) + _mm_tn(ds[c][1], qr[c] * m1) for c in two]
            dv = [_mm(kr[c] * dkf, dfs[c]) + _mm_tn(s[c][0], do[c] * m0) + _mm_tn(s[c][1], do[c] * m1) for c in two]
            for c in two:
                dp_ref[0, rows[c], :] = rope_t(dqr[c], rows[c]).astype(dp_ref.dtype)
                dkr_ref[rows[c], :] = dkr[c]
                dv_ref[rows[c], :] = dv[c]
                dstb_ref[ns[c]] = _mm_tn(qb[c], do[c]) * bd
            return gch_v * stbs[1] + kvb[1], dstf[1] + gch_v * dfs[1], dgnw

        assert nc % 2 == 0
        _, _, dgnw = lax.fori_loop(0, nc // 2, pass_b, (zst, zst, jnp.zeros((1, LANE), F32)))
        dgnw_ref[...] = jnp.concatenate([dgnw, jnp.zeros((SUB - 1, LANE), F32)], axis=0)

        def pass_c(n, d_b):
            rows = _rows(n * RC, RC)
            kr = kr_ref[rows, :]
            v = v_ref[rows, :]
            dkr = dkr_ref[rows, :] + _mm_nt(v, d_b) * dkb
            dp_ref[1, rows, :] = (rope_t(dkr, rows) * scale).astype(dp_ref.dtype)
            dp_ref[2, rows, :] = (dv_ref[rows, :] + _mm(kr * dkb, d_b)).astype(dp_ref.dtype)
            return dstb_ref[n] + gch_v * d_b

        _loop2(nc, pass_c, zst)
        j = pl.program_id(0)
        _store_strips(dp_ref, dp_out_ref, (j + 6, j + 9, j + 12, j + 15), sems)

    n_in = len(_ret_specs(T)) + 1
    return pl.pallas_call(
        body, name="ret_bwd", grid=(RET_WIDTH // LANE,),
        in_specs=_ret_specs(T) + [_strip(T, lambda j: j + 3), pl.BlockSpec(memory_space=pl.ANY)],
        out_specs=[pl.BlockSpec(memory_space=pl.ANY), pl.BlockSpec((SUB, LANE), lambda j: (0, j))],
        out_shape=[jax.ShapeDtypeStruct(dp.shape, dp.dtype), jax.ShapeDtypeStruct((SUB, RET_WIDTH), F32)],
        scratch_shapes=[pltpu.VMEM((nc, LANE, LANE), F32), pltpu.VMEM((nc, LANE, LANE), F32),
                        pltpu.VMEM((T, LANE), F32), pltpu.VMEM((T, LANE), F32),
                        pltpu.VMEM((4, T, LANE), _BF), pltpu.VMEM((T, LANE), F32), pltpu.SemaphoreType.DMA((4,))],
        input_output_aliases={n_in: 0},
        compiler_params=_cparams(("arbitrary",)),
    )(proj, proj, proj, proj, cos, sin, dec, rtab, gch, gnw8, dycat, dp)


NA_Q = 2 * GRID_W
NA_WROWS = 10
NA_K = NA_WROWS * GRID_W
NA_CHUNKS = NA_K // LANE
NA_UNROLL = 4
NA_TYPES = 5
_ONEHOT_PRECISION = lax.Precision.HIGH


def _na_onehots(rows_n):
    reps = [(0, 0), (2, 0), (4, 0), (rows_n - 4, rows_n - NA_WROWS), (rows_n - 2, rows_n - NA_WROWS)]
    rm = np.zeros((NA_TYPES, 2, NA_WROWS, 2 * NA_KH - 1), np.float32)
    for t, (r, ws) in enumerate(reps):
        for qh in range(2):
            qrow = r + qh
            rstart = min(max(qrow - NA_KH // 2, 0), rows_n - NA_KH)
            for kh in range(NA_WROWS):
                krow = ws + kh
                if rstart <= krow < rstart + NA_KH:
                    rm[t, qh, kh, krow - qrow + NA_KH - 1] = 1.0
    cm = np.zeros((GRID_W, GRID_W, 2 * NA_KW - 1), np.float32)
    for qc in range(GRID_W):
        cstart = min(max(qc - NA_KW // 2, 0), GRID_W - NA_KW)
        for kc in range(cstart, cstart + NA_KW):
            cm[qc, kc, kc - qc + NA_KW - 1] = 1.0
    rm2 = rm.reshape(NA_TYPES, 2, NA_CHUNKS, 2, 2 * NA_KH - 1)
    cm2 = np.zeros((GRID_W, LANE, 2, 2 * NA_KW - 1), np.float32)
    for z in range(2):
        cm2[:, z * GRID_W:(z + 1) * GRID_W, z, :] = cm
    return rm2, cm2


def _na_bias_tables(rpb, rows_n):
    rm, cm = _na_onehots(rows_n)
    val = jnp.einsum("hab,tqpza,xkzb->htpqxk", rpb, rm, cm, precision=_ONEHOT_PRECISION)
    valid = np.einsum("tqpz,xkz->tpqxk", rm.sum(-1), cm.sum(-1)) > 0.5
    return jnp.where(valid[None], val, NEG).reshape(2, 2, NA_TYPES, NA_CHUNKS, NA_Q, LANE)


def _na_bias_grad(dtab, rows_n):
    rm, cm = _na_onehots(rows_n)
    d6 = dtab.reshape(NA_HEADS, NA_TYPES, NA_CHUNKS, 2, GRID_W, LANE)
    return jnp.einsum("htpqxk,tqpza,xkzb->hab", d6, rm, cm, precision=_ONEHOT_PRECISION)


def _na_bias(b_ref, h, typ):
    return jnp.concatenate([b_ref[0, h, typ, c] for c in range(NA_CHUNKS)], axis=1)


def _na_step(p, npairs, rows_n):
    ws = jnp.clip(2 * p - NA_KH // 2, 0, rows_n - NA_WROWS)
    koff = pl.multiple_of(ws * GRID_W, LANE)
    typ = jnp.where(p == 0, 0, jnp.where(p == 1, 1, jnp.where(p == npairs - 2, 3, jnp.where(p == npairs - 1, 4, 2))))
    return _rows(p * NA_Q, NA_Q), pl.ds(koff, NA_K), typ


def _na_fwd_call(proj, btab):
    T = proj.shape[0]
    npairs, rows_n = T // NA_Q, T // GRID_W

    def body(q_ref, k_ref, v_ref, b_ref, o_ref):
        lane = lax.broadcasted_iota(jnp.int32, (NA_Q, LANE), 1)
        m0 = (lane < HEAD_DIM).astype(F32)
        m1 = 1.0 - m0

        def steps(i, carry):
            idx = [_na_step(NA_UNROLL * i + u, npairs, rows_n) for u in range(NA_UNROLL)]
            chains = [(u, h, m) for u in range(NA_UNROLL) for h, m in ((0, m0), (1, m1))]
            kws = [k_ref[krows, :].astype(_BF) for _, krows, _ in idx]
            vws = [v_ref[krows, :].astype(_BF) for _, krows, _ in idx]
            s = [_mm_nt(q_ref[idx[u][0], :] * m, kws[u]) for u, h, m in chains]
            s = [s_ * (HEAD_DIM ** -0.5) + _na_bias(b_ref, h, idx[u][2]) for s_, (u, h, m) in zip(s, chains)]
            e = [jnp.exp(s_ - jnp.max(s_, axis=-1, keepdims=True)) for s_ in s]
            pr = [e_ / jnp.sum(e_, axis=-1, keepdims=True) for e_ in e]
            ov = [_mm(pr_, vws[u]) * m for pr_, (u, h, m) in zip(pr, chains)]
            for u in range(NA_UNROLL):
                o_ref[idx[u][0], :] = ov[2 * u] + ov[2 * u + 1]
            return carry

        lax.fori_loop(0, npairs // NA_UNROLL, steps, 0)

    return pl.pallas_call(
        body, name="na_fwd", grid=(NA_WIDTH // LANE,),
        in_specs=[_strip(T, lambda j: j + 18), _strip(T, lambda j: j + 20), _strip(T, lambda j: j + 22),
                  pl.BlockSpec((1, 2, NA_TYPES, NA_CHUNKS, NA_Q, LANE), lambda j: (j, 0, 0, 0, 0, 0))],
        out_specs=_strip(T, lambda j: j, buffers=1),
        out_shape=jax.ShapeDtypeStruct((T, NA_WIDTH), F32),
        compiler_params=_cparams(("arbitrary",)),
    )(proj, proj, proj, btab)


def _na_bwd_call(proj, dycat, btab, dp):
    T = proj.shape[0]
    npairs, rows_n = T // NA_Q, T // GRID_W
    scale = HEAD_DIM ** -0.5

    def body(q_ref, k_ref, v_ref, do_ref, b_ref, dp_in_ref, dp_out_ref, db_ref, dka_ref, dva_ref, stage_ref, sems):
        dq_ref = stage_ref.at[0]
        lane = lax.broadcasted_iota(jnp.int32, (NA_Q, LANE), 1)
        m0 = (lane < HEAD_DIM).astype(F32)
        m1 = 1.0 - m0
        dka_ref[...] = jnp.zeros_like(dka_ref)
        dva_ref[...] = jnp.zeros_like(dva_ref)
        db_ref[...] = jnp.zeros_like(db_ref)

        def steps(i, carry):
            idx = [_na_step(NA_UNROLL * i + u, npairs, rows_n) for u in range(NA_UNROLL)]
            chains = [(u, h, m) for u in range(NA_UNROLL) for h, m in ((0, m0), (1, m1))]
            kws = [k_ref[krows, :].astype(_BF) for _, krows, _ in idx]
            vws = [v_ref[krows, :].astype(_BF) for _, krows, _ in idx]
            qm = [(q_ref[idx[u][0], :] * m).astype(_BF) for u, h, m in chains]
            dom = [(do_ref[idx[u][0], :] * m).astype(_BF) for u, h, m in chains]
            s = [_mm_nt(qm_, kws[u]) for qm_, (u, h, m) in zip(qm, chains)]
            dpr = [_mm_nt(dom_, vws[u]) for dom_, (u, h, m) in zip(dom, chains)]
            s = [s_ * scale + _na_bias(b_ref, h, idx[u][2]) for s_, (u, h, m) in zip(s, chains)]
            e = [jnp.exp(s_ - jnp.max(s_, axis=-1, keepdims=True)) for s_ in s]
            pr = [e_ / jnp.sum(e_, axis=-1, keepdims=True) for e_ in e]
            ds = [pr_ * (dpr_ - jnp.sum(pr_ * dpr_, axis=-1, keepdims=True)) for pr_, dpr_ in zip(pr, dpr)]
            dsb = [(ds_ * scale).astype(_BF) for ds_ in ds]
            dq = [_mm(dsb_, kws[u]) * m for dsb_, (u, h, m) in zip(dsb, chains)]
            dk = [_mm_tn(dsb_, qm_) for dsb_, qm_ in zip(dsb, qm)]
            dv = [_mm_tn(pr_, dom_) for pr_, dom_ in zip(pr, dom)]
            for ds_, (u, h, m) in zip(ds, chains):
                for c in range(NA_CHUNKS):
                    db_ref[0, h, idx[u][2], c] += ds_[:, c * LANE:(c + 1) * LANE]
            for u in range(NA_UNROLL):
                qrows, krows, _ = idx[u]
                dq_ref[qrows, :] = (dq[2 * u] + dq[2 * u + 1]).astype(dq_ref.dtype)
                dka_ref[krows, :] += dk[2 * u] + dk[2 * u + 1]
                dva_ref[krows, :] += dv[2 * u] + dv[2 * u + 1]
            return carry

        lax.fori_loop(0, npairs // NA_UNROLL, steps, 0)
        stage_ref[1] = dka_ref[...].astype(stage_ref.dtype)
        stage_ref[2] = dva_ref[...].astype(stage_ref.dtype)
        j = pl.program_id(0)
        _store_strips(stage_ref, dp_out_ref, (j + 18, j + 20, j + 22), sems)

    tab = pl.BlockSpec((1, 2, NA_TYPES, NA_CHUNKS, NA_Q, LANE), lambda j: (j, 0, 0, 0, 0, 0))
    return pl.pallas_call(
        body, name="na_bwd", grid=(NA_WIDTH // LANE,),
        in_specs=[_strip(T, lambda j: j + 18), _strip(T, lambda j: j + 20), _strip(T, lambda j: j + 22),
                  _strip(T, lambda j: j + 6), tab, pl.BlockSpec(memory_space=pl.ANY)],
        out_specs=[pl.BlockSpec(memory_space=pl.ANY), tab],
        out_shape=[jax.ShapeDtypeStruct(dp.shape, dp.dtype),
                   jax.ShapeDtypeStruct((2, 2, NA_TYPES, NA_CHUNKS, NA_Q, LANE), F32)],
        scratch_shapes=[pltpu.VMEM((T, LANE), F32), pltpu.VMEM((T, LANE), F32),
                        pltpu.VMEM((3, T, LANE), _BF), pltpu.SemaphoreType.DMA((3,))],
        input_output_aliases={5: 0},
        compiler_params=_cparams(("arbitrary",)),
    )(proj, proj, proj, dycat, btab, dp)


W_BLK = IN_WIDTH // N_DEV
MXU_W = 256
N_BLK = 4 * MXU_W
N_STEPS = IN_WIDTH // N_BLK
TM = 512


def _ln_fwd(z, g, b):
    zc = z - jnp.mean(z, axis=-1, keepdims=True)
    var = jnp.mean(zc * zc, axis=-1, keepdims=True)
    return zc * lax.rsqrt(var + LN_EPS) * g + b


def _ln_bwd(dy, z, g):
    zc = z - jnp.mean(z, axis=-1, keepdims=True)
    rstd = lax.rsqrt(jnp.mean(zc * zc, axis=-1, keepdims=True) + LN_EPS)
    xhat = zc * rstd
    dxh = dy * g
    dz = rstd * (dxh - jnp.mean(dxh, axis=-1, keepdims=True) - xhat * jnp.mean(dxh * xhat, axis=-1, keepdims=True))
    return dz, dy * xhat


def _row_tile(T):
    return 1024 if T % 1024 == 0 else TM


def _halves(n):
    return (pl.ds(0, n // 2), pl.ds(n // 2, n // 2))


def _inproj_call(xb, w, after):
    T = xb.shape[0]
    tm = _row_tile(T)

    def body(x_ref, w_ref, after_ref, o_ref):
        o_ref[...] = _mm(x_ref[...], w_ref[...])

    return pl.pallas_call(
        body, name="inproj", grid=(T // tm, N_STEPS),
        in_specs=[pl.BlockSpec((tm, D_MODEL), lambda i, n: (i, 0)),
                  pl.BlockSpec((D_MODEL, N_BLK), lambda i, n: (0, n)),
                  pl.BlockSpec(memory_space=pl.ANY)],
        out_specs=pl.BlockSpec((tm, N_BLK), lambda i, n: (i, n)),
        out_shape=jax.ShapeDtypeStruct((T, IN_WIDTH), F32),
        compiler_params=_cparams(("parallel", "arbitrary")),
    )(xb, w, after)


def _vec_spec():
    return pl.BlockSpec((1, D_MODEL), lambda *_: (0, 0))


def _outproj_ln_call(y_lru, y_ret, y_na, x, w, g, b, after):
    T = x.shape[0]

    def body(yl_ref, yr_ref, yn_ref, x_ref, w_ref, g_ref, b_ref, after_ref, z_ref, x1_ref, x1b_ref, yc_ref):
        yc_ref[:, 0:LRU_WIDTH] = yl_ref[...].astype(yc_ref.dtype)
        yc_ref[:, LRU_WIDTH:LRU_WIDTH + RET_WIDTH] = yr_ref[...].astype(yc_ref.dtype)
        yc_ref[:, LRU_WIDTH + RET_WIDTH:] = yn_ref[...].astype(yc_ref.dtype)
        z = ALPHA * x_ref[...] + _mm(yc_ref[...], w_ref[...])
        z_ref[...] = z
        x1 = _ln_fwd(z, g_ref[...], b_ref[...])
        x1_ref[...] = x1
        x1b_ref[...] = x1.astype(x1b_ref.dtype)

    row = lambda w_: pl.BlockSpec((TM, w_), lambda i: (i, 0))
    return pl.pallas_call(
        body, name="outproj_ln", grid=(T // TM,),
        in_specs=[row(LRU_WIDTH), row(RET_WIDTH), row(NA_WIDTH), row(D_MODEL),
                  pl.BlockSpec((D_MODEL, D_MODEL), lambda i: (0, 0)), _vec_spec(), _vec_spec(),
                  pl.BlockSpec(memory_space=pl.ANY)],
        out_specs=[row(D_MODEL)] * 4,
        out_shape=[jax.ShapeDtypeStruct((T, D_MODEL), F32), jax.ShapeDtypeStruct((T, D_MODEL), F32),
                   jax.ShapeDtypeStruct((T, D_MODEL), _BF), jax.ShapeDtypeStruct((T, D_MODEL), _BF)],
        compiler_params=_cparams(("parallel",)),
    )(y_lru, y_ret, y_na, x, w, g, b, after)


def _ffn_ln_call(x1, x1b, wg, wu, wd, g, b):
    T = x1.shape[0]

    def body(x_ref, xb_ref, wg_ref, wu_ref, wd_ref, g_ref, b_ref, z_ref, x2_ref, x2b_ref, gp_ref, up_ref, acc_ref):
        n = pl.program_id(1)

        @pl.when(n == 0)
        def _():
            acc_ref[...] = jnp.zeros_like(acc_ref)

        r0, r1 = _halves(TM)

        def pre(rows):
            xb = xb_ref[rows, :]
            return _mm(xb, wg_ref[...]), _mm(xb, wu_ref[...])

        def act(rows, gp, up):
            gp_ref[rows, :] = gp.astype(gp_ref.dtype)
            up_ref[rows, :] = up.astype(up_ref.dtype)
            return (gp * _sigmoid(gp) * up).astype(_BF)

        gp0, up0 = pre(r0)
        hid0 = act(r0, gp0, up0)
        gp1, up1 = pre(r1)
        acc_ref[r0, :] += _mm(hid0, wd_ref[...])
        hid1 = act(r1, gp1, up1)
        acc_ref[r1, :] += _mm(hid1, wd_ref[...])

        @pl.when(n == N_STEPS - 1)
        def _():
            z = ALPHA * x_ref[...] + acc_ref[...]
            z_ref[...] = z
            x2 = _ln_fwd(z, g_ref[...], b_ref[...])
            x2_ref[...] = x2
            x2b_ref[...] = x2.astype(x2b_ref.dtype)

    row = pl.BlockSpec((TM, D_MODEL), lambda i, n: (i, 0))
    return pl.pallas_call(
        body, name="ffn_ln", grid=(T // TM, N_STEPS),
        in_specs=[row, row,
                  pl.BlockSpec((D_MODEL, N_BLK), lambda i, n: (0, n)),
                  pl.BlockSpec((D_MODEL, N_BLK), lambda i, n: (0, n)),
                  pl.BlockSpec((N_BLK, D_MODEL), lambda i, n: (n, 0)), _vec_spec(), _vec_spec()],
        out_specs=[row] * 3 + [pl.BlockSpec((TM, N_BLK), lambda i, n: (i, n))] * 2,
        out_shape=[jax.ShapeDtypeStruct((T, D_MODEL), F32), jax.ShapeDtypeStruct((T, D_MODEL), F32),
                   jax.ShapeDtypeStruct((T, D_MODEL), _BF),
                   jax.ShapeDtypeStruct((T, IN_WIDTH), _BF), jax.ShapeDtypeStruct((T, IN_WIDTH), _BF)],
        scratch_shapes=[pltpu.VMEM((TM, D_MODEL), F32)],
        compiler_params=_cparams(("parallel", "arbitrary")),
    )(x1, x1b, wg, wu, wd, g, b)


def _loss_call(y, t):
    T = y.shape[0]

    def body(y_ref, t_ref, dy_ref, loss_ref):
        @pl.when(pl.program_id(0) == 0)
        def _():
            loss_ref[...] = jnp.zeros_like(loss_ref)

        err = y_ref[...] - t_ref[...]
        dy_ref[...] = err * (1.0 / D_MODEL)
        part = 0.5 * jnp.sum(jnp.mean(err * err, axis=-1, keepdims=True), axis=0, keepdims=True)
        loss_ref[...] += jnp.broadcast_to(part, loss_ref.shape)

    row = pl.BlockSpec((TM, D_MODEL), lambda i: (i, 0))
    return pl.pallas_call(
        body, name="loss", grid=(T // TM,),
        in_specs=[row, row],
        out_specs=[row, pl.BlockSpec((SUB, LANE), lambda i: (0, 0))],
        out_shape=[jax.ShapeDtypeStruct((T, D_MODEL), F32), jax.ShapeDtypeStruct((SUB, LANE), F32)],
        compiler_params=_cparams(("arbitrary",)),
    )(y, t)


def _ffn_bwd_call(dx2, z2, gpb, upb, wg, wu, wd, g, after):
    T = dx2.shape[0]

    def body(dx2_ref, z_ref, gp_ref, up_ref, wg_ref, wu_ref, wd_ref, g_ref, after_ref,
             dx1_ref, dgp_ref, dup_ref, hid_ref, dzb_ref, dln_ref, acc_ref):
        i, n = pl.program_id(0), pl.program_id(1)

        @pl.when((i == 0) & (n == 0))
        def _():
            dln_ref[...] = jnp.zeros_like(dln_ref)

        @pl.when(n == 0)
        def _():
            dy = dx2_ref[...]
            dz, dg_rows = _ln_bwd(dy, z_ref[...], g_ref[...])
            dzb_ref[...] = dz.astype(dzb_ref.dtype)
            acc_ref[...] = ALPHA * dz
            dln_ref[0:1, :] += jnp.sum(dg_rows, axis=0, keepdims=True)
            dln_ref[1:2, :] += jnp.sum(dy, axis=0, keepdims=True)

        r0, r1 = _halves(TM)

        def grads(rows, dhid):
            gp = gp_ref[rows, :].astype(F32)
            up = up_ref[rows, :].astype(F32)
            sg = _sigmoid(gp)
            act = gp * sg
            hid_ref[rows, :] = (act * up).astype(hid_ref.dtype)
            dup = (dhid * act).astype(_BF)
            dgp = (dhid * up * (sg * (1.0 + gp * (1.0 - sg)))).astype(_BF)
            dgp_ref[rows, :] = dgp.astype(dgp_ref.dtype)
            dup_ref[rows, :] = dup.astype(dup_ref.dtype)
            return dgp, dup

        dhid0 = _mm_nt(dzb_ref[r0, :], wd_ref[...])
        dhid1 = _mm_nt(dzb_ref[r1, :], wd_ref[...])
        dgp0, dup0 = grads(r0, dhid0)
        acc_ref[r0, :] += _mm_nt(dgp0, wg_ref[...]) + _mm_nt(dup0, wu_ref[...])
        dgp1, dup1 = grads(r1, dhid1)
        acc_ref[r1, :] += _mm_nt(dgp1, wg_ref[...]) + _mm_nt(dup1, wu_ref[...])

        @pl.when(n == N_STEPS - 1)
        def _():
            dx1_ref[...] = acc_ref[...]

    row = pl.BlockSpec((TM, D_MODEL), lambda i, n: (i, 0))
    blk = pl.BlockSpec((TM, N_BLK), lambda i, n: (i, n))
    return pl.pallas_call(
        body, name="ffn_bwd", grid=(T // TM, N_STEPS),
        in_specs=[row, row, blk, blk,
                  pl.BlockSpec((D_MODEL, N_BLK), lambda i, n: (0, n)),
                  pl.BlockSpec((D_MODEL, N_BLK), lambda i, n: (0, n)),
                  pl.BlockSpec((N_BLK, D_MODEL), lambda i, n: (n, 0)), _vec_spec(),
                  pl.BlockSpec(memory_space=pl.ANY)],
        out_specs=[row, blk, blk, blk, row, pl.BlockSpec((SUB, D_MODEL), lambda i, n: (0, 0))],
        out_shape=[jax.ShapeDtypeStruct((T, D_MODEL), F32),
                   jax.ShapeDtypeStruct((T, IN_WIDTH), _BF), jax.ShapeDtypeStruct((T, IN_WIDTH), _BF),
                   jax.ShapeDtypeStruct((T, IN_WIDTH), _BF), jax.ShapeDtypeStruct((T, D_MODEL), _BF),
                   jax.ShapeDtypeStruct((SUB, D_MODEL), F32)],
        scratch_shapes=[pltpu.VMEM((TM, D_MODEL), F32)],
        compiler_params=_cparams(("arbitrary", "arbitrary")),
    )(dx2, z2, gpb, upb, wg, wu, wd, g, after)


def _outproj_bwd_call(dx1, z1, w, g):
    T = dx1.shape[0]

    def body(dx_ref, z_ref, w_ref, g_ref, dzb_ref, dyc_ref, dres_ref, dln_ref):
        @pl.when(pl.program_id(0) == 0)
        def _():
            dln_ref[...] = jnp.zeros_like(dln_ref)

        dy = dx_ref[...]
        dz, dg_rows = _ln_bwd(dy, z_ref[...], g_ref[...])
        dzb_ref[...] = dz.astype(dzb_ref.dtype)
        dres_ref[...] = ALPHA * dz
        dyc_ref[...] = _mm_nt(dz, w_ref[...])
        dln_ref[0:1, :] += jnp.sum(dg_rows, axis=0, keepdims=True)
        dln_ref[1:2, :] += jnp.sum(dy, axis=0, keepdims=True)

    row = pl.BlockSpec((TM, D_MODEL), lambda i: (i, 0))
    return pl.pallas_call(
        body, name="outproj_bwd", grid=(T // TM,),
        in_specs=[row, row, pl.BlockSpec((D_MODEL, D_MODEL), lambda i: (0, 0)), _vec_spec()],
        out_specs=[row, row, row, pl.BlockSpec((SUB, D_MODEL), lambda i: (0, 0))],
        out_shape=[jax.ShapeDtypeStruct((T, D_MODEL), _BF), jax.ShapeDtypeStruct((T, D_MODEL), F32),
                   jax.ShapeDtypeStruct((T, D_MODEL), F32), jax.ShapeDtypeStruct((SUB, D_MODEL), F32)],
        compiler_params=_cparams(("arbitrary",)),
    )(dx1, z1, w, g)


def _inproj_bwd_call(dres, dp, w):
    T = dres.shape[0]

    def body(dres_ref, dp_ref, w_ref, dx_ref):
        dx_ref[...] = dres_ref[...] + _mm_nt(dp_ref[...], w_ref[...])

    row = pl.BlockSpec((TM, D_MODEL), lambda i: (i, 0))
    return pl.pallas_call(
        body, name="inproj_bwd", grid=(T // TM,),
        in_specs=[row, pl.BlockSpec((TM, IN_WIDTH), lambda i: (i, 0)),
                  pl.BlockSpec((D_MODEL, IN_WIDTH), lambda i: (0, 0), pipeline_mode=pl.Buffered(1))],
        out_specs=row,
        out_shape=jax.ShapeDtypeStruct((T, D_MODEL), F32),
        compiler_params=_cparams(("parallel",)),
    )(dres, dp, w)


def _tn_cols_call(a, b, name):
    T, ka = a.shape
    n = b.shape[1]

    def body(a_ref, b_ref, o_ref):
        o_ref[...] = _mm_tn(a_ref[...], b_ref[...]).astype(o_ref.dtype)

    return pl.pallas_call(
        body, name=name, grid=(n // N_BLK,),
        in_specs=[pl.BlockSpec((T, ka), lambda j: (0, 0), pipeline_mode=pl.Buffered(1)),
                  pl.BlockSpec((T, N_BLK), lambda j: (0, j))],
        out_specs=pl.BlockSpec((ka, N_BLK), lambda j: (0, j)),
        out_shape=jax.ShapeDtypeStruct((ka, n), _BF),
        compiler_params=_cparams(("parallel",)),
    )(a, b)


def _tn_rows_call(a, b, kb, name):
    T, ka = a.shape
    n = b.shape[1]

    def body(a_ref, b_ref, o_ref):
        o_ref[...] = _mm_tn(a_ref[...], b_ref[...]).astype(o_ref.dtype)

    return pl.pallas_call(
        body, name=name, grid=(ka // kb,),
        in_specs=[pl.BlockSpec((T, kb), lambda r: (0, r)),
                  pl.BlockSpec((T, n), lambda r: (0, 0), pipeline_mode=pl.Buffered(1))],
        out_specs=pl.BlockSpec((kb, n), lambda r: (r, 0)),
        out_shape=jax.ShapeDtypeStruct((ka, n), _BF),
        compiler_params=_cparams(("parallel",)),
    )(a, b)


def _me():
    return lax.axis_index("x"), lax.axis_index("y"), lax.axis_index("c")


def _flip(k):
    x, y, c = _me()
    return (1 - x if k & 4 else x, 1 - y if k & 2 else y, 1 - c if k & 1 else c)


def _dev_index(pos):
    return 4 * pos[0] + 2 * pos[1] + pos[2]


_HBM = pl.BlockSpec(memory_space=pltpu.HBM)
_SEM = pl.BlockSpec(memory_space=pltpu.SEMAPHORE)


def _land_shape(shape, mode):
    if mode == "all":
        return (N_DEV,) + shape
    if mode == "cols":
        return (shape[0], N_DEV * shape[1])
    if mode == "blk":
        return shape
    assert mode == "scols"
    return (N_DEV, shape[0], shape[1] // N_DEV)


def _comm_copies(ins, lands, modes, send_sems, recv_sems):
    me = _dev_index(_me())
    copies = []
    for k in range(N_DEV):
        peer = _flip(k)
        pidx = _dev_index(peer)
        for a, (src, land, mode) in enumerate(zip(ins, lands, modes)):
            if mode == "blk":
                src = src.at[pidx]
            elif mode == "scols":
                w = src.shape[1] // N_DEV
                src = src.at[:, pl.ds(pl.multiple_of(pidx * w, LANE), w)]
            if mode == "cols":
                w = src.shape[1]
                dst = land.at[:, pl.ds(pl.multiple_of(me * w, LANE), w)]
            else:
                dst = land.at[me]
            copies.append(pltpu.make_async_remote_copy(
                src_ref=src, dst_ref=dst, send_sem=send_sems.at[k * len(ins) + a], recv_sem=recv_sems.at[k * len(ins) + a],
                device_id=peer, device_id_type=MESH))
    return copies


def _comm_start_call(arrs, gather_flags, after, name):
    n = len(arrs)
    lands = [lax.empty(_land_shape(v.shape, mode), v.dtype) for v, mode in zip(arrs, gather_flags)]

    def body(*refs):
        ins, lnd = refs[:n], refs[n:2 * n]
        send_sems, recv_sems = refs[2 * n + len(after)], refs[2 * n + len(after) + 1]
        for cp in _comm_copies(ins, lnd, gather_flags, send_sems, recv_sems):
            cp.start()
        refs[-1][...] = jnp.zeros_like(refs[-1])

    hbm = [pltpu.with_memory_space_constraint(v, pltpu.HBM) for v in list(arrs) + lands]
    out = pl.pallas_call(
        body, name=name,
        out_shape=(pltpu.SemaphoreType.DMA((N_DEV * n,)), pltpu.SemaphoreType.DMA((N_DEV * n,)),
                   *[pltpu.HBM(v.shape, v.dtype) for v in hbm], jax.ShapeDtypeStruct((SUB, LANE), F32)),
        in_specs=[_HBM] * (2 * n) + [pl.BlockSpec(memory_space=pl.ANY)] * len(after),
        out_specs=(_SEM, _SEM, *[_HBM] * (2 * n), pl.BlockSpec(memory_space=pltpu.VMEM)),
        input_output_aliases={i: 2 + i for i in range(2 * n)},
        compiler_params=pltpu.CompilerParams(has_side_effects=pltpu.SideEffectType.DATAFLOW_SIDE_EFFECTING),
    )(*hbm, *after)
    return out[:-1], out[-1]


def _comm_wait_call(state, gather_flags, after, name):
    n = len(gather_flags)
    send_sems, recv_sems, thru = state[0], state[1], state[2:]

    def body(*refs):
        ins, lnd, ssem, rsem = refs[:n], refs[n:2 * n], refs[2 * n], refs[2 * n + 1]
        for cp in _comm_copies(ins, lnd, gather_flags, ssem, rsem):
            cp.wait_send()
            cp.wait_recv()

    out = pl.pallas_call(
        body, name=name,
        out_shape=tuple(pltpu.HBM(v.shape, v.dtype) for v in thru),
        in_specs=[_HBM] * (2 * n) + [_SEM, _SEM] + [pl.BlockSpec(memory_space=pl.ANY)] * len(after),
        out_specs=tuple([_HBM] * (2 * n)),
        input_output_aliases={i: i for i in range(2 * n)},
        compiler_params=pltpu.CompilerParams(has_side_effects=pltpu.SideEffectType.DATAFLOW_SIDE_EFFECTING),
    )(*thru, send_sems, recv_sems, *after)
    return out[n:]


def _sum8_call(recv, stacked, layer, nl, rows, r_out, c_out, name):
    c = recv.shape[2]

    def body(x_ref, *rest):
        o_ref = rest[-1]
        acc = x_ref[0, :, :c_out].astype(F32)
        for s in range(1, N_DEV):
            acc = acc + x_ref[s, :, :c_out].astype(F32)
        o_ref[...] = acc

    prev = [] if stacked is None else [stacked]
    return pl.pallas_call(
        body, name=name, grid=(r_out // rows,),
        in_specs=[pl.BlockSpec((N_DEV, rows, c), lambda i: (0, i, 0))] + [pl.BlockSpec(memory_space=pl.ANY)] * len(prev),
        out_specs=pl.BlockSpec((None, rows, c_out), lambda i: (layer, i, 0)),
        out_shape=jax.ShapeDtypeStruct((nl, r_out, c_out), F32),
        input_output_aliases={1: 0} if prev else {},
        compiler_params=_cparams(("parallel",)),
    )(recv, *prev)


def _adamw_call(w, g, m, v, rows, name):
    r, c = w.shape

    def body(w_ref, g_ref, m_ref, v_ref, d_ref, nm_ref, nv_ref):
        gr = g_ref[...]
        nm = ADAM_B1 * m_ref[...] + (1.0 - ADAM_B1) * gr
        nv = ADAM_B2 * v_ref[...] + (1.0 - ADAM_B2) * (gr * gr)
        m_hat = nm / (1.0 - ADAM_B1 ** ADAM_STEP)
        v_hat = nv / (1.0 - ADAM_B2 ** ADAM_STEP)
        d_ref[...] = -ADAM_LR * (m_hat / (jnp.sqrt(v_hat) + ADAM_EPS) + ADAM_WD * w_ref[...])
        nm_ref[...] = nm
        nv_ref[...] = nv

    spec = pl.BlockSpec((rows, c), lambda i: (i, 0))
    return pl.pallas_call(
        body, name=name, grid=(r // rows,),
        in_specs=[spec] * 4, out_specs=[spec] * 3,
        out_shape=[jax.ShapeDtypeStruct((r, c), F32)] * 3,
        compiler_params=_cparams(("parallel",)),
    )(w, g, m, v)


SH_ROWS = 16
SH_W = LRU_WIDTH // N_DEV
REP_ROWS = 824
_REP_SIZES = (LRU_WIDTH, 2 * 6 * 64 * 64, 2 * 6 * 64 * 64, RET_WIDTH, 1920, D_MODEL, D_MODEL, D_MODEL, D_MODEL)
_RPB_SIZE = NA_HEADS * (2 * NA_KH - 1) * (2 * NA_KW - 1)


def _pack_sh(cw, ba, bx, lam):
    return jnp.concatenate([cw, ba, bx, lam], axis=0)


def _pad_sh(p):
    pad = [(0, 0)] * (p.ndim - 2) + [(0, SH_ROWS - p.shape[-2]), (0, LANE - p.shape[-1])]
    return jnp.pad(p, pad)


def _pack_rep(cb, wa, wx, gnw, rpb, l1g, l1b, l2g, l2b):
    flat = jnp.concatenate([cb.reshape(-1), wa.reshape(-1), wx.reshape(-1), gnw.reshape(-1),
                            jnp.pad(rpb.reshape(-1), (0, 1920 - _RPB_SIZE)), l1g, l1b, l2g, l2b,
                            jnp.zeros((REP_ROWS * LANE - sum(_REP_SIZES),), F32)])
    return flat.reshape(REP_ROWS, LANE)


def _unpack_rep(p):
    nl = p.shape[0]
    flat = p.reshape(nl, -1)
    out, off = [], 0
    for size in _REP_SIZES:
        out.append(flat[:, off:off + size])
        off += size
    cb, wa, wx, gnw, rpb, l1g, l1b, l2g, l2b = out
    return (cb, wa.reshape(nl, 2, 6, 64, 64), wx.reshape(nl, 2, 6, 64, 64), gnw,
            rpb[:, :_RPB_SIZE].reshape(nl, NA_HEADS, 2 * NA_KH - 1, 2 * NA_KW - 1), l1g, l1b, l2g, l2b)


def _adamw_nd(w, g, m, v, rows, name):
    shp = w.shape
    f = lambda t: t.reshape(-1, shp[-1])
    rows = f(w).shape[0] if rows is None else rows
    return [t.reshape(shp) for t in _adamw_call(f(w), f(g), f(m), f(v), rows, name)]


def kernel(x, w_in, conv_w, conv_b, lru_w_a, lru_b_a, lru_w_x, lru_b_x, lru_lam, ret_gn_w, na_rpb, w_out, ln1_g, ln1_b, w_gate, w_up, w_down, ln2_g, ln2_b, loss_target, m_w_in, m_conv_w, m_conv_b, m_lru_w_a, m_lru_b_a, m_lru_w_x, m_lru_b_x, m_lru_lam, m_ret_gn_w, m_na_rpb, m_w_out, m_ln1_g, m_ln1_b, m_w_gate, m_w_up, m_w_down, m_ln2_g, m_ln2_b, v_w_in, v_conv_w, v_conv_b, v_lru_w_a, v_lru_b_a, v_lru_w_x, v_lru_b_x, v_lru_lam, v_ret_gn_w, v_na_rpb, v_w_out, v_ln1_g, v_ln1_b, v_w_gate, v_w_up, v_w_down, v_ln2_g, v_ln2_b):
    nl = w_in.shape[0]
    T = x.shape[1]
    rows_n = T // GRID_W
    x0, target = x[0], loss_target[0]
    ffpad = W_BLK - FF_BLK

    win_b = w_in.astype(_BF)
    wg_b = jnp.pad(w_gate, ((0, 0), (0, 0), (0, ffpad))).astype(_BF)
    wu_b = jnp.pad(w_up, ((0, 0), (0, 0), (0, ffpad))).astype(_BF)
    wd_b = jnp.pad(w_down, ((0, 0), (0, ffpad), (0, 0))).astype(_BF)
    wout_b = w_out.astype(_BF)
    def agf_start(l, after):
        sh = _pad_sh(_pack_sh(conv_w[l], lru_b_a[l], lru_b_x[l], lru_lam[l]))
        arrs, modes = [win_b[l], sh], ["cols", "all"]
        if l > 0:
            arrs, modes = arrs + [wd_b[l]], modes + ["all"]
        return _comm_start_call(arrs, modes, after, f"agf_start{l}"), modes

    def agk_start(l, after):
        arrs, modes = [wg_b[l], wu_b[l], wout_b[l]], ["cols", "cols", "all"]
        if l == 0:
            arrs, modes = arrs + [wd_b[l]], modes + ["all"]
        return _comm_start_call(arrs, modes, after, f"agk_start{l}"), modes

    tables = _ret_tables(T)
    w4_all = _lru_w4(lru_w_a, lru_w_x)
    layers = []
    gathered = []
    xs, xb = x0, x0.astype(_BF)
    (agf_state, token), agf_modes = agf_start(0, [])
    tie = 0.0 * token[0, 0]
    btabs = [_na_bias_tables(na_rpb[l] + tie, rows_n) for l in range(nl)]
    for l in range(nl):
        front = _comm_wait_call(agf_state, agf_modes, [xb] + (btabs if l == 0 else []), f"agf_wait{l}")
        win, shg = front[0], front[1]
        (agk_state, token), agk_modes = agk_start(l, [shg])
        full = shg[:, :10, :SH_W].transpose(1, 0, 2).reshape(10, LRU_WIDTH)
        vec, w4 = _lru_vec(full[0:4], conv_b[l], full[4:6], full[6:8], full[8:10]), w4_all[l]
        gnw8 = jnp.pad(ret_gn_w[l][None], ((0, SUB - 1), (0, 0)))
        btab = btabs[l]
        proj = _inproj_call(xb, win, token)
        y_lru = _lru_fwd_call(proj, vec, w4)
        y_ret = _ret_fwd_call(proj, tables, gnw8)
        y_na = _na_fwd_call(proj, btab)
        back = _comm_wait_call(agk_state, agk_modes, [y_na], f"agk_wait{l}")
        wg, wu, wout = back[0], back[1], back[2]
        wd = (back[3] if l == 0 else front[2]).reshape(IN_WIDTH, D_MODEL)
        wout = wout.reshape(D_MODEL, D_MODEL)
        gathered.append((win, wg, wu, wd, wout))
        if l + 1 < nl:
            (agf_state, token), agf_modes = agf_start(l + 1, [wout])
        z1, x1, x1b, ycb = _outproj_ln_call(y_lru, y_ret, y_na, xs, wout, ln1_g[l][None], ln1_b[l][None], token)
        z2, x2, x2b, gpb, upb = _ffn_ln_call(x1, x1b, wg, wu, wd, ln2_g[l][None], ln2_b[l][None])
        layers.append(dict(xb=xb, proj=proj, vec=vec, w4=w4, gnw8=gnw8, btab=btab,
                           z1=z1, x1b=x1b, ycb=ycb, z2=z2, gpb=gpb, upb=upb))
        xs, xb = x2, x2b

    dx, loss_blk = _loss_call(xs, target)
    loss = lax.psum(loss_blk[0, 0], ("x", "y", "c"))

    gxa_flags = ["scols", "scols", "blk", "blk"]
    gxb_flags = ["scols", "blk", "all"]
    gxa_state, gxb_state = [None] * nl, [None] * nl
    token = loss_blk
    for l in reversed(range(nl)):
        s = layers[l]
        win, wg, wu, wd, wout = gathered[l]
        dx1, dgp, dup, hid, dz2b, dln2 = _ffn_bwd_call(dx, s["z2"], s["gpb"], s["upb"], wg, wu, wd, ln2_g[l][None], token)
        dwg = _tn_cols_call(s["x1b"], dgp, "tn_cols")
        dwu = _tn_cols_call(s["x1b"], dup, "tn_cols")
        dwd = _tn_rows_call(hid, dz2b, N_BLK, "tn_rows_down").reshape(N_DEV, W_BLK, D_MODEL)
        dz1b, dyc, dres, dln1 = _outproj_bwd_call(dx1, s["z1"], wout, ln1_g[l][None])
        dwout = _tn_rows_call(s["ycb"], dz1b, D_MODEL // 2, "tn_rows_out").reshape(N_DEV, LANE, D_MODEL)
        gxa_state[l], token = _comm_start_call([dwg, dwu, dwd, dwout], gxa_flags, [], f"gxa_start{l}")
        dp, dvec, dw4 = _lru_bwd_call(s["proj"], dyc, s["vec"], s["w4"], token)
        dp, dgnw = _ret_bwd_call(s["proj"], dyc, tables, s["gnw8"], dp)
        dp, dbias = _na_bwd_call(s["proj"], dyc, s["btab"], dp)
        dwin = _tn_cols_call(s["xb"], dp, "tn_cols")
        dx = _inproj_bwd_call(dres, dp, win)
        dcw, dcb, dwa, dba, dwx, dbx, dlam = _lru_unpack(dvec, dw4)
        rep = _pack_rep(dcb, dwa, dwx, dgnw[0], _na_bias_grad(dbias, rows_n), dln1[0], dln1[1], dln2[0], dln2[1])
        sh = _pack_sh(dcw, dba, dbx, dlam).reshape(10, N_DEV, SH_W).transpose(1, 0, 2)
        gxb_state[l], token = _comm_start_call([dwin, _pad_sh(sh), rep], gxb_flags, [], f"gxb_start{l}")

    g_w_in = g_w_gate = g_w_up = g_w_down = g_w_out = g_shp = g_repp = None
    after = [dx, token]
    big = {}
    for l in reversed(range(nl)):
        ra = _comm_wait_call(gxa_state[l], gxa_flags, after, f"gxa_wait{l}")
        g_w_gate = _sum8_call(ra[0], g_w_gate, l, nl, TM, D_MODEL, FF_BLK, "sum8_ff")
        g_w_up = _sum8_call(ra[1], g_w_up, l, nl, TM, D_MODEL, FF_BLK, "sum8_ff")
        g_w_down = _sum8_call(ra[2], g_w_down, l, nl, FF_BLK, FF_BLK, D_MODEL, "sum8_down")
        g_w_out = _sum8_call(ra[3], g_w_out, l, nl, LANE, LANE, D_MODEL, "sum8_out")
        after = [g_w_out]
        if l == 0:
            big["w_gate"] = _adamw_nd(w_gate, g_w_gate, m_w_gate, v_w_gate, TM, "adamw_ff")
            big["w_up"] = _adamw_nd(w_up, g_w_up, m_w_up, v_w_up, TM, "adamw_ff")
            big["w_down"] = _adamw_nd(w_down, g_w_down, m_w_down, v_w_down, FF_BLK, "adamw_down")
            big["w_out"] = _adamw_nd(w_out, g_w_out, m_w_out, v_w_out, LANE, "adamw_out")
            after = [big[n][k] for n in ("w_gate", "w_up", "w_down", "w_out") for k in range(3)]
        rb = _comm_wait_call(gxb_state[l], gxb_flags, after, f"gxb_wait{l}")
        g_w_in = _sum8_call(rb[0], g_w_in, l, nl, TM, D_MODEL, W_BLK, "sum8_in")
        g_shp = _sum8_call(rb[1], g_shp, l, nl, SH_ROWS, SH_ROWS, LANE, "sum8_sh")
        g_repp = _sum8_call(rb[2], g_repp, l, nl, REP_ROWS, REP_ROWS, LANE, "sum8_rep")
        after = [g_repp]

    big["w_in"] = _adamw_nd(w_in, g_w_in, m_w_in, v_w_in, TM, "adamw_in")
    g_shp = g_shp[:, :, :SH_W]
    rep_names = ("conv_b", "lru_w_a", "lru_w_x", "ret_gn_w", "na_rpb", "ln1_g", "ln1_b", "ln2_g", "ln2_b")
    grads = {"w_in": g_w_in, "w_gate": g_w_gate, "w_up": g_w_up, "w_down": g_w_down, "w_out": g_w_out,
             "conv_w": g_shp[:, 0:4], "lru_b_a": g_shp[:, 4:6], "lru_b_x": g_shp[:, 6:8], "lru_lam": g_shp[:, 8:10]}
    grads.update(dict(zip(rep_names, _unpack_rep(g_repp))))
    small = {
        "conv_w": (conv_w, m_conv_w, v_conv_w), "conv_b": (conv_b, m_conv_b, v_conv_b),
        "lru_w_a": (lru_w_a, m_lru_w_a, v_lru_w_a), "lru_b_a": (lru_b_a, m_lru_b_a, v_lru_b_a),
        "lru_w_x": (lru_w_x, m_lru_w_x, v_lru_w_x), "lru_b_x": (lru_b_x, m_lru_b_x, v_lru_b_x),
        "lru_lam": (lru_lam, m_lru_lam, v_lru_lam), "ret_gn_w": (ret_gn_w, m_ret_gn_w, v_ret_gn_w),
        "na_rpb": (na_rpb, m_na_rpb, v_na_rpb), "ln1_g": (ln1_g, m_ln1_g, v_ln1_g), "ln1_b": (ln1_b, m_ln1_b, v_ln1_b),
        "ln2_g": (ln2_g, m_ln2_g, v_ln2_g), "ln2_b": (ln2_b, m_ln2_b, v_ln2_b),
    }
    for name, (w_, m_, v_) in small.items():
        big[name] = _adamw_nd(w_, grads[name], m_, v_, None, "adamw_small")
    kinds = [{n: big[n][k] for n in big} for k in range(3)]
    order = ("w_in", "conv_w", "conv_b", "lru_w_a", "lru_b_a", "lru_w_x", "lru_b_x", "lru_lam", "ret_gn_w", "na_rpb",
             "w_out", "ln1_g", "ln1_b", "w_gate", "w_up", "w_down", "ln2_g", "ln2_b")
    outs = [loss, dx[None]]
    for d in (grads, *kinds):
        outs.extend(d[n] for n in order)
    return tuple(outs)
```

```python
import functools
import math

import numpy as np
import jax
import jax.numpy as jnp
from jax import lax
from jax.experimental import pallas as pl
from jax.experimental.pallas import tpu as pltpu

F32 = jnp.float32
_BF = jnp.bfloat16

D_MODEL = 1024
DEPTH = 4
GRID_W = 64
HEAD_DIM = 64
LRU_WIDTH = 384
RET_WIDTH = 384
RET_HEADS = 6
NA_WIDTH = 256
NA_HEADS = 4
IN_WIDTH = 3072
CONV_WIDTH = 4
LRU_C = 8.0
RET_CHUNK = 128
ROPE_BASE = 10000.0
GN_EPS = 1e-6
NA_KH = 8
NA_KW = 16
D_FF = 2816
FF_BLK = 352
N_DEV = 8
ALPHA = (2 * DEPTH) ** 0.25
LN_EPS = 1e-5
ADAM_LR = 0.001
ADAM_B1 = 0.9
ADAM_B2 = 0.999
ADAM_EPS = 1e-08
ADAM_WD = 0.01
ADAM_STEP = 10

LANE = 128
SUB = 8
VMEM_MB = 56
NEG = -1e30

MESH = pl.DeviceIdType.MESH


def _cparams(sem=None, vmem_mb=VMEM_MB):
    return pltpu.CompilerParams(dimension_semantics=sem, vmem_limit_bytes=vmem_mb << 20)


def _mm(a, b):
    return jnp.dot(a.astype(_BF), b.astype(_BF), preferred_element_type=F32)


def _mm_nt(a, b):
    return lax.dot_general(a.astype(_BF), b.astype(_BF), (((1,), (1,)), ((), ())), preferred_element_type=F32)


def _mm_tn(a, b):
    return lax.dot_general(a.astype(_BF), b.astype(_BF), (((0,), (0,)), ((), ())), preferred_element_type=F32)


def _sigmoid(x):
    return jax.nn.sigmoid(x)


def _rows(start, size):
    return pl.ds(pl.multiple_of(start, SUB), size)


def _loop2(n, body, init):
    assert n % 2 == 0
    return lax.fori_loop(0, n // 2, lambda i, c: body(2 * i + 1, body(2 * i, c)), init)


def _strip(T, col, buffers=2):
    return pl.BlockSpec((T, LANE), lambda j: (0, col(j)), pipeline_mode=pl.Buffered(buffers))


LRU_CH = 1024
_GELU_C0 = math.sqrt(2.0 / math.pi)
_GELU_C1 = 0.044715


def _gelu_parts(x):
    x2 = x * x
    t = jnp.tanh(_GELU_C0 * (x + _GELU_C1 * x * x2))
    val = 0.5 * x * (1.0 + t)
    der = 0.5 * (1.0 + t) + 0.5 * x * (1.0 - t * t) * _GELU_C0 * (1.0 + 3.0 * _GELU_C1 * x2)
    return val, der


def _softplus_neg(lam):
    e = jnp.exp(-jnp.abs(lam))
    w = 1.0 + e
    l1p = jnp.where(w == 1.0, e, jnp.log(w) * (e / jnp.where(w == 1.0, 1.0, w - 1.0)))
    return jnp.maximum(-lam, 0.0) + l1p


def _window(ref, t0, ch, T):
    prev = ref[_rows(jnp.maximum(t0 - SUB, 0), SUB), :].astype(F32)
    nxt = ref[_rows(jnp.minimum(t0 + ch, T - SUB), SUB), :].astype(F32)
    prev = jnp.where(t0 > 0, prev, 0.0)
    nxt = jnp.where(t0 + ch < T, nxt, 0.0)
    return jnp.concatenate([prev, ref[_rows(t0, ch), :].astype(F32), nxt], axis=0)


def _tap(win, shift, ch):
    n = win.shape[0]
    return pltpu.roll(win, (-shift) % n, 0)[SUB:SUB + ch]


def _lru_conv(xb_ref, vec, t0, T):
    win = _window(xb_ref, t0, LRU_CH, T)
    xc = jnp.broadcast_to(vec[4:5, :], (LRU_CH, LANE))
    for j in range(CONV_WIDTH):
        xc = xc + _tap(win, j - CONV_WIDTH // 2, LRU_CH) * vec[j:j + 1, :]
    return xc


def _lru_dir(pre_a, pre_x, sp):
    r = _sigmoid(pre_a)
    i = _sigmoid(pre_x)
    log_a = (-LRU_C) * r * sp
    a = jnp.exp(log_a)
    z = jnp.tanh(-log_a) * (a * a + 1.0)
    s = jnp.sqrt(z)
    return r, i, a, s


def _scan_tile(a, b, reverse, row):
    for k in (1, 2, 4):
        if not reverse:
            a_s, b_s, m = pltpu.roll(a, k, 0), pltpu.roll(b, k, 0), row >= k
        else:
            a_s, b_s, m = pltpu.roll(a, SUB - k, 0), pltpu.roll(b, SUB - k, 0), row < SUB - k
        b = jnp.where(m, a * b_s + b, b)
        a = jnp.where(m, a * a_s, a)
    return a, b


def _bcast_row(x, r):
    return jnp.broadcast_to(x[r:r + 1, :], (SUB, LANE))


def _lru_prepare(xb_ref, w4_ref, vec, xc_ref, af_ref, uf_ref, ab_ref, ub_ref, T):
    sp_f = _softplus_neg(vec[9:10, :])
    sp_b = _softplus_neg(vec[10:11, :])
    w4 = w4_ref[0]

    def body(c, carry):
        t0 = c * LRU_CH
        xc = _lru_conv(xb_ref, vec, t0, T)
        if xc_ref is not None:
            xc_ref[_rows(t0, LRU_CH), :] = xc
        pre = _mm(xc, w4)
        _, i, a, s = _lru_dir(pre[:, 0:128] + vec[5:6, :], pre[:, 128:256] + vec[6:7, :], sp_f)
        af_ref[_rows(t0, LRU_CH), :] = a
        uf_ref[_rows(t0, LRU_CH), :] = s * (i * xc)
        _, i, a, s = _lru_dir(pre[:, 256:384] + vec[7:8, :], pre[:, 384:512] + vec[8:9, :], sp_b)
        ab_ref[_rows(t0, LRU_CH), :] = a
        ub_ref[_rows(t0, LRU_CH), :] = s * (i * xc)
        return carry

    lax.fori_loop(0, T // LRU_CH, body, 0)


def _lru_scan(af_ref, uf_ref, ab_ref, ub_ref, T):
    nt = T // SUB
    row = lax.broadcasted_iota(jnp.int32, (SUB, LANE), 0)

    def body(j, carry):
        hf, hb = carry
        sf = _rows(j * SUB, SUB)
        sb = _rows((nt - 1 - j) * SUB, SUB)
        a, b = _scan_tile(af_ref[sf, :], uf_ref[sf, :], False, row)
        h = a * hf + b
        uf_ref[sf, :] = h
        hf = _bcast_row(h, SUB - 1)
        a, b = _scan_tile(ab_ref[sb, :], ub_ref[sb, :], True, row)
        h = a * hb + b
        ub_ref[sb, :] = h
        hb = _bcast_row(h, 0)
        return hf, hb

    z = jnp.zeros((SUB, LANE), F32)
    lax.fori_loop(0, nt, body, (z, z))


def _lru_fwd_call(proj, vec, w4):
    T = proj.shape[0]

    def body(xb_ref, gate_ref, vec_ref, w4_ref, y_ref, af_ref, uf_ref, ab_ref, ub_ref):
        vec = vec_ref[...]
        _lru_prepare(xb_ref, w4_ref, vec, None, af_ref, uf_ref, ab_ref, ub_ref, T)
        _lru_scan(af_ref, uf_ref, ab_ref, ub_ref, T)

        def out(c, carry):
            rows = _rows(c * LRU_CH, LRU_CH)
            gl, _ = _gelu_parts(gate_ref[rows, :])
            y_ref[rows, :] = (uf_ref[rows, :] + ub_ref[rows, :]) * gl
            return carry

        lax.fori_loop(0, T // LRU_CH, out, 0)

    return pl.pallas_call(
        body, name="lru_fwd", grid=(LRU_WIDTH // LANE,),
        in_specs=[_strip(T, lambda j: j), _strip(T, lambda j: j + 3),
                  pl.BlockSpec((16, LANE), lambda j: (0, j)),
                  pl.BlockSpec((1, LANE, 4 * LANE), lambda j: (j, 0, 0))],
        out_specs=_strip(T, lambda j: j, buffers=1),
        out_shape=jax.ShapeDtypeStruct((T, LRU_WIDTH), F32),
        scratch_shapes=[pltpu.VMEM((T, LANE), F32)] * 4,
        compiler_params=_cparams(("arbitrary",)),
    )(proj, proj, vec, w4)


def _store_strips(stage_ref, dp_ref, cols, sems):
    copies = [pltpu.make_async_copy(stage_ref.at[b], dp_ref.at[:, pl.ds(pl.multiple_of(c * LANE, LANE), LANE)], sems.at[b])
              for b, c in enumerate(cols)]
    for cp in copies:
        cp.start()
    for cp in copies:
        cp.wait()


def _lru_bwd_call(proj, dycat, vec, w4, after):
    T = proj.shape[0]
    nt = T // SUB
    nch = T // LRU_CH

    def body(xb_ref, gate_ref, dy_ref, vec_ref, w4_ref, after_ref, dp_ref, dvec_ref, dw4_ref,
             xc_ref, af_ref, hf_ref, ab_ref, hb_ref, dh_ref, stage_ref, sems):
        dxb_ref, dgate_ref = stage_ref.at[0], stage_ref.at[1]
        vec = vec_ref[...]
        _lru_prepare(xb_ref, w4_ref, vec, xc_ref, af_ref, hf_ref, ab_ref, hb_ref, T)
        _lru_scan(af_ref, hf_ref, ab_ref, hb_ref, T)

        def gate_bwd(c, carry):
            rows = _rows(c * LRU_CH, LRU_CH)
            gl, dgl = _gelu_parts(gate_ref[rows, :])
            dy = dy_ref[rows, :]
            dgate_ref[rows, :] = (dy * (hf_ref[rows, :] + hb_ref[rows, :]) * dgl).astype(dgate_ref.dtype)
            dh_ref[rows, :] = dy * gl
            return carry

        lax.fori_loop(0, nch, gate_bwd, 0)

        row = lax.broadcasted_iota(jnp.int32, (SUB, LANE), 0)

        def adj(j, carry):
            gf, a_next, gb, a_prev = carry
            tf = nt - 1 - j
            sf = _rows(tf * SUB, SUB)
            a_t = af_ref[sf, :]
            h_t = hf_ref[sf, :]
            coef = jnp.where(row == SUB - 1, a_next, pltpu.roll(a_t, SUB - 1, 0))
            ac, bc = _scan_tile(coef, dh_ref[sf, :], True, row)
            g = ac * gf + bc
            h_prev = hf_ref[_rows(jnp.maximum(tf - 1, 0) * SUB, SUB), :]
            h_prev = jnp.where(tf > 0, _bcast_row(h_prev, SUB - 1), 0.0)
            hs = jnp.where(row == 0, h_prev, pltpu.roll(h_t, 1, 0))
            af_ref[sf, :] = g * hs
            hf_ref[sf, :] = g
            gf = _bcast_row(g, 0)
            a_next = _bcast_row(a_t, 0)
            sb = _rows(j * SUB, SUB)
            a_t = ab_ref[sb, :]
            h_t = hb_ref[sb, :]
            coef = jnp.where(row == 0, a_prev, pltpu.roll(a_t, 1, 0))
            ac, bc = _scan_tile(coef, dh_ref[sb, :], False, row)
            g = ac * gb + bc
            h_next = hb_ref[_rows(jnp.minimum(j + 1, nt - 1) * SUB, SUB), :]
            h_next = jnp.where(j < nt - 1, _bcast_row(h_next, 0), 0.0)
            hs = jnp.where(row == SUB - 1, h_next, pltpu.roll(h_t, SUB - 1, 0))
            ab_ref[sb, :] = g * hs
            hb_ref[sb, :] = g
            gb = _bcast_row(g, SUB - 1)
            a_prev = _bcast_row(a_t, SUB - 1)
            return gf, a_next, gb, a_prev

        z = jnp.zeros((SUB, LANE), F32)
        lax.fori_loop(0, nt, adj, (z, z, z, z))

        sp_f = _softplus_neg(vec[9:10, :])
        sp_b = _softplus_neg(vec[10:11, :])
        w4 = w4_ref[0]
        dw4_ref[...] = jnp.zeros_like(dw4_ref)

        def one_dir(pre_a, pre_x, sp, xc, du, da):
            r, i, a, s = _lru_dir(pre_a, pre_x, sp)
            d_i = du * s * xc
            dxc = du * s * i
            d_s = du * i * xc
            d_log = da * a - d_s * (a * a) / s
            d_r = d_log * (-LRU_C) * sp
            d_sp = jnp.sum(d_log * (-LRU_C) * r, axis=0, keepdims=True)
            return d_r * r * (1.0 - r), d_i * i * (1.0 - i), dxc, d_sp

        def gates_bwd(c, carry):
            db, dspf, dspb = carry
            rows = _rows(c * LRU_CH, LRU_CH)
            xc = xc_ref[rows, :]
            pre = _mm(xc, w4)
            dpa_f, dpx_f, dxc_f, d_sp_f = one_dir(pre[:, 0:128] + vec[5:6, :], pre[:, 128:256] + vec[6:7, :],
                                                  sp_f, xc, hf_ref[rows, :], af_ref[rows, :])
            dpa_b, dpx_b, dxc_b, d_sp_b = one_dir(pre[:, 256:384] + vec[7:8, :], pre[:, 384:512] + vec[8:9, :],
                                                  sp_b, xc, hb_ref[rows, :], ab_ref[rows, :])
            dpre = jnp.concatenate([dpa_f, dpx_f, dpa_b, dpx_b], axis=1)
            dw4_ref[0] += _mm_tn(xc, dpre)
            dh_ref[rows, :] = dxc_f + dxc_b + _mm_nt(dpre, w4)
            return db + jnp.sum(dpre, axis=0, keepdims=True), dspf + d_sp_f, dspb + d_sp_b

        z1 = jnp.zeros((1, LANE), F32)
        db, dspf, dspb = lax.fori_loop(0, nch, gates_bwd, (jnp.zeros((1, 4 * LANE), F32), z1, z1))

        def conv_bwd(c, carry):
            t0 = c * LRU_CH
            rows = _rows(t0, LRU_CH)
            dwin = _window(dh_ref, t0, LRU_CH, T)
            xwin = _window(xb_ref, t0, LRU_CH, T)
            dxc = dh_ref[rows, :]
            dxb = jnp.zeros((LRU_CH, LANE), F32)
            out = []
            for j in range(CONV_WIDTH):
                off = j - CONV_WIDTH // 2
                dxb = dxb + _tap(dwin, -off, LRU_CH) * vec[j:j + 1, :]
                out.append(carry[j] + jnp.sum(dxc * _tap(xwin, off, LRU_CH), axis=0, keepdims=True))
            dxb_ref[rows, :] = dxb.astype(dxb_ref.dtype)
            out.append(carry[CONV_WIDTH] + jnp.sum(dxc, axis=0, keepdims=True))
            return tuple(out)

        dconv = lax.fori_loop(0, nch, conv_bwd, (z1,) * (CONV_WIDTH + 1))
        dlam_f = dspf * (-_sigmoid(-vec[9:10, :]))
        dlam_b = dspb * (-_sigmoid(-vec[10:11, :]))
        dvec_ref[...] = jnp.concatenate(
            list(dconv) + [db[:, 0:128], db[:, 128:256], db[:, 256:384], db[:, 384:512], dlam_f, dlam_b,
                           jnp.zeros((5, LANE), F32)], axis=0)
        j = pl.program_id(0)
        _store_strips(stage_ref, dp_ref, (j, j + 3), sems)

    ns = LRU_WIDTH // LANE
    return pl.pallas_call(
        body, name="lru_bwd", grid=(ns,),
        in_specs=[_strip(T, lambda j: j), _strip(T, lambda j: j + 3), _strip(T, lambda j: j),
                  pl.BlockSpec((16, LANE), lambda j: (0, j)),
                  pl.BlockSpec((1, LANE, 4 * LANE), lambda j: (j, 0, 0)),
                  pl.BlockSpec(memory_space=pl.ANY)],
        out_specs=[pl.BlockSpec(memory_space=pl.ANY),
                   pl.BlockSpec((16, LANE), lambda j: (0, j)),
                   pl.BlockSpec((1, LANE, 4 * LANE), lambda j: (j, 0, 0))],
        out_shape=[jax.ShapeDtypeStruct((T, IN_WIDTH), _BF),
                   jax.ShapeDtypeStruct((16, LRU_WIDTH), F32), jax.ShapeDtypeStruct((ns, LANE, 4 * LANE), F32)],
        scratch_shapes=[pltpu.VMEM((T, LANE), F32)] * 6 + [pltpu.VMEM((2, T, LANE), _BF), pltpu.SemaphoreType.DMA((2,))],
        compiler_params=_cparams(("arbitrary",)),
    )(proj, proj, dycat, vec, w4, after)


def _lru_vec(cw, cb, ba, bx, lam):
    return jnp.concatenate([cw, cb[None], ba[0:1], bx[0:1], ba[1:2], bx[1:2], lam, jnp.zeros((5, LRU_WIDTH), F32)], axis=0)


def _lru_w4(wa, wx):
    nl = wa.shape[0]
    w = jnp.stack([wa[:, 0], wx[:, 0], wa[:, 1], wx[:, 1]], axis=1)
    w = w.reshape(nl, 4, 3, 2, 64, 64)
    eye = jnp.eye(2, dtype=w.dtype)
    bd = w[:, :, :, :, :, None, :] * eye[None, None, None, :, None, :, None]
    bd = bd.reshape(nl, 4, 3, LANE, LANE)
    return bd.transpose(0, 2, 3, 1, 4).reshape(nl, 3, LANE, 4 * LANE).astype(_BF)


def _lru_unpack(dvec, dw4):
    def blocks(m):
        m = m.reshape(3, 2, 64, 2, 64)
        return jnp.stack([m[:, 0, :, 0, :], m[:, 1, :, 1, :]], axis=1).reshape(6, 64, 64)
    parts = [blocks(dw4[:, :, k * LANE:(k + 1) * LANE]) for k in range(4)]
    dwa = jnp.stack([parts[0], parts[2]])
    dwx = jnp.stack([parts[1], parts[3]])
    dba = jnp.stack([dvec[5], dvec[7]])
    dbx = jnp.stack([dvec[6], dvec[8]])
    return dvec[0:4], dvec[4], dwa, dba, dwx, dbx, dvec[9:11]


RC = 2 * RET_CHUNK


def _ret_tables(T):
    half = HEAD_DIM // 2
    pos = jnp.arange(T, dtype=F32)
    inv_freq = ROPE_BASE ** (-jnp.arange(half, dtype=F32) / half)
    ang = pos[:, None] * inv_freq[None, :]
    cos = jnp.tile(jnp.cos(ang), (1, 4))
    sin = jnp.tile(jnp.concatenate([-jnp.sin(ang), jnp.sin(ang)], axis=1), (1, 2))
    log_g = jnp.log1p(-jnp.exp2(-5.0 - jnp.arange(RET_HEADS, dtype=F32)))
    idx = jnp.arange(RC, dtype=F32)
    dec = jnp.exp(jnp.abs(idx[:, None] - idx[None, :]) * log_g[:, None, None])
    lg = jnp.repeat(log_g, HEAD_DIM).reshape(3, 1, LANE)
    col = idx[None, :, None]
    rtab = jnp.stack([jnp.exp((RC - 1 - col) * lg), jnp.exp(col * lg),
                      jnp.exp((col + 1.0) * lg), jnp.exp((RC - col) * lg)], axis=1)
    gch = jnp.broadcast_to(jnp.exp(RC * lg), (3, SUB, LANE))
    return cos, sin, dec, rtab, gch


def _swap32(x, lane):
    return jnp.where((lane & 32) == 0, pltpu.roll(x, LANE - 32, 1), pltpu.roll(x, 32, 1))


def _head_mean(x, m0, m1):
    s0 = jnp.sum(x * m0, axis=-1, keepdims=True)
    s1 = jnp.sum(x * m1, axis=-1, keepdims=True)
    return (s0 * m0 + s1 * m1) * (1.0 / HEAD_DIM)


def _ret_masks():
    lane = lax.broadcasted_iota(jnp.int32, (RC, LANE), 1)
    m0 = (lane < HEAD_DIM).astype(F32)
    r = lax.broadcasted_iota(jnp.int32, (LANE, LANE), 0) // HEAD_DIM
    c = lax.broadcasted_iota(jnp.int32, (LANE, LANE), 1) // HEAD_DIM
    return lane, m0, 1.0 - m0, (r == c).astype(F32)


def _ret_specs(T):
    const = lambda shape, imap: pl.BlockSpec(shape, imap)
    return [_strip(T, lambda j: j + 6), _strip(T, lambda j: j + 9), _strip(T, lambda j: j + 12),
            _strip(T, lambda j: j + 15),
            pl.BlockSpec((T, LANE), lambda j: (0, 0), pipeline_mode=pl.Buffered(1)),
            pl.BlockSpec((T, LANE), lambda j: (0, 0), pipeline_mode=pl.Buffered(1)),
            const((2, RC, RC), lambda j: (j, 0, 0)),
            const((1, 4, RC, LANE), lambda j: (j, 0, 0, 0)),
            const((1, SUB, LANE), lambda j: (j, 0, 0)),
            const((SUB, LANE), lambda j: (0, j))]


def _ret_fwd_call(proj, tables, gnw8):
    T = proj.shape[0]
    nc = T // RC
    cos, sin, dec, rtab, gch = tables

    def body(q_ref, k_ref, v_ref, g_ref, cos_ref, sin_ref, dec_ref, rtab_ref, gch_ref, gnw_ref, y_ref, stf_ref, kr_ref):
        lane, m0, m1, bd = _ret_masks()
        gch_v = gch_ref[0][0:1, :]
        gnw = gnw_ref[0:1, :]
        dkf, dkb, dqf, dqb = rtab_ref[0, 0], rtab_ref[0, 1], rtab_ref[0, 2], rtab_ref[0, 3]

        def rope(x, rows):
            return x * cos_ref[rows, :] + _swap32(x, lane) * sin_ref[rows, :]

        def pass_a(n, st):
            rows = _rows(n * RC, RC)
            stf_ref[n] = st
            kr = rope(k_ref[rows, :], rows) * (HEAD_DIM ** -0.5)
            kr_ref[rows, :] = kr
            return gch_v * st + _mm_tn(kr * dkf, v_ref[rows, :]) * bd

        _loop2(nc, pass_a, jnp.zeros((LANE, LANE), F32))

        def pass_b(i, stb):
            ns = [nc - 1 - 2 * i, nc - 2 - 2 * i]
            rows = [_rows(n * RC, RC) for n in ns]
            heads = ((0, m0), (1, m1))
            qr = [rope(q_ref[r, :], r) for r in rows]
            kr = [kr_ref[r, :] for r in rows]
            v = [v_ref[r, :] for r in rows]
            kv = [_mm_tn(kr[c] * dkb, v[c]) * bd for c in range(2)]
            stbs = [stb, gch_v * stb + kv[0]]
            s = [[_mm_nt(qr[c] * m, kr[c]) * dec_ref[h] for h, m in heads] for c in range(2)]
            o = [_mm(qr[c] * dqf, stf_ref[ns[c]]) + _mm(qr[c] * dqb, stbs[c]) for c in range(2)]
            o = [o[c] + _mm(s[c][0], v[c] * m0) + _mm(s[c][1], v[c] * m1) for c in range(2)]
            oc = [o_ - _head_mean(o_, m0, m1) for o_ in o]
            on = [oc_ * lax.rsqrt(_head_mean(oc_ * oc_, m0, m1) + GN_EPS) for oc_ in oc]
            for c in range(2):
                g = g_ref[rows[c], :]
                y_ref[rows[c], :] = (g * _sigmoid(g)) * (on[c] * gnw)
            return gch_v * stbs[1] + kv[1]

        assert nc % 2 == 0
        lax.fori_loop(0, nc // 2, pass_b, jnp.zeros((LANE, LANE), F32))

    return pl.pallas_call(
        body, name="ret_fwd", grid=(RET_WIDTH // LANE,),
        in_specs=_ret_specs(T),
        out_specs=_strip(T, lambda j: j, buffers=1),
        out_shape=jax.ShapeDtypeStruct((T, RET_WIDTH), F32),
        scratch_shapes=[pltpu.VMEM((nc, LANE, LANE), F32), pltpu.VMEM((T, LANE), F32)],
        compiler_params=_cparams(("arbitrary",)),
    )(proj, proj, proj, proj, cos, sin, dec, rtab, gch, gnw8)


def _ret_bwd_call(proj, dycat, tables, gnw8, dp):
    T = proj.shape[0]
    nc = T // RC
    cos, sin, dec, rtab, gch = tables

    def body(q_ref, k_ref, v_ref, g_ref, cos_ref, sin_ref, dec_ref, rtab_ref, gch_ref, gnw_ref, dy_ref, dp_in_ref,
             dp_out_ref, dgnw_ref, stf_ref, dstb_ref, dkr_ref, dv_ref, dp_ref, kr_ref, sems):
        lane, m0, m1, bd = _ret_masks()
        gch_v = gch_ref[0][0:1, :]
        gnw = gnw_ref[0:1, :]
        dkf, dkb, dqf, dqb = rtab_ref[0, 0], rtab_ref[0, 1], rtab_ref[0, 2], rtab_ref[0, 3]
        scale = HEAD_DIM ** -0.5
        zst = jnp.zeros((LANE, LANE), F32)

        def rope(x, rows):
            return x * cos_ref[rows, :] + _swap32(x, lane) * sin_ref[rows, :]

        def rope_t(d, rows):
            return d * cos_ref[rows, :] + _swap32(d * sin_ref[rows, :], lane)

        def pass_a(n, st):
            rows = _rows(n * RC, RC)
            stf_ref[n] = st
            kr = rope(k_ref[rows, :], rows) * scale
            kr_ref[rows, :] = kr
            return gch_v * st + _mm_tn(kr * dkf, v_ref[rows, :]) * bd

        _loop2(nc, pass_a, zst)

        def pass_b(i, carry):
            stb, d_f, dgnw = carry
            two = range(2)
            heads = ((0, m0), (1, m1))
            ns = [nc - 1 - 2 * i, nc - 2 - 2 * i]
            rows = [_rows(n * RC, RC) for n in ns]
            qr = [rope(q_ref[r, :], r) for r in rows]
            kr = [kr_ref[r, :] for r in rows]
            v = [v_ref[r, :] for r in rows]
            stf = [stf_ref[n] for n in ns]
            kvb = [_mm_tn(kr[c] * dkb, v[c]) * bd for c in two]
            stbs = [stb, gch_v * stb + kvb[0]]
            qf = [qr[c] * dqf for c in two]
            qb = [qr[c] * dqb for c in two]
            s = [[_mm_nt(qr[c] * m, kr[c]) * dec_ref[h] for h, m in heads] for c in two]
            o = [_mm(qf[c], stf[c]) + _mm(qb[c], stbs[c]) for c in two]
            o = [o[c] + _mm(s[c][0], v[c] * m0) + _mm(s[c][1], v[c] * m1) for c in two]
            oc = [o_ - _head_mean(o_, m0, m1) for o_ in o]
            rstd = [lax.rsqrt(_head_mean(oc_ * oc_, m0, m1) + GN_EPS) for oc_ in oc]
            on = [oc[c] * rstd[c] for c in two]
            do = []
            for c in two:
                g = g_ref[rows[c], :]
                sg = _sigmoid(g)
                dy = dy_ref[rows[c], :]
                dp_ref[3, rows[c], :] = (dy * (on[c] * gnw) * (sg * (1.0 + g * (1.0 - sg)))).astype(dp_ref.dtype)
                t = dy * (g * sg)
                dgnw = dgnw + jnp.sum(t * on[c], axis=0, keepdims=True)
                don = t * gnw
                do.append(rstd[c] * (don - _head_mean(don, m0, m1) - on[c] * _head_mean(don * on[c], m0, m1)))
            dstf = [_mm_tn(qf[c], do[c]) * bd for c in two]
            dfs = [d_f, dstf[0] + gch_v * d_f]
            ds = [[_mm_nt(do[c] * m, v[c]) * dec_ref[h] for h, m in heads] for c in two]
            dqr = [_mm_nt(do[c], stf[c]) * dqf + _mm_nt(do[c], stbs[c]) * dqb
                   + _mm(ds[c][0], kr[c] * m0) + _mm(ds[c][1], kr[c] * m1) for c in two]
            dkr = [_mm_nt(v[c], dfs[c]) * dkf + _mm_tn(ds[c][0], qr[c] * m0) + _mm_tn(ds[c][1], qr[c] * m1) for c in two]
            dv = [_mm(kr[c] * dkf, dfs[c]) + _mm_tn(s[c][0], do[c] * m0) + _mm_tn(s[c][1], do[c] * m1) for c in two]
            for c in two:
                dp_ref[0, rows[c], :] = rope_t(dqr[c], rows[c]).astype(dp_ref.dtype)
                dkr_ref[rows[c], :] = dkr[c]
                dv_ref[rows[c], :] = dv[c]
                dstb_ref[ns[c]] = _mm_tn(qb[c], do[c]) * bd
            return gch_v * stbs[1] + kvb[1], dstf[1] + gch_v * dfs[1], dgnw

        assert nc % 2 == 0
        _, _, dgnw = lax.fori_loop(0, nc // 2, pass_b, (zst, zst, jnp.zeros((1, LANE), F32)))
        dgnw_ref[...] = jnp.concatenate([dgnw, jnp.zeros((SUB - 1, LANE), F32)], axis=0)

        def pass_c(n, d_b):
            rows = _rows(n * RC, RC)
            kr = kr_ref[rows, :]
            v = v_ref[rows, :]
            dkr = dkr_ref[rows, :] + _mm_nt(v, d_b) * dkb
            dp_ref[1, rows, :] = (rope_t(dkr, rows) * scale).astype(dp_ref.dtype)
            dp_ref[2, rows, :] = (dv_ref[rows, :] + _mm(kr * dkb, d_b)).astype(dp_ref.dtype)
            return dstb_ref[n] + gch_v * d_b

        _loop2(nc, pass_c, zst)
        j = pl.program_id(0)
        _store_strips(dp_ref, dp_out_ref, (j + 6, j + 9, j + 12, j + 15), sems)

    n_in = len(_ret_specs(T)) + 1
    return pl.pallas_call(
        body, name="ret_bwd", grid=(RET_WIDTH // LANE,),
        in_specs=_ret_specs(T) + [_strip(T, lambda j: j + 3), pl.BlockSpec(memory_space=pl.ANY)],
        out_specs=[pl.BlockSpec(memory_space=pl.ANY), pl.BlockSpec((SUB, LANE), lambda j: (0, j))],
        out_shape=[jax.ShapeDtypeStruct(dp.shape, dp.dtype), jax.ShapeDtypeStruct((SUB, RET_WIDTH), F32)],
        scratch_shapes=[pltpu.VMEM((nc, LANE, LANE), F32), pltpu.VMEM((nc, LANE, LANE), F32),
                        pltpu.VMEM((T, LANE), F32), pltpu.VMEM((T, LANE), F32),
                        pltpu.VMEM((4, T, LANE), _BF), pltpu.VMEM((T, LANE), F32), pltpu.SemaphoreType.DMA((4,))],
        input_output_aliases={n_in: 0},
        compiler_params=_cparams(("arbitrary",)),
    )(proj, proj, proj, proj, cos, sin, dec, rtab, gch, gnw8, dycat, dp)


NA_Q = 2 * GRID_W
NA_WROWS = 10
NA_K = NA_WROWS * GRID_W
NA_CHUNKS = NA_K // LANE
NA_UNROLL = 4
NA_TYPES = 5
_ONEHOT_PRECISION = lax.Precision.HIGH


def _na_onehots(rows_n):
    reps = [(0, 0), (2, 0), (4, 0), (rows_n - 4, rows_n - NA_WROWS), (rows_n - 2, rows_n - NA_WROWS)]
    rm = np.zeros((NA_TYPES, 2, NA_WROWS, 2 * NA_KH - 1), np.float32)
    for t, (r, ws) in enumerate(reps):
        for qh in range(2):
            qrow = r + qh
            rstart = min(max(qrow - NA_KH // 2, 0), rows_n - NA_KH)
            for kh in range(NA_WROWS):
                krow = ws + kh
                if rstart <= krow < rstart + NA_KH:
                    rm[t, qh, kh, krow - qrow + NA_KH - 1] = 1.0
    cm = np.zeros((GRID_W, GRID_W, 2 * NA_KW - 1), np.float32)
    for qc in range(GRID_W):
        cstart = min(max(qc - NA_KW // 2, 0), GRID_W - NA_KW)
        for kc in range(cstart, cstart + NA_KW):
            cm[qc, kc, kc - qc + NA_KW - 1] = 1.0
    rm2 = rm.reshape(NA_TYPES, 2, NA_CHUNKS, 2, 2 * NA_KH - 1)
    cm2 = np.zeros((GRID_W, LANE, 2, 2 * NA_KW - 1), np.float32)
    for z in range(2):
        cm2[:, z * GRID_W:(z + 1) * GRID_W, z, :] = cm
    return rm2, cm2


def _na_bias_tables(rpb, rows_n):
    rm, cm = _na_onehots(rows_n)
    val = jnp.einsum("hab,tqpza,xkzb->htpqxk", rpb, rm, cm, precision=_ONEHOT_PRECISION)
    valid = np.einsum("tqpz,xkz->tpqxk", rm.sum(-1), cm.sum(-1)) > 0.5
    return jnp.where(valid[None], val, NEG).reshape(2, 2, NA_TYPES, NA_CHUNKS, NA_Q, LANE)


def _na_bias_grad(dtab, rows_n):
    rm, cm = _na_onehots(rows_n)
    d6 = dtab.reshape(NA_HEADS, NA_TYPES, NA_CHUNKS, 2, GRID_W, LANE)
    return jnp.einsum("htpqxk,tqpza,xkzb->hab", d6, rm, cm, precision=_ONEHOT_PRECISION)


def _na_bias(b_ref, h, typ):
    return jnp.concatenate([b_ref[0, h, typ, c] for c in range(NA_CHUNKS)], axis=1)


def _na_step(p, npairs, rows_n):
    ws = jnp.clip(2 * p - NA_KH // 2, 0, rows_n - NA_WROWS)
    koff = pl.multiple_of(ws * GRID_W, LANE)
    typ = jnp.where(p == 0, 0, jnp.where(p == 1, 1, jnp.where(p == npairs - 2, 3, jnp.where(p == npairs - 1, 4, 2))))
    return _rows(p * NA_Q, NA_Q), pl.ds(koff, NA_K), typ


def _na_fwd_call(proj, btab):
    T = proj.shape[0]
    npairs, rows_n = T // NA_Q, T // GRID_W

    def body(q_ref, k_ref, v_ref, b_ref, o_ref):
        lane = lax.broadcasted_iota(jnp.int32, (NA_Q, LANE), 1)
        m0 = (lane < HEAD_DIM).astype(F32)
        m1 = 1.0 - m0

        def steps(i, carry):
            idx = [_na_step(NA_UNROLL * i + u, npairs, rows_n) for u in range(NA_UNROLL)]
            chains = [(u, h, m) for u in range(NA_UNROLL) for h, m in ((0, m0), (1, m1))]
            kws = [k_ref[krows, :].astype(_BF) for _, krows, _ in idx]
            vws = [v_ref[krows, :].astype(_BF) for _, krows, _ in idx]
            s = [_mm_nt(q_ref[idx[u][0], :] * m, kws[u]) for u, h, m in chains]
            s = [s_ * (HEAD_DIM ** -0.5) + _na_bias(b_ref, h, idx[u][2]) for s_, (u, h, m) in zip(s, chains)]
            e = [jnp.exp(s_ - jnp.max(s_, axis=-1, keepdims=True)) for s_ in s]
            pr = [e_ / jnp.sum(e_, axis=-1, keepdims=True) for e_ in e]
            ov = [_mm(pr_, vws[u]) * m for pr_, (u, h, m) in zip(pr, chains)]
            for u in range(NA_UNROLL):
                o_ref[idx[u][0], :] = ov[2 * u] + ov[2 * u + 1]
            return carry

        lax.fori_loop(0, npairs // NA_UNROLL, steps, 0)

    return pl.pallas_call(
        body, name="na_fwd", grid=(NA_WIDTH // LANE,),
        in_specs=[_strip(T, lambda j: j + 18), _strip(T, lambda j: j + 20), _strip(T, lambda j: j + 22),
                  pl.BlockSpec((1, 2, NA_TYPES, NA_CHUNKS, NA_Q, LANE), lambda j: (j, 0, 0, 0, 0, 0))],
        out_specs=_strip(T, lambda j: j, buffers=1),
        out_shape=jax.ShapeDtypeStruct((T, NA_WIDTH), F32),
        compiler_params=_cparams(("arbitrary",)),
    )(proj, proj, proj, btab)


def _na_bwd_call(proj, dycat, btab, dp):
    T = proj.shape[0]
    npairs, rows_n = T // NA_Q, T // GRID_W
    scale = HEAD_DIM ** -0.5

    def body(q_ref, k_ref, v_ref, do_ref, b_ref, dp_in_ref, dp_out_ref, db_ref, dka_ref, dva_ref, stage_ref, sems):
        dq_ref = stage_ref.at[0]
        lane = lax.broadcasted_iota(jnp.int32, (NA_Q, LANE), 1)
        m0 = (lane < HEAD_DIM).astype(F32)
        m1 = 1.0 - m0
        dka_ref[...] = jnp.zeros_like(dka_ref)
        dva_ref[...] = jnp.zeros_like(dva_ref)
        db_ref[...] = jnp.zeros_like(db_ref)

        def steps(i, carry):
            idx = [_na_step(NA_UNROLL * i + u, npairs, rows_n) for u in range(NA_UNROLL)]
            chains = [(u, h, m) for u in range(NA_UNROLL) for h, m in ((0, m0), (1, m1))]
            kws = [k_ref[krows, :].astype(_BF) for _, krows, _ in idx]
            vws = [v_ref[krows, :].astype(_BF) for _, krows, _ in idx]
            qm = [(q_ref[idx[u][0], :] * m).astype(_BF) for u, h, m in chains]
            dom = [(do_ref[idx[u][0], :] * m).astype(_BF) for u, h, m in chains]
            s = [_mm_nt(qm_, kws[u]) for qm_, (u, h, m) in zip(qm, chains)]
            dpr = [_mm_nt(dom_, vws[u]) for dom_, (u, h, m) in zip(dom, chains)]
            s = [s_ * scale + _na_bias(b_ref, h, idx[u][2]) for s_, (u, h, m) in zip(s, chains)]
            e = [jnp.exp(s_ - jnp.max(s_, axis=-1, keepdims=True)) for s_ in s]
            pr = [e_ / jnp.sum(e_, axis=-1, keepdims=True) for e_ in e]
            ds = [pr_ * (dpr_ - jnp.sum(pr_ * dpr_, axis=-1, keepdims=True)) for pr_, dpr_ in zip(pr, dpr)]
            dsb = [(ds_ * scale).astype(_BF) for ds_ in ds]
            dq = [_mm(dsb_, kws[u]) * m for dsb_, (u, h, m) in zip(dsb, chains)]
            dk = [_mm_tn(dsb_, qm_) for dsb_, qm_ in zip(dsb, qm)]
            dv = [_mm_tn(pr_, dom_) for pr_, dom_ in zip(pr, dom)]
            for ds_, (u, h, m) in zip(ds, chains):
                for c in range(NA_CHUNKS):
                    db_ref[0, h, idx[u][2], c] += ds_[:, c * LANE:(c + 1) * LANE]
            for u in range(NA_UNROLL):
                qrows, krows, _ = idx[u]
                dq_ref[qrows, :] = (dq[2 * u] + dq[2 * u + 1]).astype(dq_ref.dtype)
                dka_ref[krows, :] += dk[2 * u] + dk[2 * u + 1]
                dva_ref[krows, :] += dv[2 * u] + dv[2 * u + 1]
            return carry

        lax.fori_loop(0, npairs // NA_UNROLL, steps, 0)
        stage_ref[1] = dka_ref[...].astype(stage_ref.dtype)
        stage_ref[2] = dva_ref[...].astype(stage_ref.dtype)
        j = pl.program_id(0)
        _store_strips(stage_ref, dp_out_ref, (j + 18, j + 20, j + 22), sems)

    tab = pl.BlockSpec((1, 2, NA_TYPES, NA_CHUNKS, NA_Q, LANE), lambda j: (j, 0, 0, 0, 0, 0))
    return pl.pallas_call(
        body, name="na_bwd", grid=(NA_WIDTH // LANE,),
        in_specs=[_strip(T, lambda j: j + 18), _strip(T, lambda j: j + 20), _strip(T, lambda j: j + 22),
                  _strip(T, lambda j: j + 6), tab, pl.BlockSpec(memory_space=pl.ANY)],
        out_specs=[pl.BlockSpec(memory_space=pl.ANY), tab],
        out_shape=[jax.ShapeDtypeStruct(dp.shape, dp.dtype),
                   jax.ShapeDtypeStruct((2, 2, NA_TYPES, NA_CHUNKS, NA_Q, LANE), F32)],
        scratch_shapes=[pltpu.VMEM((T, LANE), F32), pltpu.VMEM((T, LANE), F32),
                        pltpu.VMEM((3, T, LANE), _BF), pltpu.SemaphoreType.DMA((3,))],
        input_output_aliases={5: 0},
        compiler_params=_cparams(("arbitrary",)),
    )(proj, proj, proj, dycat, btab, dp)


W_BLK = IN_WIDTH // N_DEV
MXU_W = 256
N_BLK = 4 * MXU_W
N_STEPS = IN_WIDTH // N_BLK
TM = 512


def _ln_fwd(z, g, b):
    zc = z - jnp.mean(z, axis=-1, keepdims=True)
    var = jnp.mean(zc * zc, axis=-1, keepdims=True)
    return zc * lax.rsqrt(var + LN_EPS) * g + b


def _ln_bwd(dy, z, g):
    zc = z - jnp.mean(z, axis=-1, keepdims=True)
    rstd = lax.rsqrt(jnp.mean(zc * zc, axis=-1, keepdims=True) + LN_EPS)
    xhat = zc * rstd
    dxh = dy * g
    dz = rstd * (dxh - jnp.mean(dxh, axis=-1, keepdims=True) - xhat * jnp.mean(dxh * xhat, axis=-1, keepdims=True))
    return dz, dy * xhat


def _row_tile(T):
    return 1024 if T % 1024 == 0 else TM


def _halves(n):
    return (pl.ds(0, n // 2), pl.ds(n // 2, n // 2))


def _inproj_call(xb, w, after):
    T = xb.shape[0]
    tm = _row_tile(T)

    def body(x_ref, w_ref, after_ref, o_ref):
        o_ref[...] = _mm(x_ref[...], w_ref[...])

    return pl.pallas_call(
        body, name="inproj", grid=(T // tm, N_STEPS),
        in_specs=[pl.BlockSpec((tm, D_MODEL), lambda i, n: (i, 0)),
                  pl.BlockSpec((D_MODEL, N_BLK), lambda i, n: (0, n)),
                  pl.BlockSpec(memory_space=pl.ANY)],
        out_specs=pl.BlockSpec((tm, N_BLK), lambda i, n: (i, n)),
        out_shape=jax.ShapeDtypeStruct((T, IN_WIDTH), F32),
        compiler_params=_cparams(("parallel", "arbitrary")),
    )(xb, w, after)


def _vec_spec():
    return pl.BlockSpec((1, D_MODEL), lambda *_: (0, 0))


def _outproj_ln_call(y_lru, y_ret, y_na, x, w, g, b, after):
    T = x.shape[0]

    def body(yl_ref, yr_ref, yn_ref, x_ref, w_ref, g_ref, b_ref, after_ref, z_ref, x1_ref, x1b_ref, yc_ref):
        yc_ref[:, 0:LRU_WIDTH] = yl_ref[...].astype(yc_ref.dtype)
        yc_ref[:, LRU_WIDTH:LRU_WIDTH + RET_WIDTH] = yr_ref[...].astype(yc_ref.dtype)
        yc_ref[:, LRU_WIDTH + RET_WIDTH:] = yn_ref[...].astype(yc_ref.dtype)
        z = ALPHA * x_ref[...] + _mm(yc_ref[...], w_ref[...])
        z_ref[...] = z
        x1 = _ln_fwd(z, g_ref[...], b_ref[...])
        x1_ref[...] = x1
        x1b_ref[...] = x1.astype(x1b_ref.dtype)

    row = lambda w_: pl.BlockSpec((TM, w_), lambda i: (i, 0))
    return pl.pallas_call(
        body, name="outproj_ln", grid=(T // TM,),
        in_specs=[row(LRU_WIDTH), row(RET_WIDTH), row(NA_WIDTH), row(D_MODEL),
                  pl.BlockSpec((D_MODEL, D_MODEL), lambda i: (0, 0)), _vec_spec(), _vec_spec(),
                  pl.BlockSpec(memory_space=pl.ANY)],
        out_specs=[row(D_MODEL)] * 4,
        out_shape=[jax.ShapeDtypeStruct((T, D_MODEL), F32), jax.ShapeDtypeStruct((T, D_MODEL), F32),
                   jax.ShapeDtypeStruct((T, D_MODEL), _BF), jax.ShapeDtypeStruct((T, D_MODEL), _BF)],
        compiler_params=_cparams(("parallel",)),
    )(y_lru, y_ret, y_na, x, w, g, b, after)


def _ffn_ln_call(x1, x1b, wg, wu, wd, g, b):
    T = x1.shape[0]

    def body(x_ref, xb_ref, wg_ref, wu_ref, wd_ref, g_ref, b_ref, z_ref, x2_ref, x2b_ref, gp_ref, up_ref, acc_ref):
        n = pl.program_id(1)

        @pl.when(n == 0)
        def _():
            acc_ref[...] = jnp.zeros_like(acc_ref)

        r0, r1 = _halves(TM)

        def pre(rows):
            xb = xb_ref[rows, :]
            return _mm(xb, wg_ref[...]), _mm(xb, wu_ref[...])

        def act(rows, gp, up):
            gp_ref[rows, :] = gp.astype(gp_ref.dtype)
            up_ref[rows, :] = up.astype(up_ref.dtype)
            return (gp * _sigmoid(gp) * up).astype(_BF)

        gp0, up0 = pre(r0)
        hid0 = act(r0, gp0, up0)
        gp1, up1 = pre(r1)
        acc_ref[r0, :] += _mm(hid0, wd_ref[...])
        hid1 = act(r1, gp1, up1)
        acc_ref[r1, :] += _mm(hid1, wd_ref[...])

        @pl.when(n == N_STEPS - 1)
        def _():
            z = ALPHA * x_ref[...] + acc_ref[...]
            z_ref[...] = z
            x2 = _ln_fwd(z, g_ref[...], b_ref[...])
            x2_ref[...] = x2
            x2b_ref[...] = x2.astype(x2b_ref.dtype)

    row = pl.BlockSpec((TM, D_MODEL), lambda i, n: (i, 0))
    return pl.pallas_call(
        body, name="ffn_ln", grid=(T // TM, N_STEPS),
        in_specs=[row, row,
                  pl.BlockSpec((D_MODEL, N_BLK), lambda i, n: (0, n)),
                  pl.BlockSpec((D_MODEL, N_BLK), lambda i, n: (0, n)),
                  pl.BlockSpec((N_BLK, D_MODEL), lambda i, n: (n, 0)), _vec_spec(), _vec_spec()],
        out_specs=[row] * 3 + [pl.BlockSpec((TM, N_BLK), lambda i, n: (i, n))] * 2,
        out_shape=[jax.ShapeDtypeStruct((T, D_MODEL), F32), jax.ShapeDtypeStruct((T, D_MODEL), F32),
                   jax.ShapeDtypeStruct((T, D_MODEL), _BF),
                   jax.ShapeDtypeStruct((T, IN_WIDTH), _BF), jax.ShapeDtypeStruct((T, IN_WIDTH), _BF)],
        scratch_shapes=[pltpu.VMEM((TM, D_MODEL), F32)],
        compiler_params=_cparams(("parallel", "arbitrary")),
    )(x1, x1b, wg, wu, wd, g, b)


def _loss_call(y, t):
    T = y.shape[0]

    def body(y_ref, t_ref, dy_ref, loss_ref):
        @pl.when(pl.program_id(0) == 0)
        def _():
            loss_ref[...] = jnp.zeros_like(loss_ref)

        err = y_ref[...] - t_ref[...]
        dy_ref[...] = err * (1.0 / D_MODEL)
        part = 0.5 * jnp.sum(jnp.mean(err * err, axis=-1, keepdims=True), axis=0, keepdims=True)
        loss_ref[...] += jnp.broadcast_to(part, loss_ref.shape)

    row = pl.BlockSpec((TM, D_MODEL), lambda i: (i, 0))
    return pl.pallas_call(
        body, name="loss", grid=(T // TM,),
        in_specs=[row, row],
        out_specs=[row, pl.BlockSpec((SUB, LANE), lambda i: (0, 0))],
        out_shape=[jax.ShapeDtypeStruct((T, D_MODEL), F32), jax.ShapeDtypeStruct((SUB, LANE), F32)],
        compiler_params=_cparams(("arbitrary",)),
    )(y, t)


def _ffn_bwd_call(dx2, z2, gpb, upb, wg, wu, wd, g, after):
    T = dx2.shape[0]

    def body(dx2_ref, z_ref, gp_ref, up_ref, wg_ref, wu_ref, wd_ref, g_ref, after_ref,
             dx1_ref, dgp_ref, dup_ref, hid_ref, dzb_ref, dln_ref, acc_ref):
        i, n = pl.program_id(0), pl.program_id(1)

        @pl.when((i == 0) & (n == 0))
        def _():
            dln_ref[...] = jnp.zeros_like(dln_ref)

        @pl.when(n == 0)
        def _():
            dy = dx2_ref[...]
            dz, dg_rows = _ln_bwd(dy, z_ref[...], g_ref[...])
            dzb_ref[...] = dz.astype(dzb_ref.dtype)
            acc_ref[...] = ALPHA * dz
            dln_ref[0:1, :] += jnp.sum(dg_rows, axis=0, keepdims=True)
            dln_ref[1:2, :] += jnp.sum(dy, axis=0, keepdims=True)

        r0, r1 = _halves(TM)

        def grads(rows, dhid):
            gp = gp_ref[rows, :].astype(F32)
            up = up_ref[rows, :].astype(F32)
            sg = _sigmoid(gp)
            act = gp * sg
            hid_ref[rows, :] = (act * up).astype(hid_ref.dtype)
            dup = (dhid * act).astype(_BF)
            dgp = (dhid * up * (sg * (1.0 + gp * (1.0 - sg)))).astype(_BF)
            dgp_ref[rows, :] = dgp.astype(dgp_ref.dtype)
            dup_ref[rows, :] = dup.astype(dup_ref.dtype)
            return dgp, dup

        dhid0 = _mm_nt(dzb_ref[r0, :], wd_ref[...])
        dhid1 = _mm_nt(dzb_ref[r1, :], wd_ref[...])
        dgp0, dup0 = grads(r0, dhid0)
        acc_ref[r0, :] += _mm_nt(dgp0, wg_ref[...]) + _mm_nt(dup0, wu_ref[...])
        dgp1, dup1 = grads(r1, dhid1)
        acc_ref[r1, :] += _mm_nt(dgp1, wg_ref[...]) + _mm_nt(dup1, wu_ref[...])

        @pl.when(n == N_STEPS - 1)
        def _():
            dx1_ref[...] = acc_ref[...]

    row = pl.BlockSpec((TM, D_MODEL), lambda i, n: (i, 0))
    blk = pl.BlockSpec((TM, N_BLK), lambda i, n: (i, n))
    return pl.pallas_call(
        body, name="ffn_bwd", grid=(T // TM, N_STEPS),
        in_specs=[row, row, blk, blk,
                  pl.BlockSpec((D_MODEL, N_BLK), lambda i, n: (0, n)),
                  pl.BlockSpec((D_MODEL, N_BLK), lambda i, n: (0, n)),
                  pl.BlockSpec((N_BLK, D_MODEL), lambda i, n: (n, 0)), _vec_spec(),
                  pl.BlockSpec(memory_space=pl.ANY)],
        out_specs=[row, blk, blk, blk, row, pl.BlockSpec((SUB, D_MODEL), lambda i, n: (0, 0))],
        out_shape=[jax.ShapeDtypeStruct((T, D_MODEL), F32),
                   jax.ShapeDtypeStruct((T, IN_WIDTH), _BF), jax.ShapeDtypeStruct((T, IN_WIDTH), _BF),
                   jax.ShapeDtypeStruct((T, IN_WIDTH), _BF), jax.ShapeDtypeStruct((T, D_MODEL), _BF),
                   jax.ShapeDtypeStruct((SUB, D_MODEL), F32)],
        scratch_shapes=[pltpu.VMEM((TM, D_MODEL), F32)],
        compiler_params=_cparams(("arbitrary", "arbitrary")),
    )(dx2, z2, gpb, upb, wg, wu, wd, g, after)


def _outproj_bwd_call(dx1, z1, w, g):
    T = dx1.shape[0]

    def body(dx_ref, z_ref, w_ref, g_ref, dzb_ref, dyc_ref, dres_ref, dln_ref):
        @pl.when(pl.program_id(0) == 0)
        def _():
            dln_ref[...] = jnp.zeros_like(dln_ref)

        dy = dx_ref[...]
        dz, dg_rows = _ln_bwd(dy, z_ref[...], g_ref[...])
        dzb_ref[...] = dz.astype(dzb_ref.dtype)
        dres_ref[...] = ALPHA * dz
        dyc_ref[...] = _mm_nt(dz, w_ref[...])
        dln_ref[0:1, :] += jnp.sum(dg_rows, axis=0, keepdims=True)
        dln_ref[1:2, :] += jnp.sum(dy, axis=0, keepdims=True)

    row = pl.BlockSpec((TM, D_MODEL), lambda i: (i, 0))
    return pl.pallas_call(
        body, name="outproj_bwd", grid=(T // TM,),
        in_specs=[row, row, pl.BlockSpec((D_MODEL, D_MODEL), lambda i: (0, 0)), _vec_spec()],
        out_specs=[row, row, row, pl.BlockSpec((SUB, D_MODEL), lambda i: (0, 0))],
        out_shape=[jax.ShapeDtypeStruct((T, D_MODEL), _BF), jax.ShapeDtypeStruct((T, D_MODEL), F32),
                   jax.ShapeDtypeStruct((T, D_MODEL), F32), jax.ShapeDtypeStruct((SUB, D_MODEL), F32)],
        compiler_params=_cparams(("arbitrary",)),
    )(dx1, z1, w, g)


def _inproj_bwd_call(dres, dp, w):
    T = dres.shape[0]

    def body(dres_ref, dp_ref, w_ref, dx_ref):
        dx_ref[...] = dres_ref[...] + _mm_nt(dp_ref[...], w_ref[...])

    row = pl.BlockSpec((TM, D_MODEL), lambda i: (i, 0))
    return pl.pallas_call(
        body, name="inproj_bwd", grid=(T // TM,),
        in_specs=[row, pl.BlockSpec((TM, IN_WIDTH), lambda i: (i, 0)),
                  pl.BlockSpec((D_MODEL, IN_WIDTH), lambda i: (0, 0), pipeline_mode=pl.Buffered(1))],
        out_specs=row,
        out_shape=jax.ShapeDtypeStruct((T, D_MODEL), F32),
        compiler_params=_cparams(("parallel",)),
    )(dres, dp, w)


def _tn_cols_call(a, b, name):
    T, ka = a.shape
    n = b.shape[1]

    def body(a_ref, b_ref, o_ref):
        o_ref[...] = _mm_tn(a_ref[...], b_ref[...]).astype(o_ref.dtype)

    return pl.pallas_call(
        body, name=name, grid=(n // N_BLK,),
        in_specs=[pl.BlockSpec((T, ka), lambda j: (0, 0), pipeline_mode=pl.Buffered(1)),
                  pl.BlockSpec((T, N_BLK), lambda j: (0, j))],
        out_specs=pl.BlockSpec((ka, N_BLK), lambda j: (0, j)),
        out_shape=jax.ShapeDtypeStruct((ka, n), _BF),
        compiler_params=_cparams(("parallel",)),
    )(a, b)


def _tn_rows_call(a, b, kb, name):
    T, ka = a.shape
    n = b.shape[1]

    def body(a_ref, b_ref, o_ref):
        o_ref[...] = _mm_tn(a_ref[...], b_ref[...]).astype(o_ref.dtype)

    return pl.pallas_call(
        body, name=name, grid=(ka // kb,),
        in_specs=[pl.BlockSpec((T, kb), lambda r: (0, r)),
                  pl.BlockSpec((T, n), lambda r: (0, 0), pipeline_mode=pl.Buffered(1))],
        out_specs=pl.BlockSpec((kb, n), lambda r: (r, 0)),
        out_shape=jax.ShapeDtypeStruct((ka, n), _BF),
        compiler_params=_cparams(("parallel",)),
    )(a, b)


def _me():
    return lax.axis_index("x"), lax.axis_index("y"), lax.axis_index("c")


def _flip(k):
    x, y, c = _me()
    return (1 - x if k & 4 else x, 1 - y if k & 2 else y, 1 - c if k & 1 else c)


def _dev_index(pos):
    return 4 * pos[0] + 2 * pos[1] + pos[2]


_HBM = pl.BlockSpec(memory_space=pltpu.HBM)
_SEM = pl.BlockSpec(memory_space=pltpu.SEMAPHORE)


def _land_shape(shape, mode):
    if mode == "all":
        return (N_DEV,) + shape
    if mode == "cols":
        return (shape[0], N_DEV * shape[1])
    if mode == "blk":
        return shape
    assert mode == "scols"
    return (N_DEV, shape[0], shape[1] // N_DEV)


def _comm_copies(ins, lands, modes, send_sems, recv_sems):
    me = _dev_index(_me())
    copies = []
    for k in range(N_DEV):
        peer = _flip(k)
        pidx = _dev_index(peer)
        for a, (src, land, mode) in enumerate(zip(ins, lands, modes)):
            if mode == "blk":
                src = src.at[pidx]
            elif mode == "scols":
                w = src.shape[1] // N_DEV
                src = src.at[:, pl.ds(pl.multiple_of(pidx * w, LANE), w)]
            if mode == "cols":
                w = src.shape[1]
                dst = land.at[:, pl.ds(pl.multiple_of(me * w, LANE), w)]
            else:
                dst = land.at[me]
            copies.append(pltpu.make_async_remote_copy(
                src_ref=src, dst_ref=dst, send_sem=send_sems.at[k * len(ins) + a], recv_sem=recv_sems.at[k * len(ins) + a],
                device_id=peer, device_id_type=MESH))
    return copies


def _comm_start_call(arrs, gather_flags, after, name):
    n = len(arrs)
    lands = [lax.empty(_land_shape(v.shape, mode), v.dtype) for v, mode in zip(arrs, gather_flags)]

    def body(*refs):
        ins, lnd = refs[:n], refs[n:2 * n]
        send_sems, recv_sems = refs[2 * n + len(after)], refs[2 * n + len(after) + 1]
        for cp in _comm_copies(ins, lnd, gather_flags, send_sems, recv_sems):
            cp.start()
        refs[-1][...] = jnp.zeros_like(refs[-1])

    hbm = [pltpu.with_memory_space_constraint(v, pltpu.HBM) for v in list(arrs) + lands]
    out = pl.pallas_call(
        body, name=name,
        out_shape=(pltpu.SemaphoreType.DMA((N_DEV * n,)), pltpu.SemaphoreType.DMA((N_DEV * n,)),
                   *[pltpu.HBM(v.shape, v.dtype) for v in hbm], jax.ShapeDtypeStruct((SUB, LANE), F32)),
        in_specs=[_HBM] * (2 * n) + [pl.BlockSpec(memory_space=pl.ANY)] * len(after),
        out_specs=(_SEM, _SEM, *[_HBM] * (2 * n), pl.BlockSpec(memory_space=pltpu.VMEM)),
        input_output_aliases={i: 2 + i for i in range(2 * n)},
        compiler_params=pltpu.CompilerParams(has_side_effects=pltpu.SideEffectType.DATAFLOW_SIDE_EFFECTING),
    )(*hbm, *after)
    return out[:-1], out[-1]


def _comm_wait_call(state, gather_flags, after, name):
    n = len(gather_flags)
    send_sems, recv_sems, thru = state[0], state[1], state[2:]

    def body(*refs):
        ins, lnd, ssem, rsem = refs[:n], refs[n:2 * n], refs[2 * n], refs[2 * n + 1]
        for cp in _comm_copies(ins, lnd, gather_flags, ssem, rsem):
            cp.wait_send()
            cp.wait_recv()

    out = pl.pallas_call(
        body, name=name,
        out_shape=tuple(pltpu.HBM(v.shape, v.dtype) for v in thru),
        in_specs=[_HBM] * (2 * n) + [_SEM, _SEM] + [pl.BlockSpec(memory_space=pl.ANY)] * len(after),
        out_specs=tuple([_HBM] * (2 * n)),
        input_output_aliases={i: i for i in range(2 * n)},
        compiler_params=pltpu.CompilerParams(has_side_effects=pltpu.SideEffectType.DATAFLOW_SIDE_EFFECTING),
    )(*thru, send_sems, recv_sems, *after)
    return out[n:]


def _sum8_call(recv, stacked, layer, nl, rows, r_out, c_out, name, transposed=False):
    c = recv.shape[2]

    def body(x_ref, *rest):
        o_ref = rest[-1]
        if transposed:
            acc = x_ref[0].astype(F32)
            for s in range(1, N_DEV):
                acc = acc + x_ref[s].astype(F32)
            o_ref[...] = acc.T[:c_out, :]
        else:
            acc = x_ref[0, :, :c_out].astype(F32)
            for s in range(1, N_DEV):
                acc = acc + x_ref[s, :, :c_out].astype(F32)
            o_ref[...] = acc

    prev = [] if stacked is None else [stacked]
    if transposed:
        out_spec = pl.BlockSpec((None, c_out, rows), lambda i: (layer, 0, i))
        out_shape = jax.ShapeDtypeStruct((nl, c_out, r_out), F32)
    else:
        out_spec = pl.BlockSpec((None, rows, c_out), lambda i: (layer, i, 0))
        out_shape = jax.ShapeDtypeStruct((nl, r_out, c_out), F32)
    return pl.pallas_call(
        body, name=name, grid=(r_out // rows,),
        in_specs=[pl.BlockSpec((N_DEV, rows, c), lambda i: (0, i, 0))] + [pl.BlockSpec(memory_space=pl.ANY)] * len(prev),
        out_specs=out_spec, out_shape=out_shape,
        input_output_aliases={1: 0} if prev else {},
        compiler_params=_cparams(("parallel",)),
    )(recv, *prev)


def _adamw_call(w, g, m, v, rows, name):
    r, c = w.shape

    def body(w_ref, g_ref, m_ref, v_ref, d_ref, nm_ref, nv_ref):
        gr = g_ref[...]
        nm = ADAM_B1 * m_ref[...] + (1.0 - ADAM_B1) * gr
        nv = ADAM_B2 * v_ref[...] + (1.0 - ADAM_B2) * (gr * gr)
        m_hat = nm / (1.0 - ADAM_B1 ** ADAM_STEP)
        v_hat = nv / (1.0 - ADAM_B2 ** ADAM_STEP)
        d_ref[...] = -ADAM_LR * (m_hat / (jnp.sqrt(v_hat) + ADAM_EPS) + ADAM_WD * w_ref[...])
        nm_ref[...] = nm
        nv_ref[...] = nv

    spec = pl.BlockSpec((rows, c), lambda i: (i, 0))
    return pl.pallas_call(
        body, name=name, grid=(r // rows,),
        in_specs=[spec] * 4, out_specs=[spec] * 3,
        out_shape=[jax.ShapeDtypeStruct((r, c), F32)] * 3,
        compiler_params=_cparams(("parallel",)),
    )(w, g, m, v)


SH_ROWS = 16
SH_W = LRU_WIDTH // N_DEV
REP_ROWS = 824
_REP_SIZES = (LRU_WIDTH, 2 * 6 * 64 * 64, 2 * 6 * 64 * 64, RET_WIDTH, 1920, D_MODEL, D_MODEL, D_MODEL, D_MODEL)
_RPB_SIZE = NA_HEADS * (2 * NA_KH - 1) * (2 * NA_KW - 1)


def _pack_sh(cw, ba, bx, lam):
    return jnp.concatenate([cw, ba, bx, lam], axis=0)


def _pad_sh(p):
    pad = [(0, 0)] * (p.ndim - 2) + [(0, SH_ROWS - p.shape[-2]), (0, LANE - p.shape[-1])]
    return jnp.pad(p, pad)


def _pack_rep(cb, wa, wx, gnw, rpb, l1g, l1b, l2g, l2b):
    flat = jnp.concatenate([cb.reshape(-1), wa.reshape(-1), wx.reshape(-1), gnw.reshape(-1),
                            jnp.pad(rpb.reshape(-1), (0, 1920 - _RPB_SIZE)), l1g, l1b, l2g, l2b,
                            jnp.zeros((REP_ROWS * LANE - sum(_REP_SIZES),), F32)])
    return flat.reshape(REP_ROWS, LANE)


def _unpack_rep(p):
    nl = p.shape[0]
    flat = p.reshape(nl, -1)
    out, off = [], 0
    for size in _REP_SIZES:
        out.append(flat[:, off:off + size])
        off += size
    cb, wa, wx, gnw, rpb, l1g, l1b, l2g, l2b = out
    return (cb, wa.reshape(nl, 2, 6, 64, 64), wx.reshape(nl, 2, 6, 64, 64), gnw,
            rpb[:, :_RPB_SIZE].reshape(nl, NA_HEADS, 2 * NA_KH - 1, 2 * NA_KW - 1), l1g, l1b, l2g, l2b)


def _adamw_nd(w, g, m, v, rows, name):
    shp = w.shape
    f = lambda t: t.reshape(-1, shp[-1])
    rows = f(w).shape[0] if rows is None else rows
    return [t.reshape(shp) for t in _adamw_call(f(w), f(g), f(m), f(v), rows, name)]


def kernel(x, w_in, conv_w, conv_b, lru_w_a, lru_b_a, lru_w_x, lru_b_x, lru_lam, ret_gn_w, na_rpb, w_out, ln1_g, ln1_b, w_gate, w_up, w_down, ln2_g, ln2_b, loss_target, m_w_in, m_conv_w, m_conv_b, m_lru_w_a, m_lru_b_a, m_lru_w_x, m_lru_b_x, m_lru_lam, m_ret_gn_w, m_na_rpb, m_w_out, m_ln1_g, m_ln1_b, m_w_gate, m_w_up, m_w_down, m_ln2_g, m_ln2_b, v_w_in, v_conv_w, v_conv_b, v_lru_w_a, v_lru_b_a, v_lru_w_x, v_lru_b_x, v_lru_lam, v_ret_gn_w, v_na_rpb, v_w_out, v_ln1_g, v_ln1_b, v_w_gate, v_w_up, v_w_down, v_ln2_g, v_ln2_b):
    nl = w_in.shape[0]
    T = x.shape[1]
    rows_n = T // GRID_W
    x0, target = x[0], loss_target[0]
    ffpad = W_BLK - FF_BLK

    win_b = w_in.astype(_BF)
    wg_b = jnp.pad(w_gate, ((0, 0), (0, 0), (0, ffpad))).astype(_BF)
    wu_b = jnp.pad(w_up, ((0, 0), (0, 0), (0, ffpad))).astype(_BF)
    wd_b = jnp.pad(w_down, ((0, 0), (0, ffpad), (0, 0))).astype(_BF)
    wout_b = w_out.astype(_BF)
    def agf_start(l, after):
        sh = _pad_sh(_pack_sh(conv_w[l], lru_b_a[l], lru_b_x[l], lru_lam[l]))
        arrs, modes = [win_b[l], sh], ["cols", "all"]
        if l > 0:
            arrs, modes = arrs + [wd_b[l]], modes + ["all"]
        return _comm_start_call(arrs, modes, after, f"agf_start{l}"), modes

    def agk_start(l, after):
        arrs, modes = [wg_b[l], wu_b[l], wout_b[l]], ["cols", "cols", "all"]
        if l == 0:
            arrs, modes = arrs + [wd_b[l]], modes + ["all"]
        return _comm_start_call(arrs, modes, after, f"agk_start{l}"), modes

    tables = _ret_tables(T)
    w4_all = _lru_w4(lru_w_a, lru_w_x)
    layers = []
    gathered = []
    xs, xb = x0, x0.astype(_BF)
    (agf_state, token), agf_modes = agf_start(0, [])
    tie = 0.0 * token[0, 0]
    btabs = [_na_bias_tables(na_rpb[l] + tie, rows_n) for l in range(nl)]
    for l in range(nl):
        front = _comm_wait_call(agf_state, agf_modes, [xb] + (btabs if l == 0 else []), f"agf_wait{l}")
        win, shg = front[0], front[1]
        (agk_state, token), agk_modes = agk_start(l, [shg])
        full = shg[:, :10, :SH_W].transpose(1, 0, 2).reshape(10, LRU_WIDTH)
        vec, w4 = _lru_vec(full[0:4], conv_b[l], full[4:6], full[6:8], full[8:10]), w4_all[l]
        gnw8 = jnp.pad(ret_gn_w[l][None], ((0, SUB - 1), (0, 0)))
        btab = btabs[l]
        proj = _inproj_call(xb, win, token)
        y_lru = _lru_fwd_call(proj, vec, w4)
        y_ret = _ret_fwd_call(proj, tables, gnw8)
        y_na = _na_fwd_call(proj, btab)
        back = _comm_wait_call(agk_state, agk_modes, [y_na], f"agk_wait{l}")
        wg, wu, wout = back[0], back[1], back[2]
        wd = (back[3] if l == 0 else front[2]).reshape(IN_WIDTH, D_MODEL)
        wout = wout.reshape(D_MODEL, D_MODEL)
        gathered.append((win, wg, wu, wd, wout))
        if l + 1 < nl:
            (agf_state, token), agf_modes = agf_start(l + 1, [wout])
        z1, x1, x1b, ycb = _outproj_ln_call(y_lru, y_ret, y_na, xs, wout, ln1_g[l][None], ln1_b[l][None], token)
        z2, x2, x2b, gpb, upb = _ffn_ln_call(x1, x1b, wg, wu, wd, ln2_g[l][None], ln2_b[l][None])
        layers.append(dict(xb=xb, proj=proj, vec=vec, w4=w4, gnw8=gnw8, btab=btab,
                           z1=z1, x1b=x1b, ycb=ycb, z2=z2, gpb=gpb, upb=upb))
        xs, xb = x2, x2b

    dx, loss_blk = _loss_call(xs, target)
    loss = lax.psum(loss_blk[0, 0], ("x", "y", "c"))

    gxa_flags = ["scols", "scols", "blk", "blk"]
    gxb_flags = ["scols", "blk", "all"]
    gxa_state, gxb_state = [None] * nl, [None] * nl
    token = loss_blk
    for l in reversed(range(nl)):
        s = layers[l]
        win, wg, wu, wd, wout = gathered[l]
        dx1, dgp, dup, hid, dz2b, dln2 = _ffn_bwd_call(dx, s["z2"], s["gpb"], s["upb"], wg, wu, wd, ln2_g[l][None], token)
        dwg = _tn_cols_call(s["x1b"], dgp, "tn_cols")
        dwu = _tn_cols_call(s["x1b"], dup, "tn_cols")
        dwd = _tn_rows_call(hid, dz2b, N_BLK, "tn_rows_down").reshape(N_DEV, W_BLK, D_MODEL)
        dz1b, dyc, dres, dln1 = _outproj_bwd_call(dx1, s["z1"], wout, ln1_g[l][None])
        dwout = _tn_rows_call(s["ycb"], dz1b, D_MODEL // 2, "tn_rows_out").reshape(N_DEV, LANE, D_MODEL)
        gxa_state[l], token = _comm_start_call([dwg, dwu, dwd, dwout], gxa_flags, [], f"gxa_start{l}")
        dp, dvec, dw4 = _lru_bwd_call(s["proj"], dyc, s["vec"], s["w4"], token)
        dp, dgnw = _ret_bwd_call(s["proj"], dyc, tables, s["gnw8"], dp)
        dp, dbias = _na_bwd_call(s["proj"], dyc, s["btab"], dp)
        dwin = _tn_cols_call(s["xb"], dp, "tn_cols")
        dx = _inproj_bwd_call(dres, dp, win)
        dcw, dcb, dwa, dba, dwx, dbx, dlam = _lru_unpack(dvec, dw4)
        rep = _pack_rep(dcb, dwa, dwx, dgnw[0], _na_bias_grad(dbias, rows_n), dln1[0], dln1[1], dln2[0], dln2[1])
        sh = _pack_sh(dcw, dba, dbx, dlam).reshape(10, N_DEV, SH_W).transpose(1, 0, 2)
        gxb_state[l], token = _comm_start_call([dwin, _pad_sh(sh), rep], gxb_flags, [], f"gxb_start{l}")

    g_w_in = g_w_gate = g_w_up = g_w_down = g_w_out = g_shp = g_repp = None
    after = [dx, token]
    big = {}
    for l in reversed(range(nl)):
        ra = _comm_wait_call(gxa_state[l], gxa_flags, after, f"gxa_wait{l}")
        g_w_gate = _sum8_call(ra[0], g_w_gate, l, nl, TM, D_MODEL, FF_BLK, "sum8_ff", transposed=True)
        g_w_up = _sum8_call(ra[1], g_w_up, l, nl, TM, D_MODEL, FF_BLK, "sum8_ff", transposed=True)
        g_w_down = _sum8_call(ra[2], g_w_down, l, nl, FF_BLK, FF_BLK, D_MODEL, "sum8_down")
        g_w_out = _sum8_call(ra[3], g_w_out, l, nl, LANE, LANE, D_MODEL, "sum8_out")
        after = [g_w_out]
        if l == 0:
            tr = lambda t: jnp.swapaxes(t, 1, 2)
            big["w_gate"] = [tr(t) for t in _adamw_nd(tr(w_gate), g_w_gate, tr(m_w_gate), tr(v_w_gate), FF_BLK, "adamw_down")]
            big["w_up"] = [tr(t) for t in _adamw_nd(tr(w_up), g_w_up, tr(m_w_up), tr(v_w_up), FF_BLK, "adamw_down")]
            g_w_gate, g_w_up = tr(g_w_gate), tr(g_w_up)
            big["w_down"] = _adamw_nd(w_down, g_w_down, m_w_down, v_w_down, FF_BLK, "adamw_down")
            big["w_out"] = _adamw_nd(w_out, g_w_out, m_w_out, v_w_out, LANE, "adamw_out")
            after = [big[n][k] for n in ("w_gate", "w_up", "w_down", "w_out") for k in range(3)]
        rb = _comm_wait_call(gxb_state[l], gxb_flags, after, f"gxb_wait{l}")
        g_w_in = _sum8_call(rb[0], g_w_in, l, nl, TM, D_MODEL, W_BLK, "sum8_in")
        g_shp = _sum8_call(rb[1], g_shp, l, nl, SH_ROWS, SH_ROWS, LANE, "sum8_sh")
        g_repp = _sum8_call(rb[2], g_repp, l, nl, REP_ROWS, REP_ROWS, LANE, "sum8_rep")
        after = [g_repp]

    big["w_in"] = _adamw_nd(w_in, g_w_in, m_w_in, v_w_in, TM, "adamw_in")
    g_shp = g_shp[:, :, :SH_W]
    rep_names = ("conv_b", "lru_w_a", "lru_w_x", "ret_gn_w", "na_rpb", "ln1_g", "ln1_b", "ln2_g", "ln2_b")
    grads = {"w_in": g_w_in, "w_gate": g_w_gate, "w_up": g_w_up, "w_down": g_w_down, "w_out": g_w_out,
             "conv_w": g_shp[:, 0:4], "lru_b_a": g_shp[:, 4:6], "lru_b_x": g_shp[:, 6:8], "lru_lam": g_shp[:, 8:10]}
    grads.update(dict(zip(rep_names, _unpack_rep(g_repp))))
    small = {
        "conv_w": (conv_w, m_conv_w, v_conv_w), "conv_b": (conv_b, m_conv_b, v_conv_b),
        "lru_w_a": (lru_w_a, m_lru_w_a, v_lru_w_a), "lru_b_a": (lru_b_a, m_lru_b_a, v_lru_b_a),
        "lru_w_x": (lru_w_x, m_lru_w_x, v_lru_w_x), "lru_b_x": (lru_b_x, m_lru_b_x, v_lru_b_x),
        "lru_lam": (lru_lam, m_lru_lam, v_lru_lam), "ret_gn_w": (ret_gn_w, m_ret_gn_w, v_ret_gn_w),
        "na_rpb": (na_rpb, m_na_rpb, v_na_rpb), "ln1_g": (ln1_g, m_ln1_g, v_ln1_g), "ln1_b": (ln1_b, m_ln1_b, v_ln1_b),
        "ln2_g": (ln2_g, m_ln2_g, v_ln2_g), "ln2_b": (ln2_b, m_ln2_b, v_ln2_b),
    }
    for name, (w_, m_, v_) in small.items():
        big[name] = _adamw_nd(w_, grads[name], m_, v_, None, "adamw_small")
    kinds = [{n: big[n][k] for n in big} for k in range(3)]
    order = ("w_in", "conv_w", "conv_b", "lru_w_a", "lru_b_a", "lru_w_x", "lru_b_x", "lru_lam", "ret_gn_w", "na_rpb",
             "w_out", "ln1_g", "ln1_b", "w_gate", "w_up", "w_down", "ln2_g", "ln2_b")
    outs = [loss, dx[None]]
    for d in (grads, *kinds):
        outs.extend(d[n] for n in order)
    return tuple(outs)
```

```python
import functools
import math

import numpy as np
import jax
import jax.numpy as jnp
from jax import lax
from jax.experimental import pallas as pl
from jax.experimental.pallas import tpu as pltpu

F32 = jnp.float32
_BF = jnp.bfloat16

D_MODEL = 1024
DEPTH = 4
GRID_W = 64
HEAD_DIM = 64
LRU_WIDTH = 384
RET_WIDTH = 384
RET_HEADS = 6
NA_WIDTH = 256
NA_HEADS = 4
IN_WIDTH = 3072
CONV_WIDTH = 4
LRU_C = 8.0
RET_CHUNK = 128
ROPE_BASE = 10000.0
GN_EPS = 1e-6
NA_KH = 8
NA_KW = 16
D_FF = 2816
FF_BLK = 352
N_DEV = 8
ALPHA = (2 * DEPTH) ** 0.25
LN_EPS = 1e-5
ADAM_LR = 0.001
ADAM_B1 = 0.9
ADAM_B2 = 0.999
ADAM_EPS = 1e-08
ADAM_WD = 0.01
ADAM_STEP = 10

LANE = 128
SUB = 8
VMEM_MB = 56
NEG = -1e30

MESH = pl.DeviceIdType.MESH


def _cparams(sem=None, vmem_mb=VMEM_MB):
    return pltpu.CompilerParams(dimension_semantics=sem, vmem_limit_bytes=vmem_mb << 20)


def _mm(a, b):
    return jnp.dot(a.astype(_BF), b.astype(_BF), preferred_element_type=F32)


def _mm_nt(a, b):
    return lax.dot_general(a.astype(_BF), b.astype(_BF), (((1,), (1,)), ((), ())), preferred_element_type=F32)


def _mm_tn(a, b):
    return lax.dot_general(a.astype(_BF), b.astype(_BF), (((0,), (0,)), ((), ())), preferred_element_type=F32)


def _sigmoid(x):
    return jax.nn.sigmoid(x)


def _rows(start, size):
    return pl.ds(pl.multiple_of(start, SUB), size)


def _loop2(n, body, init):
    assert n % 2 == 0
    return lax.fori_loop(0, n // 2, lambda i, c: body(2 * i + 1, body(2 * i, c)), init)


def _strip(T, col, buffers=2):
    return pl.BlockSpec((T, LANE), lambda j: (0, col(j)), pipeline_mode=pl.Buffered(buffers))


LRU_CH = 1024
_GELU_C0 = math.sqrt(2.0 / math.pi)
_GELU_C1 = 0.044715


def _gelu_parts(x):
    x2 = x * x
    t = jnp.tanh(_GELU_C0 * (x + _GELU_C1 * x * x2))
    val = 0.5 * x * (1.0 + t)
    der = 0.5 * (1.0 + t) + 0.5 * x * (1.0 - t * t) * _GELU_C0 * (1.0 + 3.0 * _GELU_C1 * x2)
    return val, der


def _softplus_neg(lam):
    e = jnp.exp(-jnp.abs(lam))
    w = 1.0 + e
    l1p = jnp.where(w == 1.0, e, jnp.log(w) * (e / jnp.where(w == 1.0, 1.0, w - 1.0)))
    return jnp.maximum(-lam, 0.0) + l1p


def _window(ref, t0, ch, T):
    prev = ref[_rows(jnp.maximum(t0 - SUB, 0), SUB), :].astype(F32)
    nxt = ref[_rows(jnp.minimum(t0 + ch, T - SUB), SUB), :].astype(F32)
    prev = jnp.where(t0 > 0, prev, 0.0)
    nxt = jnp.where(t0 + ch < T, nxt, 0.0)
    return jnp.concatenate([prev, ref[_rows(t0, ch), :].astype(F32), nxt], axis=0)


def _tap(win, shift, ch):
    n = win.shape[0]
    return pltpu.roll(win, (-shift) % n, 0)[SUB:SUB + ch]


def _lru_conv(xb_ref, vec, t0, T):
    win = _window(xb_ref, t0, LRU_CH, T)
    xc = jnp.broadcast_to(vec[4:5, :], (LRU_CH, LANE))
    for j in range(CONV_WIDTH):
        xc = xc + _tap(win, j - CONV_WIDTH // 2, LRU_CH) * vec[j:j + 1, :]
    return xc


def _lru_dir(pre_a, pre_x, sp):
    r = _sigmoid(pre_a)
    i = _sigmoid(pre_x)
    log_a = (-LRU_C) * r * sp
    a = jnp.exp(log_a)
    z = jnp.tanh(-log_a) * (a * a + 1.0)
    s = jnp.sqrt(z)
    return r, i, a, s


def _scan_tile(a, b, reverse, row):
    for k in (1, 2, 4):
        if not reverse:
            a_s, b_s, m = pltpu.roll(a, k, 0), pltpu.roll(b, k, 0), row >= k
        else:
            a_s, b_s, m = pltpu.roll(a, SUB - k, 0), pltpu.roll(b, SUB - k, 0), row < SUB - k
        b = jnp.where(m, a * b_s + b, b)
        a = jnp.where(m, a * a_s, a)
    return a, b


def _bcast_row(x, r):
    return jnp.broadcast_to(x[r:r + 1, :], (SUB, LANE))


def _lru_prepare(xb_ref, w4_ref, vec, xc_ref, af_ref, uf_ref, ab_ref, ub_ref, T):
    sp_f = _softplus_neg(vec[9:10, :])
    sp_b = _softplus_neg(vec[10:11, :])
    w4 = w4_ref[0]

    def body(c, carry):
        t0 = c * LRU_CH
        xc = _lru_conv(xb_ref, vec, t0, T)
        if xc_ref is not None:
            xc_ref[_rows(t0, LRU_CH), :] = xc
        pre = _mm(xc, w4)
        _, i, a, s = _lru_dir(pre[:, 0:128] + vec[5:6, :], pre[:, 128:256] + vec[6:7, :], sp_f)
        af_ref[_rows(t0, LRU_CH), :] = a
        uf_ref[_rows(t0, LRU_CH), :] = s * (i * xc)
        _, i, a, s = _lru_dir(pre[:, 256:384] + vec[7:8, :], pre[:, 384:512] + vec[8:9, :], sp_b)
        ab_ref[_rows(t0, LRU_CH), :] = a
        ub_ref[_rows(t0, LRU_CH), :] = s * (i * xc)
        return carry

    lax.fori_loop(0, T // LRU_CH, body, 0)


def _lru_scan(af_ref, uf_ref, ab_ref, ub_ref, T):
    nt = T // SUB
    row = lax.broadcasted_iota(jnp.int32, (SUB, LANE), 0)

    def body(j, carry):
        hf, hb = carry
        sf = _rows(j * SUB, SUB)
        sb = _rows((nt - 1 - j) * SUB, SUB)
        a, b = _scan_tile(af_ref[sf, :], uf_ref[sf, :], False, row)
        h = a * hf + b
        uf_ref[sf, :] = h
        hf = _bcast_row(h, SUB - 1)
        a, b = _scan_tile(ab_ref[sb, :], ub_ref[sb, :], True, row)
        h = a * hb + b
        ub_ref[sb, :] = h
        hb = _bcast_row(h, 0)
        return hf, hb

    z = jnp.zeros((SUB, LANE), F32)
    lax.fori_loop(0, nt, body, (z, z))


def _lru_fwd_call(proj, vec, w4):
    T = proj.shape[0]

    def body(xb_ref, gate_ref, vec_ref, w4_ref, y_ref, af_ref, uf_ref, ab_ref, ub_ref):
        vec = vec_ref[...]
        _lru_prepare(xb_ref, w4_ref, vec, None, af_ref, uf_ref, ab_ref, ub_ref, T)
        _lru_scan(af_ref, uf_ref, ab_ref, ub_ref, T)

        def out(c, carry):
            rows = _rows(c * LRU_CH, LRU_CH)
            gl, _ = _gelu_parts(gate_ref[rows, :])
            y_ref[rows, :] = ((uf_ref[rows, :] + ub_ref[rows, :]) * gl).astype(y_ref.dtype)
            return carry

        lax.fori_loop(0, T // LRU_CH, out, 0)

    return pl.pallas_call(
        body, name="lru_fwd", grid=(LRU_WIDTH // LANE,),
        in_specs=[_strip(T, lambda j: j), _strip(T, lambda j: j + 3),
                  pl.BlockSpec((16, LANE), lambda j: (0, j)),
                  pl.BlockSpec((1, LANE, 4 * LANE), lambda j: (j, 0, 0))],
        out_specs=_strip(T, lambda j: j, buffers=1),
        out_shape=jax.ShapeDtypeStruct((T, LRU_WIDTH), _BF),
        scratch_shapes=[pltpu.VMEM((T, LANE), F32)] * 4,
        compiler_params=_cparams(("arbitrary",)),
    )(proj, proj, vec, w4)


def _store_strips(stage_ref, dp_ref, cols, sems):
    copies = [pltpu.make_async_copy(stage_ref.at[b], dp_ref.at[:, pl.ds(pl.multiple_of(c * LANE, LANE), LANE)], sems.at[b])
              for b, c in enumerate(cols)]
    for cp in copies:
        cp.start()
    for cp in copies:
        cp.wait()


def _lru_bwd_call(proj, dycat, vec, w4, after):
    T = proj.shape[0]
    nt = T // SUB
    nch = T // LRU_CH

    def body(xb_ref, gate_ref, dy_ref, vec_ref, w4_ref, after_ref, dp_ref, dvec_ref, dw4_ref,
             xc_ref, af_ref, hf_ref, ab_ref, hb_ref, dh_ref, stage_ref, sems):
        dxb_ref, dgate_ref = stage_ref.at[0], stage_ref.at[1]
        vec = vec_ref[...]
        _lru_prepare(xb_ref, w4_ref, vec, xc_ref, af_ref, hf_ref, ab_ref, hb_ref, T)
        _lru_scan(af_ref, hf_ref, ab_ref, hb_ref, T)

        def gate_bwd(c, carry):
            rows = _rows(c * LRU_CH, LRU_CH)
            gl, dgl = _gelu_parts(gate_ref[rows, :])
            dy = dy_ref[rows, :]
            dgate_ref[rows, :] = (dy * (hf_ref[rows, :] + hb_ref[rows, :]) * dgl).astype(dgate_ref.dtype)
            dh_ref[rows, :] = dy * gl
            return carry

        lax.fori_loop(0, nch, gate_bwd, 0)

        row = lax.broadcasted_iota(jnp.int32, (SUB, LANE), 0)

        def adj(j, carry):
            gf, a_next, gb, a_prev = carry
            tf = nt - 1 - j
            sf = _rows(tf * SUB, SUB)
            a_t = af_ref[sf, :]
            h_t = hf_ref[sf, :]
            coef = jnp.where(row == SUB - 1, a_next, pltpu.roll(a_t, SUB - 1, 0))
            ac, bc = _scan_tile(coef, dh_ref[sf, :], True, row)
            g = ac * gf + bc
            h_prev = hf_ref[_rows(jnp.maximum(tf - 1, 0) * SUB, SUB), :]
            h_prev = jnp.where(tf > 0, _bcast_row(h_prev, SUB - 1), 0.0)
            hs = jnp.where(row == 0, h_prev, pltpu.roll(h_t, 1, 0))
            af_ref[sf, :] = g * hs
            hf_ref[sf, :] = g
            gf = _bcast_row(g, 0)
            a_next = _bcast_row(a_t, 0)
            sb = _rows(j * SUB, SUB)
            a_t = ab_ref[sb, :]
            h_t = hb_ref[sb, :]
            coef = jnp.where(row == 0, a_prev, pltpu.roll(a_t, 1, 0))
            ac, bc = _scan_tile(coef, dh_ref[sb, :], False, row)
            g = ac * gb + bc
            h_next = hb_ref[_rows(jnp.minimum(j + 1, nt - 1) * SUB, SUB), :]
            h_next = jnp.where(j < nt - 1, _bcast_row(h_next, 0), 0.0)
            hs = jnp.where(row == SUB - 1, h_next, pltpu.roll(h_t, SUB - 1, 0))
            ab_ref[sb, :] = g * hs
            hb_ref[sb, :] = g
            gb = _bcast_row(g, SUB - 1)
            a_prev = _bcast_row(a_t, SUB - 1)
            return gf, a_next, gb, a_prev

        z = jnp.zeros((SUB, LANE), F32)
        lax.fori_loop(0, nt, adj, (z, z, z, z))

        sp_f = _softplus_neg(vec[9:10, :])
        sp_b = _softplus_neg(vec[10:11, :])
        w4 = w4_ref[0]
        dw4_ref[...] = jnp.zeros_like(dw4_ref)

        def one_dir(pre_a, pre_x, sp, xc, du, da):
            r, i, a, s = _lru_dir(pre_a, pre_x, sp)
            d_i = du * s * xc
            dxc = du * s * i
            d_s = du * i * xc
            d_log = da * a - d_s * (a * a) / s
            d_r = d_log * (-LRU_C) * sp
            d_sp = jnp.sum(d_log * (-LRU_C) * r, axis=0, keepdims=True)
            return d_r * r * (1.0 - r), d_i * i * (1.0 - i), dxc, d_sp

        def gates_bwd(c, carry):
            db, dspf, dspb = carry
            rows = _rows(c * LRU_CH, LRU_CH)
            xc = xc_ref[rows, :]
            pre = _mm(xc, w4)
            dpa_f, dpx_f, dxc_f, d_sp_f = one_dir(pre[:, 0:128] + vec[5:6, :], pre[:, 128:256] + vec[6:7, :],
                                                  sp_f, xc, hf_ref[rows, :], af_ref[rows, :])
            dpa_b, dpx_b, dxc_b, d_sp_b = one_dir(pre[:, 256:384] + vec[7:8, :], pre[:, 384:512] + vec[8:9, :],
                                                  sp_b, xc, hb_ref[rows, :], ab_ref[rows, :])
            dpre = jnp.concatenate([dpa_f, dpx_f, dpa_b, dpx_b], axis=1)
            dw4_ref[0] += _mm_tn(xc, dpre)
            dh_ref[rows, :] = dxc_f + dxc_b + _mm_nt(dpre, w4)
            return db + jnp.sum(dpre, axis=0, keepdims=True), dspf + d_sp_f, dspb + d_sp_b

        z1 = jnp.zeros((1, LANE), F32)
        db, dspf, dspb = lax.fori_loop(0, nch, gates_bwd, (jnp.zeros((1, 4 * LANE), F32), z1, z1))

        def conv_bwd(c, carry):
            t0 = c * LRU_CH
            rows = _rows(t0, LRU_CH)
            dwin = _window(dh_ref, t0, LRU_CH, T)
            xwin = _window(xb_ref, t0, LRU_CH, T)
            dxc = dh_ref[rows, :]
            dxb = jnp.zeros((LRU_CH, LANE), F32)
            out = []
            for j in range(CONV_WIDTH):
                off = j - CONV_WIDTH // 2
                dxb = dxb + _tap(dwin, -off, LRU_CH) * vec[j:j + 1, :]
                out.append(carry[j] + jnp.sum(dxc * _tap(xwin, off, LRU_CH), axis=0, keepdims=True))
            dxb_ref[rows, :] = dxb.astype(dxb_ref.dtype)
            out.append(carry[CONV_WIDTH] + jnp.sum(dxc, axis=0, keepdims=True))
            return tuple(out)

        dconv = lax.fori_loop(0, nch, conv_bwd, (z1,) * (CONV_WIDTH + 1))
        dlam_f = dspf * (-_sigmoid(-vec[9:10, :]))
        dlam_b = dspb * (-_sigmoid(-vec[10:11, :]))
        dvec_ref[...] = jnp.concatenate(
            list(dconv) + [db[:, 0:128], db[:, 128:256], db[:, 256:384], db[:, 384:512], dlam_f, dlam_b,
                           jnp.zeros((5, LANE), F32)], axis=0)
        j = pl.program_id(0)
        _store_strips(stage_ref, dp_ref, (j, j + 3), sems)

    ns = LRU_WIDTH // LANE
    return pl.pallas_call(
        body, name="lru_bwd", grid=(ns,),
        in_specs=[_strip(T, lambda j: j), _strip(T, lambda j: j + 3), _strip(T, lambda j: j),
                  pl.BlockSpec((16, LANE), lambda j: (0, j)),
                  pl.BlockSpec((1, LANE, 4 * LANE), lambda j: (j, 0, 0)),
                  pl.BlockSpec(memory_space=pl.ANY)],
        out_specs=[pl.BlockSpec(memory_space=pl.ANY),
                   pl.BlockSpec((16, LANE), lambda j: (0, j)),
                   pl.BlockSpec((1, LANE, 4 * LANE), lambda j: (j, 0, 0))],
        out_shape=[jax.ShapeDtypeStruct((T, IN_WIDTH), _BF),
                   jax.ShapeDtypeStruct((16, LRU_WIDTH), F32), jax.ShapeDtypeStruct((ns, LANE, 4 * LANE), F32)],
        scratch_shapes=[pltpu.VMEM((T, LANE), F32)] * 6 + [pltpu.VMEM((2, T, LANE), _BF), pltpu.SemaphoreType.DMA((2,))],
        compiler_params=_cparams(("arbitrary",)),
    )(proj, proj, dycat, vec, w4, after)


def _lru_vec(cw, cb, ba, bx, lam):
    return jnp.concatenate([cw, cb[None], ba[0:1], bx[0:1], ba[1:2], bx[1:2], lam, jnp.zeros((5, LRU_WIDTH), F32)], axis=0)


def _lru_w4(wa, wx):
    nl = wa.shape[0]
    w = jnp.stack([wa[:, 0], wx[:, 0], wa[:, 1], wx[:, 1]], axis=1)
    w = w.reshape(nl, 4, 3, 2, 64, 64)
    eye = jnp.eye(2, dtype=w.dtype)
    bd = w[:, :, :, :, :, None, :] * eye[None, None, None, :, None, :, None]
    bd = bd.reshape(nl, 4, 3, LANE, LANE)
    return bd.transpose(0, 2, 3, 1, 4).reshape(nl, 3, LANE, 4 * LANE).astype(_BF)


def _lru_unpack(dvec, dw4):
    def blocks(m):
        m = m.reshape(3, 2, 64, 2, 64)
        return jnp.stack([m[:, 0, :, 0, :], m[:, 1, :, 1, :]], axis=1).reshape(6, 64, 64)
    parts = [blocks(dw4[:, :, k * LANE:(k + 1) * LANE]) for k in range(4)]
    dwa = jnp.stack([parts[0], parts[2]])
    dwx = jnp.stack([parts[1], parts[3]])
    dba = jnp.stack([dvec[5], dvec[7]])
    dbx = jnp.stack([dvec[6], dvec[8]])
    return dvec[0:4], dvec[4], dwa, dba, dwx, dbx, dvec[9:11]


RC = 2 * RET_CHUNK


def _ret_tables(T):
    half = HEAD_DIM // 2
    pos = jnp.arange(T, dtype=F32)
    inv_freq = ROPE_BASE ** (-jnp.arange(half, dtype=F32) / half)
    ang = pos[:, None] * inv_freq[None, :]
    cos = jnp.tile(jnp.cos(ang), (1, 4))
    sin = jnp.tile(jnp.concatenate([-jnp.sin(ang), jnp.sin(ang)], axis=1), (1, 2))
    log_g = jnp.log1p(-jnp.exp2(-5.0 - jnp.arange(RET_HEADS, dtype=F32)))
    idx = jnp.arange(RC, dtype=F32)
    dec = jnp.exp(jnp.abs(idx[:, None] - idx[None, :]) * log_g[:, None, None])
    lg = jnp.repeat(log_g, HEAD_DIM).reshape(3, 1, LANE)
    col = idx[None, :, None]
    rtab = jnp.stack([jnp.exp((RC - 1 - col) * lg), jnp.exp(col * lg),
                      jnp.exp((col + 1.0) * lg), jnp.exp((RC - col) * lg)], axis=1)
    gch = jnp.broadcast_to(jnp.exp(RC * lg), (3, SUB, LANE))
    return cos, sin, dec, rtab, gch


def _swap32(x, lane):
    return jnp.where((lane & 32) == 0, pltpu.roll(x, LANE - 32, 1), pltpu.roll(x, 32, 1))


def _head_mean(x, m0, m1):
    s0 = jnp.sum(x * m0, axis=-1, keepdims=True)
    s1 = jnp.sum(x * m1, axis=-1, keepdims=True)
    return (s0 * m0 + s1 * m1) * (1.0 / HEAD_DIM)


def _ret_masks():
    lane = lax.broadcasted_iota(jnp.int32, (RC, LANE), 1)
    m0 = (lane < HEAD_DIM).astype(F32)
    r = lax.broadcasted_iota(jnp.int32, (LANE, LANE), 0) // HEAD_DIM
    c = lax.broadcasted_iota(jnp.int32, (LANE, LANE), 1) // HEAD_DIM
    return lane, m0, 1.0 - m0, (r == c).astype(F32)


def _ret_specs(T):
    const = lambda shape, imap: pl.BlockSpec(shape, imap)
    return [_strip(T, lambda j: j + 6), _strip(T, lambda j: j + 9), _strip(T, lambda j: j + 12),
            _strip(T, lambda j: j + 15),
            pl.BlockSpec((T, LANE), lambda j: (0, 0), pipeline_mode=pl.Buffered(1)),
            pl.BlockSpec((T, LANE), lambda j: (0, 0), pipeline_mode=pl.Buffered(1)),
            const((2, RC, RC), lambda j: (j, 0, 0)),
            const((1, 4, RC, LANE), lambda j: (j, 0, 0, 0)),
            const((1, SUB, LANE), lambda j: (j, 0, 0)),
            const((SUB, LANE), lambda j: (0, j))]


def _ret_fwd_call(proj, tables, gnw8):
    T = proj.shape[0]
    nc = T // RC
    cos, sin, dec, rtab, gch = tables

    def body(q_ref, k_ref, v_ref, g_ref, cos_ref, sin_ref, dec_ref, rtab_ref, gch_ref, gnw_ref, y_ref, stf_ref, kr_ref):
        lane, m0, m1, bd = _ret_masks()
        gch_v = gch_ref[0][0:1, :]
        gnw = gnw_ref[0:1, :]
        dkf, dkb, dqf, dqb = rtab_ref[0, 0], rtab_ref[0, 1], rtab_ref[0, 2], rtab_ref[0, 3]

        def rope(x, rows):
            return x * cos_ref[rows, :] + _swap32(x, lane) * sin_ref[rows, :]

        def pass_a(n, st):
            rows = _rows(n * RC, RC)
            stf_ref[n] = st
            kr = rope(k_ref[rows, :], rows) * (HEAD_DIM ** -0.5)
            kr_ref[rows, :] = kr
            return gch_v * st + _mm_tn(kr * dkf, v_ref[rows, :]) * bd

        _loop2(nc, pass_a, jnp.zeros((LANE, LANE), F32))

        def pass_b(i, stb):
            ns = [nc - 1 - 2 * i, nc - 2 - 2 * i]
            rows = [_rows(n * RC, RC) for n in ns]
            heads = ((0, m0), (1, m1))
            qr = [rope(q_ref[r, :], r) for r in rows]
            kr = [kr_ref[r, :] for r in rows]
            v = [v_ref[r, :] for r in rows]
            kv = [_mm_tn(kr[c] * dkb, v[c]) * bd for c in range(2)]
            stbs = [stb, gch_v * stb + kv[0]]
            s = [[_mm_nt(qr[c] * m, kr[c]) * dec_ref[h] for h, m in heads] for c in range(2)]
            o = [_mm(qr[c] * dqf, stf_ref[ns[c]]) + _mm(qr[c] * dqb, stbs[c]) for c in range(2)]
            o = [o[c] + _mm(s[c][0], v[c] * m0) + _mm(s[c][1], v[c] * m1) for c in range(2)]
            oc = [o_ - _head_mean(o_, m0, m1) for o_ in o]
            on = [oc_ * lax.rsqrt(_head_mean(oc_ * oc_, m0, m1) + GN_EPS) for oc_ in oc]
            for c in range(2):
                g = g_ref[rows[c], :]
                y_ref[rows[c], :] = ((g * _sigmoid(g)) * (on[c] * gnw)).astype(y_ref.dtype)
            return gch_v * stbs[1] + kv[1]

        assert nc % 2 == 0
        lax.fori_loop(0, nc // 2, pass_b, jnp.zeros((LANE, LANE), F32))

    return pl.pallas_call(
        body, name="ret_fwd", grid=(RET_WIDTH // LANE,),
        in_specs=_ret_specs(T),
        out_specs=_strip(T, lambda j: j, buffers=1),
        out_shape=jax.ShapeDtypeStruct((T, RET_WIDTH), _BF),
        scratch_shapes=[pltpu.VMEM((nc, LANE, LANE), F32), pltpu.VMEM((T, LANE), F32)],
        compiler_params=_cparams(("arbitrary",)),
    )(proj, proj, proj, proj, cos, sin, dec, rtab, gch, gnw8)


def _ret_bwd_call(proj, dycat, tables, gnw8, dp):
    T = proj.shape[0]
    nc = T // RC
    cos, sin, dec, rtab, gch = tables

    def body(q_ref, k_ref, v_ref, g_ref, cos_ref, sin_ref, dec_ref, rtab_ref, gch_ref, gnw_ref, dy_ref, dp_in_ref,
             dp_out_ref, dgnw_ref, stf_ref, dstb_ref, dkr_ref, dv_ref, dp_ref, kr_ref, sems):
        lane, m0, m1, bd = _ret_masks()
        gch_v = gch_ref[0][0:1, :]
        gnw = gnw_ref[0:1, :]
        dkf, dkb, dqf, dqb = rtab_ref[0, 0], rtab_ref[0, 1], rtab_ref[0, 2], rtab_ref[0, 3]
        scale = HEAD_DIM ** -0.5
        zst = jnp.zeros((LANE, LANE), F32)

        def rope(x, rows):
            return x * cos_ref[rows, :] + _swap32(x, lane) * sin_ref[rows, :]

        def rope_t(d, rows):
            return d * cos_ref[rows, :] + _swap32(d * sin_ref[rows, :], lane)

        def pass_a(n, st):
            rows = _rows(n * RC, RC)
            stf_ref[n] = st
            kr = rope(k_ref[rows, :], rows) * scale
            kr_ref[rows, :] = kr
            return gch_v * st + _mm_tn(kr * dkf, v_ref[rows, :]) * bd

        _loop2(nc, pass_a, zst)

        def pass_b(i, carry):
            stb, d_f, dgnw = carry
            two = range(2)
            heads = ((0, m0), (1, m1))
            ns = [nc - 1 - 2 * i, nc - 2 - 2 * i]
            rows = [_rows(n * RC, RC) for n in ns]
            qr = [rope(q_ref[r, :], r) for r in rows]
            kr = [kr_ref[r, :] for r in rows]
            v = [v_ref[r, :] for r in rows]
            stf = [stf_ref[n] for n in ns]
            kvb = [_mm_tn(kr[c] * dkb, v[c]) * bd for c in two]
            stbs = [stb, gch_v * stb + kvb[0]]
            qf = [qr[c] * dqf for c in two]
            qb = [qr[c] * dqb for c in two]
            s = [[_mm_nt(qr[c] * m, kr[c]) * dec_ref[h] for h, m in heads] for c in two]
            o = [_mm(qf[c], stf[c]) + _mm(qb[c], stbs[c]) for c in two]
            o = [o[c] + _mm(s[c][0], v[c] * m0) + _mm(s[c][1], v[c] * m1) for c in two]
            oc = [o_ - _head_mean(o_, m0, m1) for o_ in o]
            rstd = [lax.rsqrt(_head_mean(oc_ * oc_, m0, m1) + GN_EPS) for oc_ in oc]
            on = [oc[c] * rstd[c] for c in two]
            do = []
            for c in two:
                g = g_ref[rows[c], :]
                sg = _sigmoid(g)
                dy = dy_ref[rows[c], :]
                dp_ref[3, rows[c], :] = (dy * (on[c] * gnw) * (sg * (1.0 + g * (1.0 - sg)))).astype(dp_ref.dtype)
                t = dy * (g * sg)
                dgnw = dgnw + jnp.sum(t * on[c], axis=0, keepdims=True)
                don = t * gnw
                do.append(rstd[c] * (don - _head_mean(don, m0, m1) - on[c] * _head_mean(don * on[c], m0, m1)))
            dstf = [_mm_tn(qf[c], do[c]) * bd for c in two]
            dfs = [d_f, dstf[0] + gch_v * d_f]
            ds = [[_mm_nt(do[c] * m, v[c]) * dec_ref[h] for h, m in heads] for c in two]
            dqr = [_mm_nt(do[c], stf[c]) * dqf + _mm_nt(do[c], stbs[c]) * dqb
                   + _mm(ds[c][0], kr[c] * m0) + _mm(ds[c][1], kr[c] * m1) for c in two]
            dkr = [_mm_nt(v[c], dfs[c]) * dkf + _mm_tn(ds[c][0], qr[c] * m0) + _mm_tn(ds[c][1], qr[c] * m1) for c in two]
            dv = [_mm(kr[c] * dkf, dfs[c]) + _mm_tn(s[c][0], do[c] * m0) + _mm_tn(s[c][1], do[c] * m1) for c in two]
            for c in two:
                dp_ref[0, rows[c], :] = rope_t(dqr[c], rows[c]).astype(dp_ref.dtype)
                dkr_ref[rows[c], :] = dkr[c]
                dv_ref[rows[c], :] = dv[c]
                dstb_ref[ns[c]] = _mm_tn(qb[c], do[c]) * bd
            return gch_v * stbs[1] + kvb[1], dstf[1] + gch_v * dfs[1], dgnw

        assert nc % 2 == 0
        _, _, dgnw = lax.fori_loop(0, nc // 2, pass_b, (zst, zst, jnp.zeros((1, LANE), F32)))
        dgnw_ref[...] = jnp.concatenate([dgnw, jnp.zeros((SUB - 1, LANE), F32)], axis=0)

        def pass_c(n, d_b):
            rows = _rows(n * RC, RC)
            kr = kr_ref[rows, :]
            v = v_ref[rows, :]
            dkr = dkr_ref[rows, :] + _mm_nt(v, d_b) * dkb
            dp_ref[1, rows, :] = (rope_t(dkr, rows) * scale).astype(dp_ref.dtype)
            dp_ref[2, rows, :] = (dv_ref[rows, :] + _mm(kr * dkb, d_b)).astype(dp_ref.dtype)
            return dstb_ref[n] + gch_v * d_b

        _loop2(nc, pass_c, zst)
        j = pl.program_id(0)
        _store_strips(dp_ref, dp_out_ref, (j + 6, j + 9, j + 12, j + 15), sems)

    n_in = len(_ret_specs(T)) + 1
    return pl.pallas_call(
        body, name="ret_bwd", grid=(RET_WIDTH // LANE,),
        in_specs=_ret_specs(T) + [_strip(T, lambda j: j + 3), pl.BlockSpec(memory_space=pl.ANY)],
        out_specs=[pl.BlockSpec(memory_space=pl.ANY), pl.BlockSpec((SUB, LANE), lambda j: (0, j))],
        out_shape=[jax.ShapeDtypeStruct(dp.shape, dp.dtype), jax.ShapeDtypeStruct((SUB, RET_WIDTH), F32)],
        scratch_shapes=[pltpu.VMEM((nc, LANE, LANE), F32), pltpu.VMEM((nc, LANE, LANE), F32),
                        pltpu.VMEM((T, LANE), F32), pltpu.VMEM((T, LANE), F32),
                        pltpu.VMEM((4, T, LANE), _BF), pltpu.VMEM((T, LANE), F32), pltpu.SemaphoreType.DMA((4,))],
        input_output_aliases={n_in: 0},
        compiler_params=_cparams(("arbitrary",)),
    )(proj, proj, proj, proj, cos, sin, dec, rtab, gch, gnw8, dycat, dp)


NA_Q = 2 * GRID_W
NA_WROWS = 10
NA_K = NA_WROWS * GRID_W
NA_CHUNKS = NA_K // LANE
NA_UNROLL = 4
NA_TYPES = 5
_ONEHOT_PRECISION = lax.Precision.HIGH


def _na_onehots(rows_n):
    reps = [(0, 0), (2, 0), (4, 0), (rows_n - 4, rows_n - NA_WROWS), (rows_n - 2, rows_n - NA_WROWS)]
    rm = np.zeros((NA_TYPES, 2, NA_WROWS, 2 * NA_KH - 1), np.float32)
    for t, (r, ws) in enumerate(reps):
        for qh in range(2):
            qrow = r + qh
            rstart = min(max(qrow - NA_KH // 2, 0), rows_n - NA_KH)
            for kh in range(NA_WROWS):
                krow = ws + kh
                if rstart <= krow < rstart + NA_KH:
                    rm[t, qh, kh, krow - qrow + NA_KH - 1] = 1.0
    cm = np.zeros((GRID_W, GRID_W, 2 * NA_KW - 1), np.float32)
    for qc in range(GRID_W):
        cstart = min(max(qc - NA_KW // 2, 0), GRID_W - NA_KW)
        for kc in range(cstart, cstart + NA_KW):
            cm[qc, kc, kc - qc + NA_KW - 1] = 1.0
    rm2 = rm.reshape(NA_TYPES, 2, NA_CHUNKS, 2, 2 * NA_KH - 1)
    cm2 = np.zeros((GRID_W, LANE, 2, 2 * NA_KW - 1), np.float32)
    for z in range(2):
        cm2[:, z * GRID_W:(z + 1) * GRID_W, z, :] = cm
    return rm2, cm2


def _na_bias_tables(rpb, rows_n):
    rm, cm = _na_onehots(rows_n)
    val = jnp.einsum("hab,tqpza,xkzb->htpqxk", rpb, rm, cm, precision=_ONEHOT_PRECISION)
    valid = np.einsum("tqpz,xkz->tpqxk", rm.sum(-1), cm.sum(-1)) > 0.5
    return jnp.where(valid[None], val, NEG).reshape(2, 2, NA_TYPES, NA_CHUNKS, NA_Q, LANE)


def _na_bias_grad(dtab, rows_n):
    rm, cm = _na_onehots(rows_n)
    d6 = dtab.reshape(NA_HEADS, NA_TYPES, NA_CHUNKS, 2, GRID_W, LANE)
    return jnp.einsum("htpqxk,tqpza,xkzb->hab", d6, rm, cm, precision=_ONEHOT_PRECISION)


def _na_bias(b_ref, h, typ):
    return jnp.concatenate([b_ref[0, h, typ, c] for c in range(NA_CHUNKS)], axis=1)


def _na_step(p, npairs, rows_n):
    ws = jnp.clip(2 * p - NA_KH // 2, 0, rows_n - NA_WROWS)
    koff = pl.multiple_of(ws * GRID_W, LANE)
    typ = jnp.where(p == 0, 0, jnp.where(p == 1, 1, jnp.where(p == npairs - 2, 3, jnp.where(p == npairs - 1, 4, 2))))
    return _rows(p * NA_Q, NA_Q), pl.ds(koff, NA_K), typ


def _na_fwd_call(proj, btab):
    T = proj.shape[0]
    npairs, rows_n = T // NA_Q, T // GRID_W

    def body(q_ref, k_ref, v_ref, b_ref, o_ref):
        lane = lax.broadcasted_iota(jnp.int32, (NA_Q, LANE), 1)
        m0 = (lane < HEAD_DIM).astype(F32)
        m1 = 1.0 - m0

        def steps(i, carry):
            idx = [_na_step(NA_UNROLL * i + u, npairs, rows_n) for u in range(NA_UNROLL)]
            chains = [(u, h, m) for u in range(NA_UNROLL) for h, m in ((0, m0), (1, m1))]
            kws = [k_ref[krows, :].astype(_BF) for _, krows, _ in idx]
            vws = [v_ref[krows, :].astype(_BF) for _, krows, _ in idx]
            s = [_mm_nt(q_ref[idx[u][0], :] * m, kws[u]) for u, h, m in chains]
            s = [s_ * (HEAD_DIM ** -0.5) + _na_bias(b_ref, h, idx[u][2]) for s_, (u, h, m) in zip(s, chains)]
            e = [jnp.exp(s_ - jnp.max(s_, axis=-1, keepdims=True)) for s_ in s]
            pr = [e_ / jnp.sum(e_, axis=-1, keepdims=True) for e_ in e]
            ov = [_mm(pr_, vws[u]) * m for pr_, (u, h, m) in zip(pr, chains)]
            for u in range(NA_UNROLL):
                o_ref[idx[u][0], :] = (ov[2 * u] + ov[2 * u + 1]).astype(o_ref.dtype)
            return carry

        lax.fori_loop(0, npairs // NA_UNROLL, steps, 0)

    return pl.pallas_call(
        body, name="na_fwd", grid=(NA_WIDTH // LANE,),
        in_specs=[_strip(T, lambda j: j + 18), _strip(T, lambda j: j + 20), _strip(T, lambda j: j + 22),
                  pl.BlockSpec((1, 2, NA_TYPES, NA_CHUNKS, NA_Q, LANE), lambda j: (j, 0, 0, 0, 0, 0))],
        out_specs=_strip(T, lambda j: j, buffers=1),
        out_shape=jax.ShapeDtypeStruct((T, NA_WIDTH), _BF),
        compiler_params=_cparams(("arbitrary",)),
    )(proj, proj, proj, btab)


def _na_bwd_call(proj, dycat, btab, dp):
    T = proj.shape[0]
    npairs, rows_n = T // NA_Q, T // GRID_W
    scale = HEAD_DIM ** -0.5

    def body(q_ref, k_ref, v_ref, do_ref, b_ref, dp_in_ref, dp_out_ref, db_ref, dka_ref, dva_ref, stage_ref, sems):
        dq_ref = stage_ref.at[0]
        lane = lax.broadcasted_iota(jnp.int32, (NA_Q, LANE), 1)
        m0 = (lane < HEAD_DIM).astype(F32)
        m1 = 1.0 - m0
        dka_ref[...] = jnp.zeros_like(dka_ref)
        dva_ref[...] = jnp.zeros_like(dva_ref)
        db_ref[...] = jnp.zeros_like(db_ref)

        def steps(i, carry):
            idx = [_na_step(NA_UNROLL * i + u, npairs, rows_n) for u in range(NA_UNROLL)]
            chains = [(u, h, m) for u in range(NA_UNROLL) for h, m in ((0, m0), (1, m1))]
            kws = [k_ref[krows, :].astype(_BF) for _, krows, _ in idx]
            vws = [v_ref[krows, :].astype(_BF) for _, krows, _ in idx]
            qm = [(q_ref[idx[u][0], :] * m).astype(_BF) for u, h, m in chains]
            dom = [(do_ref[idx[u][0], :] * m).astype(_BF) for u, h, m in chains]
            s = [_mm_nt(qm_, kws[u]) for qm_, (u, h, m) in zip(qm, chains)]
            dpr = [_mm_nt(dom_, vws[u]) for dom_, (u, h, m) in zip(dom, chains)]
            s = [s_ * scale + _na_bias(b_ref, h, idx[u][2]) for s_, (u, h, m) in zip(s, chains)]
            e = [jnp.exp(s_ - jnp.max(s_, axis=-1, keepdims=True)) for s_ in s]
            pr = [e_ / jnp.sum(e_, axis=-1, keepdims=True) for e_ in e]
            ds = [pr_ * (dpr_ - jnp.sum(pr_ * dpr_, axis=-1, keepdims=True)) for pr_, dpr_ in zip(pr, dpr)]
            dsb = [(ds_ * scale).astype(_BF) for ds_ in ds]
            dq = [_mm(dsb_, kws[u]) * m for dsb_, (u, h, m) in zip(dsb, chains)]
            dk = [_mm_tn(dsb_, qm_) for dsb_, qm_ in zip(dsb, qm)]
            dv = [_mm_tn(pr_, dom_) for pr_, dom_ in zip(pr, dom)]
            for ds_, (u, h, m) in zip(ds, chains):
                for c in range(NA_CHUNKS):
                    db_ref[0, h, idx[u][2], c] += ds_[:, c * LANE:(c + 1) * LANE]
            for u in range(NA_UNROLL):
                qrows, krows, _ = idx[u]
                dq_ref[qrows, :] = (dq[2 * u] + dq[2 * u + 1]).astype(dq_ref.dtype)
                dka_ref[krows, :] += dk[2 * u] + dk[2 * u + 1]
                dva_ref[krows, :] += dv[2 * u] + dv[2 * u + 1]
            return carry

        lax.fori_loop(0, npairs // NA_UNROLL, steps, 0)
        stage_ref[1] = dka_ref[...].astype(stage_ref.dtype)
        stage_ref[2] = dva_ref[...].astype(stage_ref.dtype)
        j = pl.program_id(0)
        _store_strips(stage_ref, dp_out_ref, (j + 18, j + 20, j + 22), sems)

    tab = pl.BlockSpec((1, 2, NA_TYPES, NA_CHUNKS, NA_Q, LANE), lambda j: (j, 0, 0, 0, 0, 0))
    return pl.pallas_call(
        body, name="na_bwd", grid=(NA_WIDTH // LANE,),
        in_specs=[_strip(T, lambda j: j + 18), _strip(T, lambda j: j + 20), _strip(T, lambda j: j + 22),
                  _strip(T, lambda j: j + 6), tab, pl.BlockSpec(memory_space=pl.ANY)],
        out_specs=[pl.BlockSpec(memory_space=pl.ANY), tab],
        out_shape=[jax.ShapeDtypeStruct(dp.shape, dp.dtype),
                   jax.ShapeDtypeStruct((2, 2, NA_TYPES, NA_CHUNKS, NA_Q, LANE), F32)],
        scratch_shapes=[pltpu.VMEM((T, LANE), F32), pltpu.VMEM((T, LANE), F32),
                        pltpu.VMEM((3, T, LANE), _BF), pltpu.SemaphoreType.DMA((3,))],
        input_output_aliases={5: 0},
        compiler_params=_cparams(("arbitrary",)),
    )(proj, proj, proj, dycat, btab, dp)


W_BLK = IN_WIDTH // N_DEV
MXU_W = 256
N_BLK = 4 * MXU_W
N_STEPS = IN_WIDTH // N_BLK
TM = 512


def _ln_fwd(z, g, b):
    zc = z - jnp.mean(z, axis=-1, keepdims=True)
    var = jnp.mean(zc * zc, axis=-1, keepdims=True)
    return zc * lax.rsqrt(var + LN_EPS) * g + b


def _ln_bwd(dy, z, g):
    zc = z - jnp.mean(z, axis=-1, keepdims=True)
    rstd = lax.rsqrt(jnp.mean(zc * zc, axis=-1, keepdims=True) + LN_EPS)
    xhat = zc * rstd
    dxh = dy * g
    dz = rstd * (dxh - jnp.mean(dxh, axis=-1, keepdims=True) - xhat * jnp.mean(dxh * xhat, axis=-1, keepdims=True))
    return dz, dy * xhat


def _row_tile(T):
    return 1024 if T % 1024 == 0 else TM


def _halves(n):
    return (pl.ds(0, n // 2), pl.ds(n // 2, n // 2))


def _inproj_call(xb, w, after):
    T = xb.shape[0]
    tm = _row_tile(T)

    def body(x_ref, w_ref, after_ref, o_ref):
        o_ref[...] = _mm(x_ref[...], w_ref[...])

    return pl.pallas_call(
        body, name="inproj", grid=(T // tm, N_STEPS),
        in_specs=[pl.BlockSpec((tm, D_MODEL), lambda i, n: (i, 0)),
                  pl.BlockSpec((D_MODEL, N_BLK), lambda i, n: (0, n)),
                  pl.BlockSpec(memory_space=pl.ANY)],
        out_specs=pl.BlockSpec((tm, N_BLK), lambda i, n: (i, n)),
        out_shape=jax.ShapeDtypeStruct((T, IN_WIDTH), F32),
        compiler_params=_cparams(("parallel", "arbitrary")),
    )(xb, w, after)


def _vec_spec():
    return pl.BlockSpec((1, D_MODEL), lambda *_: (0, 0))


def _outproj_ln_call(y_lru, y_ret, y_na, x, w, g, b, after):
    T = x.shape[0]

    def body(yl_ref, yr_ref, yn_ref, x_ref, w_ref, g_ref, b_ref, after_ref, z_ref, x1b_ref, yc_ref):
        yc_ref[:, 0:LRU_WIDTH] = yl_ref[...].astype(yc_ref.dtype)
        yc_ref[:, LRU_WIDTH:LRU_WIDTH + RET_WIDTH] = yr_ref[...].astype(yc_ref.dtype)
        yc_ref[:, LRU_WIDTH + RET_WIDTH:] = yn_ref[...].astype(yc_ref.dtype)
        z = ALPHA * x_ref[...] + _mm(yc_ref[...], w_ref[...])
        z_ref[...] = z
        x1b_ref[...] = _ln_fwd(z, g_ref[...], b_ref[...]).astype(x1b_ref.dtype)

    row = lambda w_: pl.BlockSpec((TM, w_), lambda i: (i, 0))
    return pl.pallas_call(
        body, name="outproj_ln", grid=(T // TM,),
        in_specs=[row(LRU_WIDTH), row(RET_WIDTH), row(NA_WIDTH), row(D_MODEL),
                  pl.BlockSpec((D_MODEL, D_MODEL), lambda i: (0, 0)), _vec_spec(), _vec_spec(),
                  pl.BlockSpec(memory_space=pl.ANY)],
        out_specs=[row(D_MODEL)] * 3,
        out_shape=[jax.ShapeDtypeStruct((T, D_MODEL), F32),
                   jax.ShapeDtypeStruct((T, D_MODEL), _BF), jax.ShapeDtypeStruct((T, D_MODEL), _BF)],
        compiler_params=_cparams(("parallel",)),
    )(y_lru, y_ret, y_na, x, w, g, b, after)


def _ffn_ln_call(z1, x1b, wg, wu, wd, g, b, g1, b1):
    T = z1.shape[0]

    def body(z1_ref, xb_ref, wg_ref, wu_ref, wd_ref, g_ref, b_ref, g1_ref, b1_ref,
             z_ref, x2_ref, x2b_ref, gp_ref, up_ref, acc_ref):
        n = pl.program_id(1)

        @pl.when(n == 0)
        def _():
            acc_ref[...] = jnp.zeros_like(acc_ref)

        r0, r1 = _halves(TM)

        def pre(rows):
            xb = xb_ref[rows, :]
            return _mm(xb, wg_ref[...]), _mm(xb, wu_ref[...])

        def act(rows, gp, up):
            gp_ref[rows, :] = gp.astype(gp_ref.dtype)
            up_ref[rows, :] = up.astype(up_ref.dtype)
            return (gp * _sigmoid(gp) * up).astype(_BF)

        gp0, up0 = pre(r0)
        hid0 = act(r0, gp0, up0)
        gp1, up1 = pre(r1)
        acc_ref[r0, :] += _mm(hid0, wd_ref[...])
        hid1 = act(r1, gp1, up1)
        acc_ref[r1, :] += _mm(hid1, wd_ref[...])

        @pl.when(n == N_STEPS - 1)
        def _():
            z = ALPHA * _ln_fwd(z1_ref[...], g1_ref[...], b1_ref[...]) + acc_ref[...]
            z_ref[...] = z
            x2 = _ln_fwd(z, g_ref[...], b_ref[...])
            x2_ref[...] = x2
            x2b_ref[...] = x2.astype(x2b_ref.dtype)

    row = pl.BlockSpec((TM, D_MODEL), lambda i, n: (i, 0))
    return pl.pallas_call(
        body, name="ffn_ln", grid=(T // TM, N_STEPS),
        in_specs=[row, row,
                  pl.BlockSpec((D_MODEL, N_BLK), lambda i, n: (0, n)),
                  pl.BlockSpec((D_MODEL, N_BLK), lambda i, n: (0, n)),
                  pl.BlockSpec((N_BLK, D_MODEL), lambda i, n: (n, 0)), _vec_spec(), _vec_spec(), _vec_spec(), _vec_spec()],
        out_specs=[row] * 3 + [pl.BlockSpec((TM, N_BLK), lambda i, n: (i, n))] * 2,
        out_shape=[jax.ShapeDtypeStruct((T, D_MODEL), F32), jax.ShapeDtypeStruct((T, D_MODEL), F32),
                   jax.ShapeDtypeStruct((T, D_MODEL), _BF),
                   jax.ShapeDtypeStruct((T, IN_WIDTH), _BF), jax.ShapeDtypeStruct((T, IN_WIDTH), _BF)],
        scratch_shapes=[pltpu.VMEM((TM, D_MODEL), F32)],
        compiler_params=_cparams(("parallel", "arbitrary")),
    )(z1, x1b, wg, wu, wd, g, b, g1, b1)


def _loss_call(y, t):
    T = y.shape[0]

    def body(y_ref, t_ref, dy_ref, loss_ref):
        @pl.when(pl.program_id(0) == 0)
        def _():
            loss_ref[...] = jnp.zeros_like(loss_ref)

        err = y_ref[...] - t_ref[...]
        dy_ref[...] = err * (1.0 / D_MODEL)
        part = 0.5 * jnp.sum(jnp.mean(err * err, axis=-1, keepdims=True), axis=0, keepdims=True)
        loss_ref[...] += jnp.broadcast_to(part, loss_ref.shape)

    row = pl.BlockSpec((TM, D_MODEL), lambda i: (i, 0))
    return pl.pallas_call(
        body, name="loss", grid=(T // TM,),
        in_specs=[row, row],
        out_specs=[row, pl.BlockSpec((SUB, LANE), lambda i: (0, 0))],
        out_shape=[jax.ShapeDtypeStruct((T, D_MODEL), F32), jax.ShapeDtypeStruct((SUB, LANE), F32)],
        compiler_params=_cparams(("arbitrary",)),
    )(y, t)


def _ffn_bwd_call(dx2, z2, gpb, upb, wg, wu, wd, g, after):
    T = dx2.shape[0]

    def body(dx2_ref, z_ref, gp_ref, up_ref, wg_ref, wu_ref, wd_ref, g_ref, after_ref,
             dx1_ref, dgp_ref, dup_ref, hid_ref, dzb_ref, dln_ref, acc_ref):
        i, n = pl.program_id(0), pl.program_id(1)

        @pl.when((i == 0) & (n == 0))
        def _():
            dln_ref[...] = jnp.zeros_like(dln_ref)

        @pl.when(n == 0)
        def _():
            dy = dx2_ref[...]
            dz, dg_rows = _ln_bwd(dy, z_ref[...], g_ref[...])
            dzb_ref[...] = dz.astype(dzb_ref.dtype)
            acc_ref[...] = ALPHA * dz
            dln_ref[0:1, :] += jnp.sum(dg_rows, axis=0, keepdims=True)
            dln_ref[1:2, :] += jnp.sum(dy, axis=0, keepdims=True)

        r0, r1 = _halves(TM)

        def grads(rows, dhid):
            gp = gp_ref[rows, :].astype(F32)
            up = up_ref[rows, :].astype(F32)
            sg = _sigmoid(gp)
            act = gp * sg
            hid_ref[rows, :] = (act * up).astype(hid_ref.dtype)
            dup = (dhid * act).astype(_BF)
            dgp = (dhid * up * (sg * (1.0 + gp * (1.0 - sg)))).astype(_BF)
            dgp_ref[rows, :] = dgp.astype(dgp_ref.dtype)
            dup_ref[rows, :] = dup.astype(dup_ref.dtype)
            return dgp, dup

        dhid0 = _mm_nt(dzb_ref[r0, :], wd_ref[...])
        dhid1 = _mm_nt(dzb_ref[r1, :], wd_ref[...])
        dgp0, dup0 = grads(r0, dhid0)
        acc_ref[r0, :] += _mm_nt(dgp0, wg_ref[...]) + _mm_nt(dup0, wu_ref[...])
        dgp1, dup1 = grads(r1, dhid1)
        acc_ref[r1, :] += _mm_nt(dgp1, wg_ref[...]) + _mm_nt(dup1, wu_ref[...])

        @pl.when(n == N_STEPS - 1)
        def _():
            dx1_ref[...] = acc_ref[...]

    row = pl.BlockSpec((TM, D_MODEL), lambda i, n: (i, 0))
    blk = pl.BlockSpec((TM, N_BLK), lambda i, n: (i, n))
    return pl.pallas_call(
        body, name="ffn_bwd", grid=(T // TM, N_STEPS),
        in_specs=[row, row, blk, blk,
                  pl.BlockSpec((D_MODEL, N_BLK), lambda i, n: (0, n)),
                  pl.BlockSpec((D_MODEL, N_BLK), lambda i, n: (0, n)),
                  pl.BlockSpec((N_BLK, D_MODEL), lambda i, n: (n, 0)), _vec_spec(),
                  pl.BlockSpec(memory_space=pl.ANY)],
        out_specs=[row, blk, blk, blk, row, pl.BlockSpec((SUB, D_MODEL), lambda i, n: (0, 0))],
        out_shape=[jax.ShapeDtypeStruct((T, D_MODEL), F32),
                   jax.ShapeDtypeStruct((T, IN_WIDTH), _BF), jax.ShapeDtypeStruct((T, IN_WIDTH), _BF),
                   jax.ShapeDtypeStruct((T, IN_WIDTH), _BF), jax.ShapeDtypeStruct((T, D_MODEL), _BF),
                   jax.ShapeDtypeStruct((SUB, D_MODEL), F32)],
        scratch_shapes=[pltpu.VMEM((TM, D_MODEL), F32)],
        compiler_params=_cparams(("arbitrary", "arbitrary")),
    )(dx2, z2, gpb, upb, wg, wu, wd, g, after)


def _outproj_bwd_call(dx1, z1, w, g):
    T = dx1.shape[0]

    def body(dx_ref, z_ref, w_ref, g_ref, dzb_ref, dyc_ref, dres_ref, dln_ref):
        @pl.when(pl.program_id(0) == 0)
        def _():
            dln_ref[...] = jnp.zeros_like(dln_ref)

        dy = dx_ref[...]
        dz, dg_rows = _ln_bwd(dy, z_ref[...], g_ref[...])
        dzb_ref[...] = dz.astype(dzb_ref.dtype)
        dres_ref[...] = ALPHA * dz
        dyc_ref[...] = _mm_nt(dz, w_ref[...])
        dln_ref[0:1, :] += jnp.sum(dg_rows, axis=0, keepdims=True)
        dln_ref[1:2, :] += jnp.sum(dy, axis=0, keepdims=True)

    row = pl.BlockSpec((TM, D_MODEL), lambda i: (i, 0))
    return pl.pallas_call(
        body, name="outproj_bwd", grid=(T // TM,),
        in_specs=[row, row, pl.BlockSpec((D_MODEL, D_MODEL), lambda i: (0, 0)), _vec_spec()],
        out_specs=[row, row, row, pl.BlockSpec((SUB, D_MODEL), lambda i: (0, 0))],
        out_shape=[jax.ShapeDtypeStruct((T, D_MODEL), _BF), jax.ShapeDtypeStruct((T, D_MODEL), F32),
                   jax.ShapeDtypeStruct((T, D_MODEL), F32), jax.ShapeDtypeStruct((SUB, D_MODEL), F32)],
        compiler_params=_cparams(("arbitrary",)),
    )(dx1, z1, w, g)


def _inproj_bwd_call(dres, dp, w):
    T = dres.shape[0]

    def body(dres_ref, dp_ref, w_ref, dx_ref):
        dx_ref[...] = dres_ref[...] + _mm_nt(dp_ref[...], w_ref[...])

    row = pl.BlockSpec((TM, D_MODEL), lambda i: (i, 0))
    return pl.pallas_call(
        body, name="inproj_bwd", grid=(T // TM,),
        in_specs=[row, pl.BlockSpec((TM, IN_WIDTH), lambda i: (i, 0)),
                  pl.BlockSpec((D_MODEL, IN_WIDTH), lambda i: (0, 0), pipeline_mode=pl.Buffered(1))],
        out_specs=row,
        out_shape=jax.ShapeDtypeStruct((T, D_MODEL), F32),
        compiler_params=_cparams(("parallel",)),
    )(dres, dp, w)


def _tn_cols_call(a, b, name):
    T, ka = a.shape
    n = b.shape[1]

    def body(a_ref, b_ref, o_ref):
        o_ref[...] = _mm_tn(a_ref[...], b_ref[...]).astype(o_ref.dtype)

    return pl.pallas_call(
        body, name=name, grid=(n // N_BLK,),
        in_specs=[pl.BlockSpec((T, ka), lambda j: (0, 0), pipeline_mode=pl.Buffered(1)),
                  pl.BlockSpec((T, N_BLK), lambda j: (0, j))],
        out_specs=pl.BlockSpec((ka, N_BLK), lambda j: (0, j)),
        out_shape=jax.ShapeDtypeStruct((ka, n), _BF),
        compiler_params=_cparams(("parallel",)),
    )(a, b)


def _tn_rows_call(a, b, kb, name):
    T, ka = a.shape
    n = b.shape[1]

    def body(a_ref, b_ref, o_ref):
        o_ref[...] = _mm_tn(a_ref[...], b_ref[...]).astype(o_ref.dtype)

    return pl.pallas_call(
        body, name=name, grid=(ka // kb,),
        in_specs=[pl.BlockSpec((T, kb), lambda r: (0, r)),
                  pl.BlockSpec((T, n), lambda r: (0, 0), pipeline_mode=pl.Buffered(1))],
        out_specs=pl.BlockSpec((kb, n), lambda r: (r, 0)),
        out_shape=jax.ShapeDtypeStruct((ka, n), _BF),
        compiler_params=_cparams(("parallel",)),
    )(a, b)


def _me():
    return lax.axis_index("x"), lax.axis_index("y"), lax.axis_index("c")


def _flip(k):
    x, y, c = _me()
    return (1 - x if k & 4 else x, 1 - y if k & 2 else y, 1 - c if k & 1 else c)


def _dev_index(pos):
    return 4 * pos[0] + 2 * pos[1] + pos[2]


_HBM = pl.BlockSpec(memory_space=pltpu.HBM)
_SEM = pl.BlockSpec(memory_space=pltpu.SEMAPHORE)


def _land_shape(shape, mode):
    if mode == "all":
        return (N_DEV,) + shape
    if mode == "cols":
        return (shape[0], N_DEV * shape[1])
    if mode == "blk":
        return shape
    assert mode == "scols"
    return (N_DEV, shape[0], shape[1] // N_DEV)


def _comm_copies(ins, lands, modes, send_sems, recv_sems):
    me = _dev_index(_me())
    copies = []
    for k in range(N_DEV):
        peer = _flip(k)
        pidx = _dev_index(peer)
        for a, (src, land, mode) in enumerate(zip(ins, lands, modes)):
            if mode == "blk":
                src = src.at[pidx]
            elif mode == "scols":
                w = src.shape[1] // N_DEV
                src = src.at[:, pl.ds(pl.multiple_of(pidx * w, LANE), w)]
            if mode == "cols":
                w = src.shape[1]
                dst = land.at[:, pl.ds(pl.multiple_of(me * w, LANE), w)]
            else:
                dst = land.at[me]
            copies.append(pltpu.make_async_remote_copy(
                src_ref=src, dst_ref=dst, send_sem=send_sems.at[k * len(ins) + a], recv_sem=recv_sems.at[k * len(ins) + a],
                device_id=peer, device_id_type=MESH))
    return copies


def _comm_start_call(arrs, gather_flags, after, name):
    n = len(arrs)
    lands = [lax.empty(_land_shape(v.shape, mode), v.dtype) for v, mode in zip(arrs, gather_flags)]

    def body(*refs):
        ins, lnd = refs[:n], refs[n:2 * n]
        send_sems, recv_sems = refs[2 * n + len(after)], refs[2 * n + len(after) + 1]
        for cp in _comm_copies(ins, lnd, gather_flags, send_sems, recv_sems):
            cp.start()
        refs[-1][...] = jnp.zeros_like(refs[-1])

    hbm = [pltpu.with_memory_space_constraint(v, pltpu.HBM) for v in list(arrs) + lands]
    out = pl.pallas_call(
        body, name=name,
        out_shape=(pltpu.SemaphoreType.DMA((N_DEV * n,)), pltpu.SemaphoreType.DMA((N_DEV * n,)),
                   *[pltpu.HBM(v.shape, v.dtype) for v in hbm], jax.ShapeDtypeStruct((SUB, LANE), F32)),
        in_specs=[_HBM] * (2 * n) + [pl.BlockSpec(memory_space=pl.ANY)] * len(after),
        out_specs=(_SEM, _SEM, *[_HBM] * (2 * n), pl.BlockSpec(memory_space=pltpu.VMEM)),
        input_output_aliases={i: 2 + i for i in range(2 * n)},
        compiler_params=pltpu.CompilerParams(has_side_effects=pltpu.SideEffectType.DATAFLOW_SIDE_EFFECTING),
    )(*hbm, *after)
    return out[:-1], out[-1]


def _comm_wait_call(state, gather_flags, after, name):
    n = len(gather_flags)
    send_sems, recv_sems, thru = state[0], state[1], state[2:]

    def body(*refs):
        ins, lnd, ssem, rsem = refs[:n], refs[n:2 * n], refs[2 * n], refs[2 * n + 1]
        for cp in _comm_copies(ins, lnd, gather_flags, ssem, rsem):
            cp.wait_send()
            cp.wait_recv()

    out = pl.pallas_call(
        body, name=name,
        out_shape=tuple(pltpu.HBM(v.shape, v.dtype) for v in thru),
        in_specs=[_HBM] * (2 * n) + [_SEM, _SEM] + [pl.BlockSpec(memory_space=pl.ANY)] * len(after),
        out_specs=tuple([_HBM] * (2 * n)),
        input_output_aliases={i: i for i in range(2 * n)},
        compiler_params=pltpu.CompilerParams(has_side_effects=pltpu.SideEffectType.DATAFLOW_SIDE_EFFECTING),
    )(*thru, send_sems, recv_sems, *after)
    return out[n:]


def _sum8_call(recv, stacked, layer, nl, rows, r_out, c_out, name, transposed=False):
    c = recv.shape[2]

    def body(x_ref, *rest):
        o_ref = rest[-1]
        if transposed:
            acc = x_ref[0].astype(F32)
            for s in range(1, N_DEV):
                acc = acc + x_ref[s].astype(F32)
            o_ref[...] = acc.T[:c_out, :]
        else:
            acc = x_ref[0, :, :c_out].astype(F32)
            for s in range(1, N_DEV):
                acc = acc + x_ref[s, :, :c_out].astype(F32)
            o_ref[...] = acc

    prev = [] if stacked is None else [stacked]
    if transposed:
        out_spec = pl.BlockSpec((None, c_out, rows), lambda i: (layer, 0, i))
        out_shape = jax.ShapeDtypeStruct((nl, c_out, r_out), F32)
    else:
        out_spec = pl.BlockSpec((None, rows, c_out), lambda i: (layer, i, 0))
        out_shape = jax.ShapeDtypeStruct((nl, r_out, c_out), F32)
    return pl.pallas_call(
        body, name=name, grid=(r_out // rows,),
        in_specs=[pl.BlockSpec((N_DEV, rows, c), lambda i: (0, i, 0))] + [pl.BlockSpec(memory_space=pl.ANY)] * len(prev),
        out_specs=out_spec, out_shape=out_shape,
        input_output_aliases={1: 0} if prev else {},
        compiler_params=_cparams(("parallel",)),
    )(recv, *prev)


def _adamw_call(w, g, m, v, rows, name):
    r, c = w.shape

    def body(w_ref, g_ref, m_ref, v_ref, d_ref, nm_ref, nv_ref):
        gr = g_ref[...]
        nm = ADAM_B1 * m_ref[...] + (1.0 - ADAM_B1) * gr
        nv = ADAM_B2 * v_ref[...] + (1.0 - ADAM_B2) * (gr * gr)
        m_hat = nm / (1.0 - ADAM_B1 ** ADAM_STEP)
        v_hat = nv / (1.0 - ADAM_B2 ** ADAM_STEP)
        d_ref[...] = -ADAM_LR * (m_hat / (jnp.sqrt(v_hat) + ADAM_EPS) + ADAM_WD * w_ref[...])
        nm_ref[...] = nm
        nv_ref[...] = nv

    spec = pl.BlockSpec((rows, c), lambda i: (i, 0))
    return pl.pallas_call(
        body, name=name, grid=(r // rows,),
        in_specs=[spec] * 4, out_specs=[spec] * 3,
        out_shape=[jax.ShapeDtypeStruct((r, c), F32)] * 3,
        compiler_params=_cparams(("parallel",)),
    )(w, g, m, v)


SH_ROWS = 16
SH_W = LRU_WIDTH // N_DEV
REP_ROWS = 824
_REP_SIZES = (LRU_WIDTH, 2 * 6 * 64 * 64, 2 * 6 * 64 * 64, RET_WIDTH, 1920, D_MODEL, D_MODEL, D_MODEL, D_MODEL)
_RPB_SIZE = NA_HEADS * (2 * NA_KH - 1) * (2 * NA_KW - 1)


def _pack_sh(cw, ba, bx, lam):
    return jnp.concatenate([cw, ba, bx, lam], axis=0)


def _pad_sh(p):
    pad = [(0, 0)] * (p.ndim - 2) + [(0, SH_ROWS - p.shape[-2]), (0, LANE - p.shape[-1])]
    return jnp.pad(p, pad)


def _pack_rep(cb, wa, wx, gnw, rpb, l1g, l1b, l2g, l2b):
    flat = jnp.concatenate([cb.reshape(-1), wa.reshape(-1), wx.reshape(-1), gnw.reshape(-1),
                            jnp.pad(rpb.reshape(-1), (0, 1920 - _RPB_SIZE)), l1g, l1b, l2g, l2b,
                            jnp.zeros((REP_ROWS * LANE - sum(_REP_SIZES),), F32)])
    return flat.reshape(REP_ROWS, LANE)


def _unpack_rep(p):
    nl = p.shape[0]
    flat = p.reshape(nl, -1)
    out, off = [], 0
    for size in _REP_SIZES:
        out.append(flat[:, off:off + size])
        off += size
    cb, wa, wx, gnw, rpb, l1g, l1b, l2g, l2b = out
    return (cb, wa.reshape(nl, 2, 6, 64, 64), wx.reshape(nl, 2, 6, 64, 64), gnw,
            rpb[:, :_RPB_SIZE].reshape(nl, NA_HEADS, 2 * NA_KH - 1, 2 * NA_KW - 1), l1g, l1b, l2g, l2b)


def _adamw_nd(w, g, m, v, rows, name):
    shp = w.shape
    f = lambda t: t.reshape(-1, shp[-1])
    rows = f(w).shape[0] if rows is None else rows
    return [t.reshape(shp) for t in _adamw_call(f(w), f(g), f(m), f(v), rows, name)]


def kernel(x, w_in, conv_w, conv_b, lru_w_a, lru_b_a, lru_w_x, lru_b_x, lru_lam, ret_gn_w, na_rpb, w_out, ln1_g, ln1_b, w_gate, w_up, w_down, ln2_g, ln2_b, loss_target, m_w_in, m_conv_w, m_conv_b, m_lru_w_a, m_lru_b_a, m_lru_w_x, m_lru_b_x, m_lru_lam, m_ret_gn_w, m_na_rpb, m_w_out, m_ln1_g, m_ln1_b, m_w_gate, m_w_up, m_w_down, m_ln2_g, m_ln2_b, v_w_in, v_conv_w, v_conv_b, v_lru_w_a, v_lru_b_a, v_lru_w_x, v_lru_b_x, v_lru_lam, v_ret_gn_w, v_na_rpb, v_w_out, v_ln1_g, v_ln1_b, v_w_gate, v_w_up, v_w_down, v_ln2_g, v_ln2_b):
    nl = w_in.shape[0]
    T = x.shape[1]
    rows_n = T // GRID_W
    x0, target = x[0], loss_target[0]
    ffpad = W_BLK - FF_BLK

    win_b = w_in.astype(_BF)
    wg_b = jnp.pad(w_gate, ((0, 0), (0, 0), (0, ffpad))).astype(_BF)
    wu_b = jnp.pad(w_up, ((0, 0), (0, 0), (0, ffpad))).astype(_BF)
    wd_b = jnp.pad(w_down, ((0, 0), (0, ffpad), (0, 0))).astype(_BF)
    wout_b = w_out.astype(_BF)
    def agf_start(l, after):
        sh = _pad_sh(_pack_sh(conv_w[l], lru_b_a[l], lru_b_x[l], lru_lam[l]))
        arrs, modes = [win_b[l], sh], ["cols", "all"]
        if l > 0:
            arrs, modes = arrs + [wd_b[l]], modes + ["all"]
        return _comm_start_call(arrs, modes, after, f"agf_start{l}"), modes

    def agk_start(l, after):
        arrs, modes = [wg_b[l], wu_b[l], wout_b[l]], ["cols", "cols", "all"]
        if l == 0:
            arrs, modes = arrs + [wd_b[l]], modes + ["all"]
        return _comm_start_call(arrs, modes, after, f"agk_start{l}"), modes

    tables = _ret_tables(T)
    w4_all = _lru_w4(lru_w_a, lru_w_x)
    layers = []
    gathered = []
    xs, xb = x0, x0.astype(_BF)
    (agf_state, token), agf_modes = agf_start(0, [])
    tie = 0.0 * token[0, 0]
    btabs = [_na_bias_tables(na_rpb[l] + tie, rows_n) for l in range(nl)]
    for l in range(nl):
        front = _comm_wait_call(agf_state, agf_modes, [xb] + (btabs if l == 0 else []), f"agf_wait{l}")
        win, shg = front[0], front[1]
        (agk_state, token), agk_modes = agk_start(l, [shg])
        full = shg[:, :10, :SH_W].transpose(1, 0, 2).reshape(10, LRU_WIDTH)
        vec, w4 = _lru_vec(full[0:4], conv_b[l], full[4:6], full[6:8], full[8:10]), w4_all[l]
        gnw8 = jnp.pad(ret_gn_w[l][None], ((0, SUB - 1), (0, 0)))
        btab = btabs[l]
        proj = _inproj_call(xb, win, token)
        y_lru = _lru_fwd_call(proj, vec, w4)
        y_ret = _ret_fwd_call(proj, tables, gnw8)
        y_na = _na_fwd_call(proj, btab)
        back = _comm_wait_call(agk_state, agk_modes, [y_na], f"agk_wait{l}")
        wg, wu, wout = back[0], back[1], back[2]
        wd = (back[3] if l == 0 else front[2]).reshape(IN_WIDTH, D_MODEL)
        wout = wout.reshape(D_MODEL, D_MODEL)
        gathered.append((win, wg, wu, wd, wout))
        if l + 1 < nl:
            (agf_state, token), agf_modes = agf_start(l + 1, [wout])
        z1, x1b, ycb = _outproj_ln_call(y_lru, y_ret, y_na, xs, wout, ln1_g[l][None], ln1_b[l][None], token)
        z2, x2, x2b, gpb, upb = _ffn_ln_call(z1, x1b, wg, wu, wd, ln2_g[l][None], ln2_b[l][None],
                                             ln1_g[l][None], ln1_b[l][None])
        layers.append(dict(xb=xb, proj=proj, vec=vec, w4=w4, gnw8=gnw8, btab=btab,
                           z1=z1, x1b=x1b, ycb=ycb, z2=z2, gpb=gpb, upb=upb))
        xs, xb = x2, x2b

    dx, loss_blk = _loss_call(xs, target)
    loss = lax.psum(loss_blk[0, 0], ("x", "y", "c"))

    gxa_flags = ["scols", "scols", "blk", "blk"]
    gxb_flags = ["scols", "blk", "all"]
    gxa_state, gxb_state = [None] * nl, [None] * nl
    token = loss_blk
    for l in reversed(range(nl)):
        s = layers[l]
        win, wg, wu, wd, wout = gathered[l]
        dx1, dgp, dup, hid, dz2b, dln2 = _ffn_bwd_call(dx, s["z2"], s["gpb"], s["upb"], wg, wu, wd, ln2_g[l][None], token)
        dwg = _tn_cols_call(s["x1b"], dgp, "tn_cols")
        dwu = _tn_cols_call(s["x1b"], dup, "tn_cols")
        dwd = _tn_rows_call(hid, dz2b, N_BLK, "tn_rows_down").reshape(N_DEV, W_BLK, D_MODEL)
        dz1b, dyc, dres, dln1 = _outproj_bwd_call(dx1, s["z1"], wout, ln1_g[l][None])
        dwout = _tn_rows_call(s["ycb"], dz1b, D_MODEL // 2, "tn_rows_out").reshape(N_DEV, LANE, D_MODEL)
        gxa_state[l], token = _comm_start_call([dwg, dwu, dwd, dwout], gxa_flags, [], f"gxa_start{l}")
        dp, dvec, dw4 = _lru_bwd_call(s["proj"], dyc, s["vec"], s["w4"], token)
        dp, dgnw = _ret_bwd_call(s["proj"], dyc, tables, s["gnw8"], dp)
        dp, dbias = _na_bwd_call(s["proj"], dyc, s["btab"], dp)
        dwin = _tn_cols_call(s["xb"], dp, "tn_cols")
        dx = _inproj_bwd_call(dres, dp, win)
        dcw, dcb, dwa, dba, dwx, dbx, dlam = _lru_unpack(dvec, dw4)
        rep = _pack_rep(dcb, dwa, dwx, dgnw[0], _na_bias_grad(dbias, rows_n), dln1[0], dln1[1], dln2[0], dln2[1])
        sh = _pack_sh(dcw, dba, dbx, dlam).reshape(10, N_DEV, SH_W).transpose(1, 0, 2)
        gxb_state[l], token = _comm_start_call([dwin, _pad_sh(sh), rep], gxb_flags, [], f"gxb_start{l}")

    g_w_in = g_w_gate = g_w_up = g_w_down = g_w_out = g_shp = g_repp = None
    after = [dx, token]
    big = {}
    for l in reversed(range(nl)):
        ra = _comm_wait_call(gxa_state[l], gxa_flags, after, f"gxa_wait{l}")
        g_w_gate = _sum8_call(ra[0], g_w_gate, l, nl, TM, D_MODEL, FF_BLK, "sum8_ff", transposed=True)
        g_w_up = _sum8_call(ra[1], g_w_up, l, nl, TM, D_MODEL, FF_BLK, "sum8_ff", transposed=True)
        g_w_down = _sum8_call(ra[2], g_w_down, l, nl, FF_BLK, FF_BLK, D_MODEL, "sum8_down")
        g_w_out = _sum8_call(ra[3], g_w_out, l, nl, LANE, LANE, D_MODEL, "sum8_out")
        after = [g_w_out]
        if l == 0:
            tr = lambda t: jnp.swapaxes(t, 1, 2)
            big["w_gate"] = [tr(t) for t in _adamw_nd(tr(w_gate), g_w_gate, tr(m_w_gate), tr(v_w_gate), FF_BLK, "adamw_down")]
            big["w_up"] = [tr(t) for t in _adamw_nd(tr(w_up), g_w_up, tr(m_w_up), tr(v_w_up), FF_BLK, "adamw_down")]
            g_w_gate, g_w_up = tr(g_w_gate), tr(g_w_up)
            big["w_down"] = _adamw_nd(w_down, g_w_down, m_w_down, v_w_down, FF_BLK, "adamw_down")
            big["w_out"] = _adamw_nd(w_out, g_w_out, m_w_out, v_w_out, LANE, "adamw_out")
            after = [big[n][k] for n in ("w_gate", "w_up", "w_down", "w_out") for k in range(3)]
        rb = _comm_wait_call(gxb_state[l], gxb_flags, after, f"gxb_wait{l}")
        g_w_in = _sum8_call(rb[0], g_w_in, l, nl, TM, D_MODEL, W_BLK, "sum8_in")
        g_shp = _sum8_call(rb[1], g_shp, l, nl, SH_ROWS, SH_ROWS, LANE, "sum8_sh")
        g_repp = _sum8_call(rb[2], g_repp, l, nl, REP_ROWS, REP_ROWS, LANE, "sum8_rep")
        after = [g_repp]

    big["w_in"] = _adamw_nd(w_in, g_w_in, m_w_in, v_w_in, TM, "adamw_in")
    g_shp = g_shp[:, :, :SH_W]
    rep_names = ("conv_b", "lru_w_a", "lru_w_x", "ret_gn_w", "na_rpb", "ln1_g", "ln1_b", "ln2_g", "ln2_b")
    grads = {"w_in": g_w_in, "w_gate": g_w_gate, "w_up": g_w_up, "w_down": g_w_down, "w_out": g_w_out,
             "conv_w": g_shp[:, 0:4], "lru_b_a": g_shp[:, 4:6], "lru_b_x": g_shp[:, 6:8], "lru_lam": g_shp[:, 8:10]}
    grads.update(dict(zip(rep_names, _unpack_rep(g_repp))))
    small = {
        "conv_w": (conv_w, m_conv_w, v_conv_w), "conv_b": (conv_b, m_conv_b, v_conv_b),
        "lru_w_a": (lru_w_a, m_lru_w_a, v_lru_w_a), "lru_b_a": (lru_b_a, m_lru_b_a, v_lru_b_a),
        "lru_w_x": (lru_w_x, m_lru_w_x, v_lru_w_x), "lru_b_x": (lru_b_x, m_lru_b_x, v_lru_b_x),
        "lru_lam": (lru_lam, m_lru_lam, v_lru_lam), "ret_gn_w": (ret_gn_w, m_ret_gn_w, v_ret_gn_w),
        "na_rpb": (na_rpb, m_na_rpb, v_na_rpb), "ln1_g": (ln1_g, m_ln1_g, v_ln1_g), "ln1_b": (ln1_b, m_ln1_b, v_ln1_b),
        "ln2_g": (ln2_g, m_ln2_g, v_ln2_g), "ln2_b": (ln2_b, m_ln2_b, v_ln2_b),
    }
    for name, (w_, m_, v_) in small.items():
        big[name] = _adamw_nd(w_, grads[name], m_, v_, None, "adamw_small")
    kinds = [{n: big[n][k] for n in big} for k in range(3)]
    order = ("w_in", "conv_w", "conv_b", "lru_w_a", "lru_b_a", "lru_w_x", "lru_b_x", "lru_lam", "ret_gn_w", "na_rpb",
             "w_out", "ln1_g", "ln1_b", "w_gate", "w_up", "w_down", "ln2_g", "ln2_b")
    outs = [loss, dx[None]]
    for d in (grads, *kinds):
        outs.extend(d[n] for n in order)
    return tuple(outs)
```

```python
import functools
import math

import numpy as np
import jax
import jax.numpy as jnp
from jax import lax
from jax.experimental import pallas as pl
from jax.experimental.pallas import tpu as pltpu

F32 = jnp.float32
_BF = jnp.bfloat16

D_MODEL = 1024
DEPTH = 4
GRID_W = 64
HEAD_DIM = 64
LRU_WIDTH = 384
RET_WIDTH = 384
RET_HEADS = 6
NA_WIDTH = 256
NA_HEADS = 4
IN_WIDTH = 3072
CONV_WIDTH = 4
LRU_C = 8.0
RET_CHUNK = 128
ROPE_BASE = 10000.0
GN_EPS = 1e-6
NA_KH = 8
NA_KW = 16
D_FF = 2816
FF_BLK = 352
N_DEV = 8
ALPHA = (2 * DEPTH) ** 0.25
LN_EPS = 1e-5
ADAM_LR = 0.001
ADAM_B1 = 0.9
ADAM_B2 = 0.999
ADAM_EPS = 1e-08
ADAM_WD = 0.01
ADAM_STEP = 10

LANE = 128
SUB = 8
VMEM_MB = 56
NEG = -1e30

MESH = pl.DeviceIdType.MESH


def _cparams(sem=None, vmem_mb=VMEM_MB):
    return pltpu.CompilerParams(dimension_semantics=sem, vmem_limit_bytes=vmem_mb << 20)


def _mm(a, b):
    return jnp.dot(a.astype(_BF), b.astype(_BF), preferred_element_type=F32)


def _mm_nt(a, b):
    return lax.dot_general(a.astype(_BF), b.astype(_BF), (((1,), (1,)), ((), ())), preferred_element_type=F32)


def _mm_tn(a, b):
    return lax.dot_general(a.astype(_BF), b.astype(_BF), (((0,), (0,)), ((), ())), preferred_element_type=F32)


def _sigmoid(x):
    return jax.nn.sigmoid(x)


def _rows(start, size):
    return pl.ds(pl.multiple_of(start, SUB), size)


def _loop2(n, body, init):
    assert n % 2 == 0
    return lax.fori_loop(0, n // 2, lambda i, c: body(2 * i + 1, body(2 * i, c)), init)


def _strip(T, col, buffers=2):
    return pl.BlockSpec((T, LANE), lambda j: (0, col(j)), pipeline_mode=pl.Buffered(buffers))


LRU_CH = 1024
_GELU_C0 = math.sqrt(2.0 / math.pi)
_GELU_C1 = 0.044715


def _gelu_parts(x):
    x2 = x * x
    t = jnp.tanh(_GELU_C0 * (x + _GELU_C1 * x * x2))
    val = 0.5 * x * (1.0 + t)
    der = 0.5 * (1.0 + t) + 0.5 * x * (1.0 - t * t) * _GELU_C0 * (1.0 + 3.0 * _GELU_C1 * x2)
    return val, der


def _softplus_neg(lam):
    e = jnp.exp(-jnp.abs(lam))
    w = 1.0 + e
    l1p = jnp.where(w == 1.0, e, jnp.log(w) * (e / jnp.where(w == 1.0, 1.0, w - 1.0)))
    return jnp.maximum(-lam, 0.0) + l1p


def _window(ref, t0, ch, T):
    prev = ref[_rows(jnp.maximum(t0 - SUB, 0), SUB), :].astype(F32)
    nxt = ref[_rows(jnp.minimum(t0 + ch, T - SUB), SUB), :].astype(F32)
    prev = jnp.where(t0 > 0, prev, 0.0)
    nxt = jnp.where(t0 + ch < T, nxt, 0.0)
    return jnp.concatenate([prev, ref[_rows(t0, ch), :].astype(F32), nxt], axis=0)


def _tap(win, shift, ch):
    n = win.shape[0]
    return pltpu.roll(win, (-shift) % n, 0)[SUB:SUB + ch]


def _lru_conv(xb_ref, vec, t0, T):
    win = _window(xb_ref, t0, LRU_CH, T)
    xc = jnp.broadcast_to(vec[4:5, :], (LRU_CH, LANE))
    for j in range(CONV_WIDTH):
        xc = xc + _tap(win, j - CONV_WIDTH // 2, LRU_CH) * vec[j:j + 1, :]
    return xc


def _lru_dir(pre_a, pre_x, sp):
    r = _sigmoid(pre_a)
    i = _sigmoid(pre_x)
    log_a = (-LRU_C) * r * sp
    a = jnp.exp(log_a)
    z = jnp.tanh(-log_a) * (a * a + 1.0)
    s = jnp.sqrt(z)
    return r, i, a, s


def _scan_tile(a, b, reverse, row):
    for k in (1, 2, 4):
        if not reverse:
            a_s, b_s, m = pltpu.roll(a, k, 0), pltpu.roll(b, k, 0), row >= k
        else:
            a_s, b_s, m = pltpu.roll(a, SUB - k, 0), pltpu.roll(b, SUB - k, 0), row < SUB - k
        b = jnp.where(m, a * b_s + b, b)
        a = jnp.where(m, a * a_s, a)
    return a, b


def _bcast_row(x, r):
    return jnp.broadcast_to(x[r:r + 1, :], (SUB, LANE))


def _lru_prepare(xb_ref, w4_ref, vec, xc_ref, af_ref, uf_ref, ab_ref, ub_ref, T):
    sp_f = _softplus_neg(vec[9:10, :])
    sp_b = _softplus_neg(vec[10:11, :])
    w4 = w4_ref[0]

    def body(c, carry):
        t0 = c * LRU_CH
        xc = _lru_conv(xb_ref, vec, t0, T)
        if xc_ref is not None:
            xc_ref[_rows(t0, LRU_CH), :] = xc
        pre = _mm(xc, w4)
        _, i, a, s = _lru_dir(pre[:, 0:128] + vec[5:6, :], pre[:, 128:256] + vec[6:7, :], sp_f)
        af_ref[_rows(t0, LRU_CH), :] = a
        uf_ref[_rows(t0, LRU_CH), :] = s * (i * xc)
        _, i, a, s = _lru_dir(pre[:, 256:384] + vec[7:8, :], pre[:, 384:512] + vec[8:9, :], sp_b)
        ab_ref[_rows(t0, LRU_CH), :] = a
        ub_ref[_rows(t0, LRU_CH), :] = s * (i * xc)
        return carry

    lax.fori_loop(0, T // LRU_CH, body, 0)


def _lru_scan(af_ref, uf_ref, ab_ref, ub_ref, T):
    nt = T // SUB
    row = lax.broadcasted_iota(jnp.int32, (SUB, LANE), 0)

    def body(j, carry):
        hf, hb = carry
        sf = _rows(j * SUB, SUB)
        sb = _rows((nt - 1 - j) * SUB, SUB)
        a, b = _scan_tile(af_ref[sf, :], uf_ref[sf, :], False, row)
        h = a * hf + b
        uf_ref[sf, :] = h
        hf = _bcast_row(h, SUB - 1)
        a, b = _scan_tile(ab_ref[sb, :], ub_ref[sb, :], True, row)
        h = a * hb + b
        ub_ref[sb, :] = h
        hb = _bcast_row(h, 0)
        return hf, hb

    z = jnp.zeros((SUB, LANE), F32)
    lax.fori_loop(0, nt, body, (z, z))


def _lru_fwd_call(proj, vec, w4):
    T = proj.shape[0]

    def body(xb_ref, gate_ref, vec_ref, w4_ref, y_ref, af_ref, uf_ref, ab_ref, ub_ref):
        vec = vec_ref[...]
        _lru_prepare(xb_ref, w4_ref, vec, None, af_ref, uf_ref, ab_ref, ub_ref, T)
        _lru_scan(af_ref, uf_ref, ab_ref, ub_ref, T)

        def out(c, carry):
            rows = _rows(c * LRU_CH, LRU_CH)
            gl, _ = _gelu_parts(gate_ref[rows, :])
            y_ref[rows, :] = ((uf_ref[rows, :] + ub_ref[rows, :]) * gl).astype(y_ref.dtype)
            return carry

        lax.fori_loop(0, T // LRU_CH, out, 0)

    return pl.pallas_call(
        body, name="lru_fwd", grid=(LRU_WIDTH // LANE,),
        in_specs=[_strip(T, lambda j: j), _strip(T, lambda j: j + 3),
                  pl.BlockSpec((16, LANE), lambda j: (0, j)),
                  pl.BlockSpec((1, LANE, 4 * LANE), lambda j: (j, 0, 0))],
        out_specs=_strip(T, lambda j: j, buffers=1),
        out_shape=jax.ShapeDtypeStruct((T, LRU_WIDTH), _BF),
        scratch_shapes=[pltpu.VMEM((T, LANE), F32)] * 4,
        compiler_params=_cparams(("arbitrary",)),
    )(proj, proj, vec, w4)


def _store_strips(stage_ref, dp_ref, cols, sems):
    copies = [pltpu.make_async_copy(stage_ref.at[b], dp_ref.at[:, pl.ds(pl.multiple_of(c * LANE, LANE), LANE)], sems.at[b])
              for b, c in enumerate(cols)]
    for cp in copies:
        cp.start()
    for cp in copies:
        cp.wait()


def _lru_bwd_call(proj, dycat, vec, w4, after):
    T = proj.shape[0]
    nt = T // SUB
    nch = T // LRU_CH

    def body(xb_ref, gate_ref, dy_ref, vec_ref, w4_ref, after_ref, dp_ref, dvec_ref, dw4_ref,
             xc_ref, af_ref, hf_ref, ab_ref, hb_ref, dh_ref, stage_ref, sems):
        dxb_ref, dgate_ref = stage_ref.at[0], stage_ref.at[1]
        vec = vec_ref[...]
        _lru_prepare(xb_ref, w4_ref, vec, xc_ref, af_ref, hf_ref, ab_ref, hb_ref, T)
        _lru_scan(af_ref, hf_ref, ab_ref, hb_ref, T)

        def gate_bwd(c, carry):
            rows = _rows(c * LRU_CH, LRU_CH)
            gl, dgl = _gelu_parts(gate_ref[rows, :])
            dy = dy_ref[rows, :]
            dgate_ref[rows, :] = (dy * (hf_ref[rows, :] + hb_ref[rows, :]) * dgl).astype(dgate_ref.dtype)
            dh_ref[rows, :] = dy * gl
            return carry

        lax.fori_loop(0, nch, gate_bwd, 0)

        row = lax.broadcasted_iota(jnp.int32, (SUB, LANE), 0)

        def adj(j, carry):
            gf, a_next, gb, a_prev = carry
            tf = nt - 1 - j
            sf = _rows(tf * SUB, SUB)
            a_t = af_ref[sf, :]
            h_t = hf_ref[sf, :]
            coef = jnp.where(row == SUB - 1, a_next, pltpu.roll(a_t, SUB - 1, 0))
            ac, bc = _scan_tile(coef, dh_ref[sf, :], True, row)
            g = ac * gf + bc
            h_prev = hf_ref[_rows(jnp.maximum(tf - 1, 0) * SUB, SUB), :]
            h_prev = jnp.where(tf > 0, _bcast_row(h_prev, SUB - 1), 0.0)
            hs = jnp.where(row == 0, h_prev, pltpu.roll(h_t, 1, 0))
            af_ref[sf, :] = g * hs
            hf_ref[sf, :] = g
            gf = _bcast_row(g, 0)
            a_next = _bcast_row(a_t, 0)
            sb = _rows(j * SUB, SUB)
            a_t = ab_ref[sb, :]
            h_t = hb_ref[sb, :]
            coef = jnp.where(row == 0, a_prev, pltpu.roll(a_t, 1, 0))
            ac, bc = _scan_tile(coef, dh_ref[sb, :], False, row)
            g = ac * gb + bc
            h_next = hb_ref[_rows(jnp.minimum(j + 1, nt - 1) * SUB, SUB), :]
            h_next = jnp.where(j < nt - 1, _bcast_row(h_next, 0), 0.0)
            hs = jnp.where(row == SUB - 1, h_next, pltpu.roll(h_t, SUB - 1, 0))
            ab_ref[sb, :] = g * hs
            hb_ref[sb, :] = g
            gb = _bcast_row(g, SUB - 1)
            a_prev = _bcast_row(a_t, SUB - 1)
            return gf, a_next, gb, a_prev

        z = jnp.zeros((SUB, LANE), F32)
        lax.fori_loop(0, nt, adj, (z, z, z, z))

        sp_f = _softplus_neg(vec[9:10, :])
        sp_b = _softplus_neg(vec[10:11, :])
        w4 = w4_ref[0]
        dw4_ref[...] = jnp.zeros_like(dw4_ref)

        def one_dir(pre_a, pre_x, sp, xc, du, da):
            r, i, a, s = _lru_dir(pre_a, pre_x, sp)
            d_i = du * s * xc
            dxc = du * s * i
            d_s = du * i * xc
            d_log = da * a - d_s * (a * a) / s
            d_r = d_log * (-LRU_C) * sp
            d_sp = jnp.sum(d_log * (-LRU_C) * r, axis=0, keepdims=True)
            return d_r * r * (1.0 - r), d_i * i * (1.0 - i), dxc, d_sp

        def gates_bwd(c, carry):
            db, dspf, dspb = carry
            rows = _rows(c * LRU_CH, LRU_CH)
            xc = xc_ref[rows, :]
            pre = _mm(xc, w4)
            dpa_f, dpx_f, dxc_f, d_sp_f = one_dir(pre[:, 0:128] + vec[5:6, :], pre[:, 128:256] + vec[6:7, :],
                                                  sp_f, xc, hf_ref[rows, :], af_ref[rows, :])
            dpa_b, dpx_b, dxc_b, d_sp_b = one_dir(pre[:, 256:384] + vec[7:8, :], pre[:, 384:512] + vec[8:9, :],
                                                  sp_b, xc, hb_ref[rows, :], ab_ref[rows, :])
            dpre = jnp.concatenate([dpa_f, dpx_f, dpa_b, dpx_b], axis=1)
            dw4_ref[0] += _mm_tn(xc, dpre)
            dh_ref[rows, :] = dxc_f + dxc_b + _mm_nt(dpre, w4)
            return db + jnp.sum(dpre, axis=0, keepdims=True), dspf + d_sp_f, dspb + d_sp_b

        z1 = jnp.zeros((1, LANE), F32)
        db, dspf, dspb = lax.fori_loop(0, nch, gates_bwd, (jnp.zeros((1, 4 * LANE), F32), z1, z1))

        def conv_bwd(c, carry):
            t0 = c * LRU_CH
            rows = _rows(t0, LRU_CH)
            dwin = _window(dh_ref, t0, LRU_CH, T)
            xwin = _window(xb_ref, t0, LRU_CH, T)
            dxc = dh_ref[rows, :]
            dxb = jnp.zeros((LRU_CH, LANE), F32)
            out = []
            for j in range(CONV_WIDTH):
                off = j - CONV_WIDTH // 2
                dxb = dxb + _tap(dwin, -off, LRU_CH) * vec[j:j + 1, :]
                out.append(carry[j] + jnp.sum(dxc * _tap(xwin, off, LRU_CH), axis=0, keepdims=True))
            dxb_ref[rows, :] = dxb.astype(dxb_ref.dtype)
            out.append(carry[CONV_WIDTH] + jnp.sum(dxc, axis=0, keepdims=True))
            return tuple(out)

        dconv = lax.fori_loop(0, nch, conv_bwd, (z1,) * (CONV_WIDTH + 1))
        dlam_f = dspf * (-_sigmoid(-vec[9:10, :]))
        dlam_b = dspb * (-_sigmoid(-vec[10:11, :]))
        dvec_ref[...] = jnp.concatenate(
            list(dconv) + [db[:, 0:128], db[:, 128:256], db[:, 256:384], db[:, 384:512], dlam_f, dlam_b,
                           jnp.zeros((5, LANE), F32)], axis=0)
        j = pl.program_id(0)
        _store_strips(stage_ref, dp_ref, (j, j + 3), sems)

    ns = LRU_WIDTH // LANE
    return pl.pallas_call(
        body, name="lru_bwd", grid=(ns,),
        in_specs=[_strip(T, lambda j: j), _strip(T, lambda j: j + 3), _strip(T, lambda j: j),
                  pl.BlockSpec((16, LANE), lambda j: (0, j)),
                  pl.BlockSpec((1, LANE, 4 * LANE), lambda j: (j, 0, 0)),
                  pl.BlockSpec(memory_space=pl.ANY)],
        out_specs=[pl.BlockSpec(memory_space=pl.ANY),
                   pl.BlockSpec((16, LANE), lambda j: (0, j)),
                   pl.BlockSpec((1, LANE, 4 * LANE), lambda j: (j, 0, 0))],
        out_shape=[jax.ShapeDtypeStruct((T, IN_WIDTH), _BF),
                   jax.ShapeDtypeStruct((16, LRU_WIDTH), F32), jax.ShapeDtypeStruct((ns, LANE, 4 * LANE), F32)],
        scratch_shapes=[pltpu.VMEM((T, LANE), F32)] * 6 + [pltpu.VMEM((2, T, LANE), _BF), pltpu.SemaphoreType.DMA((2,))],
        compiler_params=_cparams(("arbitrary",)),
    )(proj, proj, dycat, vec, w4, after)


def _lru_vec(cw, cb, ba, bx, lam):
    return jnp.concatenate([cw, cb[None], ba[0:1], bx[0:1], ba[1:2], bx[1:2], lam, jnp.zeros((5, LRU_WIDTH), F32)], axis=0)


def _lru_w4(wa, wx):
    nl = wa.shape[0]
    w = jnp.stack([wa[:, 0], wx[:, 0], wa[:, 1], wx[:, 1]], axis=1)
    w = w.reshape(nl, 4, 3, 2, 64, 64)
    eye = jnp.eye(2, dtype=w.dtype)
    bd = w[:, :, :, :, :, None, :] * eye[None, None, None, :, None, :, None]
    bd = bd.reshape(nl, 4, 3, LANE, LANE)
    return bd.transpose(0, 2, 3, 1, 4).reshape(nl, 3, LANE, 4 * LANE).astype(_BF)


def _lru_unpack(dvec, dw4):
    def blocks(m):
        m = m.reshape(3, 2, 64, 2, 64)
        return jnp.stack([m[:, 0, :, 0, :], m[:, 1, :, 1, :]], axis=1).reshape(6, 64, 64)
    parts = [blocks(dw4[:, :, k * LANE:(k + 1) * LANE]) for k in range(4)]
    dwa = jnp.stack([parts[0], parts[2]])
    dwx = jnp.stack([parts[1], parts[3]])
    dba = jnp.stack([dvec[5], dvec[7]])
    dbx = jnp.stack([dvec[6], dvec[8]])
    return dvec[0:4], dvec[4], dwa, dba, dwx, dbx, dvec[9:11]


RC = 2 * RET_CHUNK


def _ret_tables(T):
    half = HEAD_DIM // 2
    pos = jnp.arange(T, dtype=F32)
    inv_freq = ROPE_BASE ** (-jnp.arange(half, dtype=F32) / half)
    ang = pos[:, None] * inv_freq[None, :]
    cos = jnp.tile(jnp.cos(ang), (1, 4))
    sin = jnp.tile(jnp.concatenate([-jnp.sin(ang), jnp.sin(ang)], axis=1), (1, 2))
    log_g = jnp.log1p(-jnp.exp2(-5.0 - jnp.arange(RET_HEADS, dtype=F32)))
    idx = jnp.arange(RC, dtype=F32)
    dec = jnp.exp(jnp.abs(idx[:, None] - idx[None, :]) * log_g[:, None, None])
    lg = jnp.repeat(log_g, HEAD_DIM).reshape(3, 1, LANE)
    col = idx[None, :, None]
    rtab = jnp.stack([jnp.exp((RC - 1 - col) * lg), jnp.exp(col * lg),
                      jnp.exp((col + 1.0) * lg), jnp.exp((RC - col) * lg)], axis=1)
    gch = jnp.broadcast_to(jnp.exp(RC * lg), (3, SUB, LANE))
    return cos, sin, dec, rtab, gch


def _swap32(x, lane):
    return jnp.where((lane & 32) == 0, pltpu.roll(x, LANE - 32, 1), pltpu.roll(x, 32, 1))


def _head_mean(x, m0, m1):
    s0 = jnp.sum(x * m0, axis=-1, keepdims=True)
    s1 = jnp.sum(x * m1, axis=-1, keepdims=True)
    return (s0 * m0 + s1 * m1) * (1.0 / HEAD_DIM)


def _ret_masks():
    lane = lax.broadcasted_iota(jnp.int32, (RC, LANE), 1)
    m0 = (lane < HEAD_DIM).astype(F32)
    r = lax.broadcasted_iota(jnp.int32, (LANE, LANE), 0) // HEAD_DIM
    c = lax.broadcasted_iota(jnp.int32, (LANE, LANE), 1) // HEAD_DIM
    return lane, m0, 1.0 - m0, (r == c).astype(F32)


def _ret_specs(T):
    const = lambda shape, imap: pl.BlockSpec(shape, imap)
    return [_strip(T, lambda j: j + 6), _strip(T, lambda j: j + 9), _strip(T, lambda j: j + 12),
            _strip(T, lambda j: j + 15),
            pl.BlockSpec((T, LANE), lambda j: (0, 0), pipeline_mode=pl.Buffered(1)),
            pl.BlockSpec((T, LANE), lambda j: (0, 0), pipeline_mode=pl.Buffered(1)),
            const((2, RC, RC), lambda j: (j, 0, 0)),
            const((1, 4, RC, LANE), lambda j: (j, 0, 0, 0)),
            const((1, SUB, LANE), lambda j: (j, 0, 0)),
            const((SUB, LANE), lambda j: (0, j))]


def _ret_fwd_call(proj, tables, gnw8):
    T = proj.shape[0]
    nc = T // RC
    cos, sin, dec, rtab, gch = tables

    def body(q_ref, k_ref, v_ref, g_ref, cos_ref, sin_ref, dec_ref, rtab_ref, gch_ref, gnw_ref, y_ref, stf_ref, kr_ref):
        lane, m0, m1, bd = _ret_masks()
        gch_v = gch_ref[0][0:1, :]
        gnw = gnw_ref[0:1, :]
        dkf, dkb, dqf, dqb = rtab_ref[0, 0], rtab_ref[0, 1], rtab_ref[0, 2], rtab_ref[0, 3]

        def rope(x, rows):
            return x * cos_ref[rows, :] + _swap32(x, lane) * sin_ref[rows, :]

        def pass_a(n, st):
            rows = _rows(n * RC, RC)
            stf_ref[n] = st
            kr = rope(k_ref[rows, :], rows) * (HEAD_DIM ** -0.5)
            kr_ref[rows, :] = kr
            return gch_v * st + _mm_tn(kr * dkf, v_ref[rows, :]) * bd

        _loop2(nc, pass_a, jnp.zeros((LANE, LANE), F32))

        def pass_b(i, stb):
            ns = [nc - 1 - 2 * i, nc - 2 - 2 * i]
            rows = [_rows(n * RC, RC) for n in ns]
            heads = ((0, m0), (1, m1))
            qr = [rope(q_ref[r, :], r) for r in rows]
            kr = [kr_ref[r, :] for r in rows]
            v = [v_ref[r, :] for r in rows]
            kv = [_mm_tn(kr[c] * dkb, v[c]) * bd for c in range(2)]
            stbs = [stb, gch_v * stb + kv[0]]
            s = [[_mm_nt(qr[c] * m, kr[c]) * dec_ref[h] for h, m in heads] for c in range(2)]
            o = [_mm(qr[c] * dqf, stf_ref[ns[c]]) + _mm(qr[c] * dqb, stbs[c]) for c in range(2)]
            o = [o[c] + _mm(s[c][0], v[c] * m0) + _mm(s[c][1], v[c] * m1) for c in range(2)]
            oc = [o_ - _head_mean(o_, m0, m1) for o_ in o]
            on = [oc_ * lax.rsqrt(_head_mean(oc_ * oc_, m0, m1) + GN_EPS) for oc_ in oc]
            for c in range(2):
                g = g_ref[rows[c], :]
                y_ref[rows[c], :] = ((g * _sigmoid(g)) * (on[c] * gnw)).astype(y_ref.dtype)
            return gch_v * stbs[1] + kv[1]

        assert nc % 2 == 0
        lax.fori_loop(0, nc // 2, pass_b, jnp.zeros((LANE, LANE), F32))

    return pl.pallas_call(
        body, name="ret_fwd", grid=(RET_WIDTH // LANE,),
        in_specs=_ret_specs(T),
        out_specs=_strip(T, lambda j: j, buffers=1),
        out_shape=jax.ShapeDtypeStruct((T, RET_WIDTH), _BF),
        scratch_shapes=[pltpu.VMEM((nc, LANE, LANE), F32), pltpu.VMEM((T, LANE), F32)],
        compiler_params=_cparams(("arbitrary",)),
    )(proj, proj, proj, proj, cos, sin, dec, rtab, gch, gnw8)


def _ret_bwd_call(proj, dycat, tables, gnw8, dp):
    T = proj.shape[0]
    nc = T // RC
    cos, sin, dec, rtab, gch = tables

    def body(q_ref, k_ref, v_ref, g_ref, cos_ref, sin_ref, dec_ref, rtab_ref, gch_ref, gnw_ref, dy_ref, dp_in_ref,
             dp_out_ref, dgnw_ref, stf_ref, dstb_ref, dkr_ref, dv_ref, dp_ref, kr_ref, sems):
        lane, m0, m1, bd = _ret_masks()
        gch_v = gch_ref[0][0:1, :]
        gnw = gnw_ref[0:1, :]
        dkf, dkb, dqf, dqb = rtab_ref[0, 0], rtab_ref[0, 1], rtab_ref[0, 2], rtab_ref[0, 3]
        scale = HEAD_DIM ** -0.5
        zst = jnp.zeros((LANE, LANE), F32)

        def rope(x, rows):
            return x * cos_ref[rows, :] + _swap32(x, lane) * sin_ref[rows, :]

        def rope_t(d, rows):
            return d * cos_ref[rows, :] + _swap32(d * sin_ref[rows, :], lane)

        def pass_a(n, st):
            rows = _rows(n * RC, RC)
            stf_ref[n] = st
            kr = rope(k_ref[rows, :], rows) * scale
            kr_ref[rows, :] = kr
            return gch_v * st + _mm_tn(kr * dkf, v_ref[rows, :]) * bd

        _loop2(nc, pass_a, zst)

        def pass_b(i, carry):
            stb, d_f, dgnw = carry
            two = range(2)
            heads = ((0, m0), (1, m1))
            ns = [nc - 1 - 2 * i, nc - 2 - 2 * i]
            rows = [_rows(n * RC, RC) for n in ns]
            qr = [rope(q_ref[r, :], r) for r in rows]
            kr = [kr_ref[r, :] for r in rows]
            v = [v_ref[r, :] for r in rows]
            stf = [stf_ref[n] for n in ns]
            kvb = [_mm_tn(kr[c] * dkb, v[c]) * bd for c in two]
            stbs = [stb, gch_v * stb + kvb[0]]
            qf = [qr[c] * dqf for c in two]
            qb = [qr[c] * dqb for c in two]
            s = [[_mm_nt(qr[c] * m, kr[c]) * dec_ref[h] for h, m in heads] for c in two]
            o = [_mm(qf[c], stf[c]) + _mm(qb[c], stbs[c]) for c in two]
            o = [o[c] + _mm(s[c][0], v[c] * m0) + _mm(s[c][1], v[c] * m1) for c in two]
            oc = [o_ - _head_mean(o_, m0, m1) for o_ in o]
            rstd = [lax.rsqrt(_head_mean(oc_ * oc_, m0, m1) + GN_EPS) for oc_ in oc]
            on = [oc[c] * rstd[c] for c in two]
            do = []
            for c in two:
                g = g_ref[rows[c], :]
                sg = _sigmoid(g)
                dy = dy_ref[rows[c], :]
                dp_ref[3, rows[c], :] = (dy * (on[c] * gnw) * (sg * (1.0 + g * (1.0 - sg)))).astype(dp_ref.dtype)
                t = dy * (g * sg)
                dgnw = dgnw + jnp.sum(t * on[c], axis=0, keepdims=True)
                don = t * gnw
                do.append(rstd[c] * (don - _head_mean(don, m0, m1) - on[c] * _head_mean(don * on[c], m0, m1)))
            dstf = [_mm_tn(qf[c], do[c]) * bd for c in two]
            dfs = [d_f, dstf[0] + gch_v * d_f]
            ds = [[_mm_nt(do[c] * m, v[c]) * dec_ref[h] for h, m in heads] for c in two]
            dqr = [_mm_nt(do[c], stf[c]) * dqf + _mm_nt(do[c], stbs[c]) * dqb
                   + _mm(ds[c][0], kr[c] * m0) + _mm(ds[c][1], kr[c] * m1) for c in two]
            dkr = [_mm_nt(v[c], dfs[c]) * dkf + _mm_tn(ds[c][0], qr[c] * m0) + _mm_tn(ds[c][1], qr[c] * m1) for c in two]
            dv = [_mm(kr[c] * dkf, dfs[c]) + _mm_tn(s[c][0], do[c] * m0) + _mm_tn(s[c][1], do[c] * m1) for c in two]
            for c in two:
                dp_ref[0, rows[c], :] = rope_t(dqr[c], rows[c]).astype(dp_ref.dtype)
                dkr_ref[rows[c], :] = dkr[c]
                dv_ref[rows[c], :] = dv[c]
                dstb_ref[ns[c]] = _mm_tn(qb[c], do[c]) * bd
            return gch_v * stbs[1] + kvb[1], dstf[1] + gch_v * dfs[1], dgnw

        assert nc % 2 == 0
        _, _, dgnw = lax.fori_loop(0, nc // 2, pass_b, (zst, zst, jnp.zeros((1, LANE), F32)))
        dgnw_ref[...] = jnp.concatenate([dgnw, jnp.zeros((SUB - 1, LANE), F32)], axis=0)

        def pass_c(n, d_b):
            rows = _rows(n * RC, RC)
            kr = kr_ref[rows, :]
            v = v_ref[rows, :]
            dkr = dkr_ref[rows, :] + _mm_nt(v, d_b) * dkb
            dp_ref[1, rows, :] = (rope_t(dkr, rows) * scale).astype(dp_ref.dtype)
            dp_ref[2, rows, :] = (dv_ref[rows, :] + _mm(kr * dkb, d_b)).astype(dp_ref.dtype)
            return dstb_ref[n] + gch_v * d_b

        _loop2(nc, pass_c, zst)
        j = pl.program_id(0)
        _store_strips(dp_ref, dp_out_ref, (j + 6, j + 9, j + 12, j + 15), sems)

    n_in = len(_ret_specs(T)) + 1
    return pl.pallas_call(
        body, name="ret_bwd", grid=(RET_WIDTH // LANE,),
        in_specs=_ret_specs(T) + [_strip(T, lambda j: j + 3), pl.BlockSpec(memory_space=pl.ANY)],
        out_specs=[pl.BlockSpec(memory_space=pl.ANY), pl.BlockSpec((SUB, LANE), lambda j: (0, j))],
        out_shape=[jax.ShapeDtypeStruct(dp.shape, dp.dtype), jax.ShapeDtypeStruct((SUB, RET_WIDTH), F32)],
        scratch_shapes=[pltpu.VMEM((nc, LANE, LANE), F32), pltpu.VMEM((nc, LANE, LANE), F32),
                        pltpu.VMEM((T, LANE), F32), pltpu.VMEM((T, LANE), F32),
                        pltpu.VMEM((4, T, LANE), _BF), pltpu.VMEM((T, LANE), F32), pltpu.SemaphoreType.DMA((4,))],
        input_output_aliases={n_in: 0},
        compiler_params=_cparams(("arbitrary",)),
    )(proj, proj, proj, proj, cos, sin, dec, rtab, gch, gnw8, dycat, dp)


NA_Q = 2 * GRID_W
NA_WROWS = 10
NA_K = NA_WROWS * GRID_W
NA_CHUNKS = NA_K // LANE
NA_UNROLL = 4
NA_UNROLL_FWD = 8
NA_TYPES = 5
_ONEHOT_PRECISION = lax.Precision.HIGH


def _na_onehots(rows_n):
    reps = [(0, 0), (2, 0), (4, 0), (rows_n - 4, rows_n - NA_WROWS), (rows_n - 2, rows_n - NA_WROWS)]
    rm = np.zeros((NA_TYPES, 2, NA_WROWS, 2 * NA_KH - 1), np.float32)
    for t, (r, ws) in enumerate(reps):
        for qh in range(2):
            qrow = r + qh
            rstart = min(max(qrow - NA_KH // 2, 0), rows_n - NA_KH)
            for kh in range(NA_WROWS):
                krow = ws + kh
                if rstart <= krow < rstart + NA_KH:
                    rm[t, qh, kh, krow - qrow + NA_KH - 1] = 1.0
    cm = np.zeros((GRID_W, GRID_W, 2 * NA_KW - 1), np.float32)
    for qc in range(GRID_W):
        cstart = min(max(qc - NA_KW // 2, 0), GRID_W - NA_KW)
        for kc in range(cstart, cstart + NA_KW):
            cm[qc, kc, kc - qc + NA_KW - 1] = 1.0
    rm2 = rm.reshape(NA_TYPES, 2, NA_CHUNKS, 2, 2 * NA_KH - 1)
    cm2 = np.zeros((GRID_W, LANE, 2, 2 * NA_KW - 1), np.float32)
    for z in range(2):
        cm2[:, z * GRID_W:(z + 1) * GRID_W, z, :] = cm
    return rm2, cm2


def _na_bias_tables(rpb, rows_n):
    rm, cm = _na_onehots(rows_n)
    val = jnp.einsum("hab,tqpza,xkzb->htpqxk", rpb, rm, cm, precision=_ONEHOT_PRECISION)
    valid = np.einsum("tqpz,xkz->tpqxk", rm.sum(-1), cm.sum(-1)) > 0.5
    return jnp.where(valid[None], val, NEG).reshape(2, 2, NA_TYPES, NA_CHUNKS, NA_Q, LANE)


def _na_bias_grad(dtab, rows_n):
    rm, cm = _na_onehots(rows_n)
    d6 = dtab.reshape(NA_HEADS, NA_TYPES, NA_CHUNKS, 2, GRID_W, LANE)
    return jnp.einsum("htpqxk,tqpza,xkzb->hab", d6, rm, cm, precision=_ONEHOT_PRECISION)


def _na_bias(b_ref, h, typ):
    return jnp.concatenate([b_ref[0, h, typ, c] for c in range(NA_CHUNKS)], axis=1)


def _na_step(p, npairs, rows_n):
    ws = jnp.clip(2 * p - NA_KH // 2, 0, rows_n - NA_WROWS)
    koff = pl.multiple_of(ws * GRID_W, LANE)
    typ = jnp.where(p == 0, 0, jnp.where(p == 1, 1, jnp.where(p == npairs - 2, 3, jnp.where(p == npairs - 1, 4, 2))))
    return _rows(p * NA_Q, NA_Q), pl.ds(koff, NA_K), typ


def _na_fwd_call(proj, btab):
    T = proj.shape[0]
    npairs, rows_n = T // NA_Q, T // GRID_W

    def body(q_ref, k_ref, v_ref, b_ref, o_ref):
        lane = lax.broadcasted_iota(jnp.int32, (NA_Q, LANE), 1)
        m0 = (lane < HEAD_DIM).astype(F32)
        m1 = 1.0 - m0

        def steps(i, carry):
            idx = [_na_step(unroll * i + u, npairs, rows_n) for u in range(unroll)]
            chains = [(u, h, m) for u in range(unroll) for h, m in ((0, m0), (1, m1))]
            kws = [k_ref[krows, :].astype(_BF) for _, krows, _ in idx]
            vws = [v_ref[krows, :].astype(_BF) for _, krows, _ in idx]
            s = [_mm_nt(q_ref[idx[u][0], :] * m, kws[u]) for u, h, m in chains]
            s = [s_ * (HEAD_DIM ** -0.5) + _na_bias(b_ref, h, idx[u][2]) for s_, (u, h, m) in zip(s, chains)]
            e = [jnp.exp(s_ - jnp.max(s_, axis=-1, keepdims=True)) for s_ in s]
            pr = [e_ / jnp.sum(e_, axis=-1, keepdims=True) for e_ in e]
            ov = [_mm(pr_, vws[u]) * m for pr_, (u, h, m) in zip(pr, chains)]
            for u in range(unroll):
                o_ref[idx[u][0], :] = (ov[2 * u] + ov[2 * u + 1]).astype(o_ref.dtype)
            return carry

        unroll = NA_UNROLL_FWD if npairs % NA_UNROLL_FWD == 0 else NA_UNROLL
        lax.fori_loop(0, npairs // unroll, steps, 0)

    return pl.pallas_call(
        body, name="na_fwd", grid=(NA_WIDTH // LANE,),
        in_specs=[_strip(T, lambda j: j + 18), _strip(T, lambda j: j + 20), _strip(T, lambda j: j + 22),
                  pl.BlockSpec((1, 2, NA_TYPES, NA_CHUNKS, NA_Q, LANE), lambda j: (j, 0, 0, 0, 0, 0))],
        out_specs=_strip(T, lambda j: j, buffers=1),
        out_shape=jax.ShapeDtypeStruct((T, NA_WIDTH), _BF),
        compiler_params=_cparams(("arbitrary",)),
    )(proj, proj, proj, btab)


def _na_bwd_call(proj, dycat, btab, dp):
    T = proj.shape[0]
    npairs, rows_n = T // NA_Q, T // GRID_W
    scale = HEAD_DIM ** -0.5

    def body(q_ref, k_ref, v_ref, do_ref, b_ref, dp_in_ref, dp_out_ref, db_ref, dka_ref, dva_ref, stage_ref, sems):
        dq_ref = stage_ref.at[0]
        lane = lax.broadcasted_iota(jnp.int32, (NA_Q, LANE), 1)
        m0 = (lane < HEAD_DIM).astype(F32)
        m1 = 1.0 - m0
        dka_ref[...] = jnp.zeros_like(dka_ref)
        dva_ref[...] = jnp.zeros_like(dva_ref)
        db_ref[...] = jnp.zeros_like(db_ref)

        def steps(i, carry):
            idx = [_na_step(NA_UNROLL * i + u, npairs, rows_n) for u in range(NA_UNROLL)]
            chains = [(u, h, m) for u in range(NA_UNROLL) for h, m in ((0, m0), (1, m1))]
            kws = [k_ref[krows, :].astype(_BF) for _, krows, _ in idx]
            vws = [v_ref[krows, :].astype(_BF) for _, krows, _ in idx]
            qm = [(q_ref[idx[u][0], :] * m).astype(_BF) for u, h, m in chains]
            dom = [(do_ref[idx[u][0], :] * m).astype(_BF) for u, h, m in chains]
            s = [_mm_nt(qm_, kws[u]) for qm_, (u, h, m) in zip(qm, chains)]
            dpr = [_mm_nt(dom_, vws[u]) for dom_, (u, h, m) in zip(dom, chains)]
            s = [s_ * scale + _na_bias(b_ref, h, idx[u][2]) for s_, (u, h, m) in zip(s, chains)]
            e = [jnp.exp(s_ - jnp.max(s_, axis=-1, keepdims=True)) for s_ in s]
            pr = [e_ / jnp.sum(e_, axis=-1, keepdims=True) for e_ in e]
            ds = [pr_ * (dpr_ - jnp.sum(pr_ * dpr_, axis=-1, keepdims=True)) for pr_, dpr_ in zip(pr, dpr)]
            dsb = [(ds_ * scale).astype(_BF) for ds_ in ds]
            dq = [_mm(dsb_, kws[u]) * m for dsb_, (u, h, m) in zip(dsb, chains)]
            dk = [_mm_tn(dsb_, qm_) for dsb_, qm_ in zip(dsb, qm)]
            dv = [_mm_tn(pr_, dom_) for pr_, dom_ in zip(pr, dom)]
            for ds_, (u, h, m) in zip(ds, chains):
                for c in range(NA_CHUNKS):
                    db_ref[0, h, idx[u][2], c] += ds_[:, c * LANE:(c + 1) * LANE]
            for u in range(NA_UNROLL):
                qrows, krows, _ = idx[u]
                dq_ref[qrows, :] = (dq[2 * u] + dq[2 * u + 1]).astype(dq_ref.dtype)
                dka_ref[krows, :] += dk[2 * u] + dk[2 * u + 1]
                dva_ref[krows, :] += dv[2 * u] + dv[2 * u + 1]
            return carry

        lax.fori_loop(0, npairs // NA_UNROLL, steps, 0)
        stage_ref[1] = dka_ref[...].astype(stage_ref.dtype)
        stage_ref[2] = dva_ref[...].astype(stage_ref.dtype)
        j = pl.program_id(0)
        _store_strips(stage_ref, dp_out_ref, (j + 18, j + 20, j + 22), sems)

    tab = pl.BlockSpec((1, 2, NA_TYPES, NA_CHUNKS, NA_Q, LANE), lambda j: (j, 0, 0, 0, 0, 0))
    return pl.pallas_call(
        body, name="na_bwd", grid=(NA_WIDTH // LANE,),
        in_specs=[_strip(T, lambda j: j + 18), _strip(T, lambda j: j + 20), _strip(T, lambda j: j + 22),
                  _strip(T, lambda j: j + 6), tab, pl.BlockSpec(memory_space=pl.ANY)],
        out_specs=[pl.BlockSpec(memory_space=pl.ANY), tab],
        out_shape=[jax.ShapeDtypeStruct(dp.shape, dp.dtype),
                   jax.ShapeDtypeStruct((2, 2, NA_TYPES, NA_CHUNKS, NA_Q, LANE), F32)],
        scratch_shapes=[pltpu.VMEM((T, LANE), F32), pltpu.VMEM((T, LANE), F32),
                        pltpu.VMEM((3, T, LANE), _BF), pltpu.SemaphoreType.DMA((3,))],
        input_output_aliases={5: 0},
        compiler_params=_cparams(("arbitrary",)),
    )(proj, proj, proj, dycat, btab, dp)


W_BLK = IN_WIDTH // N_DEV
MXU_W = 256
N_BLK = 4 * MXU_W
N_STEPS = IN_WIDTH // N_BLK
TM = 512


def _ln_fwd(z, g, b):
    zc = z - jnp.mean(z, axis=-1, keepdims=True)
    var = jnp.mean(zc * zc, axis=-1, keepdims=True)
    return zc * lax.rsqrt(var + LN_EPS) * g + b


def _ln_bwd(dy, z, g):
    zc = z - jnp.mean(z, axis=-1, keepdims=True)
    rstd = lax.rsqrt(jnp.mean(zc * zc, axis=-1, keepdims=True) + LN_EPS)
    xhat = zc * rstd
    dxh = dy * g
    dz = rstd * (dxh - jnp.mean(dxh, axis=-1, keepdims=True) - xhat * jnp.mean(dxh * xhat, axis=-1, keepdims=True))
    return dz, dy * xhat


def _row_tile(T):
    return 1024 if T % 1024 == 0 else TM


def _halves(n):
    return (pl.ds(0, n // 2), pl.ds(n // 2, n // 2))


def _inproj_call(xb, w, after):
    T = xb.shape[0]
    tm = _row_tile(T)

    def body(x_ref, w_ref, after_ref, o_ref):
        o_ref[...] = _mm(x_ref[...], w_ref[...])

    return pl.pallas_call(
        body, name="inproj", grid=(T // tm, N_STEPS),
        in_specs=[pl.BlockSpec((tm, D_MODEL), lambda i, n: (i, 0)),
                  pl.BlockSpec((D_MODEL, N_BLK), lambda i, n: (0, n)),
                  pl.BlockSpec(memory_space=pl.ANY)],
        out_specs=pl.BlockSpec((tm, N_BLK), lambda i, n: (i, n)),
        out_shape=jax.ShapeDtypeStruct((T, IN_WIDTH), F32),
        compiler_params=_cparams(("parallel", "arbitrary")),
    )(xb, w, after)


def _vec_spec():
    return pl.BlockSpec((1, D_MODEL), lambda *_: (0, 0))


def _outproj_ln_call(y_lru, y_ret, y_na, x, w, g, b, after):
    T = x.shape[0]

    def body(yl_ref, yr_ref, yn_ref, x_ref, w_ref, g_ref, b_ref, after_ref, z_ref, x1b_ref, yc_ref):
        yc_ref[:, 0:LRU_WIDTH] = yl_ref[...].astype(yc_ref.dtype)
        yc_ref[:, LRU_WIDTH:LRU_WIDTH + RET_WIDTH] = yr_ref[...].astype(yc_ref.dtype)
        yc_ref[:, LRU_WIDTH + RET_WIDTH:] = yn_ref[...].astype(yc_ref.dtype)
        z = ALPHA * x_ref[...] + _mm(yc_ref[...], w_ref[...])
        z_ref[...] = z
        x1b_ref[...] = _ln_fwd(z, g_ref[...], b_ref[...]).astype(x1b_ref.dtype)

    row = lambda w_: pl.BlockSpec((TM, w_), lambda i: (i, 0))
    return pl.pallas_call(
        body, name="outproj_ln", grid=(T // TM,),
        in_specs=[row(LRU_WIDTH), row(RET_WIDTH), row(NA_WIDTH), row(D_MODEL),
                  pl.BlockSpec((D_MODEL, D_MODEL), lambda i: (0, 0)), _vec_spec(), _vec_spec(),
                  pl.BlockSpec(memory_space=pl.ANY)],
        out_specs=[row(D_MODEL)] * 3,
        out_shape=[jax.ShapeDtypeStruct((T, D_MODEL), F32),
                   jax.ShapeDtypeStruct((T, D_MODEL), _BF), jax.ShapeDtypeStruct((T, D_MODEL), _BF)],
        compiler_params=_cparams(("parallel",)),
    )(y_lru, y_ret, y_na, x, w, g, b, after)


def _ffn_ln_call(z1, x1b, wg, wu, wd, g, b, g1, b1):
    T = z1.shape[0]

    def body(z1_ref, xb_ref, wg_ref, wu_ref, wd_ref, g_ref, b_ref, g1_ref, b1_ref,
             z_ref, x2_ref, x2b_ref, gp_ref, up_ref, acc_ref):
        n = pl.program_id(1)

        @pl.when(n == 0)
        def _():
            acc_ref[...] = jnp.zeros_like(acc_ref)

        r0, r1 = _halves(TM)

        def pre(rows):
            xb = xb_ref[rows, :]
            return _mm(xb, wg_ref[...]), _mm(xb, wu_ref[...])

        def act(rows, gp, up):
            gp_ref[rows, :] = gp.astype(gp_ref.dtype)
            up_ref[rows, :] = up.astype(up_ref.dtype)
            return (gp * _sigmoid(gp) * up).astype(_BF)

        gp0, up0 = pre(r0)
        hid0 = act(r0, gp0, up0)
        gp1, up1 = pre(r1)
        acc_ref[r0, :] += _mm(hid0, wd_ref[...])
        hid1 = act(r1, gp1, up1)
        acc_ref[r1, :] += _mm(hid1, wd_ref[...])

        @pl.when(n == N_STEPS - 1)
        def _():
            z = ALPHA * _ln_fwd(z1_ref[...], g1_ref[...], b1_ref[...]) + acc_ref[...]
            z_ref[...] = z
            x2 = _ln_fwd(z, g_ref[...], b_ref[...])
            x2_ref[...] = x2
            x2b_ref[...] = x2.astype(x2b_ref.dtype)

    row = pl.BlockSpec((TM, D_MODEL), lambda i, n: (i, 0))
    return pl.pallas_call(
        body, name="ffn_ln", grid=(T // TM, N_STEPS),
        in_specs=[row, row,
                  pl.BlockSpec((D_MODEL, N_BLK), lambda i, n: (0, n)),
                  pl.BlockSpec((D_MODEL, N_BLK), lambda i, n: (0, n)),
                  pl.BlockSpec((N_BLK, D_MODEL), lambda i, n: (n, 0)), _vec_spec(), _vec_spec(), _vec_spec(), _vec_spec()],
        out_specs=[row] * 3 + [pl.BlockSpec((TM, N_BLK), lambda i, n: (i, n))] * 2,
        out_shape=[jax.ShapeDtypeStruct((T, D_MODEL), F32), jax.ShapeDtypeStruct((T, D_MODEL), F32),
                   jax.ShapeDtypeStruct((T, D_MODEL), _BF),
                   jax.ShapeDtypeStruct((T, IN_WIDTH), _BF), jax.ShapeDtypeStruct((T, IN_WIDTH), _BF)],
        scratch_shapes=[pltpu.VMEM((TM, D_MODEL), F32)],
        compiler_params=_cparams(("parallel", "arbitrary")),
    )(z1, x1b, wg, wu, wd, g, b, g1, b1)


def _loss_call(y, t):
    T = y.shape[0]

    def body(y_ref, t_ref, dy_ref, loss_ref):
        @pl.when(pl.program_id(0) == 0)
        def _():
            loss_ref[...] = jnp.zeros_like(loss_ref)

        err = y_ref[...] - t_ref[...]
        dy_ref[...] = err * (1.0 / D_MODEL)
        part = 0.5 * jnp.sum(jnp.mean(err * err, axis=-1, keepdims=True), axis=0, keepdims=True)
        loss_ref[...] += jnp.broadcast_to(part, loss_ref.shape)

    row = pl.BlockSpec((TM, D_MODEL), lambda i: (i, 0))
    return pl.pallas_call(
        body, name="loss", grid=(T // TM,),
        in_specs=[row, row],
        out_specs=[row, pl.BlockSpec((SUB, LANE), lambda i: (0, 0))],
        out_shape=[jax.ShapeDtypeStruct((T, D_MODEL), F32), jax.ShapeDtypeStruct((SUB, LANE), F32)],
        compiler_params=_cparams(("arbitrary",)),
    )(y, t)


def _ffn_bwd_call(dx2, z2, gpb, upb, wg, wu, wd, g, after):
    T = dx2.shape[0]

    def body(dx2_ref, z_ref, gp_ref, up_ref, wg_ref, wu_ref, wd_ref, g_ref, after_ref,
             dx1_ref, dgp_ref, dup_ref, hid_ref, dzb_ref, dln_ref, acc_ref):
        i, n = pl.program_id(0), pl.program_id(1)

        @pl.when((i == 0) & (n == 0))
        def _():
            dln_ref[...] = jnp.zeros_like(dln_ref)

        @pl.when(n == 0)
        def _():
            dy = dx2_ref[...]
            dz, dg_rows = _ln_bwd(dy, z_ref[...], g_ref[...])
            dzb_ref[...] = dz.astype(dzb_ref.dtype)
            acc_ref[...] = ALPHA * dz
            dln_ref[0:1, :] += jnp.sum(dg_rows, axis=0, keepdims=True)
            dln_ref[1:2, :] += jnp.sum(dy, axis=0, keepdims=True)

        r0, r1 = _halves(TM)

        def grads(rows, dhid):
            gp = gp_ref[rows, :].astype(F32)
            up = up_ref[rows, :].astype(F32)
            sg = _sigmoid(gp)
            act = gp * sg
            hid_ref[rows, :] = (act * up).astype(hid_ref.dtype)
            dup = (dhid * act).astype(_BF)
            dgp = (dhid * up * (sg * (1.0 + gp * (1.0 - sg)))).astype(_BF)
            dgp_ref[rows, :] = dgp.astype(dgp_ref.dtype)
            dup_ref[rows, :] = dup.astype(dup_ref.dtype)
            return dgp, dup

        dhid0 = _mm_nt(dzb_ref[r0, :], wd_ref[...])
        dhid1 = _mm_nt(dzb_ref[r1, :], wd_ref[...])
        dgp0, dup0 = grads(r0, dhid0)
        acc_ref[r0, :] += _mm_nt(dgp0, wg_ref[...]) + _mm_nt(dup0, wu_ref[...])
        dgp1, dup1 = grads(r1, dhid1)
        acc_ref[r1, :] += _mm_nt(dgp1, wg_ref[...]) + _mm_nt(dup1, wu_ref[...])

        @pl.when(n == N_STEPS - 1)
        def _():
            dx1_ref[...] = acc_ref[...]

    row = pl.BlockSpec((TM, D_MODEL), lambda i, n: (i, 0))
    blk = pl.BlockSpec((TM, N_BLK), lambda i, n: (i, n))
    return pl.pallas_call(
        body, name="ffn_bwd", grid=(T // TM, N_STEPS),
        in_specs=[row, row, blk, blk,
                  pl.BlockSpec((D_MODEL, N_BLK), lambda i, n: (0, n)),
                  pl.BlockSpec((D_MODEL, N_BLK), lambda i, n: (0, n)),
                  pl.BlockSpec((N_BLK, D_MODEL), lambda i, n: (n, 0)), _vec_spec(),
                  pl.BlockSpec(memory_space=pl.ANY)],
        out_specs=[row, blk, blk, blk, row, pl.BlockSpec((SUB, D_MODEL), lambda i, n: (0, 0))],
        out_shape=[jax.ShapeDtypeStruct((T, D_MODEL), F32),
                   jax.ShapeDtypeStruct((T, IN_WIDTH), _BF), jax.ShapeDtypeStruct((T, IN_WIDTH), _BF),
                   jax.ShapeDtypeStruct((T, IN_WIDTH), _BF), jax.ShapeDtypeStruct((T, D_MODEL), _BF),
                   jax.ShapeDtypeStruct((SUB, D_MODEL), F32)],
        scratch_shapes=[pltpu.VMEM((TM, D_MODEL), F32)],
        compiler_params=_cparams(("arbitrary", "arbitrary")),
    )(dx2, z2, gpb, upb, wg, wu, wd, g, after)


def _outproj_bwd_call(dx1, z1, w, g):
    T = dx1.shape[0]

    def body(dx_ref, z_ref, w_ref, g_ref, dzb_ref, dyc_ref, dres_ref, dln_ref):
        @pl.when(pl.program_id(0) == 0)
        def _():
            dln_ref[...] = jnp.zeros_like(dln_ref)

        dy = dx_ref[...]
        dz, dg_rows = _ln_bwd(dy, z_ref[...], g_ref[...])
        dzb_ref[...] = dz.astype(dzb_ref.dtype)
        dres_ref[...] = ALPHA * dz
        dyc_ref[...] = _mm_nt(dz, w_ref[...])
        dln_ref[0:1, :] += jnp.sum(dg_rows, axis=0, keepdims=True)
        dln_ref[1:2, :] += jnp.sum(dy, axis=0, keepdims=True)

    row = pl.BlockSpec((TM, D_MODEL), lambda i: (i, 0))
    return pl.pallas_call(
        body, name="outproj_bwd", grid=(T // TM,),
        in_specs=[row, row, pl.BlockSpec((D_MODEL, D_MODEL), lambda i: (0, 0)), _vec_spec()],
        out_specs=[row, row, row, pl.BlockSpec((SUB, D_MODEL), lambda i: (0, 0))],
        out_shape=[jax.ShapeDtypeStruct((T, D_MODEL), _BF), jax.ShapeDtypeStruct((T, D_MODEL), F32),
                   jax.ShapeDtypeStruct((T, D_MODEL), F32), jax.ShapeDtypeStruct((SUB, D_MODEL), F32)],
        compiler_params=_cparams(("arbitrary",)),
    )(dx1, z1, w, g)


def _inproj_bwd_call(dres, dp, w):
    T = dres.shape[0]

    def body(dres_ref, dp_ref, w_ref, dx_ref):
        dx_ref[...] = dres_ref[...] + _mm_nt(dp_ref[...], w_ref[...])

    row = pl.BlockSpec((TM, D_MODEL), lambda i: (i, 0))
    return pl.pallas_call(
        body, name="inproj_bwd", grid=(T // TM,),
        in_specs=[row, pl.BlockSpec((TM, IN_WIDTH), lambda i: (i, 0)),
                  pl.BlockSpec((D_MODEL, IN_WIDTH), lambda i: (0, 0), pipeline_mode=pl.Buffered(1))],
        out_specs=row,
        out_shape=jax.ShapeDtypeStruct((T, D_MODEL), F32),
        compiler_params=_cparams(("parallel",)),
    )(dres, dp, w)


def _tn_cols_call(a, b, name):
    T, ka = a.shape
    n = b.shape[1]

    def body(a_ref, b_ref, o_ref):
        o_ref[...] = _mm_tn(a_ref[...], b_ref[...]).astype(o_ref.dtype)

    return pl.pallas_call(
        body, name=name, grid=(n // N_BLK,),
        in_specs=[pl.BlockSpec((T, ka), lambda j: (0, 0), pipeline_mode=pl.Buffered(1)),
                  pl.BlockSpec((T, N_BLK), lambda j: (0, j))],
        out_specs=pl.BlockSpec((ka, N_BLK), lambda j: (0, j)),
        out_shape=jax.ShapeDtypeStruct((ka, n), _BF),
        compiler_params=_cparams(("parallel",)),
    )(a, b)


def _tn_rows_call(a, b, kb, name):
    T, ka = a.shape
    n = b.shape[1]

    def body(a_ref, b_ref, o_ref):
        o_ref[...] = _mm_tn(a_ref[...], b_ref[...]).astype(o_ref.dtype)

    return pl.pallas_call(
        body, name=name, grid=(ka // kb,),
        in_specs=[pl.BlockSpec((T, kb), lambda r: (0, r)),
                  pl.BlockSpec((T, n), lambda r: (0, 0), pipeline_mode=pl.Buffered(1))],
        out_specs=pl.BlockSpec((kb, n), lambda r: (r, 0)),
        out_shape=jax.ShapeDtypeStruct((ka, n), _BF),
        compiler_params=_cparams(("parallel",)),
    )(a, b)


def _me():
    return lax.axis_index("x"), lax.axis_index("y"), lax.axis_index("c")


def _flip(k):
    x, y, c = _me()
    return (1 - x if k & 4 else x, 1 - y if k & 2 else y, 1 - c if k & 1 else c)


def _dev_index(pos):
    return 4 * pos[0] + 2 * pos[1] + pos[2]


_HBM = pl.BlockSpec(memory_space=pltpu.HBM)
_SEM = pl.BlockSpec(memory_space=pltpu.SEMAPHORE)


def _land_shape(shape, mode):
    if mode == "all":
        return (N_DEV,) + shape
    if mode == "cols":
        return (shape[0], N_DEV * shape[1])
    if mode == "blk":
        return shape
    assert mode == "scols"
    return (N_DEV, shape[0], shape[1] // N_DEV)


def _comm_copies(ins, lands, modes, send_sems, recv_sems):
    me = _dev_index(_me())
    copies = []
    for k in range(N_DEV):
        peer = _flip(k)
        pidx = _dev_index(peer)
        for a, (src, land, mode) in enumerate(zip(ins, lands, modes)):
            if mode == "blk":
                src = src.at[pidx]
            elif mode == "scols":
                w = src.shape[1] // N_DEV
                src = src.at[:, pl.ds(pl.multiple_of(pidx * w, LANE), w)]
            if mode == "cols":
                w = src.shape[1]
                dst = land.at[:, pl.ds(pl.multiple_of(me * w, LANE), w)]
            else:
                dst = land.at[me]
            copies.append(pltpu.make_async_remote_copy(
                src_ref=src, dst_ref=dst, send_sem=send_sems.at[k * len(ins) + a], recv_sem=recv_sems.at[k * len(ins) + a],
                device_id=peer, device_id_type=MESH))
    return copies


def _comm_start_call(arrs, gather_flags, after, name):
    n = len(arrs)
    lands = [lax.empty(_land_shape(v.shape, mode), v.dtype) for v, mode in zip(arrs, gather_flags)]

    def body(*refs):
        ins, lnd = refs[:n], refs[n:2 * n]
        send_sems, recv_sems = refs[2 * n + len(after)], refs[2 * n + len(after) + 1]
        for cp in _comm_copies(ins, lnd, gather_flags, send_sems, recv_sems):
            cp.start()
        refs[-1][...] = jnp.zeros_like(refs[-1])

    hbm = [pltpu.with_memory_space_constraint(v, pltpu.HBM) for v in list(arrs) + lands]
    out = pl.pallas_call(
        body, name=name,
        out_shape=(pltpu.SemaphoreType.DMA((N_DEV * n,)), pltpu.SemaphoreType.DMA((N_DEV * n,)),
                   *[pltpu.HBM(v.shape, v.dtype) for v in hbm], jax.ShapeDtypeStruct((SUB, LANE), F32)),
        in_specs=[_HBM] * (2 * n) + [pl.BlockSpec(memory_space=pl.ANY)] * len(after),
        out_specs=(_SEM, _SEM, *[_HBM] * (2 * n), pl.BlockSpec(memory_space=pltpu.VMEM)),
        input_output_aliases={i: 2 + i for i in range(2 * n)},
        compiler_params=pltpu.CompilerParams(has_side_effects=pltpu.SideEffectType.DATAFLOW_SIDE_EFFECTING),
    )(*hbm, *after)
    return out[:-1], out[-1]


def _comm_wait_call(state, gather_flags, after, name):
    n = len(gather_flags)
    send_sems, recv_sems, thru = state[0], state[1], state[2:]

    def body(*refs):
        ins, lnd, ssem, rsem = refs[:n], refs[n:2 * n], refs[2 * n], refs[2 * n + 1]
        for cp in _comm_copies(ins, lnd, gather_flags, ssem, rsem):
            cp.wait_send()
            cp.wait_recv()

    out = pl.pallas_call(
        body, name=name,
        out_shape=tuple(pltpu.HBM(v.shape, v.dtype) for v in thru),
        in_specs=[_HBM] * (2 * n) + [_SEM, _SEM] + [pl.BlockSpec(memory_space=pl.ANY)] * len(after),
        out_specs=tuple([_HBM] * (2 * n)),
        input_output_aliases={i: i for i in range(2 * n)},
        compiler_params=pltpu.CompilerParams(has_side_effects=pltpu.SideEffectType.DATAFLOW_SIDE_EFFECTING),
    )(*thru, send_sems, recv_sems, *after)
    return out[n:]


def _sum8_call(recv, stacked, layer, nl, rows, r_out, c_out, name, transposed=False):
    c = recv.shape[2]

    def body(x_ref, *rest):
        o_ref = rest[-1]
        if transposed:
            acc = x_ref[0].astype(F32)
            for s in range(1, N_DEV):
                acc = acc + x_ref[s].astype(F32)
            o_ref[...] = acc.T[:c_out, :]
        else:
            acc = x_ref[0, :, :c_out].astype(F32)
            for s in range(1, N_DEV):
                acc = acc + x_ref[s, :, :c_out].astype(F32)
            o_ref[...] = acc

    prev = [] if stacked is None else [stacked]
    if transposed:
        out_spec = pl.BlockSpec((None, c_out, rows), lambda i: (layer, 0, i))
        out_shape = jax.ShapeDtypeStruct((nl, c_out, r_out), F32)
    else:
        out_spec = pl.BlockSpec((None, rows, c_out), lambda i: (layer, i, 0))
        out_shape = jax.ShapeDtypeStruct((nl, r_out, c_out), F32)
    return pl.pallas_call(
        body, name=name, grid=(r_out // rows,),
        in_specs=[pl.BlockSpec((N_DEV, rows, c), lambda i: (0, i, 0))] + [pl.BlockSpec(memory_space=pl.ANY)] * len(prev),
        out_specs=out_spec, out_shape=out_shape,
        input_output_aliases={1: 0} if prev else {},
        compiler_params=_cparams(("parallel",)),
    )(recv, *prev)


def _adamw_call(w, g, m, v, rows, name):
    r, c = w.shape

    def body(w_ref, g_ref, m_ref, v_ref, d_ref, nm_ref, nv_ref):
        gr = g_ref[...]
        nm = ADAM_B1 * m_ref[...] + (1.0 - ADAM_B1) * gr
        nv = ADAM_B2 * v_ref[...] + (1.0 - ADAM_B2) * (gr * gr)
        m_hat = nm / (1.0 - ADAM_B1 ** ADAM_STEP)
        v_hat = nv / (1.0 - ADAM_B2 ** ADAM_STEP)
        d_ref[...] = -ADAM_LR * (m_hat / (jnp.sqrt(v_hat) + ADAM_EPS) + ADAM_WD * w_ref[...])
        nm_ref[...] = nm
        nv_ref[...] = nv

    spec = pl.BlockSpec((rows, c), lambda i: (i, 0))
    return pl.pallas_call(
        body, name=name, grid=(r // rows,),
        in_specs=[spec] * 4, out_specs=[spec] * 3,
        out_shape=[jax.ShapeDtypeStruct((r, c), F32)] * 3,
        compiler_params=_cparams(("parallel",)),
    )(w, g, m, v)


SH_ROWS = 16
SH_W = LRU_WIDTH // N_DEV
REP_ROWS = 824
_REP_SIZES = (LRU_WIDTH, 2 * 6 * 64 * 64, 2 * 6 * 64 * 64, RET_WIDTH, 1920, D_MODEL, D_MODEL, D_MODEL, D_MODEL)
_RPB_SIZE = NA_HEADS * (2 * NA_KH - 1) * (2 * NA_KW - 1)


def _pack_sh(cw, ba, bx, lam):
    return jnp.concatenate([cw, ba, bx, lam], axis=0)


def _pad_sh(p):
    pad = [(0, 0)] * (p.ndim - 2) + [(0, SH_ROWS - p.shape[-2]), (0, LANE - p.shape[-1])]
    return jnp.pad(p, pad)


def _pack_rep(cb, wa, wx, gnw, rpb, l1g, l1b, l2g, l2b):
    flat = jnp.concatenate([cb.reshape(-1), wa.reshape(-1), wx.reshape(-1), gnw.reshape(-1),
                            jnp.pad(rpb.reshape(-1), (0, 1920 - _RPB_SIZE)), l1g, l1b, l2g, l2b,
                            jnp.zeros((REP_ROWS * LANE - sum(_REP_SIZES),), F32)])
    return flat.reshape(REP_ROWS, LANE)


def _unpack_rep(p):
    nl = p.shape[0]
    flat = p.reshape(nl, -1)
    out, off = [], 0
    for size in _REP_SIZES:
        out.append(flat[:, off:off + size])
        off += size
    cb, wa, wx, gnw, rpb, l1g, l1b, l2g, l2b = out
    return (cb, wa.reshape(nl, 2, 6, 64, 64), wx.reshape(nl, 2, 6, 64, 64), gnw,
            rpb[:, :_RPB_SIZE].reshape(nl, NA_HEADS, 2 * NA_KH - 1, 2 * NA_KW - 1), l1g, l1b, l2g, l2b)


def _adamw_nd(w, g, m, v, rows, name):
    shp = w.shape
    f = lambda t: t.reshape(-1, shp[-1])
    rows = f(w).shape[0] if rows is None else rows
    return [t.reshape(shp) for t in _adamw_call(f(w), f(g), f(m), f(v), rows, name)]


def kernel(x, w_in, conv_w, conv_b, lru_w_a, lru_b_a, lru_w_x, lru_b_x, lru_lam, ret_gn_w, na_rpb, w_out, ln1_g, ln1_b, w_gate, w_up, w_down, ln2_g, ln2_b, loss_target, m_w_in, m_conv_w, m_conv_b, m_lru_w_a, m_lru_b_a, m_lru_w_x, m_lru_b_x, m_lru_lam, m_ret_gn_w, m_na_rpb, m_w_out, m_ln1_g, m_ln1_b, m_w_gate, m_w_up, m_w_down, m_ln2_g, m_ln2_b, v_w_in, v_conv_w, v_conv_b, v_lru_w_a, v_lru_b_a, v_lru_w_x, v_lru_b_x, v_lru_lam, v_ret_gn_w, v_na_rpb, v_w_out, v_ln1_g, v_ln1_b, v_w_gate, v_w_up, v_w_down, v_ln2_g, v_ln2_b):
    nl = w_in.shape[0]
    T = x.shape[1]
    rows_n = T // GRID_W
    x0, target = x[0], loss_target[0]
    ffpad = W_BLK - FF_BLK

    win_b = w_in.astype(_BF)
    wg_b = jnp.pad(w_gate, ((0, 0), (0, 0), (0, ffpad))).astype(_BF)
    wu_b = jnp.pad(w_up, ((0, 0), (0, 0), (0, ffpad))).astype(_BF)
    wd_b = jnp.pad(w_down, ((0, 0), (0, ffpad), (0, 0))).astype(_BF)
    wout_b = w_out.astype(_BF)
    def agf_start(l, after):
        sh = _pad_sh(_pack_sh(conv_w[l], lru_b_a[l], lru_b_x[l], lru_lam[l]))
        arrs, modes = [win_b[l], sh], ["cols", "all"]
        if l > 0:
            arrs, modes = arrs + [wd_b[l]], modes + ["all"]
        return _comm_start_call(arrs, modes, after, f"agf_start{l}"), modes

    def agk_start(l, after):
        arrs, modes = [wg_b[l], wu_b[l], wout_b[l]], ["cols", "cols", "all"]
        if l == 0:
            arrs, modes = arrs + [wd_b[l]], modes + ["all"]
        return _comm_start_call(arrs, modes, after, f"agk_start{l}"), modes

    tables = _ret_tables(T)
    w4_all = _lru_w4(lru_w_a, lru_w_x)
    layers = []
    gathered = []
    xs, xb = x0, x0.astype(_BF)
    (agf_state, token), agf_modes = agf_start(0, [])
    tie = 0.0 * token[0, 0]
    btabs = [_na_bias_tables(na_rpb[l] + tie, rows_n) for l in range(nl)]
    for l in range(nl):
        front = _comm_wait_call(agf_state, agf_modes, [xb] + (btabs if l == 0 else []), f"agf_wait{l}")
        win, shg = front[0], front[1]
        (agk_state, token), agk_modes = agk_start(l, [shg])
        full = shg[:, :10, :SH_W].transpose(1, 0, 2).reshape(10, LRU_WIDTH)
        vec, w4 = _lru_vec(full[0:4], conv_b[l], full[4:6], full[6:8], full[8:10]), w4_all[l]
        gnw8 = jnp.pad(ret_gn_w[l][None], ((0, SUB - 1), (0, 0)))
        btab = btabs[l]
        proj = _inproj_call(xb, win, token)
        y_lru = _lru_fwd_call(proj, vec, w4)
        y_ret = _ret_fwd_call(proj, tables, gnw8)
        y_na = _na_fwd_call(proj, btab)
        back = _comm_wait_call(agk_state, agk_modes, [y_na], f"agk_wait{l}")
        wg, wu, wout = back[0], back[1], back[2]
        wd = (back[3] if l == 0 else front[2]).reshape(IN_WIDTH, D_MODEL)
        wout = wout.reshape(D_MODEL, D_MODEL)
        gathered.append((win, wg, wu, wd, wout))
        if l + 1 < nl:
            (agf_state, token), agf_modes = agf_start(l + 1, [wout])
        z1, x1b, ycb = _outproj_ln_call(y_lru, y_ret, y_na, xs, wout, ln1_g[l][None], ln1_b[l][None], token)
        z2, x2, x2b, gpb, upb = _ffn_ln_call(z1, x1b, wg, wu, wd, ln2_g[l][None], ln2_b[l][None],
                                             ln1_g[l][None], ln1_b[l][None])
        layers.append(dict(xb=xb, proj=proj, vec=vec, w4=w4, gnw8=gnw8, btab=btab,
                           z1=z1, x1b=x1b, ycb=ycb, z2=z2, gpb=gpb, upb=upb))
        xs, xb = x2, x2b

    dx, loss_blk = _loss_call(xs, target)
    loss = lax.psum(loss_blk[0, 0], ("x", "y", "c"))

    gxa_flags = ["scols", "scols", "blk", "blk"]
    gxb_flags = ["scols", "blk", "all"]
    gxa_state, gxb_state = [None] * nl, [None] * nl
    token = loss_blk
    for l in reversed(range(nl)):
        s = layers[l]
        win, wg, wu, wd, wout = gathered[l]
        dx1, dgp, dup, hid, dz2b, dln2 = _ffn_bwd_call(dx, s["z2"], s["gpb"], s["upb"], wg, wu, wd, ln2_g[l][None], token)
        dwg = _tn_cols_call(s["x1b"], dgp, "tn_cols")
        dwu = _tn_cols_call(s["x1b"], dup, "tn_cols")
        dwd = _tn_rows_call(hid, dz2b, N_BLK, "tn_rows_down").reshape(N_DEV, W_BLK, D_MODEL)
        dz1b, dyc, dres, dln1 = _outproj_bwd_call(dx1, s["z1"], wout, ln1_g[l][None])
        dwout = _tn_rows_call(s["ycb"], dz1b, D_MODEL // 2, "tn_rows_out").reshape(N_DEV, LANE, D_MODEL)
        gxa_state[l], token = _comm_start_call([dwg, dwu, dwd, dwout], gxa_flags, [], f"gxa_start{l}")
        dp, dvec, dw4 = _lru_bwd_call(s["proj"], dyc, s["vec"], s["w4"], token)
        dp, dgnw = _ret_bwd_call(s["proj"], dyc, tables, s["gnw8"], dp)
        dp, dbias = _na_bwd_call(s["proj"], dyc, s["btab"], dp)
        dwin = _tn_cols_call(s["xb"], dp, "tn_cols")
        dx = _inproj_bwd_call(dres, dp, win)
        dcw, dcb, dwa, dba, dwx, dbx, dlam = _lru_unpack(dvec, dw4)
        rep = _pack_rep(dcb, dwa, dwx, dgnw[0], _na_bias_grad(dbias, rows_n), dln1[0], dln1[1], dln2[0], dln2[1])
        sh = _pack_sh(dcw, dba, dbx, dlam).reshape(10, N_DEV, SH_W).transpose(1, 0, 2)
        gxb_state[l], token = _comm_start_call([dwin, _pad_sh(sh), rep], gxb_flags, [], f"gxb_start{l}")

    g_w_in = g_w_gate = g_w_up = g_w_down = g_w_out = g_shp = g_repp = None
    after = [dx, token]
    big = {}
    for l in reversed(range(nl)):
        ra = _comm_wait_call(gxa_state[l], gxa_flags, after, f"gxa_wait{l}")
        g_w_gate = _sum8_call(ra[0], g_w_gate, l, nl, TM, D_MODEL, FF_BLK, "sum8_ff", transposed=True)
        g_w_up = _sum8_call(ra[1], g_w_up, l, nl, TM, D_MODEL, FF_BLK, "sum8_ff", transposed=True)
        g_w_down = _sum8_call(ra[2], g_w_down, l, nl, FF_BLK, FF_BLK, D_MODEL, "sum8_down")
        g_w_out = _sum8_call(ra[3], g_w_out, l, nl, LANE, LANE, D_MODEL, "sum8_out")
        after = [g_w_out]
        if l == 0:
            tr = lambda t: jnp.swapaxes(t, 1, 2)
            big["w_gate"] = [tr(t) for t in _adamw_nd(tr(w_gate), g_w_gate, tr(m_w_gate), tr(v_w_gate), FF_BLK, "adamw_down")]
            big["w_up"] = [tr(t) for t in _adamw_nd(tr(w_up), g_w_up, tr(m_w_up), tr(v_w_up), FF_BLK, "adamw_down")]
            g_w_gate, g_w_up = tr(g_w_gate), tr(g_w_up)
            big["w_down"] = _adamw_nd(w_down, g_w_down, m_w_down, v_w_down, FF_BLK, "adamw_down")
            big["w_out"] = _adamw_nd(w_out, g_w_out, m_w_out, v_w_out, LANE, "adamw_out")
            after = [big[n][k] for n in ("w_gate", "w_up", "w_down", "w_out") for k in range(3)]
        rb = _comm_wait_call(gxb_state[l], gxb_flags, after, f"gxb_wait{l}")
        g_w_in = _sum8_call(rb[0], g_w_in, l, nl, TM, D_MODEL, W_BLK, "sum8_in")
        g_shp = _sum8_call(rb[1], g_shp, l, nl, SH_ROWS, SH_ROWS, LANE, "sum8_sh")
        g_repp = _sum8_call(rb[2], g_repp, l, nl, REP_ROWS, REP_ROWS, LANE, "sum8_rep")
        after = [g_repp]

    big["w_in"] = _adamw_nd(w_in, g_w_in, m_w_in, v_w_in, TM, "adamw_in")
    g_shp = g_shp[:, :, :SH_W]
    rep_names = ("conv_b", "lru_w_a", "lru_w_x", "ret_gn_w", "na_rpb", "ln1_g", "ln1_b", "ln2_g", "ln2_b")
    grads = {"w_in": g_w_in, "w_gate": g_w_gate, "w_up": g_w_up, "w_down": g_w_down, "w_out": g_w_out,
             "conv_w": g_shp[:, 0:4], "lru_b_a": g_shp[:, 4:6], "lru_b_x": g_shp[:, 6:8], "lru_lam": g_shp[:, 8:10]}
    grads.update(dict(zip(rep_names, _unpack_rep(g_repp))))
    small = {
        "conv_w": (conv_w, m_conv_w, v_conv_w), "conv_b": (conv_b, m_conv_b, v_conv_b),
        "lru_w_a": (lru_w_a, m_lru_w_a, v_lru_w_a), "lru_b_a": (lru_b_a, m_lru_b_a, v_lru_b_a),
        "lru_w_x": (lru_w_x, m_lru_w_x, v_lru_w_x), "lru_b_x": (lru_b_x, m_lru_b_x, v_lru_b_x),
        "lru_lam": (lru_lam, m_lru_lam, v_lru_lam), "ret_gn_w": (ret_gn_w, m_ret_gn_w, v_ret_gn_w),
        "na_rpb": (na_rpb, m_na_rpb, v_na_rpb), "ln1_g": (ln1_g, m_ln1_g, v_ln1_g), "ln1_b": (ln1_b, m_ln1_b, v_ln1_b),
        "ln2_g": (ln2_g, m_ln2_g, v_ln2_g), "ln2_b": (ln2_b, m_ln2_b, v_ln2_b),
    }
    for name, (w_, m_, v_) in small.items():
        big[name] = _adamw_nd(w_, grads[name], m_, v_, None, "adamw_small")
    kinds = [{n: big[n][k] for n in big} for k in range(3)]
    order = ("w_in", "conv_w", "conv_b", "lru_w_a", "lru_b_a", "lru_w_x", "lru_b_x", "lru_lam", "ret_gn_w", "na_rpb",
             "w_out", "ln1_g", "ln1_b", "w_gate", "w_up", "w_down", "ln2_g", "ln2_b")
    outs = [loss, dx[None]]
    for d in (grads, *kinds):
        outs.extend(d[n] for n in order)
    return tuple(outs)
```

```python
import functools
import math

import numpy as np
import jax
import jax.numpy as jnp
from jax import lax
from jax.experimental import pallas as pl
from jax.experimental.pallas import tpu as pltpu

F32 = jnp.float32
_BF = jnp.bfloat16

D_MODEL = 1024
DEPTH = 4
GRID_W = 64
HEAD_DIM = 64
LRU_WIDTH = 384
RET_WIDTH = 384
RET_HEADS = 6
NA_WIDTH = 256
NA_HEADS = 4
IN_WIDTH = 3072
CONV_WIDTH = 4
LRU_C = 8.0
RET_CHUNK = 128
ROPE_BASE = 10000.0
GN_EPS = 1e-6
NA_KH = 8
NA_KW = 16
D_FF = 2816
FF_BLK = 352
N_DEV = 8
ALPHA = (2 * DEPTH) ** 0.25
LN_EPS = 1e-5
ADAM_LR = 0.001
ADAM_B1 = 0.9
ADAM_B2 = 0.999
ADAM_EPS = 1e-08
ADAM_WD = 0.01
ADAM_STEP = 10

LANE = 128
SUB = 8
VMEM_MB = 56
NEG = -1e30

MESH = pl.DeviceIdType.MESH


def _cparams(sem=None, vmem_mb=VMEM_MB):
    return pltpu.CompilerParams(dimension_semantics=sem, vmem_limit_bytes=vmem_mb << 20)


def _mm(a, b):
    return jnp.dot(a.astype(_BF), b.astype(_BF), preferred_element_type=F32)


def _mm_nt(a, b):
    return lax.dot_general(a.astype(_BF), b.astype(_BF), (((1,), (1,)), ((), ())), preferred_element_type=F32)


def _mm_tn(a, b):
    return lax.dot_general(a.astype(_BF), b.astype(_BF), (((0,), (0,)), ((), ())), preferred_element_type=F32)


def _sigmoid(x):
    return jax.nn.sigmoid(x)


def _rows(start, size):
    return pl.ds(pl.multiple_of(start, SUB), size)


def _loop2(n, body, init):
    assert n % 2 == 0
    return lax.fori_loop(0, n // 2, lambda i, c: body(2 * i + 1, body(2 * i, c)), init)


def _strip(T, col, buffers=2):
    return pl.BlockSpec((T, LANE), lambda j: (0, col(j)), pipeline_mode=pl.Buffered(buffers))


LRU_CH = 1024
_GELU_C0 = math.sqrt(2.0 / math.pi)
_GELU_C1 = 0.044715


def _gelu_parts(x):
    x2 = x * x
    t = jnp.tanh(_GELU_C0 * (x + _GELU_C1 * x * x2))
    val = 0.5 * x * (1.0 + t)
    der = 0.5 * (1.0 + t) + 0.5 * x * (1.0 - t * t) * _GELU_C0 * (1.0 + 3.0 * _GELU_C1 * x2)
    return val, der


def _softplus_neg(lam):
    e = jnp.exp(-jnp.abs(lam))
    w = 1.0 + e
    l1p = jnp.where(w == 1.0, e, jnp.log(w) * (e / jnp.where(w == 1.0, 1.0, w - 1.0)))
    return jnp.maximum(-lam, 0.0) + l1p


def _window(ref, t0, ch, T):
    prev = ref[_rows(jnp.maximum(t0 - SUB, 0), SUB), :].astype(F32)
    nxt = ref[_rows(jnp.minimum(t0 + ch, T - SUB), SUB), :].astype(F32)
    prev = jnp.where(t0 > 0, prev, 0.0)
    nxt = jnp.where(t0 + ch < T, nxt, 0.0)
    return jnp.concatenate([prev, ref[_rows(t0, ch), :].astype(F32), nxt], axis=0)


def _tap(win, shift, ch):
    n = win.shape[0]
    return pltpu.roll(win, (-shift) % n, 0)[SUB:SUB + ch]


def _lru_conv(xb_ref, vec, t0, T):
    win = _window(xb_ref, t0, LRU_CH, T)
    xc = jnp.broadcast_to(vec[4:5, :], (LRU_CH, LANE))
    for j in range(CONV_WIDTH):
        xc = xc + _tap(win, j - CONV_WIDTH // 2, LRU_CH) * vec[j:j + 1, :]
    return xc


def _lru_dir(pre_a, pre_x, sp):
    r = _sigmoid(pre_a)
    i = _sigmoid(pre_x)
    log_a = (-LRU_C) * r * sp
    a = jnp.exp(log_a)
    z = jnp.tanh(-log_a) * (a * a + 1.0)
    s = jnp.sqrt(z)
    return r, i, a, s


def _scan_tile(a, b, reverse, row):
    for k in (1, 2, 4):
        if not reverse:
            a_s, b_s, m = pltpu.roll(a, k, 0), pltpu.roll(b, k, 0), row >= k
        else:
            a_s, b_s, m = pltpu.roll(a, SUB - k, 0), pltpu.roll(b, SUB - k, 0), row < SUB - k
        b = jnp.where(m, a * b_s + b, b)
        a = jnp.where(m, a * a_s, a)
    return a, b


def _bcast_row(x, r):
    return jnp.broadcast_to(x[r:r + 1, :], (SUB, LANE))


def _lru_prepare(xb_ref, w4_ref, vec, xc_ref, af_ref, uf_ref, ab_ref, ub_ref, T):
    sp_f = _softplus_neg(vec[9:10, :])
    sp_b = _softplus_neg(vec[10:11, :])
    w4 = w4_ref[0]

    def body(c, carry):
        t0 = c * LRU_CH
        xc = _lru_conv(xb_ref, vec, t0, T)
        if xc_ref is not None:
            xc_ref[_rows(t0, LRU_CH), :] = xc
        pre = _mm(xc, w4)
        _, i, a, s = _lru_dir(pre[:, 0:128] + vec[5:6, :], pre[:, 128:256] + vec[6:7, :], sp_f)
        af_ref[_rows(t0, LRU_CH), :] = a
        uf_ref[_rows(t0, LRU_CH), :] = s * (i * xc)
        _, i, a, s = _lru_dir(pre[:, 256:384] + vec[7:8, :], pre[:, 384:512] + vec[8:9, :], sp_b)
        ab_ref[_rows(t0, LRU_CH), :] = a
        ub_ref[_rows(t0, LRU_CH), :] = s * (i * xc)
        return carry

    lax.fori_loop(0, T // LRU_CH, body, 0)


def _lru_scan(af_ref, uf_ref, ab_ref, ub_ref, T):
    nt = T // SUB
    row = lax.broadcasted_iota(jnp.int32, (SUB, LANE), 0)

    def body(j, carry):
        hf, hb = carry
        sf = _rows(j * SUB, SUB)
        sb = _rows((nt - 1 - j) * SUB, SUB)
        a, b = _scan_tile(af_ref[sf, :], uf_ref[sf, :], False, row)
        h = a * hf + b
        uf_ref[sf, :] = h
        hf = _bcast_row(h, SUB - 1)
        a, b = _scan_tile(ab_ref[sb, :], ub_ref[sb, :], True, row)
        h = a * hb + b
        ub_ref[sb, :] = h
        hb = _bcast_row(h, 0)
        return hf, hb

    z = jnp.zeros((SUB, LANE), F32)
    lax.fori_loop(0, nt, body, (z, z))


def _lru_fwd_call(proj, vec, w4):
    T = proj.shape[0]

    def body(xb_ref, gate_ref, vec_ref, w4_ref, y_ref, af_ref, uf_ref, ab_ref, ub_ref):
        vec = vec_ref[...]
        _lru_prepare(xb_ref, w4_ref, vec, None, af_ref, uf_ref, ab_ref, ub_ref, T)
        _lru_scan(af_ref, uf_ref, ab_ref, ub_ref, T)

        def out(c, carry):
            rows = _rows(c * LRU_CH, LRU_CH)
            gl, _ = _gelu_parts(gate_ref[rows, :])
            y_ref[rows, :] = ((uf_ref[rows, :] + ub_ref[rows, :]) * gl).astype(y_ref.dtype)
            return carry

        lax.fori_loop(0, T // LRU_CH, out, 0)

    return pl.pallas_call(
        body, name="lru_fwd", grid=(LRU_WIDTH // LANE,),
        in_specs=[_strip(T, lambda j: j), _strip(T, lambda j: j + 3),
                  pl.BlockSpec((16, LANE), lambda j: (0, j)),
                  pl.BlockSpec((1, LANE, 4 * LANE), lambda j: (j, 0, 0))],
        out_specs=_strip(T, lambda j: j, buffers=1),
        out_shape=jax.ShapeDtypeStruct((T, LRU_WIDTH), _BF),
        scratch_shapes=[pltpu.VMEM((T, LANE), F32)] * 4,
        compiler_params=_cparams(("arbitrary",)),
    )(proj, proj, vec, w4)


def _store_strips(stage_ref, dp_ref, cols, sems):
    copies = [pltpu.make_async_copy(stage_ref.at[b], dp_ref.at[:, pl.ds(pl.multiple_of(c * LANE, LANE), LANE)], sems.at[b])
              for b, c in enumerate(cols)]
    for cp in copies:
        cp.start()
    for cp in copies:
        cp.wait()


def _lru_bwd_call(proj, dycat, vec, w4, after):
    T = proj.shape[0]
    nt = T // SUB
    nch = T // LRU_CH

    def body(xb_ref, gate_ref, dy_ref, vec_ref, w4_ref, after_ref, dp_ref, dvec_ref, dw4_ref,
             xc_ref, af_ref, hf_ref, ab_ref, hb_ref, dh_ref, stage_ref, sems):
        dxb_ref, dgate_ref = stage_ref.at[0], stage_ref.at[1]
        vec = vec_ref[...]
        _lru_prepare(xb_ref, w4_ref, vec, xc_ref, af_ref, hf_ref, ab_ref, hb_ref, T)
        _lru_scan(af_ref, hf_ref, ab_ref, hb_ref, T)

        def gate_bwd(c, carry):
            rows = _rows(c * LRU_CH, LRU_CH)
            gl, dgl = _gelu_parts(gate_ref[rows, :])
            dy = dy_ref[rows, :]
            dgate_ref[rows, :] = (dy * (hf_ref[rows, :] + hb_ref[rows, :]) * dgl).astype(dgate_ref.dtype)
            dh_ref[rows, :] = dy * gl
            return carry

        lax.fori_loop(0, nch, gate_bwd, 0)

        row = lax.broadcasted_iota(jnp.int32, (SUB, LANE), 0)

        def adj(j, carry):
            gf, a_next, gb, a_prev = carry
            tf = nt - 1 - j
            sf = _rows(tf * SUB, SUB)
            a_t = af_ref[sf, :]
            h_t = hf_ref[sf, :]
            coef = jnp.where(row == SUB - 1, a_next, pltpu.roll(a_t, SUB - 1, 0))
            ac, bc = _scan_tile(coef, dh_ref[sf, :], True, row)
            g = ac * gf + bc
            h_prev = hf_ref[_rows(jnp.maximum(tf - 1, 0) * SUB, SUB), :]
            h_prev = jnp.where(tf > 0, _bcast_row(h_prev, SUB - 1), 0.0)
            hs = jnp.where(row == 0, h_prev, pltpu.roll(h_t, 1, 0))
            af_ref[sf, :] = g * hs
            hf_ref[sf, :] = g
            gf = _bcast_row(g, 0)
            a_next = _bcast_row(a_t, 0)
            sb = _rows(j * SUB, SUB)
            a_t = ab_ref[sb, :]
            h_t = hb_ref[sb, :]
            coef = jnp.where(row == 0, a_prev, pltpu.roll(a_t, 1, 0))
            ac, bc = _scan_tile(coef, dh_ref[sb, :], False, row)
            g = ac * gb + bc
            h_next = hb_ref[_rows(jnp.minimum(j + 1, nt - 1) * SUB, SUB), :]
            h_next = jnp.where(j < nt - 1, _bcast_row(h_next, 0), 0.0)
            hs = jnp.where(row == SUB - 1, h_next, pltpu.roll(h_t, SUB - 1, 0))
            ab_ref[sb, :] = g * hs
            hb_ref[sb, :] = g
            gb = _bcast_row(g, SUB - 1)
            a_prev = _bcast_row(a_t, SUB - 1)
            return gf, a_next, gb, a_prev

        z = jnp.zeros((SUB, LANE), F32)
        lax.fori_loop(0, nt, adj, (z, z, z, z))

        sp_f = _softplus_neg(vec[9:10, :])
        sp_b = _softplus_neg(vec[10:11, :])
        w4 = w4_ref[0]
        dw4_ref[...] = jnp.zeros_like(dw4_ref)

        def one_dir(pre_a, pre_x, sp, xc, du, da):
            r, i, a, s = _lru_dir(pre_a, pre_x, sp)
            d_i = du * s * xc
            dxc = du * s * i
            d_s = du * i * xc
            d_log = da * a - d_s * (a * a) / s
            d_r = d_log * (-LRU_C) * sp
            d_sp = jnp.sum(d_log * (-LRU_C) * r, axis=0, keepdims=True)
            return d_r * r * (1.0 - r), d_i * i * (1.0 - i), dxc, d_sp

        def gates_bwd(c, carry):
            db, dspf, dspb = carry
            rows = _rows(c * LRU_CH, LRU_CH)
            xc = xc_ref[rows, :]
            pre = _mm(xc, w4)
            dpa_f, dpx_f, dxc_f, d_sp_f = one_dir(pre[:, 0:128] + vec[5:6, :], pre[:, 128:256] + vec[6:7, :],
                                                  sp_f, xc, hf_ref[rows, :], af_ref[rows, :])
            dpa_b, dpx_b, dxc_b, d_sp_b = one_dir(pre[:, 256:384] + vec[7:8, :], pre[:, 384:512] + vec[8:9, :],
                                                  sp_b, xc, hb_ref[rows, :], ab_ref[rows, :])
            dpre = jnp.concatenate([dpa_f, dpx_f, dpa_b, dpx_b], axis=1)
            dw4_ref[0] += _mm_tn(xc, dpre)
            dh_ref[rows, :] = dxc_f + dxc_b + _mm_nt(dpre, w4)
            return db + jnp.sum(dpre, axis=0, keepdims=True), dspf + d_sp_f, dspb + d_sp_b

        z1 = jnp.zeros((1, LANE), F32)
        db, dspf, dspb = lax.fori_loop(0, nch, gates_bwd, (jnp.zeros((1, 4 * LANE), F32), z1, z1))

        def conv_bwd(c, carry):
            t0 = c * LRU_CH
            rows = _rows(t0, LRU_CH)
            dwin = _window(dh_ref, t0, LRU_CH, T)
            xwin = _window(xb_ref, t0, LRU_CH, T)
            dxc = dh_ref[rows, :]
            dxb = jnp.zeros((LRU_CH, LANE), F32)
            out = []
            for j in range(CONV_WIDTH):
                off = j - CONV_WIDTH // 2
                dxb = dxb + _tap(dwin, -off, LRU_CH) * vec[j:j + 1, :]
                out.append(carry[j] + jnp.sum(dxc * _tap(xwin, off, LRU_CH), axis=0, keepdims=True))
            dxb_ref[rows, :] = dxb.astype(dxb_ref.dtype)
            out.append(carry[CONV_WIDTH] + jnp.sum(dxc, axis=0, keepdims=True))
            return tuple(out)

        dconv = lax.fori_loop(0, nch, conv_bwd, (z1,) * (CONV_WIDTH + 1))
        dlam_f = dspf * (-_sigmoid(-vec[9:10, :]))
        dlam_b = dspb * (-_sigmoid(-vec[10:11, :]))
        dvec_ref[...] = jnp.concatenate(
            list(dconv) + [db[:, 0:128], db[:, 128:256], db[:, 256:384], db[:, 384:512], dlam_f, dlam_b,
                           jnp.zeros((5, LANE), F32)], axis=0)
        j = pl.program_id(0)
        _store_strips(stage_ref, dp_ref, (j, j + 3), sems)

    ns = LRU_WIDTH // LANE
    return pl.pallas_call(
        body, name="lru_bwd", grid=(ns,),
        in_specs=[_strip(T, lambda j: j), _strip(T, lambda j: j + 3), _strip(T, lambda j: j),
                  pl.BlockSpec((16, LANE), lambda j: (0, j)),
                  pl.BlockSpec((1, LANE, 4 * LANE), lambda j: (j, 0, 0)),
                  pl.BlockSpec(memory_space=pl.ANY)],
        out_specs=[pl.BlockSpec(memory_space=pl.ANY),
                   pl.BlockSpec((16, LANE), lambda j: (0, j)),
                   pl.BlockSpec((1, LANE, 4 * LANE), lambda j: (j, 0, 0))],
        out_shape=[jax.ShapeDtypeStruct((T, IN_WIDTH), _BF),
                   jax.ShapeDtypeStruct((16, LRU_WIDTH), F32), jax.ShapeDtypeStruct((ns, LANE, 4 * LANE), F32)],
        scratch_shapes=[pltpu.VMEM((T, LANE), F32)] * 6 + [pltpu.VMEM((2, T, LANE), _BF), pltpu.SemaphoreType.DMA((2,))],
        compiler_params=_cparams(("arbitrary",)),
    )(proj, proj, dycat, vec, w4, after)


def _lru_vec(cw, cb, ba, bx, lam):
    return jnp.concatenate([cw, cb[None], ba[0:1], bx[0:1], ba[1:2], bx[1:2], lam, jnp.zeros((5, LRU_WIDTH), F32)], axis=0)


def _lru_w4(wa, wx):
    nl = wa.shape[0]
    w = jnp.stack([wa[:, 0], wx[:, 0], wa[:, 1], wx[:, 1]], axis=1)
    w = w.reshape(nl, 4, 3, 2, 64, 64)
    eye = jnp.eye(2, dtype=w.dtype)
    bd = w[:, :, :, :, :, None, :] * eye[None, None, None, :, None, :, None]
    bd = bd.reshape(nl, 4, 3, LANE, LANE)
    return bd.transpose(0, 2, 3, 1, 4).reshape(nl, 3, LANE, 4 * LANE).astype(_BF)


def _lru_unpack(dvec, dw4):
    def blocks(m):
        m = m.reshape(3, 2, 64, 2, 64)
        return jnp.stack([m[:, 0, :, 0, :], m[:, 1, :, 1, :]], axis=1).reshape(6, 64, 64)
    parts = [blocks(dw4[:, :, k * LANE:(k + 1) * LANE]) for k in range(4)]
    dwa = jnp.stack([parts[0], parts[2]])
    dwx = jnp.stack([parts[1], parts[3]])
    dba = jnp.stack([dvec[5], dvec[7]])
    dbx = jnp.stack([dvec[6], dvec[8]])
    return dvec[0:4], dvec[4], dwa, dba, dwx, dbx, dvec[9:11]


RC = 2 * RET_CHUNK


def _ret_tables(T):
    half = HEAD_DIM // 2
    pos = jnp.arange(T, dtype=F32)
    inv_freq = ROPE_BASE ** (-jnp.arange(half, dtype=F32) / half)
    ang = pos[:, None] * inv_freq[None, :]
    cos = jnp.tile(jnp.cos(ang), (1, 4))
    sin = jnp.tile(jnp.concatenate([-jnp.sin(ang), jnp.sin(ang)], axis=1), (1, 2))
    log_g = jnp.log1p(-jnp.exp2(-5.0 - jnp.arange(RET_HEADS, dtype=F32)))
    idx = jnp.arange(RC, dtype=F32)
    dec = jnp.exp(jnp.abs(idx[:, None] - idx[None, :]) * log_g[:, None, None])
    lg = jnp.repeat(log_g, HEAD_DIM).reshape(3, 1, LANE)
    col = idx[None, :, None]
    rtab = jnp.stack([jnp.exp((RC - 1 - col) * lg), jnp.exp(col * lg),
                      jnp.exp((col + 1.0) * lg), jnp.exp((RC - col) * lg)], axis=1)
    gch = jnp.broadcast_to(jnp.exp(RC * lg), (3, SUB, LANE))
    return cos, sin, dec, rtab, gch


def _swap32(x, lane):
    return jnp.where((lane & 32) == 0, pltpu.roll(x, LANE - 32, 1), pltpu.roll(x, 32, 1))


def _head_mean(x, m0, m1):
    s0 = jnp.sum(x * m0, axis=-1, keepdims=True)
    s1 = jnp.sum(x * m1, axis=-1, keepdims=True)
    return (s0 * m0 + s1 * m1) * (1.0 / HEAD_DIM)


def _ret_masks():
    lane = lax.broadcasted_iota(jnp.int32, (RC, LANE), 1)
    m0 = (lane < HEAD_DIM).astype(F32)
    r = lax.broadcasted_iota(jnp.int32, (LANE, LANE), 0) // HEAD_DIM
    c = lax.broadcasted_iota(jnp.int32, (LANE, LANE), 1) // HEAD_DIM
    return lane, m0, 1.0 - m0, (r == c).astype(F32)


def _ret_specs(T):
    const = lambda shape, imap: pl.BlockSpec(shape, imap)
    return [_strip(T, lambda j: j + 6), _strip(T, lambda j: j + 9), _strip(T, lambda j: j + 12),
            _strip(T, lambda j: j + 15),
            pl.BlockSpec((T, LANE), lambda j: (0, 0), pipeline_mode=pl.Buffered(1)),
            pl.BlockSpec((T, LANE), lambda j: (0, 0), pipeline_mode=pl.Buffered(1)),
            const((2, RC, RC), lambda j: (j, 0, 0)),
            const((1, 4, RC, LANE), lambda j: (j, 0, 0, 0)),
            const((1, SUB, LANE), lambda j: (j, 0, 0)),
            const((SUB, LANE), lambda j: (0, j))]


def _ret_fwd_call(proj, tables, gnw8):
    T = proj.shape[0]
    nc = T // RC
    cos, sin, dec, rtab, gch = tables

    def body(q_ref, k_ref, v_ref, g_ref, cos_ref, sin_ref, dec_ref, rtab_ref, gch_ref, gnw_ref, y_ref, stf_ref, kr_ref):
        lane, m0, m1, bd = _ret_masks()
        gch_v = gch_ref[0][0:1, :]
        gnw = gnw_ref[0:1, :]
        dkf, dkb, dqf, dqb = rtab_ref[0, 0], rtab_ref[0, 1], rtab_ref[0, 2], rtab_ref[0, 3]

        def rope(x, rows):
            return x * cos_ref[rows, :] + _swap32(x, lane) * sin_ref[rows, :]

        def pass_a(n, st):
            rows = _rows(n * RC, RC)
            stf_ref[n] = st
            kr = rope(k_ref[rows, :], rows) * (HEAD_DIM ** -0.5)
            kr_ref[rows, :] = kr
            return gch_v * st + _mm_tn(kr * dkf, v_ref[rows, :]) * bd

        _loop2(nc, pass_a, jnp.zeros((LANE, LANE), F32))

        def pass_b(i, stb):
            ns = [nc - 1 - 2 * i, nc - 2 - 2 * i]
            rows = [_rows(n * RC, RC) for n in ns]
            heads = ((0, m0), (1, m1))
            qr = [rope(q_ref[r, :], r) for r in rows]
            kr = [kr_ref[r, :] for r in rows]
            v = [v_ref[r, :] for r in rows]
            kv = [_mm_tn(kr[c] * dkb, v[c]) * bd for c in range(2)]
            stbs = [stb, gch_v * stb + kv[0]]
            s = [[_mm_nt(qr[c] * m, kr[c]) * dec_ref[h] for h, m in heads] for c in range(2)]
            o = [_mm(qr[c] * dqf, stf_ref[ns[c]]) + _mm(qr[c] * dqb, stbs[c]) for c in range(2)]
            o = [o[c] + _mm(s[c][0], v[c] * m0) + _mm(s[c][1], v[c] * m1) for c in range(2)]
            oc = [o_ - _head_mean(o_, m0, m1) for o_ in o]
            on = [oc_ * lax.rsqrt(_head_mean(oc_ * oc_, m0, m1) + GN_EPS) for oc_ in oc]
            for c in range(2):
                g = g_ref[rows[c], :]
                y_ref[rows[c], :] = ((g * _sigmoid(g)) * (on[c] * gnw)).astype(y_ref.dtype)
            return gch_v * stbs[1] + kv[1]

        assert nc % 2 == 0
        lax.fori_loop(0, nc // 2, pass_b, jnp.zeros((LANE, LANE), F32))

    return pl.pallas_call(
        body, name="ret_fwd", grid=(RET_WIDTH // LANE,),
        in_specs=_ret_specs(T),
        out_specs=_strip(T, lambda j: j, buffers=1),
        out_shape=jax.ShapeDtypeStruct((T, RET_WIDTH), _BF),
        scratch_shapes=[pltpu.VMEM((nc, LANE, LANE), F32), pltpu.VMEM((T, LANE), F32)],
        compiler_params=_cparams(("arbitrary",)),
    )(proj, proj, proj, proj, cos, sin, dec, rtab, gch, gnw8)


def _ret_bwd_call(proj, dycat, tables, gnw8, dp):
    T = proj.shape[0]
    nc = T // RC
    cos, sin, dec, rtab, gch = tables

    def body(q_ref, k_ref, v_ref, g_ref, cos_ref, sin_ref, dec_ref, rtab_ref, gch_ref, gnw_ref, dy_ref, dp_in_ref,
             dp_out_ref, dgnw_ref, stf_ref, dstb_ref, dkr_ref, dv_ref, dp_ref, kr_ref, sems):
        lane, m0, m1, bd = _ret_masks()
        gch_v = gch_ref[0][0:1, :]
        gnw = gnw_ref[0:1, :]
        dkf, dkb, dqf, dqb = rtab_ref[0, 0], rtab_ref[0, 1], rtab_ref[0, 2], rtab_ref[0, 3]
        scale = HEAD_DIM ** -0.5
        zst = jnp.zeros((LANE, LANE), F32)

        def rope(x, rows):
            return x * cos_ref[rows, :] + _swap32(x, lane) * sin_ref[rows, :]

        def rope_t(d, rows):
            return d * cos_ref[rows, :] + _swap32(d * sin_ref[rows, :], lane)

        def pass_a(n, st):
            rows = _rows(n * RC, RC)
            stf_ref[n] = st
            kr = rope(k_ref[rows, :], rows) * scale
            kr_ref[rows, :] = kr
            return gch_v * st + _mm_tn(kr * dkf, v_ref[rows, :]) * bd

        _loop2(nc, pass_a, zst)

        def pass_b(i, carry):
            stb, d_f, dgnw = carry
            two = range(2)
            heads = ((0, m0), (1, m1))
            ns = [nc - 1 - 2 * i, nc - 2 - 2 * i]
            rows = [_rows(n * RC, RC) for n in ns]
            qr = [rope(q_ref[r, :], r) for r in rows]
            kr = [kr_ref[r, :] for r in rows]
            v = [v_ref[r, :] for r in rows]
            stf = [stf_ref[n] for n in ns]
            kvb = [_mm_tn(kr[c] * dkb, v[c]) * bd for c in two]
            stbs = [stb, gch_v * stb + kvb[0]]
            qf = [qr[c] * dqf for c in two]
            qb = [qr[c] * dqb for c in two]
            s = [[_mm_nt(qr[c] * m, kr[c]) * dec_ref[h] for h, m in heads] for c in two]
            o = [_mm(qf[c], stf[c]) + _mm(qb[c], stbs[c]) for c in two]
            o = [o[c] + _mm(s[c][0], v[c] * m0) + _mm(s[c][1], v[c] * m1) for c in two]
            oc = [o_ - _head_mean(o_, m0, m1) for o_ in o]
            rstd = [lax.rsqrt(_head_mean(oc_ * oc_, m0, m1) + GN_EPS) for oc_ in oc]
            on = [oc[c] * rstd[c] for c in two]
            do = []
            for c in two:
                g = g_ref[rows[c], :]
                sg = _sigmoid(g)
                dy = dy_ref[rows[c], :]
                dp_ref[3, rows[c], :] = (dy * (on[c] * gnw) * (sg * (1.0 + g * (1.0 - sg)))).astype(dp_ref.dtype)
                t = dy * (g * sg)
                dgnw = dgnw + jnp.sum(t * on[c], axis=0, keepdims=True)
                don = t * gnw
                do.append(rstd[c] * (don - _head_mean(don, m0, m1) - on[c] * _head_mean(don * on[c], m0, m1)))
            dstf = [_mm_tn(qf[c], do[c]) * bd for c in two]
            dfs = [d_f, dstf[0] + gch_v * d_f]
            ds = [[_mm_nt(do[c] * m, v[c]) * dec_ref[h] for h, m in heads] for c in two]
            dqr = [_mm_nt(do[c], stf[c]) * dqf + _mm_nt(do[c], stbs[c]) * dqb
                   + _mm(ds[c][0], kr[c] * m0) + _mm(ds[c][1], kr[c] * m1) for c in two]
            dkr = [_mm_nt(v[c], dfs[c]) * dkf + _mm_tn(ds[c][0], qr[c] * m0) + _mm_tn(ds[c][1], qr[c] * m1) for c in two]
            dv = [_mm(kr[c] * dkf, dfs[c]) + _mm_tn(s[c][0], do[c] * m0) + _mm_tn(s[c][1], do[c] * m1) for c in two]
            for c in two:
                dp_ref[0, rows[c], :] = rope_t(dqr[c], rows[c]).astype(dp_ref.dtype)
                dkr_ref[rows[c], :] = dkr[c]
                dv_ref[rows[c], :] = dv[c]
                dstb_ref[ns[c]] = _mm_tn(qb[c], do[c]) * bd
            return gch_v * stbs[1] + kvb[1], dstf[1] + gch_v * dfs[1], dgnw

        assert nc % 2 == 0
        _, _, dgnw = lax.fori_loop(0, nc // 2, pass_b, (zst, zst, jnp.zeros((1, LANE), F32)))
        dgnw_ref[...] = jnp.concatenate([dgnw, jnp.zeros((SUB - 1, LANE), F32)], axis=0)

        def pass_c(n, d_b):
            rows = _rows(n * RC, RC)
            kr = kr_ref[rows, :]
            v = v_ref[rows, :]
            dkr = dkr_ref[rows, :] + _mm_nt(v, d_b) * dkb
            dp_ref[1, rows, :] = (rope_t(dkr, rows) * scale).astype(dp_ref.dtype)
            dp_ref[2, rows, :] = (dv_ref[rows, :] + _mm(kr * dkb, d_b)).astype(dp_ref.dtype)
            return dstb_ref[n] + gch_v * d_b

        _loop2(nc, pass_c, zst)
        j = pl.program_id(0)
        _store_strips(dp_ref, dp_out_ref, (j + 6, j + 9, j + 12, j + 15), sems)

    n_in = len(_ret_specs(T)) + 1
    return pl.pallas_call(
        body, name="ret_bwd", grid=(RET_WIDTH // LANE,),
        in_specs=_ret_specs(T) + [_strip(T, lambda j: j + 3), pl.BlockSpec(memory_space=pl.ANY)],
        out_specs=[pl.BlockSpec(memory_space=pl.ANY), pl.BlockSpec((SUB, LANE), lambda j: (0, j))],
        out_shape=[jax.ShapeDtypeStruct(dp.shape, dp.dtype), jax.ShapeDtypeStruct((SUB, RET_WIDTH), F32)],
        scratch_shapes=[pltpu.VMEM((nc, LANE, LANE), F32), pltpu.VMEM((nc, LANE, LANE), F32),
                        pltpu.VMEM((T, LANE), F32), pltpu.VMEM((T, LANE), F32),
                        pltpu.VMEM((4, T, LANE), _BF), pltpu.VMEM((T, LANE), F32), pltpu.SemaphoreType.DMA((4,))],
        input_output_aliases={n_in: 0},
        compiler_params=_cparams(("arbitrary",)),
    )(proj, proj, proj, proj, cos, sin, dec, rtab, gch, gnw8, dycat, dp)


NA_Q = 2 * GRID_W
NA_WROWS = 10
NA_K = NA_WROWS * GRID_W
NA_CHUNKS = NA_K // LANE
NA_UNROLL = 4
NA_UNROLL_FWD = 8
NA_TYPES = 5
_ONEHOT_PRECISION = lax.Precision.HIGH


def _na_onehots(rows_n):
    reps = [(0, 0), (2, 0), (4, 0), (rows_n - 4, rows_n - NA_WROWS), (rows_n - 2, rows_n - NA_WROWS)]
    rm = np.zeros((NA_TYPES, 2, NA_WROWS, 2 * NA_KH - 1), np.float32)
    for t, (r, ws) in enumerate(reps):
        for qh in range(2):
            qrow = r + qh
            rstart = min(max(qrow - NA_KH // 2, 0), rows_n - NA_KH)
            for kh in range(NA_WROWS):
                krow = ws + kh
                if rstart <= krow < rstart + NA_KH:
                    rm[t, qh, kh, krow - qrow + NA_KH - 1] = 1.0
    cm = np.zeros((GRID_W, GRID_W, 2 * NA_KW - 1), np.float32)
    for qc in range(GRID_W):
        cstart = min(max(qc - NA_KW // 2, 0), GRID_W - NA_KW)
        for kc in range(cstart, cstart + NA_KW):
            cm[qc, kc, kc - qc + NA_KW - 1] = 1.0
    rm2 = rm.reshape(NA_TYPES, 2, NA_CHUNKS, 2, 2 * NA_KH - 1)
    cm2 = np.zeros((GRID_W, LANE, 2, 2 * NA_KW - 1), np.float32)
    for z in range(2):
        cm2[:, z * GRID_W:(z + 1) * GRID_W, z, :] = cm
    return rm2, cm2


def _na_bias_tables(rpb, rows_n):
    rm, cm = _na_onehots(rows_n)
    val = jnp.einsum("hab,tqpza,xkzb->htpqxk", rpb, rm, cm, precision=_ONEHOT_PRECISION)
    valid = np.einsum("tqpz,xkz->tpqxk", rm.sum(-1), cm.sum(-1)) > 0.5
    return jnp.where(valid[None], val, NEG).reshape(2, 2, NA_TYPES, NA_CHUNKS, NA_Q, LANE)


def _na_bias_grad(dtab, rows_n):
    rm, cm = _na_onehots(rows_n)
    d6 = dtab.reshape(NA_HEADS, NA_TYPES, NA_CHUNKS, 2, GRID_W, LANE)
    return jnp.einsum("htpqxk,tqpza,xkzb->hab", d6, rm, cm, precision=_ONEHOT_PRECISION)


def _na_bias(b_ref, h, typ):
    return jnp.concatenate([b_ref[0, h, typ, c] for c in range(NA_CHUNKS)], axis=1)


def _na_step(p, npairs, rows_n):
    ws = jnp.clip(2 * p - NA_KH // 2, 0, rows_n - NA_WROWS)
    koff = pl.multiple_of(ws * GRID_W, LANE)
    typ = jnp.where(p == 0, 0, jnp.where(p == 1, 1, jnp.where(p == npairs - 2, 3, jnp.where(p == npairs - 1, 4, 2))))
    return _rows(p * NA_Q, NA_Q), pl.ds(koff, NA_K), typ


def _na_fwd_call(proj, btab):
    T = proj.shape[0]
    npairs, rows_n = T // NA_Q, T // GRID_W

    def body(q_ref, k_ref, v_ref, b_ref, o_ref):
        lane = lax.broadcasted_iota(jnp.int32, (NA_Q, LANE), 1)
        m0 = (lane < HEAD_DIM).astype(F32)
        m1 = 1.0 - m0

        def steps(i, carry):
            idx = [_na_step(unroll * i + u, npairs, rows_n) for u in range(unroll)]
            chains = [(u, h, m) for u in range(unroll) for h, m in ((0, m0), (1, m1))]
            kws = [k_ref[krows, :].astype(_BF) for _, krows, _ in idx]
            vws = [v_ref[krows, :].astype(_BF) for _, krows, _ in idx]
            s = [_mm_nt(q_ref[idx[u][0], :] * m, kws[u]) for u, h, m in chains]
            s = [s_ * (HEAD_DIM ** -0.5) + _na_bias(b_ref, h, idx[u][2]) for s_, (u, h, m) in zip(s, chains)]
            e = [jnp.exp(s_ - jnp.max(s_, axis=-1, keepdims=True)) for s_ in s]
            pr = [e_ / jnp.sum(e_, axis=-1, keepdims=True) for e_ in e]
            ov = [_mm(pr_, vws[u]) * m for pr_, (u, h, m) in zip(pr, chains)]
            for u in range(unroll):
                o_ref[idx[u][0], :] = (ov[2 * u] + ov[2 * u + 1]).astype(o_ref.dtype)
            return carry

        unroll = NA_UNROLL_FWD if npairs % NA_UNROLL_FWD == 0 else NA_UNROLL
        lax.fori_loop(0, npairs // unroll, steps, 0)

    return pl.pallas_call(
        body, name="na_fwd", grid=(NA_WIDTH // LANE,),
        in_specs=[_strip(T, lambda j: j + 18), _strip(T, lambda j: j + 20), _strip(T, lambda j: j + 22),
                  pl.BlockSpec((1, 2, NA_TYPES, NA_CHUNKS, NA_Q, LANE), lambda j: (j, 0, 0, 0, 0, 0))],
        out_specs=_strip(T, lambda j: j, buffers=1),
        out_shape=jax.ShapeDtypeStruct((T, NA_WIDTH), _BF),
        compiler_params=_cparams(("arbitrary",)),
    )(proj, proj, proj, btab)


def _na_bwd_call(proj, dycat, btab, dp):
    T = proj.shape[0]
    npairs, rows_n = T // NA_Q, T // GRID_W
    scale = HEAD_DIM ** -0.5

    def body(q_ref, k_ref, v_ref, do_ref, b_ref, dp_in_ref, dp_out_ref, db_ref, dka_ref, dva_ref, stage_ref, sems):
        dq_ref = stage_ref.at[0]
        lane = lax.broadcasted_iota(jnp.int32, (NA_Q, LANE), 1)
        m0 = (lane < HEAD_DIM).astype(F32)
        m1 = 1.0 - m0
        dka_ref[...] = jnp.zeros_like(dka_ref)
        dva_ref[...] = jnp.zeros_like(dva_ref)
        db_ref[...] = jnp.zeros_like(db_ref)

        def steps(i, carry):
            idx = [_na_step(NA_UNROLL * i + u, npairs, rows_n) for u in range(NA_UNROLL)]
            chains = [(u, h, m) for u in range(NA_UNROLL) for h, m in ((0, m0), (1, m1))]
            kws = [k_ref[krows, :].astype(_BF) for _, krows, _ in idx]
            vws = [v_ref[krows, :].astype(_BF) for _, krows, _ in idx]
            qm = [(q_ref[idx[u][0], :] * m).astype(_BF) for u, h, m in chains]
            dom = [(do_ref[idx[u][0], :] * m).astype(_BF) for u, h, m in chains]
            s = [_mm_nt(qm_, kws[u]) for qm_, (u, h, m) in zip(qm, chains)]
            dpr = [_mm_nt(dom_, vws[u]) for dom_, (u, h, m) in zip(dom, chains)]
            s = [s_ * scale + _na_bias(b_ref, h, idx[u][2]) for s_, (u, h, m) in zip(s, chains)]
            e = [jnp.exp(s_ - jnp.max(s_, axis=-1, keepdims=True)) for s_ in s]
            pr = [e_ / jnp.sum(e_, axis=-1, keepdims=True) for e_ in e]
            ds = [pr_ * (dpr_ - jnp.sum(pr_ * dpr_, axis=-1, keepdims=True)) for pr_, dpr_ in zip(pr, dpr)]
            dsb = [(ds_ * scale).astype(_BF) for ds_ in ds]
            dq = [_mm(dsb_, kws[u]) * m for dsb_, (u, h, m) in zip(dsb, chains)]
            dk = [_mm_tn(dsb_, qm_) for dsb_, qm_ in zip(dsb, qm)]
            dv = [_mm_tn(pr_, dom_) for pr_, dom_ in zip(pr, dom)]
            for ds_, (u, h, m) in zip(ds, chains):
                for c in range(NA_CHUNKS):
                    db_ref[0, h, idx[u][2], c] += ds_[:, c * LANE:(c + 1) * LANE]
            for u in range(NA_UNROLL):
                qrows, krows, _ = idx[u]
                dq_ref[qrows, :] = (dq[2 * u] + dq[2 * u + 1]).astype(dq_ref.dtype)
                dka_ref[krows, :] += dk[2 * u] + dk[2 * u + 1]
                dva_ref[krows, :] += dv[2 * u] + dv[2 * u + 1]
            return carry

        lax.fori_loop(0, npairs // NA_UNROLL, steps, 0)
        stage_ref[1] = dka_ref[...].astype(stage_ref.dtype)
        stage_ref[2] = dva_ref[...].astype(stage_ref.dtype)
        j = pl.program_id(0)
        _store_strips(stage_ref, dp_out_ref, (j + 18, j + 20, j + 22), sems)

    tab = pl.BlockSpec((1, 2, NA_TYPES, NA_CHUNKS, NA_Q, LANE), lambda j: (j, 0, 0, 0, 0, 0))
    return pl.pallas_call(
        body, name="na_bwd", grid=(NA_WIDTH // LANE,),
        in_specs=[_strip(T, lambda j: j + 18), _strip(T, lambda j: j + 20), _strip(T, lambda j: j + 22),
                  _strip(T, lambda j: j + 6), tab, pl.BlockSpec(memory_space=pl.ANY)],
        out_specs=[pl.BlockSpec(memory_space=pl.ANY), tab],
        out_shape=[jax.ShapeDtypeStruct(dp.shape, dp.dtype),
                   jax.ShapeDtypeStruct((2, 2, NA_TYPES, NA_CHUNKS, NA_Q, LANE), F32)],
        scratch_shapes=[pltpu.VMEM((T, LANE), F32), pltpu.VMEM((T, LANE), F32),
                        pltpu.VMEM((3, T, LANE), _BF), pltpu.SemaphoreType.DMA((3,))],
        input_output_aliases={5: 0},
        compiler_params=_cparams(("arbitrary",)),
    )(proj, proj, proj, dycat, btab, dp)


W_BLK = IN_WIDTH // N_DEV
MXU_W = 256
N_BLK = 4 * MXU_W
N_STEPS = IN_WIDTH // N_BLK
TM = 512


def _ln_fwd(z, g, b):
    zc = z - jnp.mean(z, axis=-1, keepdims=True)
    var = jnp.mean(zc * zc, axis=-1, keepdims=True)
    return zc * lax.rsqrt(var + LN_EPS) * g + b


def _ln_bwd(dy, z, g):
    zc = z - jnp.mean(z, axis=-1, keepdims=True)
    rstd = lax.rsqrt(jnp.mean(zc * zc, axis=-1, keepdims=True) + LN_EPS)
    xhat = zc * rstd
    dxh = dy * g
    dz = rstd * (dxh - jnp.mean(dxh, axis=-1, keepdims=True) - xhat * jnp.mean(dxh * xhat, axis=-1, keepdims=True))
    return dz, dy * xhat


def _row_tile(T):
    return 1024 if T % 1024 == 0 else TM


def _halves(n):
    return (pl.ds(0, n // 2), pl.ds(n // 2, n // 2))


def _inproj_call(xb, w, after):
    T = xb.shape[0]
    tm = _row_tile(T)

    def body(x_ref, w_ref, after_ref, o_ref):
        o_ref[...] = _mm(x_ref[...], w_ref[...])

    return pl.pallas_call(
        body, name="inproj", grid=(T // tm, N_STEPS),
        in_specs=[pl.BlockSpec((tm, D_MODEL), lambda i, n: (i, 0)),
                  pl.BlockSpec((D_MODEL, N_BLK), lambda i, n: (0, n)),
                  pl.BlockSpec(memory_space=pl.ANY)],
        out_specs=pl.BlockSpec((tm, N_BLK), lambda i, n: (i, n)),
        out_shape=jax.ShapeDtypeStruct((T, IN_WIDTH), F32),
        compiler_params=_cparams(("parallel", "arbitrary")),
    )(xb, w, after)


def _vec_spec():
    return pl.BlockSpec((1, D_MODEL), lambda *_: (0, 0))


def _outproj_ln_call(y_lru, y_ret, y_na, x, w, g, b, after):
    T = x.shape[0]

    def body(yl_ref, yr_ref, yn_ref, x_ref, w_ref, g_ref, b_ref, after_ref, z_ref, x1b_ref, yc_ref):
        yc_ref[:, 0:LRU_WIDTH] = yl_ref[...].astype(yc_ref.dtype)
        yc_ref[:, LRU_WIDTH:LRU_WIDTH + RET_WIDTH] = yr_ref[...].astype(yc_ref.dtype)
        yc_ref[:, LRU_WIDTH + RET_WIDTH:] = yn_ref[...].astype(yc_ref.dtype)
        z = ALPHA * x_ref[...] + _mm(yc_ref[...], w_ref[...])
        z_ref[...] = z
        x1b_ref[...] = _ln_fwd(z, g_ref[...], b_ref[...]).astype(x1b_ref.dtype)

    row = lambda w_: pl.BlockSpec((TM, w_), lambda i: (i, 0))
    return pl.pallas_call(
        body, name="outproj_ln", grid=(T // TM,),
        in_specs=[row(LRU_WIDTH), row(RET_WIDTH), row(NA_WIDTH), row(D_MODEL),
                  pl.BlockSpec((D_MODEL, D_MODEL), lambda i: (0, 0)), _vec_spec(), _vec_spec(),
                  pl.BlockSpec(memory_space=pl.ANY)],
        out_specs=[row(D_MODEL)] * 3,
        out_shape=[jax.ShapeDtypeStruct((T, D_MODEL), F32),
                   jax.ShapeDtypeStruct((T, D_MODEL), _BF), jax.ShapeDtypeStruct((T, D_MODEL), _BF)],
        compiler_params=_cparams(("parallel",)),
    )(y_lru, y_ret, y_na, x, w, g, b, after)


def _ffn_ln_call(z1, x1b, wg, wu, wd, g, b, g1, b1):
    T = z1.shape[0]

    def body(z1_ref, xb_ref, wg_ref, wu_ref, wd_ref, g_ref, b_ref, g1_ref, b1_ref,
             z_ref, x2_ref, x2b_ref, gp_ref, up_ref, acc_ref):
        n = pl.program_id(1)

        @pl.when(n == 0)
        def _():
            acc_ref[...] = jnp.zeros_like(acc_ref)

        r0, r1 = _halves(TM)

        def pre(rows):
            xb = xb_ref[rows, :]
            return _mm(xb, wg_ref[...]), _mm(xb, wu_ref[...])

        def act(rows, gp, up):
            gp_ref[rows, :] = gp.astype(gp_ref.dtype)
            up_ref[rows, :] = up.astype(up_ref.dtype)
            return (gp * _sigmoid(gp) * up).astype(_BF)

        gp0, up0 = pre(r0)
        hid0 = act(r0, gp0, up0)
        gp1, up1 = pre(r1)
        acc_ref[r0, :] += _mm(hid0, wd_ref[...])
        hid1 = act(r1, gp1, up1)
        acc_ref[r1, :] += _mm(hid1, wd_ref[...])

        @pl.when(n == N_STEPS - 1)
        def _():
            z = ALPHA * _ln_fwd(z1_ref[...], g1_ref[...], b1_ref[...]) + acc_ref[...]
            z_ref[...] = z
            x2 = _ln_fwd(z, g_ref[...], b_ref[...])
            x2_ref[...] = x2
            x2b_ref[...] = x2.astype(x2b_ref.dtype)

    row = pl.BlockSpec((TM, D_MODEL), lambda i, n: (i, 0))
    return pl.pallas_call(
        body, name="ffn_ln", grid=(T // TM, N_STEPS),
        in_specs=[row, row,
                  pl.BlockSpec((D_MODEL, N_BLK), lambda i, n: (0, n)),
                  pl.BlockSpec((D_MODEL, N_BLK), lambda i, n: (0, n)),
                  pl.BlockSpec((N_BLK, D_MODEL), lambda i, n: (n, 0)), _vec_spec(), _vec_spec(), _vec_spec(), _vec_spec()],
        out_specs=[row] * 3 + [pl.BlockSpec((TM, N_BLK), lambda i, n: (i, n))] * 2,
        out_shape=[jax.ShapeDtypeStruct((T, D_MODEL), F32), jax.ShapeDtypeStruct((T, D_MODEL), F32),
                   jax.ShapeDtypeStruct((T, D_MODEL), _BF),
                   jax.ShapeDtypeStruct((T, IN_WIDTH), _BF), jax.ShapeDtypeStruct((T, IN_WIDTH), _BF)],
        scratch_shapes=[pltpu.VMEM((TM, D_MODEL), F32)],
        compiler_params=_cparams(("parallel", "arbitrary")),
    )(z1, x1b, wg, wu, wd, g, b, g1, b1)


def _loss_call(y, t):
    T = y.shape[0]

    def body(y_ref, t_ref, dy_ref, loss_ref):
        @pl.when(pl.program_id(0) == 0)
        def _():
            loss_ref[...] = jnp.zeros_like(loss_ref)

        err = y_ref[...] - t_ref[...]
        dy_ref[...] = err * (1.0 / D_MODEL)
        part = 0.5 * jnp.sum(jnp.mean(err * err, axis=-1, keepdims=True), axis=0, keepdims=True)
        loss_ref[...] += jnp.broadcast_to(part, loss_ref.shape)

    row = pl.BlockSpec((TM, D_MODEL), lambda i: (i, 0))
    return pl.pallas_call(
        body, name="loss", grid=(T // TM,),
        in_specs=[row, row],
        out_specs=[row, pl.BlockSpec((SUB, LANE), lambda i: (0, 0))],
        out_shape=[jax.ShapeDtypeStruct((T, D_MODEL), F32), jax.ShapeDtypeStruct((SUB, LANE), F32)],
        compiler_params=_cparams(("arbitrary",)),
    )(y, t)


def _ffn_bwd_call(dx2, z2, gpb, upb, wg, wu, wd, g, after):
    T = dx2.shape[0]

    def body(dx2_ref, z_ref, gp_ref, up_ref, wg_ref, wu_ref, wd_ref, g_ref, after_ref,
             dx1_ref, dgp_ref, dup_ref, hid_ref, dzb_ref, dln_ref, acc_ref):
        i, n = pl.program_id(0), pl.program_id(1)

        @pl.when((i == 0) & (n == 0))
        def _():
            dln_ref[...] = jnp.zeros_like(dln_ref)

        @pl.when(n == 0)
        def _():
            dy = dx2_ref[...]
            dz, dg_rows = _ln_bwd(dy, z_ref[...], g_ref[...])
            dzb_ref[...] = dz.astype(dzb_ref.dtype)
            acc_ref[...] = ALPHA * dz
            dln_ref[0:1, :] += jnp.sum(dg_rows, axis=0, keepdims=True)
            dln_ref[1:2, :] += jnp.sum(dy, axis=0, keepdims=True)

        r0, r1 = _halves(TM)

        def grads(rows, dhid):
            gp = gp_ref[rows, :].astype(F32)
            up = up_ref[rows, :].astype(F32)
            sg = _sigmoid(gp)
            act = gp * sg
            hid_ref[rows, :] = (act * up).astype(hid_ref.dtype)
            dup = (dhid * act).astype(_BF)
            dgp = (dhid * up * (sg * (1.0 + gp * (1.0 - sg)))).astype(_BF)
            dgp_ref[rows, :] = dgp.astype(dgp_ref.dtype)
            dup_ref[rows, :] = dup.astype(dup_ref.dtype)
            return dgp, dup

        dhid0 = _mm_nt(dzb_ref[r0, :], wd_ref[...])
        dhid1 = _mm_nt(dzb_ref[r1, :], wd_ref[...])
        dgp0, dup0 = grads(r0, dhid0)
        acc_ref[r0, :] += _mm_nt(dgp0, wg_ref[...]) + _mm_nt(dup0, wu_ref[...])
        dgp1, dup1 = grads(r1, dhid1)
        acc_ref[r1, :] += _mm_nt(dgp1, wg_ref[...]) + _mm_nt(dup1, wu_ref[...])

        @pl.when(n == N_STEPS - 1)
        def _():
            dx1_ref[...] = acc_ref[...]

    row = pl.BlockSpec((TM, D_MODEL), lambda i, n: (i, 0))
    blk = pl.BlockSpec((TM, N_BLK), lambda i, n: (i, n))
    return pl.pallas_call(
        body, name="ffn_bwd", grid=(T // TM, N_STEPS),
        in_specs=[row, row, blk, blk,
                  pl.BlockSpec((D_MODEL, N_BLK), lambda i, n: (0, n)),
                  pl.BlockSpec((D_MODEL, N_BLK), lambda i, n: (0, n)),
                  pl.BlockSpec((N_BLK, D_MODEL), lambda i, n: (n, 0)), _vec_spec(),
                  pl.BlockSpec(memory_space=pl.ANY)],
        out_specs=[row, blk, blk, blk, row, pl.BlockSpec((SUB, D_MODEL), lambda i, n: (0, 0))],
        out_shape=[jax.ShapeDtypeStruct((T, D_MODEL), F32),
                   jax.ShapeDtypeStruct((T, IN_WIDTH), _BF), jax.ShapeDtypeStruct((T, IN_WIDTH), _BF),
                   jax.ShapeDtypeStruct((T, IN_WIDTH), _BF), jax.ShapeDtypeStruct((T, D_MODEL), _BF),
                   jax.ShapeDtypeStruct((SUB, D_MODEL), F32)],
        scratch_shapes=[pltpu.VMEM((TM, D_MODEL), F32)],
        compiler_params=_cparams(("arbitrary", "arbitrary")),
    )(dx2, z2, gpb, upb, wg, wu, wd, g, after)


def _outproj_bwd_call(dx1, z1, w, g):
    T = dx1.shape[0]

    def body(dx_ref, z_ref, w_ref, g_ref, dzb_ref, dyc_ref, dres_ref, dln_ref):
        @pl.when(pl.program_id(0) == 0)
        def _():
            dln_ref[...] = jnp.zeros_like(dln_ref)

        dy = dx_ref[...]
        dz, dg_rows = _ln_bwd(dy, z_ref[...], g_ref[...])
        dzb_ref[...] = dz.astype(dzb_ref.dtype)
        dres_ref[...] = ALPHA * dz
        dyc_ref[...] = _mm_nt(dz, w_ref[...])
        dln_ref[0:1, :] += jnp.sum(dg_rows, axis=0, keepdims=True)
        dln_ref[1:2, :] += jnp.sum(dy, axis=0, keepdims=True)

    row = pl.BlockSpec((TM, D_MODEL), lambda i: (i, 0))
    return pl.pallas_call(
        body, name="outproj_bwd", grid=(T // TM,),
        in_specs=[row, row, pl.BlockSpec((D_MODEL, D_MODEL), lambda i: (0, 0)), _vec_spec()],
        out_specs=[row, row, row, pl.BlockSpec((SUB, D_MODEL), lambda i: (0, 0))],
        out_shape=[jax.ShapeDtypeStruct((T, D_MODEL), _BF), jax.ShapeDtypeStruct((T, D_MODEL), F32),
                   jax.ShapeDtypeStruct((T, D_MODEL), F32), jax.ShapeDtypeStruct((SUB, D_MODEL), F32)],
        compiler_params=_cparams(("arbitrary",)),
    )(dx1, z1, w, g)


def _inproj_bwd_call(dres, dp, w):
    T = dres.shape[0]

    def body(dres_ref, dp_ref, w_ref, dx_ref):
        dx_ref[...] = dres_ref[...] + _mm_nt(dp_ref[...], w_ref[...])

    row = pl.BlockSpec((TM, D_MODEL), lambda i: (i, 0))
    return pl.pallas_call(
        body, name="inproj_bwd", grid=(T // TM,),
        in_specs=[row, pl.BlockSpec((TM, IN_WIDTH), lambda i: (i, 0)),
                  pl.BlockSpec((D_MODEL, IN_WIDTH), lambda i: (0, 0), pipeline_mode=pl.Buffered(1))],
        out_specs=row,
        out_shape=jax.ShapeDtypeStruct((T, D_MODEL), F32),
        compiler_params=_cparams(("parallel",)),
    )(dres, dp, w)


def _tn_cols_call(a, b, name):
    T, ka = a.shape
    n = b.shape[1]

    def body(a_ref, b_ref, o_ref):
        o_ref[...] = _mm_tn(a_ref[...], b_ref[...]).astype(o_ref.dtype)

    return pl.pallas_call(
        body, name=name, grid=(n // N_BLK,),
        in_specs=[pl.BlockSpec((T, ka), lambda j: (0, 0), pipeline_mode=pl.Buffered(1)),
                  pl.BlockSpec((T, N_BLK), lambda j: (0, j))],
        out_specs=pl.BlockSpec((ka, N_BLK), lambda j: (0, j)),
        out_shape=jax.ShapeDtypeStruct((ka, n), _BF),
        compiler_params=_cparams(("parallel",)),
    )(a, b)


def _tn_cols2_call(a, b1, b2, name):
    T, ka = a.shape
    n = b1.shape[1]
    nb = n // N_BLK

    def body(a_ref, b1_ref, b2_ref, o1_ref, o2_ref):
        w = pl.program_id(0)

        @pl.when(w == 0)
        def _():
            o1_ref[...] = _mm_tn(a_ref[...], b1_ref[...]).astype(o1_ref.dtype)

        @pl.when(w == 1)
        def _():
            o2_ref[...] = _mm_tn(a_ref[...], b2_ref[...]).astype(o2_ref.dtype)

    first = lambda w, j: (0, jnp.where(w == 0, j, nb - 1))
    second = lambda w, j: (0, jnp.where(w == 0, 0, j))
    return pl.pallas_call(
        body, name=name, grid=(2, nb),
        in_specs=[pl.BlockSpec((T, ka), lambda w, j: (0, 0), pipeline_mode=pl.Buffered(1)),
                  pl.BlockSpec((T, N_BLK), first), pl.BlockSpec((T, N_BLK), second)],
        out_specs=[pl.BlockSpec((ka, N_BLK), first), pl.BlockSpec((ka, N_BLK), second)],
        out_shape=[jax.ShapeDtypeStruct((ka, n), _BF)] * 2,
        compiler_params=_cparams(("arbitrary", "arbitrary")),
    )(a, b1, b2)


def _tn_rows_call(a, b, kb, name):
    T, ka = a.shape
    n = b.shape[1]

    def body(a_ref, b_ref, o_ref):
        o_ref[...] = _mm_tn(a_ref[...], b_ref[...]).astype(o_ref.dtype)

    return pl.pallas_call(
        body, name=name, grid=(ka // kb,),
        in_specs=[pl.BlockSpec((T, kb), lambda r: (0, r)),
                  pl.BlockSpec((T, n), lambda r: (0, 0), pipeline_mode=pl.Buffered(1))],
        out_specs=pl.BlockSpec((kb, n), lambda r: (r, 0)),
        out_shape=jax.ShapeDtypeStruct((ka, n), _BF),
        compiler_params=_cparams(("parallel",)),
    )(a, b)


def _me():
    return lax.axis_index("x"), lax.axis_index("y"), lax.axis_index("c")


def _flip(k):
    x, y, c = _me()
    return (1 - x if k & 4 else x, 1 - y if k & 2 else y, 1 - c if k & 1 else c)


def _dev_index(pos):
    return 4 * pos[0] + 2 * pos[1] + pos[2]


_HBM = pl.BlockSpec(memory_space=pltpu.HBM)
_SEM = pl.BlockSpec(memory_space=pltpu.SEMAPHORE)


def _land_shape(shape, mode):
    if mode == "all":
        return (N_DEV,) + shape
    if mode == "cols":
        return (shape[0], N_DEV * shape[1])
    if mode == "blk":
        return shape
    assert mode == "scols"
    return (N_DEV, shape[0], shape[1] // N_DEV)


def _comm_copies(ins, lands, modes, send_sems, recv_sems):
    me = _dev_index(_me())
    copies = []
    for k in range(N_DEV):
        peer = _flip(k)
        pidx = _dev_index(peer)
        for a, (src, land, mode) in enumerate(zip(ins, lands, modes)):
            if mode == "blk":
                src = src.at[pidx]
            elif mode == "scols":
                w = src.shape[1] // N_DEV
                src = src.at[:, pl.ds(pl.multiple_of(pidx * w, LANE), w)]
            if mode == "cols":
                w = src.shape[1]
                dst = land.at[:, pl.ds(pl.multiple_of(me * w, LANE), w)]
            else:
                dst = land.at[me]
            copies.append(pltpu.make_async_remote_copy(
                src_ref=src, dst_ref=dst, send_sem=send_sems.at[k * len(ins) + a], recv_sem=recv_sems.at[k * len(ins) + a],
                device_id=peer, device_id_type=MESH))
    return copies


def _comm_start_call(arrs, gather_flags, after, name):
    n = len(arrs)
    lands = [lax.empty(_land_shape(v.shape, mode), v.dtype) for v, mode in zip(arrs, gather_flags)]

    def body(*refs):
        ins, lnd = refs[:n], refs[n:2 * n]
        send_sems, recv_sems = refs[2 * n + len(after)], refs[2 * n + len(after) + 1]
        for cp in _comm_copies(ins, lnd, gather_flags, send_sems, recv_sems):
            cp.start()
        refs[-1][...] = jnp.zeros_like(refs[-1])

    hbm = [pltpu.with_memory_space_constraint(v, pltpu.HBM) for v in list(arrs) + lands]
    out = pl.pallas_call(
        body, name=name,
        out_shape=(pltpu.SemaphoreType.DMA((N_DEV * n,)), pltpu.SemaphoreType.DMA((N_DEV * n,)),
                   *[pltpu.HBM(v.shape, v.dtype) for v in hbm], jax.ShapeDtypeStruct((SUB, LANE), F32)),
        in_specs=[_HBM] * (2 * n) + [pl.BlockSpec(memory_space=pl.ANY)] * len(after),
        out_specs=(_SEM, _SEM, *[_HBM] * (2 * n), pl.BlockSpec(memory_space=pltpu.VMEM)),
        input_output_aliases={i: 2 + i for i in range(2 * n)},
        compiler_params=pltpu.CompilerParams(has_side_effects=pltpu.SideEffectType.DATAFLOW_SIDE_EFFECTING),
    )(*hbm, *after)
    return out[:-1], out[-1]


def _comm_wait_call(state, gather_flags, after, name):
    n = len(gather_flags)
    send_sems, recv_sems, thru = state[0], state[1], state[2:]

    def body(*refs):
        ins, lnd, ssem, rsem = refs[:n], refs[n:2 * n], refs[2 * n], refs[2 * n + 1]
        for cp in _comm_copies(ins, lnd, gather_flags, ssem, rsem):
            cp.wait_send()
            cp.wait_recv()

    out = pl.pallas_call(
        body, name=name,
        out_shape=tuple(pltpu.HBM(v.shape, v.dtype) for v in thru),
        in_specs=[_HBM] * (2 * n) + [_SEM, _SEM] + [pl.BlockSpec(memory_space=pl.ANY)] * len(after),
        out_specs=tuple([_HBM] * (2 * n)),
        input_output_aliases={i: i for i in range(2 * n)},
        compiler_params=pltpu.CompilerParams(has_side_effects=pltpu.SideEffectType.DATAFLOW_SIDE_EFFECTING),
    )(*thru, send_sems, recv_sems, *after)
    return out[n:]


def _sum8_call(recv, stacked, layer, nl, rows, r_out, c_out, name, transposed=False):
    c = recv.shape[2]

    def body(x_ref, *rest):
        o_ref = rest[-1]
        if transposed:
            acc = x_ref[0].astype(F32)
            for s in range(1, N_DEV):
                acc = acc + x_ref[s].astype(F32)
            o_ref[...] = acc.T[:c_out, :]
        else:
            acc = x_ref[0, :, :c_out].astype(F32)
            for s in range(1, N_DEV):
                acc = acc + x_ref[s, :, :c_out].astype(F32)
            o_ref[...] = acc

    prev = [] if stacked is None else [stacked]
    if transposed:
        out_spec = pl.BlockSpec((None, c_out, rows), lambda i: (layer, 0, i))
        out_shape = jax.ShapeDtypeStruct((nl, c_out, r_out), F32)
    else:
        out_spec = pl.BlockSpec((None, rows, c_out), lambda i: (layer, i, 0))
        out_shape = jax.ShapeDtypeStruct((nl, r_out, c_out), F32)
    return pl.pallas_call(
        body, name=name, grid=(r_out // rows,),
        in_specs=[pl.BlockSpec((N_DEV, rows, c), lambda i: (0, i, 0))] + [pl.BlockSpec(memory_space=pl.ANY)] * len(prev),
        out_specs=out_spec, out_shape=out_shape,
        input_output_aliases={1: 0} if prev else {},
        compiler_params=_cparams(("parallel",)),
    )(recv, *prev)


def _adamw_call(w, g, m, v, rows, name):
    r, c = w.shape

    def body(w_ref, g_ref, m_ref, v_ref, d_ref, nm_ref, nv_ref):
        gr = g_ref[...]
        nm = ADAM_B1 * m_ref[...] + (1.0 - ADAM_B1) * gr
        nv = ADAM_B2 * v_ref[...] + (1.0 - ADAM_B2) * (gr * gr)
        m_hat = nm / (1.0 - ADAM_B1 ** ADAM_STEP)
        v_hat = nv / (1.0 - ADAM_B2 ** ADAM_STEP)
        d_ref[...] = -ADAM_LR * (m_hat / (jnp.sqrt(v_hat) + ADAM_EPS) + ADAM_WD * w_ref[...])
        nm_ref[...] = nm
        nv_ref[...] = nv

    spec = pl.BlockSpec((rows, c), lambda i: (i, 0))
    return pl.pallas_call(
        body, name=name, grid=(r // rows,),
        in_specs=[spec] * 4, out_specs=[spec] * 3,
        out_shape=[jax.ShapeDtypeStruct((r, c), F32)] * 3,
        compiler_params=_cparams(("parallel",)),
    )(w, g, m, v)


SH_ROWS = 16
SH_W = LRU_WIDTH // N_DEV
REP_ROWS = 824
_REP_SIZES = (LRU_WIDTH, 2 * 6 * 64 * 64, 2 * 6 * 64 * 64, RET_WIDTH, 1920, D_MODEL, D_MODEL, D_MODEL, D_MODEL)
_RPB_SIZE = NA_HEADS * (2 * NA_KH - 1) * (2 * NA_KW - 1)


def _pack_sh(cw, ba, bx, lam):
    return jnp.concatenate([cw, ba, bx, lam], axis=0)


def _pad_sh(p):
    pad = [(0, 0)] * (p.ndim - 2) + [(0, SH_ROWS - p.shape[-2]), (0, LANE - p.shape[-1])]
    return jnp.pad(p, pad)


def _pack_rep(cb, wa, wx, gnw, rpb, l1g, l1b, l2g, l2b):
    flat = jnp.concatenate([cb.reshape(-1), wa.reshape(-1), wx.reshape(-1), gnw.reshape(-1),
                            jnp.pad(rpb.reshape(-1), (0, 1920 - _RPB_SIZE)), l1g, l1b, l2g, l2b,
                            jnp.zeros((REP_ROWS * LANE - sum(_REP_SIZES),), F32)])
    return flat.reshape(REP_ROWS, LANE)


def _unpack_rep(p):
    nl = p.shape[0]
    flat = p.reshape(nl, -1)
    out, off = [], 0
    for size in _REP_SIZES:
        out.append(flat[:, off:off + size])
        off += size
    cb, wa, wx, gnw, rpb, l1g, l1b, l2g, l2b = out
    return (cb, wa.reshape(nl, 2, 6, 64, 64), wx.reshape(nl, 2, 6, 64, 64), gnw,
            rpb[:, :_RPB_SIZE].reshape(nl, NA_HEADS, 2 * NA_KH - 1, 2 * NA_KW - 1), l1g, l1b, l2g, l2b)


def _adamw_nd(w, g, m, v, rows, name):
    shp = w.shape
    f = lambda t: t.reshape(-1, shp[-1])
    rows = f(w).shape[0] if rows is None else rows
    return [t.reshape(shp) for t in _adamw_call(f(w), f(g), f(m), f(v), rows, name)]


def kernel(x, w_in, conv_w, conv_b, lru_w_a, lru_b_a, lru_w_x, lru_b_x, lru_lam, ret_gn_w, na_rpb, w_out, ln1_g, ln1_b, w_gate, w_up, w_down, ln2_g, ln2_b, loss_target, m_w_in, m_conv_w, m_conv_b, m_lru_w_a, m_lru_b_a, m_lru_w_x, m_lru_b_x, m_lru_lam, m_ret_gn_w, m_na_rpb, m_w_out, m_ln1_g, m_ln1_b, m_w_gate, m_w_up, m_w_down, m_ln2_g, m_ln2_b, v_w_in, v_conv_w, v_conv_b, v_lru_w_a, v_lru_b_a, v_lru_w_x, v_lru_b_x, v_lru_lam, v_ret_gn_w, v_na_rpb, v_w_out, v_ln1_g, v_ln1_b, v_w_gate, v_w_up, v_w_down, v_ln2_g, v_ln2_b):
    nl = w_in.shape[0]
    T = x.shape[1]
    rows_n = T // GRID_W
    x0, target = x[0], loss_target[0]
    ffpad = W_BLK - FF_BLK

    win_b = w_in.astype(_BF)
    wg_b = jnp.pad(w_gate, ((0, 0), (0, 0), (0, ffpad))).astype(_BF)
    wu_b = jnp.pad(w_up, ((0, 0), (0, 0), (0, ffpad))).astype(_BF)
    wd_b = jnp.pad(w_down, ((0, 0), (0, ffpad), (0, 0))).astype(_BF)
    wout_b = w_out.astype(_BF)
    def agf_start(l, after):
        sh = _pad_sh(_pack_sh(conv_w[l], lru_b_a[l], lru_b_x[l], lru_lam[l]))
        arrs, modes = [win_b[l], sh], ["cols", "all"]
        if l > 0:
            arrs, modes = arrs + [wd_b[l]], modes + ["all"]
        return _comm_start_call(arrs, modes, after, f"agf_start{l}"), modes

    def agk_start(l, after):
        arrs, modes = [wg_b[l], wu_b[l], wout_b[l]], ["cols", "cols", "all"]
        if l == 0:
            arrs, modes = arrs + [wd_b[l]], modes + ["all"]
        return _comm_start_call(arrs, modes, after, f"agk_start{l}"), modes

    tables = _ret_tables(T)
    w4_all = _lru_w4(lru_w_a, lru_w_x)
    layers = []
    gathered = []
    xs, xb = x0, x0.astype(_BF)
    (agf_state, token), agf_modes = agf_start(0, [])
    tie = 0.0 * token[0, 0]
    btabs = [_na_bias_tables(na_rpb[l] + tie, rows_n) for l in range(nl)]
    for l in range(nl):
        front = _comm_wait_call(agf_state, agf_modes, [xb] + (btabs if l == 0 else []), f"agf_wait{l}")
        win, shg = front[0], front[1]
        (agk_state, token), agk_modes = agk_start(l, [shg])
        full = shg[:, :10, :SH_W].transpose(1, 0, 2).reshape(10, LRU_WIDTH)
        vec, w4 = _lru_vec(full[0:4], conv_b[l], full[4:6], full[6:8], full[8:10]), w4_all[l]
        gnw8 = jnp.pad(ret_gn_w[l][None], ((0, SUB - 1), (0, 0)))
        btab = btabs[l]
        proj = _inproj_call(xb, win, token)
        y_lru = _lru_fwd_call(proj, vec, w4)
        y_ret = _ret_fwd_call(proj, tables, gnw8)
        y_na = _na_fwd_call(proj, btab)
        back = _comm_wait_call(agk_state, agk_modes, [y_na], f"agk_wait{l}")
        wg, wu, wout = back[0], back[1], back[2]
        wd = (back[3] if l == 0 else front[2]).reshape(IN_WIDTH, D_MODEL)
        wout = wout.reshape(D_MODEL, D_MODEL)
        gathered.append((win, wg, wu, wd, wout))
        if l + 1 < nl:
            (agf_state, token), agf_modes = agf_start(l + 1, [wout])
        z1, x1b, ycb = _outproj_ln_call(y_lru, y_ret, y_na, xs, wout, ln1_g[l][None], ln1_b[l][None], token)
        z2, x2, x2b, gpb, upb = _ffn_ln_call(z1, x1b, wg, wu, wd, ln2_g[l][None], ln2_b[l][None],
                                             ln1_g[l][None], ln1_b[l][None])
        layers.append(dict(xb=xb, proj=proj, vec=vec, w4=w4, gnw8=gnw8, btab=btab,
                           z1=z1, x1b=x1b, ycb=ycb, z2=z2, gpb=gpb, upb=upb))
        xs, xb = x2, x2b

    dx, loss_blk = _loss_call(xs, target)
    loss = lax.psum(loss_blk[0, 0], ("x", "y", "c"))

    gxa_flags = ["scols", "scols", "blk", "blk"]
    gxb_flags = ["scols", "blk", "all"]
    gxa_state, gxb_state = [None] * nl, [None] * nl
    token = loss_blk
    for l in reversed(range(nl)):
        s = layers[l]
        win, wg, wu, wd, wout = gathered[l]
        dx1, dgp, dup, hid, dz2b, dln2 = _ffn_bwd_call(dx, s["z2"], s["gpb"], s["upb"], wg, wu, wd, ln2_g[l][None], token)
        dwg, dwu = _tn_cols2_call(s["x1b"], dgp, dup, "tn_cols2")
        dwd = _tn_rows_call(hid, dz2b, N_BLK, "tn_rows_down").reshape(N_DEV, W_BLK, D_MODEL)
        dz1b, dyc, dres, dln1 = _outproj_bwd_call(dx1, s["z1"], wout, ln1_g[l][None])
        dwout = _tn_rows_call(s["ycb"], dz1b, D_MODEL // 2, "tn_rows_out").reshape(N_DEV, LANE, D_MODEL)
        gxa_state[l], token = _comm_start_call([dwg, dwu, dwd, dwout], gxa_flags, [], f"gxa_start{l}")
        dp, dvec, dw4 = _lru_bwd_call(s["proj"], dyc, s["vec"], s["w4"], token)
        dp, dgnw = _ret_bwd_call(s["proj"], dyc, tables, s["gnw8"], dp)
        dp, dbias = _na_bwd_call(s["proj"], dyc, s["btab"], dp)
        dwin = _tn_cols_call(s["xb"], dp, "tn_cols")
        dx = _inproj_bwd_call(dres, dp, win)
        dcw, dcb, dwa, dba, dwx, dbx, dlam = _lru_unpack(dvec, dw4)
        rep = _pack_rep(dcb, dwa, dwx, dgnw[0], _na_bias_grad(dbias, rows_n), dln1[0], dln1[1], dln2[0], dln2[1])
        sh = _pack_sh(dcw, dba, dbx, dlam).reshape(10, N_DEV, SH_W).transpose(1, 0, 2)
        gxb_state[l], token = _comm_start_call([dwin, _pad_sh(sh), rep], gxb_flags, [], f"gxb_start{l}")

    g_w_in = g_w_gate = g_w_up = g_w_down = g_w_out = g_shp = g_repp = None
    after = [dx, token]
    big = {}
    for l in reversed(range(nl)):
        ra = _comm_wait_call(gxa_state[l], gxa_flags, after, f"gxa_wait{l}")
        g_w_gate = _sum8_call(ra[0], g_w_gate, l, nl, TM, D_MODEL, FF_BLK, "sum8_ff", transposed=True)
        g_w_up = _sum8_call(ra[1], g_w_up, l, nl, TM, D_MODEL, FF_BLK, "sum8_ff", transposed=True)
        g_w_down = _sum8_call(ra[2], g_w_down, l, nl, FF_BLK, FF_BLK, D_MODEL, "sum8_down")
        g_w_out = _sum8_call(ra[3], g_w_out, l, nl, LANE, LANE, D_MODEL, "sum8_out")
        after = [g_w_out]
        if l == 0:
            tr = lambda t: jnp.swapaxes(t, 1, 2)
            big["w_gate"] = [tr(t) for t in _adamw_nd(tr(w_gate), g_w_gate, tr(m_w_gate), tr(v_w_gate), FF_BLK, "adamw_down")]
            big["w_up"] = [tr(t) for t in _adamw_nd(tr(w_up), g_w_up, tr(m_w_up), tr(v_w_up), FF_BLK, "adamw_down")]
            g_w_gate, g_w_up = tr(g_w_gate), tr(g_w_up)
            big["w_down"] = _adamw_nd(w_down, g_w_down, m_w_down, v_w_down, FF_BLK, "adamw_down")
            big["w_out"] = _adamw_nd(w_out, g_w_out, m_w_out, v_w_out, LANE, "adamw_out")
            after = [big[n][k] for n in ("w_gate", "w_up", "w_down", "w_out") for k in range(3)]
        rb = _comm_wait_call(gxb_state[l], gxb_flags, after, f"gxb_wait{l}")
        g_w_in = _sum8_call(rb[0], g_w_in, l, nl, TM, D_MODEL, W_BLK, "sum8_in")
        g_shp = _sum8_call(rb[1], g_shp, l, nl, SH_ROWS, SH_ROWS, LANE, "sum8_sh")
        g_repp = _sum8_call(rb[2], g_repp, l, nl, REP_ROWS, REP_ROWS, LANE, "sum8_rep")
        after = [g_repp]

    big["w_in"] = _adamw_nd(w_in, g_w_in, m_w_in, v_w_in, TM, "adamw_in")
    g_shp = g_shp[:, :, :SH_W]
    rep_names = ("conv_b", "lru_w_a", "lru_w_x", "ret_gn_w", "na_rpb", "ln1_g", "ln1_b", "ln2_g", "ln2_b")
    grads = {"w_in": g_w_in, "w_gate": g_w_gate, "w_up": g_w_up, "w_down": g_w_down, "w_out": g_w_out,
             "conv_w": g_shp[:, 0:4], "lru_b_a": g_shp[:, 4:6], "lru_b_x": g_shp[:, 6:8], "lru_lam": g_shp[:, 8:10]}
    grads.update(dict(zip(rep_names, _unpack_rep(g_repp))))
    small = {
        "conv_w": (conv_w, m_conv_w, v_conv_w), "conv_b": (conv_b, m_conv_b, v_conv_b),
        "lru_w_a": (lru_w_a, m_lru_w_a, v_lru_w_a), "lru_b_a": (lru_b_a, m_lru_b_a, v_lru_b_a),
        "lru_w_x": (lru_w_x, m_lru_w_x, v_lru_w_x), "lru_b_x": (lru_b_x, m_lru_b_x, v_lru_b_x),
        "lru_lam": (lru_lam, m_lru_lam, v_lru_lam), "ret_gn_w": (ret_gn_w, m_ret_gn_w, v_ret_gn_w),
        "na_rpb": (na_rpb, m_na_rpb, v_na_rpb), "ln1_g": (ln1_g, m_ln1_g, v_ln1_g), "ln1_b": (ln1_b, m_ln1_b, v_ln1_b),
        "ln2_g": (ln2_g, m_ln2_g, v_ln2_g), "ln2_b": (ln2_b, m_ln2_b, v_ln2_b),
    }
    for name, (w_, m_, v_) in small.items():
        big[name] = _adamw_nd(w_, grads[name], m_, v_, None, "adamw_small")
    kinds = [{n: big[n][k] for n in big} for k in range(3)]
    order = ("w_in", "conv_w", "conv_b", "lru_w_a", "lru_b_a", "lru_w_x", "lru_b_x", "lru_lam", "ret_gn_w", "na_rpb",
             "w_out", "ln1_g", "ln1_b", "w_gate", "w_up", "w_down", "ln2_g", "ln2_b")
    outs = [loss, dx[None]]
    for d in (grads, *kinds):
        outs.extend(d[n] for n in order)
    return tuple(outs)
```
